```python
import jax, jax.numpy as jnp
from jax import lax
import numpy as np

D_MODEL = 4096
BATCH = 8
SEQ = 4096
DEPTH = 1

MEM_LEN = 256
MIX_WIDTH = D_MODEL
GROUP_DIM = 128
CONV_WIDTH = MIX_WIDTH // 2
SCONV_WIDTH = MIX_WIDTH - CONV_WIDTH
N_CONV_GROUPS = CONV_WIDTH // GROUP_DIM
N_SCONV_HEADS = SCONV_WIDTH // GROUP_DIM
IN_WIDTH = 2 * CONV_WIDTH + 3 * SCONV_WIDTH
CONV_KERNEL = 31
SHORT_KERNEL = 3
FFN_KERNEL = 3
D_FF = ((8 * D_MODEL // 3 + 255) // 256) * 256
N_XATTN_HEADS = 4
XATTN_HEAD_DIM = D_MODEL // N_XATTN_HEADS
EPS = 1e-6

kernel_name = "hybrid_conformer_shortconv_xattn_convffn"


def rms_norm(x, g):
    xf = x.astype(jnp.float32)
    y = xf * lax.rsqrt(jnp.mean(xf * xf, axis=-1, keepdims=True) + EPS)
    return (y * g.astype(jnp.float32)).astype(x.dtype)


def group_layer_norm(x, g, b, group):
    shape = x.shape
    xf = x.astype(jnp.float32).reshape(*shape[:-1], shape[-1] // group, group)
    mu = jnp.mean(xf, axis=-1, keepdims=True)
    xc = xf - mu
    var = jnp.mean(xc * xc, axis=-1, keepdims=True)
    y = (xc * lax.rsqrt(var + EPS)).reshape(shape)
    return (y * g.astype(jnp.float32) + b.astype(jnp.float32)).astype(x.dtype)


def causal_dwconv(x, w):
    K = w.shape[0]
    T = x.shape[1]
    xp = jnp.pad(x, ((0, 0), (K - 1, 0), (0, 0)))
    y = xp[:, 0:T, :] * w[0]
    for k in range(1, K):
        y = y + xp[:, k:k + T, :] * w[k]
    return y


def _fwd_setup_inputs(seed: int = 0) -> dict:
    key = jax.random.key(seed)
    ks = jax.random.split(key, 24)

    def nrm(k, shape, scale):
        return jax.random.normal(k, shape, jnp.float32) * scale

    def gain(k, shape):
        return 1.0 + 0.02 * jax.random.normal(k, shape, jnp.float32)

    L = DEPTH
    return {
        "x": nrm(ks[0], (BATCH, SEQ, D_MODEL), 1.0),
        "mem": nrm(ks[1], (BATCH, MEM_LEN, D_MODEL), 1.0),
        "g_mix": gain(ks[2], (L, D_MODEL)),
        "w_in": nrm(ks[3], (L, D_MODEL, IN_WIDTH), D_MODEL ** -0.5),
        "conv_a_w": nrm(ks[4], (L, CONV_KERNEL, CONV_WIDTH), CONV_KERNEL ** -0.5),
        "conv_a_b": nrm(ks[5], (L, CONV_WIDTH), 0.02),
        "ln_a_g": gain(ks[6], (L, CONV_WIDTH)),
        "ln_a_b": nrm(ks[7], (L, CONV_WIDTH), 0.02),
        "conv_b_w": nrm(ks[8], (L, SHORT_KERNEL, SCONV_WIDTH), SHORT_KERNEL ** -0.5),
        "w_out": nrm(ks[9], (L, MIX_WIDTH, D_MODEL), MIX_WIDTH ** -0.5),
        "g_xattn": gain(ks[10], (L, D_MODEL)),
        "g_mem": gain(ks[11], (D_MODEL,)),
        "w_q": nrm(ks[12], (L, D_MODEL, D_MODEL), D_MODEL ** -0.5),
        "w_k": nrm(ks[13], (L, D_MODEL, D_MODEL), D_MODEL ** -0.5),
        "w_v": nrm(ks[14], (L, D_MODEL, D_MODEL), D_MODEL ** -0.5),
        "w_o": nrm(ks[15], (L, D_MODEL, D_MODEL), D_MODEL ** -0.5),
        "g_ffn": gain(ks[16], (L, D_MODEL)),
        "w_gate": nrm(ks[17], (L, D_MODEL, D_FF), D_MODEL ** -0.5),
        "w_up": nrm(ks[18], (L, D_MODEL, D_FF), D_MODEL ** -0.5),
        "conv_f_w": nrm(ks[19], (L, FFN_KERNEL, D_FF), FFN_KERNEL ** -0.5),
        "w_down": nrm(ks[20], (L, D_FF, D_MODEL), D_FF ** -0.5),
        "g_final": gain(ks[21], (D_MODEL,)),
    }


def _fwd_reference(x, mem, g_mix, w_in, conv_a_w, conv_a_b, ln_a_g, ln_a_b, conv_b_w,
              w_out, g_xattn, g_mem, w_q, w_k, w_v, w_o, g_ffn, w_gate, w_up,
              conv_f_w, w_down, g_final):
    B, T, D = x.shape
    M = mem.shape[1]
    memn = rms_norm(mem, g_mem)
    splits = [CONV_WIDTH, 2 * CONV_WIDTH, 2 * CONV_WIDTH + SCONV_WIDTH,
              2 * CONV_WIDTH + 2 * SCONV_WIDTH]
    h = x
    for l in range(DEPTH):
        xn = rms_norm(h, g_mix[l])
        proj = xn @ w_in[l]
        a_val, a_gate, b_gate, c_gate, b_h = jnp.split(proj, splits, axis=-1)
        u = a_val * jax.nn.sigmoid(a_gate)
        u = causal_dwconv(u, conv_a_w[l]) + conv_a_b[l]
        u = jax.nn.silu(group_layer_norm(u, ln_a_g[l], ln_a_b[l], GROUP_DIM))
        v = b_gate * causal_dwconv(c_gate * b_h, conv_b_w[l])
        mix = jnp.concatenate([u, v], axis=-1)
        h = h + mix @ w_out[l]

        xn = rms_norm(h, g_xattn[l])
        q = (xn @ w_q[l]).reshape(B, T, N_XATTN_HEADS, XATTN_HEAD_DIM)
        k = (memn @ w_k[l]).reshape(B, M, N_XATTN_HEADS, XATTN_HEAD_DIM)
        vm = (memn @ w_v[l]).reshape(B, M, N_XATTN_HEADS, XATTN_HEAD_DIM)
        s = jnp.einsum('bthd,bmhd->bhtm', q.astype(jnp.float32), k.astype(jnp.float32))
        p = jax.nn.softmax(s * (XATTN_HEAD_DIM ** -0.5), axis=-1).astype(vm.dtype)
        o = jnp.einsum('bhtm,bmhd->bthd', p, vm).reshape(B, T, D)
        h = h + o @ w_o[l]

        xn = rms_norm(h, g_ffn[l])
        g = causal_dwconv(xn @ w_gate[l], conv_f_w[l])
        f = jax.nn.silu(g) * (xn @ w_up[l])
        h = h + f @ w_down[l]
    return rms_norm(h, g_final)


import jax as _jax
import jax.numpy as _jnp

TWIN_FORMAT = 'train_step'
FWD_PARAMS = ['x', 'mem', 'g_mix', 'w_in', 'conv_a_w', 'conv_a_b', 'ln_a_g', 'ln_a_b', 'conv_b_w', 'w_out', 'g_xattn', 'g_mem', 'w_q', 'w_k', 'w_v', 'w_o', 'g_ffn', 'w_gate', 'w_up', 'conv_f_w', 'w_down', 'g_final']
TWIN_WEIGHTS = ['g_mix', 'w_in', 'conv_a_w', 'conv_a_b', 'ln_a_g', 'ln_a_b', 'conv_b_w', 'w_out', 'g_xattn', 'g_mem', 'w_q', 'w_k', 'w_v', 'w_o', 'g_ffn', 'w_gate', 'w_up', 'conv_f_w', 'w_down', 'g_final']
TWIN_DIFF_INPUT = 'x'
TWIN_INPUTS = ['x', 'mem', 'g_mix', 'w_in', 'conv_a_w', 'conv_a_b', 'ln_a_g', 'ln_a_b', 'conv_b_w', 'w_out', 'g_xattn', 'g_mem', 'w_q', 'w_k', 'w_v', 'w_o', 'g_ffn', 'w_gate', 'w_up', 'conv_f_w', 'w_down', 'g_final', 'loss_target', 'm_g_mix', 'm_w_in', 'm_conv_a_w', 'm_conv_a_b', 'm_ln_a_g', 'm_ln_a_b', 'm_conv_b_w', 'm_w_out', 'm_g_xattn', 'm_g_mem', 'm_w_q', 'm_w_k', 'm_w_v', 'm_w_o', 'm_g_ffn', 'm_w_gate', 'm_w_up', 'm_conv_f_w', 'm_w_down', 'm_g_final', 'v_g_mix', 'v_w_in', 'v_conv_a_w', 'v_conv_a_b', 'v_ln_a_g', 'v_ln_a_b', 'v_conv_b_w', 'v_w_out', 'v_g_xattn', 'v_g_mem', 'v_w_q', 'v_w_k', 'v_w_v', 'v_w_o', 'v_g_ffn', 'v_w_gate', 'v_w_up', 'v_conv_f_w', 'v_w_down', 'v_g_final']
TWIN_OUTPUTS = ['loss', 'grad_x', 'grad_g_mix', 'grad_w_in', 'grad_conv_a_w', 'grad_conv_a_b', 'grad_ln_a_g', 'grad_ln_a_b', 'grad_conv_b_w', 'grad_w_out', 'grad_g_xattn', 'grad_g_mem', 'grad_w_q', 'grad_w_k', 'grad_w_v', 'grad_w_o', 'grad_g_ffn', 'grad_w_gate', 'grad_w_up', 'grad_conv_f_w', 'grad_w_down', 'grad_g_final', 'delta_g_mix', 'delta_w_in', 'delta_conv_a_w', 'delta_conv_a_b', 'delta_ln_a_g', 'delta_ln_a_b', 'delta_conv_b_w', 'delta_w_out', 'delta_g_xattn', 'delta_g_mem', 'delta_w_q', 'delta_w_k', 'delta_w_v', 'delta_w_o', 'delta_g_ffn', 'delta_w_gate', 'delta_w_up', 'delta_conv_f_w', 'delta_w_down', 'delta_g_final', 'new_m_g_mix', 'new_m_w_in', 'new_m_conv_a_w', 'new_m_conv_a_b', 'new_m_ln_a_g', 'new_m_ln_a_b', 'new_m_conv_b_w', 'new_m_w_out', 'new_m_g_xattn', 'new_m_g_mem', 'new_m_w_q', 'new_m_w_k', 'new_m_w_v', 'new_m_w_o', 'new_m_g_ffn', 'new_m_w_gate', 'new_m_w_up', 'new_m_conv_f_w', 'new_m_w_down', 'new_m_g_final', 'new_v_g_mix', 'new_v_w_in', 'new_v_conv_a_w', 'new_v_conv_a_b', 'new_v_ln_a_g', 'new_v_ln_a_b', 'new_v_conv_b_w', 'new_v_w_out', 'new_v_g_xattn', 'new_v_g_mem', 'new_v_w_q', 'new_v_w_k', 'new_v_w_v', 'new_v_w_o', 'new_v_g_ffn', 'new_v_w_gate', 'new_v_w_up', 'new_v_conv_f_w', 'new_v_w_down', 'new_v_g_final']
TWIN_LEAF_KINDS = {'loss': 'loss', 'grad_x': 'grad_x', 'grad_g_mix': 'grad_w', 'grad_w_in': 'grad_w', 'grad_conv_a_w': 'grad_w', 'grad_conv_a_b': 'grad_w', 'grad_ln_a_g': 'grad_w', 'grad_ln_a_b': 'grad_w', 'grad_conv_b_w': 'grad_w', 'grad_w_out': 'grad_w', 'grad_g_xattn': 'grad_w', 'grad_g_mem': 'grad_w', 'grad_w_q': 'grad_w', 'grad_w_k': 'grad_w', 'grad_w_v': 'grad_w', 'grad_w_o': 'grad_w', 'grad_g_ffn': 'grad_w', 'grad_w_gate': 'grad_w', 'grad_w_up': 'grad_w', 'grad_conv_f_w': 'grad_w', 'grad_w_down': 'grad_w', 'grad_g_final': 'grad_w', 'delta_g_mix': 'delta_w', 'delta_w_in': 'delta_w', 'delta_conv_a_w': 'delta_w', 'delta_conv_a_b': 'delta_w', 'delta_ln_a_g': 'delta_w', 'delta_ln_a_b': 'delta_w', 'delta_conv_b_w': 'delta_w', 'delta_w_out': 'delta_w', 'delta_g_xattn': 'delta_w', 'delta_g_mem': 'delta_w', 'delta_w_q': 'delta_w', 'delta_w_k': 'delta_w', 'delta_w_v': 'delta_w', 'delta_w_o': 'delta_w', 'delta_g_ffn': 'delta_w', 'delta_w_gate': 'delta_w', 'delta_w_up': 'delta_w', 'delta_conv_f_w': 'delta_w', 'delta_w_down': 'delta_w', 'delta_g_final': 'delta_w', 'new_m_g_mix': 'new_m', 'new_m_w_in': 'new_m', 'new_m_conv_a_w': 'new_m', 'new_m_conv_a_b': 'new_m', 'new_m_ln_a_g': 'new_m', 'new_m_ln_a_b': 'new_m', 'new_m_conv_b_w': 'new_m', 'new_m_w_out': 'new_m', 'new_m_g_xattn': 'new_m', 'new_m_g_mem': 'new_m', 'new_m_w_q': 'new_m', 'new_m_w_k': 'new_m', 'new_m_w_v': 'new_m', 'new_m_w_o': 'new_m', 'new_m_g_ffn': 'new_m', 'new_m_w_gate': 'new_m', 'new_m_w_up': 'new_m', 'new_m_conv_f_w': 'new_m', 'new_m_w_down': 'new_m', 'new_m_g_final': 'new_m', 'new_v_g_mix': 'new_v', 'new_v_w_in': 'new_v', 'new_v_conv_a_w': 'new_v', 'new_v_conv_a_b': 'new_v', 'new_v_ln_a_g': 'new_v', 'new_v_ln_a_b': 'new_v', 'new_v_conv_b_w': 'new_v', 'new_v_w_out': 'new_v', 'new_v_g_xattn': 'new_v', 'new_v_g_mem': 'new_v', 'new_v_w_q': 'new_v', 'new_v_w_k': 'new_v', 'new_v_w_v': 'new_v', 'new_v_w_o': 'new_v', 'new_v_g_ffn': 'new_v', 'new_v_w_gate': 'new_v', 'new_v_w_up': 'new_v', 'new_v_conv_f_w': 'new_v', 'new_v_w_down': 'new_v', 'new_v_g_final': 'new_v'}


def _forward(args):
    return _fwd_reference(*[args[k] for k in FWD_PARAMS])


def _output_shape():
    out = _jax.eval_shape(lambda: _forward(_fwd_setup_inputs(0)))
    return out.shape, out.dtype

N_MICROBATCH = 1
ADAM_LR = 0.001
ADAM_B1 = 0.9
ADAM_B2 = 0.999
ADAM_EPS = 1e-08
ADAM_WD = 0.01
ADAM_STEP = 10
PER_EXAMPLE_BATCH_AXIS = {'x': 0, 'mem': 0, 'loss_target': 0}
SHARED_INPUTS = []
_WEIGHT_DTYPES = {'g_mix': _jnp.float32, 'w_in': _jnp.float32, 'conv_a_w': _jnp.float32, 'conv_a_b': _jnp.float32, 'ln_a_g': _jnp.float32, 'ln_a_b': _jnp.float32, 'conv_b_w': _jnp.float32, 'w_out': _jnp.float32, 'g_xattn': _jnp.float32, 'g_mem': _jnp.float32, 'w_q': _jnp.float32, 'w_k': _jnp.float32, 'w_v': _jnp.float32, 'w_o': _jnp.float32, 'g_ffn': _jnp.float32, 'w_gate': _jnp.float32, 'w_up': _jnp.float32, 'conv_f_w': _jnp.float32, 'w_down': _jnp.float32, 'g_final': _jnp.float32}
MOMENT_SCALE = {'g_mix': 4.864933e-02, 'w_in': 3.087594e-02, 'conv_a_w': 2.318392e-02, 'conv_a_b': 4.956648e-02, 'ln_a_g': 2.698777e-02, 'ln_a_b': 2.452406e-02, 'conv_b_w': 3.806767e-02, 'w_out': 3.070481e-02, 'g_xattn': 3.841732e-03, 'g_mem': 5.678513e-03, 'w_q': 3.774999e-03, 'w_k': 3.770822e-03, 'w_v': 3.882146e-03, 'w_o': 3.836871e-03, 'g_ffn': 2.705590e-02, 'w_gate': 1.181759e-02, 'w_up': 1.144612e-02, 'conv_f_w': 1.206293e-02, 'w_down': 1.875351e-02, 'g_final': 7.986570e+00}


def _to_microbatches(a, axis):
    t = _jnp.moveaxis(a, axis, 0)
    t = t.reshape((N_MICROBATCH, t.shape[0] // N_MICROBATCH) + t.shape[1:])
    return _jnp.moveaxis(t, 1, axis + 1)


def setup_inputs(seed: int = 0) -> dict:
    inp = _fwd_setup_inputs(seed)
    key = _jax.random.fold_in(_jax.random.key(seed), 7919)
    shape, _ = _output_shape()
    out = dict(inp)
    out["loss_target"] = _jax.random.normal(_jax.random.fold_in(key, 0), shape, _jnp.float32)
    for i, name in enumerate(TWIN_WEIGHTS):
        w = inp[name].astype(_jnp.float32)
        if MOMENT_SCALE is None:
            s = _jnp.sqrt(_jnp.mean(_jnp.square(w)) + 1e-30)
        else:
            s = MOMENT_SCALE[name]
        km, kv = _jax.random.split(_jax.random.fold_in(key, i + 1))
        out[name] = w
        out["m_" + name] = s * _jax.random.normal(km, w.shape, _jnp.float32)
        out["v_" + name] = (s * s) * _jax.random.uniform(kv, w.shape, _jnp.float32, 0.5, 1.5)
    if N_MICROBATCH > 1:
        for name, axis in PER_EXAMPLE_BATCH_AXIS.items():
            out[name] = _to_microbatches(out[name], axis)
    return {'x': out['x'], 'mem': out['mem'], 'g_mix': out['g_mix'], 'w_in': out['w_in'], 'conv_a_w': out['conv_a_w'], 'conv_a_b': out['conv_a_b'], 'ln_a_g': out['ln_a_g'], 'ln_a_b': out['ln_a_b'], 'conv_b_w': out['conv_b_w'], 'w_out': out['w_out'], 'g_xattn': out['g_xattn'], 'g_mem': out['g_mem'], 'w_q': out['w_q'], 'w_k': out['w_k'], 'w_v': out['w_v'], 'w_o': out['w_o'], 'g_ffn': out['g_ffn'], 'w_gate': out['w_gate'], 'w_up': out['w_up'], 'conv_f_w': out['conv_f_w'], 'w_down': out['w_down'], 'g_final': out['g_final'], 'loss_target': out['loss_target'], 'm_g_mix': out['m_g_mix'], 'm_w_in': out['m_w_in'], 'm_conv_a_w': out['m_conv_a_w'], 'm_conv_a_b': out['m_conv_a_b'], 'm_ln_a_g': out['m_ln_a_g'], 'm_ln_a_b': out['m_ln_a_b'], 'm_conv_b_w': out['m_conv_b_w'], 'm_w_out': out['m_w_out'], 'm_g_xattn': out['m_g_xattn'], 'm_g_mem': out['m_g_mem'], 'm_w_q': out['m_w_q'], 'm_w_k': out['m_w_k'], 'm_w_v': out['m_w_v'], 'm_w_o': out['m_w_o'], 'm_g_ffn': out['m_g_ffn'], 'm_w_gate': out['m_w_gate'], 'm_w_up': out['m_w_up'], 'm_conv_f_w': out['m_conv_f_w'], 'm_w_down': out['m_w_down'], 'm_g_final': out['m_g_final'], 'v_g_mix': out['v_g_mix'], 'v_w_in': out['v_w_in'], 'v_conv_a_w': out['v_conv_a_w'], 'v_conv_a_b': out['v_conv_a_b'], 'v_ln_a_g': out['v_ln_a_g'], 'v_ln_a_b': out['v_ln_a_b'], 'v_conv_b_w': out['v_conv_b_w'], 'v_w_out': out['v_w_out'], 'v_g_xattn': out['v_g_xattn'], 'v_g_mem': out['v_g_mem'], 'v_w_q': out['v_w_q'], 'v_w_k': out['v_w_k'], 'v_w_v': out['v_w_v'], 'v_w_o': out['v_w_o'], 'v_g_ffn': out['v_g_ffn'], 'v_w_gate': out['v_w_gate'], 'v_w_up': out['v_w_up'], 'v_conv_f_w': out['v_conv_f_w'], 'v_w_down': out['v_w_down'], 'v_g_final': out['v_g_final']}


def _loss(weights, diff, rest, loss_target):
    with _jax.named_scope("forward"):
        args = {**rest, TWIN_DIFF_INPUT: diff, **{k: w.astype(_WEIGHT_DTYPES[k]) for k, w in weights.items()}}
        y = _forward(args)
    with _jax.named_scope("loss_head"):
        err = _jnp.square(y.astype(_jnp.float32) - loss_target)
        return 0.5 * _jnp.sum(_jnp.mean(err, axis=-1)) if err.ndim else 0.5 * err


def _adamw(w, g, m, v):
    m = ADAM_B1 * m + (1.0 - ADAM_B1) * g
    v = ADAM_B2 * v + (1.0 - ADAM_B2) * _jnp.square(g)
    m_hat = m / (1.0 - ADAM_B1 ** ADAM_STEP)
    v_hat = v / (1.0 - ADAM_B2 ** ADAM_STEP)
    delta = -ADAM_LR * (m_hat / (_jnp.sqrt(v_hat) + ADAM_EPS) + ADAM_WD * w)
    return delta, m, v


def reference(x, mem, g_mix, w_in, conv_a_w, conv_a_b, ln_a_g, ln_a_b, conv_b_w, w_out, g_xattn, g_mem, w_q, w_k, w_v, w_o, g_ffn, w_gate, w_up, conv_f_w, w_down, g_final, loss_target, m_g_mix, m_w_in, m_conv_a_w, m_conv_a_b, m_ln_a_g, m_ln_a_b, m_conv_b_w, m_w_out, m_g_xattn, m_g_mem, m_w_q, m_w_k, m_w_v, m_w_o, m_g_ffn, m_w_gate, m_w_up, m_conv_f_w, m_w_down, m_g_final, v_g_mix, v_w_in, v_conv_a_w, v_conv_a_b, v_ln_a_g, v_ln_a_b, v_conv_b_w, v_w_out, v_g_xattn, v_g_mem, v_w_q, v_w_k, v_w_v, v_w_o, v_g_ffn, v_w_gate, v_w_up, v_conv_f_w, v_w_down, v_g_final):
    given = dict(x=x, mem=mem, g_mix=g_mix, w_in=w_in, conv_a_w=conv_a_w, conv_a_b=conv_a_b, ln_a_g=ln_a_g, ln_a_b=ln_a_b, conv_b_w=conv_b_w, w_out=w_out, g_xattn=g_xattn, g_mem=g_mem, w_q=w_q, w_k=w_k, w_v=w_v, w_o=w_o, g_ffn=g_ffn, w_gate=w_gate, w_up=w_up, conv_f_w=conv_f_w, w_down=w_down, g_final=g_final, loss_target=loss_target, m_g_mix=m_g_mix, m_w_in=m_w_in, m_conv_a_w=m_conv_a_w, m_conv_a_b=m_conv_a_b, m_ln_a_g=m_ln_a_g, m_ln_a_b=m_ln_a_b, m_conv_b_w=m_conv_b_w, m_w_out=m_w_out, m_g_xattn=m_g_xattn, m_g_mem=m_g_mem, m_w_q=m_w_q, m_w_k=m_w_k, m_w_v=m_w_v, m_w_o=m_w_o, m_g_ffn=m_g_ffn, m_w_gate=m_w_gate, m_w_up=m_w_up, m_conv_f_w=m_conv_f_w, m_w_down=m_w_down, m_g_final=m_g_final, v_g_mix=v_g_mix, v_w_in=v_w_in, v_conv_a_w=v_conv_a_w, v_conv_a_b=v_conv_a_b, v_ln_a_g=v_ln_a_g, v_ln_a_b=v_ln_a_b, v_conv_b_w=v_conv_b_w, v_w_out=v_w_out, v_g_xattn=v_g_xattn, v_g_mem=v_g_mem, v_w_q=v_w_q, v_w_k=v_w_k, v_w_v=v_w_v, v_w_o=v_w_o, v_g_ffn=v_g_ffn, v_w_gate=v_w_gate, v_w_up=v_w_up, v_conv_f_w=v_conv_f_w, v_w_down=v_w_down, v_g_final=v_g_final)
    weights = {n: given[n] for n in TWIN_WEIGHTS}
    shared = {n: given[n] for n in SHARED_INPUTS}
    per_example = {n: given[n] for n in ['x', 'mem']}
    grad_fn = _jax.value_and_grad(_loss, argnums=(0, 1))

    def one_microbatch(ex, loss_target):
        ex = dict(ex)
        diff = ex.pop(TWIN_DIFF_INPUT)
        return grad_fn(weights, diff, {**shared, **ex}, loss_target)

    if N_MICROBATCH == 1:
        loss, (grad_w, grad_x) = one_microbatch(per_example, given["loss_target"])
    else:
        def body(carry, xs):
            loss_sum, grad_sum = carry
            l_k, (gw_k, gx_k) = one_microbatch(xs[0], xs[1])
            with _jax.named_scope("update"):
                return (loss_sum + l_k, _jax.tree.map(_jnp.add, grad_sum, gw_k)), gx_k

        init = (_jnp.zeros((), _jnp.float32), _jax.tree.map(_jnp.zeros_like, weights))
        (loss, grad_w), grad_x = _jax.lax.scan(body, init, (per_example, given["loss_target"]))
    with _jax.named_scope("update"):
        delta_w, new_m, new_v = {}, {}, {}
        for n in TWIN_WEIGHTS:
            delta_w[n], new_m[n], new_v[n] = _adamw(weights[n], grad_w[n], given["m_" + n], given["v_" + n])
    return (loss, grad_x, *[grad_w[n] for n in TWIN_WEIGHTS], *[delta_w[n] for n in TWIN_WEIGHTS],
            *[new_m[n] for n in TWIN_WEIGHTS], *[new_v[n] for n in TWIN_WEIGHTS])
```

```python
import functools

import jax
import jax.numpy as jnp
from jax import lax
from jax.experimental import pallas as pl
from jax.experimental.pallas import tpu as pltpu

F32 = jnp.float32
BF16 = jnp.bfloat16
EPS = 1e-6
N_HEADS = 4
GROUP = 128
KA = 31
KS = 3
KA_ROWS = 32
KS_ROWS = 8
HALO_A = 32
HALO_S = 8
N_CHIPS = 4
N_DEV = 8
LANES = 128
VMEM_LIMIT = 56 * 1024 * 1024
ADAMW_BLOCK_BYTES = 1 << 20
MESH = pl.DeviceIdType.MESH
ANY = pl.BlockSpec(memory_space=pl.ANY)

ADAM_LR = 0.001
ADAM_B1 = 0.9
ADAM_B2 = 0.999
ADAM_EPS = 1e-08
ADAM_WD = 0.01
ADAM_STEP = 10


def _pick(dim, pref, mult=LANES):
    if dim <= pref:
        return dim
    t = (pref // mult) * mult
    while t >= mult:
        if dim % t == 0:
            return t
        t -= mult
    return dim


def _round_up(n, m):
    return ((n + m - 1) // m) * m


def _params(*sem):
    return pltpu.CompilerParams(dimension_semantics=sem, vmem_limit_bytes=VMEM_LIMIT)


def _sigmoid(x):
    return jax.nn.sigmoid(x)


def _mm(a, b, *, name, ta=False, tb=False, out_dtype=F32, add=None, tm=1024, tn=1024, tk=1024):
    if ta:
        K, M = a.shape
    else:
        M, K = a.shape
    if tb:
        N, K2 = b.shape
    else:
        K2, N = b.shape
    assert K == K2, (a.shape, b.shape)
    tm, tn, tk = _pick(M, tm), _pick(N, tn), _pick(K, tk)
    nk = K // tk
    a_spec = pl.BlockSpec((tk, tm), lambda i, j, k: (k, i)) if ta else pl.BlockSpec((tm, tk), lambda i, j, k: (i, k))
    b_spec = pl.BlockSpec((tn, tk), lambda i, j, k: (j, k)) if tb else pl.BlockSpec((tk, tn), lambda i, j, k: (k, j))
    o_spec = pl.BlockSpec((tm, tn), lambda i, j, k: (i, j))
    dims = (((0,) if ta else (1,), (1,) if tb else (0,)), ((), ()))
    has_add = add is not None

    def body(*refs):
        if has_add:
            a_ref, b_ref, add_ref, o_ref, acc_ref = refs
        else:
            a_ref, b_ref, o_ref, acc_ref = refs
            add_ref = None
        k = pl.program_id(2)
        part = lax.dot_general(a_ref[...], b_ref[...], dims, preferred_element_type=F32)

        def finish(r):
            if add_ref is not None:
                r = add_ref[...] + r
            o_ref[...] = r.astype(out_dtype)

        if nk == 1:
            finish(part)
        else:
            @pl.when(k == 0)
            def _():
                acc_ref[...] = part

            @pl.when(jnp.logical_and(k > 0, k < nk - 1))
            def _():
                acc_ref[...] += part

            @pl.when(k == nk - 1)
            def _():
                finish(acc_ref[...] + part)

    in_specs = [a_spec, b_spec] + ([o_spec] if has_add else [])
    args = (a, b) + ((add,) if has_add else ())
    return pl.pallas_call(
        body,
        name=name,
        out_shape=jax.ShapeDtypeStruct((M, N), out_dtype),
        grid=(M // tm, N // tn, nk),
        in_specs=in_specs,
        out_specs=o_spec,
        scratch_shapes=[pltpu.VMEM((tm, tn), F32)],
        compiler_params=_params("parallel", "parallel", "arbitrary"),
    )(*args)


def _norm_fwd(h, g, *, name):
    T, D = h.shape
    R = _pick(T, 256, 8)

    def body(h_ref, g_ref, xn_ref, r_ref):
        x = h_ref[...]
        r = lax.rsqrt(jnp.mean(x * x, axis=-1, keepdims=True) + EPS)
        xn_ref[...] = ((x * r) * g_ref[...]).astype(BF16)
        r_ref[...] = r

    return pl.pallas_call(
        body,
        name=name,
        out_shape=(jax.ShapeDtypeStruct((T, D), BF16), jax.ShapeDtypeStruct((T, 1), F32)),
        grid=(T // R,),
        in_specs=[pl.BlockSpec((R, D), lambda i: (i, 0)), pl.BlockSpec((1, D), lambda i: (0, 0))],
        out_specs=(pl.BlockSpec((R, D), lambda i: (i, 0)), pl.BlockSpec((R, 1), lambda i: (i, 0))),
        compiler_params=_params("parallel"),
    )(h, g)


def _norm_bwd(h, r, g, dxn, dres, *, name, want_dh=True):
    T, D = h.shape
    R = _pick(T, 128, 8)
    has_res = dres is not None

    def body(*refs):
        h_ref, r_ref, g_ref, dxn_ref = refs[:4]
        pos = 4
        dres_ref = None
        if has_res:
            dres_ref = refs[pos]
            pos += 1
        if want_dh:
            dh_ref, dhb_ref, dg_ref = refs[pos:pos + 3]
        else:
            dg_ref = refs[pos]
        i = pl.program_id(0)
        rr = r_ref[...]
        hn = h_ref[...] * rr
        d = dxn_ref[...].astype(F32)
        gd = d * g_ref[...]
        part = jnp.sum(d * hn, axis=0, keepdims=True)

        @pl.when(i == 0)
        def _():
            dg_ref[...] = part

        @pl.when(i > 0)
        def _():
            dg_ref[...] += part

        if want_dh:
            dh = rr * (gd - hn * jnp.mean(gd * hn, axis=-1, keepdims=True))
            if dres_ref is not None:
                dh = dres_ref[...] + dh
            dh_ref[...] = dh
            dhb_ref[...] = dh.astype(BF16)

    row = pl.BlockSpec((R, D), lambda i: (i, 0))
    vec = pl.BlockSpec((1, D), lambda i: (0, 0))
    in_specs = [row, pl.BlockSpec((R, 1), lambda i: (i, 0)), vec, row] + ([row] if has_res else [])
    args = (h, r, g, dxn) + ((dres,) if has_res else ())
    if want_dh:
        out_shape = (jax.ShapeDtypeStruct((T, D), F32), jax.ShapeDtypeStruct((T, D), BF16), jax.ShapeDtypeStruct((1, D), F32))
        out_specs = (row, row, vec)
    else:
        out_shape = jax.ShapeDtypeStruct((1, D), F32)
        out_specs = vec
    return pl.pallas_call(
        body, name=name, out_shape=out_shape, grid=(T // R,), in_specs=in_specs, out_specs=out_specs,
        compiler_params=_params("arbitrary"),
    )(*args)


def _loss_head(h, g, tgt, *, name):
    T, D = h.shape
    R = _pick(T, 128, 8)

    def body(h_ref, g_ref, t_ref, loss_ref, dh_ref, dhb_ref, dg_ref):
        i = pl.program_id(0)
        x = h_ref[...]
        gg = g_ref[...]
        r = lax.rsqrt(jnp.mean(x * x, axis=-1, keepdims=True) + EPS)
        hn = x * r
        e = hn * gg - t_ref[...]
        loss_ref[...] = 0.5 * jnp.mean(e * e, axis=-1, keepdims=True)
        dy = e * (1.0 / D)
        gd = dy * gg
        dh = r * (gd - hn * jnp.mean(gd * hn, axis=-1, keepdims=True))
        dh_ref[...] = dh
        dhb_ref[...] = dh.astype(BF16)
        part = jnp.sum(dy * hn, axis=0, keepdims=True)

        @pl.when(i == 0)
        def _():
            dg_ref[...] = part

        @pl.when(i > 0)
        def _():
            dg_ref[...] += part

    row = pl.BlockSpec((R, D), lambda i: (i, 0))
    vec = pl.BlockSpec((1, D), lambda i: (0, 0))
    return pl.pallas_call(
        body,
        name=name,
        out_shape=(jax.ShapeDtypeStruct((T, 1), F32), jax.ShapeDtypeStruct((T, D), F32),
                   jax.ShapeDtypeStruct((T, D), BF16), jax.ShapeDtypeStruct((1, D), F32)),
        grid=(T // R,),
        in_specs=[row, vec, row],
        out_specs=(pl.BlockSpec((R, 1), lambda i: (i, 0)), row, row, vec),
        compiler_params=_params("arbitrary"),
    )(h, g, tgt)


_NT = (((1,), (1,)), ((), ()))
_TN = (((0,), (0,)), ((), ()))
_NN = (((1,), (0,)), ((), ()))


def _softmax_rows(s):
    m = jnp.max(s, axis=-1, keepdims=True)
    e = jnp.exp(s - m)
    return e / jnp.sum(e, axis=-1, keepdims=True)


def _attn_fwd(q, k, v, *, name):
    T, D = q.shape
    ML = k.shape[0]
    dh = D // N_HEADS
    scale = dh ** -0.5
    R = _pick(T, 512, 16)

    def body(q_ref, k_ref, v_ref, o_ref):
        s = lax.dot_general(q_ref[...], k_ref[...], _NT, preferred_element_type=F32) * scale
        p = _softmax_rows(s)
        o_ref[...] = lax.dot_general(p.astype(BF16), v_ref[...], _NN, preferred_element_type=F32).astype(BF16)

    qs = pl.BlockSpec((R, dh), lambda i, h: (i, h))
    ks = pl.BlockSpec((ML, dh), lambda i, h: (0, h))
    return pl.pallas_call(
        body, name=name, out_shape=jax.ShapeDtypeStruct((T, D), BF16), grid=(T // R, N_HEADS),
        in_specs=[qs, ks, ks], out_specs=qs, compiler_params=_params("parallel", "parallel"),
    )(q, k, v)


def _attn_bwd(q, k, v, do, *, name):
    T, D = q.shape
    ML = k.shape[0]
    dh = D // N_HEADS
    scale = dh ** -0.5
    R = _pick(T, 512, 16)

    def body(q_ref, k_ref, v_ref, do_ref, dq_ref, dk_ref, dv_ref):
        i = pl.program_id(1)
        qq, kk, vv, dd = q_ref[...], k_ref[...], v_ref[...], do_ref[...]
        s = lax.dot_general(qq, kk, _NT, preferred_element_type=F32) * scale
        p = _softmax_rows(s)
        dp = lax.dot_general(dd, vv, _NT, preferred_element_type=F32)
        dv_part = lax.dot_general(p.astype(BF16), dd, _TN, preferred_element_type=F32)
        ds = (p * (dp - jnp.sum(p * dp, axis=-1, keepdims=True)) * scale).astype(BF16)
        dq_ref[...] = lax.dot_general(ds, kk, _NN, preferred_element_type=F32).astype(BF16)
        dk_part = lax.dot_general(ds, qq, _TN, preferred_element_type=F32)

        @pl.when(i == 0)
        def _():
            dk_ref[...] = dk_part
            dv_ref[...] = dv_part

        @pl.when(i > 0)
        def _():
            dk_ref[...] += dk_part
            dv_ref[...] += dv_part

    qs = pl.BlockSpec((R, dh), lambda h, i: (i, h))
    ks = pl.BlockSpec((ML, dh), lambda h, i: (0, h))
    return pl.pallas_call(
        body,
        name=name,
        out_shape=(jax.ShapeDtypeStruct((T, D), BF16), jax.ShapeDtypeStruct((ML, D), F32), jax.ShapeDtypeStruct((ML, D), F32)),
        grid=(N_HEADS, T // R),
        in_specs=[qs, ks, ks, qs],
        out_specs=(qs, ks, ks),
        compiler_params=_params("parallel", "arbitrary"),
    )(q, k, v, do)


def _prev_halo(R, halo, CT, col):
    per = R // halo
    return lambda i, j: (jnp.maximum(i * per - 1, 0), col(j))


def _next_halo(R, halo, T, col):
    per = R // halo
    last = T // halo - 1
    return lambda i, j: (jnp.minimum((i + 1) * per, last), col(j))


def _swap(f):
    return lambda j, i: f(i, j)


def _ffn_act_fwd(gp, up, wf, *, name):
    T, F = gp.shape
    R = _pick(T, 256, 8)
    CT = _pick(F, 1024)
    H = HALO_S

    def body(g_ref, gh_ref, u_ref, w_ref, f_ref, ext):
        i = pl.program_id(0)
        ext[pl.ds(0, H), :] = jnp.where(i > 0, gh_ref[...], 0.0)
        ext[pl.ds(H, R), :] = g_ref[...]
        w = w_ref[...]
        g = ext[pl.ds(H - 2, R), :] * w[0:1, :]
        g = g + ext[pl.ds(H - 1, R), :] * w[1:2, :]
        g = g + ext[pl.ds(H, R), :] * w[2:3, :]
        f_ref[...] = (g * _sigmoid(g) * u_ref[...]).astype(BF16)

    main = pl.BlockSpec((R, CT), lambda i, j: (i, j))
    return pl.pallas_call(
        body,
        name=name,
        out_shape=jax.ShapeDtypeStruct((T, F), BF16),
        grid=(T // R, F // CT),
        in_specs=[main, pl.BlockSpec((H, CT), _prev_halo(R, H, CT, lambda j: j)), main,
                  pl.BlockSpec((KS_ROWS, CT), lambda i, j: (0, j))],
        out_specs=main,
        scratch_shapes=[pltpu.VMEM((R + H, CT), F32)],
        compiler_params=_params("parallel", "parallel"),
    )(gp, gp, up, wf)


def _ffn_act_bwd(gp, up, df, wf, *, name):
    T, F = gp.shape
    R = _pick(T, 256, 16)
    CT = _pick(F, 1024)
    H = HALO_S
    HB = 16
    n_t = T // R

    def body(g_ref, gp_ref, gn_ref, u_ref, un_ref, d_ref, dn_ref, w_ref, dg_out, du_out, dw_ref, ext, dge):
        i = pl.program_id(1)
        last = i == n_t - 1
        ext[pl.ds(0, H), :] = jnp.where(i > 0, gp_ref[...], 0.0)
        ext[pl.ds(H, R), :] = g_ref[...]
        ext[pl.ds(H + R, H), :] = gn_ref[...]
        w = w_ref[...]
        w0, w1, w2 = w[0:1, :], w[1:2, :], w[2:3, :]

        def conv(start, n):
            g = ext[pl.ds(start + H - 2, n), :] * w0
            g = g + ext[pl.ds(start + H - 1, n), :] * w1
            return g + ext[pl.ds(start + H, n), :] * w2

        def dact(g, u, d):
            sg = _sigmoid(g)
            return d * u * (sg * (1.0 + g * (1.0 - sg))), d * (g * sg)

        d_main = d_ref[...].astype(F32)
        dg_main, du_main = dact(conv(0, R), u_ref[...], d_main)
        du_out[...] = du_main.astype(BF16)
        d_next = jnp.where(last, 0.0, dn_ref[pl.ds(0, H), :].astype(F32))
        dg_next, _ = dact(conv(R, H), un_ref[...], d_next)
        dg_next = jnp.where(last, 0.0, dg_next)
        dge[pl.ds(0, R), :] = dg_main
        dge[pl.ds(R, H), :] = dg_next
        dgp = dge[pl.ds(0, R), :] * w2 + dge[pl.ds(1, R), :] * w1 + dge[pl.ds(2, R), :] * w0
        dg_out[...] = dgp.astype(BF16)
        rows = [jnp.sum(dg_main * ext[pl.ds(H - 2 + k, R), :], axis=0, keepdims=True) for k in range(KS)]
        rows.append(jnp.zeros((KS_ROWS - KS, CT), F32))
        _acc_rows_block(dw_ref, i, rows)

    col = lambda j: j
    main = pl.BlockSpec((R, CT), lambda j, i: (i, j))
    prev8 = pl.BlockSpec((H, CT), _swap(_prev_halo(R, H, CT, col)))
    next8 = pl.BlockSpec((H, CT), _swap(_next_halo(R, H, T, col)))
    next16 = pl.BlockSpec((HB, CT), _swap(_next_halo(R, HB, T, col)))
    wspec = pl.BlockSpec((KS_ROWS, CT), lambda j, i: (0, j))
    return pl.pallas_call(
        body,
        name=name,
        out_shape=(jax.ShapeDtypeStruct((T, F), BF16), jax.ShapeDtypeStruct((T, F), BF16), jax.ShapeDtypeStruct((KS_ROWS, F), F32)),
        grid=(F // CT, n_t),
        in_specs=[main, prev8, next8, main, next8, main, next16, wspec],
        out_specs=(main, main, wspec),
        scratch_shapes=[pltpu.VMEM((R + 2 * H, CT), F32), pltpu.VMEM((R + H, CT), F32)],
        compiler_params=_params("parallel", "arbitrary"),
    )(gp, gp, gp, up, up, df, df, wf)


def _acc_rows_block(ref, i, rows):
    *singles, pad = rows

    @pl.when(i == 0)
    def _():
        for k, row in enumerate(singles):
            ref[pl.ds(k, 1), :] = row
        ref[pl.ds(len(singles), pad.shape[0]), :] = pad

    @pl.when(i > 0)
    def _():
        for k, row in enumerate(singles):
            ref[pl.ds(k, 1), :] += row


def _group_stats(x):
    mu = jnp.mean(x, axis=-1, keepdims=True)
    xc = x - mu
    var = jnp.mean(xc * xc, axis=-1, keepdims=True)
    return xc, lax.rsqrt(var + EPS)


def _mix_a_fwd(proj, wa, ba, lg, lb, *, cw, name):
    T = proj.shape[0]
    R = _pick(T, 128, HALO_A)
    CT = _pick(cw, 256)
    nc = cw // CT
    H = HALO_A

    def body(av_ref, ag_ref, avh_ref, agh_ref, w_ref, b_ref, lg_ref, lb_ref, u3_ref, u1_ref, ext):
        i = pl.program_id(0)
        ext[pl.ds(0, H), :] = jnp.where(i > 0, avh_ref[...] * _sigmoid(agh_ref[...]), 0.0)
        ext[pl.ds(H, R), :] = av_ref[...] * _sigmoid(ag_ref[...])
        w = w_ref[...]
        acc = ext[pl.ds(H - (KA - 1), R), :] * w[0:1, :]
        for k in range(1, KA):
            acc = acc + ext[pl.ds(H - (KA - 1) + k, R), :] * w[k:k + 1, :]
        u1 = acc + b_ref[...]
        u1_ref[...] = u1
        for gi in range(CT // GROUP):
            sl = slice(gi * GROUP, (gi + 1) * GROUP)
            xc, rstd = _group_stats(u1[:, sl])
            u2 = (xc * rstd) * lg_ref[:, sl] + lb_ref[:, sl]
            u3_ref[:, sl] = (u2 * _sigmoid(u2)).astype(BF16)

    main_v = pl.BlockSpec((R, CT), lambda i, j: (i, j))
    main_g = pl.BlockSpec((R, CT), lambda i, j: (i, j + nc))
    halo_v = pl.BlockSpec((H, CT), _prev_halo(R, H, CT, lambda j: j))
    halo_g = pl.BlockSpec((H, CT), _prev_halo(R, H, CT, lambda j: j + nc))
    vec = pl.BlockSpec((1, CT), lambda i, j: (0, j))
    out = pl.BlockSpec((R, CT), lambda i, j: (i, j))
    return pl.pallas_call(
        body,
        name=name,
        out_shape=(jax.ShapeDtypeStruct((T, cw), BF16), jax.ShapeDtypeStruct((T, cw), F32)),
        grid=(T // R, nc),
        in_specs=[main_v, main_g, halo_v, halo_g, pl.BlockSpec((KA_ROWS, CT), lambda i, j: (0, j)), vec, vec, vec],
        out_specs=(out, out),
        scratch_shapes=[pltpu.VMEM((R + H, CT), F32)],
        compiler_params=_params("parallel", "parallel"),
    )(proj, proj, proj, proj, wa, ba, lg, lb)


def _mix_a_bwd(proj, u1, dmix, wa, lg, lb, *, cw, name):
    T = proj.shape[0]
    R = _pick(T, 128, HALO_A)
    CT = _pick(cw, 256)
    nc = cw // CT
    H = HALO_A
    n_t = T // R
    NG = CT // GROUP

    def body(av_ref, ag_ref, avh_ref, agh_ref, u1_ref, u1n_ref, d3_ref, d3n_ref, w_ref, lg_ref, lb_ref,
             dav_ref, dag_ref, dw_ref, db_ref, dlg_ref, dlb_ref, ext, d1):
        i = pl.program_id(1)
        last = i == n_t - 1

        def ln_bwd(u1, d3, sl):
            xc, rstd = _group_stats(u1)
            xh = xc * rstd
            g = lg_ref[:, sl]
            u2 = xh * g + lb_ref[:, sl]
            sg = _sigmoid(u2)
            du2 = d3 * (sg * (1.0 + u2 * (1.0 - sg)))
            dxh = du2 * g
            du1 = rstd * (dxh - jnp.mean(dxh, axis=-1, keepdims=True) - xh * jnp.mean(dxh * xh, axis=-1, keepdims=True))
            return du1, du2 * xh, du2

        dlg_parts, dlb_parts = [], []
        for gi in range(NG):
            sl = slice(gi * GROUP, (gi + 1) * GROUP)
            du1, dlg, dlb = ln_bwd(u1_ref[:, sl], d3_ref[:, sl], sl)
            d1[pl.ds(0, R), sl] = du1
            dlg_parts.append(jnp.sum(dlg, axis=0, keepdims=True))
            dlb_parts.append(jnp.sum(dlb, axis=0, keepdims=True))
            du1n, _, _ = ln_bwd(u1n_ref[:, sl], jnp.where(last, 0.0, d3n_ref[:, sl]), sl)
            d1[pl.ds(R, H), sl] = jnp.where(last, 0.0, du1n)
        dlg_row = jnp.concatenate(dlg_parts, axis=1) if NG > 1 else dlg_parts[0]
        dlb_row = jnp.concatenate(dlb_parts, axis=1) if NG > 1 else dlb_parts[0]
        d1_main = d1[pl.ds(0, R), :]
        db_row = jnp.sum(d1_main, axis=0, keepdims=True)

        @pl.when(i == 0)
        def _():
            dlg_ref[...] = dlg_row
            dlb_ref[...] = dlb_row
            db_ref[...] = db_row

        @pl.when(i > 0)
        def _():
            dlg_ref[...] += dlg_row
            dlb_ref[...] += dlb_row
            db_ref[...] += db_row

        w = w_ref[...]
        du0 = d1[pl.ds(KA - 1, R), :] * w[0:1, :]
        for k in range(1, KA):
            du0 = du0 + d1[pl.ds(KA - 1 - k, R), :] * w[k:k + 1, :]
        av = av_ref[...]
        sg = _sigmoid(ag_ref[...])
        dav_ref[...] = (du0 * sg).astype(BF16)
        dag_ref[...] = (du0 * av * sg * (1.0 - sg)).astype(BF16)

        ext[pl.ds(0, H), :] = jnp.where(i > 0, avh_ref[...] * _sigmoid(agh_ref[...]), 0.0)
        ext[pl.ds(H, R), :] = av * sg
        rows = [jnp.sum(d1_main * ext[pl.ds(H - (KA - 1) + k, R), :], axis=0, keepdims=True) for k in range(KA)]
        rows.append(jnp.zeros((KA_ROWS - KA, CT), F32))
        _acc_rows_block(dw_ref, i, rows)

    cv = lambda j: j
    cg = lambda j: j + nc
    main_v = pl.BlockSpec((R, CT), lambda j, i: (i, j))
    main_g = pl.BlockSpec((R, CT), lambda j, i: (i, j + nc))
    prev_v = pl.BlockSpec((H, CT), _swap(_prev_halo(R, H, CT, cv)))
    prev_g = pl.BlockSpec((H, CT), _swap(_prev_halo(R, H, CT, cg)))
    nxt = pl.BlockSpec((H, CT), _swap(_next_halo(R, H, T, cv)))
    wspec = pl.BlockSpec((KA_ROWS, CT), lambda j, i: (0, j))
    vec = pl.BlockSpec((1, CT), lambda j, i: (0, j))
    vshape = jax.ShapeDtypeStruct((1, cw), F32)
    return pl.pallas_call(
        body,
        name=name,
        out_shape=(jax.ShapeDtypeStruct((T, cw), BF16), jax.ShapeDtypeStruct((T, cw), BF16),
                   jax.ShapeDtypeStruct((KA_ROWS, cw), F32), vshape, vshape, vshape),
        grid=(nc, n_t),
        in_specs=[main_v, main_g, prev_v, prev_g, main_v, nxt, main_v, nxt, wspec, vec, vec],
        out_specs=(main_v, main_v, wspec, vec, vec, vec),
        scratch_shapes=[pltpu.VMEM((R + H, CT), F32), pltpu.VMEM((R + H, CT), F32)],
        compiler_params=_params("parallel", "arbitrary"),
    )(proj, proj, proj, proj, u1, u1, dmix, dmix, wa, lg, lb)


def _mix_b_fwd(proj, wb, *, cw, sw, name):
    T = proj.shape[0]
    R = _pick(T, 256, 8)
    CT = _pick(sw, 512)
    nb, nc, nh = (2 * cw) // CT, (2 * cw + sw) // CT, (2 * cw + 2 * sw) // CT
    H = HALO_S

    def body(b_ref, c_ref, h_ref, ch_ref, hh_ref, w_ref, v_ref, ext):
        i = pl.program_id(0)
        ext[pl.ds(0, H), :] = jnp.where(i > 0, ch_ref[...] * hh_ref[...], 0.0)
        ext[pl.ds(H, R), :] = c_ref[...] * h_ref[...]
        w = w_ref[...]
        zc = ext[pl.ds(H - 2, R), :] * w[0:1, :]
        zc = zc + ext[pl.ds(H - 1, R), :] * w[1:2, :]
        zc = zc + ext[pl.ds(H, R), :] * w[2:3, :]
        v_ref[...] = (b_ref[...] * zc).astype(BF16)

    def main(off):
        return pl.BlockSpec((R, CT), lambda i, j: (i, j + off))

    def prev(off):
        return pl.BlockSpec((H, CT), _prev_halo(R, H, CT, lambda j: j + off))

    return pl.pallas_call(
        body,
        name=name,
        out_shape=jax.ShapeDtypeStruct((T, sw), BF16),
        grid=(T // R, sw // CT),
        in_specs=[main(nb), main(nc), main(nh), prev(nc), prev(nh), pl.BlockSpec((KS_ROWS, CT), lambda i, j: (0, j))],
        out_specs=pl.BlockSpec((R, CT), lambda i, j: (i, j)),
        scratch_shapes=[pltpu.VMEM((R + H, CT), F32)],
        compiler_params=_params("parallel", "parallel"),
    )(proj, proj, proj, proj, proj, wb)


def _mix_b_bwd(proj, dmix, wb, *, cw, sw, name):
    T = proj.shape[0]
    R = _pick(T, 256, 8)
    CT = _pick(sw, 512)
    nb, nc, nh = (2 * cw) // CT, (2 * cw + sw) // CT, (2 * cw + 2 * sw) // CT
    nd = cw // CT
    H = HALO_S
    n_t = T // R

    def body(b_ref, bn_ref, c_ref, cp_ref, cn_ref, h_ref, hp_ref, hn_ref, d_ref, dn_ref, w_ref,
             db_ref, dc_ref, dhh_ref, dw_ref, ext, dze):
        i = pl.program_id(1)
        last = i == n_t - 1
        cc, hh = c_ref[...], h_ref[...]
        ext[pl.ds(0, H), :] = jnp.where(i > 0, cp_ref[...] * hp_ref[...], 0.0)
        ext[pl.ds(H, R), :] = cc * hh
        ext[pl.ds(H + R, H), :] = cn_ref[...] * hn_ref[...]
        w = w_ref[...]
        w0, w1, w2 = w[0:1, :], w[1:2, :], w[2:3, :]

        def conv(start, n):
            z = ext[pl.ds(start + H - 2, n), :] * w0
            z = z + ext[pl.ds(start + H - 1, n), :] * w1
            return z + ext[pl.ds(start + H, n), :] * w2

        d_main = d_ref[...]
        db_ref[...] = (d_main * conv(0, R)).astype(BF16)
        dzc_main = d_main * b_ref[...]
        dze[pl.ds(0, R), :] = dzc_main
        dze[pl.ds(R, H), :] = jnp.where(last, 0.0, dn_ref[...] * bn_ref[...])
        dz = dze[pl.ds(0, R), :] * w2 + dze[pl.ds(1, R), :] * w1 + dze[pl.ds(2, R), :] * w0
        dc_ref[...] = (dz * hh).astype(BF16)
        dhh_ref[...] = (dz * cc).astype(BF16)
        rows = [jnp.sum(dzc_main * ext[pl.ds(H - 2 + k, R), :], axis=0, keepdims=True) for k in range(KS)]
        rows.append(jnp.zeros((KS_ROWS - KS, CT), F32))
        _acc_rows_block(dw_ref, i, rows)

    def main(off):
        return pl.BlockSpec((R, CT), lambda j, i: (i, j + off))

    def prev(off):
        return pl.BlockSpec((H, CT), _swap(_prev_halo(R, H, CT, lambda j: j + off)))

    def nxt(off):
        return pl.BlockSpec((H, CT), _swap(_next_halo(R, H, T, lambda j: j + off)))

    out = pl.BlockSpec((R, CT), lambda j, i: (i, j))
    wspec = pl.BlockSpec((KS_ROWS, CT), lambda j, i: (0, j))
    act = jax.ShapeDtypeStruct((T, sw), BF16)
    return pl.pallas_call(
        body,
        name=name,
        out_shape=(act, act, act, jax.ShapeDtypeStruct((KS_ROWS, sw), F32)),
        grid=(sw // CT, n_t),
        in_specs=[main(nb), nxt(nb), main(nc), prev(nc), nxt(nc), main(nh), prev(nh), nxt(nh), main(nd), nxt(nd), wspec],
        out_specs=(out, out, out, wspec),
        scratch_shapes=[pltpu.VMEM((R + 2 * H, CT), F32), pltpu.VMEM((R + H, CT), F32)],
        compiler_params=_params("parallel", "arbitrary"),
    )(proj, proj, proj, proj, proj, proj, proj, proj, dmix, dmix, wb)


def _adamw(w, g, m, v, *, name):
    Rr, Cc = w.shape
    R = _pick(Rr, max(8, ADAMW_BLOCK_BYTES // (4 * Cc)), 8)

    def body(w_ref, g_ref, m_ref, v_ref, d_ref, mo_ref, vo_ref):
        g = g_ref[...]
        m2 = ADAM_B1 * m_ref[...] + (1.0 - ADAM_B1) * g
        v2 = ADAM_B2 * v_ref[...] + (1.0 - ADAM_B2) * (g * g)
        m_hat = m2 / (1.0 - ADAM_B1 ** ADAM_STEP)
        v_hat = v2 / (1.0 - ADAM_B2 ** ADAM_STEP)
        d_ref[...] = -ADAM_LR * (m_hat / (jnp.sqrt(v_hat) + ADAM_EPS) + ADAM_WD * w_ref[...])
        mo_ref[...] = m2
        vo_ref[...] = v2

    spec = pl.BlockSpec((R, Cc), lambda i: (i, 0))
    shp = jax.ShapeDtypeStruct((Rr, Cc), F32)
    return pl.pallas_call(
        body, name=name, out_shape=(shp, shp, shp), grid=(Rr // R,), in_specs=[spec] * 4, out_specs=(spec, spec, spec),
        compiler_params=_params("parallel"),
    )(w, g, m, v)


def _half_sum(g, ra, c, *, rows_split, name):
    Rr, Cc = ra.shape
    tr, tc = _pick(Rr, 512, 16), _pick(Cc, 2048)
    nr, ncol = Rr // tr, Cc // tc
    if rows_split:
        g_map = lambda i, j, c_ref: (i + c_ref[0] * nr, j)
    else:
        g_map = lambda i, j, c_ref: (i, j + c_ref[0] * ncol)

    def body(c_ref, g_ref, ra_ref, o_ref):
        o_ref[...] = (g_ref[...] + ra_ref[...]).astype(BF16)

    spec = pl.BlockSpec((tr, tc), lambda i, j, c_ref: (i, j))
    return pl.pallas_call(
        body,
        name=name,
        out_shape=jax.ShapeDtypeStruct((Rr, Cc), BF16),
        grid_spec=pltpu.PrefetchScalarGridSpec(
            num_scalar_prefetch=1, grid=(nr, ncol), in_specs=[pl.BlockSpec((tr, tc), g_map), spec], out_specs=spec),
        compiler_params=_params("parallel", "parallel"),
    )(c, g, ra)


def _sum_chips(rb, *, name):
    _, Rr, Cc = rb.shape
    tr, tc = _pick(Rr, 512, 16), _pick(Cc, 2048)

    def body(rb_ref, o_ref):
        acc = rb_ref[0].astype(F32)
        for q in range(1, N_CHIPS):
            acc = acc + rb_ref[q].astype(F32)
        o_ref[...] = acc

    return pl.pallas_call(
        body,
        name=name,
        out_shape=jax.ShapeDtypeStruct((Rr, Cc), F32),
        grid=(Rr // tr, Cc // tc),
        in_specs=[pl.BlockSpec((N_CHIPS, tr, tc), lambda i, j: (0, i, j))],
        out_specs=pl.BlockSpec((tr, tc), lambda i, j: (i, j)),
        compiler_params=_params("parallel", "parallel"),
    )(rb)


def _place():
    x, y, c = lax.axis_index("x"), lax.axis_index("y"), lax.axis_index("c")
    return x, y, c, 2 * x + y


def _other_chips(x, y):
    return [(1 - x, y, 2 * (1 - x) + y), (x, 1 - y, 2 * x + (1 - y)), (1 - x, 1 - y, 2 * (1 - x) + (1 - y))]


def _allgather_small(buf, *, name, reduce):
    S = buf.shape[0]

    def body(x_ref, o_ref, gat, send_sems, recv_sems):
        x, y, c, _ = _place()
        me = 4 * x + 2 * y + c
        gat[me] = x_ref[...]
        copies = []
        for k in range(1, N_DEV):
            fx, fy, fc = (k >> 2) & 1, (k >> 1) & 1, k & 1
            px = 1 - x if fx else x
            py = 1 - y if fy else y
            pc = 1 - c if fc else c
            peer = 4 * px + 2 * py + pc
            send = pltpu.make_async_remote_copy(
                src_ref=x_ref, dst_ref=gat.at[me], send_sem=send_sems.at[k - 1], recv_sem=recv_sems.at[k - 1],
                device_id=(px, py, pc), device_id_type=MESH)
            send.start()
            arrival = pltpu.make_async_remote_copy(
                src_ref=x_ref, dst_ref=gat.at[peer], send_sem=send_sems.at[k - 1], recv_sem=recv_sems.at[k - 1],
                device_id=(px, py, pc), device_id_type=MESH)
            copies.append((send, arrival))
        for send, arrival in copies:
            arrival.wait_recv()
        for send, arrival in copies:
            send.wait_send()
        if reduce:
            acc = gat[0]
            for d in range(1, N_DEV):
                acc = acc + gat[d]
            o_ref[...] = acc
        else:
            o_ref[...] = gat[...]

    out_shape = jax.ShapeDtypeStruct((S, LANES) if reduce else (N_DEV, S, LANES), F32)
    return pl.pallas_call(
        body,
        name=name,
        out_shape=out_shape,
        in_specs=[pl.BlockSpec(memory_space=pltpu.VMEM)],
        out_specs=pl.BlockSpec(memory_space=pltpu.VMEM),
        scratch_shapes=[pltpu.VMEM((N_DEV, S, LANES), F32), pltpu.SemaphoreType.DMA((N_DEV - 1,)),
                        pltpu.SemaphoreType.DMA((N_DEV - 1,))],
        compiler_params=pltpu.CompilerParams(vmem_limit_bytes=VMEM_LIMIT),
    )(buf)


class _Sharded:
    def __init__(self, shape, axis):
        self.shape = shape
        self.axis = axis
        self.block = shape[axis] // N_CHIPS
        self.half = shape[1 - axis] // 2

    def _sl(self, along, across):
        return (along, across) if self.axis == 0 else (across, along)

    def block_slice(self, q):
        return self._sl(pl.ds(q * self.block, self.block), pl.ds(0, self.shape[1 - self.axis]))

    def block_half_slice(self, q, c):
        return self._sl(pl.ds(q * self.block, self.block), pl.ds(c * self.half, self.half))

    def half_slice(self, c):
        return self._sl(pl.ds(0, self.shape[self.axis]), pl.ds(c * self.half, self.half))

    def shard_half_slice(self, c):
        return self._sl(pl.ds(0, self.block), pl.ds(c * self.half, self.half))

    def shard_shape(self):
        return self._sl(self.block, self.shape[1 - self.axis])

    def half_shape(self):
        return self._sl(self.shape[self.axis], self.half)

    def block_half_shape(self):
        return self._sl(self.block, self.half)

    def block_in_half_slice(self, q):
        return self._sl(pl.ds(q * self.block, self.block), pl.ds(0, self.half))


def _at(ref, sl):
    return ref.at[sl[0], sl[1]]


def _allgather_weights(shards, layouts, *, name):
    n = len(shards)

    def body(*refs):
        srcs, outs = refs[:n], refs[n:2 * n]
        local_sems, send_sems, recv_sems, fsend_sems, frecv_sems = refs[2 * n:]
        x, y, c, p = _place()
        sibling = (x, y, 1 - c)
        chips = _other_chips(x, y)
        locals_, sends, passed = [], [], []
        for i in range(n):
            lay = layouts[i]
            cp = pltpu.make_async_copy(srcs[i], _at(outs[i], lay.block_slice(p)), local_sems.at[i])
            cp.start()
            locals_.append(cp)
            for j, (qx, qy, q) in enumerate(chips):
                dst = _at(outs[i], lay.block_half_slice(p, c))
                cp = pltpu.make_async_remote_copy(
                    src_ref=_at(srcs[i], lay.shard_half_slice(c)), dst_ref=dst, send_sem=send_sems.at[i, j],
                    recv_sem=recv_sems.at[i, j], device_id=(qx, qy, c), device_id_type=MESH)
                cp.start()
                sends.append(cp)
        for i in range(n):
            lay = layouts[i]
            for j, (qx, qy, q) in enumerate(chips):
                landed = _at(outs[i], lay.block_half_slice(q, c))
                pltpu.make_async_remote_copy(
                    src_ref=landed, dst_ref=landed, send_sem=send_sems.at[i, j], recv_sem=recv_sems.at[i, j],
                    device_id=(qx, qy, c), device_id_type=MESH).wait_recv()
                cp = pltpu.make_async_remote_copy(
                    src_ref=landed, dst_ref=landed, send_sem=fsend_sems.at[i, j], recv_sem=frecv_sems.at[i, j],
                    device_id=sibling, device_id_type=MESH)
                cp.start()
                passed.append(cp)
        for i in range(n):
            lay = layouts[i]
            for j, (qx, qy, q) in enumerate(chips):
                theirs = _at(outs[i], lay.block_half_slice(q, 1 - c))
                pltpu.make_async_remote_copy(
                    src_ref=theirs, dst_ref=theirs, send_sem=fsend_sems.at[i, j], recv_sem=frecv_sems.at[i, j],
                    device_id=sibling, device_id_type=MESH).wait_recv()
        for cp in sends + passed:
            cp.wait_send()
        for cp in locals_:
            cp.wait()

    sem = pltpu.SemaphoreType.DMA
    return pl.pallas_call(
        body,
        name=name,
        out_shape=[jax.ShapeDtypeStruct(lay.shape, BF16) for lay in layouts],
        in_specs=[ANY] * n,
        out_specs=[ANY] * n,
        scratch_shapes=[sem((n,)), sem((n, 3)), sem((n, 3)), sem((n, 3)), sem((n, 3))],
    )(*shards)


def _rs_sibling_halves(grads, layouts, *, name):
    n = len(grads)

    def body(*refs):
        srcs, outs = refs[:n], refs[n:2 * n]
        send_sems, recv_sems = refs[2 * n:]
        x, y, c, _ = _place()
        copies = []
        for i in range(n):
            cp = pltpu.make_async_remote_copy(
                src_ref=_at(srcs[i], layouts[i].half_slice(1 - c)), dst_ref=outs[i], send_sem=send_sems.at[i],
                recv_sem=recv_sems.at[i], device_id=(x, y, 1 - c), device_id_type=MESH)
            cp.start()
            copies.append(cp)
        for cp in copies:
            cp.wait_recv()
        for cp in copies:
            cp.wait_send()

    sem = pltpu.SemaphoreType.DMA
    return pl.pallas_call(
        body,
        name=name,
        out_shape=[jax.ShapeDtypeStruct(lay.half_shape(), F32) for lay in layouts],
        in_specs=[ANY] * n,
        out_specs=[ANY] * n,
        scratch_shapes=[sem((n,)), sem((n,))],
    )(*grads)


def _rs_chips(sums, layouts, *, name):
    n = len(sums)

    def body(*refs):
        srcs, outs = refs[:n], refs[n:2 * n]
        local_sems, send_sems, recv_sems = refs[2 * n:]
        x, y, c, p = _place()
        chips = _other_chips(x, y)
        locals_, sends = [], []
        for i in range(n):
            lay = layouts[i]
            cp = pltpu.make_async_copy(_at(srcs[i], lay.block_in_half_slice(p)), outs[i].at[p], local_sems.at[i])
            cp.start()
            locals_.append(cp)
            for j, (qx, qy, q) in enumerate(chips):
                cp = pltpu.make_async_remote_copy(
                    src_ref=_at(srcs[i], lay.block_in_half_slice(q)), dst_ref=outs[i].at[p], send_sem=send_sems.at[i, j],
                    recv_sem=recv_sems.at[i, j], device_id=(qx, qy, c), device_id_type=MESH)
                cp.start()
                sends.append(cp)
        for i in range(n):
            for j, (qx, qy, q) in enumerate(chips):
                pltpu.make_async_remote_copy(
                    src_ref=outs[i].at[q], dst_ref=outs[i].at[q], send_sem=send_sems.at[i, j], recv_sem=recv_sems.at[i, j],
                    device_id=(qx, qy, c), device_id_type=MESH).wait_recv()
        for cp in sends:
            cp.wait_send()
        for cp in locals_:
            cp.wait()

    sem = pltpu.SemaphoreType.DMA
    return pl.pallas_call(
        body,
        name=name,
        out_shape=[jax.ShapeDtypeStruct((N_CHIPS,) + lay.block_half_shape(), BF16) for lay in layouts],
        in_specs=[ANY] * n,
        out_specs=[ANY] * n,
        scratch_shapes=[sem((n,)), sem((n, 3)), sem((n, 3))],
    )(*sums)


def _rs_join_halves(halves, layouts, *, name):
    n = len(halves)

    def body(*refs):
        srcs, outs = refs[:n], refs[n:2 * n]
        local_sems, send_sems, recv_sems = refs[2 * n:]
        x, y, c, _ = _place()
        locals_, sends = [], []
        for i in range(n):
            lay = layouts[i]
            mine = _at(outs[i], lay.shard_half_slice(c))
            cp = pltpu.make_async_copy(srcs[i], mine, local_sems.at[i])
            cp.start()
            locals_.append(cp)
            cp = pltpu.make_async_remote_copy(
                src_ref=srcs[i], dst_ref=mine, send_sem=send_sems.at[i], recv_sem=recv_sems.at[i],
                device_id=(x, y, 1 - c), device_id_type=MESH)
            cp.start()
            sends.append(cp)
        for i in range(n):
            theirs = _at(outs[i], layouts[i].shard_half_slice(1 - c))
            pltpu.make_async_remote_copy(
                src_ref=theirs, dst_ref=theirs, send_sem=send_sems.at[i], recv_sem=recv_sems.at[i],
                device_id=(x, y, 1 - c), device_id_type=MESH).wait_recv()
        for cp in sends:
            cp.wait_send()
        for cp in locals_:
            cp.wait()

    sem = pltpu.SemaphoreType.DMA
    return pl.pallas_call(
        body,
        name=name,
        out_shape=[jax.ShapeDtypeStruct(lay.shard_shape(), F32) for lay in layouts],
        in_specs=[ANY] * n,
        out_specs=[ANY] * n,
        scratch_shapes=[sem((n,)), sem((n,)), sem((n,))],
    )(*halves)


def _pack(arrays):
    flat = [a.reshape(-1) for a in arrays]
    sizes = [f.shape[0] for f in flat]
    total = sum(sizes)
    rows = _round_up(-(-total // LANES), 8)
    flat.append(jnp.zeros((rows * LANES - total,), F32))
    return jnp.concatenate(flat).reshape(rows, LANES)


def _unpack(buf, shapes):
    flat = buf.reshape(-1)
    out, pos = [], 0
    for shp in shapes:
        n = 1
        for d in shp:
            n *= d
        out.append(flat[pos:pos + n].reshape(shp))
        pos += n
    return out


def _pad_to(a, rows, cols):
    return jnp.pad(a, ((0, rows - a.shape[0]), (0, cols - a.shape[1])))


def kernel(x, mem, g_mix, w_in, conv_a_w, conv_a_b, ln_a_g, ln_a_b, conv_b_w, w_out, g_xattn, g_mem, w_q, w_k, w_v, w_o, g_ffn, w_gate, w_up, conv_f_w, w_down, g_final, loss_target, m_g_mix, m_w_in, m_conv_a_w, m_conv_a_b, m_ln_a_g, m_ln_a_b, m_conv_b_w, m_w_out, m_g_xattn, m_g_mem, m_w_q, m_w_k, m_w_v, m_w_o, m_g_ffn, m_w_gate, m_w_up, m_conv_f_w, m_w_down, m_g_final, v_g_mix, v_w_in, v_conv_a_w, v_conv_a_b, v_ln_a_g, v_ln_a_b, v_conv_b_w, v_w_out, v_g_xattn, v_g_mem, v_w_q, v_w_k, v_w_v, v_w_o, v_g_ffn, v_w_gate, v_w_up, v_conv_f_w, v_w_down, v_g_final):
    T, D = x.shape[1], x.shape[2]
    in_sh = w_in.shape[2]
    cw_sh = conv_a_w.shape[2]
    cw = N_CHIPS * cw_sh
    f_sh = w_gate.shape[2]
    fp = _round_up(f_sh, 256)
    F = N_CHIPS * fp
    rs = w_out.shape[1]
    c_idx = lax.axis_index("c")
    p_idx = 2 * lax.axis_index("x") + lax.axis_index("y")

    big = {
        "w_in": (w_in[0].astype(BF16), _Sharded((D, N_CHIPS * in_sh), 1)),
        "w_out": (w_out[0].astype(BF16), _Sharded((N_CHIPS * rs, D), 0)),
        "w_q": (w_q[0].astype(BF16), _Sharded((D, D), 0)),
        "w_k": (w_k[0].astype(BF16), _Sharded((D, D), 0)),
        "w_v": (w_v[0].astype(BF16), _Sharded((D, D), 0)),
        "w_o": (w_o[0].astype(BF16), _Sharded((D, D), 0)),
        "w_gate": (_pad_to(w_gate[0].astype(BF16), D, fp), _Sharded((D, F), 1)),
        "w_up": (_pad_to(w_up[0].astype(BF16), D, fp), _Sharded((D, F), 1)),
        "w_down": (_pad_to(w_down[0].astype(BF16), fp, D), _Sharded((F, D), 0)),
    }
    names = list(big)
    layouts = [big[k][1] for k in names]
    gathered = _allgather_weights([big[k][0] for k in names], layouts, name="allgather_weights")
    W = dict(zip(names, gathered))

    conv_shapes = [(KA_ROWS, cw_sh), (KS_ROWS, cw_sh), (KS_ROWS, fp)]
    conv_pack = _pack([_pad_to(conv_a_w[0], KA_ROWS, cw_sh), _pad_to(conv_b_w[0], KS_ROWS, cw_sh),
                       _pad_to(conv_f_w[0], KS_ROWS, fp)])
    conv_all = _allgather_small(conv_pack, name="allgather_conv", reduce=False)
    per_chip = [_unpack(conv_all[2 * q], conv_shapes) for q in range(N_CHIPS)]
    wa = jnp.concatenate([pc[0] for pc in per_chip], axis=1)
    wb = jnp.concatenate([pc[1] for pc in per_chip], axis=1)
    wf = jnp.concatenate([pc[2] for pc in per_chip], axis=1)

    xs, tgt, mems = x[0], loss_target[0], mem[0]
    g_mem2, g_final2 = g_mem[None, :], g_final[None, :]

    memn, rm = _norm_fwd(mems, g_mem2, name="norm_mem")
    xn1, r1 = _norm_fwd(xs, g_mix, name="norm_mix")
    proj = _mm(xn1, W["w_in"], name="mm_in")
    u3, u1 = _mix_a_fwd(proj, wa, conv_a_b, ln_a_g, ln_a_b, cw=cw, name="mix_a_fwd")
    vb = _mix_b_fwd(proj, wb, cw=cw, sw=cw, name="mix_b_fwd")
    mix = jnp.concatenate([u3, vb], axis=1)
    h1 = _mm(mix, W["w_out"], add=xs, name="mm_out")
    xn2, r2 = _norm_fwd(h1, g_xattn, name="norm_xattn")
    q = _mm(xn2, W["w_q"], out_dtype=BF16, name="mm_q")
    k = _mm(memn, W["w_k"], out_dtype=BF16, name="mm_k")
    vm = _mm(memn, W["w_v"], out_dtype=BF16, name="mm_v")
    o = _attn_fwd(q, k, vm, name="attn_fwd")
    h2 = _mm(o, W["w_o"], add=h1, name="mm_o")
    xn3, r3 = _norm_fwd(h2, g_ffn, name="norm_ffn")
    gp = _mm(xn3, W["w_gate"], name="mm_gate")
    up = _mm(xn3, W["w_up"], name="mm_up")
    f = _ffn_act_fwd(gp, up, wf, name="ffn_act_fwd")
    h3 = _mm(f, W["w_down"], add=h2, name="mm_down")

    loss_rows, dh3, dh3b, dg_final = _loss_head(h3, g_final2, tgt, name="loss_head")
    G = {}
    df = _mm(dh3b, W["w_down"], tb=True, out_dtype=BF16, name="mm_d_f")
    G["w_down"] = _mm(f, dh3b, ta=True, name="mm_dw_down")
    dgp, dup, dwf = _ffn_act_bwd(gp, up, df, wf, name="ffn_act_bwd")
    G["w_gate"] = _mm(xn3, dgp, ta=True, name="mm_dw_gate")
    G["w_up"] = _mm(xn3, dup, ta=True, name="mm_dw_up")
    dxn3 = _mm(dgp, W["w_gate"], tb=True, name="mm_dxn3_gate")
    dxn3 = _mm(dup, W["w_up"], tb=True, add=dxn3, name="mm_dxn3_up")
    dh2, dh2b, dg_ffn = _norm_bwd(h2, r3, g_ffn, dxn3, dh3, name="norm_ffn_bwd")
    do = _mm(dh2b, W["w_o"], tb=True, out_dtype=BF16, name="mm_d_o")
    G["w_o"] = _mm(o, dh2b, ta=True, name="mm_dw_o")
    dq, dk, dvm = _attn_bwd(q, k, vm, do, name="attn_bwd")
    dkb, dvb = dk.astype(BF16), dvm.astype(BF16)
    G["w_q"] = _mm(xn2, dq, ta=True, name="mm_dw_q")
    G["w_k"] = _mm(memn, dkb, ta=True, name="mm_dw_k")
    G["w_v"] = _mm(memn, dvb, ta=True, name="mm_dw_v")
    dxn2 = _mm(dq, W["w_q"], tb=True, name="mm_dxn2")
    dmemn = _mm(dkb, W["w_k"], tb=True, name="mm_dmem_k")
    dmemn = _mm(dvb, W["w_v"], tb=True, add=dmemn, name="mm_dmem_v")
    dg_mem = _norm_bwd(mems, rm, g_mem2, dmemn, None, name="norm_mem_bwd", want_dh=False)
    dh1, dh1b, dg_xattn = _norm_bwd(h1, r2, g_xattn, dxn2, dh2, name="norm_xattn_bwd")
    dmix = _mm(dh1b, W["w_out"], tb=True, name="mm_d_mix")
    G["w_out"] = _mm(mix, dh1b, ta=True, name="mm_dw_out")
    dav, dag, dwa, dba, dlg, dlb = _mix_a_bwd(proj, u1, dmix, wa, ln_a_g, ln_a_b, cw=cw, name="mix_a_bwd")
    dbg, dcg, dbh, dwb = _mix_b_bwd(proj, dmix, wb, cw=cw, sw=cw, name="mix_b_bwd")
    dproj = jnp.concatenate([dav, dag, dbg, dcg, dbh], axis=1)
    G["w_in"] = _mm(xn1, dproj, ta=True, name="mm_dw_in")
    dxn1 = _mm(dproj, W["w_in"], tb=True, name="mm_dxn1")
    dx, _, dg_mix = _norm_bwd(xs, r1, g_mix, dxn1, dh1, name="norm_mix_bwd")

    full = [G[k] for k in names]
    from_sibling = _rs_sibling_halves(full, layouts, name="rs_sibling_halves")
    c_arr = c_idx.astype(jnp.int32).reshape(1)
    chip_sums = [
        _half_sum(full[i], from_sibling[i], c_arr, rows_split=(layouts[i].axis == 1), name=f"half_sum_{names[i]}")
        for i in range(len(names))
    ]
    from_chips = _rs_chips(chip_sums, layouts, name="rs_chips")
    halves = [_sum_chips(from_chips[i], name=f"sum_chips_{names[i]}") for i in range(len(names))]
    shard_grads = dict(zip(names, _rs_join_halves(halves, layouts, name="rs_join_halves")))
    shard_grads["w_gate"] = shard_grads["w_gate"][:, :f_sh]
    shard_grads["w_up"] = shard_grads["w_up"][:, :f_sh]
    shard_grads["w_down"] = shard_grads["w_down"][:f_sh, :]

    loss_part = jnp.sum(loss_rows).reshape(1, 1)
    small_parts = [dg_mix, dba, dlg, dlb, dg_xattn, dg_mem, dg_ffn, dg_final, dwa, dwb, dwf, loss_part]
    small_shapes = [a.shape for a in small_parts]
    reduced = _allgather_small(_pack(small_parts), name="allreduce_small", reduce=True)
    (sg_mix, sba, slg, slb, sg_xattn, sg_mem, sg_ffn, sg_final, swa, swb, swf, loss_sum) = _unpack(reduced, small_shapes)
    loss = loss_sum.reshape(())
    ga_w = lax.dynamic_slice(swa, (0, p_idx * cw_sh), (KA, cw_sh))
    gb_w = lax.dynamic_slice(swb, (0, p_idx * cw_sh), (KS, cw_sh))
    gf_w = lax.dynamic_slice(swf, (0, p_idx * fp), (KS, f_sh))

    weights = dict(g_mix=g_mix, w_in=w_in, conv_a_w=conv_a_w, conv_a_b=conv_a_b, ln_a_g=ln_a_g, ln_a_b=ln_a_b,
                   conv_b_w=conv_b_w, w_out=w_out, g_xattn=g_xattn, g_mem=g_mem, w_q=w_q, w_k=w_k, w_v=w_v, w_o=w_o,
                   g_ffn=g_ffn, w_gate=w_gate, w_up=w_up, conv_f_w=conv_f_w, w_down=w_down, g_final=g_final)
    m_in = dict(g_mix=m_g_mix, w_in=m_w_in, conv_a_w=m_conv_a_w, conv_a_b=m_conv_a_b, ln_a_g=m_ln_a_g, ln_a_b=m_ln_a_b,
                conv_b_w=m_conv_b_w, w_out=m_w_out, g_xattn=m_g_xattn, g_mem=m_g_mem, w_q=m_w_q, w_k=m_w_k, w_v=m_w_v,
                w_o=m_w_o, g_ffn=m_g_ffn, w_gate=m_w_gate, w_up=m_w_up, conv_f_w=m_conv_f_w, w_down=m_w_down,
                g_final=m_g_final)
    v_in = dict(g_mix=v_g_mix, w_in=v_w_in, conv_a_w=v_conv_a_w, conv_a_b=v_conv_a_b, ln_a_g=v_ln_a_g, ln_a_b=v_ln_a_b,
                conv_b_w=v_conv_b_w, w_out=v_w_out, g_xattn=v_g_xattn, g_mem=v_g_mem, w_q=v_w_q, w_k=v_w_k, w_v=v_w_v,
                w_o=v_w_o, g_ffn=v_g_ffn, w_gate=v_w_gate, w_up=v_w_up, conv_f_w=v_conv_f_w, w_down=v_w_down,
                g_final=v_g_final)
    order = list(weights)
    grads = dict(shard_grads)
    grads.update(g_mix=sg_mix, conv_a_w=ga_w, conv_a_b=sba, ln_a_g=slg, ln_a_b=slb, conv_b_w=gb_w, g_xattn=sg_xattn,
                 g_mem=sg_mem, g_ffn=sg_ffn, conv_f_w=gf_w, g_final=sg_final)
    grads = {k: grads[k].reshape(weights[k].shape) for k in order}

    delta, new_m, new_v = {}, {}, {}
    for k in names:
        shp = weights[k].shape
        d_, m_, v_ = _adamw(weights[k][0], grads[k][0], m_in[k][0], v_in[k][0], name=f"adamw_{k}")
        delta[k], new_m[k], new_v[k] = d_.reshape(shp), m_.reshape(shp), v_.reshape(shp)
    small = [k for k in order if k not in big]
    small_shapes = [weights[k].shape for k in small]
    packed = [_pack([src[k] for k in small]) for src in (weights, grads, m_in, v_in)]
    d_, m_, v_ = _adamw(*packed, name="adamw_small")
    for k, dd, mm_, vv in zip(small, _unpack(d_, small_shapes), _unpack(m_, small_shapes), _unpack(v_, small_shapes)):
        delta[k], new_m[k], new_v[k] = dd, mm_, vv

    return (loss, dx[None], *[grads[k] for k in order], *[delta[k] for k in order],
            *[new_m[k] for k in order], *[new_v[k] for k in order])
```

```python
import functools

import jax
import jax.numpy as jnp
from jax import lax
from jax.experimental import pallas as pl
from jax.experimental.pallas import tpu as pltpu

F32 = jnp.float32
BF16 = jnp.bfloat16
EPS = 1e-6
N_HEADS = 4
GROUP = 128
KA = 31
KS = 3
KA_ROWS = 32
KS_ROWS = 8
HALO_A = 32
HALO_S = 8
N_CHIPS = 4
N_DEV = 8
LANES = 128
VMEM_LIMIT = 56 * 1024 * 1024
ADAMW_BLOCK_BYTES = 1 << 20
MESH = pl.DeviceIdType.MESH
ANY = pl.BlockSpec(memory_space=pl.ANY)

ADAM_LR = 0.001
ADAM_B1 = 0.9
ADAM_B2 = 0.999
ADAM_EPS = 1e-08
ADAM_WD = 0.01
ADAM_STEP = 10


def _pick(dim, pref, mult=LANES):
    if dim <= pref:
        return dim
    t = (pref // mult) * mult
    while t >= mult:
        if dim % t == 0:
            return t
        t -= mult
    return dim


def _round_up(n, m):
    return ((n + m - 1) // m) * m


def _params(*sem):
    return pltpu.CompilerParams(dimension_semantics=sem, vmem_limit_bytes=VMEM_LIMIT)


def _sigmoid(x):
    return jax.nn.sigmoid(x)


def _mm(a, b, *, name, ta=False, tb=False, out_dtype=F32, add=None, tm=1024, tn=512, tk=4096):
    if ta:
        K, M = a.shape
    else:
        M, K = a.shape
    if tb:
        N, K2 = b.shape
    else:
        K2, N = b.shape
    assert K == K2, (a.shape, b.shape)
    tm, tn, tk = _pick(M, tm), _pick(N, tn), _pick(K, tk)
    nk = K // tk
    a_spec = pl.BlockSpec((tk, tm), lambda i, j, k: (k, i)) if ta else pl.BlockSpec((tm, tk), lambda i, j, k: (i, k))
    b_spec = pl.BlockSpec((tn, tk), lambda i, j, k: (j, k)) if tb else pl.BlockSpec((tk, tn), lambda i, j, k: (k, j))
    o_spec = pl.BlockSpec((tm, tn), lambda i, j, k: (i, j))
    dims = (((0,) if ta else (1,), (1,) if tb else (0,)), ((), ()))
    has_add = add is not None

    def body(*refs):
        a_ref, b_ref = refs[:2]
        add_ref = refs[2] if has_add else None
        o_ref = refs[3 if has_add else 2]
        acc_ref = refs[-1] if nk > 1 else None
        k = pl.program_id(2)
        part = lax.dot_general(a_ref[...], b_ref[...], dims, preferred_element_type=F32)

        def finish(r):
            if add_ref is not None:
                r = add_ref[...] + r
            o_ref[...] = r.astype(out_dtype)

        if nk == 1:
            finish(part)
        else:
            @pl.when(k == 0)
            def _():
                acc_ref[...] = part

            @pl.when(jnp.logical_and(k > 0, k < nk - 1))
            def _():
                acc_ref[...] += part

            @pl.when(k == nk - 1)
            def _():
                finish(acc_ref[...] + part)

    in_specs = [a_spec, b_spec] + ([o_spec] if has_add else [])
    args = (a, b) + ((add,) if has_add else ())
    return pl.pallas_call(
        body,
        name=name,
        out_shape=jax.ShapeDtypeStruct((M, N), out_dtype),
        grid=(M // tm, N // tn, nk),
        in_specs=in_specs,
        out_specs=o_spec,
        scratch_shapes=[pltpu.VMEM((tm, tn), F32)] if nk > 1 else [],
        compiler_params=_params("parallel", "parallel", "arbitrary"),
    )(*args)


def _norm_fwd(h, g, *, name):
    T, D = h.shape
    R = _pick(T, 256, 8)

    def body(h_ref, g_ref, xn_ref, r_ref):
        x = h_ref[...]
        r = lax.rsqrt(jnp.mean(x * x, axis=-1, keepdims=True) + EPS)
        xn_ref[...] = ((x * r) * g_ref[...]).astype(BF16)
        r_ref[...] = r

    return pl.pallas_call(
        body,
        name=name,
        out_shape=(jax.ShapeDtypeStruct((T, D), BF16), jax.ShapeDtypeStruct((T, 1), F32)),
        grid=(T // R,),
        in_specs=[pl.BlockSpec((R, D), lambda i: (i, 0)), pl.BlockSpec((1, D), lambda i: (0, 0))],
        out_specs=(pl.BlockSpec((R, D), lambda i: (i, 0)), pl.BlockSpec((R, 1), lambda i: (i, 0))),
        compiler_params=_params("parallel"),
    )(h, g)


def _norm_bwd(h, r, g, dxn, dres, *, name, want_dh=True):
    T, D = h.shape
    R = _pick(T, 128, 8)
    has_res = dres is not None

    def body(*refs):
        h_ref, r_ref, g_ref, dxn_ref = refs[:4]
        pos = 4
        dres_ref = None
        if has_res:
            dres_ref = refs[pos]
            pos += 1
        if want_dh:
            dh_ref, dhb_ref, dg_ref = refs[pos:pos + 3]
        else:
            dg_ref = refs[pos]
        i = pl.program_id(0)
        rr = r_ref[...]
        hn = h_ref[...] * rr
        d = dxn_ref[...].astype(F32)
        gd = d * g_ref[...]
        part = jnp.sum(d * hn, axis=0, keepdims=True)

        @pl.when(i == 0)
        def _():
            dg_ref[...] = part

        @pl.when(i > 0)
        def _():
            dg_ref[...] += part

        if want_dh:
            dh = rr * (gd - hn * jnp.mean(gd * hn, axis=-1, keepdims=True))
            if dres_ref is not None:
                dh = dres_ref[...] + dh
            dh_ref[...] = dh
            dhb_ref[...] = dh.astype(BF16)

    row = pl.BlockSpec((R, D), lambda i: (i, 0))
    vec = pl.BlockSpec((1, D), lambda i: (0, 0))
    in_specs = [row, pl.BlockSpec((R, 1), lambda i: (i, 0)), vec, row] + ([row] if has_res else [])
    args = (h, r, g, dxn) + ((dres,) if has_res else ())
    if want_dh:
        out_shape = (jax.ShapeDtypeStruct((T, D), F32), jax.ShapeDtypeStruct((T, D), BF16), jax.ShapeDtypeStruct((1, D), F32))
        out_specs = (row, row, vec)
    else:
        out_shape = jax.ShapeDtypeStruct((1, D), F32)
        out_specs = vec
    return pl.pallas_call(
        body, name=name, out_shape=out_shape, grid=(T // R,), in_specs=in_specs, out_specs=out_specs,
        compiler_params=_params("arbitrary"),
    )(*args)


def _loss_head(h, g, tgt, *, name):
    T, D = h.shape
    R = _pick(T, 128, 8)

    def body(h_ref, g_ref, t_ref, loss_ref, dh_ref, dhb_ref, dg_ref):
        i = pl.program_id(0)
        x = h_ref[...]
        gg = g_ref[...]
        r = lax.rsqrt(jnp.mean(x * x, axis=-1, keepdims=True) + EPS)
        hn = x * r
        e = hn * gg - t_ref[...]
        loss_ref[...] = 0.5 * jnp.mean(e * e, axis=-1, keepdims=True)
        dy = e * (1.0 / D)
        gd = dy * gg
        dh = r * (gd - hn * jnp.mean(gd * hn, axis=-1, keepdims=True))
        dh_ref[...] = dh
        dhb_ref[...] = dh.astype(BF16)
        part = jnp.sum(dy * hn, axis=0, keepdims=True)

        @pl.when(i == 0)
        def _():
            dg_ref[...] = part

        @pl.when(i > 0)
        def _():
            dg_ref[...] += part

    row = pl.BlockSpec((R, D), lambda i: (i, 0))
    vec = pl.BlockSpec((1, D), lambda i: (0, 0))
    return pl.pallas_call(
        body,
        name=name,
        out_shape=(jax.ShapeDtypeStruct((T, 1), F32), jax.ShapeDtypeStruct((T, D), F32),
                   jax.ShapeDtypeStruct((T, D), BF16), jax.ShapeDtypeStruct((1, D), F32)),
        grid=(T // R,),
        in_specs=[row, vec, row],
        out_specs=(pl.BlockSpec((R, 1), lambda i: (i, 0)), row, row, vec),
        compiler_params=_params("arbitrary"),
    )(h, g, tgt)


_NT = (((1,), (1,)), ((), ()))
_TN = (((0,), (0,)), ((), ()))
_NN = (((1,), (0,)), ((), ()))


def _softmax_rows(s):
    m = jnp.max(s, axis=-1, keepdims=True)
    e = jnp.exp(s - m)
    return e / jnp.sum(e, axis=-1, keepdims=True)


def _attn_fwd(q, k, v, *, name):
    T, D = q.shape
    ML = k.shape[0]
    dh = D // N_HEADS
    scale = dh ** -0.5
    R = _pick(T, 512, 16)

    def body(q_ref, k_ref, v_ref, o_ref):
        s = lax.dot_general(q_ref[...], k_ref[...], _NT, preferred_element_type=F32) * scale
        p = _softmax_rows(s)
        o_ref[...] = lax.dot_general(p.astype(BF16), v_ref[...], _NN, preferred_element_type=F32).astype(BF16)

    qs = pl.BlockSpec((R, dh), lambda i, h: (i, h))
    ks = pl.BlockSpec((ML, dh), lambda i, h: (0, h))
    return pl.pallas_call(
        body, name=name, out_shape=jax.ShapeDtypeStruct((T, D), BF16), grid=(T // R, N_HEADS),
        in_specs=[qs, ks, ks], out_specs=qs, compiler_params=_params("parallel", "parallel"),
    )(q, k, v)


def _attn_bwd(q, k, v, do, *, name):
    T, D = q.shape
    ML = k.shape[0]
    dh = D // N_HEADS
    scale = dh ** -0.5
    R = _pick(T, 512, 16)

    def body(q_ref, k_ref, v_ref, do_ref, dq_ref, dk_ref, dv_ref):
        i = pl.program_id(1)
        qq, kk, vv, dd = q_ref[...], k_ref[...], v_ref[...], do_ref[...]
        s = lax.dot_general(qq, kk, _NT, preferred_element_type=F32) * scale
        p = _softmax_rows(s)
        dp = lax.dot_general(dd, vv, _NT, preferred_element_type=F32)
        dv_part = lax.dot_general(p.astype(BF16), dd, _TN, preferred_element_type=F32)
        ds = (p * (dp - jnp.sum(p * dp, axis=-1, keepdims=True)) * scale).astype(BF16)
        dq_ref[...] = lax.dot_general(ds, kk, _NN, preferred_element_type=F32).astype(BF16)
        dk_part = lax.dot_general(ds, qq, _TN, preferred_element_type=F32)

        @pl.when(i == 0)
        def _():
            dk_ref[...] = dk_part
            dv_ref[...] = dv_part

        @pl.when(i > 0)
        def _():
            dk_ref[...] += dk_part
            dv_ref[...] += dv_part

    qs = pl.BlockSpec((R, dh), lambda h, i: (i, h))
    ks = pl.BlockSpec((ML, dh), lambda h, i: (0, h))
    return pl.pallas_call(
        body,
        name=name,
        out_shape=(jax.ShapeDtypeStruct((T, D), BF16), jax.ShapeDtypeStruct((ML, D), F32), jax.ShapeDtypeStruct((ML, D), F32)),
        grid=(N_HEADS, T // R),
        in_specs=[qs, ks, ks, qs],
        out_specs=(qs, ks, ks),
        compiler_params=_params("parallel", "arbitrary"),
    )(q, k, v, do)


def _prev_halo(R, halo, CT, col):
    per = R // halo
    return lambda i, j: (jnp.maximum(i * per - 1, 0), col(j))


def _next_halo(R, halo, T, col):
    per = R // halo
    last = T // halo - 1
    return lambda i, j: (jnp.minimum((i + 1) * per, last), col(j))


def _swap(f):
    return lambda j, i: f(i, j)


def _ffn_act_fwd(gp, up, wf, *, name):
    T, F = gp.shape
    R = _pick(T, 256, 8)
    CT = _pick(F, 1024)
    H = HALO_S

    def body(g_ref, gh_ref, u_ref, w_ref, f_ref, ext):
        i = pl.program_id(0)
        ext[pl.ds(0, H), :] = jnp.where(i > 0, gh_ref[...], 0.0)
        ext[pl.ds(H, R), :] = g_ref[...]
        w = w_ref[...]
        g = ext[pl.ds(H - 2, R), :] * w[0:1, :]
        g = g + ext[pl.ds(H - 1, R), :] * w[1:2, :]
        g = g + ext[pl.ds(H, R), :] * w[2:3, :]
        f_ref[...] = (g * _sigmoid(g) * u_ref[...]).astype(BF16)

    main = pl.BlockSpec((R, CT), lambda i, j: (i, j))
    return pl.pallas_call(
        body,
        name=name,
        out_shape=jax.ShapeDtypeStruct((T, F), BF16),
        grid=(T // R, F // CT),
        in_specs=[main, pl.BlockSpec((H, CT), _prev_halo(R, H, CT, lambda j: j)), main,
                  pl.BlockSpec((KS_ROWS, CT), lambda i, j: (0, j))],
        out_specs=main,
        scratch_shapes=[pltpu.VMEM((R + H, CT), F32)],
        compiler_params=_params("parallel", "parallel"),
    )(gp, gp, up, wf)


def _ffn_act_bwd(gp, up, df, wf, *, name):
    T, F = gp.shape
    R = _pick(T, 256, 16)
    CT = _pick(F, 1024)
    H = HALO_S
    HB = 16
    n_t = T // R

    def body(g_ref, gp_ref, gn_ref, u_ref, un_ref, d_ref, dn_ref, w_ref, dg_out, du_out, dw_ref, ext, dge):
        i = pl.program_id(1)
        last = i == n_t - 1
        ext[pl.ds(0, H), :] = jnp.where(i > 0, gp_ref[...], 0.0)
        ext[pl.ds(H, R), :] = g_ref[...]
        ext[pl.ds(H + R, H), :] = gn_ref[...]
        w = w_ref[...]
        w0, w1, w2 = w[0:1, :], w[1:2, :], w[2:3, :]

        def conv(start, n):
            g = ext[pl.ds(start + H - 2, n), :] * w0
            g = g + ext[pl.ds(start + H - 1, n), :] * w1
            return g + ext[pl.ds(start + H, n), :] * w2

        def dact(g, u, d):
            sg = _sigmoid(g)
            return d * u * (sg * (1.0 + g * (1.0 - sg))), d * (g * sg)

        d_main = d_ref[...].astype(F32)
        dg_main, du_main = dact(conv(0, R), u_ref[...], d_main)
        du_out[...] = du_main.astype(BF16)
        d_next = jnp.where(last, 0.0, dn_ref[pl.ds(0, H), :].astype(F32))
        dg_next, _ = dact(conv(R, H), un_ref[...], d_next)
        dg_next = jnp.where(last, 0.0, dg_next)
        dge[pl.ds(0, R), :] = dg_main
        dge[pl.ds(R, H), :] = dg_next
        dgp = dge[pl.ds(0, R), :] * w2 + dge[pl.ds(1, R), :] * w1 + dge[pl.ds(2, R), :] * w0
        dg_out[...] = dgp.astype(BF16)
        rows = [jnp.sum(dg_main * ext[pl.ds(H - 2 + k, R), :], axis=0, keepdims=True) for k in range(KS)]
        rows.append(jnp.zeros((KS_ROWS - KS, CT), F32))
        _acc_rows_block(dw_ref, i, rows)

    col = lambda j: j
    main = pl.BlockSpec((R, CT), lambda j, i: (i, j))
    prev8 = pl.BlockSpec((H, CT), _swap(_prev_halo(R, H, CT, col)))
    next8 = pl.BlockSpec((H, CT), _swap(_next_halo(R, H, T, col)))
    next16 = pl.BlockSpec((HB, CT), _swap(_next_halo(R, HB, T, col)))
    wspec = pl.BlockSpec((KS_ROWS, CT), lambda j, i: (0, j))
    return pl.pallas_call(
        body,
        name=name,
        out_shape=(jax.ShapeDtypeStruct((T, F), BF16), jax.ShapeDtypeStruct((T, F), BF16), jax.ShapeDtypeStruct((KS_ROWS, F), F32)),
        grid=(F // CT, n_t),
        in_specs=[main, prev8, next8, main, next8, main, next16, wspec],
        out_specs=(main, main, wspec),
        scratch_shapes=[pltpu.VMEM((R + 2 * H, CT), F32), pltpu.VMEM((R + H, CT), F32)],
        compiler_params=_params("parallel", "arbitrary"),
    )(gp, gp, gp, up, up, df, df, wf)


def _acc_rows_block(ref, i, rows):
    *singles, pad = rows

    @pl.when(i == 0)
    def _():
        for k, row in enumerate(singles):
            ref[pl.ds(k, 1), :] = row
        ref[pl.ds(len(singles), pad.shape[0]), :] = pad

    @pl.when(i > 0)
    def _():
        for k, row in enumerate(singles):
            ref[pl.ds(k, 1), :] += row


def _group_stats(x):
    mu = jnp.mean(x, axis=-1, keepdims=True)
    xc = x - mu
    var = jnp.mean(xc * xc, axis=-1, keepdims=True)
    return xc, lax.rsqrt(var + EPS)


def _mix_a_fwd(proj, wa, ba, lg, lb, *, cw, name):
    T = proj.shape[0]
    R = _pick(T, 128, HALO_A)
    CT = _pick(cw, 256)
    nc = cw // CT
    H = HALO_A

    def body(av_ref, ag_ref, avh_ref, agh_ref, w_ref, b_ref, lg_ref, lb_ref, u3_ref, u1_ref, ext):
        i = pl.program_id(0)
        ext[pl.ds(0, H), :] = jnp.where(i > 0, avh_ref[...] * _sigmoid(agh_ref[...]), 0.0)
        ext[pl.ds(H, R), :] = av_ref[...] * _sigmoid(ag_ref[...])
        w = w_ref[...]
        acc = ext[pl.ds(H - (KA - 1), R), :] * w[0:1, :]
        for k in range(1, KA):
            acc = acc + ext[pl.ds(H - (KA - 1) + k, R), :] * w[k:k + 1, :]
        u1 = acc + b_ref[...]
        u1_ref[...] = u1
        for gi in range(CT // GROUP):
            sl = slice(gi * GROUP, (gi + 1) * GROUP)
            xc, rstd = _group_stats(u1[:, sl])
            u2 = (xc * rstd) * lg_ref[:, sl] + lb_ref[:, sl]
            u3_ref[:, sl] = (u2 * _sigmoid(u2)).astype(BF16)

    main_v = pl.BlockSpec((R, CT), lambda i, j: (i, j))
    main_g = pl.BlockSpec((R, CT), lambda i, j: (i, j + nc))
    halo_v = pl.BlockSpec((H, CT), _prev_halo(R, H, CT, lambda j: j))
    halo_g = pl.BlockSpec((H, CT), _prev_halo(R, H, CT, lambda j: j + nc))
    vec = pl.BlockSpec((1, CT), lambda i, j: (0, j))
    out = pl.BlockSpec((R, CT), lambda i, j: (i, j))
    return pl.pallas_call(
        body,
        name=name,
        out_shape=(jax.ShapeDtypeStruct((T, cw), BF16), jax.ShapeDtypeStruct((T, cw), F32)),
        grid=(T // R, nc),
        in_specs=[main_v, main_g, halo_v, halo_g, pl.BlockSpec((KA_ROWS, CT), lambda i, j: (0, j)), vec, vec, vec],
        out_specs=(out, out),
        scratch_shapes=[pltpu.VMEM((R + H, CT), F32)],
        compiler_params=_params("parallel", "parallel"),
    )(proj, proj, proj, proj, wa, ba, lg, lb)


def _mix_a_bwd(proj, u1, dmix, wa, lg, lb, *, cw, name):
    T = proj.shape[0]
    R = _pick(T, 128, HALO_A)
    CT = _pick(cw, 256)
    nc = cw // CT
    H = HALO_A
    n_t = T // R
    NG = CT // GROUP

    def body(av_ref, ag_ref, avh_ref, agh_ref, u1_ref, u1n_ref, d3_ref, d3n_ref, w_ref, lg_ref, lb_ref,
             dav_ref, dag_ref, dw_ref, db_ref, dlg_ref, dlb_ref, ext, d1):
        i = pl.program_id(1)
        last = i == n_t - 1

        def ln_bwd(u1, d3, sl):
            xc, rstd = _group_stats(u1)
            xh = xc * rstd
            g = lg_ref[:, sl]
            u2 = xh * g + lb_ref[:, sl]
            sg = _sigmoid(u2)
            du2 = d3 * (sg * (1.0 + u2 * (1.0 - sg)))
            dxh = du2 * g
            du1 = rstd * (dxh - jnp.mean(dxh, axis=-1, keepdims=True) - xh * jnp.mean(dxh * xh, axis=-1, keepdims=True))
            return du1, du2 * xh, du2

        dlg_parts, dlb_parts = [], []
        for gi in range(NG):
            sl = slice(gi * GROUP, (gi + 1) * GROUP)
            du1, dlg, dlb = ln_bwd(u1_ref[:, sl], d3_ref[:, sl], sl)
            d1[pl.ds(0, R), sl] = du1
            dlg_parts.append(jnp.sum(dlg, axis=0, keepdims=True))
            dlb_parts.append(jnp.sum(dlb, axis=0, keepdims=True))
            du1n, _, _ = ln_bwd(u1n_ref[:, sl], jnp.where(last, 0.0, d3n_ref[:, sl]), sl)
            d1[pl.ds(R, H), sl] = jnp.where(last, 0.0, du1n)
        dlg_row = jnp.concatenate(dlg_parts, axis=1) if NG > 1 else dlg_parts[0]
        dlb_row = jnp.concatenate(dlb_parts, axis=1) if NG > 1 else dlb_parts[0]
        d1_main = d1[pl.ds(0, R), :]
        db_row = jnp.sum(d1_main, axis=0, keepdims=True)

        @pl.when(i == 0)
        def _():
            dlg_ref[...] = dlg_row
            dlb_ref[...] = dlb_row
            db_ref[...] = db_row

        @pl.when(i > 0)
        def _():
            dlg_ref[...] += dlg_row
            dlb_ref[...] += dlb_row
            db_ref[...] += db_row

        w = w_ref[...]
        du0 = d1[pl.ds(KA - 1, R), :] * w[0:1, :]
        for k in range(1, KA):
            du0 = du0 + d1[pl.ds(KA - 1 - k, R), :] * w[k:k + 1, :]
        av = av_ref[...]
        sg = _sigmoid(ag_ref[...])
        dav_ref[...] = (du0 * sg).astype(BF16)
        dag_ref[...] = (du0 * av * sg * (1.0 - sg)).astype(BF16)

        ext[pl.ds(0, H), :] = jnp.where(i > 0, avh_ref[...] * _sigmoid(agh_ref[...]), 0.0)
        ext[pl.ds(H, R), :] = av * sg
        rows = [jnp.sum(d1_main * ext[pl.ds(H - (KA - 1) + k, R), :], axis=0, keepdims=True) for k in range(KA)]
        rows.append(jnp.zeros((KA_ROWS - KA, CT), F32))
        _acc_rows_block(dw_ref, i, rows)

    cv = lambda j: j
    cg = lambda j: j + nc
    main_v = pl.BlockSpec((R, CT), lambda j, i: (i, j))
    main_g = pl.BlockSpec((R, CT), lambda j, i: (i, j + nc))
    prev_v = pl.BlockSpec((H, CT), _swap(_prev_halo(R, H, CT, cv)))
    prev_g = pl.BlockSpec((H, CT), _swap(_prev_halo(R, H, CT, cg)))
    nxt = pl.BlockSpec((H, CT), _swap(_next_halo(R, H, T, cv)))
    wspec = pl.BlockSpec((KA_ROWS, CT), lambda j, i: (0, j))
    vec = pl.BlockSpec((1, CT), lambda j, i: (0, j))
    vshape = jax.ShapeDtypeStruct((1, cw), F32)
    return pl.pallas_call(
        body,
        name=name,
        out_shape=(jax.ShapeDtypeStruct((T, cw), BF16), jax.ShapeDtypeStruct((T, cw), BF16),
                   jax.ShapeDtypeStruct((KA_ROWS, cw), F32), vshape, vshape, vshape),
        grid=(nc, n_t),
        in_specs=[main_v, main_g, prev_v, prev_g, main_v, nxt, main_v, nxt, wspec, vec, vec],
        out_specs=(main_v, main_v, wspec, vec, vec, vec),
        scratch_shapes=[pltpu.VMEM((R + H, CT), F32), pltpu.VMEM((R + H, CT), F32)],
        compiler_params=_params("parallel", "arbitrary"),
    )(proj, proj, proj, proj, u1, u1, dmix, dmix, wa, lg, lb)


def _mix_b_fwd(proj, wb, *, cw, sw, name):
    T = proj.shape[0]
    R = _pick(T, 256, 8)
    CT = _pick(sw, 512)
    nb, nc, nh = (2 * cw) // CT, (2 * cw + sw) // CT, (2 * cw + 2 * sw) // CT
    H = HALO_S

    def body(b_ref, c_ref, h_ref, ch_ref, hh_ref, w_ref, v_ref, ext):
        i = pl.program_id(0)
        ext[pl.ds(0, H), :] = jnp.where(i > 0, ch_ref[...] * hh_ref[...], 0.0)
        ext[pl.ds(H, R), :] = c_ref[...] * h_ref[...]
        w = w_ref[...]
        zc = ext[pl.ds(H - 2, R), :] * w[0:1, :]
        zc = zc + ext[pl.ds(H - 1, R), :] * w[1:2, :]
        zc = zc + ext[pl.ds(H, R), :] * w[2:3, :]
        v_ref[...] = (b_ref[...] * zc).astype(BF16)

    def main(off):
        return pl.BlockSpec((R, CT), lambda i, j: (i, j + off))

    def prev(off):
        return pl.BlockSpec((H, CT), _prev_halo(R, H, CT, lambda j: j + off))

    return pl.pallas_call(
        body,
        name=name,
        out_shape=jax.ShapeDtypeStruct((T, sw), BF16),
        grid=(T // R, sw // CT),
        in_specs=[main(nb), main(nc), main(nh), prev(nc), prev(nh), pl.BlockSpec((KS_ROWS, CT), lambda i, j: (0, j))],
        out_specs=pl.BlockSpec((R, CT), lambda i, j: (i, j)),
        scratch_shapes=[pltpu.VMEM((R + H, CT), F32)],
        compiler_params=_params("parallel", "parallel"),
    )(proj, proj, proj, proj, proj, wb)


def _mix_b_bwd(proj, dmix, wb, *, cw, sw, name):
    T = proj.shape[0]
    R = _pick(T, 256, 8)
    CT = _pick(sw, 512)
    nb, nc, nh = (2 * cw) // CT, (2 * cw + sw) // CT, (2 * cw + 2 * sw) // CT
    nd = cw // CT
    H = HALO_S
    n_t = T // R

    def body(b_ref, bn_ref, c_ref, cp_ref, cn_ref, h_ref, hp_ref, hn_ref, d_ref, dn_ref, w_ref,
             db_ref, dc_ref, dhh_ref, dw_ref, ext, dze):
        i = pl.program_id(1)
        last = i == n_t - 1
        cc, hh = c_ref[...], h_ref[...]
        ext[pl.ds(0, H), :] = jnp.where(i > 0, cp_ref[...] * hp_ref[...], 0.0)
        ext[pl.ds(H, R), :] = cc * hh
        ext[pl.ds(H + R, H), :] = cn_ref[...] * hn_ref[...]
        w = w_ref[...]
        w0, w1, w2 = w[0:1, :], w[1:2, :], w[2:3, :]

        def conv(start, n):
            z = ext[pl.ds(start + H - 2, n), :] * w0
            z = z + ext[pl.ds(start + H - 1, n), :] * w1
            return z + ext[pl.ds(start + H, n), :] * w2

        d_main = d_ref[...]
        db_ref[...] = (d_main * conv(0, R)).astype(BF16)
        dzc_main = d_main * b_ref[...]
        dze[pl.ds(0, R), :] = dzc_main
        dze[pl.ds(R, H), :] = jnp.where(last, 0.0, dn_ref[...] * bn_ref[...])
        dz = dze[pl.ds(0, R), :] * w2 + dze[pl.ds(1, R), :] * w1 + dze[pl.ds(2, R), :] * w0
        dc_ref[...] = (dz * hh).astype(BF16)
        dhh_ref[...] = (dz * cc).astype(BF16)
        rows = [jnp.sum(dzc_main * ext[pl.ds(H - 2 + k, R), :], axis=0, keepdims=True) for k in range(KS)]
        rows.append(jnp.zeros((KS_ROWS - KS, CT), F32))
        _acc_rows_block(dw_ref, i, rows)

    def main(off):
        return pl.BlockSpec((R, CT), lambda j, i: (i, j + off))

    def prev(off):
        return pl.BlockSpec((H, CT), _swap(_prev_halo(R, H, CT, lambda j: j + off)))

    def nxt(off):
        return pl.BlockSpec((H, CT), _swap(_next_halo(R, H, T, lambda j: j + off)))

    out = pl.BlockSpec((R, CT), lambda j, i: (i, j))
    wspec = pl.BlockSpec((KS_ROWS, CT), lambda j, i: (0, j))
    act = jax.ShapeDtypeStruct((T, sw), BF16)
    return pl.pallas_call(
        body,
        name=name,
        out_shape=(act, act, act, jax.ShapeDtypeStruct((KS_ROWS, sw), F32)),
        grid=(sw // CT, n_t),
        in_specs=[main(nb), nxt(nb), main(nc), prev(nc), nxt(nc), main(nh), prev(nh), nxt(nh), main(nd), nxt(nd), wspec],
        out_specs=(out, out, out, wspec),
        scratch_shapes=[pltpu.VMEM((R + 2 * H, CT), F32), pltpu.VMEM((R + H, CT), F32)],
        compiler_params=_params("parallel", "arbitrary"),
    )(proj, proj, proj, proj, proj, proj, proj, proj, dmix, dmix, wb)


def _adamw(w, g, m, v, *, name):
    Rr, Cc = w.shape
    R = _pick(Rr, max(8, ADAMW_BLOCK_BYTES // (4 * Cc)), 8)

    def body(w_ref, g_ref, m_ref, v_ref, d_ref, mo_ref, vo_ref):
        g = g_ref[...]
        m2 = ADAM_B1 * m_ref[...] + (1.0 - ADAM_B1) * g
        v2 = ADAM_B2 * v_ref[...] + (1.0 - ADAM_B2) * (g * g)
        m_hat = m2 / (1.0 - ADAM_B1 ** ADAM_STEP)
        v_hat = v2 / (1.0 - ADAM_B2 ** ADAM_STEP)
        d_ref[...] = -ADAM_LR * (m_hat / (jnp.sqrt(v_hat) + ADAM_EPS) + ADAM_WD * w_ref[...])
        mo_ref[...] = m2
        vo_ref[...] = v2

    spec = pl.BlockSpec((R, Cc), lambda i: (i, 0))
    shp = jax.ShapeDtypeStruct((Rr, Cc), F32)
    return pl.pallas_call(
        body, name=name, out_shape=(shp, shp, shp), grid=(Rr // R,), in_specs=[spec] * 4, out_specs=(spec, spec, spec),
        compiler_params=_params("parallel"),
    )(w, g, m, v)


def _half_sum(g, ra, c, *, rows_split, name):
    Rr, Cc = ra.shape
    tr, tc = _pick(Rr, 512, 16), _pick(Cc, 2048)
    nr, ncol = Rr // tr, Cc // tc
    if rows_split:
        g_map = lambda i, j, c_ref: (i + c_ref[0] * nr, j)
    else:
        g_map = lambda i, j, c_ref: (i, j + c_ref[0] * ncol)

    def body(c_ref, g_ref, ra_ref, o_ref):
        o_ref[...] = (g_ref[...] + ra_ref[...]).astype(BF16)

    spec = pl.BlockSpec((tr, tc), lambda i, j, c_ref: (i, j))
    return pl.pallas_call(
        body,
        name=name,
        out_shape=jax.ShapeDtypeStruct((Rr, Cc), BF16),
        grid_spec=pltpu.PrefetchScalarGridSpec(
            num_scalar_prefetch=1, grid=(nr, ncol), in_specs=[pl.BlockSpec((tr, tc), g_map), spec], out_specs=spec),
        compiler_params=_params("parallel", "parallel"),
    )(c, g, ra)


def _sum_chips(cs, rb, pc, lay, *, name):
    _, Rr, Cc = rb.shape
    tr, tc = _pick(Rr, 512, 16), _pick(Cc, 2048)
    nr, ncol = Rr // tr, Cc // tc
    if lay.axis == 0:
        own_map = lambda i, j, s: (i + s[0] * nr, j)
        out_map = lambda i, j, s: (i, j + s[1] * ncol)
    else:
        own_map = lambda i, j, s: (i, j + s[0] * ncol)
        out_map = lambda i, j, s: (i + s[1] * nr, j)

    def body(s_ref, own_ref, rb_ref, o_ref):
        acc = own_ref[...].astype(F32)
        for j in range(N_CHIPS - 1):
            acc = acc + rb_ref[j].astype(F32)
        o_ref[...] = acc

    return pl.pallas_call(
        body,
        name=name,
        out_shape=jax.ShapeDtypeStruct(lay.shard_shape(), F32),
        grid_spec=pltpu.PrefetchScalarGridSpec(
            num_scalar_prefetch=1,
            grid=(nr, ncol),
            in_specs=[pl.BlockSpec((tr, tc), own_map), pl.BlockSpec((N_CHIPS - 1, tr, tc), lambda i, j, s: (0, i, j))],
            out_specs=pl.BlockSpec((tr, tc), out_map)),
        compiler_params=_params("parallel", "parallel"),
    )(pc, cs, rb)


def _place():
    x, y, c = lax.axis_index("x"), lax.axis_index("y"), lax.axis_index("c")
    return x, y, c, 2 * x + y


def _other_chips(x, y):
    return [(1 - x, y, 2 * (1 - x) + y), (x, 1 - y, 2 * x + (1 - y)), (1 - x, 1 - y, 2 * (1 - x) + (1 - y))]


def _allgather_small(buf, *, name, reduce):
    S = buf.shape[0]

    def body(x_ref, o_ref, gat, send_sems, recv_sems):
        x, y, c, _ = _place()
        me = 4 * x + 2 * y + c
        gat[me] = x_ref[...]
        copies = []
        for k in range(1, N_DEV):
            fx, fy, fc = (k >> 2) & 1, (k >> 1) & 1, k & 1
            px = 1 - x if fx else x
            py = 1 - y if fy else y
            pc = 1 - c if fc else c
            peer = 4 * px + 2 * py + pc
            send = pltpu.make_async_remote_copy(
                src_ref=x_ref, dst_ref=gat.at[me], send_sem=send_sems.at[k - 1], recv_sem=recv_sems.at[k - 1],
                device_id=(px, py, pc), device_id_type=MESH)
            send.start()
            arrival = pltpu.make_async_remote_copy(
                src_ref=x_ref, dst_ref=gat.at[peer], send_sem=send_sems.at[k - 1], recv_sem=recv_sems.at[k - 1],
                device_id=(px, py, pc), device_id_type=MESH)
            copies.append((send, arrival))
        for send, arrival in copies:
            arrival.wait_recv()
        for send, arrival in copies:
            send.wait_send()
        if reduce:
            acc = gat[0]
            for d in range(1, N_DEV):
                acc = acc + gat[d]
            o_ref[...] = acc
        else:
            o_ref[...] = gat[...]

    out_shape = jax.ShapeDtypeStruct((S, LANES) if reduce else (N_DEV, S, LANES), F32)
    return pl.pallas_call(
        body,
        name=name,
        out_shape=out_shape,
        in_specs=[pl.BlockSpec(memory_space=pltpu.VMEM)],
        out_specs=pl.BlockSpec(memory_space=pltpu.VMEM),
        scratch_shapes=[pltpu.VMEM((N_DEV, S, LANES), F32), pltpu.SemaphoreType.DMA((N_DEV - 1,)),
                        pltpu.SemaphoreType.DMA((N_DEV - 1,))],
        compiler_params=pltpu.CompilerParams(vmem_limit_bytes=VMEM_LIMIT),
    )(buf)


class _Sharded:
    def __init__(self, shape, axis):
        self.shape = shape
        self.axis = axis
        self.block = shape[axis] // N_CHIPS
        self.half = shape[1 - axis] // 2

    def _sl(self, along, across):
        return (along, across) if self.axis == 0 else (across, along)

    def block_slice(self, q):
        return self._sl(pl.ds(q * self.block, self.block), pl.ds(0, self.shape[1 - self.axis]))

    def block_half_slice(self, q, c):
        return self._sl(pl.ds(q * self.block, self.block), pl.ds(c * self.half, self.half))

    def half_slice(self, c):
        return self._sl(pl.ds(0, self.shape[self.axis]), pl.ds(c * self.half, self.half))

    def shard_half_slice(self, c):
        return self._sl(pl.ds(0, self.block), pl.ds(c * self.half, self.half))

    def shard_shape(self):
        return self._sl(self.block, self.shape[1 - self.axis])

    def half_shape(self):
        return self._sl(self.shape[self.axis], self.half)

    def block_half_shape(self):
        return self._sl(self.block, self.half)

    def block_in_half_slice(self, q):
        return self._sl(pl.ds(q * self.block, self.block), pl.ds(0, self.half))


def _at(ref, sl):
    return ref.at[sl[0], sl[1]]


def _allgather_weights(shards, layouts, *, name):
    n = len(shards)

    def body(*refs):
        srcs, outs = refs[:n], refs[n:2 * n]
        own_send_sems, own_recv_sems, send_sems, recv_sems, fsend_sems, frecv_sems = refs[2 * n:]
        x, y, c, p = _place()
        sibling = (x, y, 1 - c)
        chips = _other_chips(x, y)
        owns, sends, passed = [], [], []
        for i in range(n):
            lay = layouts[i]
            cp = pltpu.make_async_remote_copy(
                src_ref=srcs[i], dst_ref=_at(outs[i], lay.block_slice(p)), send_sem=own_send_sems.at[i],
                recv_sem=own_recv_sems.at[i], device_id=sibling, device_id_type=MESH)
            cp.start()
            owns.append(cp)
            for j, (qx, qy, q) in enumerate(chips):
                dst = _at(outs[i], lay.block_half_slice(p, c))
                cp = pltpu.make_async_remote_copy(
                    src_ref=_at(srcs[i], lay.shard_half_slice(c)), dst_ref=dst, send_sem=send_sems.at[i, j],
                    recv_sem=recv_sems.at[i, j], device_id=(qx, qy, c), device_id_type=MESH)
                cp.start()
                sends.append(cp)
        for i in range(n):
            lay = layouts[i]
            for j, (qx, qy, q) in enumerate(chips):
                landed = _at(outs[i], lay.block_half_slice(q, c))
                pltpu.make_async_remote_copy(
                    src_ref=landed, dst_ref=landed, send_sem=send_sems.at[i, j], recv_sem=recv_sems.at[i, j],
                    device_id=(qx, qy, c), device_id_type=MESH).wait_recv()
                cp = pltpu.make_async_remote_copy(
                    src_ref=landed, dst_ref=landed, send_sem=fsend_sems.at[i, j], recv_sem=frecv_sems.at[i, j],
                    device_id=sibling, device_id_type=MESH)
                cp.start()
                passed.append(cp)
        for i in range(n):
            lay = layouts[i]
            for j, (qx, qy, q) in enumerate(chips):
                theirs = _at(outs[i], lay.block_half_slice(q, 1 - c))
                pltpu.make_async_remote_copy(
                    src_ref=theirs, dst_ref=theirs, send_sem=fsend_sems.at[i, j], recv_sem=frecv_sems.at[i, j],
                    device_id=sibling, device_id_type=MESH).wait_recv()
        for cp in owns:
            cp.wait_recv()
        for cp in sends + passed + owns:
            cp.wait_send()

    sem = pltpu.SemaphoreType.DMA
    return pl.pallas_call(
        body,
        name=name,
        out_shape=[jax.ShapeDtypeStruct(lay.shape, BF16) for lay in layouts],
        in_specs=[ANY] * n,
        out_specs=[ANY] * n,
        scratch_shapes=[sem((n,)), sem((n,)), sem((n, 3)), sem((n, 3)), sem((n, 3)), sem((n, 3))],
    )(*shards)


def _rs_sibling_halves(grads, layouts, *, name):
    n = len(grads)

    def body(*refs):
        srcs, outs = refs[:n], refs[n:2 * n]
        send_sems, recv_sems = refs[2 * n:]
        x, y, c, _ = _place()
        copies = []
        for i in range(n):
            cp = pltpu.make_async_remote_copy(
                src_ref=_at(srcs[i], layouts[i].half_slice(1 - c)), dst_ref=outs[i], send_sem=send_sems.at[i],
                recv_sem=recv_sems.at[i], device_id=(x, y, 1 - c), device_id_type=MESH)
            cp.start()
            copies.append(cp)
        for cp in copies:
            cp.wait_recv()
        for cp in copies:
            cp.wait_send()

    sem = pltpu.SemaphoreType.DMA
    return pl.pallas_call(
        body,
        name=name,
        out_shape=[jax.ShapeDtypeStruct(lay.half_shape(), F32) for lay in layouts],
        in_specs=[ANY] * n,
        out_specs=[ANY] * n,
        scratch_shapes=[sem((n,)), sem((n,))],
    )(*grads)


def _rs_chips(sums, layouts, *, name):
    n = len(sums)

    def body(*refs):
        srcs, outs = refs[:n], refs[n:2 * n]
        send_sems, recv_sems = refs[2 * n:]
        x, y, c, p = _place()
        chips = _other_chips(x, y)
        sends = []
        for i in range(n):
            lay = layouts[i]
            for j, (qx, qy, q) in enumerate(chips):
                cp = pltpu.make_async_remote_copy(
                    src_ref=_at(srcs[i], lay.block_in_half_slice(q)), dst_ref=outs[i].at[j], send_sem=send_sems.at[i, j],
                    recv_sem=recv_sems.at[i, j], device_id=(qx, qy, c), device_id_type=MESH)
                cp.start()
                sends.append(cp)
        for cp in sends:
            cp.wait_recv()
        for cp in sends:
            cp.wait_send()

    sem = pltpu.SemaphoreType.DMA
    return pl.pallas_call(
        body,
        name=name,
        out_shape=[jax.ShapeDtypeStruct((N_CHIPS - 1,) + lay.block_half_shape(), BF16) for lay in layouts],
        in_specs=[ANY] * n,
        out_specs=[ANY] * n,
        scratch_shapes=[sem((n, 3)), sem((n, 3))],
    )(*sums)


def _rs_join_halves(shards, layouts, *, name):
    n = len(shards)

    def body(*refs):
        outs = refs[n:2 * n]
        send_sems, recv_sems = refs[2 * n:]
        x, y, c, _ = _place()
        sends = []
        for i in range(n):
            mine = _at(outs[i], layouts[i].shard_half_slice(c))
            cp = pltpu.make_async_remote_copy(
                src_ref=mine, dst_ref=mine, send_sem=send_sems.at[i], recv_sem=recv_sems.at[i],
                device_id=(x, y, 1 - c), device_id_type=MESH)
            cp.start()
            sends.append(cp)
        for i in range(n):
            theirs = _at(outs[i], layouts[i].shard_half_slice(1 - c))
            pltpu.make_async_remote_copy(
                src_ref=theirs, dst_ref=theirs, send_sem=send_sems.at[i], recv_sem=recv_sems.at[i],
                device_id=(x, y, 1 - c), device_id_type=MESH).wait_recv()
        for cp in sends:
            cp.wait_send()

    sem = pltpu.SemaphoreType.DMA
    return pl.pallas_call(
        body,
        name=name,
        out_shape=[jax.ShapeDtypeStruct(lay.shard_shape(), F32) for lay in layouts],
        in_specs=[ANY] * n,
        out_specs=[ANY] * n,
        input_output_aliases={i: i for i in range(n)},
        scratch_shapes=[sem((n,)), sem((n,))],
    )(*shards)


def _pack(arrays):
    flat = [a.reshape(-1) for a in arrays]
    sizes = [f.shape[0] for f in flat]
    total = sum(sizes)
    rows = _round_up(-(-total // LANES), 8)
    flat.append(jnp.zeros((rows * LANES - total,), F32))
    return jnp.concatenate(flat).reshape(rows, LANES)


def _unpack(buf, shapes):
    flat = buf.reshape(-1)
    out, pos = [], 0
    for shp in shapes:
        n = 1
        for d in shp:
            n *= d
        out.append(flat[pos:pos + n].reshape(shp))
        pos += n
    return out


def _pad_to(a, rows, cols):
    return jnp.pad(a, ((0, rows - a.shape[0]), (0, cols - a.shape[1])))


def kernel(x, mem, g_mix, w_in, conv_a_w, conv_a_b, ln_a_g, ln_a_b, conv_b_w, w_out, g_xattn, g_mem, w_q, w_k, w_v, w_o, g_ffn, w_gate, w_up, conv_f_w, w_down, g_final, loss_target, m_g_mix, m_w_in, m_conv_a_w, m_conv_a_b, m_ln_a_g, m_ln_a_b, m_conv_b_w, m_w_out, m_g_xattn, m_g_mem, m_w_q, m_w_k, m_w_v, m_w_o, m_g_ffn, m_w_gate, m_w_up, m_conv_f_w, m_w_down, m_g_final, v_g_mix, v_w_in, v_conv_a_w, v_conv_a_b, v_ln_a_g, v_ln_a_b, v_conv_b_w, v_w_out, v_g_xattn, v_g_mem, v_w_q, v_w_k, v_w_v, v_w_o, v_g_ffn, v_w_gate, v_w_up, v_conv_f_w, v_w_down, v_g_final):
    T, D = x.shape[1], x.shape[2]
    in_sh = w_in.shape[2]
    cw_sh = conv_a_w.shape[2]
    cw = N_CHIPS * cw_sh
    f_sh = w_gate.shape[2]
    fp = _round_up(f_sh, 256)
    F = N_CHIPS * fp
    rs = w_out.shape[1]
    c_idx = lax.axis_index("c")
    p_idx = 2 * lax.axis_index("x") + lax.axis_index("y")

    big = {
        "w_in": (w_in[0].astype(BF16), _Sharded((D, N_CHIPS * in_sh), 1)),
        "w_out": (w_out[0].astype(BF16), _Sharded((N_CHIPS * rs, D), 0)),
        "w_q": (w_q[0].astype(BF16), _Sharded((D, D), 0)),
        "w_k": (w_k[0].astype(BF16), _Sharded((D, D), 0)),
        "w_v": (w_v[0].astype(BF16), _Sharded((D, D), 0)),
        "w_o": (w_o[0].astype(BF16), _Sharded((D, D), 0)),
        "w_gate": (_pad_to(w_gate[0].astype(BF16), D, fp), _Sharded((D, F), 1)),
        "w_up": (_pad_to(w_up[0].astype(BF16), D, fp), _Sharded((D, F), 1)),
        "w_down": (_pad_to(w_down[0].astype(BF16), fp, D), _Sharded((F, D), 0)),
    }
    names = list(big)
    layouts = [big[k][1] for k in names]
    gathered = _allgather_weights([big[k][0] for k in names], layouts, name="allgather_weights")
    W = dict(zip(names, gathered))

    conv_shapes = [(KA_ROWS, cw_sh), (KS_ROWS, cw_sh), (KS_ROWS, fp)]
    conv_pack = _pack([_pad_to(conv_a_w[0], KA_ROWS, cw_sh), _pad_to(conv_b_w[0], KS_ROWS, cw_sh),
                       _pad_to(conv_f_w[0], KS_ROWS, fp)])
    conv_all = _allgather_small(conv_pack, name="allgather_conv", reduce=False)
    per_chip = [_unpack(conv_all[2 * q], conv_shapes) for q in range(N_CHIPS)]
    wa = jnp.concatenate([pc[0] for pc in per_chip], axis=1)
    wb = jnp.concatenate([pc[1] for pc in per_chip], axis=1)
    wf = jnp.concatenate([pc[2] for pc in per_chip], axis=1)

    xs, tgt, mems = x[0], loss_target[0], mem[0]
    g_mem2, g_final2 = g_mem[None, :], g_final[None, :]

    memn, rm = _norm_fwd(mems, g_mem2, name="norm_mem")
    xn1, r1 = _norm_fwd(xs, g_mix, name="norm_mix")
    proj = _mm(xn1, W["w_in"], name="mm_in")
    u3, u1 = _mix_a_fwd(proj, wa, conv_a_b, ln_a_g, ln_a_b, cw=cw, name="mix_a_fwd")
    vb = _mix_b_fwd(proj, wb, cw=cw, sw=cw, name="mix_b_fwd")
    mix = jnp.concatenate([u3, vb], axis=1)
    h1 = _mm(mix, W["w_out"], add=xs, name="mm_out")
    xn2, r2 = _norm_fwd(h1, g_xattn, name="norm_xattn")
    q = _mm(xn2, W["w_q"], out_dtype=BF16, name="mm_q")
    k = _mm(memn, W["w_k"], out_dtype=BF16, name="mm_k")
    vm = _mm(memn, W["w_v"], out_dtype=BF16, name="mm_v")
    o = _attn_fwd(q, k, vm, name="attn_fwd")
    h2 = _mm(o, W["w_o"], add=h1, name="mm_o")
    xn3, r3 = _norm_fwd(h2, g_ffn, name="norm_ffn")
    gp = _mm(xn3, W["w_gate"], name="mm_gate")
    up = _mm(xn3, W["w_up"], name="mm_up")
    f = _ffn_act_fwd(gp, up, wf, name="ffn_act_fwd")
    h3 = _mm(f, W["w_down"], add=h2, name="mm_down")

    loss_rows, dh3, dh3b, dg_final = _loss_head(h3, g_final2, tgt, name="loss_head")
    G = {}
    df = _mm(dh3b, W["w_down"], tb=True, out_dtype=BF16, name="mm_d_f")
    G["w_down"] = _mm(f, dh3b, ta=True, name="mm_dw_down")
    dgp, dup, dwf = _ffn_act_bwd(gp, up, df, wf, name="ffn_act_bwd")
    G["w_gate"] = _mm(xn3, dgp, ta=True, name="mm_dw_gate")
    G["w_up"] = _mm(xn3, dup, ta=True, name="mm_dw_up")
    dxn3 = _mm(dgp, W["w_gate"], tb=True, name="mm_dxn3_gate")
    dxn3 = _mm(dup, W["w_up"], tb=True, add=dxn3, name="mm_dxn3_up")
    dh2, dh2b, dg_ffn = _norm_bwd(h2, r3, g_ffn, dxn3, dh3, name="norm_ffn_bwd")
    do = _mm(dh2b, W["w_o"], tb=True, out_dtype=BF16, name="mm_d_o")
    G["w_o"] = _mm(o, dh2b, ta=True, name="mm_dw_o")
    dq, dk, dvm = _attn_bwd(q, k, vm, do, name="attn_bwd")
    dkb, dvb = dk.astype(BF16), dvm.astype(BF16)
    G["w_q"] = _mm(xn2, dq, ta=True, name="mm_dw_q")
    G["w_k"] = _mm(memn, dkb, ta=True, name="mm_dw_k")
    G["w_v"] = _mm(memn, dvb, ta=True, name="mm_dw_v")
    dxn2 = _mm(dq, W["w_q"], tb=True, name="mm_dxn2")
    dmemn = _mm(dkb, W["w_k"], tb=True, name="mm_dmem_k")
    dmemn = _mm(dvb, W["w_v"], tb=True, add=dmemn, name="mm_dmem_v")
    dg_mem = _norm_bwd(mems, rm, g_mem2, dmemn, None, name="norm_mem_bwd", want_dh=False)
    dh1, dh1b, dg_xattn = _norm_bwd(h1, r2, g_xattn, dxn2, dh2, name="norm_xattn_bwd")
    dmix = _mm(dh1b, W["w_out"], tb=True, name="mm_d_mix")
    G["w_out"] = _mm(mix, dh1b, ta=True, name="mm_dw_out")
    dav, dag, dwa, dba, dlg, dlb = _mix_a_bwd(proj, u1, dmix, wa, ln_a_g, ln_a_b, cw=cw, name="mix_a_bwd")
    dbg, dcg, dbh, dwb = _mix_b_bwd(proj, dmix, wb, cw=cw, sw=cw, name="mix_b_bwd")
    dproj = jnp.concatenate([dav, dag, dbg, dcg, dbh], axis=1)
    G["w_in"] = _mm(xn1, dproj, ta=True, name="mm_dw_in")
    dxn1 = _mm(dproj, W["w_in"], tb=True, name="mm_dxn1")
    dx, _, dg_mix = _norm_bwd(xs, r1, g_mix, dxn1, dh1, name="norm_mix_bwd")

    full = [G[k] for k in names]
    from_sibling = _rs_sibling_halves(full, layouts, name="rs_sibling_halves")
    c_arr = c_idx.astype(jnp.int32).reshape(1)
    chip_sums = [
        _half_sum(full[i], from_sibling[i], c_arr, rows_split=(layouts[i].axis == 1), name=f"half_sum_{names[i]}")
        for i in range(len(names))
    ]
    from_chips = _rs_chips(chip_sums, layouts, name="rs_chips")
    pc_arr = jnp.stack([p_idx, c_idx]).astype(jnp.int32)
    halves = [_sum_chips(chip_sums[i], from_chips[i], pc_arr, layouts[i], name=f"sum_chips_{names[i]}")
              for i in range(len(names))]
    shard_grads = dict(zip(names, _rs_join_halves(halves, layouts, name="rs_join_halves")))
    shard_grads["w_gate"] = shard_grads["w_gate"][:, :f_sh]
    shard_grads["w_up"] = shard_grads["w_up"][:, :f_sh]
    shard_grads["w_down"] = shard_grads["w_down"][:f_sh, :]

    loss_part = jnp.sum(loss_rows).reshape(1, 1)
    small_parts = [dg_mix, dba, dlg, dlb, dg_xattn, dg_mem, dg_ffn, dg_final, dwa, dwb, dwf, loss_part]
    small_shapes = [a.shape for a in small_parts]
    reduced = _allgather_small(_pack(small_parts), name="allreduce_small", reduce=True)
    (sg_mix, sba, slg, slb, sg_xattn, sg_mem, sg_ffn, sg_final, swa, swb, swf, loss_sum) = _unpack(reduced, small_shapes)
    loss = loss_sum.reshape(())
    ga_w = lax.dynamic_slice(swa, (0, p_idx * cw_sh), (KA, cw_sh))
    gb_w = lax.dynamic_slice(swb, (0, p_idx * cw_sh), (KS, cw_sh))
    gf_w = lax.dynamic_slice(swf, (0, p_idx * fp), (KS, f_sh))

    weights = dict(g_mix=g_mix, w_in=w_in, conv_a_w=conv_a_w, conv_a_b=conv_a_b, ln_a_g=ln_a_g, ln_a_b=ln_a_b,
                   conv_b_w=conv_b_w, w_out=w_out, g_xattn=g_xattn, g_mem=g_mem, w_q=w_q, w_k=w_k, w_v=w_v, w_o=w_o,
                   g_ffn=g_ffn, w_gate=w_gate, w_up=w_up, conv_f_w=conv_f_w, w_down=w_down, g_final=g_final)
    m_in = dict(g_mix=m_g_mix, w_in=m_w_in, conv_a_w=m_conv_a_w, conv_a_b=m_conv_a_b, ln_a_g=m_ln_a_g, ln_a_b=m_ln_a_b,
                conv_b_w=m_conv_b_w, w_out=m_w_out, g_xattn=m_g_xattn, g_mem=m_g_mem, w_q=m_w_q, w_k=m_w_k, w_v=m_w_v,
                w_o=m_w_o, g_ffn=m_g_ffn, w_gate=m_w_gate, w_up=m_w_up, conv_f_w=m_conv_f_w, w_down=m_w_down,
                g_final=m_g_final)
    v_in = dict(g_mix=v_g_mix, w_in=v_w_in, conv_a_w=v_conv_a_w, conv_a_b=v_conv_a_b, ln_a_g=v_ln_a_g, ln_a_b=v_ln_a_b,
                conv_b_w=v_conv_b_w, w_out=v_w_out, g_xattn=v_g_xattn, g_mem=v_g_mem, w_q=v_w_q, w_k=v_w_k, w_v=v_w_v,
                w_o=v_w_o, g_ffn=v_g_ffn, w_gate=v_w_gate, w_up=v_w_up, conv_f_w=v_conv_f_w, w_down=v_w_down,
                g_final=v_g_final)
    order = list(weights)
    grads = dict(shard_grads)
    grads.update(g_mix=sg_mix, conv_a_w=ga_w, conv_a_b=sba, ln_a_g=slg, ln_a_b=slb, conv_b_w=gb_w, g_xattn=sg_xattn,
                 g_mem=sg_mem, g_ffn=sg_ffn, conv_f_w=gf_w, g_final=sg_final)
    grads = {k: grads[k].reshape(weights[k].shape) for k in order}

    delta, new_m, new_v = {}, {}, {}
    for k in names:
        shp = weights[k].shape
        d_, m_, v_ = _adamw(weights[k][0], grads[k][0], m_in[k][0], v_in[k][0], name=f"adamw_{k}")
        delta[k], new_m[k], new_v[k] = d_.reshape(shp), m_.reshape(shp), v_.reshape(shp)
    small = [k for k in order if k not in big]
    small_shapes = [weights[k].shape for k in small]
    packed = [_pack([src[k] for k in small]) for src in (weights, grads, m_in, v_in)]
    d_, m_, v_ = _adamw(*packed, name="adamw_small")
    for k, dd, mm_, vv in zip(small, _unpack(d_, small_shapes), _unpack(m_, small_shapes), _unpack(v_, small_shapes)):
        delta[k], new_m[k], new_v[k] = dd, mm_, vv

    return (loss, dx[None], *[grads[k] for k in order], *[delta[k] for k in order],
            *[new_m[k] for k in order], *[new_v[k] for k in order])
```

```python
import functools

import jax
import jax.numpy as jnp
from jax import lax
from jax.experimental import pallas as pl
from jax.experimental.pallas import tpu as pltpu

F32 = jnp.float32
BF16 = jnp.bfloat16
EPS = 1e-6
N_HEADS = 4
GROUP = 128
KA = 31
KS = 3
KA_ROWS = 32
KS_ROWS = 8
HALO_A = 32
HALO_S = 8
N_CHIPS = 4
N_DEV = 8
LANES = 128
VMEM_LIMIT = 56 * 1024 * 1024
ADAMW_BLOCK_BYTES = 1 << 20
MESH = pl.DeviceIdType.MESH
ANY = pl.BlockSpec(memory_space=pl.ANY)

ADAM_LR = 0.001
ADAM_B1 = 0.9
ADAM_B2 = 0.999
ADAM_EPS = 1e-08
ADAM_WD = 0.01
ADAM_STEP = 10


def _pick(dim, pref, mult=LANES):
    if dim <= pref:
        return dim
    t = (pref // mult) * mult
    while t >= mult:
        if dim % t == 0:
            return t
        t -= mult
    return dim


def _round_up(n, m):
    return ((n + m - 1) // m) * m


def _params(*sem):
    return pltpu.CompilerParams(dimension_semantics=sem, vmem_limit_bytes=VMEM_LIMIT)


def _sigmoid(x):
    return jax.nn.sigmoid(x)


def _mm(a, b, *, name, ta=False, tb=False, out_dtype=F32, add=None, deps=(), tm=1024, tn=512, tk=4096):
    if ta:
        K, M = a.shape
    else:
        M, K = a.shape
    if tb:
        N, K2 = b.shape
    else:
        K2, N = b.shape
    assert K == K2, (a.shape, b.shape)
    tm, tn, tk = _pick(M, tm), _pick(N, tn), _pick(K, tk)
    nk = K // tk
    a_spec = pl.BlockSpec((tk, tm), lambda i, j, k: (k, i)) if ta else pl.BlockSpec((tm, tk), lambda i, j, k: (i, k))
    b_spec = pl.BlockSpec((tn, tk), lambda i, j, k: (j, k)) if tb else pl.BlockSpec((tk, tn), lambda i, j, k: (k, j))
    o_spec = pl.BlockSpec((tm, tn), lambda i, j, k: (i, j))
    dims = (((0,) if ta else (1,), (1,) if tb else (0,)), ((), ()))
    has_add = add is not None

    def body(*refs):
        a_ref, b_ref = refs[:2]
        add_ref = refs[2] if has_add else None
        o_ref = refs[(3 if has_add else 2) + len(deps)]
        acc_ref = refs[-1] if nk > 1 else None
        k = pl.program_id(2)
        part = lax.dot_general(a_ref[...], b_ref[...], dims, preferred_element_type=F32)

        def finish(r):
            if add_ref is not None:
                r = add_ref[...] + r
            o_ref[...] = r.astype(out_dtype)

        if nk == 1:
            finish(part)
        else:
            @pl.when(k == 0)
            def _():
                acc_ref[...] = part

            @pl.when(jnp.logical_and(k > 0, k < nk - 1))
            def _():
                acc_ref[...] += part

            @pl.when(k == nk - 1)
            def _():
                finish(acc_ref[...] + part)

    in_specs = [a_spec, b_spec] + ([o_spec] if has_add else []) + [ANY] * len(deps)
    args = (a, b) + ((add,) if has_add else ()) + tuple(deps)
    return pl.pallas_call(
        body,
        name=name,
        out_shape=jax.ShapeDtypeStruct((M, N), out_dtype),
        grid=(M // tm, N // tn, nk),
        in_specs=in_specs,
        out_specs=o_spec,
        scratch_shapes=[pltpu.VMEM((tm, tn), F32)] if nk > 1 else [],
        compiler_params=_params("parallel", "parallel", "arbitrary"),
    )(*args)


def _norm_fwd(h, g, *, name):
    T, D = h.shape
    R = _pick(T, 256, 8)

    def body(h_ref, g_ref, xn_ref, r_ref):
        x = h_ref[...]
        r = lax.rsqrt(jnp.mean(x * x, axis=-1, keepdims=True) + EPS)
        xn_ref[...] = ((x * r) * g_ref[...]).astype(BF16)
        r_ref[...] = r

    return pl.pallas_call(
        body,
        name=name,
        out_shape=(jax.ShapeDtypeStruct((T, D), BF16), jax.ShapeDtypeStruct((T, 1), F32)),
        grid=(T // R,),
        in_specs=[pl.BlockSpec((R, D), lambda i: (i, 0)), pl.BlockSpec((1, D), lambda i: (0, 0))],
        out_specs=(pl.BlockSpec((R, D), lambda i: (i, 0)), pl.BlockSpec((R, 1), lambda i: (i, 0))),
        compiler_params=_params("parallel"),
    )(h, g)


def _norm_bwd(h, r, g, dxn, dres, *, name, want_dh=True):
    T, D = h.shape
    R = _pick(T, 128, 8)
    has_res = dres is not None

    def body(*refs):
        h_ref, r_ref, g_ref, dxn_ref = refs[:4]
        pos = 4
        dres_ref = None
        if has_res:
            dres_ref = refs[pos]
            pos += 1
        if want_dh:
            dh_ref, dhb_ref, dg_ref = refs[pos:pos + 3]
        else:
            dg_ref = refs[pos]
        i = pl.program_id(0)
        rr = r_ref[...]
        hn = h_ref[...] * rr
        d = dxn_ref[...].astype(F32)
        gd = d * g_ref[...]
        part = jnp.sum(d * hn, axis=0, keepdims=True)

        @pl.when(i == 0)
        def _():
            dg_ref[...] = part

        @pl.when(i > 0)
        def _():
            dg_ref[...] += part

        if want_dh:
            dh = rr * (gd - hn * jnp.mean(gd * hn, axis=-1, keepdims=True))
            if dres_ref is not None:
                dh = dres_ref[...] + dh
            dh_ref[...] = dh
            dhb_ref[...] = dh.astype(BF16)

    row = pl.BlockSpec((R, D), lambda i: (i, 0))
    vec = pl.BlockSpec((1, D), lambda i: (0, 0))
    in_specs = [row, pl.BlockSpec((R, 1), lambda i: (i, 0)), vec, row] + ([row] if has_res else [])
    args = (h, r, g, dxn) + ((dres,) if has_res else ())
    if want_dh:
        out_shape = (jax.ShapeDtypeStruct((T, D), F32), jax.ShapeDtypeStruct((T, D), BF16), jax.ShapeDtypeStruct((1, D), F32))
        out_specs = (row, row, vec)
    else:
        out_shape = jax.ShapeDtypeStruct((1, D), F32)
        out_specs = vec
    return pl.pallas_call(
        body, name=name, out_shape=out_shape, grid=(T // R,), in_specs=in_specs, out_specs=out_specs,
        compiler_params=_params("arbitrary"),
    )(*args)


def _loss_head(h, g, tgt, *, name):
    T, D = h.shape
    R = _pick(T, 128, 8)

    def body(h_ref, g_ref, t_ref, loss_ref, dh_ref, dhb_ref, dg_ref):
        i = pl.program_id(0)
        x = h_ref[...]
        gg = g_ref[...]
        r = lax.rsqrt(jnp.mean(x * x, axis=-1, keepdims=True) + EPS)
        hn = x * r
        e = hn * gg - t_ref[...]
        loss_ref[...] = 0.5 * jnp.mean(e * e, axis=-1, keepdims=True)
        dy = e * (1.0 / D)
        gd = dy * gg
        dh = r * (gd - hn * jnp.mean(gd * hn, axis=-1, keepdims=True))
        dh_ref[...] = dh
        dhb_ref[...] = dh.astype(BF16)
        part = jnp.sum(dy * hn, axis=0, keepdims=True)

        @pl.when(i == 0)
        def _():
            dg_ref[...] = part

        @pl.when(i > 0)
        def _():
            dg_ref[...] += part

    row = pl.BlockSpec((R, D), lambda i: (i, 0))
    vec = pl.BlockSpec((1, D), lambda i: (0, 0))
    return pl.pallas_call(
        body,
        name=name,
        out_shape=(jax.ShapeDtypeStruct((T, 1), F32), jax.ShapeDtypeStruct((T, D), F32),
                   jax.ShapeDtypeStruct((T, D), BF16), jax.ShapeDtypeStruct((1, D), F32)),
        grid=(T // R,),
        in_specs=[row, vec, row],
        out_specs=(pl.BlockSpec((R, 1), lambda i: (i, 0)), row, row, vec),
        compiler_params=_params("arbitrary"),
    )(h, g, tgt)


_NT = (((1,), (1,)), ((), ()))
_TN = (((0,), (0,)), ((), ()))
_NN = (((1,), (0,)), ((), ()))


def _softmax_rows(s):
    m = jnp.max(s, axis=-1, keepdims=True)
    e = jnp.exp(s - m)
    return e / jnp.sum(e, axis=-1, keepdims=True)


def _attn_fwd(q, k, v, *, name):
    T, D = q.shape
    ML = k.shape[0]
    dh = D // N_HEADS
    scale = dh ** -0.5
    R = _pick(T, 512, 16)

    def body(q_ref, k_ref, v_ref, o_ref):
        s = lax.dot_general(q_ref[...], k_ref[...], _NT, preferred_element_type=F32) * scale
        p = _softmax_rows(s)
        o_ref[...] = lax.dot_general(p.astype(BF16), v_ref[...], _NN, preferred_element_type=F32).astype(BF16)

    qs = pl.BlockSpec((R, dh), lambda i, h: (i, h))
    ks = pl.BlockSpec((ML, dh), lambda i, h: (0, h))
    return pl.pallas_call(
        body, name=name, out_shape=jax.ShapeDtypeStruct((T, D), BF16), grid=(T // R, N_HEADS),
        in_specs=[qs, ks, ks], out_specs=qs, compiler_params=_params("parallel", "parallel"),
    )(q, k, v)


def _attn_bwd(q, k, v, do, *, name):
    T, D = q.shape
    ML = k.shape[0]
    dh = D // N_HEADS
    scale = dh ** -0.5
    R = _pick(T, 512, 16)

    def body(q_ref, k_ref, v_ref, do_ref, dq_ref, dk_ref, dv_ref):
        i = pl.program_id(1)
        qq, kk, vv, dd = q_ref[...], k_ref[...], v_ref[...], do_ref[...]
        s = lax.dot_general(qq, kk, _NT, preferred_element_type=F32) * scale
        p = _softmax_rows(s)
        dp = lax.dot_general(dd, vv, _NT, preferred_element_type=F32)
        dv_part = lax.dot_general(p.astype(BF16), dd, _TN, preferred_element_type=F32)
        ds = (p * (dp - jnp.sum(p * dp, axis=-1, keepdims=True)) * scale).astype(BF16)
        dq_ref[...] = lax.dot_general(ds, kk, _NN, preferred_element_type=F32).astype(BF16)
        dk_part = lax.dot_general(ds, qq, _TN, preferred_element_type=F32)

        @pl.when(i == 0)
        def _():
            dk_ref[...] = dk_part
            dv_ref[...] = dv_part

        @pl.when(i > 0)
        def _():
            dk_ref[...] += dk_part
            dv_ref[...] += dv_part

    qs = pl.BlockSpec((R, dh), lambda h, i: (i, h))
    ks = pl.BlockSpec((ML, dh), lambda h, i: (0, h))
    return pl.pallas_call(
        body,
        name=name,
        out_shape=(jax.ShapeDtypeStruct((T, D), BF16), jax.ShapeDtypeStruct((ML, D), F32), jax.ShapeDtypeStruct((ML, D), F32)),
        grid=(N_HEADS, T // R),
        in_specs=[qs, ks, ks, qs],
        out_specs=(qs, ks, ks),
        compiler_params=_params("parallel", "arbitrary"),
    )(q, k, v, do)


def _prev_halo(R, halo, CT, col):
    per = R // halo
    return lambda i, j: (jnp.maximum(i * per - 1, 0), col(j))


def _next_halo(R, halo, T, col):
    per = R // halo
    last = T // halo - 1
    return lambda i, j: (jnp.minimum((i + 1) * per, last), col(j))


def _swap(f):
    return lambda j, i: f(i, j)


def _ffn_act_fwd(gp, up, wf, *, name):
    T, F = gp.shape
    R = _pick(T, 256, 8)
    CT = _pick(F, 1024)
    H = HALO_S

    def body(g_ref, gh_ref, u_ref, w_ref, f_ref, ext):
        i = pl.program_id(0)
        ext[pl.ds(0, H), :] = jnp.where(i > 0, gh_ref[...], 0.0)
        ext[pl.ds(H, R), :] = g_ref[...]
        w = w_ref[...]
        g = ext[pl.ds(H - 2, R), :] * w[0:1, :]
        g = g + ext[pl.ds(H - 1, R), :] * w[1:2, :]
        g = g + ext[pl.ds(H, R), :] * w[2:3, :]
        f_ref[...] = (g * _sigmoid(g) * u_ref[...]).astype(BF16)

    main = pl.BlockSpec((R, CT), lambda i, j: (i, j))
    return pl.pallas_call(
        body,
        name=name,
        out_shape=jax.ShapeDtypeStruct((T, F), BF16),
        grid=(T // R, F // CT),
        in_specs=[main, pl.BlockSpec((H, CT), _prev_halo(R, H, CT, lambda j: j)), main,
                  pl.BlockSpec((KS_ROWS, CT), lambda i, j: (0, j))],
        out_specs=main,
        scratch_shapes=[pltpu.VMEM((R + H, CT), F32)],
        compiler_params=_params("parallel", "parallel"),
    )(gp, gp, up, wf)


def _ffn_act_bwd(gp, up, df, wf, *, name):
    T, F = gp.shape
    R = _pick(T, 256, 16)
    CT = _pick(F, 1024)
    H = HALO_S
    HB = 16
    n_t = T // R

    def body(g_ref, gp_ref, gn_ref, u_ref, un_ref, d_ref, dn_ref, w_ref, dg_out, du_out, dw_ref, ext, dge):
        i = pl.program_id(1)
        last = i == n_t - 1
        ext[pl.ds(0, H), :] = jnp.where(i > 0, gp_ref[...], 0.0)
        ext[pl.ds(H, R), :] = g_ref[...]
        ext[pl.ds(H + R, H), :] = gn_ref[...]
        w = w_ref[...]
        w0, w1, w2 = w[0:1, :], w[1:2, :], w[2:3, :]

        def conv(start, n):
            g = ext[pl.ds(start + H - 2, n), :] * w0
            g = g + ext[pl.ds(start + H - 1, n), :] * w1
            return g + ext[pl.ds(start + H, n), :] * w2

        def dact(g, u, d):
            sg = _sigmoid(g)
            return d * u * (sg * (1.0 + g * (1.0 - sg))), d * (g * sg)

        d_main = d_ref[...].astype(F32)
        dg_main, du_main = dact(conv(0, R), u_ref[...], d_main)
        du_out[...] = du_main.astype(BF16)
        d_next = jnp.where(last, 0.0, dn_ref[pl.ds(0, H), :].astype(F32))
        dg_next, _ = dact(conv(R, H), un_ref[...], d_next)
        dg_next = jnp.where(last, 0.0, dg_next)
        dge[pl.ds(0, R), :] = dg_main
        dge[pl.ds(R, H), :] = dg_next
        dgp = dge[pl.ds(0, R), :] * w2 + dge[pl.ds(1, R), :] * w1 + dge[pl.ds(2, R), :] * w0
        dg_out[...] = dgp.astype(BF16)
        rows = [jnp.sum(dg_main * ext[pl.ds(H - 2 + k, R), :], axis=0, keepdims=True) for k in range(KS)]
        rows.append(jnp.zeros((KS_ROWS - KS, CT), F32))
        _acc_rows_block(dw_ref, i, rows)

    col = lambda j: j
    main = pl.BlockSpec((R, CT), lambda j, i: (i, j))
    prev8 = pl.BlockSpec((H, CT), _swap(_prev_halo(R, H, CT, col)))
    next8 = pl.BlockSpec((H, CT), _swap(_next_halo(R, H, T, col)))
    next16 = pl.BlockSpec((HB, CT), _swap(_next_halo(R, HB, T, col)))
    wspec = pl.BlockSpec((KS_ROWS, CT), lambda j, i: (0, j))
    return pl.pallas_call(
        body,
        name=name,
        out_shape=(jax.ShapeDtypeStruct((T, F), BF16), jax.ShapeDtypeStruct((T, F), BF16), jax.ShapeDtypeStruct((KS_ROWS, F), F32)),
        grid=(F // CT, n_t),
        in_specs=[main, prev8, next8, main, next8, main, next16, wspec],
        out_specs=(main, main, wspec),
        scratch_shapes=[pltpu.VMEM((R + 2 * H, CT), F32), pltpu.VMEM((R + H, CT), F32)],
        compiler_params=_params("parallel", "arbitrary"),
    )(gp, gp, gp, up, up, df, df, wf)


def _acc_rows_block(ref, i, rows):
    *singles, pad = rows

    @pl.when(i == 0)
    def _():
        for k, row in enumerate(singles):
            ref[pl.ds(k, 1), :] = row
        ref[pl.ds(len(singles), pad.shape[0]), :] = pad

    @pl.when(i > 0)
    def _():
        for k, row in enumerate(singles):
            ref[pl.ds(k, 1), :] += row


def _group_stats(x):
    mu = jnp.mean(x, axis=-1, keepdims=True)
    xc = x - mu
    var = jnp.mean(xc * xc, axis=-1, keepdims=True)
    return xc, lax.rsqrt(var + EPS)


def _mix_a_fwd(proj, wa, ba, lg, lb, *, cw, name):
    T = proj.shape[0]
    R = _pick(T, 128, HALO_A)
    CT = _pick(cw, 256)
    nc = cw // CT
    H = HALO_A

    def body(av_ref, ag_ref, avh_ref, agh_ref, w_ref, b_ref, lg_ref, lb_ref, u3_ref, u1_ref, ext):
        i = pl.program_id(0)
        ext[pl.ds(0, H), :] = jnp.where(i > 0, avh_ref[...] * _sigmoid(agh_ref[...]), 0.0)
        ext[pl.ds(H, R), :] = av_ref[...] * _sigmoid(ag_ref[...])
        w = w_ref[...]
        acc = ext[pl.ds(H - (KA - 1), R), :] * w[0:1, :]
        for k in range(1, KA):
            acc = acc + ext[pl.ds(H - (KA - 1) + k, R), :] * w[k:k + 1, :]
        u1 = acc + b_ref[...]
        u1_ref[...] = u1
        for gi in range(CT // GROUP):
            sl = slice(gi * GROUP, (gi + 1) * GROUP)
            xc, rstd = _group_stats(u1[:, sl])
            u2 = (xc * rstd) * lg_ref[:, sl] + lb_ref[:, sl]
            u3_ref[:, sl] = (u2 * _sigmoid(u2)).astype(BF16)

    main_v = pl.BlockSpec((R, CT), lambda i, j: (i, j))
    main_g = pl.BlockSpec((R, CT), lambda i, j: (i, j + nc))
    halo_v = pl.BlockSpec((H, CT), _prev_halo(R, H, CT, lambda j: j))
    halo_g = pl.BlockSpec((H, CT), _prev_halo(R, H, CT, lambda j: j + nc))
    vec = pl.BlockSpec((1, CT), lambda i, j: (0, j))
    out = pl.BlockSpec((R, CT), lambda i, j: (i, j))
    return pl.pallas_call(
        body,
        name=name,
        out_shape=(jax.ShapeDtypeStruct((T, cw), BF16), jax.ShapeDtypeStruct((T, cw), F32)),
        grid=(T // R, nc),
        in_specs=[main_v, main_g, halo_v, halo_g, pl.BlockSpec((KA_ROWS, CT), lambda i, j: (0, j)), vec, vec, vec],
        out_specs=(out, out),
        scratch_shapes=[pltpu.VMEM((R + H, CT), F32)],
        compiler_params=_params("parallel", "parallel"),
    )(proj, proj, proj, proj, wa, ba, lg, lb)


def _mix_a_bwd(proj, u1, dmix, wa, lg, lb, *, cw, name):
    T = proj.shape[0]
    R = _pick(T, 128, HALO_A)
    CT = _pick(cw, 256)
    nc = cw // CT
    H = HALO_A
    n_t = T // R
    NG = CT // GROUP

    def body(av_ref, ag_ref, avh_ref, agh_ref, u1_ref, u1n_ref, d3_ref, d3n_ref, w_ref, lg_ref, lb_ref,
             dav_ref, dag_ref, dw_ref, db_ref, dlg_ref, dlb_ref, ext, d1):
        i = pl.program_id(1)
        last = i == n_t - 1

        def ln_bwd(u1, d3, sl):
            xc, rstd = _group_stats(u1)
            xh = xc * rstd
            g = lg_ref[:, sl]
            u2 = xh * g + lb_ref[:, sl]
            sg = _sigmoid(u2)
            du2 = d3 * (sg * (1.0 + u2 * (1.0 - sg)))
            dxh = du2 * g
            du1 = rstd * (dxh - jnp.mean(dxh, axis=-1, keepdims=True) - xh * jnp.mean(dxh * xh, axis=-1, keepdims=True))
            return du1, du2 * xh, du2

        dlg_parts, dlb_parts = [], []
        for gi in range(NG):
            sl = slice(gi * GROUP, (gi + 1) * GROUP)
            du1, dlg, dlb = ln_bwd(u1_ref[:, sl], d3_ref[:, sl], sl)
            d1[pl.ds(0, R), sl] = du1
            dlg_parts.append(jnp.sum(dlg, axis=0, keepdims=True))
            dlb_parts.append(jnp.sum(dlb, axis=0, keepdims=True))
            du1n, _, _ = ln_bwd(u1n_ref[:, sl], jnp.where(last, 0.0, d3n_ref[:, sl]), sl)
            d1[pl.ds(R, H), sl] = jnp.where(last, 0.0, du1n)
        dlg_row = jnp.concatenate(dlg_parts, axis=1) if NG > 1 else dlg_parts[0]
        dlb_row = jnp.concatenate(dlb_parts, axis=1) if NG > 1 else dlb_parts[0]
        d1_main = d1[pl.ds(0, R), :]
        db_row = jnp.sum(d1_main, axis=0, keepdims=True)

        @pl.when(i == 0)
        def _():
            dlg_ref[...] = dlg_row
            dlb_ref[...] = dlb_row
            db_ref[...] = db_row

        @pl.when(i > 0)
        def _():
            dlg_ref[...] += dlg_row
            dlb_ref[...] += dlb_row
            db_ref[...] += db_row

        w = w_ref[...]
        du0 = d1[pl.ds(KA - 1, R), :] * w[0:1, :]
        for k in range(1, KA):
            du0 = du0 + d1[pl.ds(KA - 1 - k, R), :] * w[k:k + 1, :]
        av = av_ref[...]
        sg = _sigmoid(ag_ref[...])
        dav_ref[...] = (du0 * sg).astype(BF16)
        dag_ref[...] = (du0 * av * sg * (1.0 - sg)).astype(BF16)

        ext[pl.ds(0, H), :] = jnp.where(i > 0, avh_ref[...] * _sigmoid(agh_ref[...]), 0.0)
        ext[pl.ds(H, R), :] = av * sg
        rows = [jnp.sum(d1_main * ext[pl.ds(H - (KA - 1) + k, R), :], axis=0, keepdims=True) for k in range(KA)]
        rows.append(jnp.zeros((KA_ROWS - KA, CT), F32))
        _acc_rows_block(dw_ref, i, rows)

    cv = lambda j: j
    cg = lambda j: j + nc
    main_v = pl.BlockSpec((R, CT), lambda j, i: (i, j))
    main_g = pl.BlockSpec((R, CT), lambda j, i: (i, j + nc))
    prev_v = pl.BlockSpec((H, CT), _swap(_prev_halo(R, H, CT, cv)))
    prev_g = pl.BlockSpec((H, CT), _swap(_prev_halo(R, H, CT, cg)))
    nxt = pl.BlockSpec((H, CT), _swap(_next_halo(R, H, T, cv)))
    wspec = pl.BlockSpec((KA_ROWS, CT), lambda j, i: (0, j))
    vec = pl.BlockSpec((1, CT), lambda j, i: (0, j))
    vshape = jax.ShapeDtypeStruct((1, cw), F32)
    return pl.pallas_call(
        body,
        name=name,
        out_shape=(jax.ShapeDtypeStruct((T, cw), BF16), jax.ShapeDtypeStruct((T, cw), BF16),
                   jax.ShapeDtypeStruct((KA_ROWS, cw), F32), vshape, vshape, vshape),
        grid=(nc, n_t),
        in_specs=[main_v, main_g, prev_v, prev_g, main_v, nxt, main_v, nxt, wspec, vec, vec],
        out_specs=(main_v, main_v, wspec, vec, vec, vec),
        scratch_shapes=[pltpu.VMEM((R + H, CT), F32), pltpu.VMEM((R + H, CT), F32)],
        compiler_params=_params("parallel", "arbitrary"),
    )(proj, proj, proj, proj, u1, u1, dmix, dmix, wa, lg, lb)


def _mix_b_fwd(proj, wb, *, cw, sw, name):
    T = proj.shape[0]
    R = _pick(T, 256, 8)
    CT = _pick(sw, 512)
    nb, nc, nh = (2 * cw) // CT, (2 * cw + sw) // CT, (2 * cw + 2 * sw) // CT
    H = HALO_S

    def body(b_ref, c_ref, h_ref, ch_ref, hh_ref, w_ref, v_ref, ext):
        i = pl.program_id(0)
        ext[pl.ds(0, H), :] = jnp.where(i > 0, ch_ref[...] * hh_ref[...], 0.0)
        ext[pl.ds(H, R), :] = c_ref[...] * h_ref[...]
        w = w_ref[...]
        zc = ext[pl.ds(H - 2, R), :] * w[0:1, :]
        zc = zc + ext[pl.ds(H - 1, R), :] * w[1:2, :]
        zc = zc + ext[pl.ds(H, R), :] * w[2:3, :]
        v_ref[...] = (b_ref[...] * zc).astype(BF16)

    def main(off):
        return pl.BlockSpec((R, CT), lambda i, j: (i, j + off))

    def prev(off):
        return pl.BlockSpec((H, CT), _prev_halo(R, H, CT, lambda j: j + off))

    return pl.pallas_call(
        body,
        name=name,
        out_shape=jax.ShapeDtypeStruct((T, sw), BF16),
        grid=(T // R, sw // CT),
        in_specs=[main(nb), main(nc), main(nh), prev(nc), prev(nh), pl.BlockSpec((KS_ROWS, CT), lambda i, j: (0, j))],
        out_specs=pl.BlockSpec((R, CT), lambda i, j: (i, j)),
        scratch_shapes=[pltpu.VMEM((R + H, CT), F32)],
        compiler_params=_params("parallel", "parallel"),
    )(proj, proj, proj, proj, proj, wb)


def _mix_b_bwd(proj, dmix, wb, *, cw, sw, name):
    T = proj.shape[0]
    R = _pick(T, 256, 8)
    CT = _pick(sw, 512)
    nb, nc, nh = (2 * cw) // CT, (2 * cw + sw) // CT, (2 * cw + 2 * sw) // CT
    nd = cw // CT
    H = HALO_S
    n_t = T // R

    def body(b_ref, bn_ref, c_ref, cp_ref, cn_ref, h_ref, hp_ref, hn_ref, d_ref, dn_ref, w_ref,
             db_ref, dc_ref, dhh_ref, dw_ref, ext, dze):
        i = pl.program_id(1)
        last = i == n_t - 1
        cc, hh = c_ref[...], h_ref[...]
        ext[pl.ds(0, H), :] = jnp.where(i > 0, cp_ref[...] * hp_ref[...], 0.0)
        ext[pl.ds(H, R), :] = cc * hh
        ext[pl.ds(H + R, H), :] = cn_ref[...] * hn_ref[...]
        w = w_ref[...]
        w0, w1, w2 = w[0:1, :], w[1:2, :], w[2:3, :]

        def conv(start, n):
            z = ext[pl.ds(start + H - 2, n), :] * w0
            z = z + ext[pl.ds(start + H - 1, n), :] * w1
            return z + ext[pl.ds(start + H, n), :] * w2

        d_main = d_ref[...]
        db_ref[...] = (d_main * conv(0, R)).astype(BF16)
        dzc_main = d_main * b_ref[...]
        dze[pl.ds(0, R), :] = dzc_main
        dze[pl.ds(R, H), :] = jnp.where(last, 0.0, dn_ref[...] * bn_ref[...])
        dz = dze[pl.ds(0, R), :] * w2 + dze[pl.ds(1, R), :] * w1 + dze[pl.ds(2, R), :] * w0
        dc_ref[...] = (dz * hh).astype(BF16)
        dhh_ref[...] = (dz * cc).astype(BF16)
        rows = [jnp.sum(dzc_main * ext[pl.ds(H - 2 + k, R), :], axis=0, keepdims=True) for k in range(KS)]
        rows.append(jnp.zeros((KS_ROWS - KS, CT), F32))
        _acc_rows_block(dw_ref, i, rows)

    def main(off):
        return pl.BlockSpec((R, CT), lambda j, i: (i, j + off))

    def prev(off):
        return pl.BlockSpec((H, CT), _swap(_prev_halo(R, H, CT, lambda j: j + off)))

    def nxt(off):
        return pl.BlockSpec((H, CT), _swap(_next_halo(R, H, T, lambda j: j + off)))

    out = pl.BlockSpec((R, CT), lambda j, i: (i, j))
    wspec = pl.BlockSpec((KS_ROWS, CT), lambda j, i: (0, j))
    act = jax.ShapeDtypeStruct((T, sw), BF16)
    return pl.pallas_call(
        body,
        name=name,
        out_shape=(act, act, act, jax.ShapeDtypeStruct((KS_ROWS, sw), F32)),
        grid=(sw // CT, n_t),
        in_specs=[main(nb), nxt(nb), main(nc), prev(nc), nxt(nc), main(nh), prev(nh), nxt(nh), main(nd), nxt(nd), wspec],
        out_specs=(out, out, out, wspec),
        scratch_shapes=[pltpu.VMEM((R + 2 * H, CT), F32), pltpu.VMEM((R + H, CT), F32)],
        compiler_params=_params("parallel", "arbitrary"),
    )(proj, proj, proj, proj, proj, proj, proj, proj, dmix, dmix, wb)


def _adamw(w, g, m, v, *, name):
    Rr, Cc = w.shape
    R = _pick(Rr, max(8, ADAMW_BLOCK_BYTES // (4 * Cc)), 8)

    def body(w_ref, g_ref, m_ref, v_ref, d_ref, mo_ref, vo_ref):
        g = g_ref[...]
        m2 = ADAM_B1 * m_ref[...] + (1.0 - ADAM_B1) * g
        v2 = ADAM_B2 * v_ref[...] + (1.0 - ADAM_B2) * (g * g)
        m_hat = m2 / (1.0 - ADAM_B1 ** ADAM_STEP)
        v_hat = v2 / (1.0 - ADAM_B2 ** ADAM_STEP)
        d_ref[...] = -ADAM_LR * (m_hat / (jnp.sqrt(v_hat) + ADAM_EPS) + ADAM_WD * w_ref[...])
        mo_ref[...] = m2
        vo_ref[...] = v2

    spec = pl.BlockSpec((R, Cc), lambda i: (i, 0))
    shp = jax.ShapeDtypeStruct((Rr, Cc), F32)
    return pl.pallas_call(
        body, name=name, out_shape=(shp, shp, shp), grid=(Rr // R,), in_specs=[spec] * 4, out_specs=(spec, spec, spec),
        compiler_params=_params("parallel"),
    )(w, g, m, v)


def _half_sum(g, ra, c, *, rows_split, name):
    Rr, Cc = ra.shape
    tr, tc = _pick(Rr, 512, 16), _pick(Cc, 2048)
    nr, ncol = Rr // tr, Cc // tc
    if rows_split:
        g_map = lambda i, j, c_ref: (i + c_ref[0] * nr, j)
    else:
        g_map = lambda i, j, c_ref: (i, j + c_ref[0] * ncol)

    def body(c_ref, g_ref, ra_ref, o_ref):
        o_ref[...] = (g_ref[...] + ra_ref[...]).astype(BF16)

    spec = pl.BlockSpec((tr, tc), lambda i, j, c_ref: (i, j))
    return pl.pallas_call(
        body,
        name=name,
        out_shape=jax.ShapeDtypeStruct((Rr, Cc), BF16),
        grid_spec=pltpu.PrefetchScalarGridSpec(
            num_scalar_prefetch=1, grid=(nr, ncol), in_specs=[pl.BlockSpec((tr, tc), g_map), spec], out_specs=spec),
        compiler_params=_params("parallel", "parallel"),
    )(c, g, ra)


def _sum_chips(cs, rb, pc, lay, *, name):
    _, Rr, Cc = rb.shape
    tr, tc = _pick(Rr, 512, 16), _pick(Cc, 2048)
    nr, ncol = Rr // tr, Cc // tc
    if lay.axis == 0:
        own_map = lambda i, j, s: (i + s[0] * nr, j)
        out_map = lambda i, j, s: (i, j + s[1] * ncol)
    else:
        own_map = lambda i, j, s: (i, j + s[0] * ncol)
        out_map = lambda i, j, s: (i + s[1] * nr, j)

    def body(s_ref, own_ref, rb_ref, o_ref):
        acc = own_ref[...].astype(F32)
        for j in range(N_CHIPS - 1):
            acc = acc + rb_ref[j].astype(F32)
        o_ref[...] = acc

    return pl.pallas_call(
        body,
        name=name,
        out_shape=jax.ShapeDtypeStruct(lay.shard_shape(), F32),
        grid_spec=pltpu.PrefetchScalarGridSpec(
            num_scalar_prefetch=1,
            grid=(nr, ncol),
            in_specs=[pl.BlockSpec((tr, tc), own_map), pl.BlockSpec((N_CHIPS - 1, tr, tc), lambda i, j, s: (0, i, j))],
            out_specs=pl.BlockSpec((tr, tc), out_map)),
        compiler_params=_params("parallel", "parallel"),
    )(pc, cs, rb)


def _place():
    x, y, c = lax.axis_index("x"), lax.axis_index("y"), lax.axis_index("c")
    return x, y, c, 2 * x + y


def _other_chips(x, y):
    return [(1 - x, y, 2 * (1 - x) + y), (x, 1 - y, 2 * x + (1 - y)), (1 - x, 1 - y, 2 * (1 - x) + (1 - y))]


def _allgather_small(buf, *, name, reduce):
    S = buf.shape[0]

    def body(x_ref, o_ref, gat, send_sems, recv_sems):
        x, y, c, _ = _place()
        me = 4 * x + 2 * y + c
        gat[me] = x_ref[...]
        copies = []
        for k in range(1, N_DEV):
            fx, fy, fc = (k >> 2) & 1, (k >> 1) & 1, k & 1
            px = 1 - x if fx else x
            py = 1 - y if fy else y
            pc = 1 - c if fc else c
            peer = 4 * px + 2 * py + pc
            send = pltpu.make_async_remote_copy(
                src_ref=x_ref, dst_ref=gat.at[me], send_sem=send_sems.at[k - 1], recv_sem=recv_sems.at[k - 1],
                device_id=(px, py, pc), device_id_type=MESH)
            send.start()
            arrival = pltpu.make_async_remote_copy(
                src_ref=x_ref, dst_ref=gat.at[peer], send_sem=send_sems.at[k - 1], recv_sem=recv_sems.at[k - 1],
                device_id=(px, py, pc), device_id_type=MESH)
            copies.append((send, arrival))
        for send, arrival in copies:
            arrival.wait_recv()
        for send, arrival in copies:
            send.wait_send()
        if reduce:
            acc = gat[0]
            for d in range(1, N_DEV):
                acc = acc + gat[d]
            o_ref[...] = acc
        else:
            o_ref[...] = gat[...]

    out_shape = jax.ShapeDtypeStruct((S, LANES) if reduce else (N_DEV, S, LANES), F32)
    return pl.pallas_call(
        body,
        name=name,
        out_shape=out_shape,
        in_specs=[pl.BlockSpec(memory_space=pltpu.VMEM)],
        out_specs=pl.BlockSpec(memory_space=pltpu.VMEM),
        scratch_shapes=[pltpu.VMEM((N_DEV, S, LANES), F32), pltpu.SemaphoreType.DMA((N_DEV - 1,)),
                        pltpu.SemaphoreType.DMA((N_DEV - 1,))],
        compiler_params=pltpu.CompilerParams(vmem_limit_bytes=VMEM_LIMIT),
    )(buf)


class _Sharded:
    def __init__(self, shape, axis):
        self.shape = shape
        self.axis = axis
        self.block = shape[axis] // N_CHIPS
        self.half = shape[1 - axis] // 2

    def _sl(self, along, across):
        return (along, across) if self.axis == 0 else (across, along)

    def block_slice(self, q):
        return self._sl(pl.ds(q * self.block, self.block), pl.ds(0, self.shape[1 - self.axis]))

    def block_half_slice(self, q, c):
        return self._sl(pl.ds(q * self.block, self.block), pl.ds(c * self.half, self.half))

    def half_slice(self, c):
        return self._sl(pl.ds(0, self.shape[self.axis]), pl.ds(c * self.half, self.half))

    def shard_half_slice(self, c):
        return self._sl(pl.ds(0, self.block), pl.ds(c * self.half, self.half))

    def shard_shape(self):
        return self._sl(self.block, self.shape[1 - self.axis])

    def half_shape(self):
        return self._sl(self.shape[self.axis], self.half)

    def block_half_shape(self):
        return self._sl(self.block, self.half)

    def block_in_half_slice(self, q):
        return self._sl(pl.ds(q * self.block, self.block), pl.ds(0, self.half))


def _at(ref, sl):
    return ref.at[sl[0], sl[1]]


class _Copy:
    def __init__(self, src, dst, arrive, dev):
        self.src, self.dst, self.arrive, self.dev = src, dst, arrive, dev


HBM = pl.BlockSpec(memory_space=pltpu.HBM)
SEM = pl.BlockSpec(memory_space=pltpu.SEMAPHORE)
EFFECT = pltpu.SideEffectType.DATAFLOW_SIDE_EFFECTING


def _exchange_start(srcs, land_shapes, plan, after, *, name):
    ns, nl = len(srcs), len(land_shapes)
    n_copies = len(plan([None] * ns, [None] * nl, dry=True))

    def body(*refs):
        src_refs, land_refs = refs[:ns], refs[ns:ns + nl]
        send_sems, recv_sems = refs[ns + nl + 1], refs[ns + nl + 2]
        token = refs[-1]
        for k, cp in enumerate(plan(src_refs, land_refs)):
            pltpu.make_async_remote_copy(
                src_ref=cp.src, dst_ref=cp.dst, send_sem=send_sems.at[k], recv_sem=recv_sems.at[k],
                device_id=cp.dev, device_id_type=MESH).start()
        token[...] = jnp.zeros_like(token)

    sem = pltpu.SemaphoreType.DMA((n_copies,))
    lands = [pltpu.with_memory_space_constraint(lax.empty(shp, dt), pltpu.HBM) for shp, dt in land_shapes]
    srcs = [pltpu.with_memory_space_constraint(a, pltpu.HBM) for a in srcs]
    thru = [pltpu.HBM(a.shape, a.dtype) for a in srcs + lands]
    outs = pl.pallas_call(
        body,
        name=name,
        out_shape=(sem, sem, *thru, jax.ShapeDtypeStruct((8, LANES), F32)),
        in_specs=[HBM] * (ns + nl) + [ANY],
        out_specs=(SEM, SEM, *[HBM] * (ns + nl), pl.BlockSpec(memory_space=pltpu.VMEM)),
        input_output_aliases={i: 2 + i for i in range(ns + nl)},
        compiler_params=pltpu.CompilerParams(has_side_effects=EFFECT),
    )(*srcs, *lands, after)
    return outs[0], outs[1], list(outs[2:2 + ns]), list(outs[2 + ns:2 + ns + nl]), outs[-1]


def _exchange_wait(send_sems, recv_sems, srcs, lands, plan, after, *, name):
    ns, nl = len(srcs), len(lands)

    def body(*refs):
        src_refs, land_refs = refs[:ns], refs[ns:ns + nl]
        send_sems, recv_sems = refs[ns + nl], refs[ns + nl + 1]
        copies = [
            pltpu.make_async_remote_copy(
                src_ref=cp.src, dst_ref=cp.arrive, send_sem=send_sems.at[k], recv_sem=recv_sems.at[k],
                device_id=cp.dev, device_id_type=MESH)
            for k, cp in enumerate(plan(src_refs, land_refs))
        ]
        for cp in copies:
            cp.wait_recv()
        for cp in copies:
            cp.wait_send()

    thru = [pltpu.HBM(a.shape, a.dtype) for a in list(srcs) + list(lands)]
    outs = pl.pallas_call(
        body,
        name=name,
        out_shape=tuple(thru),
        in_specs=[HBM] * (ns + nl) + [SEM, SEM, ANY],
        out_specs=tuple([HBM] * (ns + nl)),
        input_output_aliases={i: i for i in range(ns + nl)},
        compiler_params=pltpu.CompilerParams(has_side_effects=EFFECT),
    )(*srcs, *lands, send_sems, recv_sems, after)
    return list(outs[:ns]), list(outs[ns:])


def _gather_plan(layouts):
    def plan(srcs, lands, dry=False):
        if dry:
            return [None] * (4 * len(layouts))
        x, y, c, p = _place()
        copies = []
        for s, g, lay in zip(srcs, lands, layouts):
            own = _at(g, lay.block_slice(p))
            copies.append(_Copy(s, own, own, (x, y, 1 - c)))
            for qx, qy, q in _other_chips(x, y):
                copies.append(_Copy(_at(s, lay.shard_half_slice(c)), _at(g, lay.block_half_slice(p, c)),
                                    _at(g, lay.block_half_slice(q, c)), (qx, qy, c)))
        return copies

    return plan


def _forward_halves(gathered, layouts, *, name):
    n = len(gathered)

    def body(*refs):
        outs = refs[n:2 * n]
        send_sems, recv_sems = refs[2 * n:]
        x, y, c, _ = _place()
        sends = []
        for i in range(n):
            for j, (qx, qy, q) in enumerate(_other_chips(x, y)):
                landed = _at(outs[i], layouts[i].block_half_slice(q, c))
                cp = pltpu.make_async_remote_copy(
                    src_ref=landed, dst_ref=landed, send_sem=send_sems.at[i, j], recv_sem=recv_sems.at[i, j],
                    device_id=(x, y, 1 - c), device_id_type=MESH)
                cp.start()
                sends.append(cp)
        for i in range(n):
            for j, (qx, qy, q) in enumerate(_other_chips(x, y)):
                theirs = _at(outs[i], layouts[i].block_half_slice(q, 1 - c))
                pltpu.make_async_remote_copy(
                    src_ref=theirs, dst_ref=theirs, send_sem=send_sems.at[i, j], recv_sem=recv_sems.at[i, j],
                    device_id=(x, y, 1 - c), device_id_type=MESH).wait_recv()
        for cp in sends:
            cp.wait_send()

    sem = pltpu.SemaphoreType.DMA
    return pl.pallas_call(
        body,
        name=name,
        out_shape=[jax.ShapeDtypeStruct(lay.shape, BF16) for lay in layouts],
        in_specs=[ANY] * n,
        out_specs=[ANY] * n,
        input_output_aliases={i: i for i in range(n)},
        scratch_shapes=[sem((n, 3)), sem((n, 3))],
    )(*gathered)


def _sibling_halves_plan(layouts):
    def plan(srcs, lands, dry=False):
        if dry:
            return [None] * len(layouts)
        x, y, c, _ = _place()
        return [_Copy(_at(s, lay.half_slice(1 - c)), ra, ra, (x, y, 1 - c)) for s, ra, lay in zip(srcs, lands, layouts)]

    return plan


def _chips_plan(layouts):
    def plan(srcs, lands, dry=False):
        if dry:
            return [None] * (3 * len(layouts))
        x, y, c, _ = _place()
        copies = []
        for s, rb, lay in zip(srcs, lands, layouts):
            for j, (qx, qy, q) in enumerate(_other_chips(x, y)):
                copies.append(_Copy(_at(s, lay.block_in_half_slice(q)), rb.at[j], rb.at[j], (qx, qy, c)))
        return copies

    return plan


def _rs_join_halves(shards, layouts, *, name):
    n = len(shards)

    def body(*refs):
        outs = refs[n:2 * n]
        send_sems, recv_sems = refs[2 * n:]
        x, y, c, _ = _place()
        sends = []
        for i in range(n):
            mine = _at(outs[i], layouts[i].shard_half_slice(c))
            cp = pltpu.make_async_remote_copy(
                src_ref=mine, dst_ref=mine, send_sem=send_sems.at[i], recv_sem=recv_sems.at[i],
                device_id=(x, y, 1 - c), device_id_type=MESH)
            cp.start()
            sends.append(cp)
        for i in range(n):
            theirs = _at(outs[i], layouts[i].shard_half_slice(1 - c))
            pltpu.make_async_remote_copy(
                src_ref=theirs, dst_ref=theirs, send_sem=send_sems.at[i], recv_sem=recv_sems.at[i],
                device_id=(x, y, 1 - c), device_id_type=MESH).wait_recv()
        for cp in sends:
            cp.wait_send()

    sem = pltpu.SemaphoreType.DMA
    return pl.pallas_call(
        body,
        name=name,
        out_shape=[jax.ShapeDtypeStruct(lay.shard_shape(), F32) for lay in layouts],
        in_specs=[ANY] * n,
        out_specs=[ANY] * n,
        input_output_aliases={i: i for i in range(n)},
        scratch_shapes=[sem((n,)), sem((n,))],
    )(*shards)


def _pack(arrays):
    flat = [a.reshape(-1) for a in arrays]
    sizes = [f.shape[0] for f in flat]
    total = sum(sizes)
    rows = _round_up(-(-total // LANES), 8)
    flat.append(jnp.zeros((rows * LANES - total,), F32))
    return jnp.concatenate(flat).reshape(rows, LANES)


def _unpack(buf, shapes):
    flat = buf.reshape(-1)
    out, pos = [], 0
    for shp in shapes:
        n = 1
        for d in shp:
            n *= d
        out.append(flat[pos:pos + n].reshape(shp))
        pos += n
    return out


def _pad_to(a, rows, cols):
    return jnp.pad(a, ((0, rows - a.shape[0]), (0, cols - a.shape[1])))


def kernel(x, mem, g_mix, w_in, conv_a_w, conv_a_b, ln_a_g, ln_a_b, conv_b_w, w_out, g_xattn, g_mem, w_q, w_k, w_v, w_o, g_ffn, w_gate, w_up, conv_f_w, w_down, g_final, loss_target, m_g_mix, m_w_in, m_conv_a_w, m_conv_a_b, m_ln_a_g, m_ln_a_b, m_conv_b_w, m_w_out, m_g_xattn, m_g_mem, m_w_q, m_w_k, m_w_v, m_w_o, m_g_ffn, m_w_gate, m_w_up, m_conv_f_w, m_w_down, m_g_final, v_g_mix, v_w_in, v_conv_a_w, v_conv_a_b, v_ln_a_g, v_ln_a_b, v_conv_b_w, v_w_out, v_g_xattn, v_g_mem, v_w_q, v_w_k, v_w_v, v_w_o, v_g_ffn, v_w_gate, v_w_up, v_conv_f_w, v_w_down, v_g_final):
    T, D = x.shape[1], x.shape[2]
    in_sh = w_in.shape[2]
    cw_sh = conv_a_w.shape[2]
    cw = N_CHIPS * cw_sh
    f_sh = w_gate.shape[2]
    fp = _round_up(f_sh, 256)
    F = N_CHIPS * fp
    rs = w_out.shape[1]
    c_idx = lax.axis_index("c")
    p_idx = 2 * lax.axis_index("x") + lax.axis_index("y")

    big = {
        "w_in": (w_in[0].astype(BF16), _Sharded((D, N_CHIPS * in_sh), 1)),
        "w_out": (w_out[0].astype(BF16), _Sharded((N_CHIPS * rs, D), 0)),
        "w_q": (w_q[0].astype(BF16), _Sharded((D, D), 0)),
        "w_k": (w_k[0].astype(BF16), _Sharded((D, D), 0)),
        "w_v": (w_v[0].astype(BF16), _Sharded((D, D), 0)),
        "w_o": (w_o[0].astype(BF16), _Sharded((D, D), 0)),
        "w_gate": (_pad_to(w_gate[0].astype(BF16), D, fp), _Sharded((D, F), 1)),
        "w_up": (_pad_to(w_up[0].astype(BF16), D, fp), _Sharded((D, F), 1)),
        "w_down": (_pad_to(w_down[0].astype(BF16), fp, D), _Sharded((F, D), 0)),
    }
    names = list(big)
    lay = {k: big[k][1] for k in names}
    c_arr = c_idx.astype(jnp.int32).reshape(1)
    pc_arr = jnp.stack([p_idx, c_idx]).astype(jnp.int32)

    tok = jnp.zeros((8, LANES), F32)

    gather_groups = [["w_in"], ["w_out"], ["w_q", "w_k", "w_v"], ["w_o"], ["w_gate"], ["w_up"], ["w_down"]]
    gathers = []
    for gi, grp in enumerate(gather_groups):
        lays = [lay[k] for k in grp]
        plan = _gather_plan(lays)
        ssem, rsem, srcs, lands, tok = _exchange_start(
            [big[k][0] for k in grp], [(l.shape, BF16) for l in lays], plan, tok, name=f"gather_start_{gi}")
        gathers.append((ssem, rsem, srcs, lands, plan, lays, grp))
    W = {}

    def gathered(gi, after):
        ssem, rsem, srcs, lands, plan, lays, grp = gathers[gi]
        _, lands = _exchange_wait(ssem, rsem, srcs, lands, plan, after, name=f"gather_wait_{gi}")
        W.update(zip(grp, _forward_halves(lands, lays, name=f"gather_forward_{gi}")))

    gathered(0, tok)

    conv_shapes = [(KA_ROWS, cw_sh), (KS_ROWS, cw_sh), (KS_ROWS, fp)]
    conv_pack = _pack([_pad_to(conv_a_w[0], KA_ROWS, cw_sh), _pad_to(conv_b_w[0], KS_ROWS, cw_sh),
                       _pad_to(conv_f_w[0], KS_ROWS, fp)])
    conv_all = _allgather_small(conv_pack, name="allgather_conv", reduce=False)
    per_chip = [_unpack(conv_all[2 * q], conv_shapes) for q in range(N_CHIPS)]
    wa = jnp.concatenate([pc[0] for pc in per_chip], axis=1)
    wb = jnp.concatenate([pc[1] for pc in per_chip], axis=1)
    wf = jnp.concatenate([pc[2] for pc in per_chip], axis=1)

    xs, tgt, mems = x[0], loss_target[0], mem[0]
    g_mem2, g_final2 = g_mem[None, :], g_final[None, :]

    memn, rm = _norm_fwd(mems, g_mem2, name="norm_mem")
    xn1, r1 = _norm_fwd(xs, g_mix, name="norm_mix")
    proj = _mm(xn1, W["w_in"], name="mm_in")
    u3, u1 = _mix_a_fwd(proj, wa, conv_a_b, ln_a_g, ln_a_b, cw=cw, name="mix_a_fwd")
    vb = _mix_b_fwd(proj, wb, cw=cw, sw=cw, name="mix_b_fwd")
    mix = jnp.concatenate([u3, vb], axis=1)
    gathered(1, mix)
    h1 = _mm(mix, W["w_out"], add=xs, name="mm_out")
    xn2, r2 = _norm_fwd(h1, g_xattn, name="norm_xattn")
    gathered(2, xn2)
    q = _mm(xn2, W["w_q"], out_dtype=BF16, name="mm_q")
    k = _mm(memn, W["w_k"], out_dtype=BF16, name="mm_k")
    vm = _mm(memn, W["w_v"], out_dtype=BF16, name="mm_v")
    o = _attn_fwd(q, k, vm, name="attn_fwd")
    gathered(3, o)
    h2 = _mm(o, W["w_o"], add=h1, name="mm_o")
    xn3, r3 = _norm_fwd(h2, g_ffn, name="norm_ffn")
    gathered(4, xn3)
    gp = _mm(xn3, W["w_gate"], name="mm_gate")
    gathered(5, gp)
    up = _mm(xn3, W["w_up"], name="mm_up")
    f = _ffn_act_fwd(gp, up, wf, name="ffn_act_fwd")
    gathered(6, f)
    h3 = _mm(f, W["w_down"], add=h2, name="mm_down")

    G = {}

    def siblings_start(tag, grp, after):
        lays = [lay[k] for k in grp]
        plan = _sibling_halves_plan(lays)
        ssem, rsem, srcs, lands, token = _exchange_start(
            [G[k] for k in grp], [(l.half_shape(), F32) for l in lays], plan, after, name=f"rs_siblings_start_{tag}")
        return (tag, grp, lays, plan, ssem, rsem, srcs, lands), token

    def chips_start(state, after):
        tag, grp, lays, plan, ssem, rsem, srcs, lands = state
        srcs, lands = _exchange_wait(ssem, rsem, srcs, lands, plan, after, name=f"rs_siblings_wait_{tag}")
        sums = [_half_sum(g, ra, c_arr, rows_split=(l.axis == 1), name=f"half_sum_{k}")
                for k, g, ra, l in zip(grp, srcs, lands, lays)]
        plan = _chips_plan(lays)
        ssem, rsem, sums, lands, token = _exchange_start(
            sums, [((N_CHIPS - 1,) + l.block_half_shape(), BF16) for l in lays], plan, sums[-1], name=f"rs_chips_start_{tag}")
        return (tag, grp, lays, plan, ssem, rsem, sums, lands), token

    def chips_finish(state, after):
        tag, grp, lays, plan, ssem, rsem, sums, lands = state
        sums, lands = _exchange_wait(ssem, rsem, sums, lands, plan, after, name=f"rs_chips_wait_{tag}")
        halves = [_sum_chips(cs, rb, pc_arr, l, name=f"sum_chips_{k}") for k, cs, rb, l in zip(grp, sums, lands, lays)]
        return dict(zip(grp, _rs_join_halves(halves, lays, name=f"rs_join_{tag}")))

    loss_rows, dh3, dh3b, dg_final = _loss_head(h3, g_final2, tgt, name="loss_head")
    df = _mm(dh3b, W["w_down"], tb=True, out_dtype=BF16, name="mm_d_f")
    G["w_down"] = _mm(f, dh3b, ta=True, name="mm_dw_down")
    rs_down, tok = siblings_start("down", ["w_down"], tok)
    dgp, dup, dwf = _ffn_act_bwd(gp, up, df, wf, name="ffn_act_bwd")
    G["w_gate"] = _mm(xn3, dgp, ta=True, deps=[tok], name="mm_dw_gate")
    G["w_up"] = _mm(xn3, dup, ta=True, name="mm_dw_up")
    rs_ffn, tok = siblings_start("ffn", ["w_gate", "w_up"], tok)
    rs_down, tok = chips_start(rs_down, tok)
    dxn3 = _mm(dgp, W["w_gate"], tb=True, deps=[tok], name="mm_dxn3_gate")
    dxn3 = _mm(dup, W["w_up"], tb=True, add=dxn3, name="mm_dxn3_up")
    dh2, dh2b, dg_ffn = _norm_bwd(h2, r3, g_ffn, dxn3, dh3, name="norm_ffn_bwd")
    rs_ffn, tok = chips_start(rs_ffn, dh2b)
    do = _mm(dh2b, W["w_o"], tb=True, out_dtype=BF16, deps=[tok], name="mm_d_o")
    G["w_o"] = _mm(o, dh2b, ta=True, name="mm_dw_o")
    dq, dk, dvm = _attn_bwd(q, k, vm, do, name="attn_bwd")
    dkb, dvb = dk.astype(BF16), dvm.astype(BF16)
    G["w_q"] = _mm(xn2, dq, ta=True, name="mm_dw_q")
    G["w_k"] = _mm(memn, dkb, ta=True, name="mm_dw_k")
    G["w_v"] = _mm(memn, dvb, ta=True, name="mm_dw_v")
    rs_att, tok = siblings_start("att", ["w_o", "w_q", "w_k", "w_v"], tok)
    dxn2 = _mm(dq, W["w_q"], tb=True, deps=[tok], name="mm_dxn2")
    dmemn = _mm(dkb, W["w_k"], tb=True, name="mm_dmem_k")
    dmemn = _mm(dvb, W["w_v"], tb=True, add=dmemn, name="mm_dmem_v")
    dg_mem = _norm_bwd(mems, rm, g_mem2, dmemn, None, name="norm_mem_bwd", want_dh=False)
    dh1, dh1b, dg_xattn = _norm_bwd(h1, r2, g_xattn, dxn2, dh2, name="norm_xattn_bwd")
    rs_att, tok = chips_start(rs_att, dh1b)
    dmix = _mm(dh1b, W["w_out"], tb=True, deps=[tok], name="mm_d_mix")
    G["w_out"] = _mm(mix, dh1b, ta=True, name="mm_dw_out")
    rs_out, tok = siblings_start("out", ["w_out"], tok)
    dav, dag, dwa, dba, dlg, dlb = _mix_a_bwd(proj, u1, dmix, wa, ln_a_g, ln_a_b, cw=cw, name="mix_a_bwd")
    dbg, dcg, dbh, dwb = _mix_b_bwd(proj, dmix, wb, cw=cw, sw=cw, name="mix_b_bwd")
    dproj = jnp.concatenate([dav, dag, dbg, dcg, dbh], axis=1)
    rs_out, tok = chips_start(rs_out, dproj)
    G["w_in"] = _mm(xn1, dproj, ta=True, deps=[tok], name="mm_dw_in")
    rs_in, tok = siblings_start("in", ["w_in"], tok)
    dxn1 = _mm(dproj, W["w_in"], tb=True, deps=[tok], name="mm_dxn1")
    dx, _, dg_mix = _norm_bwd(xs, r1, g_mix, dxn1, dh1, name="norm_mix_bwd")
    rs_in, tok = chips_start(rs_in, dx)

    loss_part = jnp.sum(loss_rows).reshape(1, 1)
    small_parts = [dg_mix, dba, dlg, dlb, dg_xattn, dg_mem, dg_ffn, dg_final, dwa, dwb, dwf, loss_part]
    small_shapes = [a.shape for a in small_parts]
    reduced = _allgather_small(_pack(small_parts), name="allreduce_small", reduce=True)
    (sg_mix, sba, slg, slb, sg_xattn, sg_mem, sg_ffn, sg_final, swa, swb, swf, loss_sum) = _unpack(reduced, small_shapes)
    loss = loss_sum.reshape(())
    ga_w = lax.dynamic_slice(swa, (0, p_idx * cw_sh), (KA, cw_sh))
    gb_w = lax.dynamic_slice(swb, (0, p_idx * cw_sh), (KS, cw_sh))
    gf_w = lax.dynamic_slice(swf, (0, p_idx * fp), (KS, f_sh))

    weights = dict(g_mix=g_mix, w_in=w_in, conv_a_w=conv_a_w, conv_a_b=conv_a_b, ln_a_g=ln_a_g, ln_a_b=ln_a_b,
                   conv_b_w=conv_b_w, w_out=w_out, g_xattn=g_xattn, g_mem=g_mem, w_q=w_q, w_k=w_k, w_v=w_v, w_o=w_o,
                   g_ffn=g_ffn, w_gate=w_gate, w_up=w_up, conv_f_w=conv_f_w, w_down=w_down, g_final=g_final)
    m_in = dict(g_mix=m_g_mix, w_in=m_w_in, conv_a_w=m_conv_a_w, conv_a_b=m_conv_a_b, ln_a_g=m_ln_a_g, ln_a_b=m_ln_a_b,
                conv_b_w=m_conv_b_w, w_out=m_w_out, g_xattn=m_g_xattn, g_mem=m_g_mem, w_q=m_w_q, w_k=m_w_k, w_v=m_w_v,
                w_o=m_w_o, g_ffn=m_g_ffn, w_gate=m_w_gate, w_up=m_w_up, conv_f_w=m_conv_f_w, w_down=m_w_down,
                g_final=m_g_final)
    v_in = dict(g_mix=v_g_mix, w_in=v_w_in, conv_a_w=v_conv_a_w, conv_a_b=v_conv_a_b, ln_a_g=v_ln_a_g, ln_a_b=v_ln_a_b,
                conv_b_w=v_conv_b_w, w_out=v_w_out, g_xattn=v_g_xattn, g_mem=v_g_mem, w_q=v_w_q, w_k=v_w_k, w_v=v_w_v,
                w_o=v_w_o, g_ffn=v_g_ffn, w_gate=v_w_gate, w_up=v_w_up, conv_f_w=v_conv_f_w, w_down=v_w_down,
                g_final=v_g_final)
    order = list(weights)
    grads = dict(g_mix=sg_mix, conv_a_w=ga_w, conv_a_b=sba, ln_a_g=slg, ln_a_b=slb, conv_b_w=gb_w, g_xattn=sg_xattn,
                 g_mem=sg_mem, g_ffn=sg_ffn, conv_f_w=gf_w, g_final=sg_final)
    grads = {k: g.reshape(weights[k].shape) for k, g in grads.items()}

    delta, new_m, new_v = {}, {}, {}
    small = [k for k in order if k not in big]
    small_shapes = [weights[k].shape for k in small]
    packed = [_pack([src[k] for k in small]) for src in (weights, grads, m_in, v_in)]
    d_, m_, v_ = _adamw(*packed, name="adamw_small")
    for k, dd, mm_, vv in zip(small, _unpack(d_, small_shapes), _unpack(m_, small_shapes), _unpack(v_, small_shapes)):
        delta[k], new_m[k], new_v[k] = dd, mm_, vv

    unpad = {"w_gate": lambda g: g[:, :f_sh], "w_up": lambda g: g[:, :f_sh], "w_down": lambda g: g[:f_sh, :]}
    after = d_
    for state in (rs_down, rs_ffn, rs_att, rs_out, rs_in):
        for k, g in chips_finish(state, after).items():
            g = unpad[k](g) if k in unpad else g
            shp = weights[k].shape
            grads[k] = g.reshape(shp)
            d_, m_, v_ = _adamw(weights[k][0], g, m_in[k][0], v_in[k][0], name=f"adamw_{k}")
            delta[k], new_m[k], new_v[k] = d_.reshape(shp), m_.reshape(shp), v_.reshape(shp)
            after = d_

    return (loss, dx[None], *[grads[k] for k in order], *[delta[k] for k in order],
            *[new_m[k] for k in order], *[new_v[k] for k in order])
```

```python
import functools

import jax
import jax.numpy as jnp
from jax import lax
from jax.experimental import pallas as pl
from jax.experimental.pallas import tpu as pltpu

F32 = jnp.float32
BF16 = jnp.bfloat16
EPS = 1e-6
N_HEADS = 4
GROUP = 128
KA = 31
KS = 3
KA_ROWS = 32
KS_ROWS = 8
HALO_A = 32
HALO_S = 8
N_CHIPS = 4
N_DEV = 8
LANES = 128
VMEM_LIMIT = 56 * 1024 * 1024
ADAMW_BLOCK_BYTES = 1 << 20
MESH = pl.DeviceIdType.MESH
ANY = pl.BlockSpec(memory_space=pl.ANY)

ADAM_LR = 0.001
ADAM_B1 = 0.9
ADAM_B2 = 0.999
ADAM_EPS = 1e-08
ADAM_WD = 0.01
ADAM_STEP = 10


def _pick(dim, pref, mult=LANES):
    if dim <= pref:
        return dim
    t = (pref // mult) * mult
    while t >= mult:
        if dim % t == 0:
            return t
        t -= mult
    return dim


def _round_up(n, m):
    return ((n + m - 1) // m) * m


def _params(*sem):
    return pltpu.CompilerParams(dimension_semantics=sem, vmem_limit_bytes=VMEM_LIMIT)


def _sigmoid(x):
    return jax.nn.sigmoid(x)


def _mm(a, b, *, name, ta=False, tb=False, out_dtype=F32, add=None, deps=(), tm=1024, tn=512, tk=4096):
    if ta:
        K, M = a.shape
    else:
        M, K = a.shape
    if tb:
        N, K2 = b.shape
    else:
        K2, N = b.shape
    assert K == K2, (a.shape, b.shape)
    tm, tn, tk = _pick(M, tm), _pick(N, tn), _pick(K, tk)
    nk = K // tk
    a_spec = pl.BlockSpec((tk, tm), lambda i, j, k: (k, i)) if ta else pl.BlockSpec((tm, tk), lambda i, j, k: (i, k))
    b_spec = pl.BlockSpec((tn, tk), lambda i, j, k: (j, k)) if tb else pl.BlockSpec((tk, tn), lambda i, j, k: (k, j))
    o_spec = pl.BlockSpec((tm, tn), lambda i, j, k: (i, j))
    dims = (((0,) if ta else (1,), (1,) if tb else (0,)), ((), ()))
    has_add = add is not None

    def body(*refs):
        a_ref, b_ref = refs[:2]
        add_ref = refs[2] if has_add else None
        o_ref = refs[(3 if has_add else 2) + len(deps)]
        acc_ref = refs[-1] if nk > 1 else None
        k = pl.program_id(2)
        part = lax.dot_general(a_ref[...], b_ref[...], dims, preferred_element_type=F32)

        def finish(r):
            if add_ref is not None:
                r = add_ref[...] + r
            o_ref[...] = r.astype(out_dtype)

        if nk == 1:
            finish(part)
        else:
            @pl.when(k == 0)
            def _():
                acc_ref[...] = part

            @pl.when(jnp.logical_and(k > 0, k < nk - 1))
            def _():
                acc_ref[...] += part

            @pl.when(k == nk - 1)
            def _():
                finish(acc_ref[...] + part)

    in_specs = [a_spec, b_spec] + ([o_spec] if has_add else []) + [ANY] * len(deps)
    args = (a, b) + ((add,) if has_add else ()) + tuple(deps)
    return pl.pallas_call(
        body,
        name=name,
        out_shape=jax.ShapeDtypeStruct((M, N), out_dtype),
        grid=(M // tm, N // tn, nk),
        in_specs=in_specs,
        out_specs=o_spec,
        scratch_shapes=[pltpu.VMEM((tm, tn), F32)] if nk > 1 else [],
        compiler_params=_params("parallel", "parallel", "arbitrary"),
    )(*args)


def _norm_fwd(h, g, *, name):
    T, D = h.shape
    R = _pick(T, 256, 8)

    def body(h_ref, g_ref, xn_ref, r_ref):
        x = h_ref[...]
        r = lax.rsqrt(jnp.mean(x * x, axis=-1, keepdims=True) + EPS)
        xn_ref[...] = ((x * r) * g_ref[...]).astype(BF16)
        r_ref[...] = r

    return pl.pallas_call(
        body,
        name=name,
        out_shape=(jax.ShapeDtypeStruct((T, D), BF16), jax.ShapeDtypeStruct((T, 1), F32)),
        grid=(T // R,),
        in_specs=[pl.BlockSpec((R, D), lambda i: (i, 0)), pl.BlockSpec((1, D), lambda i: (0, 0))],
        out_specs=(pl.BlockSpec((R, D), lambda i: (i, 0)), pl.BlockSpec((R, 1), lambda i: (i, 0))),
        compiler_params=_params("parallel"),
    )(h, g)


def _norm_bwd(h, r, g, dxn, dres, *, name, want_dh=True):
    T, D = h.shape
    R = _pick(T, 128, 8)
    has_res = dres is not None

    def body(*refs):
        h_ref, r_ref, g_ref, dxn_ref = refs[:4]
        pos = 4
        dres_ref = None
        if has_res:
            dres_ref = refs[pos]
            pos += 1
        if want_dh:
            dh_ref, dhb_ref, dg_ref = refs[pos:pos + 3]
        else:
            dg_ref = refs[pos]
        i = pl.program_id(0)
        rr = r_ref[...]
        hn = h_ref[...] * rr
        d = dxn_ref[...].astype(F32)
        gd = d * g_ref[...]
        part = jnp.sum(d * hn, axis=0, keepdims=True)

        @pl.when(i == 0)
        def _():
            dg_ref[...] = part

        @pl.when(i > 0)
        def _():
            dg_ref[...] += part

        if want_dh:
            dh = rr * (gd - hn * jnp.mean(gd * hn, axis=-1, keepdims=True))
            if dres_ref is not None:
                dh = dres_ref[...] + dh
            dh_ref[...] = dh
            dhb_ref[...] = dh.astype(BF16)

    row = pl.BlockSpec((R, D), lambda i: (i, 0))
    vec = pl.BlockSpec((1, D), lambda i: (0, 0))
    in_specs = [row, pl.BlockSpec((R, 1), lambda i: (i, 0)), vec, row] + ([row] if has_res else [])
    args = (h, r, g, dxn) + ((dres,) if has_res else ())
    if want_dh:
        out_shape = (jax.ShapeDtypeStruct((T, D), F32), jax.ShapeDtypeStruct((T, D), BF16), jax.ShapeDtypeStruct((1, D), F32))
        out_specs = (row, row, vec)
    else:
        out_shape = jax.ShapeDtypeStruct((1, D), F32)
        out_specs = vec
    return pl.pallas_call(
        body, name=name, out_shape=out_shape, grid=(T // R,), in_specs=in_specs, out_specs=out_specs,
        compiler_params=_params("arbitrary"),
    )(*args)


def _loss_head(h, g, tgt, *, name):
    T, D = h.shape
    R = _pick(T, 128, 8)

    def body(h_ref, g_ref, t_ref, loss_ref, dh_ref, dhb_ref, dg_ref):
        i = pl.program_id(0)
        x = h_ref[...]
        gg = g_ref[...]
        r = lax.rsqrt(jnp.mean(x * x, axis=-1, keepdims=True) + EPS)
        hn = x * r
        e = hn * gg - t_ref[...]
        loss_ref[...] = 0.5 * jnp.mean(e * e, axis=-1, keepdims=True)
        dy = e * (1.0 / D)
        gd = dy * gg
        dh = r * (gd - hn * jnp.mean(gd * hn, axis=-1, keepdims=True))
        dh_ref[...] = dh
        dhb_ref[...] = dh.astype(BF16)
        part = jnp.sum(dy * hn, axis=0, keepdims=True)

        @pl.when(i == 0)
        def _():
            dg_ref[...] = part

        @pl.when(i > 0)
        def _():
            dg_ref[...] += part

    row = pl.BlockSpec((R, D), lambda i: (i, 0))
    vec = pl.BlockSpec((1, D), lambda i: (0, 0))
    return pl.pallas_call(
        body,
        name=name,
        out_shape=(jax.ShapeDtypeStruct((T, 1), F32), jax.ShapeDtypeStruct((T, D), F32),
                   jax.ShapeDtypeStruct((T, D), BF16), jax.ShapeDtypeStruct((1, D), F32)),
        grid=(T // R,),
        in_specs=[row, vec, row],
        out_specs=(pl.BlockSpec((R, 1), lambda i: (i, 0)), row, row, vec),
        compiler_params=_params("arbitrary"),
    )(h, g, tgt)


_NT = (((1,), (1,)), ((), ()))
_TN = (((0,), (0,)), ((), ()))
_NN = (((1,), (0,)), ((), ()))


def _softmax_rows(s):
    m = jnp.max(s, axis=-1, keepdims=True)
    e = jnp.exp(s - m)
    return e / jnp.sum(e, axis=-1, keepdims=True)


def _attn_fwd(q, k, v, *, name):
    T, D = q.shape
    ML = k.shape[0]
    dh = D // N_HEADS
    scale = dh ** -0.5
    R = _pick(T, 512, 16)

    def body(q_ref, k_ref, v_ref, o_ref):
        s = lax.dot_general(q_ref[...], k_ref[...], _NT, preferred_element_type=F32) * scale
        p = _softmax_rows(s)
        o_ref[...] = lax.dot_general(p.astype(BF16), v_ref[...], _NN, preferred_element_type=F32).astype(BF16)

    qs = pl.BlockSpec((R, dh), lambda i, h: (i, h))
    ks = pl.BlockSpec((ML, dh), lambda i, h: (0, h))
    return pl.pallas_call(
        body, name=name, out_shape=jax.ShapeDtypeStruct((T, D), BF16), grid=(T // R, N_HEADS),
        in_specs=[qs, ks, ks], out_specs=qs, compiler_params=_params("parallel", "parallel"),
    )(q, k, v)


def _attn_bwd(q, k, v, do, *, name):
    T, D = q.shape
    ML = k.shape[0]
    dh = D // N_HEADS
    scale = dh ** -0.5
    R = _pick(T, 512, 16)

    def body(q_ref, k_ref, v_ref, do_ref, dq_ref, dk_ref, dv_ref):
        i = pl.program_id(1)
        qq, kk, vv, dd = q_ref[...], k_ref[...], v_ref[...], do_ref[...]
        s = lax.dot_general(qq, kk, _NT, preferred_element_type=F32) * scale
        p = _softmax_rows(s)
        dp = lax.dot_general(dd, vv, _NT, preferred_element_type=F32)
        dv_part = lax.dot_general(p.astype(BF16), dd, _TN, preferred_element_type=F32)
        ds = (p * (dp - jnp.sum(p * dp, axis=-1, keepdims=True)) * scale).astype(BF16)
        dq_ref[...] = lax.dot_general(ds, kk, _NN, preferred_element_type=F32).astype(BF16)
        dk_part = lax.dot_general(ds, qq, _TN, preferred_element_type=F32)

        @pl.when(i == 0)
        def _():
            dk_ref[...] = dk_part
            dv_ref[...] = dv_part

        @pl.when(i > 0)
        def _():
            dk_ref[...] += dk_part
            dv_ref[...] += dv_part

    qs = pl.BlockSpec((R, dh), lambda h, i: (i, h))
    ks = pl.BlockSpec((ML, dh), lambda h, i: (0, h))
    return pl.pallas_call(
        body,
        name=name,
        out_shape=(jax.ShapeDtypeStruct((T, D), BF16), jax.ShapeDtypeStruct((ML, D), F32), jax.ShapeDtypeStruct((ML, D), F32)),
        grid=(N_HEADS, T // R),
        in_specs=[qs, ks, ks, qs],
        out_specs=(qs, ks, ks),
        compiler_params=_params("parallel", "arbitrary"),
    )(q, k, v, do)


def _prev_halo(R, halo, CT, col):
    per = R // halo
    return lambda i, j: (jnp.maximum(i * per - 1, 0), col(j))


def _next_halo(R, halo, T, col):
    per = R // halo
    last = T // halo - 1
    return lambda i, j: (jnp.minimum((i + 1) * per, last), col(j))


def _swap(f):
    return lambda j, i: f(i, j)


def _ffn_act_fwd(gp, up, wf, *, name):
    T, F = gp.shape
    R = _pick(T, 256, 8)
    CT = _pick(F, 1024)
    H = HALO_S

    def body(g_ref, gh_ref, u_ref, w_ref, f_ref, ext):
        i = pl.program_id(0)
        ext[pl.ds(0, H), :] = jnp.where(i > 0, gh_ref[...], 0.0)
        ext[pl.ds(H, R), :] = g_ref[...]
        w = w_ref[...]
        g = ext[pl.ds(H - 2, R), :] * w[0:1, :]
        g = g + ext[pl.ds(H - 1, R), :] * w[1:2, :]
        g = g + ext[pl.ds(H, R), :] * w[2:3, :]
        f_ref[...] = (g * _sigmoid(g) * u_ref[...]).astype(BF16)

    main = pl.BlockSpec((R, CT), lambda i, j: (i, j))
    return pl.pallas_call(
        body,
        name=name,
        out_shape=jax.ShapeDtypeStruct((T, F), BF16),
        grid=(T // R, F // CT),
        in_specs=[main, pl.BlockSpec((H, CT), _prev_halo(R, H, CT, lambda j: j)), main,
                  pl.BlockSpec((KS_ROWS, CT), lambda i, j: (0, j))],
        out_specs=main,
        scratch_shapes=[pltpu.VMEM((R + H, CT), F32)],
        compiler_params=_params("parallel", "parallel"),
    )(gp, gp, up, wf)


def _ffn_act_bwd(gp, up, df, wf, *, name):
    T, F = gp.shape
    R = _pick(T, 256, 16)
    CT = _pick(F, 1024)
    H = HALO_S
    HB = 16
    n_t = T // R

    def body(g_ref, gp_ref, gn_ref, u_ref, un_ref, d_ref, dn_ref, w_ref, dg_out, du_out, dw_ref, ext, dge):
        i = pl.program_id(1)
        last = i == n_t - 1
        ext[pl.ds(0, H), :] = jnp.where(i > 0, gp_ref[...], 0.0)
        ext[pl.ds(H, R), :] = g_ref[...]
        ext[pl.ds(H + R, H), :] = gn_ref[...]
        w = w_ref[...]
        w0, w1, w2 = w[0:1, :], w[1:2, :], w[2:3, :]

        def conv(start, n):
            g = ext[pl.ds(start + H - 2, n), :] * w0
            g = g + ext[pl.ds(start + H - 1, n), :] * w1
            return g + ext[pl.ds(start + H, n), :] * w2

        def dact(g, u, d):
            sg = _sigmoid(g)
            return d * u * (sg * (1.0 + g * (1.0 - sg))), d * (g * sg)

        d_main = d_ref[...].astype(F32)
        dg_main, du_main = dact(conv(0, R), u_ref[...], d_main)
        du_out[...] = du_main.astype(BF16)
        d_next = jnp.where(last, 0.0, dn_ref[pl.ds(0, H), :].astype(F32))
        dg_next, _ = dact(conv(R, H), un_ref[...], d_next)
        dg_next = jnp.where(last, 0.0, dg_next)
        dge[pl.ds(0, R), :] = dg_main
        dge[pl.ds(R, H), :] = dg_next
        dgp = dge[pl.ds(0, R), :] * w2 + dge[pl.ds(1, R), :] * w1 + dge[pl.ds(2, R), :] * w0
        dg_out[...] = dgp.astype(BF16)
        rows = [jnp.sum(dg_main * ext[pl.ds(H - 2 + k, R), :], axis=0, keepdims=True) for k in range(KS)]
        rows.append(jnp.zeros((KS_ROWS - KS, CT), F32))
        _acc_rows_block(dw_ref, i, rows)

    col = lambda j: j
    main = pl.BlockSpec((R, CT), lambda j, i: (i, j))
    prev8 = pl.BlockSpec((H, CT), _swap(_prev_halo(R, H, CT, col)))
    next8 = pl.BlockSpec((H, CT), _swap(_next_halo(R, H, T, col)))
    next16 = pl.BlockSpec((HB, CT), _swap(_next_halo(R, HB, T, col)))
    wspec = pl.BlockSpec((KS_ROWS, CT), lambda j, i: (0, j))
    return pl.pallas_call(
        body,
        name=name,
        out_shape=(jax.ShapeDtypeStruct((T, F), BF16), jax.ShapeDtypeStruct((T, F), BF16), jax.ShapeDtypeStruct((KS_ROWS, F), F32)),
        grid=(F // CT, n_t),
        in_specs=[main, prev8, next8, main, next8, main, next16, wspec],
        out_specs=(main, main, wspec),
        scratch_shapes=[pltpu.VMEM((R + 2 * H, CT), F32), pltpu.VMEM((R + H, CT), F32)],
        compiler_params=_params("parallel", "arbitrary"),
    )(gp, gp, gp, up, up, df, df, wf)


def _acc_rows_block(ref, i, rows):
    *singles, pad = rows

    @pl.when(i == 0)
    def _():
        for k, row in enumerate(singles):
            ref[pl.ds(k, 1), :] = row
        ref[pl.ds(len(singles), pad.shape[0]), :] = pad

    @pl.when(i > 0)
    def _():
        for k, row in enumerate(singles):
            ref[pl.ds(k, 1), :] += row


def _group_stats(x):
    mu = jnp.mean(x, axis=-1, keepdims=True)
    xc = x - mu
    var = jnp.mean(xc * xc, axis=-1, keepdims=True)
    return xc, lax.rsqrt(var + EPS)


def _mix_a_fwd(proj, wa, ba, lg, lb, *, cw, name):
    T = proj.shape[0]
    R = _pick(T, 128, HALO_A)
    CT = _pick(cw, 256)
    nc = cw // CT
    H = HALO_A

    def body(av_ref, ag_ref, avh_ref, agh_ref, w_ref, b_ref, lg_ref, lb_ref, u3_ref, u1_ref, ext):
        i = pl.program_id(0)
        ext[pl.ds(0, H), :] = jnp.where(i > 0, avh_ref[...] * _sigmoid(agh_ref[...]), 0.0)
        ext[pl.ds(H, R), :] = av_ref[...] * _sigmoid(ag_ref[...])
        w = w_ref[...]
        acc = ext[pl.ds(H - (KA - 1), R), :] * w[0:1, :]
        for k in range(1, KA):
            acc = acc + ext[pl.ds(H - (KA - 1) + k, R), :] * w[k:k + 1, :]
        u1 = acc + b_ref[...]
        u1_ref[...] = u1
        for gi in range(CT // GROUP):
            sl = slice(gi * GROUP, (gi + 1) * GROUP)
            xc, rstd = _group_stats(u1[:, sl])
            u2 = (xc * rstd) * lg_ref[:, sl] + lb_ref[:, sl]
            u3_ref[:, sl] = (u2 * _sigmoid(u2)).astype(BF16)

    main_v = pl.BlockSpec((R, CT), lambda i, j: (i, j))
    main_g = pl.BlockSpec((R, CT), lambda i, j: (i, j + nc))
    halo_v = pl.BlockSpec((H, CT), _prev_halo(R, H, CT, lambda j: j))
    halo_g = pl.BlockSpec((H, CT), _prev_halo(R, H, CT, lambda j: j + nc))
    vec = pl.BlockSpec((1, CT), lambda i, j: (0, j))
    out = pl.BlockSpec((R, CT), lambda i, j: (i, j))
    return pl.pallas_call(
        body,
        name=name,
        out_shape=(jax.ShapeDtypeStruct((T, cw), BF16), jax.ShapeDtypeStruct((T, cw), F32)),
        grid=(T // R, nc),
        in_specs=[main_v, main_g, halo_v, halo_g, pl.BlockSpec((KA_ROWS, CT), lambda i, j: (0, j)), vec, vec, vec],
        out_specs=(out, out),
        scratch_shapes=[pltpu.VMEM((R + H, CT), F32)],
        compiler_params=_params("parallel", "parallel"),
    )(proj, proj, proj, proj, wa, ba, lg, lb)


def _mix_a_bwd(proj, u1, dmix, wa, lg, lb, *, cw, name):
    T = proj.shape[0]
    R = _pick(T, 128, HALO_A)
    CT = _pick(cw, 256)
    nc = cw // CT
    H = HALO_A
    n_t = T // R
    NG = CT // GROUP

    def body(av_ref, ag_ref, avh_ref, agh_ref, u1_ref, u1n_ref, d3_ref, d3n_ref, w_ref, lg_ref, lb_ref,
             dav_ref, dag_ref, dw_ref, db_ref, dlg_ref, dlb_ref, ext, d1):
        i = pl.program_id(1)
        last = i == n_t - 1

        def ln_bwd(u1, d3, sl):
            xc, rstd = _group_stats(u1)
            xh = xc * rstd
            g = lg_ref[:, sl]
            u2 = xh * g + lb_ref[:, sl]
            sg = _sigmoid(u2)
            du2 = d3 * (sg * (1.0 + u2 * (1.0 - sg)))
            dxh = du2 * g
            du1 = rstd * (dxh - jnp.mean(dxh, axis=-1, keepdims=True) - xh * jnp.mean(dxh * xh, axis=-1, keepdims=True))
            return du1, du2 * xh, du2

        dlg_parts, dlb_parts = [], []
        for gi in range(NG):
            sl = slice(gi * GROUP, (gi + 1) * GROUP)
            du1, dlg, dlb = ln_bwd(u1_ref[:, sl], d3_ref[:, sl], sl)
            d1[pl.ds(0, R), sl] = du1
            dlg_parts.append(jnp.sum(dlg, axis=0, keepdims=True))
            dlb_parts.append(jnp.sum(dlb, axis=0, keepdims=True))
            du1n, _, _ = ln_bwd(u1n_ref[:, sl], jnp.where(last, 0.0, d3n_ref[:, sl]), sl)
            d1[pl.ds(R, H), sl] = jnp.where(last, 0.0, du1n)
        dlg_row = jnp.concatenate(dlg_parts, axis=1) if NG > 1 else dlg_parts[0]
        dlb_row = jnp.concatenate(dlb_parts, axis=1) if NG > 1 else dlb_parts[0]
        d1_main = d1[pl.ds(0, R), :]
        db_row = jnp.sum(d1_main, axis=0, keepdims=True)

        @pl.when(i == 0)
        def _():
            dlg_ref[...] = dlg_row
            dlb_ref[...] = dlb_row
            db_ref[...] = db_row

        @pl.when(i > 0)
        def _():
            dlg_ref[...] += dlg_row
            dlb_ref[...] += dlb_row
            db_ref[...] += db_row

        w = w_ref[...]
        du0 = d1[pl.ds(KA - 1, R), :] * w[0:1, :]
        for k in range(1, KA):
            du0 = du0 + d1[pl.ds(KA - 1 - k, R), :] * w[k:k + 1, :]
        av = av_ref[...]
        sg = _sigmoid(ag_ref[...])
        dav_ref[...] = (du0 * sg).astype(BF16)
        dag_ref[...] = (du0 * av * sg * (1.0 - sg)).astype(BF16)

        ext[pl.ds(0, H), :] = jnp.where(i > 0, avh_ref[...] * _sigmoid(agh_ref[...]), 0.0)
        ext[pl.ds(H, R), :] = av * sg
        rows = [jnp.sum(d1_main * ext[pl.ds(H - (KA - 1) + k, R), :], axis=0, keepdims=True) for k in range(KA)]
        rows.append(jnp.zeros((KA_ROWS - KA, CT), F32))
        _acc_rows_block(dw_ref, i, rows)

    cv = lambda j: j
    cg = lambda j: j + nc
    main_v = pl.BlockSpec((R, CT), lambda j, i: (i, j))
    main_g = pl.BlockSpec((R, CT), lambda j, i: (i, j + nc))
    prev_v = pl.BlockSpec((H, CT), _swap(_prev_halo(R, H, CT, cv)))
    prev_g = pl.BlockSpec((H, CT), _swap(_prev_halo(R, H, CT, cg)))
    nxt = pl.BlockSpec((H, CT), _swap(_next_halo(R, H, T, cv)))
    wspec = pl.BlockSpec((KA_ROWS, CT), lambda j, i: (0, j))
    vec = pl.BlockSpec((1, CT), lambda j, i: (0, j))
    vshape = jax.ShapeDtypeStruct((1, cw), F32)
    return pl.pallas_call(
        body,
        name=name,
        out_shape=(jax.ShapeDtypeStruct((T, cw), BF16), jax.ShapeDtypeStruct((T, cw), BF16),
                   jax.ShapeDtypeStruct((KA_ROWS, cw), F32), vshape, vshape, vshape),
        grid=(nc, n_t),
        in_specs=[main_v, main_g, prev_v, prev_g, main_v, nxt, main_v, nxt, wspec, vec, vec],
        out_specs=(main_v, main_v, wspec, vec, vec, vec),
        scratch_shapes=[pltpu.VMEM((R + H, CT), F32), pltpu.VMEM((R + H, CT), F32)],
        compiler_params=_params("parallel", "arbitrary"),
    )(proj, proj, proj, proj, u1, u1, dmix, dmix, wa, lg, lb)


def _mix_b_fwd(proj, wb, *, cw, sw, name):
    T = proj.shape[0]
    R = _pick(T, 256, 8)
    CT = _pick(sw, 512)
    nb, nc, nh = (2 * cw) // CT, (2 * cw + sw) // CT, (2 * cw + 2 * sw) // CT
    H = HALO_S

    def body(b_ref, c_ref, h_ref, ch_ref, hh_ref, w_ref, v_ref, ext):
        i = pl.program_id(0)
        ext[pl.ds(0, H), :] = jnp.where(i > 0, ch_ref[...] * hh_ref[...], 0.0)
        ext[pl.ds(H, R), :] = c_ref[...] * h_ref[...]
        w = w_ref[...]
        zc = ext[pl.ds(H - 2, R), :] * w[0:1, :]
        zc = zc + ext[pl.ds(H - 1, R), :] * w[1:2, :]
        zc = zc + ext[pl.ds(H, R), :] * w[2:3, :]
        v_ref[...] = (b_ref[...] * zc).astype(BF16)

    def main(off):
        return pl.BlockSpec((R, CT), lambda i, j: (i, j + off))

    def prev(off):
        return pl.BlockSpec((H, CT), _prev_halo(R, H, CT, lambda j: j + off))

    return pl.pallas_call(
        body,
        name=name,
        out_shape=jax.ShapeDtypeStruct((T, sw), BF16),
        grid=(T // R, sw // CT),
        in_specs=[main(nb), main(nc), main(nh), prev(nc), prev(nh), pl.BlockSpec((KS_ROWS, CT), lambda i, j: (0, j))],
        out_specs=pl.BlockSpec((R, CT), lambda i, j: (i, j)),
        scratch_shapes=[pltpu.VMEM((R + H, CT), F32)],
        compiler_params=_params("parallel", "parallel"),
    )(proj, proj, proj, proj, proj, wb)


def _mix_b_bwd(proj, dmix, wb, *, cw, sw, name):
    T = proj.shape[0]
    R = _pick(T, 256, 8)
    CT = _pick(sw, 512)
    nb, nc, nh = (2 * cw) // CT, (2 * cw + sw) // CT, (2 * cw + 2 * sw) // CT
    nd = cw // CT
    H = HALO_S
    n_t = T // R

    def body(b_ref, bn_ref, c_ref, cp_ref, cn_ref, h_ref, hp_ref, hn_ref, d_ref, dn_ref, w_ref,
             db_ref, dc_ref, dhh_ref, dw_ref, ext, dze):
        i = pl.program_id(1)
        last = i == n_t - 1
        cc, hh = c_ref[...], h_ref[...]
        ext[pl.ds(0, H), :] = jnp.where(i > 0, cp_ref[...] * hp_ref[...], 0.0)
        ext[pl.ds(H, R), :] = cc * hh
        ext[pl.ds(H + R, H), :] = cn_ref[...] * hn_ref[...]
        w = w_ref[...]
        w0, w1, w2 = w[0:1, :], w[1:2, :], w[2:3, :]

        def conv(start, n):
            z = ext[pl.ds(start + H - 2, n), :] * w0
            z = z + ext[pl.ds(start + H - 1, n), :] * w1
            return z + ext[pl.ds(start + H, n), :] * w2

        d_main = d_ref[...]
        db_ref[...] = (d_main * conv(0, R)).astype(BF16)
        dzc_main = d_main * b_ref[...]
        dze[pl.ds(0, R), :] = dzc_main
        dze[pl.ds(R, H), :] = jnp.where(last, 0.0, dn_ref[...] * bn_ref[...])
        dz = dze[pl.ds(0, R), :] * w2 + dze[pl.ds(1, R), :] * w1 + dze[pl.ds(2, R), :] * w0
        dc_ref[...] = (dz * hh).astype(BF16)
        dhh_ref[...] = (dz * cc).astype(BF16)
        rows = [jnp.sum(dzc_main * ext[pl.ds(H - 2 + k, R), :], axis=0, keepdims=True) for k in range(KS)]
        rows.append(jnp.zeros((KS_ROWS - KS, CT), F32))
        _acc_rows_block(dw_ref, i, rows)

    def main(off):
        return pl.BlockSpec((R, CT), lambda j, i: (i, j + off))

    def prev(off):
        return pl.BlockSpec((H, CT), _swap(_prev_halo(R, H, CT, lambda j: j + off)))

    def nxt(off):
        return pl.BlockSpec((H, CT), _swap(_next_halo(R, H, T, lambda j: j + off)))

    out = pl.BlockSpec((R, CT), lambda j, i: (i, j))
    wspec = pl.BlockSpec((KS_ROWS, CT), lambda j, i: (0, j))
    act = jax.ShapeDtypeStruct((T, sw), BF16)
    return pl.pallas_call(
        body,
        name=name,
        out_shape=(act, act, act, jax.ShapeDtypeStruct((KS_ROWS, sw), F32)),
        grid=(sw // CT, n_t),
        in_specs=[main(nb), nxt(nb), main(nc), prev(nc), nxt(nc), main(nh), prev(nh), nxt(nh), main(nd), nxt(nd), wspec],
        out_specs=(out, out, out, wspec),
        scratch_shapes=[pltpu.VMEM((R + 2 * H, CT), F32), pltpu.VMEM((R + H, CT), F32)],
        compiler_params=_params("parallel", "arbitrary"),
    )(proj, proj, proj, proj, proj, proj, proj, proj, dmix, dmix, wb)


def _adamw(w, g, m, v, *, name, emit_grad=False):
    Rr, Cc = w.shape
    R = _pick(Rr, max(8, ADAMW_BLOCK_BYTES // (4 * Cc)), 8)

    def body(w_ref, g_ref, m_ref, v_ref, d_ref, mo_ref, vo_ref, *g_out):
        g = g_ref[...]
        m2 = ADAM_B1 * m_ref[...] + (1.0 - ADAM_B1) * g
        v2 = ADAM_B2 * v_ref[...] + (1.0 - ADAM_B2) * (g * g)
        m_hat = m2 / (1.0 - ADAM_B1 ** ADAM_STEP)
        v_hat = v2 / (1.0 - ADAM_B2 ** ADAM_STEP)
        d_ref[...] = -ADAM_LR * (m_hat / (jnp.sqrt(v_hat) + ADAM_EPS) + ADAM_WD * w_ref[...])
        mo_ref[...] = m2
        vo_ref[...] = v2
        if emit_grad:
            g_out[0][...] = g

    spec = pl.BlockSpec((R, Cc), lambda i: (i, 0))
    shp = jax.ShapeDtypeStruct((Rr, Cc), F32)
    n_out = 4 if emit_grad else 3
    return pl.pallas_call(
        body, name=name, out_shape=(shp,) * n_out, grid=(Rr // R,), in_specs=[spec] * 4, out_specs=(spec,) * n_out,
        compiler_params=_params("parallel"),
    )(w, g, m, v)


def _half_sum(g, ra, c, *, rows_split, name):
    Rr, Cc = ra.shape
    tr, tc = _pick(Rr, 512, 16), _pick(Cc, 2048)
    nr, ncol = Rr // tr, Cc // tc
    if rows_split:
        g_map = lambda i, j, c_ref: (i + c_ref[0] * nr, j)
    else:
        g_map = lambda i, j, c_ref: (i, j + c_ref[0] * ncol)

    def body(c_ref, g_ref, ra_ref, o_ref):
        o_ref[...] = (g_ref[...] + ra_ref[...]).astype(BF16)

    spec = pl.BlockSpec((tr, tc), lambda i, j, c_ref: (i, j))
    return pl.pallas_call(
        body,
        name=name,
        out_shape=jax.ShapeDtypeStruct((Rr, Cc), BF16),
        grid_spec=pltpu.PrefetchScalarGridSpec(
            num_scalar_prefetch=1, grid=(nr, ncol), in_specs=[pl.BlockSpec((tr, tc), g_map), spec], out_specs=spec),
        compiler_params=_params("parallel", "parallel"),
    )(c, g, ra)


def _sum_chips(cs, rb, pc, lay, *, name):
    _, Rr, Cc = rb.shape
    tr, tc = _pick(Rr, 512, 16), _pick(Cc, 2048)
    nr, ncol = Rr // tr, Cc // tc
    if lay.axis == 0:
        own_map = lambda i, j, s: (i + s[0] * nr, j)
        out_map = lambda i, j, s: (i, j + s[1] * ncol)
    else:
        own_map = lambda i, j, s: (i, j + s[0] * ncol)
        out_map = lambda i, j, s: (i + s[1] * nr, j)

    def body(s_ref, own_ref, rb_ref, o_ref):
        acc = own_ref[...].astype(F32)
        for j in range(N_CHIPS - 1):
            acc = acc + rb_ref[j].astype(F32)
        o_ref[...] = acc

    return pl.pallas_call(
        body,
        name=name,
        out_shape=jax.ShapeDtypeStruct(lay.shard_shape(), F32),
        grid_spec=pltpu.PrefetchScalarGridSpec(
            num_scalar_prefetch=1,
            grid=(nr, ncol),
            in_specs=[pl.BlockSpec((tr, tc), own_map), pl.BlockSpec((N_CHIPS - 1, tr, tc), lambda i, j, s: (0, i, j))],
            out_specs=pl.BlockSpec((tr, tc), out_map)),
        compiler_params=_params("parallel", "parallel"),
    )(pc, cs, rb)


def _place():
    x, y, c = lax.axis_index("x"), lax.axis_index("y"), lax.axis_index("c")
    return x, y, c, 2 * x + y


def _other_chips(x, y):
    return [(1 - x, y, 2 * (1 - x) + y), (x, 1 - y, 2 * x + (1 - y)), (1 - x, 1 - y, 2 * (1 - x) + (1 - y))]


def _allgather_small(buf, *, name, reduce):
    S = buf.shape[0]

    def body(x_ref, o_ref, gat, send_sems, recv_sems):
        x, y, c, _ = _place()
        me = 4 * x + 2 * y + c
        gat[me] = x_ref[...]
        copies = []
        for k in range(1, N_DEV):
            fx, fy, fc = (k >> 2) & 1, (k >> 1) & 1, k & 1
            px = 1 - x if fx else x
            py = 1 - y if fy else y
            pc = 1 - c if fc else c
            peer = 4 * px + 2 * py + pc
            send = pltpu.make_async_remote_copy(
                src_ref=x_ref, dst_ref=gat.at[me], send_sem=send_sems.at[k - 1], recv_sem=recv_sems.at[k - 1],
                device_id=(px, py, pc), device_id_type=MESH)
            send.start()
            arrival = pltpu.make_async_remote_copy(
                src_ref=x_ref, dst_ref=gat.at[peer], send_sem=send_sems.at[k - 1], recv_sem=recv_sems.at[k - 1],
                device_id=(px, py, pc), device_id_type=MESH)
            copies.append((send, arrival))
        for send, arrival in copies:
            arrival.wait_recv()
        for send, arrival in copies:
            send.wait_send()
        if reduce:
            acc = gat[0]
            for d in range(1, N_DEV):
                acc = acc + gat[d]
            o_ref[...] = acc
        else:
            o_ref[...] = gat[...]

    out_shape = jax.ShapeDtypeStruct((S, LANES) if reduce else (N_DEV, S, LANES), F32)
    return pl.pallas_call(
        body,
        name=name,
        out_shape=out_shape,
        in_specs=[pl.BlockSpec(memory_space=pltpu.VMEM)],
        out_specs=pl.BlockSpec(memory_space=pltpu.VMEM),
        scratch_shapes=[pltpu.VMEM((N_DEV, S, LANES), F32), pltpu.SemaphoreType.DMA((N_DEV - 1,)),
                        pltpu.SemaphoreType.DMA((N_DEV - 1,))],
        compiler_params=pltpu.CompilerParams(vmem_limit_bytes=VMEM_LIMIT),
    )(buf)


class _Sharded:
    def __init__(self, shape, axis):
        self.shape = shape
        self.axis = axis
        self.block = shape[axis] // N_CHIPS
        self.half = shape[1 - axis] // 2

    def _sl(self, along, across):
        return (along, across) if self.axis == 0 else (across, along)

    def block_slice(self, q):
        return self._sl(pl.ds(q * self.block, self.block), pl.ds(0, self.shape[1 - self.axis]))

    def block_half_slice(self, q, c):
        return self._sl(pl.ds(q * self.block, self.block), pl.ds(c * self.half, self.half))

    def half_slice(self, c):
        return self._sl(pl.ds(0, self.shape[self.axis]), pl.ds(c * self.half, self.half))

    def shard_half_slice(self, c):
        return self._sl(pl.ds(0, self.block), pl.ds(c * self.half, self.half))

    def shard_shape(self):
        return self._sl(self.block, self.shape[1 - self.axis])

    def half_shape(self):
        return self._sl(self.shape[self.axis], self.half)

    def block_half_shape(self):
        return self._sl(self.block, self.half)

    def block_in_half_slice(self, q):
        return self._sl(pl.ds(q * self.block, self.block), pl.ds(0, self.half))


def _at(ref, sl):
    return ref.at[sl[0], sl[1]]


class _Copy:
    def __init__(self, src, dst, arrive, dev):
        self.src, self.dst, self.arrive, self.dev = src, dst, arrive, dev


HBM = pl.BlockSpec(memory_space=pltpu.HBM)
SEM = pl.BlockSpec(memory_space=pltpu.SEMAPHORE)
EFFECT = pltpu.SideEffectType.DATAFLOW_SIDE_EFFECTING


def _exchange_start(srcs, land_shapes, plan, after, *, name):
    ns, nl = len(srcs), len(land_shapes)
    n_copies = len(plan([None] * ns, [None] * nl, dry=True))

    def body(*refs):
        src_refs, land_refs = refs[:ns], refs[ns:ns + nl]
        send_sems, recv_sems = refs[ns + nl + 1], refs[ns + nl + 2]
        token = refs[-1]
        for k, cp in enumerate(plan(src_refs, land_refs)):
            pltpu.make_async_remote_copy(
                src_ref=cp.src, dst_ref=cp.dst, send_sem=send_sems.at[k], recv_sem=recv_sems.at[k],
                device_id=cp.dev, device_id_type=MESH).start()
        token[...] = jnp.zeros_like(token)

    sem = pltpu.SemaphoreType.DMA((n_copies,))
    lands = [pltpu.with_memory_space_constraint(lax.empty(shp, dt), pltpu.HBM) for shp, dt in land_shapes]
    srcs = [pltpu.with_memory_space_constraint(a, pltpu.HBM) for a in srcs]
    thru = [pltpu.HBM(a.shape, a.dtype) for a in srcs + lands]
    outs = pl.pallas_call(
        body,
        name=name,
        out_shape=(sem, sem, *thru, jax.ShapeDtypeStruct((8, LANES), F32)),
        in_specs=[HBM] * (ns + nl) + [ANY],
        out_specs=(SEM, SEM, *[HBM] * (ns + nl), pl.BlockSpec(memory_space=pltpu.VMEM)),
        input_output_aliases={i: 2 + i for i in range(ns + nl)},
        compiler_params=pltpu.CompilerParams(has_side_effects=EFFECT),
    )(*srcs, *lands, after)
    return outs[0], outs[1], list(outs[2:2 + ns]), list(outs[2 + ns:2 + ns + nl]), outs[-1]


def _exchange_wait(send_sems, recv_sems, srcs, lands, plan, after, *, name):
    ns, nl = len(srcs), len(lands)
    after = tuple(after) if isinstance(after, (tuple, list)) else (after,)

    def body(*refs):
        src_refs, land_refs = refs[:ns], refs[ns:ns + nl]
        send_sems, recv_sems = refs[ns + nl], refs[ns + nl + 1]
        copies = [
            pltpu.make_async_remote_copy(
                src_ref=cp.src, dst_ref=cp.arrive, send_sem=send_sems.at[k], recv_sem=recv_sems.at[k],
                device_id=cp.dev, device_id_type=MESH)
            for k, cp in enumerate(plan(src_refs, land_refs))
        ]
        for cp in copies:
            cp.wait_recv()
        for cp in copies:
            cp.wait_send()

    thru = [pltpu.HBM(a.shape, a.dtype) for a in list(srcs) + list(lands)]
    outs = pl.pallas_call(
        body,
        name=name,
        out_shape=tuple(thru),
        in_specs=[HBM] * (ns + nl) + [SEM, SEM] + [ANY] * len(after),
        out_specs=tuple([HBM] * (ns + nl)),
        input_output_aliases={i: i for i in range(ns + nl)},
        compiler_params=pltpu.CompilerParams(has_side_effects=EFFECT),
    )(*srcs, *lands, send_sems, recv_sems, *after)
    return list(outs[:ns]), list(outs[ns:])


def _gather_plan(layouts):
    def plan(srcs, lands, dry=False):
        if dry:
            return [None] * (4 * len(layouts))
        x, y, c, p = _place()
        copies = []
        for s, g, lay in zip(srcs, lands, layouts):
            own = _at(g, lay.block_slice(p))
            copies.append(_Copy(s, own, own, (x, y, 1 - c)))
            for qx, qy, q in _other_chips(x, y):
                copies.append(_Copy(_at(s, lay.shard_half_slice(c)), _at(g, lay.block_half_slice(p, c)),
                                    _at(g, lay.block_half_slice(q, c)), (qx, qy, c)))
        return copies

    return plan


def _forward_halves(gathered, layouts, *, name):
    n = len(gathered)

    def body(*refs):
        outs = refs[n:2 * n]
        send_sems, recv_sems = refs[2 * n:]
        x, y, c, _ = _place()
        sends = []
        for i in range(n):
            for j, (qx, qy, q) in enumerate(_other_chips(x, y)):
                landed = _at(outs[i], layouts[i].block_half_slice(q, c))
                cp = pltpu.make_async_remote_copy(
                    src_ref=landed, dst_ref=landed, send_sem=send_sems.at[i, j], recv_sem=recv_sems.at[i, j],
                    device_id=(x, y, 1 - c), device_id_type=MESH)
                cp.start()
                sends.append(cp)
        for i in range(n):
            for j, (qx, qy, q) in enumerate(_other_chips(x, y)):
                theirs = _at(outs[i], layouts[i].block_half_slice(q, 1 - c))
                pltpu.make_async_remote_copy(
                    src_ref=theirs, dst_ref=theirs, send_sem=send_sems.at[i, j], recv_sem=recv_sems.at[i, j],
                    device_id=(x, y, 1 - c), device_id_type=MESH).wait_recv()
        for cp in sends:
            cp.wait_send()

    sem = pltpu.SemaphoreType.DMA
    return pl.pallas_call(
        body,
        name=name,
        out_shape=[jax.ShapeDtypeStruct(lay.shape, BF16) for lay in layouts],
        in_specs=[ANY] * n,
        out_specs=[ANY] * n,
        input_output_aliases={i: i for i in range(n)},
        scratch_shapes=[sem((n, 3)), sem((n, 3))],
    )(*gathered)


def _sibling_halves_plan(layouts):
    def plan(srcs, lands, dry=False):
        if dry:
            return [None] * len(layouts)
        x, y, c, _ = _place()
        return [_Copy(_at(s, lay.half_slice(1 - c)), ra, ra, (x, y, 1 - c)) for s, ra, lay in zip(srcs, lands, layouts)]

    return plan


def _chips_plan(layouts):
    def plan(srcs, lands, dry=False):
        if dry:
            return [None] * (3 * len(layouts))
        x, y, c, _ = _place()
        copies = []
        for s, rb, lay in zip(srcs, lands, layouts):
            for j, (qx, qy, q) in enumerate(_other_chips(x, y)):
                copies.append(_Copy(_at(s, lay.block_in_half_slice(q)), rb.at[j], rb.at[j], (qx, qy, c)))
        return copies

    return plan


def _rs_join_halves(shards, layouts, *, name):
    n = len(shards)

    def body(*refs):
        outs = refs[n:2 * n]
        send_sems, recv_sems = refs[2 * n:]
        x, y, c, _ = _place()
        sends = []
        for i in range(n):
            mine = _at(outs[i], layouts[i].shard_half_slice(c))
            cp = pltpu.make_async_remote_copy(
                src_ref=mine, dst_ref=mine, send_sem=send_sems.at[i], recv_sem=recv_sems.at[i],
                device_id=(x, y, 1 - c), device_id_type=MESH)
            cp.start()
            sends.append(cp)
        for i in range(n):
            theirs = _at(outs[i], layouts[i].shard_half_slice(1 - c))
            pltpu.make_async_remote_copy(
                src_ref=theirs, dst_ref=theirs, send_sem=send_sems.at[i], recv_sem=recv_sems.at[i],
                device_id=(x, y, 1 - c), device_id_type=MESH).wait_recv()
        for cp in sends:
            cp.wait_send()

    sem = pltpu.SemaphoreType.DMA
    return pl.pallas_call(
        body,
        name=name,
        out_shape=[jax.ShapeDtypeStruct(lay.shard_shape(), F32) for lay in layouts],
        in_specs=[ANY] * n,
        out_specs=[ANY] * n,
        input_output_aliases={i: i for i in range(n)},
        scratch_shapes=[sem((n,)), sem((n,))],
    )(*shards)


def _pack(arrays):
    flat = [a.reshape(-1) for a in arrays]
    sizes = [f.shape[0] for f in flat]
    total = sum(sizes)
    rows = _round_up(-(-total // LANES), 8)
    flat.append(jnp.zeros((rows * LANES - total,), F32))
    return jnp.concatenate(flat).reshape(rows, LANES)


def _unpack(buf, shapes):
    flat = buf.reshape(-1)
    out, pos = [], 0
    for shp in shapes:
        n = 1
        for d in shp:
            n *= d
        out.append(flat[pos:pos + n].reshape(shp))
        pos += n
    return out


def _pad_to(a, rows, cols):
    return jnp.pad(a, ((0, rows - a.shape[0]), (0, cols - a.shape[1])))


def kernel(x, mem, g_mix, w_in, conv_a_w, conv_a_b, ln_a_g, ln_a_b, conv_b_w, w_out, g_xattn, g_mem, w_q, w_k, w_v, w_o, g_ffn, w_gate, w_up, conv_f_w, w_down, g_final, loss_target, m_g_mix, m_w_in, m_conv_a_w, m_conv_a_b, m_ln_a_g, m_ln_a_b, m_conv_b_w, m_w_out, m_g_xattn, m_g_mem, m_w_q, m_w_k, m_w_v, m_w_o, m_g_ffn, m_w_gate, m_w_up, m_conv_f_w, m_w_down, m_g_final, v_g_mix, v_w_in, v_conv_a_w, v_conv_a_b, v_ln_a_g, v_ln_a_b, v_conv_b_w, v_w_out, v_g_xattn, v_g_mem, v_w_q, v_w_k, v_w_v, v_w_o, v_g_ffn, v_w_gate, v_w_up, v_conv_f_w, v_w_down, v_g_final):
    T, D = x.shape[1], x.shape[2]
    in_sh = w_in.shape[2]
    cw_sh = conv_a_w.shape[2]
    cw = N_CHIPS * cw_sh
    f_sh = w_gate.shape[2]
    fp = _round_up(f_sh, 256)
    F = N_CHIPS * fp
    rs = w_out.shape[1]
    c_idx = lax.axis_index("c")
    p_idx = 2 * lax.axis_index("x") + lax.axis_index("y")

    def t_(a):
        return jnp.swapaxes(a[0], 0, 1)

    big = {
        "w_in": (lambda: w_in[0].astype(BF16), _Sharded((D, N_CHIPS * in_sh), 1)),
        "w_out": (lambda: w_out[0].astype(BF16), _Sharded((N_CHIPS * rs, D), 0)),
        "w_q": (lambda: w_q[0].astype(BF16), _Sharded((D, D), 0)),
        "w_k": (lambda: w_k[0].astype(BF16), _Sharded((D, D), 0)),
        "w_v": (lambda: w_v[0].astype(BF16), _Sharded((D, D), 0)),
        "w_o": (lambda: w_o[0].astype(BF16), _Sharded((D, D), 0)),
        "w_gate": (lambda: _pad_to(t_(w_gate).astype(BF16), fp, D), _Sharded((F, D), 0)),
        "w_up": (lambda: _pad_to(t_(w_up).astype(BF16), fp, D), _Sharded((F, D), 0)),
        "w_down": (lambda: _pad_to(w_down[0].astype(BF16), fp, D), _Sharded((F, D), 0)),
    }
    names = list(big)
    lay = {k: big[k][1] for k in names}
    c_arr = c_idx.astype(jnp.int32).reshape(1)
    pc_arr = jnp.stack([p_idx, c_idx]).astype(jnp.int32)

    tok = jnp.zeros((8, LANES), F32)

    gather_groups = [["w_in"], ["w_out"], ["w_q", "w_k", "w_v"], ["w_o"], ["w_gate"], ["w_up"], ["w_down"]]
    gathers = []
    for gi, grp in enumerate(gather_groups):
        lays = [lay[k] for k in grp]
        plan = _gather_plan(lays)
        ssem, rsem, srcs, lands, tok = _exchange_start(
            [big[k][0]() for k in grp], [(l.shape, BF16) for l in lays], plan, tok, name=f"gather_start_{gi}")
        gathers.append((ssem, rsem, srcs, lands, plan, lays, grp))
    W = {}

    def gathered(gi, after):
        ssem, rsem, srcs, lands, plan, lays, grp = gathers[gi]
        _, lands = _exchange_wait(ssem, rsem, srcs, lands, plan, after, name=f"gather_wait_{gi}")
        W.update(zip(grp, _forward_halves(lands, lays, name=f"gather_forward_{gi}")))

    gathered(0, tok)

    conv_shapes = [(KA_ROWS, cw_sh), (KS_ROWS, cw_sh), (KS_ROWS, fp)]
    conv_pack = _pack([_pad_to(conv_a_w[0], KA_ROWS, cw_sh), _pad_to(conv_b_w[0], KS_ROWS, cw_sh),
                       _pad_to(conv_f_w[0], KS_ROWS, fp)])
    conv_all = _allgather_small(conv_pack, name="allgather_conv", reduce=False)
    per_chip = [_unpack(conv_all[2 * q], conv_shapes) for q in range(N_CHIPS)]
    wa = jnp.concatenate([pc[0] for pc in per_chip], axis=1)
    wb = jnp.concatenate([pc[1] for pc in per_chip], axis=1)
    wf = jnp.concatenate([pc[2] for pc in per_chip], axis=1)

    xs, tgt, mems = x[0], loss_target[0], mem[0]
    g_mem2, g_final2 = g_mem[None, :], g_final[None, :]

    memn, rm = _norm_fwd(mems, g_mem2, name="norm_mem")
    xn1, r1 = _norm_fwd(xs, g_mix, name="norm_mix")
    proj = _mm(xn1, W["w_in"], name="mm_in")
    u3, u1 = _mix_a_fwd(proj, wa, conv_a_b, ln_a_g, ln_a_b, cw=cw, name="mix_a_fwd")
    vb = _mix_b_fwd(proj, wb, cw=cw, sw=cw, name="mix_b_fwd")
    mix = jnp.concatenate([u3, vb], axis=1)
    gathered(1, mix)
    h1 = _mm(mix, W["w_out"], add=xs, name="mm_out")
    xn2, r2 = _norm_fwd(h1, g_xattn, name="norm_xattn")
    gathered(2, xn2)
    q = _mm(xn2, W["w_q"], out_dtype=BF16, name="mm_q")
    k = _mm(memn, W["w_k"], out_dtype=BF16, name="mm_k")
    vm = _mm(memn, W["w_v"], out_dtype=BF16, name="mm_v")
    o = _attn_fwd(q, k, vm, name="attn_fwd")
    gathered(3, o)
    h2 = _mm(o, W["w_o"], add=h1, name="mm_o")
    xn3, r3 = _norm_fwd(h2, g_ffn, name="norm_ffn")
    gathered(4, xn3)
    gp = _mm(xn3, W["w_gate"], tb=True, name="mm_gate")
    gathered(5, gp)
    up = _mm(xn3, W["w_up"], tb=True, name="mm_up")
    f = _ffn_act_fwd(gp, up, wf, name="ffn_act_fwd")
    gathered(6, f)
    h3 = _mm(f, W["w_down"], add=h2, name="mm_down")

    G = {}

    def siblings_start(tag, grp, after):
        lays = [lay[k] for k in grp]
        plan = _sibling_halves_plan(lays)
        ssem, rsem, srcs, lands, token = _exchange_start(
            [G[k] for k in grp], [(l.half_shape(), F32) for l in lays], plan, after, name=f"rs_siblings_start_{tag}")
        return (tag, grp, lays, plan, ssem, rsem, srcs, lands), token

    def chips_start(state, after):
        tag, grp, lays, plan, ssem, rsem, srcs, lands = state
        srcs, lands = _exchange_wait(ssem, rsem, srcs, lands, plan, after, name=f"rs_siblings_wait_{tag}")
        sums = [_half_sum(g, ra, c_arr, rows_split=(l.axis == 1), name=f"half_sum_{k}")
                for k, g, ra, l in zip(grp, srcs, lands, lays)]
        plan = _chips_plan(lays)
        ssem, rsem, sums, lands, token = _exchange_start(
            sums, [((N_CHIPS - 1,) + l.block_half_shape(), BF16) for l in lays], plan, sums[-1], name=f"rs_chips_start_{tag}")
        return (tag, grp, lays, plan, ssem, rsem, sums, lands), token

    def chips_finish(state, after):
        tag, grp, lays, plan, ssem, rsem, sums, lands = state
        sums, lands = _exchange_wait(ssem, rsem, sums, lands, plan, after, name=f"rs_chips_wait_{tag}")
        halves = [_sum_chips(cs, rb, pc_arr, l, name=f"sum_chips_{k}") for k, cs, rb, l in zip(grp, sums, lands, lays)]
        return dict(zip(grp, _rs_join_halves(halves, lays, name=f"rs_join_{tag}")))

    loss_rows, dh3, dh3b, dg_final = _loss_head(h3, g_final2, tgt, name="loss_head")
    df = _mm(dh3b, W["w_down"], tb=True, out_dtype=BF16, name="mm_d_f")
    G["w_down"] = _mm(f, dh3b, ta=True, name="mm_dw_down")
    rs_down, tok = siblings_start("down", ["w_down"], tok)
    dgp, dup, dwf = _ffn_act_bwd(gp, up, df, wf, name="ffn_act_bwd")
    G["w_gate"] = _mm(dgp, xn3, ta=True, deps=[tok], name="mm_dw_gate")
    G["w_up"] = _mm(dup, xn3, ta=True, name="mm_dw_up")
    rs_ffn, tok = siblings_start("ffn", ["w_gate", "w_up"], tok)
    rs_down, tok = chips_start(rs_down, tok)
    dxn3 = _mm(dgp, W["w_gate"], deps=[tok], name="mm_dxn3_gate")
    dxn3 = _mm(dup, W["w_up"], add=dxn3, name="mm_dxn3_up")
    dh2, dh2b, dg_ffn = _norm_bwd(h2, r3, g_ffn, dxn3, dh3, name="norm_ffn_bwd")
    rs_ffn, tok = chips_start(rs_ffn, dh2b)
    do = _mm(dh2b, W["w_o"], tb=True, out_dtype=BF16, deps=[tok], name="mm_d_o")
    G["w_o"] = _mm(o, dh2b, ta=True, name="mm_dw_o")
    dq, dk, dvm = _attn_bwd(q, k, vm, do, name="attn_bwd")
    dkb, dvb = dk.astype(BF16), dvm.astype(BF16)
    G["w_q"] = _mm(xn2, dq, ta=True, name="mm_dw_q")
    G["w_k"] = _mm(memn, dkb, ta=True, name="mm_dw_k")
    G["w_v"] = _mm(memn, dvb, ta=True, name="mm_dw_v")
    rs_att, tok = siblings_start("att", ["w_o", "w_q", "w_k", "w_v"], tok)
    dxn2 = _mm(dq, W["w_q"], tb=True, deps=[tok], name="mm_dxn2")
    dmemn = _mm(dkb, W["w_k"], tb=True, name="mm_dmem_k")
    dmemn = _mm(dvb, W["w_v"], tb=True, add=dmemn, name="mm_dmem_v")
    dg_mem = _norm_bwd(mems, rm, g_mem2, dmemn, None, name="norm_mem_bwd", want_dh=False)
    dh1, dh1b, dg_xattn = _norm_bwd(h1, r2, g_xattn, dxn2, dh2, name="norm_xattn_bwd")
    rs_att, tok = chips_start(rs_att, dh1b)
    dmix = _mm(dh1b, W["w_out"], tb=True, deps=[tok], name="mm_d_mix")
    G["w_out"] = _mm(mix, dh1b, ta=True, name="mm_dw_out")
    rs_out, tok = siblings_start("out", ["w_out"], tok)
    dav, dag, dwa, dba, dlg, dlb = _mix_a_bwd(proj, u1, dmix, wa, ln_a_g, ln_a_b, cw=cw, name="mix_a_bwd")
    dbg, dcg, dbh, dwb = _mix_b_bwd(proj, dmix, wb, cw=cw, sw=cw, name="mix_b_bwd")
    dproj = jnp.concatenate([dav, dag, dbg, dcg, dbh], axis=1)
    rs_out, tok = chips_start(rs_out, dproj)
    G["w_in"] = _mm(xn1, dproj, ta=True, deps=[tok], name="mm_dw_in")
    rs_in, tok = siblings_start("in", ["w_in"], tok)
    dxn1 = _mm(dproj, W["w_in"], tb=True, deps=[tok], name="mm_dxn1")
    dx, _, dg_mix = _norm_bwd(xs, r1, g_mix, dxn1, dh1, name="norm_mix_bwd")
    rs_in, tok = chips_start(rs_in, dx)

    loss_part = jnp.sum(loss_rows).reshape(1, 1)
    small_parts = [dg_mix, dba, dlg, dlb, dg_xattn, dg_mem, dg_ffn, dg_final, dwa, dwb, dwf, loss_part]
    small_shapes = [a.shape for a in small_parts]
    reduced = _allgather_small(_pack(small_parts), name="allreduce_small", reduce=True)
    (sg_mix, sba, slg, slb, sg_xattn, sg_mem, sg_ffn, sg_final, swa, swb, swf, loss_sum) = _unpack(reduced, small_shapes)
    loss = loss_sum.reshape(())
    ga_w = lax.dynamic_slice(swa, (0, p_idx * cw_sh), (KA, cw_sh))
    gb_w = lax.dynamic_slice(swb, (0, p_idx * cw_sh), (KS, cw_sh))
    gf_w = lax.dynamic_slice(swf, (0, p_idx * fp), (KS, f_sh))

    weights = dict(g_mix=g_mix, w_in=w_in, conv_a_w=conv_a_w, conv_a_b=conv_a_b, ln_a_g=ln_a_g, ln_a_b=ln_a_b,
                   conv_b_w=conv_b_w, w_out=w_out, g_xattn=g_xattn, g_mem=g_mem, w_q=w_q, w_k=w_k, w_v=w_v, w_o=w_o,
                   g_ffn=g_ffn, w_gate=w_gate, w_up=w_up, conv_f_w=conv_f_w, w_down=w_down, g_final=g_final)
    m_in = dict(g_mix=m_g_mix, w_in=m_w_in, conv_a_w=m_conv_a_w, conv_a_b=m_conv_a_b, ln_a_g=m_ln_a_g, ln_a_b=m_ln_a_b,
                conv_b_w=m_conv_b_w, w_out=m_w_out, g_xattn=m_g_xattn, g_mem=m_g_mem, w_q=m_w_q, w_k=m_w_k, w_v=m_w_v,
                w_o=m_w_o, g_ffn=m_g_ffn, w_gate=m_w_gate, w_up=m_w_up, conv_f_w=m_conv_f_w, w_down=m_w_down,
                g_final=m_g_final)
    v_in = dict(g_mix=v_g_mix, w_in=v_w_in, conv_a_w=v_conv_a_w, conv_a_b=v_conv_a_b, ln_a_g=v_ln_a_g, ln_a_b=v_ln_a_b,
                conv_b_w=v_conv_b_w, w_out=v_w_out, g_xattn=v_g_xattn, g_mem=v_g_mem, w_q=v_w_q, w_k=v_w_k, w_v=v_w_v,
                w_o=v_w_o, g_ffn=v_g_ffn, w_gate=v_w_gate, w_up=v_w_up, conv_f_w=v_conv_f_w, w_down=v_w_down,
                g_final=v_g_final)
    order = list(weights)
    grads = dict(g_mix=sg_mix, conv_a_w=ga_w, conv_a_b=sba, ln_a_g=slg, ln_a_b=slb, conv_b_w=gb_w, g_xattn=sg_xattn,
                 g_mem=sg_mem, g_ffn=sg_ffn, conv_f_w=gf_w, g_final=sg_final)
    grads = {k: g.reshape(weights[k].shape) for k, g in grads.items()}

    delta, new_m, new_v = {}, {}, {}
    small = [k for k in order if k not in big]
    small_shapes = [weights[k].shape for k in small]
    packed = [_pack([src[k] for k in small]) for src in (weights, grads, m_in, v_in)]
    d_, m_, v_ = _adamw(*packed, name="adamw_small")
    for k, dd, mm_, vv in zip(small, _unpack(d_, small_shapes), _unpack(m_, small_shapes), _unpack(v_, small_shapes)):
        delta[k], new_m[k], new_v[k] = dd, mm_, vv

    transposed = ("w_gate", "w_up")
    after = (d_, tok)
    for state in (rs_down, rs_ffn, rs_att, rs_out, rs_in):
        for k, g in chips_finish(state, after).items():
            view = t_ if k in transposed else (lambda a: a[0])
            back = (lambda a: jnp.swapaxes(a, 0, 1)[None]) if k in transposed else (lambda a: a[None])
            padded = g.shape != view(weights[k]).shape
            outs = _adamw(view(weights[k]), g, view(m_in[k]), view(v_in[k]), emit_grad=padded, name=f"adamw_{k}")
            delta[k], new_m[k], new_v[k] = back(outs[0]), back(outs[1]), back(outs[2])
            grads[k] = back(outs[3] if padded else g)
            after = outs[0]

    return (loss, dx[None], *[grads[k] for k in order], *[delta[k] for k in order],
            *[new_m[k] for k in order], *[new_v[k] for k in order])
```

```python
import functools

import jax
import jax.numpy as jnp
from jax import lax
from jax.experimental import pallas as pl
from jax.experimental.pallas import tpu as pltpu

F32 = jnp.float32
BF16 = jnp.bfloat16
EPS = 1e-6
N_HEADS = 4
GROUP = 128
KA = 31
KS = 3
KA_ROWS = 32
KS_ROWS = 8
HALO_A = 32
HALO_S = 8
N_CHIPS = 4
N_DEV = 8
LANES = 128
VMEM_LIMIT = 56 * 1024 * 1024
ADAMW_BLOCK_BYTES = 1 << 20
MESH = pl.DeviceIdType.MESH
ANY = pl.BlockSpec(memory_space=pl.ANY)

ADAM_LR = 0.001
ADAM_B1 = 0.9
ADAM_B2 = 0.999
ADAM_EPS = 1e-08
ADAM_WD = 0.01
ADAM_STEP = 10


def _pick(dim, pref, mult=LANES):
    if dim <= pref:
        return dim
    t = (pref // mult) * mult
    while t >= mult:
        if dim % t == 0:
            return t
        t -= mult
    return dim


def _round_up(n, m):
    return ((n + m - 1) // m) * m


def _params(*sem):
    return pltpu.CompilerParams(dimension_semantics=sem, vmem_limit_bytes=VMEM_LIMIT)


def _sigmoid(x):
    return jax.nn.sigmoid(x)


def _mm(a, b, *, name, ta=False, tb=False, out_dtype=F32, add=None, deps=(), tm=1024, tn=512, tk=5632):
    if ta:
        K, M = a.shape
    else:
        M, K = a.shape
    if tb:
        N, K2 = b.shape
    else:
        K2, N = b.shape
    assert K == K2, (a.shape, b.shape)
    tm, tn, tk = _pick(M, tm), _pick(N, tn), _pick(K, tk)
    nk = K // tk
    a_spec = pl.BlockSpec((tk, tm), lambda i, j, k: (k, i)) if ta else pl.BlockSpec((tm, tk), lambda i, j, k: (i, k))
    b_spec = pl.BlockSpec((tn, tk), lambda i, j, k: (j, k)) if tb else pl.BlockSpec((tk, tn), lambda i, j, k: (k, j))
    o_spec = pl.BlockSpec((tm, tn), lambda i, j, k: (i, j))
    dims = (((0,) if ta else (1,), (1,) if tb else (0,)), ((), ()))
    has_add = add is not None

    def body(*refs):
        a_ref, b_ref = refs[:2]
        add_ref = refs[2] if has_add else None
        o_ref = refs[(3 if has_add else 2) + len(deps)]
        acc_ref = refs[-1] if nk > 1 else None
        k = pl.program_id(2)
        part = lax.dot_general(a_ref[...], b_ref[...], dims, preferred_element_type=F32)

        def finish(r):
            if add_ref is not None:
                r = add_ref[...] + r
            o_ref[...] = r.astype(out_dtype)

        if nk == 1:
            finish(part)
        else:
            @pl.when(k == 0)
            def _():
                acc_ref[...] = part

            @pl.when(jnp.logical_and(k > 0, k < nk - 1))
            def _():
                acc_ref[...] += part

            @pl.when(k == nk - 1)
            def _():
                finish(acc_ref[...] + part)

    in_specs = [a_spec, b_spec] + ([o_spec] if has_add else []) + [ANY] * len(deps)
    args = (a, b) + ((add,) if has_add else ()) + tuple(deps)
    return pl.pallas_call(
        body,
        name=name,
        out_shape=jax.ShapeDtypeStruct((M, N), out_dtype),
        grid=(M // tm, N // tn, nk),
        in_specs=in_specs,
        out_specs=o_spec,
        scratch_shapes=[pltpu.VMEM((tm, tn), F32)] if nk > 1 else [],
        compiler_params=_params("parallel", "parallel", "arbitrary"),
    )(*args)


def _norm_fwd(h, g, *, name):
    T, D = h.shape
    R = _pick(T, 256, 8)

    def body(h_ref, g_ref, xn_ref, r_ref):
        x = h_ref[...]
        r = lax.rsqrt(jnp.mean(x * x, axis=-1, keepdims=True) + EPS)
        xn_ref[...] = ((x * r) * g_ref[...]).astype(BF16)
        r_ref[...] = r

    return pl.pallas_call(
        body,
        name=name,
        out_shape=(jax.ShapeDtypeStruct((T, D), BF16), jax.ShapeDtypeStruct((T, 1), F32)),
        grid=(T // R,),
        in_specs=[pl.BlockSpec((R, D), lambda i: (i, 0)), pl.BlockSpec((1, D), lambda i: (0, 0))],
        out_specs=(pl.BlockSpec((R, D), lambda i: (i, 0)), pl.BlockSpec((R, 1), lambda i: (i, 0))),
        compiler_params=_params("parallel"),
    )(h, g)


def _norm_bwd(h, r, g, dxn, dres, *, name, want_dh=True):
    T, D = h.shape
    R = _pick(T, 128, 8)
    has_res = dres is not None

    def body(*refs):
        h_ref, r_ref, g_ref, dxn_ref = refs[:4]
        pos = 4
        dres_ref = None
        if has_res:
            dres_ref = refs[pos]
            pos += 1
        if want_dh:
            dh_ref, dhb_ref, dg_ref = refs[pos:pos + 3]
        else:
            dg_ref = refs[pos]
        i = pl.program_id(0)
        rr = r_ref[...]
        hn = h_ref[...] * rr
        d = dxn_ref[...].astype(F32)
        gd = d * g_ref[...]
        part = jnp.sum(d * hn, axis=0, keepdims=True)

        @pl.when(i == 0)
        def _():
            dg_ref[...] = part

        @pl.when(i > 0)
        def _():
            dg_ref[...] += part

        if want_dh:
            dh = rr * (gd - hn * jnp.mean(gd * hn, axis=-1, keepdims=True))
            if dres_ref is not None:
                dh = dres_ref[...] + dh
            dh_ref[...] = dh
            dhb_ref[...] = dh.astype(BF16)

    row = pl.BlockSpec((R, D), lambda i: (i, 0))
    vec = pl.BlockSpec((1, D), lambda i: (0, 0))
    in_specs = [row, pl.BlockSpec((R, 1), lambda i: (i, 0)), vec, row] + ([row] if has_res else [])
    args = (h, r, g, dxn) + ((dres,) if has_res else ())
    if want_dh:
        out_shape = (jax.ShapeDtypeStruct((T, D), F32), jax.ShapeDtypeStruct((T, D), BF16), jax.ShapeDtypeStruct((1, D), F32))
        out_specs = (row, row, vec)
    else:
        out_shape = jax.ShapeDtypeStruct((1, D), F32)
        out_specs = vec
    return pl.pallas_call(
        body, name=name, out_shape=out_shape, grid=(T // R,), in_specs=in_specs, out_specs=out_specs,
        compiler_params=_params("arbitrary"),
    )(*args)


def _loss_head(h, g, tgt, *, name):
    T, D = h.shape
    R = _pick(T, 128, 8)

    def body(h_ref, g_ref, t_ref, loss_ref, dh_ref, dhb_ref, dg_ref):
        i = pl.program_id(0)
        x = h_ref[...]
        gg = g_ref[...]
        r = lax.rsqrt(jnp.mean(x * x, axis=-1, keepdims=True) + EPS)
        hn = x * r
        e = hn * gg - t_ref[...]
        loss_ref[...] = 0.5 * jnp.mean(e * e, axis=-1, keepdims=True)
        dy = e * (1.0 / D)
        gd = dy * gg
        dh = r * (gd - hn * jnp.mean(gd * hn, axis=-1, keepdims=True))
        dh_ref[...] = dh
        dhb_ref[...] = dh.astype(BF16)
        part = jnp.sum(dy * hn, axis=0, keepdims=True)

        @pl.when(i == 0)
        def _():
            dg_ref[...] = part

        @pl.when(i > 0)
        def _():
            dg_ref[...] += part

    row = pl.BlockSpec((R, D), lambda i: (i, 0))
    vec = pl.BlockSpec((1, D), lambda i: (0, 0))
    return pl.pallas_call(
        body,
        name=name,
        out_shape=(jax.ShapeDtypeStruct((T, 1), F32), jax.ShapeDtypeStruct((T, D), F32),
                   jax.ShapeDtypeStruct((T, D), BF16), jax.ShapeDtypeStruct((1, D), F32)),
        grid=(T // R,),
        in_specs=[row, vec, row],
        out_specs=(pl.BlockSpec((R, 1), lambda i: (i, 0)), row, row, vec),
        compiler_params=_params("arbitrary"),
    )(h, g, tgt)


_NT = (((1,), (1,)), ((), ()))
_TN = (((0,), (0,)), ((), ()))
_NN = (((1,), (0,)), ((), ()))


def _softmax_rows(s):
    m = jnp.max(s, axis=-1, keepdims=True)
    e = jnp.exp(s - m)
    return e / jnp.sum(e, axis=-1, keepdims=True)


def _attn_fwd(q, k, v, *, name):
    T, D = q.shape
    ML = k.shape[0]
    dh = D // N_HEADS
    scale = dh ** -0.5
    R = _pick(T, 512, 16)

    def body(q_ref, k_ref, v_ref, o_ref):
        s = lax.dot_general(q_ref[...], k_ref[...], _NT, preferred_element_type=F32) * scale
        p = _softmax_rows(s)
        o_ref[...] = lax.dot_general(p.astype(BF16), v_ref[...], _NN, preferred_element_type=F32).astype(BF16)

    qs = pl.BlockSpec((R, dh), lambda i, h: (i, h))
    ks = pl.BlockSpec((ML, dh), lambda i, h: (0, h))
    return pl.pallas_call(
        body, name=name, out_shape=jax.ShapeDtypeStruct((T, D), BF16), grid=(T // R, N_HEADS),
        in_specs=[qs, ks, ks], out_specs=qs, compiler_params=_params("parallel", "parallel"),
    )(q, k, v)


def _attn_bwd(q, k, v, do, *, name):
    T, D = q.shape
    ML = k.shape[0]
    dh = D // N_HEADS
    scale = dh ** -0.5
    R = _pick(T, 512, 16)

    def body(q_ref, k_ref, v_ref, do_ref, dq_ref, dk_ref, dv_ref):
        i = pl.program_id(1)
        qq, kk, vv, dd = q_ref[...], k_ref[...], v_ref[...], do_ref[...]
        s = lax.dot_general(qq, kk, _NT, preferred_element_type=F32) * scale
        p = _softmax_rows(s)
        dp = lax.dot_general(dd, vv, _NT, preferred_element_type=F32)
        dv_part = lax.dot_general(p.astype(BF16), dd, _TN, preferred_element_type=F32)
        ds = (p * (dp - jnp.sum(p * dp, axis=-1, keepdims=True)) * scale).astype(BF16)
        dq_ref[...] = lax.dot_general(ds, kk, _NN, preferred_element_type=F32).astype(BF16)
        dk_part = lax.dot_general(ds, qq, _TN, preferred_element_type=F32)

        @pl.when(i == 0)
        def _():
            dk_ref[...] = dk_part
            dv_ref[...] = dv_part

        @pl.when(i > 0)
        def _():
            dk_ref[...] += dk_part
            dv_ref[...] += dv_part

    qs = pl.BlockSpec((R, dh), lambda h, i: (i, h))
    ks = pl.BlockSpec((ML, dh), lambda h, i: (0, h))
    return pl.pallas_call(
        body,
        name=name,
        out_shape=(jax.ShapeDtypeStruct((T, D), BF16), jax.ShapeDtypeStruct((ML, D), F32), jax.ShapeDtypeStruct((ML, D), F32)),
        grid=(N_HEADS, T // R),
        in_specs=[qs, ks, ks, qs],
        out_specs=(qs, ks, ks),
        compiler_params=_params("parallel", "arbitrary"),
    )(q, k, v, do)


def _prev_halo(R, halo, CT, col):
    per = R // halo
    return lambda i, j: (jnp.maximum(i * per - 1, 0), col(j))


def _next_halo(R, halo, T, col):
    per = R // halo
    last = T // halo - 1
    return lambda i, j: (jnp.minimum((i + 1) * per, last), col(j))


def _swap(f):
    return lambda j, i: f(i, j)


def _ffn_act_fwd(gp, up, wf, *, name):
    T, F = gp.shape
    R = _pick(T, 256, 8)
    CT = _pick(F, 1024)
    H = HALO_S

    def body(g_ref, gh_ref, u_ref, w_ref, f_ref, ext):
        i = pl.program_id(0)
        ext[pl.ds(0, H), :] = jnp.where(i > 0, gh_ref[...], 0.0)
        ext[pl.ds(H, R), :] = g_ref[...]
        w = w_ref[...]
        g = ext[pl.ds(H - 2, R), :] * w[0:1, :]
        g = g + ext[pl.ds(H - 1, R), :] * w[1:2, :]
        g = g + ext[pl.ds(H, R), :] * w[2:3, :]
        f_ref[...] = (g * _sigmoid(g) * u_ref[...]).astype(BF16)

    main = pl.BlockSpec((R, CT), lambda i, j: (i, j))
    return pl.pallas_call(
        body,
        name=name,
        out_shape=jax.ShapeDtypeStruct((T, F), BF16),
        grid=(T // R, F // CT),
        in_specs=[main, pl.BlockSpec((H, CT), _prev_halo(R, H, CT, lambda j: j)), main,
                  pl.BlockSpec((KS_ROWS, CT), lambda i, j: (0, j))],
        out_specs=main,
        scratch_shapes=[pltpu.VMEM((R + H, CT), F32)],
        compiler_params=_params("parallel", "parallel"),
    )(gp, gp, up, wf)


def _ffn_act_bwd(gp, up, df, wf, *, name):
    T, F = gp.shape
    R = _pick(T, 256, 16)
    CT = _pick(F, 1024)
    H = HALO_S
    HB = 16
    n_t = T // R

    def body(g_ref, gp_ref, gn_ref, u_ref, un_ref, d_ref, dn_ref, w_ref, dg_out, du_out, dw_ref, ext, dge):
        i = pl.program_id(1)
        last = i == n_t - 1
        ext[pl.ds(0, H), :] = jnp.where(i > 0, gp_ref[...], 0.0)
        ext[pl.ds(H, R), :] = g_ref[...]
        ext[pl.ds(H + R, H), :] = gn_ref[...]
        w = w_ref[...]
        w0, w1, w2 = w[0:1, :], w[1:2, :], w[2:3, :]

        def conv(start, n):
            g = ext[pl.ds(start + H - 2, n), :] * w0
            g = g + ext[pl.ds(start + H - 1, n), :] * w1
            return g + ext[pl.ds(start + H, n), :] * w2

        def dact(g, u, d):
            sg = _sigmoid(g)
            return d * u * (sg * (1.0 + g * (1.0 - sg))), d * (g * sg)

        d_main = d_ref[...].astype(F32)
        dg_main, du_main = dact(conv(0, R), u_ref[...], d_main)
        du_out[...] = du_main.astype(BF16)
        d_next = jnp.where(last, 0.0, dn_ref[pl.ds(0, H), :].astype(F32))
        dg_next, _ = dact(conv(R, H), un_ref[...], d_next)
        dg_next = jnp.where(last, 0.0, dg_next)
        dge[pl.ds(0, R), :] = dg_main
        dge[pl.ds(R, H), :] = dg_next
        dgp = dge[pl.ds(0, R), :] * w2 + dge[pl.ds(1, R), :] * w1 + dge[pl.ds(2, R), :] * w0
        dg_out[...] = dgp.astype(BF16)
        rows = [jnp.sum(dg_main * ext[pl.ds(H - 2 + k, R), :], axis=0, keepdims=True) for k in range(KS)]
        rows.append(jnp.zeros((KS_ROWS - KS, CT), F32))
        _acc_rows_block(dw_ref, i, rows)

    col = lambda j: j
    main = pl.BlockSpec((R, CT), lambda j, i: (i, j))
    prev8 = pl.BlockSpec((H, CT), _swap(_prev_halo(R, H, CT, col)))
    next8 = pl.BlockSpec((H, CT), _swap(_next_halo(R, H, T, col)))
    next16 = pl.BlockSpec((HB, CT), _swap(_next_halo(R, HB, T, col)))
    wspec = pl.BlockSpec((KS_ROWS, CT), lambda j, i: (0, j))
    return pl.pallas_call(
        body,
        name=name,
        out_shape=(jax.ShapeDtypeStruct((T, F), BF16), jax.ShapeDtypeStruct((T, F), BF16), jax.ShapeDtypeStruct((KS_ROWS, F), F32)),
        grid=(F // CT, n_t),
        in_specs=[main, prev8, next8, main, next8, main, next16, wspec],
        out_specs=(main, main, wspec),
        scratch_shapes=[pltpu.VMEM((R + 2 * H, CT), F32), pltpu.VMEM((R + H, CT), F32)],
        compiler_params=_params("parallel", "arbitrary"),
    )(gp, gp, gp, up, up, df, df, wf)


def _acc_rows_block(ref, i, rows):
    *singles, pad = rows

    @pl.when(i == 0)
    def _():
        for k, row in enumerate(singles):
            ref[pl.ds(k, 1), :] = row
        ref[pl.ds(len(singles), pad.shape[0]), :] = pad

    @pl.when(i > 0)
    def _():
        for k, row in enumerate(singles):
            ref[pl.ds(k, 1), :] += row


def _group_stats(x):
    mu = jnp.mean(x, axis=-1, keepdims=True)
    xc = x - mu
    var = jnp.mean(xc * xc, axis=-1, keepdims=True)
    return xc, lax.rsqrt(var + EPS)


def _mix_a_fwd(proj, wa, ba, lg, lb, *, cw, name):
    T = proj.shape[0]
    R = _pick(T, 128, HALO_A)
    CT = _pick(cw, 256)
    nc = cw // CT
    H = HALO_A

    def body(av_ref, ag_ref, avh_ref, agh_ref, w_ref, b_ref, lg_ref, lb_ref, u3_ref, u1_ref, ext):
        i = pl.program_id(0)
        ext[pl.ds(0, H), :] = jnp.where(i > 0, avh_ref[...] * _sigmoid(agh_ref[...]), 0.0)
        ext[pl.ds(H, R), :] = av_ref[...] * _sigmoid(ag_ref[...])
        w = w_ref[...]
        acc = ext[pl.ds(H - (KA - 1), R), :] * w[0:1, :]
        for k in range(1, KA):
            acc = acc + ext[pl.ds(H - (KA - 1) + k, R), :] * w[k:k + 1, :]
        u1 = acc + b_ref[...]
        u1_ref[...] = u1
        for gi in range(CT // GROUP):
            sl = slice(gi * GROUP, (gi + 1) * GROUP)
            xc, rstd = _group_stats(u1[:, sl])
            u2 = (xc * rstd) * lg_ref[:, sl] + lb_ref[:, sl]
            u3_ref[:, sl] = (u2 * _sigmoid(u2)).astype(BF16)

    main_v = pl.BlockSpec((R, CT), lambda i, j: (i, j))
    main_g = pl.BlockSpec((R, CT), lambda i, j: (i, j + nc))
    halo_v = pl.BlockSpec((H, CT), _prev_halo(R, H, CT, lambda j: j))
    halo_g = pl.BlockSpec((H, CT), _prev_halo(R, H, CT, lambda j: j + nc))
    vec = pl.BlockSpec((1, CT), lambda i, j: (0, j))
    out = pl.BlockSpec((R, CT), lambda i, j: (i, j))
    return pl.pallas_call(
        body,
        name=name,
        out_shape=(jax.ShapeDtypeStruct((T, cw), BF16), jax.ShapeDtypeStruct((T, cw), F32)),
        grid=(T // R, nc),
        in_specs=[main_v, main_g, halo_v, halo_g, pl.BlockSpec((KA_ROWS, CT), lambda i, j: (0, j)), vec, vec, vec],
        out_specs=(out, out),
        scratch_shapes=[pltpu.VMEM((R + H, CT), F32)],
        compiler_params=_params("parallel", "parallel"),
    )(proj, proj, proj, proj, wa, ba, lg, lb)


def _mix_a_bwd(proj, u1, dmix, wa, lg, lb, *, cw, name):
    T = proj.shape[0]
    R = _pick(T, 128, HALO_A)
    CT = _pick(cw, 256)
    nc = cw // CT
    H = HALO_A
    n_t = T // R
    NG = CT // GROUP

    def body(av_ref, ag_ref, avh_ref, agh_ref, u1_ref, u1n_ref, d3_ref, d3n_ref, w_ref, lg_ref, lb_ref,
             dav_ref, dag_ref, dw_ref, db_ref, dlg_ref, dlb_ref, ext, d1):
        i = pl.program_id(1)
        last = i == n_t - 1

        def ln_bwd(u1, d3, sl):
            xc, rstd = _group_stats(u1)
            xh = xc * rstd
            g = lg_ref[:, sl]
            u2 = xh * g + lb_ref[:, sl]
            sg = _sigmoid(u2)
            du2 = d3 * (sg * (1.0 + u2 * (1.0 - sg)))
            dxh = du2 * g
            du1 = rstd * (dxh - jnp.mean(dxh, axis=-1, keepdims=True) - xh * jnp.mean(dxh * xh, axis=-1, keepdims=True))
            return du1, du2 * xh, du2

        dlg_parts, dlb_parts = [], []
        for gi in range(NG):
            sl = slice(gi * GROUP, (gi + 1) * GROUP)
            du1, dlg, dlb = ln_bwd(u1_ref[:, sl], d3_ref[:, sl], sl)
            d1[pl.ds(0, R), sl] = du1
            dlg_parts.append(jnp.sum(dlg, axis=0, keepdims=True))
            dlb_parts.append(jnp.sum(dlb, axis=0, keepdims=True))
            du1n, _, _ = ln_bwd(u1n_ref[:, sl], jnp.where(last, 0.0, d3n_ref[:, sl]), sl)
            d1[pl.ds(R, H), sl] = jnp.where(last, 0.0, du1n)
        dlg_row = jnp.concatenate(dlg_parts, axis=1) if NG > 1 else dlg_parts[0]
        dlb_row = jnp.concatenate(dlb_parts, axis=1) if NG > 1 else dlb_parts[0]
        d1_main = d1[pl.ds(0, R), :]
        db_row = jnp.sum(d1_main, axis=0, keepdims=True)

        @pl.when(i == 0)
        def _():
            dlg_ref[...] = dlg_row
            dlb_ref[...] = dlb_row
            db_ref[...] = db_row

        @pl.when(i > 0)
        def _():
            dlg_ref[...] += dlg_row
            dlb_ref[...] += dlb_row
            db_ref[...] += db_row

        w = w_ref[...]
        du0 = d1[pl.ds(KA - 1, R), :] * w[0:1, :]
        for k in range(1, KA):
            du0 = du0 + d1[pl.ds(KA - 1 - k, R), :] * w[k:k + 1, :]
        av = av_ref[...]
        sg = _sigmoid(ag_ref[...])
        dav_ref[...] = (du0 * sg).astype(BF16)
        dag_ref[...] = (du0 * av * sg * (1.0 - sg)).astype(BF16)

        ext[pl.ds(0, H), :] = jnp.where(i > 0, avh_ref[...] * _sigmoid(agh_ref[...]), 0.0)
        ext[pl.ds(H, R), :] = av * sg
        rows = [jnp.sum(d1_main * ext[pl.ds(H - (KA - 1) + k, R), :], axis=0, keepdims=True) for k in range(KA)]
        rows.append(jnp.zeros((KA_ROWS - KA, CT), F32))
        _acc_rows_block(dw_ref, i, rows)

    cv = lambda j: j
    cg = lambda j: j + nc
    main_v = pl.BlockSpec((R, CT), lambda j, i: (i, j))
    main_g = pl.BlockSpec((R, CT), lambda j, i: (i, j + nc))
    prev_v = pl.BlockSpec((H, CT), _swap(_prev_halo(R, H, CT, cv)))
    prev_g = pl.BlockSpec((H, CT), _swap(_prev_halo(R, H, CT, cg)))
    nxt = pl.BlockSpec((H, CT), _swap(_next_halo(R, H, T, cv)))
    wspec = pl.BlockSpec((KA_ROWS, CT), lambda j, i: (0, j))
    vec = pl.BlockSpec((1, CT), lambda j, i: (0, j))
    vshape = jax.ShapeDtypeStruct((1, cw), F32)
    return pl.pallas_call(
        body,
        name=name,
        out_shape=(jax.ShapeDtypeStruct((T, cw), BF16), jax.ShapeDtypeStruct((T, cw), BF16),
                   jax.ShapeDtypeStruct((KA_ROWS, cw), F32), vshape, vshape, vshape),
        grid=(nc, n_t),
        in_specs=[main_v, main_g, prev_v, prev_g, main_v, nxt, main_v, nxt, wspec, vec, vec],
        out_specs=(main_v, main_v, wspec, vec, vec, vec),
        scratch_shapes=[pltpu.VMEM((R + H, CT), F32), pltpu.VMEM((R + H, CT), F32)],
        compiler_params=_params("parallel", "arbitrary"),
    )(proj, proj, proj, proj, u1, u1, dmix, dmix, wa, lg, lb)


def _mix_b_fwd(proj, wb, *, cw, sw, name):
    T = proj.shape[0]
    R = _pick(T, 256, 8)
    CT = _pick(sw, 512)
    nb, nc, nh = (2 * cw) // CT, (2 * cw + sw) // CT, (2 * cw + 2 * sw) // CT
    H = HALO_S

    def body(b_ref, c_ref, h_ref, ch_ref, hh_ref, w_ref, v_ref, ext):
        i = pl.program_id(0)
        ext[pl.ds(0, H), :] = jnp.where(i > 0, ch_ref[...] * hh_ref[...], 0.0)
        ext[pl.ds(H, R), :] = c_ref[...] * h_ref[...]
        w = w_ref[...]
        zc = ext[pl.ds(H - 2, R), :] * w[0:1, :]
        zc = zc + ext[pl.ds(H - 1, R), :] * w[1:2, :]
        zc = zc + ext[pl.ds(H, R), :] * w[2:3, :]
        v_ref[...] = (b_ref[...] * zc).astype(BF16)

    def main(off):
        return pl.BlockSpec((R, CT), lambda i, j: (i, j + off))

    def prev(off):
        return pl.BlockSpec((H, CT), _prev_halo(R, H, CT, lambda j: j + off))

    return pl.pallas_call(
        body,
        name=name,
        out_shape=jax.ShapeDtypeStruct((T, sw), BF16),
        grid=(T // R, sw // CT),
        in_specs=[main(nb), main(nc), main(nh), prev(nc), prev(nh), pl.BlockSpec((KS_ROWS, CT), lambda i, j: (0, j))],
        out_specs=pl.BlockSpec((R, CT), lambda i, j: (i, j)),
        scratch_shapes=[pltpu.VMEM((R + H, CT), F32)],
        compiler_params=_params("parallel", "parallel"),
    )(proj, proj, proj, proj, proj, wb)


def _mix_b_bwd(proj, dmix, wb, *, cw, sw, name):
    T = proj.shape[0]
    R = _pick(T, 256, 8)
    CT = _pick(sw, 512)
    nb, nc, nh = (2 * cw) // CT, (2 * cw + sw) // CT, (2 * cw + 2 * sw) // CT
    nd = cw // CT
    H = HALO_S
    n_t = T // R

    def body(b_ref, bn_ref, c_ref, cp_ref, cn_ref, h_ref, hp_ref, hn_ref, d_ref, dn_ref, w_ref,
             db_ref, dc_ref, dhh_ref, dw_ref, ext, dze):
        i = pl.program_id(1)
        last = i == n_t - 1
        cc, hh = c_ref[...], h_ref[...]
        ext[pl.ds(0, H), :] = jnp.where(i > 0, cp_ref[...] * hp_ref[...], 0.0)
        ext[pl.ds(H, R), :] = cc * hh
        ext[pl.ds(H + R, H), :] = cn_ref[...] * hn_ref[...]
        w = w_ref[...]
        w0, w1, w2 = w[0:1, :], w[1:2, :], w[2:3, :]

        def conv(start, n):
            z = ext[pl.ds(start + H - 2, n), :] * w0
            z = z + ext[pl.ds(start + H - 1, n), :] * w1
            return z + ext[pl.ds(start + H, n), :] * w2

        d_main = d_ref[...]
        db_ref[...] = (d_main * conv(0, R)).astype(BF16)
        dzc_main = d_main * b_ref[...]
        dze[pl.ds(0, R), :] = dzc_main
        dze[pl.ds(R, H), :] = jnp.where(last, 0.0, dn_ref[...] * bn_ref[...])
        dz = dze[pl.ds(0, R), :] * w2 + dze[pl.ds(1, R), :] * w1 + dze[pl.ds(2, R), :] * w0
        dc_ref[...] = (dz * hh).astype(BF16)
        dhh_ref[...] = (dz * cc).astype(BF16)
        rows = [jnp.sum(dzc_main * ext[pl.ds(H - 2 + k, R), :], axis=0, keepdims=True) for k in range(KS)]
        rows.append(jnp.zeros((KS_ROWS - KS, CT), F32))
        _acc_rows_block(dw_ref, i, rows)

    def main(off):
        return pl.BlockSpec((R, CT), lambda j, i: (i, j + off))

    def prev(off):
        return pl.BlockSpec((H, CT), _swap(_prev_halo(R, H, CT, lambda j: j + off)))

    def nxt(off):
        return pl.BlockSpec((H, CT), _swap(_next_halo(R, H, T, lambda j: j + off)))

    out = pl.BlockSpec((R, CT), lambda j, i: (i, j))
    wspec = pl.BlockSpec((KS_ROWS, CT), lambda j, i: (0, j))
    act = jax.ShapeDtypeStruct((T, sw), BF16)
    return pl.pallas_call(
        body,
        name=name,
        out_shape=(act, act, act, jax.ShapeDtypeStruct((KS_ROWS, sw), F32)),
        grid=(sw // CT, n_t),
        in_specs=[main(nb), nxt(nb), main(nc), prev(nc), nxt(nc), main(nh), prev(nh), nxt(nh), main(nd), nxt(nd), wspec],
        out_specs=(out, out, out, wspec),
        scratch_shapes=[pltpu.VMEM((R + 2 * H, CT), F32), pltpu.VMEM((R + H, CT), F32)],
        compiler_params=_params("parallel", "arbitrary"),
    )(proj, proj, proj, proj, proj, proj, proj, proj, dmix, dmix, wb)


def _adamw(w, g, m, v, *, name, emit_grad=False):
    Rr, Cc = w.shape
    R = _pick(Rr, max(8, ADAMW_BLOCK_BYTES // (4 * Cc)), 8)

    def body(w_ref, g_ref, m_ref, v_ref, d_ref, mo_ref, vo_ref, *g_out):
        g = g_ref[...]
        m2 = ADAM_B1 * m_ref[...] + (1.0 - ADAM_B1) * g
        v2 = ADAM_B2 * v_ref[...] + (1.0 - ADAM_B2) * (g * g)
        m_hat = m2 / (1.0 - ADAM_B1 ** ADAM_STEP)
        v_hat = v2 / (1.0 - ADAM_B2 ** ADAM_STEP)
        d_ref[...] = -ADAM_LR * (m_hat / (jnp.sqrt(v_hat) + ADAM_EPS) + ADAM_WD * w_ref[...])
        mo_ref[...] = m2
        vo_ref[...] = v2
        if emit_grad:
            g_out[0][...] = g

    spec = pl.BlockSpec((R, Cc), lambda i: (i, 0))
    shp = jax.ShapeDtypeStruct((Rr, Cc), F32)
    n_out = 4 if emit_grad else 3
    return pl.pallas_call(
        body, name=name, out_shape=(shp,) * n_out, grid=(Rr // R,), in_specs=[spec] * 4, out_specs=(spec,) * n_out,
        compiler_params=_params("parallel"),
    )(w, g, m, v)


def _half_sum(g, ra, c, *, rows_split, name):
    Rr, Cc = ra.shape
    tr, tc = _pick(Rr, 512, 16), _pick(Cc, 2048)
    nr, ncol = Rr // tr, Cc // tc
    if rows_split:
        g_map = lambda i, j, c_ref: (i + c_ref[0] * nr, j)
    else:
        g_map = lambda i, j, c_ref: (i, j + c_ref[0] * ncol)

    def body(c_ref, g_ref, ra_ref, o_ref):
        o_ref[...] = (g_ref[...] + ra_ref[...]).astype(BF16)

    spec = pl.BlockSpec((tr, tc), lambda i, j, c_ref: (i, j))
    return pl.pallas_call(
        body,
        name=name,
        out_shape=jax.ShapeDtypeStruct((Rr, Cc), BF16),
        grid_spec=pltpu.PrefetchScalarGridSpec(
            num_scalar_prefetch=1, grid=(nr, ncol), in_specs=[pl.BlockSpec((tr, tc), g_map), spec], out_specs=spec),
        compiler_params=_params("parallel", "parallel"),
    )(c, g, ra)


def _sum_chips(cs, rb, pc, lay, *, name):
    _, Rr, Cc = rb.shape
    tr, tc = _pick(Rr, 512, 16), _pick(Cc, 2048)
    nr, ncol = Rr // tr, Cc // tc
    if lay.axis == 0:
        own_map = lambda i, j, s: (i + s[0] * nr, j)
        out_map = lambda i, j, s: (i, j + s[1] * ncol)
    else:
        own_map = lambda i, j, s: (i, j + s[0] * ncol)
        out_map = lambda i, j, s: (i + s[1] * nr, j)

    def body(s_ref, own_ref, rb_ref, o_ref):
        acc = own_ref[...].astype(F32)
        for j in range(N_CHIPS - 1):
            acc = acc + rb_ref[j].astype(F32)
        o_ref[...] = acc

    return pl.pallas_call(
        body,
        name=name,
        out_shape=jax.ShapeDtypeStruct(lay.shard_shape(), F32),
        grid_spec=pltpu.PrefetchScalarGridSpec(
            num_scalar_prefetch=1,
            grid=(nr, ncol),
            in_specs=[pl.BlockSpec((tr, tc), own_map), pl.BlockSpec((N_CHIPS - 1, tr, tc), lambda i, j, s: (0, i, j))],
            out_specs=pl.BlockSpec((tr, tc), out_map)),
        compiler_params=_params("parallel", "parallel"),
    )(pc, cs, rb)


def _place():
    x, y, c = lax.axis_index("x"), lax.axis_index("y"), lax.axis_index("c")
    return x, y, c, 2 * x + y


def _other_chips(x, y):
    return [(1 - x, y, 2 * (1 - x) + y), (x, 1 - y, 2 * x + (1 - y)), (1 - x, 1 - y, 2 * (1 - x) + (1 - y))]


def _allgather_small(buf, *, name, reduce):
    S = buf.shape[0]

    def body(x_ref, o_ref, gat, send_sems, recv_sems):
        x, y, c, _ = _place()
        me = 4 * x + 2 * y + c
        gat[me] = x_ref[...]
        copies = []
        for k in range(1, N_DEV):
            fx, fy, fc = (k >> 2) & 1, (k >> 1) & 1, k & 1
            px = 1 - x if fx else x
            py = 1 - y if fy else y
            pc = 1 - c if fc else c
            peer = 4 * px + 2 * py + pc
            send = pltpu.make_async_remote_copy(
                src_ref=x_ref, dst_ref=gat.at[me], send_sem=send_sems.at[k - 1], recv_sem=recv_sems.at[k - 1],
                device_id=(px, py, pc), device_id_type=MESH)
            send.start()
            arrival = pltpu.make_async_remote_copy(
                src_ref=x_ref, dst_ref=gat.at[peer], send_sem=send_sems.at[k - 1], recv_sem=recv_sems.at[k - 1],
                device_id=(px, py, pc), device_id_type=MESH)
            copies.append((send, arrival))
        for send, arrival in copies:
            arrival.wait_recv()
        for send, arrival in copies:
            send.wait_send()
        if reduce:
            acc = gat[0]
            for d in range(1, N_DEV):
                acc = acc + gat[d]
            o_ref[...] = acc
        else:
            o_ref[...] = gat[...]

    out_shape = jax.ShapeDtypeStruct((S, LANES) if reduce else (N_DEV, S, LANES), F32)
    return pl.pallas_call(
        body,
        name=name,
        out_shape=out_shape,
        in_specs=[pl.BlockSpec(memory_space=pltpu.VMEM)],
        out_specs=pl.BlockSpec(memory_space=pltpu.VMEM),
        scratch_shapes=[pltpu.VMEM((N_DEV, S, LANES), F32), pltpu.SemaphoreType.DMA((N_DEV - 1,)),
                        pltpu.SemaphoreType.DMA((N_DEV - 1,))],
        compiler_params=pltpu.CompilerParams(vmem_limit_bytes=VMEM_LIMIT),
    )(buf)


class _Sharded:
    def __init__(self, shape, axis):
        self.shape = shape
        self.axis = axis
        self.block = shape[axis] // N_CHIPS
        self.half = shape[1 - axis] // 2

    def _sl(self, along, across):
        return (along, across) if self.axis == 0 else (across, along)

    def block_slice(self, q):
        return self._sl(pl.ds(q * self.block, self.block), pl.ds(0, self.shape[1 - self.axis]))

    def block_half_slice(self, q, c):
        return self._sl(pl.ds(q * self.block, self.block), pl.ds(c * self.half, self.half))

    def half_slice(self, c):
        return self._sl(pl.ds(0, self.shape[self.axis]), pl.ds(c * self.half, self.half))

    def shard_half_slice(self, c):
        return self._sl(pl.ds(0, self.block), pl.ds(c * self.half, self.half))

    def shard_shape(self):
        return self._sl(self.block, self.shape[1 - self.axis])

    def half_shape(self):
        return self._sl(self.shape[self.axis], self.half)

    def block_half_shape(self):
        return self._sl(self.block, self.half)

    def block_in_half_slice(self, q):
        return self._sl(pl.ds(q * self.block, self.block), pl.ds(0, self.half))


def _at(ref, sl):
    return ref.at[sl[0], sl[1]]


class _Copy:
    def __init__(self, src, dst, arrive, dev):
        self.src, self.dst, self.arrive, self.dev = src, dst, arrive, dev


HBM = pl.BlockSpec(memory_space=pltpu.HBM)
SEM = pl.BlockSpec(memory_space=pltpu.SEMAPHORE)
EFFECT = pltpu.SideEffectType.DATAFLOW_SIDE_EFFECTING


def _exchange_start(srcs, land_shapes, plan, after, *, name):
    ns, nl = len(srcs), len(land_shapes)
    n_copies = len(plan([None] * ns, [None] * nl, dry=True))

    def body(*refs):
        src_refs, land_refs = refs[:ns], refs[ns:ns + nl]
        send_sems, recv_sems = refs[ns + nl + 1], refs[ns + nl + 2]
        token = refs[-1]
        for k, cp in enumerate(plan(src_refs, land_refs)):
            pltpu.make_async_remote_copy(
                src_ref=cp.src, dst_ref=cp.dst, send_sem=send_sems.at[k], recv_sem=recv_sems.at[k],
                device_id=cp.dev, device_id_type=MESH).start()
        token[...] = jnp.zeros_like(token)

    sem = pltpu.SemaphoreType.DMA((n_copies,))
    lands = [pltpu.with_memory_space_constraint(lax.empty(shp, dt), pltpu.HBM) for shp, dt in land_shapes]
    srcs = [pltpu.with_memory_space_constraint(a, pltpu.HBM) for a in srcs]
    thru = [pltpu.HBM(a.shape, a.dtype) for a in srcs + lands]
    outs = pl.pallas_call(
        body,
        name=name,
        out_shape=(sem, sem, *thru, jax.ShapeDtypeStruct((8, LANES), F32)),
        in_specs=[HBM] * (ns + nl) + [ANY],
        out_specs=(SEM, SEM, *[HBM] * (ns + nl), pl.BlockSpec(memory_space=pltpu.VMEM)),
        input_output_aliases={i: 2 + i for i in range(ns + nl)},
        compiler_params=pltpu.CompilerParams(has_side_effects=EFFECT),
    )(*srcs, *lands, after)
    return outs[0], outs[1], list(outs[2:2 + ns]), list(outs[2 + ns:2 + ns + nl]), outs[-1]


def _exchange_wait(send_sems, recv_sems, srcs, lands, plan, after, *, name):
    ns, nl = len(srcs), len(lands)
    after = tuple(after) if isinstance(after, (tuple, list)) else (after,)

    def body(*refs):
        src_refs, land_refs = refs[:ns], refs[ns:ns + nl]
        send_sems, recv_sems = refs[ns + nl], refs[ns + nl + 1]
        copies = [
            pltpu.make_async_remote_copy(
                src_ref=cp.src, dst_ref=cp.arrive, send_sem=send_sems.at[k], recv_sem=recv_sems.at[k],
                device_id=cp.dev, device_id_type=MESH)
            for k, cp in enumerate(plan(src_refs, land_refs))
        ]
        for cp in copies:
            cp.wait_recv()
        for cp in copies:
            cp.wait_send()

    thru = [pltpu.HBM(a.shape, a.dtype) for a in list(srcs) + list(lands)]
    outs = pl.pallas_call(
        body,
        name=name,
        out_shape=tuple(thru),
        in_specs=[HBM] * (ns + nl) + [SEM, SEM] + [ANY] * len(after),
        out_specs=tuple([HBM] * (ns + nl)),
        input_output_aliases={i: i for i in range(ns + nl)},
        compiler_params=pltpu.CompilerParams(has_side_effects=EFFECT),
    )(*srcs, *lands, send_sems, recv_sems, *after)
    return list(outs[:ns]), list(outs[ns:])


def _gather_plan(layouts):
    def plan(srcs, lands, dry=False):
        if dry:
            return [None] * (4 * len(layouts))
        x, y, c, p = _place()
        copies = []
        for s, g, lay in zip(srcs, lands, layouts):
            own = _at(g, lay.block_slice(p))
            copies.append(_Copy(s, own, own, (x, y, 1 - c)))
            for qx, qy, q in _other_chips(x, y):
                copies.append(_Copy(_at(s, lay.shard_half_slice(c)), _at(g, lay.block_half_slice(p, c)),
                                    _at(g, lay.block_half_slice(q, c)), (qx, qy, c)))
        return copies

    return plan


def _forward_plan(layouts):
    def plan(srcs, lands, dry=False):
        if dry:
            return [None] * (3 * len(layouts))
        x, y, c, _ = _place()
        copies = []
        for g, lay in zip(srcs, layouts):
            for qx, qy, q in _other_chips(x, y):
                landed = _at(g, lay.block_half_slice(q, c))
                copies.append(_Copy(landed, landed, _at(g, lay.block_half_slice(q, 1 - c)), (x, y, 1 - c)))
        return copies

    return plan


def _sibling_halves_plan(layouts):
    def plan(srcs, lands, dry=False):
        if dry:
            return [None] * len(layouts)
        x, y, c, _ = _place()
        return [_Copy(_at(s, lay.half_slice(1 - c)), ra, ra, (x, y, 1 - c)) for s, ra, lay in zip(srcs, lands, layouts)]

    return plan


def _chips_plan(layouts):
    def plan(srcs, lands, dry=False):
        if dry:
            return [None] * (3 * len(layouts))
        x, y, c, _ = _place()
        copies = []
        for s, rb, lay in zip(srcs, lands, layouts):
            for j, (qx, qy, q) in enumerate(_other_chips(x, y)):
                copies.append(_Copy(_at(s, lay.block_in_half_slice(q)), rb.at[j], rb.at[j], (qx, qy, c)))
        return copies

    return plan


def _rs_join_halves(shards, layouts, *, name):
    n = len(shards)

    def body(*refs):
        outs = refs[n:2 * n]
        send_sems, recv_sems = refs[2 * n:]
        x, y, c, _ = _place()
        sends = []
        for i in range(n):
            mine = _at(outs[i], layouts[i].shard_half_slice(c))
            cp = pltpu.make_async_remote_copy(
                src_ref=mine, dst_ref=mine, send_sem=send_sems.at[i], recv_sem=recv_sems.at[i],
                device_id=(x, y, 1 - c), device_id_type=MESH)
            cp.start()
            sends.append(cp)
        for i in range(n):
            theirs = _at(outs[i], layouts[i].shard_half_slice(1 - c))
            pltpu.make_async_remote_copy(
                src_ref=theirs, dst_ref=theirs, send_sem=send_sems.at[i], recv_sem=recv_sems.at[i],
                device_id=(x, y, 1 - c), device_id_type=MESH).wait_recv()
        for cp in sends:
            cp.wait_send()

    sem = pltpu.SemaphoreType.DMA
    return pl.pallas_call(
        body,
        name=name,
        out_shape=[jax.ShapeDtypeStruct(lay.shard_shape(), F32) for lay in layouts],
        in_specs=[ANY] * n,
        out_specs=[ANY] * n,
        input_output_aliases={i: i for i in range(n)},
        scratch_shapes=[sem((n,)), sem((n,))],
    )(*shards)


def _pack(arrays):
    flat = [a.reshape(-1) for a in arrays]
    sizes = [f.shape[0] for f in flat]
    total = sum(sizes)
    rows = _round_up(-(-total // LANES), 8)
    flat.append(jnp.zeros((rows * LANES - total,), F32))
    return jnp.concatenate(flat).reshape(rows, LANES)


def _unpack(buf, shapes):
    flat = buf.reshape(-1)
    out, pos = [], 0
    for shp in shapes:
        n = 1
        for d in shp:
            n *= d
        out.append(flat[pos:pos + n].reshape(shp))
        pos += n
    return out


def _pad_to(a, rows, cols):
    return jnp.pad(a, ((0, rows - a.shape[0]), (0, cols - a.shape[1])))


def kernel(x, mem, g_mix, w_in, conv_a_w, conv_a_b, ln_a_g, ln_a_b, conv_b_w, w_out, g_xattn, g_mem, w_q, w_k, w_v, w_o, g_ffn, w_gate, w_up, conv_f_w, w_down, g_final, loss_target, m_g_mix, m_w_in, m_conv_a_w, m_conv_a_b, m_ln_a_g, m_ln_a_b, m_conv_b_w, m_w_out, m_g_xattn, m_g_mem, m_w_q, m_w_k, m_w_v, m_w_o, m_g_ffn, m_w_gate, m_w_up, m_conv_f_w, m_w_down, m_g_final, v_g_mix, v_w_in, v_conv_a_w, v_conv_a_b, v_ln_a_g, v_ln_a_b, v_conv_b_w, v_w_out, v_g_xattn, v_g_mem, v_w_q, v_w_k, v_w_v, v_w_o, v_g_ffn, v_w_gate, v_w_up, v_conv_f_w, v_w_down, v_g_final):
    T, D = x.shape[1], x.shape[2]
    in_sh = w_in.shape[2]
    cw_sh = conv_a_w.shape[2]
    cw = N_CHIPS * cw_sh
    f_sh = w_gate.shape[2]
    fp = _round_up(f_sh, 256)
    F = N_CHIPS * fp
    rs = w_out.shape[1]
    c_idx = lax.axis_index("c")
    p_idx = 2 * lax.axis_index("x") + lax.axis_index("y")

    def t_(a):
        return jnp.swapaxes(a[0], 0, 1)

    big = {
        "w_in": (lambda: w_in[0].astype(BF16), _Sharded((D, N_CHIPS * in_sh), 1)),
        "w_out": (lambda: w_out[0].astype(BF16), _Sharded((N_CHIPS * rs, D), 0)),
        "w_q": (lambda: w_q[0].astype(BF16), _Sharded((D, D), 0)),
        "w_k": (lambda: w_k[0].astype(BF16), _Sharded((D, D), 0)),
        "w_v": (lambda: w_v[0].astype(BF16), _Sharded((D, D), 0)),
        "w_o": (lambda: w_o[0].astype(BF16), _Sharded((D, D), 0)),
        "w_gate": (lambda: _pad_to(t_(w_gate).astype(BF16), fp, D), _Sharded((F, D), 0)),
        "w_up": (lambda: _pad_to(t_(w_up).astype(BF16), fp, D), _Sharded((F, D), 0)),
        "w_down": (lambda: _pad_to(w_down[0].astype(BF16), fp, D), _Sharded((F, D), 0)),
    }
    names = list(big)
    lay = {k: big[k][1] for k in names}
    c_arr = c_idx.astype(jnp.int32).reshape(1)
    pc_arr = jnp.stack([p_idx, c_idx]).astype(jnp.int32)

    conv_shapes = [(KA_ROWS, cw_sh), (KS_ROWS, cw_sh), (KS_ROWS, fp)]
    conv_pack = _pack([_pad_to(conv_a_w[0], KA_ROWS, cw_sh), _pad_to(conv_b_w[0], KS_ROWS, cw_sh),
                       _pad_to(conv_f_w[0], KS_ROWS, fp)])
    conv_all = _allgather_small(conv_pack, name="allgather_conv", reduce=False)
    per_chip = [_unpack(conv_all[2 * q], conv_shapes) for q in range(N_CHIPS)]
    wa = jnp.concatenate([pc[0] for pc in per_chip], axis=1)
    wb = jnp.concatenate([pc[1] for pc in per_chip], axis=1)
    wf = jnp.concatenate([pc[2] for pc in per_chip], axis=1)

    tok = conv_all

    gather_groups = [["w_in"], ["w_out"], ["w_q", "w_k", "w_v"], ["w_o"], ["w_gate"], ["w_up"], ["w_down"]]
    gathers = []
    for gi, grp in enumerate(gather_groups):
        lays = [lay[k] for k in grp]
        plan = _gather_plan(lays)
        ssem, rsem, srcs, lands, tok = _exchange_start(
            [big[k][0]() for k in grp], [(l.shape, BF16) for l in lays], plan, tok, name=f"gather_start_{gi}")
        gathers.append((ssem, rsem, srcs, lands, plan, lays, grp))
    W = {}
    relays = {}

    def relay(gi, after):
        ssem, rsem, srcs, lands, plan, lays, grp = gathers[gi]
        _, lands = _exchange_wait(ssem, rsem, srcs, lands, plan, after, name=f"gather_wait_{gi}")
        plan = _forward_plan(lays)
        ssem, rsem, lands, _, _ = _exchange_start(lands, [], plan, lands[0], name=f"gather_forward_start_{gi}")
        relays[gi] = (ssem, rsem, lands, plan, grp)

    def gathered(gi, after):
        if gi not in relays:
            relay(gi, after)
        ssem, rsem, lands, plan, grp = relays[gi]
        lands, _ = _exchange_wait(ssem, rsem, lands, [], plan, after, name=f"gather_forward_wait_{gi}")
        W.update(zip(grp, lands))

    relay(0, tok)

    xs, tgt, mems = x[0], loss_target[0], mem[0]
    g_mem2, g_final2 = g_mem[None, :], g_final[None, :]

    memn, rm = _norm_fwd(mems, g_mem2, name="norm_mem")
    xn1, r1 = _norm_fwd(xs, g_mix, name="norm_mix")
    gathered(0, xn1)
    proj = _mm(xn1, W["w_in"], name="mm_in")
    relay(1, proj)
    u3, u1 = _mix_a_fwd(proj, wa, conv_a_b, ln_a_g, ln_a_b, cw=cw, name="mix_a_fwd")
    vb = _mix_b_fwd(proj, wb, cw=cw, sw=cw, name="mix_b_fwd")
    mix = jnp.concatenate([u3, vb], axis=1)
    gathered(1, mix)
    relay(2, mix)
    h1 = _mm(mix, W["w_out"], add=xs, name="mm_out")
    xn2, r2 = _norm_fwd(h1, g_xattn, name="norm_xattn")
    gathered(2, xn2)
    relay(3, xn2)
    q = _mm(xn2, W["w_q"], out_dtype=BF16, name="mm_q")
    k = _mm(memn, W["w_k"], out_dtype=BF16, name="mm_k")
    vm = _mm(memn, W["w_v"], out_dtype=BF16, name="mm_v")
    o = _attn_fwd(q, k, vm, name="attn_fwd")
    gathered(3, o)
    relay(4, o)
    h2 = _mm(o, W["w_o"], add=h1, name="mm_o")
    xn3, r3 = _norm_fwd(h2, g_ffn, name="norm_ffn")
    gathered(4, xn3)
    gp = _mm(xn3, W["w_gate"], tb=True, name="mm_gate")
    gathered(5, gp)
    up = _mm(xn3, W["w_up"], tb=True, name="mm_up")
    f = _ffn_act_fwd(gp, up, wf, name="ffn_act_fwd")
    gathered(6, f)
    h3 = _mm(f, W["w_down"], add=h2, name="mm_down")

    G = {}

    def siblings_start(tag, grp, after):
        lays = [lay[k] for k in grp]
        plan = _sibling_halves_plan(lays)
        ssem, rsem, srcs, lands, token = _exchange_start(
            [G[k] for k in grp], [(l.half_shape(), F32) for l in lays], plan, after, name=f"rs_siblings_start_{tag}")
        return (tag, grp, lays, plan, ssem, rsem, srcs, lands), token

    def chips_start(state, after):
        tag, grp, lays, plan, ssem, rsem, srcs, lands = state
        srcs, lands = _exchange_wait(ssem, rsem, srcs, lands, plan, after, name=f"rs_siblings_wait_{tag}")
        sums = [_half_sum(g, ra, c_arr, rows_split=(l.axis == 1), name=f"half_sum_{k}")
                for k, g, ra, l in zip(grp, srcs, lands, lays)]
        plan = _chips_plan(lays)
        ssem, rsem, sums, lands, token = _exchange_start(
            sums, [((N_CHIPS - 1,) + l.block_half_shape(), BF16) for l in lays], plan, sums[-1], name=f"rs_chips_start_{tag}")
        return (tag, grp, lays, plan, ssem, rsem, sums, lands), token

    def chips_finish(state, after):
        tag, grp, lays, plan, ssem, rsem, sums, lands = state
        sums, lands = _exchange_wait(ssem, rsem, sums, lands, plan, after, name=f"rs_chips_wait_{tag}")
        halves = [_sum_chips(cs, rb, pc_arr, l, name=f"sum_chips_{k}") for k, cs, rb, l in zip(grp, sums, lands, lays)]
        return dict(zip(grp, _rs_join_halves(halves, lays, name=f"rs_join_{tag}")))

    loss_rows, dh3, dh3b, dg_final = _loss_head(h3, g_final2, tgt, name="loss_head")
    df = _mm(dh3b, W["w_down"], tb=True, out_dtype=BF16, name="mm_d_f")
    G["w_down"] = _mm(f, dh3b, ta=True, name="mm_dw_down")
    rs_down, tok = siblings_start("down", ["w_down"], tok)
    dgp, dup, dwf = _ffn_act_bwd(gp, up, df, wf, name="ffn_act_bwd")
    G["w_gate"] = _mm(dgp, xn3, ta=True, deps=[tok], name="mm_dw_gate")
    G["w_up"] = _mm(dup, xn3, ta=True, name="mm_dw_up")
    rs_ffn, tok = siblings_start("ffn", ["w_gate", "w_up"], tok)
    rs_down, tok = chips_start(rs_down, tok)
    dxn3 = _mm(dgp, W["w_gate"], deps=[tok], name="mm_dxn3_gate")
    dxn3 = _mm(dup, W["w_up"], add=dxn3, name="mm_dxn3_up")
    dh2, dh2b, dg_ffn = _norm_bwd(h2, r3, g_ffn, dxn3, dh3, name="norm_ffn_bwd")
    rs_ffn, tok = chips_start(rs_ffn, dh2b)
    do = _mm(dh2b, W["w_o"], tb=True, out_dtype=BF16, deps=[tok], name="mm_d_o")
    G["w_o"] = _mm(o, dh2b, ta=True, name="mm_dw_o")
    dq, dk, dvm = _attn_bwd(q, k, vm, do, name="attn_bwd")
    dkb, dvb = dk.astype(BF16), dvm.astype(BF16)
    G["w_q"] = _mm(xn2, dq, ta=True, name="mm_dw_q")
    G["w_k"] = _mm(memn, dkb, ta=True, name="mm_dw_k")
    G["w_v"] = _mm(memn, dvb, ta=True, name="mm_dw_v")
    rs_att, tok = siblings_start("att", ["w_o", "w_q", "w_k", "w_v"], tok)
    dxn2 = _mm(dq, W["w_q"], tb=True, deps=[tok], name="mm_dxn2")
    dmemn = _mm(dkb, W["w_k"], tb=True, name="mm_dmem_k")
    dmemn = _mm(dvb, W["w_v"], tb=True, add=dmemn, name="mm_dmem_v")
    dg_mem = _norm_bwd(mems, rm, g_mem2, dmemn, None, name="norm_mem_bwd", want_dh=False)
    dh1, dh1b, dg_xattn = _norm_bwd(h1, r2, g_xattn, dxn2, dh2, name="norm_xattn_bwd")
    rs_att, tok = chips_start(rs_att, dh1b)
    dmix = _mm(dh1b, W["w_out"], tb=True, deps=[tok], name="mm_d_mix")
    G["w_out"] = _mm(mix, dh1b, ta=True, name="mm_dw_out")
    rs_out, tok = siblings_start("out", ["w_out"], tok)
    dav, dag, dwa, dba, dlg, dlb = _mix_a_bwd(proj, u1, dmix, wa, ln_a_g, ln_a_b, cw=cw, name="mix_a_bwd")
    dbg, dcg, dbh, dwb = _mix_b_bwd(proj, dmix, wb, cw=cw, sw=cw, name="mix_b_bwd")
    dproj = jnp.concatenate([dav, dag, dbg, dcg, dbh], axis=1)
    rs_out, tok = chips_start(rs_out, dproj)
    G["w_in"] = _mm(xn1, dproj, ta=True, deps=[tok], name="mm_dw_in")
    rs_in, tok = siblings_start("in", ["w_in"], tok)
    dxn1 = _mm(dproj, W["w_in"], tb=True, deps=[tok], name="mm_dxn1")
    dx, _, dg_mix = _norm_bwd(xs, r1, g_mix, dxn1, dh1, name="norm_mix_bwd")
    rs_in, tok = chips_start(rs_in, dx)

    loss_part = jnp.sum(loss_rows).reshape(1, 1)
    small_parts = [dg_mix, dba, dlg, dlb, dg_xattn, dg_mem, dg_ffn, dg_final, dwa, dwb, dwf, loss_part]
    small_shapes = [a.shape for a in small_parts]
    reduced = _allgather_small(_pack(small_parts), name="allreduce_small", reduce=True)
    (sg_mix, sba, slg, slb, sg_xattn, sg_mem, sg_ffn, sg_final, swa, swb, swf, loss_sum) = _unpack(reduced, small_shapes)
    loss = loss_sum.reshape(())
    ga_w = lax.dynamic_slice(swa, (0, p_idx * cw_sh), (KA, cw_sh))
    gb_w = lax.dynamic_slice(swb, (0, p_idx * cw_sh), (KS, cw_sh))
    gf_w = lax.dynamic_slice(swf, (0, p_idx * fp), (KS, f_sh))

    weights = dict(g_mix=g_mix, w_in=w_in, conv_a_w=conv_a_w, conv_a_b=conv_a_b, ln_a_g=ln_a_g, ln_a_b=ln_a_b,
                   conv_b_w=conv_b_w, w_out=w_out, g_xattn=g_xattn, g_mem=g_mem, w_q=w_q, w_k=w_k, w_v=w_v, w_o=w_o,
                   g_ffn=g_ffn, w_gate=w_gate, w_up=w_up, conv_f_w=conv_f_w, w_down=w_down, g_final=g_final)
    m_in = dict(g_mix=m_g_mix, w_in=m_w_in, conv_a_w=m_conv_a_w, conv_a_b=m_conv_a_b, ln_a_g=m_ln_a_g, ln_a_b=m_ln_a_b,
                conv_b_w=m_conv_b_w, w_out=m_w_out, g_xattn=m_g_xattn, g_mem=m_g_mem, w_q=m_w_q, w_k=m_w_k, w_v=m_w_v,
                w_o=m_w_o, g_ffn=m_g_ffn, w_gate=m_w_gate, w_up=m_w_up, conv_f_w=m_conv_f_w, w_down=m_w_down,
                g_final=m_g_final)
    v_in = dict(g_mix=v_g_mix, w_in=v_w_in, conv_a_w=v_conv_a_w, conv_a_b=v_conv_a_b, ln_a_g=v_ln_a_g, ln_a_b=v_ln_a_b,
                conv_b_w=v_conv_b_w, w_out=v_w_out, g_xattn=v_g_xattn, g_mem=v_g_mem, w_q=v_w_q, w_k=v_w_k, w_v=v_w_v,
                w_o=v_w_o, g_ffn=v_g_ffn, w_gate=v_w_gate, w_up=v_w_up, conv_f_w=v_conv_f_w, w_down=v_w_down,
                g_final=v_g_final)
    order = list(weights)
    grads = dict(g_mix=sg_mix, conv_a_w=ga_w, conv_a_b=sba, ln_a_g=slg, ln_a_b=slb, conv_b_w=gb_w, g_xattn=sg_xattn,
                 g_mem=sg_mem, g_ffn=sg_ffn, conv_f_w=gf_w, g_final=sg_final)
    grads = {k: g.reshape(weights[k].shape) for k, g in grads.items()}

    delta, new_m, new_v = {}, {}, {}
    small = [k for k in order if k not in big]
    small_shapes = [weights[k].shape for k in small]
    packed = [_pack([src[k] for k in small]) for src in (weights, grads, m_in, v_in)]
    d_, m_, v_ = _adamw(*packed, name="adamw_small")
    for k, dd, mm_, vv in zip(small, _unpack(d_, small_shapes), _unpack(m_, small_shapes), _unpack(v_, small_shapes)):
        delta[k], new_m[k], new_v[k] = dd, mm_, vv

    transposed = ("w_gate", "w_up")
    after = (d_, tok)
    for state in (rs_down, rs_ffn, rs_att, rs_out, rs_in):
        for k, g in chips_finish(state, after).items():
            view = t_ if k in transposed else (lambda a: a[0])
            back = (lambda a: jnp.swapaxes(a, 0, 1)[None]) if k in transposed else (lambda a: a[None])
            padded = g.shape != view(weights[k]).shape
            outs = _adamw(view(weights[k]), g, view(m_in[k]), view(v_in[k]), emit_grad=padded, name=f"adamw_{k}")
            delta[k], new_m[k], new_v[k] = back(outs[0]), back(outs[1]), back(outs[2])
            grads[k] = back(outs[3] if padded else g)
            after = outs[0]

    return (loss, dx[None], *[grads[k] for k in order], *[delta[k] for k in order],
            *[new_m[k] for k in order], *[new_v[k] for k in order])
```

```python
import functools

import jax
import jax.numpy as jnp
from jax import lax
from jax.experimental import pallas as pl
from jax.experimental.pallas import tpu as pltpu

F32 = jnp.float32
BF16 = jnp.bfloat16
EPS = 1e-6
N_HEADS = 4
GROUP = 128
KA = 31
KS = 3
KA_ROWS = 32
KS_ROWS = 8
HALO_A = 32
HALO_S = 8
CHUNK_ROWS = 16
CHUNK_COLS = 256
N_CHIPS = 4
N_DEV = 8
LANES = 128
VMEM_LIMIT = 56 * 1024 * 1024
ADAMW_BLOCK_BYTES = 1 << 20
MESH = pl.DeviceIdType.MESH
ANY = pl.BlockSpec(memory_space=pl.ANY)

ADAM_LR = 0.001
ADAM_B1 = 0.9
ADAM_B2 = 0.999
ADAM_EPS = 1e-08
ADAM_WD = 0.01
ADAM_STEP = 10


def _pick(dim, pref, mult=LANES):
    if dim <= pref:
        return dim
    t = (pref // mult) * mult
    while t >= mult:
        if dim % t == 0:
            return t
        t -= mult
    return dim


def _round_up(n, m):
    return ((n + m - 1) // m) * m


def _params(*sem):
    return pltpu.CompilerParams(dimension_semantics=sem, vmem_limit_bytes=VMEM_LIMIT)


def _sigmoid(x):
    return jax.nn.sigmoid(x)


def _mm(a, b, *, name, ta=False, tb=False, out_dtype=F32, add=None, deps=(), tm=1024, tn=512, tk=5632):
    if ta:
        K, M = a.shape
    else:
        M, K = a.shape
    if tb:
        N, K2 = b.shape
    else:
        K2, N = b.shape
    assert K == K2, (a.shape, b.shape)
    tm, tn, tk = _pick(M, tm), _pick(N, tn), _pick(K, tk)
    nk = K // tk
    a_spec = pl.BlockSpec((tk, tm), lambda i, j, k: (k, i)) if ta else pl.BlockSpec((tm, tk), lambda i, j, k: (i, k))
    b_spec = pl.BlockSpec((tn, tk), lambda i, j, k: (j, k)) if tb else pl.BlockSpec((tk, tn), lambda i, j, k: (k, j))
    o_spec = pl.BlockSpec((tm, tn), lambda i, j, k: (i, j))
    dims = (((0,) if ta else (1,), (1,) if tb else (0,)), ((), ()))
    has_add = add is not None

    def body(*refs):
        a_ref, b_ref = refs[:2]
        add_ref = refs[2] if has_add else None
        o_ref = refs[(3 if has_add else 2) + len(deps)]
        acc_ref = refs[-1] if nk > 1 else None
        k = pl.program_id(2)
        part = lax.dot_general(a_ref[...], b_ref[...], dims, preferred_element_type=F32)

        def finish(r):
            if add_ref is not None:
                r = add_ref[...] + r
            o_ref[...] = r.astype(out_dtype)

        if nk == 1:
            finish(part)
        else:
            @pl.when(k == 0)
            def _():
                acc_ref[...] = part

            @pl.when(jnp.logical_and(k > 0, k < nk - 1))
            def _():
                acc_ref[...] += part

            @pl.when(k == nk - 1)
            def _():
                finish(acc_ref[...] + part)

    in_specs = [a_spec, b_spec] + ([o_spec] if has_add else []) + [ANY] * len(deps)
    args = (a, b) + ((add,) if has_add else ()) + tuple(deps)
    return pl.pallas_call(
        body,
        name=name,
        out_shape=jax.ShapeDtypeStruct((M, N), out_dtype),
        grid=(M // tm, N // tn, nk),
        in_specs=in_specs,
        out_specs=o_spec,
        scratch_shapes=[pltpu.VMEM((tm, tn), F32)] if nk > 1 else [],
        compiler_params=_params("parallel", "parallel", "arbitrary"),
    )(*args)


def _norm_fwd(h, g, *, name):
    T, D = h.shape
    R = _pick(T, 256, 8)

    def body(h_ref, g_ref, xn_ref, r_ref):
        x = h_ref[...]
        r = lax.rsqrt(jnp.mean(x * x, axis=-1, keepdims=True) + EPS)
        xn_ref[...] = ((x * r) * g_ref[...]).astype(BF16)
        r_ref[...] = r

    return pl.pallas_call(
        body,
        name=name,
        out_shape=(jax.ShapeDtypeStruct((T, D), BF16), jax.ShapeDtypeStruct((T, 1), F32)),
        grid=(T // R,),
        in_specs=[pl.BlockSpec((R, D), lambda i: (i, 0)), pl.BlockSpec((1, D), lambda i: (0, 0))],
        out_specs=(pl.BlockSpec((R, D), lambda i: (i, 0)), pl.BlockSpec((R, 1), lambda i: (i, 0))),
        compiler_params=_params("parallel"),
    )(h, g)


def _norm_bwd(h, r, g, dxn, dres, *, name, want_dh=True):
    T, D = h.shape
    R = _pick(T, 128, 8)
    has_res = dres is not None

    def body(*refs):
        h_ref, r_ref, g_ref, dxn_ref = refs[:4]
        pos = 4
        dres_ref = None
        if has_res:
            dres_ref = refs[pos]
            pos += 1
        if want_dh:
            dh_ref, dhb_ref, dg_ref = refs[pos:pos + 3]
        else:
            dg_ref = refs[pos]
        i = pl.program_id(0)
        rr = r_ref[...]
        hn = h_ref[...] * rr
        d = dxn_ref[...].astype(F32)
        gd = d * g_ref[...]
        part = jnp.sum(d * hn, axis=0, keepdims=True)

        @pl.when(i == 0)
        def _():
            dg_ref[...] = part

        @pl.when(i > 0)
        def _():
            dg_ref[...] += part

        if want_dh:
            dh = rr * (gd - hn * jnp.mean(gd * hn, axis=-1, keepdims=True))
            if dres_ref is not None:
                dh = dres_ref[...] + dh
            dh_ref[...] = dh
            dhb_ref[...] = dh.astype(BF16)

    row = pl.BlockSpec((R, D), lambda i: (i, 0))
    vec = pl.BlockSpec((1, D), lambda i: (0, 0))
    in_specs = [row, pl.BlockSpec((R, 1), lambda i: (i, 0)), vec, row] + ([row] if has_res else [])
    args = (h, r, g, dxn) + ((dres,) if has_res else ())
    if want_dh:
        out_shape = (jax.ShapeDtypeStruct((T, D), F32), jax.ShapeDtypeStruct((T, D), BF16), jax.ShapeDtypeStruct((1, D), F32))
        out_specs = (row, row, vec)
    else:
        out_shape = jax.ShapeDtypeStruct((1, D), F32)
        out_specs = vec
    return pl.pallas_call(
        body, name=name, out_shape=out_shape, grid=(T // R,), in_specs=in_specs, out_specs=out_specs,
        compiler_params=_params("arbitrary"),
    )(*args)


def _loss_head(h, g, tgt, *, name):
    T, D = h.shape
    R = _pick(T, 128, 8)

    def body(h_ref, g_ref, t_ref, loss_ref, dh_ref, dhb_ref, dg_ref):
        i = pl.program_id(0)
        x = h_ref[...]
        gg = g_ref[...]
        r = lax.rsqrt(jnp.mean(x * x, axis=-1, keepdims=True) + EPS)
        hn = x * r
        e = hn * gg - t_ref[...]
        loss_ref[...] = 0.5 * jnp.mean(e * e, axis=-1, keepdims=True)
        dy = e * (1.0 / D)
        gd = dy * gg
        dh = r * (gd - hn * jnp.mean(gd * hn, axis=-1, keepdims=True))
        dh_ref[...] = dh
        dhb_ref[...] = dh.astype(BF16)
        part = jnp.sum(dy * hn, axis=0, keepdims=True)

        @pl.when(i == 0)
        def _():
            dg_ref[...] = part

        @pl.when(i > 0)
        def _():
            dg_ref[...] += part

    row = pl.BlockSpec((R, D), lambda i: (i, 0))
    vec = pl.BlockSpec((1, D), lambda i: (0, 0))
    return pl.pallas_call(
        body,
        name=name,
        out_shape=(jax.ShapeDtypeStruct((T, 1), F32), jax.ShapeDtypeStruct((T, D), F32),
                   jax.ShapeDtypeStruct((T, D), BF16), jax.ShapeDtypeStruct((1, D), F32)),
        grid=(T // R,),
        in_specs=[row, vec, row],
        out_specs=(pl.BlockSpec((R, 1), lambda i: (i, 0)), row, row, vec),
        compiler_params=_params("arbitrary"),
    )(h, g, tgt)


_NT = (((1,), (1,)), ((), ()))
_TN = (((0,), (0,)), ((), ()))
_NN = (((1,), (0,)), ((), ()))


def _softmax_rows(s):
    m = jnp.max(s, axis=-1, keepdims=True)
    e = jnp.exp(s - m)
    return e / jnp.sum(e, axis=-1, keepdims=True)


def _attn_fwd(q, k, v, *, name):
    T, D = q.shape
    ML = k.shape[0]
    dh = D // N_HEADS
    scale = dh ** -0.5
    R = _pick(T, 512, 16)

    def body(q_ref, k_ref, v_ref, o_ref):
        s = lax.dot_general(q_ref[...], k_ref[...], _NT, preferred_element_type=F32) * scale
        p = _softmax_rows(s)
        o_ref[...] = lax.dot_general(p.astype(BF16), v_ref[...], _NN, preferred_element_type=F32).astype(BF16)

    qs = pl.BlockSpec((R, dh), lambda i, h: (i, h))
    ks = pl.BlockSpec((ML, dh), lambda i, h: (0, h))
    return pl.pallas_call(
        body, name=name, out_shape=jax.ShapeDtypeStruct((T, D), BF16), grid=(T // R, N_HEADS),
        in_specs=[qs, ks, ks], out_specs=qs, compiler_params=_params("parallel", "parallel"),
    )(q, k, v)


def _attn_bwd(q, k, v, do, *, name):
    T, D = q.shape
    ML = k.shape[0]
    dh = D // N_HEADS
    scale = dh ** -0.5
    R = _pick(T, 512, 16)

    def body(q_ref, k_ref, v_ref, do_ref, dq_ref, dk_ref, dv_ref):
        i = pl.program_id(1)
        qq, kk, vv, dd = q_ref[...], k_ref[...], v_ref[...], do_ref[...]
        s = lax.dot_general(qq, kk, _NT, preferred_element_type=F32) * scale
        p = _softmax_rows(s)
        dp = lax.dot_general(dd, vv, _NT, preferred_element_type=F32)
        dv_part = lax.dot_general(p.astype(BF16), dd, _TN, preferred_element_type=F32)
        ds = (p * (dp - jnp.sum(p * dp, axis=-1, keepdims=True)) * scale).astype(BF16)
        dq_ref[...] = lax.dot_general(ds, kk, _NN, preferred_element_type=F32).astype(BF16)
        dk_part = lax.dot_general(ds, qq, _TN, preferred_element_type=F32)

        @pl.when(i == 0)
        def _():
            dk_ref[...] = dk_part
            dv_ref[...] = dv_part

        @pl.when(i > 0)
        def _():
            dk_ref[...] += dk_part
            dv_ref[...] += dv_part

    qs = pl.BlockSpec((R, dh), lambda h, i: (i, h))
    ks = pl.BlockSpec((ML, dh), lambda h, i: (0, h))
    return pl.pallas_call(
        body,
        name=name,
        out_shape=(jax.ShapeDtypeStruct((T, D), BF16), jax.ShapeDtypeStruct((ML, D), F32), jax.ShapeDtypeStruct((ML, D), F32)),
        grid=(N_HEADS, T // R),
        in_specs=[qs, ks, ks, qs],
        out_specs=(qs, ks, ks),
        compiler_params=_params("parallel", "arbitrary"),
    )(q, k, v, do)


def _prev_halo(R, halo, CT, col):
    per = R // halo
    return lambda i, j: (jnp.maximum(i * per - 1, 0), col(j))


def _next_halo(R, halo, T, col):
    per = R // halo
    last = T // halo - 1
    return lambda i, j: (jnp.minimum((i + 1) * per, last), col(j))


def _swap(f):
    return lambda j, i: f(i, j)


def _ffn_act_fwd(gp, up, wf, *, name):
    T, F = gp.shape
    R = _pick(T, 256, CHUNK_ROWS)
    CT = _pick(F, 512, CHUNK_COLS)
    H = HALO_S
    RC, CC = CHUNK_ROWS, min(CHUNK_COLS, CT)

    def body(g_ref, gh_ref, u_ref, w_ref, f_ref, ext):
        i = pl.program_id(0)
        ext[pl.ds(0, H), :] = jnp.where(i > 0, gh_ref[...], 0.0)
        ext[pl.ds(H, R), :] = g_ref[...]
        for c0 in range(0, CT, CC):
            cols = pl.ds(c0, CC)
            w0, w1, w2 = [jnp.broadcast_to(w_ref[pl.ds(k, 1), cols], (RC, CC)) for k in range(KS)]
            for r0 in range(0, R, RC):
                g = ext[pl.ds(r0 + H - 2, RC), cols] * w0
                g = g + ext[pl.ds(r0 + H - 1, RC), cols] * w1
                g = g + ext[pl.ds(r0 + H, RC), cols] * w2
                f_ref[pl.ds(r0, RC), cols] = (g * _sigmoid(g) * u_ref[pl.ds(r0, RC), cols]).astype(BF16)

    main = pl.BlockSpec((R, CT), lambda i, j: (i, j))
    return pl.pallas_call(
        body,
        name=name,
        out_shape=jax.ShapeDtypeStruct((T, F), BF16),
        grid=(T // R, F // CT),
        in_specs=[main, pl.BlockSpec((H, CT), _prev_halo(R, H, CT, lambda j: j)), main,
                  pl.BlockSpec((KS_ROWS, CT), lambda i, j: (0, j))],
        out_specs=main,
        scratch_shapes=[pltpu.VMEM((R + H, CT), F32)],
        compiler_params=_params("parallel", "parallel"),
    )(gp, gp, up, wf)


def _ffn_act_bwd(gp, up, df, wf, *, name):
    T, F = gp.shape
    R = _pick(T, 256, CHUNK_ROWS)
    CT = _pick(F, 512, CHUNK_COLS)
    H = HALO_S
    HB = 16
    n_t = T // R
    RC, CC = CHUNK_ROWS, min(CHUNK_COLS, CT)

    def body(g_ref, gp_ref, gn_ref, u_ref, un_ref, d_ref, dn_ref, w_ref, dg_out, du_out, dw_ref, ext, dge):
        i = pl.program_id(1)
        last = i == n_t - 1
        ext[pl.ds(0, H), :] = jnp.where(i > 0, gp_ref[...], 0.0)
        ext[pl.ds(H, R), :] = g_ref[...]
        ext[pl.ds(H + R, H), :] = gn_ref[...]

        def dact(g, u, d):
            sg = _sigmoid(g)
            return d * u * (sg * (1.0 + g * (1.0 - sg))), d * (g * sg)

        dw_rows = [[] for _ in range(KS)]
        for c0 in range(0, CT, CC):
            cols = pl.ds(c0, CC)
            w = [jnp.broadcast_to(w_ref[pl.ds(k, 1), cols], (RC, CC)) for k in range(KS)]
            acc = [jnp.zeros((RC, CC), F32) for _ in range(KS)]
            for r0 in range(0, R, RC):
                rows = pl.ds(r0, RC)
                taps = [ext[pl.ds(r0 + H - 2 + k, RC), cols] for k in range(KS)]
                g = taps[0] * w[0] + taps[1] * w[1] + taps[2] * w[2]
                dg, du = dact(g, u_ref[rows, cols], d_ref[rows, cols].astype(F32))
                du_out[rows, cols] = du.astype(BF16)
                dge[rows, cols] = dg
                acc = [a + dg * t for a, t in zip(acc, taps)]
            taps = [ext[pl.ds(R + H - 2 + k, H), cols] for k in range(KS)]
            g = taps[0] * w[0][:H] + taps[1] * w[1][:H] + taps[2] * w[2][:H]
            d_next = jnp.where(last, 0.0, dn_ref[pl.ds(0, H), cols].astype(F32))
            dg_next, _ = dact(g, un_ref[:, cols], d_next)
            dge[pl.ds(R, H), cols] = jnp.where(last, 0.0, dg_next)
            for r0 in range(0, R, RC):
                dgp = (dge[pl.ds(r0, RC), cols] * w[2] + dge[pl.ds(r0 + 1, RC), cols] * w[1]
                       + dge[pl.ds(r0 + 2, RC), cols] * w[0])
                dg_out[pl.ds(r0, RC), cols] = dgp.astype(BF16)
            for k in range(KS):
                dw_rows[k].append(jnp.sum(acc[k], axis=0, keepdims=True))
        rows = [jnp.concatenate(r, axis=1) if len(r) > 1 else r[0] for r in dw_rows]
        rows.append(jnp.zeros((KS_ROWS - KS, CT), F32))
        _acc_rows_block(dw_ref, i, rows)

    col = lambda j: j
    main = pl.BlockSpec((R, CT), lambda j, i: (i, j))
    prev8 = pl.BlockSpec((H, CT), _swap(_prev_halo(R, H, CT, col)))
    next8 = pl.BlockSpec((H, CT), _swap(_next_halo(R, H, T, col)))
    next16 = pl.BlockSpec((HB, CT), _swap(_next_halo(R, HB, T, col)))
    wspec = pl.BlockSpec((KS_ROWS, CT), lambda j, i: (0, j))
    return pl.pallas_call(
        body,
        name=name,
        out_shape=(jax.ShapeDtypeStruct((T, F), BF16), jax.ShapeDtypeStruct((T, F), BF16), jax.ShapeDtypeStruct((KS_ROWS, F), F32)),
        grid=(F // CT, n_t),
        in_specs=[main, prev8, next8, main, next8, main, next16, wspec],
        out_specs=(main, main, wspec),
        scratch_shapes=[pltpu.VMEM((R + 2 * H, CT), F32), pltpu.VMEM((R + H, CT), F32)],
        compiler_params=_params("parallel", "arbitrary"),
    )(gp, gp, gp, up, up, df, df, wf)


def _acc_rows_block(ref, i, rows):
    *singles, pad = rows

    @pl.when(i == 0)
    def _():
        for k, row in enumerate(singles):
            ref[pl.ds(k, 1), :] = row
        ref[pl.ds(len(singles), pad.shape[0]), :] = pad

    @pl.when(i > 0)
    def _():
        for k, row in enumerate(singles):
            ref[pl.ds(k, 1), :] += row


def _group_stats(x):
    mu = jnp.mean(x, axis=-1, keepdims=True)
    xc = x - mu
    var = jnp.mean(xc * xc, axis=-1, keepdims=True)
    return xc, lax.rsqrt(var + EPS)


def _shifted_rows(rolled, x, cols):
    n = x.shape[0]
    for b in range(1, 8):
        rolled[b - 1, :, cols] = pltpu.roll(x, n - b, axis=0)


def _window(src, rolled, off, rows, cols):
    b = off % 8
    if b == 0:
        return src[pl.ds(off, rows), cols]
    return rolled[b - 1, pl.ds(off - b, rows), cols]


def _mix_a_fwd(proj, wa, ba, lg, lb, *, cw, name, deps=()):
    T = proj.shape[0]
    R = _pick(T, 128, HALO_A)
    CT = _pick(cw, 256)
    nc = cw // CT
    H = HALO_A

    def body(av_ref, ag_ref, avh_ref, agh_ref, w_ref, b_ref, lg_ref, lb_ref, *rest):
        u3_ref, u1_ref, ext, rolled = rest[len(deps):]
        i = pl.program_id(0)
        for gi in range(CT // GROUP):
            cols = pl.ds(gi * GROUP, GROUP)
            ext[pl.ds(0, H), cols] = jnp.where(i > 0, avh_ref[:, cols] * _sigmoid(agh_ref[:, cols]), 0.0)
            ext[pl.ds(H, R), cols] = av_ref[:, cols] * _sigmoid(ag_ref[:, cols])
            _shifted_rows(rolled, ext[:, cols], cols)
            acc = _window(ext, rolled, H - (KA - 1), R, cols) * w_ref[pl.ds(0, 1), cols]
            for k in range(1, KA):
                acc = acc + _window(ext, rolled, H - (KA - 1) + k, R, cols) * w_ref[pl.ds(k, 1), cols]
            u1 = acc + b_ref[:, cols]
            u1_ref[:, cols] = u1
            xc, rstd = _group_stats(u1)
            u2 = (xc * rstd) * lg_ref[:, cols] + lb_ref[:, cols]
            u3_ref[:, cols] = (u2 * _sigmoid(u2)).astype(BF16)

    main_v = pl.BlockSpec((R, CT), lambda i, j: (i, j))
    main_g = pl.BlockSpec((R, CT), lambda i, j: (i, j + nc))
    halo_v = pl.BlockSpec((H, CT), _prev_halo(R, H, CT, lambda j: j))
    halo_g = pl.BlockSpec((H, CT), _prev_halo(R, H, CT, lambda j: j + nc))
    vec = pl.BlockSpec((1, CT), lambda i, j: (0, j))
    out = pl.BlockSpec((R, CT), lambda i, j: (i, j))
    return pl.pallas_call(
        body,
        name=name,
        out_shape=(jax.ShapeDtypeStruct((T, cw), BF16), jax.ShapeDtypeStruct((T, cw), F32)),
        grid=(T // R, nc),
        in_specs=[main_v, main_g, halo_v, halo_g, pl.BlockSpec((KA_ROWS, CT), lambda i, j: (0, j)), vec, vec, vec]
        + [ANY] * len(deps),
        out_specs=(out, out),
        scratch_shapes=[pltpu.VMEM((R + H, CT), F32), pltpu.VMEM((7, R + H, CT), F32)],
        compiler_params=_params("parallel", "parallel"),
    )(proj, proj, proj, proj, wa, ba, lg, lb, *deps)


def _mix_a_bwd(proj, u1, dmix, wa, lg, lb, *, cw, name):
    T = proj.shape[0]
    R = _pick(T, 128, HALO_A)
    CT = _pick(cw, 256)
    nc = cw // CT
    H = HALO_A
    n_t = T // R
    NG = CT // GROUP

    def body(av_ref, ag_ref, avh_ref, agh_ref, u1_ref, u1n_ref, d3_ref, d3n_ref, w_ref, lg_ref, lb_ref,
             dav_ref, dag_ref, dw_ref, db_ref, dlg_ref, dlb_ref, ext, d1):
        i = pl.program_id(1)
        last = i == n_t - 1

        def ln_bwd(u1, d3, sl):
            xc, rstd = _group_stats(u1)
            xh = xc * rstd
            g = lg_ref[:, sl]
            u2 = xh * g + lb_ref[:, sl]
            sg = _sigmoid(u2)
            du2 = d3 * (sg * (1.0 + u2 * (1.0 - sg)))
            dxh = du2 * g
            du1 = rstd * (dxh - jnp.mean(dxh, axis=-1, keepdims=True) - xh * jnp.mean(dxh * xh, axis=-1, keepdims=True))
            return du1, du2 * xh, du2

        dlg_parts, dlb_parts = [], []
        for gi in range(NG):
            sl = slice(gi * GROUP, (gi + 1) * GROUP)
            du1, dlg, dlb = ln_bwd(u1_ref[:, sl], d3_ref[:, sl], sl)
            d1[pl.ds(0, R), sl] = du1
            dlg_parts.append(jnp.sum(dlg, axis=0, keepdims=True))
            dlb_parts.append(jnp.sum(dlb, axis=0, keepdims=True))
            du1n, _, _ = ln_bwd(u1n_ref[:, sl], jnp.where(last, 0.0, d3n_ref[:, sl]), sl)
            d1[pl.ds(R, H), sl] = jnp.where(last, 0.0, du1n)
        dlg_row = jnp.concatenate(dlg_parts, axis=1) if NG > 1 else dlg_parts[0]
        dlb_row = jnp.concatenate(dlb_parts, axis=1) if NG > 1 else dlb_parts[0]
        d1_main = d1[pl.ds(0, R), :]
        db_row = jnp.sum(d1_main, axis=0, keepdims=True)

        @pl.when(i == 0)
        def _():
            dlg_ref[...] = dlg_row
            dlb_ref[...] = dlb_row
            db_ref[...] = db_row

        @pl.when(i > 0)
        def _():
            dlg_ref[...] += dlg_row
            dlb_ref[...] += dlb_row
            db_ref[...] += db_row

        w = w_ref[...]
        du0 = d1[pl.ds(KA - 1, R), :] * w[0:1, :]
        for k in range(1, KA):
            du0 = du0 + d1[pl.ds(KA - 1 - k, R), :] * w[k:k + 1, :]
        av = av_ref[...]
        sg = _sigmoid(ag_ref[...])
        dav_ref[...] = (du0 * sg).astype(BF16)
        dag_ref[...] = (du0 * av * sg * (1.0 - sg)).astype(BF16)

        ext[pl.ds(0, H), :] = jnp.where(i > 0, avh_ref[...] * _sigmoid(agh_ref[...]), 0.0)
        ext[pl.ds(H, R), :] = av * sg
        rows = [jnp.sum(d1_main * ext[pl.ds(H - (KA - 1) + k, R), :], axis=0, keepdims=True) for k in range(KA)]
        rows.append(jnp.zeros((KA_ROWS - KA, CT), F32))
        _acc_rows_block(dw_ref, i, rows)

    cv = lambda j: j
    cg = lambda j: j + nc
    main_v = pl.BlockSpec((R, CT), lambda j, i: (i, j))
    main_g = pl.BlockSpec((R, CT), lambda j, i: (i, j + nc))
    prev_v = pl.BlockSpec((H, CT), _swap(_prev_halo(R, H, CT, cv)))
    prev_g = pl.BlockSpec((H, CT), _swap(_prev_halo(R, H, CT, cg)))
    nxt = pl.BlockSpec((H, CT), _swap(_next_halo(R, H, T, cv)))
    wspec = pl.BlockSpec((KA_ROWS, CT), lambda j, i: (0, j))
    vec = pl.BlockSpec((1, CT), lambda j, i: (0, j))
    vshape = jax.ShapeDtypeStruct((1, cw), F32)
    return pl.pallas_call(
        body,
        name=name,
        out_shape=(jax.ShapeDtypeStruct((T, cw), BF16), jax.ShapeDtypeStruct((T, cw), BF16),
                   jax.ShapeDtypeStruct((KA_ROWS, cw), F32), vshape, vshape, vshape),
        grid=(nc, n_t),
        in_specs=[main_v, main_g, prev_v, prev_g, main_v, nxt, main_v, nxt, wspec, vec, vec],
        out_specs=(main_v, main_v, wspec, vec, vec, vec),
        scratch_shapes=[pltpu.VMEM((R + H, CT), F32), pltpu.VMEM((R + H, CT), F32)],
        compiler_params=_params("parallel", "arbitrary"),
    )(proj, proj, proj, proj, u1, u1, dmix, dmix, wa, lg, lb)


def _mix_b_fwd(proj, wb, *, cw, sw, name):
    T = proj.shape[0]
    R = _pick(T, 256, 8)
    CT = _pick(sw, 512)
    nb, nc, nh = (2 * cw) // CT, (2 * cw + sw) // CT, (2 * cw + 2 * sw) // CT
    H = HALO_S

    def body(b_ref, c_ref, h_ref, ch_ref, hh_ref, w_ref, v_ref, ext):
        i = pl.program_id(0)
        ext[pl.ds(0, H), :] = jnp.where(i > 0, ch_ref[...] * hh_ref[...], 0.0)
        ext[pl.ds(H, R), :] = c_ref[...] * h_ref[...]
        w = w_ref[...]
        zc = ext[pl.ds(H - 2, R), :] * w[0:1, :]
        zc = zc + ext[pl.ds(H - 1, R), :] * w[1:2, :]
        zc = zc + ext[pl.ds(H, R), :] * w[2:3, :]
        v_ref[...] = (b_ref[...] * zc).astype(BF16)

    def main(off):
        return pl.BlockSpec((R, CT), lambda i, j: (i, j + off))

    def prev(off):
        return pl.BlockSpec((H, CT), _prev_halo(R, H, CT, lambda j: j + off))

    return pl.pallas_call(
        body,
        name=name,
        out_shape=jax.ShapeDtypeStruct((T, sw), BF16),
        grid=(T // R, sw // CT),
        in_specs=[main(nb), main(nc), main(nh), prev(nc), prev(nh), pl.BlockSpec((KS_ROWS, CT), lambda i, j: (0, j))],
        out_specs=pl.BlockSpec((R, CT), lambda i, j: (i, j)),
        scratch_shapes=[pltpu.VMEM((R + H, CT), F32)],
        compiler_params=_params("parallel", "parallel"),
    )(proj, proj, proj, proj, proj, wb)


def _mix_b_bwd(proj, dmix, wb, *, cw, sw, name):
    T = proj.shape[0]
    R = _pick(T, 256, 8)
    CT = _pick(sw, 512)
    nb, nc, nh = (2 * cw) // CT, (2 * cw + sw) // CT, (2 * cw + 2 * sw) // CT
    nd = cw // CT
    H = HALO_S
    n_t = T // R

    def body(b_ref, bn_ref, c_ref, cp_ref, cn_ref, h_ref, hp_ref, hn_ref, d_ref, dn_ref, w_ref,
             db_ref, dc_ref, dhh_ref, dw_ref, ext, dze):
        i = pl.program_id(1)
        last = i == n_t - 1
        cc, hh = c_ref[...], h_ref[...]
        ext[pl.ds(0, H), :] = jnp.where(i > 0, cp_ref[...] * hp_ref[...], 0.0)
        ext[pl.ds(H, R), :] = cc * hh
        ext[pl.ds(H + R, H), :] = cn_ref[...] * hn_ref[...]
        w = w_ref[...]
        w0, w1, w2 = w[0:1, :], w[1:2, :], w[2:3, :]

        def conv(start, n):
            z = ext[pl.ds(start + H - 2, n), :] * w0
            z = z + ext[pl.ds(start + H - 1, n), :] * w1
            return z + ext[pl.ds(start + H, n), :] * w2

        d_main = d_ref[...]
        db_ref[...] = (d_main * conv(0, R)).astype(BF16)
        dzc_main = d_main * b_ref[...]
        dze[pl.ds(0, R), :] = dzc_main
        dze[pl.ds(R, H), :] = jnp.where(last, 0.0, dn_ref[...] * bn_ref[...])
        dz = dze[pl.ds(0, R), :] * w2 + dze[pl.ds(1, R), :] * w1 + dze[pl.ds(2, R), :] * w0
        dc_ref[...] = (dz * hh).astype(BF16)
        dhh_ref[...] = (dz * cc).astype(BF16)
        rows = [jnp.sum(dzc_main * ext[pl.ds(H - 2 + k, R), :], axis=0, keepdims=True) for k in range(KS)]
        rows.append(jnp.zeros((KS_ROWS - KS, CT), F32))
        _acc_rows_block(dw_ref, i, rows)

    def main(off):
        return pl.BlockSpec((R, CT), lambda j, i: (i, j + off))

    def prev(off):
        return pl.BlockSpec((H, CT), _swap(_prev_halo(R, H, CT, lambda j: j + off)))

    def nxt(off):
        return pl.BlockSpec((H, CT), _swap(_next_halo(R, H, T, lambda j: j + off)))

    out = pl.BlockSpec((R, CT), lambda j, i: (i, j))
    wspec = pl.BlockSpec((KS_ROWS, CT), lambda j, i: (0, j))
    act = jax.ShapeDtypeStruct((T, sw), BF16)
    return pl.pallas_call(
        body,
        name=name,
        out_shape=(act, act, act, jax.ShapeDtypeStruct((KS_ROWS, sw), F32)),
        grid=(sw // CT, n_t),
        in_specs=[main(nb), nxt(nb), main(nc), prev(nc), nxt(nc), main(nh), prev(nh), nxt(nh), main(nd), nxt(nd), wspec],
        out_specs=(out, out, out, wspec),
        scratch_shapes=[pltpu.VMEM((R + 2 * H, CT), F32), pltpu.VMEM((R + H, CT), F32)],
        compiler_params=_params("parallel", "arbitrary"),
    )(proj, proj, proj, proj, proj, proj, proj, proj, dmix, dmix, wb)


def _adamw(w, g, m, v, *, name, emit_grad=False):
    Rr, Cc = w.shape
    R = _pick(Rr, max(8, ADAMW_BLOCK_BYTES // (4 * Cc)), 8)

    def body(w_ref, g_ref, m_ref, v_ref, d_ref, mo_ref, vo_ref, *g_out):
        g = g_ref[...]
        m2 = ADAM_B1 * m_ref[...] + (1.0 - ADAM_B1) * g
        v2 = ADAM_B2 * v_ref[...] + (1.0 - ADAM_B2) * (g * g)
        m_hat = m2 / (1.0 - ADAM_B1 ** ADAM_STEP)
        v_hat = v2 / (1.0 - ADAM_B2 ** ADAM_STEP)
        d_ref[...] = -ADAM_LR * (m_hat / (jnp.sqrt(v_hat) + ADAM_EPS) + ADAM_WD * w_ref[...])
        mo_ref[...] = m2
        vo_ref[...] = v2
        if emit_grad:
            g_out[0][...] = g

    spec = pl.BlockSpec((R, Cc), lambda i: (i, 0))
    shp = jax.ShapeDtypeStruct((Rr, Cc), F32)
    n_out = 4 if emit_grad else 3
    return pl.pallas_call(
        body, name=name, out_shape=(shp,) * n_out, grid=(Rr // R,), in_specs=[spec] * 4, out_specs=(spec,) * n_out,
        compiler_params=_params("parallel"),
    )(w, g, m, v)


def _half_sum(g, ra, c, *, rows_split, name):
    Rr, Cc = ra.shape
    tr, tc = _pick(Rr, 512, 16), _pick(Cc, 2048)
    nr, ncol = Rr // tr, Cc // tc
    if rows_split:
        g_map = lambda i, j, c_ref: (i + c_ref[0] * nr, j)
    else:
        g_map = lambda i, j, c_ref: (i, j + c_ref[0] * ncol)

    def body(c_ref, g_ref, ra_ref, o_ref):
        o_ref[...] = (g_ref[...] + ra_ref[...]).astype(BF16)

    spec = pl.BlockSpec((tr, tc), lambda i, j, c_ref: (i, j))
    return pl.pallas_call(
        body,
        name=name,
        out_shape=jax.ShapeDtypeStruct((Rr, Cc), BF16),
        grid_spec=pltpu.PrefetchScalarGridSpec(
            num_scalar_prefetch=1, grid=(nr, ncol), in_specs=[pl.BlockSpec((tr, tc), g_map), spec], out_specs=spec),
        compiler_params=_params("parallel", "parallel"),
    )(c, g, ra)


def _sum_chips(cs, rb, pc, lay, *, name):
    _, Rr, Cc = rb.shape
    tr, tc = _pick(Rr, 512, 16), _pick(Cc, 2048)
    nr, ncol = Rr // tr, Cc // tc
    if lay.axis == 0:
        own_map = lambda i, j, s: (i + s[0] * nr, j)
        out_map = lambda i, j, s: (i, j + s[1] * ncol)
    else:
        own_map = lambda i, j, s: (i, j + s[0] * ncol)
        out_map = lambda i, j, s: (i + s[1] * nr, j)

    def body(s_ref, own_ref, rb_ref, o_ref):
        acc = own_ref[...].astype(F32)
        for j in range(N_CHIPS - 1):
            acc = acc + rb_ref[j].astype(F32)
        o_ref[...] = acc

    return pl.pallas_call(
        body,
        name=name,
        out_shape=jax.ShapeDtypeStruct(lay.shard_shape(), F32),
        grid_spec=pltpu.PrefetchScalarGridSpec(
            num_scalar_prefetch=1,
            grid=(nr, ncol),
            in_specs=[pl.BlockSpec((tr, tc), own_map), pl.BlockSpec((N_CHIPS - 1, tr, tc), lambda i, j, s: (0, i, j))],
            out_specs=pl.BlockSpec((tr, tc), out_map)),
        compiler_params=_params("parallel", "parallel"),
    )(pc, cs, rb)


def _place():
    x, y, c = lax.axis_index("x"), lax.axis_index("y"), lax.axis_index("c")
    return x, y, c, 2 * x + y


def _other_chips(x, y):
    return [(1 - x, y, 2 * (1 - x) + y), (x, 1 - y, 2 * x + (1 - y)), (1 - x, 1 - y, 2 * (1 - x) + (1 - y))]


def _allgather_small(buf, *, name, reduce):
    S = buf.shape[0]

    def body(x_ref, o_ref, gat, send_sems, recv_sems):
        x, y, c, _ = _place()
        me = 4 * x + 2 * y + c
        gat[me] = x_ref[...]
        copies = []
        for k in range(1, N_DEV):
            fx, fy, fc = (k >> 2) & 1, (k >> 1) & 1, k & 1
            px = 1 - x if fx else x
            py = 1 - y if fy else y
            pc = 1 - c if fc else c
            peer = 4 * px + 2 * py + pc
            send = pltpu.make_async_remote_copy(
                src_ref=x_ref, dst_ref=gat.at[me], send_sem=send_sems.at[k - 1], recv_sem=recv_sems.at[k - 1],
                device_id=(px, py, pc), device_id_type=MESH)
            send.start()
            arrival = pltpu.make_async_remote_copy(
                src_ref=x_ref, dst_ref=gat.at[peer], send_sem=send_sems.at[k - 1], recv_sem=recv_sems.at[k - 1],
                device_id=(px, py, pc), device_id_type=MESH)
            copies.append((send, arrival))
        for send, arrival in copies:
            arrival.wait_recv()
        for send, arrival in copies:
            send.wait_send()
        if reduce:
            acc = gat[0]
            for d in range(1, N_DEV):
                acc = acc + gat[d]
            o_ref[...] = acc
        else:
            o_ref[...] = gat[...]

    out_shape = jax.ShapeDtypeStruct((S, LANES) if reduce else (N_DEV, S, LANES), F32)
    return pl.pallas_call(
        body,
        name=name,
        out_shape=out_shape,
        in_specs=[pl.BlockSpec(memory_space=pltpu.VMEM)],
        out_specs=pl.BlockSpec(memory_space=pltpu.VMEM),
        scratch_shapes=[pltpu.VMEM((N_DEV, S, LANES), F32), pltpu.SemaphoreType.DMA((N_DEV - 1,)),
                        pltpu.SemaphoreType.DMA((N_DEV - 1,))],
        compiler_params=pltpu.CompilerParams(vmem_limit_bytes=VMEM_LIMIT),
    )(buf)


class _Sharded:
    def __init__(self, shape, axis):
        self.shape = shape
        self.axis = axis
        self.block = shape[axis] // N_CHIPS
        self.half = shape[1 - axis] // 2

    def _sl(self, along, across):
        return (along, across) if self.axis == 0 else (across, along)

    def block_slice(self, q):
        return self._sl(pl.ds(q * self.block, self.block), pl.ds(0, self.shape[1 - self.axis]))

    def block_half_slice(self, q, c):
        return self._sl(pl.ds(q * self.block, self.block), pl.ds(c * self.half, self.half))

    def half_slice(self, c):
        return self._sl(pl.ds(0, self.shape[self.axis]), pl.ds(c * self.half, self.half))

    def shard_half_slice(self, c):
        return self._sl(pl.ds(0, self.block), pl.ds(c * self.half, self.half))

    def shard_shape(self):
        return self._sl(self.block, self.shape[1 - self.axis])

    def half_shape(self):
        return self._sl(self.shape[self.axis], self.half)

    def block_half_shape(self):
        return self._sl(self.block, self.half)

    def block_in_half_slice(self, q):
        return self._sl(pl.ds(q * self.block, self.block), pl.ds(0, self.half))


def _at(ref, sl):
    return ref.at[sl[0], sl[1]]


class _Copy:
    def __init__(self, src, dst, arrive, dev):
        self.src, self.dst, self.arrive, self.dev = src, dst, arrive, dev


HBM = pl.BlockSpec(memory_space=pltpu.HBM)
SEM = pl.BlockSpec(memory_space=pltpu.SEMAPHORE)
EFFECT = pltpu.SideEffectType.DATAFLOW_SIDE_EFFECTING


def _exchange_start(srcs, land_shapes, plan, after, *, name):
    ns, nl = len(srcs), len(land_shapes)
    n_copies = len(plan([None] * ns, [None] * nl, dry=True))

    def body(*refs):
        src_refs, land_refs = refs[:ns], refs[ns:ns + nl]
        send_sems, recv_sems = refs[ns + nl + 1], refs[ns + nl + 2]
        token = refs[-1]
        for k, cp in enumerate(plan(src_refs, land_refs)):
            pltpu.make_async_remote_copy(
                src_ref=cp.src, dst_ref=cp.dst, send_sem=send_sems.at[k], recv_sem=recv_sems.at[k],
                device_id=cp.dev, device_id_type=MESH).start()
        token[...] = jnp.zeros_like(token)

    sem = pltpu.SemaphoreType.DMA((n_copies,))
    lands = [pltpu.with_memory_space_constraint(lax.empty(shp, dt), pltpu.HBM) for shp, dt in land_shapes]
    srcs = [pltpu.with_memory_space_constraint(a, pltpu.HBM) for a in srcs]
    thru = [pltpu.HBM(a.shape, a.dtype) for a in srcs + lands]
    outs = pl.pallas_call(
        body,
        name=name,
        out_shape=(sem, sem, *thru, jax.ShapeDtypeStruct((8, LANES), F32)),
        in_specs=[HBM] * (ns + nl) + [ANY],
        out_specs=(SEM, SEM, *[HBM] * (ns + nl), pl.BlockSpec(memory_space=pltpu.VMEM)),
        input_output_aliases={i: 2 + i for i in range(ns + nl)},
        compiler_params=pltpu.CompilerParams(has_side_effects=EFFECT),
    )(*srcs, *lands, after)
    return outs[0], outs[1], list(outs[2:2 + ns]), list(outs[2 + ns:2 + ns + nl]), outs[-1]


def _exchange_wait(send_sems, recv_sems, srcs, lands, plan, after, *, name):
    ns, nl = len(srcs), len(lands)
    after = tuple(after) if isinstance(after, (tuple, list)) else (after,)

    def body(*refs):
        src_refs, land_refs = refs[:ns], refs[ns:ns + nl]
        send_sems, recv_sems = refs[ns + nl], refs[ns + nl + 1]
        copies = [
            pltpu.make_async_remote_copy(
                src_ref=cp.src, dst_ref=cp.arrive, send_sem=send_sems.at[k], recv_sem=recv_sems.at[k],
                device_id=cp.dev, device_id_type=MESH)
            for k, cp in enumerate(plan(src_refs, land_refs))
        ]
        for cp in copies:
            cp.wait_recv()
        for cp in copies:
            cp.wait_send()

    thru = [pltpu.HBM(a.shape, a.dtype) for a in list(srcs) + list(lands)]
    outs = pl.pallas_call(
        body,
        name=name,
        out_shape=tuple(thru),
        in_specs=[HBM] * (ns + nl) + [SEM, SEM] + [ANY] * len(after),
        out_specs=tuple([HBM] * (ns + nl)),
        input_output_aliases={i: i for i in range(ns + nl)},
        compiler_params=pltpu.CompilerParams(has_side_effects=EFFECT),
    )(*srcs, *lands, send_sems, recv_sems, *after)
    return list(outs[:ns]), list(outs[ns:])


def _gather_plan(layouts):
    def plan(srcs, lands, dry=False):
        if dry:
            return [None] * (4 * len(layouts))
        x, y, c, p = _place()
        copies = []
        for s, g, lay in zip(srcs, lands, layouts):
            own = _at(g, lay.block_slice(p))
            copies.append(_Copy(s, own, own, (x, y, 1 - c)))
            for qx, qy, q in _other_chips(x, y):
                copies.append(_Copy(_at(s, lay.shard_half_slice(c)), _at(g, lay.block_half_slice(p, c)),
                                    _at(g, lay.block_half_slice(q, c)), (qx, qy, c)))
        return copies

    return plan


def _forward_plan(layouts):
    def plan(srcs, lands, dry=False):
        if dry:
            return [None] * (3 * len(layouts))
        x, y, c, _ = _place()
        copies = []
        for g, lay in zip(srcs, layouts):
            for qx, qy, q in _other_chips(x, y):
                landed = _at(g, lay.block_half_slice(q, c))
                copies.append(_Copy(landed, landed, _at(g, lay.block_half_slice(q, 1 - c)), (x, y, 1 - c)))
        return copies

    return plan


def _sibling_halves_plan(layouts):
    def plan(srcs, lands, dry=False):
        if dry:
            return [None] * len(layouts)
        x, y, c, _ = _place()
        return [_Copy(_at(s, lay.half_slice(1 - c)), ra, ra, (x, y, 1 - c)) for s, ra, lay in zip(srcs, lands, layouts)]

    return plan


def _chips_plan(layouts):
    def plan(srcs, lands, dry=False):
        if dry:
            return [None] * (3 * len(layouts))
        x, y, c, _ = _place()
        copies = []
        for s, rb, lay in zip(srcs, lands, layouts):
            for j, (qx, qy, q) in enumerate(_other_chips(x, y)):
                copies.append(_Copy(_at(s, lay.block_in_half_slice(q)), rb.at[j], rb.at[j], (qx, qy, c)))
        return copies

    return plan


def _rs_join_halves(shards, layouts, *, name):
    n = len(shards)

    def body(*refs):
        outs = refs[n:2 * n]
        send_sems, recv_sems = refs[2 * n:]
        x, y, c, _ = _place()
        sends = []
        for i in range(n):
            mine = _at(outs[i], layouts[i].shard_half_slice(c))
            cp = pltpu.make_async_remote_copy(
                src_ref=mine, dst_ref=mine, send_sem=send_sems.at[i], recv_sem=recv_sems.at[i],
                device_id=(x, y, 1 - c), device_id_type=MESH)
            cp.start()
            sends.append(cp)
        for i in range(n):
            theirs = _at(outs[i], layouts[i].shard_half_slice(1 - c))
            pltpu.make_async_remote_copy(
                src_ref=theirs, dst_ref=theirs, send_sem=send_sems.at[i], recv_sem=recv_sems.at[i],
                device_id=(x, y, 1 - c), device_id_type=MESH).wait_recv()
        for cp in sends:
            cp.wait_send()

    sem = pltpu.SemaphoreType.DMA
    return pl.pallas_call(
        body,
        name=name,
        out_shape=[jax.ShapeDtypeStruct(lay.shard_shape(), F32) for lay in layouts],
        in_specs=[ANY] * n,
        out_specs=[ANY] * n,
        input_output_aliases={i: i for i in range(n)},
        scratch_shapes=[sem((n,)), sem((n,))],
    )(*shards)


def _pack(arrays):
    flat = [a.reshape(-1) for a in arrays]
    sizes = [f.shape[0] for f in flat]
    total = sum(sizes)
    rows = _round_up(-(-total // LANES), 8)
    flat.append(jnp.zeros((rows * LANES - total,), F32))
    return jnp.concatenate(flat).reshape(rows, LANES)


def _unpack(buf, shapes):
    flat = buf.reshape(-1)
    out, pos = [], 0
    for shp in shapes:
        n = 1
        for d in shp:
            n *= d
        out.append(flat[pos:pos + n].reshape(shp))
        pos += n
    return out


def _pad_to(a, rows, cols):
    return jnp.pad(a, ((0, rows - a.shape[0]), (0, cols - a.shape[1])))


def kernel(x, mem, g_mix, w_in, conv_a_w, conv_a_b, ln_a_g, ln_a_b, conv_b_w, w_out, g_xattn, g_mem, w_q, w_k, w_v, w_o, g_ffn, w_gate, w_up, conv_f_w, w_down, g_final, loss_target, m_g_mix, m_w_in, m_conv_a_w, m_conv_a_b, m_ln_a_g, m_ln_a_b, m_conv_b_w, m_w_out, m_g_xattn, m_g_mem, m_w_q, m_w_k, m_w_v, m_w_o, m_g_ffn, m_w_gate, m_w_up, m_conv_f_w, m_w_down, m_g_final, v_g_mix, v_w_in, v_conv_a_w, v_conv_a_b, v_ln_a_g, v_ln_a_b, v_conv_b_w, v_w_out, v_g_xattn, v_g_mem, v_w_q, v_w_k, v_w_v, v_w_o, v_g_ffn, v_w_gate, v_w_up, v_conv_f_w, v_w_down, v_g_final):
    T, D = x.shape[1], x.shape[2]
    in_sh = w_in.shape[2]
    cw_sh = conv_a_w.shape[2]
    cw = N_CHIPS * cw_sh
    f_sh = w_gate.shape[2]
    fp = _round_up(f_sh, 256)
    F = N_CHIPS * fp
    rs = w_out.shape[1]
    c_idx = lax.axis_index("c")
    p_idx = 2 * lax.axis_index("x") + lax.axis_index("y")

    def t_(a):
        return jnp.swapaxes(a[0], 0, 1)

    big = {
        "w_in": (lambda: w_in[0].astype(BF16), _Sharded((D, N_CHIPS * in_sh), 1)),
        "w_out": (lambda: w_out[0].astype(BF16), _Sharded((N_CHIPS * rs, D), 0)),
        "w_q": (lambda: w_q[0].astype(BF16), _Sharded((D, D), 0)),
        "w_k": (lambda: w_k[0].astype(BF16), _Sharded((D, D), 0)),
        "w_v": (lambda: w_v[0].astype(BF16), _Sharded((D, D), 0)),
        "w_o": (lambda: w_o[0].astype(BF16), _Sharded((D, D), 0)),
        "w_gate": (lambda: _pad_to(t_(w_gate).astype(BF16), fp, D), _Sharded((F, D), 0)),
        "w_up": (lambda: _pad_to(t_(w_up).astype(BF16), fp, D), _Sharded((F, D), 0)),
        "w_down": (lambda: _pad_to(w_down[0].astype(BF16), fp, D), _Sharded((F, D), 0)),
    }
    names = list(big)
    lay = {k: big[k][1] for k in names}
    c_arr = c_idx.astype(jnp.int32).reshape(1)
    pc_arr = jnp.stack([p_idx, c_idx]).astype(jnp.int32)

    conv_shapes = [(KA_ROWS, cw_sh), (KS_ROWS, cw_sh), (KS_ROWS, fp)]
    conv_pack = _pack([_pad_to(conv_a_w[0], KA_ROWS, cw_sh), _pad_to(conv_b_w[0], KS_ROWS, cw_sh),
                       _pad_to(conv_f_w[0], KS_ROWS, fp)])
    conv_all = _allgather_small(conv_pack, name="allgather_conv", reduce=False)
    per_chip = [_unpack(conv_all[2 * q], conv_shapes) for q in range(N_CHIPS)]
    wa = jnp.concatenate([pc[0] for pc in per_chip], axis=1)
    wb = jnp.concatenate([pc[1] for pc in per_chip], axis=1)
    wf = jnp.concatenate([pc[2] for pc in per_chip], axis=1)

    tok = conv_all

    gather_groups = [["w_in"], ["w_out"], ["w_q", "w_k", "w_v"], ["w_o"], ["w_gate"], ["w_up"], ["w_down"]]
    gathers = []
    for gi, grp in enumerate(gather_groups):
        lays = [lay[k] for k in grp]
        plan = _gather_plan(lays)
        ssem, rsem, srcs, lands, tok = _exchange_start(
            [big[k][0]() for k in grp], [(l.shape, BF16) for l in lays], plan, tok, name=f"gather_start_{gi}")
        gathers.append((ssem, rsem, srcs, lands, plan, lays, grp))
    W = {}
    relays = {}

    def relay(gi, after):
        ssem, rsem, srcs, lands, plan, lays, grp = gathers[gi]
        _, lands = _exchange_wait(ssem, rsem, srcs, lands, plan, after, name=f"gather_wait_{gi}")
        plan = _forward_plan(lays)
        ssem, rsem, lands, _, token = _exchange_start(lands, [], plan, after, name=f"gather_forward_start_{gi}")
        relays[gi] = (ssem, rsem, lands, plan, grp)
        return token

    def gathered(gi, after):
        if gi not in relays:
            relay(gi, after)
        ssem, rsem, lands, plan, grp = relays[gi]
        lands, _ = _exchange_wait(ssem, rsem, lands, [], plan, after, name=f"gather_forward_wait_{gi}")
        W.update(zip(grp, lands))

    relay(0, tok)

    xs, tgt, mems = x[0], loss_target[0], mem[0]
    g_mem2, g_final2 = g_mem[None, :], g_final[None, :]

    memn, rm = _norm_fwd(mems, g_mem2, name="norm_mem")
    xn1, r1 = _norm_fwd(xs, g_mix, name="norm_mix")
    gathered(0, xn1)
    proj = _mm(xn1, W["w_in"], name="mm_in")
    ftok = relay(1, proj)
    u3, u1 = _mix_a_fwd(proj, wa, conv_a_b, ln_a_g, ln_a_b, cw=cw, deps=[ftok], name="mix_a_fwd")
    vb = _mix_b_fwd(proj, wb, cw=cw, sw=cw, name="mix_b_fwd")
    mix = jnp.concatenate([u3, vb], axis=1)
    gathered(1, mix)
    ftok = relay(2, mix)
    h1 = _mm(mix, W["w_out"], add=xs, deps=[ftok], name="mm_out")
    xn2, r2 = _norm_fwd(h1, g_xattn, name="norm_xattn")
    gathered(2, xn2)
    ftok = relay(3, xn2)
    q = _mm(xn2, W["w_q"], out_dtype=BF16, deps=[ftok], name="mm_q")
    k = _mm(memn, W["w_k"], out_dtype=BF16, name="mm_k")
    vm = _mm(memn, W["w_v"], out_dtype=BF16, name="mm_v")
    o = _attn_fwd(q, k, vm, name="attn_fwd")
    gathered(3, o)
    ftok = relay(4, o)
    h2 = _mm(o, W["w_o"], add=h1, deps=[ftok], name="mm_o")
    xn3, r3 = _norm_fwd(h2, g_ffn, name="norm_ffn")
    gathered(4, xn3)
    gp = _mm(xn3, W["w_gate"], tb=True, name="mm_gate")
    gathered(5, gp)
    up = _mm(xn3, W["w_up"], tb=True, name="mm_up")
    f = _ffn_act_fwd(gp, up, wf, name="ffn_act_fwd")
    gathered(6, f)
    h3 = _mm(f, W["w_down"], add=h2, name="mm_down")

    G = {}

    def siblings_start(tag, grp, after):
        lays = [lay[k] for k in grp]
        plan = _sibling_halves_plan(lays)
        ssem, rsem, srcs, lands, token = _exchange_start(
            [G[k] for k in grp], [(l.half_shape(), F32) for l in lays], plan, after, name=f"rs_siblings_start_{tag}")
        return (tag, grp, lays, plan, ssem, rsem, srcs, lands), token

    def chips_start(state, after):
        tag, grp, lays, plan, ssem, rsem, srcs, lands = state
        srcs, lands = _exchange_wait(ssem, rsem, srcs, lands, plan, after, name=f"rs_siblings_wait_{tag}")
        sums = [_half_sum(g, ra, c_arr, rows_split=(l.axis == 1), name=f"half_sum_{k}")
                for k, g, ra, l in zip(grp, srcs, lands, lays)]
        plan = _chips_plan(lays)
        ssem, rsem, sums, lands, token = _exchange_start(
            sums, [((N_CHIPS - 1,) + l.block_half_shape(), BF16) for l in lays], plan, after, name=f"rs_chips_start_{tag}")
        return (tag, grp, lays, plan, ssem, rsem, sums, lands), token

    def chips_finish(state, after):
        tag, grp, lays, plan, ssem, rsem, sums, lands = state
        sums, lands = _exchange_wait(ssem, rsem, sums, lands, plan, after, name=f"rs_chips_wait_{tag}")
        halves = [_sum_chips(cs, rb, pc_arr, l, name=f"sum_chips_{k}") for k, cs, rb, l in zip(grp, sums, lands, lays)]
        return dict(zip(grp, _rs_join_halves(halves, lays, name=f"rs_join_{tag}")))

    loss_rows, dh3, dh3b, dg_final = _loss_head(h3, g_final2, tgt, name="loss_head")
    df = _mm(dh3b, W["w_down"], tb=True, out_dtype=BF16, name="mm_d_f")
    G["w_down"] = _mm(f, dh3b, ta=True, name="mm_dw_down")
    rs_down, tok = siblings_start("down", ["w_down"], tok)
    dgp, dup, dwf = _ffn_act_bwd(gp, up, df, wf, name="ffn_act_bwd")
    G["w_gate"] = _mm(dgp, xn3, ta=True, deps=[tok], name="mm_dw_gate")
    G["w_up"] = _mm(dup, xn3, ta=True, name="mm_dw_up")
    rs_ffn, tok = siblings_start("ffn", ["w_gate", "w_up"], tok)
    rs_down, tok = chips_start(rs_down, tok)
    dxn3 = _mm(dgp, W["w_gate"], deps=[tok], name="mm_dxn3_gate")
    dxn3 = _mm(dup, W["w_up"], add=dxn3, name="mm_dxn3_up")
    dh2, dh2b, dg_ffn = _norm_bwd(h2, r3, g_ffn, dxn3, dh3, name="norm_ffn_bwd")
    rs_ffn, tok = chips_start(rs_ffn, dh2b)
    do = _mm(dh2b, W["w_o"], tb=True, out_dtype=BF16, deps=[tok], name="mm_d_o")
    G["w_o"] = _mm(o, dh2b, ta=True, name="mm_dw_o")
    dq, dk, dvm = _attn_bwd(q, k, vm, do, name="attn_bwd")
    dkb, dvb = dk.astype(BF16), dvm.astype(BF16)
    G["w_q"] = _mm(xn2, dq, ta=True, name="mm_dw_q")
    G["w_k"] = _mm(memn, dkb, ta=True, name="mm_dw_k")
    G["w_v"] = _mm(memn, dvb, ta=True, name="mm_dw_v")
    rs_att, tok = siblings_start("att", ["w_o", "w_q", "w_k", "w_v"], tok)
    dxn2 = _mm(dq, W["w_q"], tb=True, deps=[tok], name="mm_dxn2")
    dmemn = _mm(dkb, W["w_k"], tb=True, name="mm_dmem_k")
    dmemn = _mm(dvb, W["w_v"], tb=True, add=dmemn, name="mm_dmem_v")
    dg_mem = _norm_bwd(mems, rm, g_mem2, dmemn, None, name="norm_mem_bwd", want_dh=False)
    dh1, dh1b, dg_xattn = _norm_bwd(h1, r2, g_xattn, dxn2, dh2, name="norm_xattn_bwd")
    rs_att, tok = chips_start(rs_att, dh1b)
    dmix = _mm(dh1b, W["w_out"], tb=True, deps=[tok], name="mm_d_mix")
    G["w_out"] = _mm(mix, dh1b, ta=True, name="mm_dw_out")
    rs_out, tok = siblings_start("out", ["w_out"], tok)
    dav, dag, dwa, dba, dlg, dlb = _mix_a_bwd(proj, u1, dmix, wa, ln_a_g, ln_a_b, cw=cw, name="mix_a_bwd")
    dbg, dcg, dbh, dwb = _mix_b_bwd(proj, dmix, wb, cw=cw, sw=cw, name="mix_b_bwd")
    dproj = jnp.concatenate([dav, dag, dbg, dcg, dbh], axis=1)
    rs_out, tok = chips_start(rs_out, dproj)
    G["w_in"] = _mm(xn1, dproj, ta=True, deps=[tok], name="mm_dw_in")
    rs_in, tok = siblings_start("in", ["w_in"], tok)
    dxn1 = _mm(dproj, W["w_in"], tb=True, deps=[tok], name="mm_dxn1")
    dx, _, dg_mix = _norm_bwd(xs, r1, g_mix, dxn1, dh1, name="norm_mix_bwd")
    rs_in, tok = chips_start(rs_in, dx)

    loss_part = jnp.sum(loss_rows).reshape(1, 1)
    small_parts = [dg_mix, dba, dlg, dlb, dg_xattn, dg_mem, dg_ffn, dg_final, dwa, dwb, dwf, loss_part]
    small_shapes = [a.shape for a in small_parts]
    reduced = _allgather_small(_pack(small_parts), name="allreduce_small", reduce=True)
    (sg_mix, sba, slg, slb, sg_xattn, sg_mem, sg_ffn, sg_final, swa, swb, swf, loss_sum) = _unpack(reduced, small_shapes)
    loss = loss_sum.reshape(())
    ga_w = lax.dynamic_slice(swa, (0, p_idx * cw_sh), (KA, cw_sh))
    gb_w = lax.dynamic_slice(swb, (0, p_idx * cw_sh), (KS, cw_sh))
    gf_w = lax.dynamic_slice(swf, (0, p_idx * fp), (KS, f_sh))

    weights = dict(g_mix=g_mix, w_in=w_in, conv_a_w=conv_a_w, conv_a_b=conv_a_b, ln_a_g=ln_a_g, ln_a_b=ln_a_b,
                   conv_b_w=conv_b_w, w_out=w_out, g_xattn=g_xattn, g_mem=g_mem, w_q=w_q, w_k=w_k, w_v=w_v, w_o=w_o,
                   g_ffn=g_ffn, w_gate=w_gate, w_up=w_up, conv_f_w=conv_f_w, w_down=w_down, g_final=g_final)
    m_in = dict(g_mix=m_g_mix, w_in=m_w_in, conv_a_w=m_conv_a_w, conv_a_b=m_conv_a_b, ln_a_g=m_ln_a_g, ln_a_b=m_ln_a_b,
                conv_b_w=m_conv_b_w, w_out=m_w_out, g_xattn=m_g_xattn, g_mem=m_g_mem, w_q=m_w_q, w_k=m_w_k, w_v=m_w_v,
                w_o=m_w_o, g_ffn=m_g_ffn, w_gate=m_w_gate, w_up=m_w_up, conv_f_w=m_conv_f_w, w_down=m_w_down,
                g_final=m_g_final)
    v_in = dict(g_mix=v_g_mix, w_in=v_w_in, conv_a_w=v_conv_a_w, conv_a_b=v_conv_a_b, ln_a_g=v_ln_a_g, ln_a_b=v_ln_a_b,
                conv_b_w=v_conv_b_w, w_out=v_w_out, g_xattn=v_g_xattn, g_mem=v_g_mem, w_q=v_w_q, w_k=v_w_k, w_v=v_w_v,
                w_o=v_w_o, g_ffn=v_g_ffn, w_gate=v_w_gate, w_up=v_w_up, conv_f_w=v_conv_f_w, w_down=v_w_down,
                g_final=v_g_final)
    order = list(weights)
    grads = dict(g_mix=sg_mix, conv_a_w=ga_w, conv_a_b=sba, ln_a_g=slg, ln_a_b=slb, conv_b_w=gb_w, g_xattn=sg_xattn,
                 g_mem=sg_mem, g_ffn=sg_ffn, conv_f_w=gf_w, g_final=sg_final)
    grads = {k: g.reshape(weights[k].shape) for k, g in grads.items()}

    delta, new_m, new_v = {}, {}, {}
    small = [k for k in order if k not in big]
    small_shapes = [weights[k].shape for k in small]
    packed = [_pack([src[k] for k in small]) for src in (weights, grads, m_in, v_in)]
    d_, m_, v_ = _adamw(*packed, name="adamw_small")
    for k, dd, mm_, vv in zip(small, _unpack(d_, small_shapes), _unpack(m_, small_shapes), _unpack(v_, small_shapes)):
        delta[k], new_m[k], new_v[k] = dd, mm_, vv

    transposed = ("w_gate", "w_up")
    after = (d_, tok)
    for state in (rs_down, rs_ffn, rs_att, rs_out, rs_in):
        for k, g in chips_finish(state, after).items():
            view = t_ if k in transposed else (lambda a: a[0])
            back = (lambda a: jnp.swapaxes(a, 0, 1)[None]) if k in transposed else (lambda a: a[None])
            padded = g.shape != view(weights[k]).shape
            outs = _adamw(view(weights[k]), g, view(m_in[k]), view(v_in[k]), emit_grad=padded, name=f"adamw_{k}")
            delta[k], new_m[k], new_v[k] = back(outs[0]), back(outs[1]), back(outs[2])
            grads[k] = back(outs[3] if padded else g)
            after = outs[0]

    return (loss, dx[None], *[grads[k] for k in order], *[delta[k] for k in order],
            *[new_m[k] for k in order], *[new_v[k] for k in order])
```

```python
import functools

import jax
import jax.numpy as jnp
from jax import lax
from jax.experimental import pallas as pl
from jax.experimental.pallas import tpu as pltpu

F32 = jnp.float32
BF16 = jnp.bfloat16
EPS = 1e-6
N_HEADS = 4
GROUP = 128
KA = 31
KS = 3
KA_ROWS = 32
KS_ROWS = 8
HALO_A = 32
HALO_S = 8
CHUNK_ROWS = 16
CHUNK_COLS = 256
N_CHIPS = 4
N_DEV = 8
LANES = 128
VMEM_LIMIT = 56 * 1024 * 1024
ADAMW_BLOCK_BYTES = 1 << 20
MESH = pl.DeviceIdType.MESH
ANY = pl.BlockSpec(memory_space=pl.ANY)

ADAM_LR = 0.001
ADAM_B1 = 0.9
ADAM_B2 = 0.999
ADAM_EPS = 1e-08
ADAM_WD = 0.01
ADAM_STEP = 10


def _pick(dim, pref, mult=LANES):
    if dim <= pref:
        return dim
    t = (pref // mult) * mult
    while t >= mult:
        if dim % t == 0:
            return t
        t -= mult
    return dim


def _round_up(n, m):
    return ((n + m - 1) // m) * m


def _params(*sem):
    return pltpu.CompilerParams(dimension_semantics=sem, vmem_limit_bytes=VMEM_LIMIT)


def _sigmoid(x):
    return jax.nn.sigmoid(x)


def _mm(a, b, *, name, ta=False, tb=False, out_dtype=F32, add=None, deps=(), tm=1024, tn=512, tk=5632):
    if ta:
        K, M = a.shape
    else:
        M, K = a.shape
    if tb:
        N, K2 = b.shape
    else:
        K2, N = b.shape
    assert K == K2, (a.shape, b.shape)
    tm, tn, tk = _pick(M, tm), _pick(N, tn), _pick(K, tk)
    nk = K // tk
    a_spec = pl.BlockSpec((tk, tm), lambda i, j, k: (k, i)) if ta else pl.BlockSpec((tm, tk), lambda i, j, k: (i, k))
    b_spec = pl.BlockSpec((tn, tk), lambda i, j, k: (j, k)) if tb else pl.BlockSpec((tk, tn), lambda i, j, k: (k, j))
    o_spec = pl.BlockSpec((tm, tn), lambda i, j, k: (i, j))
    dims = (((0,) if ta else (1,), (1,) if tb else (0,)), ((), ()))
    has_add = add is not None

    def body(*refs):
        a_ref, b_ref = refs[:2]
        add_ref = refs[2] if has_add else None
        o_ref = refs[(3 if has_add else 2) + len(deps)]
        acc_ref = refs[-1] if nk > 1 else None
        k = pl.program_id(2)
        part = lax.dot_general(a_ref[...], b_ref[...], dims, preferred_element_type=F32)

        def finish(r):
            if add_ref is not None:
                r = add_ref[...] + r
            o_ref[...] = r.astype(out_dtype)

        if nk == 1:
            finish(part)
        else:
            @pl.when(k == 0)
            def _():
                acc_ref[...] = part

            @pl.when(jnp.logical_and(k > 0, k < nk - 1))
            def _():
                acc_ref[...] += part

            @pl.when(k == nk - 1)
            def _():
                finish(acc_ref[...] + part)

    in_specs = [a_spec, b_spec] + ([o_spec] if has_add else []) + [ANY] * len(deps)
    args = (a, b) + ((add,) if has_add else ()) + tuple(deps)
    return pl.pallas_call(
        body,
        name=name,
        out_shape=jax.ShapeDtypeStruct((M, N), out_dtype),
        grid=(M // tm, N // tn, nk),
        in_specs=in_specs,
        out_specs=o_spec,
        scratch_shapes=[pltpu.VMEM((tm, tn), F32)] if nk > 1 else [],
        compiler_params=_params("parallel", "parallel", "arbitrary"),
    )(*args)


def _norm_fwd(h, g, *, name):
    T, D = h.shape
    R = _pick(T, 256, 8)

    def body(h_ref, g_ref, xn_ref, r_ref):
        x = h_ref[...]
        r = lax.rsqrt(jnp.mean(x * x, axis=-1, keepdims=True) + EPS)
        xn_ref[...] = ((x * r) * g_ref[...]).astype(BF16)
        r_ref[...] = r

    return pl.pallas_call(
        body,
        name=name,
        out_shape=(jax.ShapeDtypeStruct((T, D), BF16), jax.ShapeDtypeStruct((T, 1), F32)),
        grid=(T // R,),
        in_specs=[pl.BlockSpec((R, D), lambda i: (i, 0)), pl.BlockSpec((1, D), lambda i: (0, 0))],
        out_specs=(pl.BlockSpec((R, D), lambda i: (i, 0)), pl.BlockSpec((R, 1), lambda i: (i, 0))),
        compiler_params=_params("parallel"),
    )(h, g)


def _norm_bwd(h, r, g, dxn, dres, *, name, want_dh=True):
    T, D = h.shape
    R = _pick(T, 128, 8)
    has_res = dres is not None

    def body(*refs):
        h_ref, r_ref, g_ref, dxn_ref = refs[:4]
        pos = 4
        dres_ref = None
        if has_res:
            dres_ref = refs[pos]
            pos += 1
        if want_dh:
            dh_ref, dhb_ref, dg_ref = refs[pos:pos + 3]
        else:
            dg_ref = refs[pos]
        i = pl.program_id(0)
        rr = r_ref[...]
        hn = h_ref[...] * rr
        d = dxn_ref[...].astype(F32)
        gd = d * g_ref[...]
        part = jnp.sum(d * hn, axis=0, keepdims=True)

        @pl.when(i == 0)
        def _():
            dg_ref[...] = part

        @pl.when(i > 0)
        def _():
            dg_ref[...] += part

        if want_dh:
            dh = rr * (gd - hn * jnp.mean(gd * hn, axis=-1, keepdims=True))
            if dres_ref is not None:
                dh = dres_ref[...] + dh
            dh_ref[...] = dh
            dhb_ref[...] = dh.astype(BF16)

    row = pl.BlockSpec((R, D), lambda i: (i, 0))
    vec = pl.BlockSpec((1, D), lambda i: (0, 0))
    in_specs = [row, pl.BlockSpec((R, 1), lambda i: (i, 0)), vec, row] + ([row] if has_res else [])
    args = (h, r, g, dxn) + ((dres,) if has_res else ())
    if want_dh:
        out_shape = (jax.ShapeDtypeStruct((T, D), F32), jax.ShapeDtypeStruct((T, D), BF16), jax.ShapeDtypeStruct((1, D), F32))
        out_specs = (row, row, vec)
    else:
        out_shape = jax.ShapeDtypeStruct((1, D), F32)
        out_specs = vec
    return pl.pallas_call(
        body, name=name, out_shape=out_shape, grid=(T // R,), in_specs=in_specs, out_specs=out_specs,
        compiler_params=_params("arbitrary"),
    )(*args)


def _loss_head(h, g, tgt, *, name):
    T, D = h.shape
    R = _pick(T, 128, 8)

    def body(h_ref, g_ref, t_ref, loss_ref, dh_ref, dhb_ref, dg_ref):
        i = pl.program_id(0)
        x = h_ref[...]
        gg = g_ref[...]
        r = lax.rsqrt(jnp.mean(x * x, axis=-1, keepdims=True) + EPS)
        hn = x * r
        e = hn * gg - t_ref[...]
        loss_ref[...] = 0.5 * jnp.mean(e * e, axis=-1, keepdims=True)
        dy = e * (1.0 / D)
        gd = dy * gg
        dh = r * (gd - hn * jnp.mean(gd * hn, axis=-1, keepdims=True))
        dh_ref[...] = dh
        dhb_ref[...] = dh.astype(BF16)
        part = jnp.sum(dy * hn, axis=0, keepdims=True)

        @pl.when(i == 0)
        def _():
            dg_ref[...] = part

        @pl.when(i > 0)
        def _():
            dg_ref[...] += part

    row = pl.BlockSpec((R, D), lambda i: (i, 0))
    vec = pl.BlockSpec((1, D), lambda i: (0, 0))
    return pl.pallas_call(
        body,
        name=name,
        out_shape=(jax.ShapeDtypeStruct((T, 1), F32), jax.ShapeDtypeStruct((T, D), F32),
                   jax.ShapeDtypeStruct((T, D), BF16), jax.ShapeDtypeStruct((1, D), F32)),
        grid=(T // R,),
        in_specs=[row, vec, row],
        out_specs=(pl.BlockSpec((R, 1), lambda i: (i, 0)), row, row, vec),
        compiler_params=_params("arbitrary"),
    )(h, g, tgt)


_NT = (((1,), (1,)), ((), ()))
_TN = (((0,), (0,)), ((), ()))
_NN = (((1,), (0,)), ((), ()))


def _softmax_rows(s):
    m = jnp.max(s, axis=-1, keepdims=True)
    e = jnp.exp(s - m)
    return e / jnp.sum(e, axis=-1, keepdims=True)


def _attn_fwd(q, k, v, *, name):
    T, D = q.shape
    ML = k.shape[0]
    dh = D // N_HEADS
    scale = dh ** -0.5
    R = _pick(T, 512, 16)

    def body(q_ref, k_ref, v_ref, o_ref):
        s = lax.dot_general(q_ref[...], k_ref[...], _NT, preferred_element_type=F32) * scale
        p = _softmax_rows(s)
        o_ref[...] = lax.dot_general(p.astype(BF16), v_ref[...], _NN, preferred_element_type=F32).astype(BF16)

    qs = pl.BlockSpec((R, dh), lambda i, h: (i, h))
    ks = pl.BlockSpec((ML, dh), lambda i, h: (0, h))
    return pl.pallas_call(
        body, name=name, out_shape=jax.ShapeDtypeStruct((T, D), BF16), grid=(T // R, N_HEADS),
        in_specs=[qs, ks, ks], out_specs=qs, compiler_params=_params("parallel", "parallel"),
    )(q, k, v)


def _attn_bwd(q, k, v, do, *, name):
    T, D = q.shape
    ML = k.shape[0]
    dh = D // N_HEADS
    scale = dh ** -0.5
    R = _pick(T, 512, 16)

    def body(q_ref, k_ref, v_ref, do_ref, dq_ref, dk_ref, dv_ref):
        i = pl.program_id(1)
        qq, kk, vv, dd = q_ref[...], k_ref[...], v_ref[...], do_ref[...]
        s = lax.dot_general(qq, kk, _NT, preferred_element_type=F32) * scale
        p = _softmax_rows(s)
        dp = lax.dot_general(dd, vv, _NT, preferred_element_type=F32)
        dv_part = lax.dot_general(p.astype(BF16), dd, _TN, preferred_element_type=F32)
        ds = (p * (dp - jnp.sum(p * dp, axis=-1, keepdims=True)) * scale).astype(BF16)
        dq_ref[...] = lax.dot_general(ds, kk, _NN, preferred_element_type=F32).astype(BF16)
        dk_part = lax.dot_general(ds, qq, _TN, preferred_element_type=F32)

        @pl.when(i == 0)
        def _():
            dk_ref[...] = dk_part
            dv_ref[...] = dv_part

        @pl.when(i > 0)
        def _():
            dk_ref[...] += dk_part
            dv_ref[...] += dv_part

    qs = pl.BlockSpec((R, dh), lambda h, i: (i, h))
    ks = pl.BlockSpec((ML, dh), lambda h, i: (0, h))
    return pl.pallas_call(
        body,
        name=name,
        out_shape=(jax.ShapeDtypeStruct((T, D), BF16), jax.ShapeDtypeStruct((ML, D), F32), jax.ShapeDtypeStruct((ML, D), F32)),
        grid=(N_HEADS, T // R),
        in_specs=[qs, ks, ks, qs],
        out_specs=(qs, ks, ks),
        compiler_params=_params("parallel", "arbitrary"),
    )(q, k, v, do)


def _prev_halo(R, halo, CT, col):
    per = R // halo
    return lambda i, j: (jnp.maximum(i * per - 1, 0), col(j))


def _next_halo(R, halo, T, col):
    per = R // halo
    last = T // halo - 1
    return lambda i, j: (jnp.minimum((i + 1) * per, last), col(j))


def _swap(f):
    return lambda j, i: f(i, j)


def _ffn_act_fwd(gp, up, wf, *, name):
    T, F = gp.shape
    R = _pick(T, 256, CHUNK_ROWS)
    CT = _pick(F, 1024, CHUNK_COLS)
    H = HALO_S
    RC, CC = CHUNK_ROWS, min(CHUNK_COLS, CT)

    def body(g_ref, gh_ref, u_ref, w_ref, f_ref, ext):
        i = pl.program_id(0)
        ext[pl.ds(0, H), :] = jnp.where(i > 0, gh_ref[...], 0.0)
        ext[pl.ds(H, R), :] = g_ref[...]
        for c0 in range(0, CT, CC):
            cols = pl.ds(c0, CC)
            w0, w1, w2 = [jnp.broadcast_to(w_ref[pl.ds(k, 1), cols], (RC, CC)) for k in range(KS)]
            for r0 in range(0, R, RC):
                g = ext[pl.ds(r0 + H - 2, RC), cols] * w0
                g = g + ext[pl.ds(r0 + H - 1, RC), cols] * w1
                g = g + ext[pl.ds(r0 + H, RC), cols] * w2
                f_ref[pl.ds(r0, RC), cols] = (g * _sigmoid(g) * u_ref[pl.ds(r0, RC), cols]).astype(BF16)

    main = pl.BlockSpec((R, CT), lambda i, j: (i, j))
    return pl.pallas_call(
        body,
        name=name,
        out_shape=jax.ShapeDtypeStruct((T, F), BF16),
        grid=(T // R, F // CT),
        in_specs=[main, pl.BlockSpec((H, CT), _prev_halo(R, H, CT, lambda j: j)), main,
                  pl.BlockSpec((KS_ROWS, CT), lambda i, j: (0, j))],
        out_specs=main,
        scratch_shapes=[pltpu.VMEM((R + H, CT), F32)],
        compiler_params=_params("parallel", "parallel"),
    )(gp, gp, up, wf)


def _ffn_act_bwd(gp, up, df, wf, *, name):
    T, F = gp.shape
    R = _pick(T, 256, CHUNK_ROWS)
    CT = _pick(F, 512, CHUNK_COLS)
    H = HALO_S
    HB = 16
    n_t = T // R
    RC, CC = CHUNK_ROWS, min(CHUNK_COLS, CT)

    def body(g_ref, gp_ref, gn_ref, u_ref, un_ref, d_ref, dn_ref, w_ref, dg_out, du_out, dw_ref, ext, dge):
        i = pl.program_id(1)
        last = i == n_t - 1
        ext[pl.ds(0, H), :] = jnp.where(i > 0, gp_ref[...], 0.0)
        ext[pl.ds(H, R), :] = g_ref[...]
        ext[pl.ds(H + R, H), :] = gn_ref[...]

        def dact(g, u, d):
            sg = _sigmoid(g)
            return d * u * (sg * (1.0 + g * (1.0 - sg))), d * (g * sg)

        dw_rows = [[] for _ in range(KS)]
        for c0 in range(0, CT, CC):
            cols = pl.ds(c0, CC)
            w = [jnp.broadcast_to(w_ref[pl.ds(k, 1), cols], (RC, CC)) for k in range(KS)]
            acc = [jnp.zeros((RC, CC), F32) for _ in range(KS)]
            for r0 in range(0, R, RC):
                rows = pl.ds(r0, RC)
                taps = [ext[pl.ds(r0 + H - 2 + k, RC), cols] for k in range(KS)]
                g = taps[0] * w[0] + taps[1] * w[1] + taps[2] * w[2]
                dg, du = dact(g, u_ref[rows, cols], d_ref[rows, cols].astype(F32))
                du_out[rows, cols] = du.astype(BF16)
                dge[rows, cols] = dg
                acc = [a + dg * t for a, t in zip(acc, taps)]
            taps = [ext[pl.ds(R + H - 2 + k, H), cols] for k in range(KS)]
            g = taps[0] * w[0][:H] + taps[1] * w[1][:H] + taps[2] * w[2][:H]
            d_next = jnp.where(last, 0.0, dn_ref[pl.ds(0, H), cols].astype(F32))
            dg_next, _ = dact(g, un_ref[:, cols], d_next)
            dge[pl.ds(R, H), cols] = jnp.where(last, 0.0, dg_next)
            for r0 in range(0, R, RC):
                dgp = (dge[pl.ds(r0, RC), cols] * w[2] + dge[pl.ds(r0 + 1, RC), cols] * w[1]
                       + dge[pl.ds(r0 + 2, RC), cols] * w[0])
                dg_out[pl.ds(r0, RC), cols] = dgp.astype(BF16)
            for k in range(KS):
                dw_rows[k].append(jnp.sum(acc[k], axis=0, keepdims=True))
        rows = [jnp.concatenate(r, axis=1) if len(r) > 1 else r[0] for r in dw_rows]
        rows.append(jnp.zeros((KS_ROWS - KS, CT), F32))
        _acc_rows_block(dw_ref, i, rows)

    col = lambda j: j
    main = pl.BlockSpec((R, CT), lambda j, i: (i, j))
    prev8 = pl.BlockSpec((H, CT), _swap(_prev_halo(R, H, CT, col)))
    next8 = pl.BlockSpec((H, CT), _swap(_next_halo(R, H, T, col)))
    next16 = pl.BlockSpec((HB, CT), _swap(_next_halo(R, HB, T, col)))
    wspec = pl.BlockSpec((KS_ROWS, CT), lambda j, i: (0, j))
    return pl.pallas_call(
        body,
        name=name,
        out_shape=(jax.ShapeDtypeStruct((T, F), BF16), jax.ShapeDtypeStruct((T, F), BF16), jax.ShapeDtypeStruct((KS_ROWS, F), F32)),
        grid=(F // CT, n_t),
        in_specs=[main, prev8, next8, main, next8, main, next16, wspec],
        out_specs=(main, main, wspec),
        scratch_shapes=[pltpu.VMEM((R + 2 * H, CT), F32), pltpu.VMEM((R + H, CT), F32)],
        compiler_params=_params("parallel", "arbitrary"),
    )(gp, gp, gp, up, up, df, df, wf)


def _acc_rows_block(ref, i, rows):
    *singles, pad = rows

    @pl.when(i == 0)
    def _():
        for k, row in enumerate(singles):
            ref[pl.ds(k, 1), :] = row
        ref[pl.ds(len(singles), pad.shape[0]), :] = pad

    @pl.when(i > 0)
    def _():
        for k, row in enumerate(singles):
            ref[pl.ds(k, 1), :] += row


def _group_stats(x):
    mu = jnp.mean(x, axis=-1, keepdims=True)
    xc = x - mu
    var = jnp.mean(xc * xc, axis=-1, keepdims=True)
    return xc, lax.rsqrt(var + EPS)


def _shifted_rows(rolled, x, cols):
    n = x.shape[0]
    for b in range(1, 8):
        rolled[b - 1, :, cols] = pltpu.roll(x, n - b, axis=0)


def _window(src, rolled, off, rows, cols):
    b = off % 8
    if b == 0:
        return src[pl.ds(off, rows), cols]
    return rolled[b - 1, pl.ds(off - b, rows), cols]


def _mix_a_fwd(proj, wa, ba, lg, lb, *, cw, name, deps=()):
    T = proj.shape[0]
    R = _pick(T, 128, HALO_A)
    CT = _pick(cw, 256)
    nc = cw // CT
    H = HALO_A

    def body(av_ref, ag_ref, avh_ref, agh_ref, w_ref, b_ref, lg_ref, lb_ref, *rest):
        u3_ref, u1_ref, ext, rolled = rest[len(deps):]
        i = pl.program_id(0)
        for gi in range(CT // GROUP):
            cols = pl.ds(gi * GROUP, GROUP)
            ext[pl.ds(0, H), cols] = jnp.where(i > 0, avh_ref[:, cols] * _sigmoid(agh_ref[:, cols]), 0.0)
            ext[pl.ds(H, R), cols] = av_ref[:, cols] * _sigmoid(ag_ref[:, cols])
            _shifted_rows(rolled, ext[:, cols], cols)
            acc = _window(ext, rolled, H - (KA - 1), R, cols) * w_ref[pl.ds(0, 1), cols]
            for k in range(1, KA):
                acc = acc + _window(ext, rolled, H - (KA - 1) + k, R, cols) * w_ref[pl.ds(k, 1), cols]
            u1 = acc + b_ref[:, cols]
            u1_ref[:, cols] = u1
            xc, rstd = _group_stats(u1)
            u2 = (xc * rstd) * lg_ref[:, cols] + lb_ref[:, cols]
            u3_ref[:, cols] = (u2 * _sigmoid(u2)).astype(BF16)

    main_v = pl.BlockSpec((R, CT), lambda i, j: (i, j))
    main_g = pl.BlockSpec((R, CT), lambda i, j: (i, j + nc))
    halo_v = pl.BlockSpec((H, CT), _prev_halo(R, H, CT, lambda j: j))
    halo_g = pl.BlockSpec((H, CT), _prev_halo(R, H, CT, lambda j: j + nc))
    vec = pl.BlockSpec((1, CT), lambda i, j: (0, j))
    out = pl.BlockSpec((R, CT), lambda i, j: (i, j))
    return pl.pallas_call(
        body,
        name=name,
        out_shape=(jax.ShapeDtypeStruct((T, cw), BF16), jax.ShapeDtypeStruct((T, cw), F32)),
        grid=(T // R, nc),
        in_specs=[main_v, main_g, halo_v, halo_g, pl.BlockSpec((KA_ROWS, CT), lambda i, j: (0, j)), vec, vec, vec]
        + [ANY] * len(deps),
        out_specs=(out, out),
        scratch_shapes=[pltpu.VMEM((R + H, CT), F32), pltpu.VMEM((7, R + H, CT), F32)],
        compiler_params=_params("parallel", "parallel"),
    )(proj, proj, proj, proj, wa, ba, lg, lb, *deps)


def _mix_a_bwd(proj, u1, dmix, wa, lg, lb, *, cw, name):
    T = proj.shape[0]
    R = _pick(T, 128, HALO_A)
    CT = _pick(cw, 256)
    nc = cw // CT
    H = HALO_A
    n_t = T // R
    NG = CT // GROUP

    def body(av_ref, ag_ref, avh_ref, agh_ref, u1_ref, u1n_ref, d3_ref, d3n_ref, w_ref, lg_ref, lb_ref,
             dav_ref, dag_ref, dw_ref, db_ref, dlg_ref, dlb_ref, ext, d1, ext_rolled, d1_rolled):
        i = pl.program_id(1)
        last = i == n_t - 1

        def ln_bwd(u1, d3, cols):
            xc, rstd = _group_stats(u1)
            xh = xc * rstd
            g = lg_ref[:, cols]
            u2 = xh * g + lb_ref[:, cols]
            sg = _sigmoid(u2)
            du2 = d3 * (sg * (1.0 + u2 * (1.0 - sg)))
            dxh = du2 * g
            du1 = rstd * (dxh - jnp.mean(dxh, axis=-1, keepdims=True) - xh * jnp.mean(dxh * xh, axis=-1, keepdims=True))
            return du1, du2 * xh, du2

        def lanes(parts):
            return jnp.concatenate(parts, axis=1) if len(parts) > 1 else parts[0]

        dlg_parts, dlb_parts, db_parts = [], [], []
        dw_parts = [[] for _ in range(KA)]
        for gi in range(NG):
            cols = pl.ds(gi * GROUP, GROUP)
            du1, dlg, dlb = ln_bwd(u1_ref[:, cols], d3_ref[:, cols], cols)
            d1[pl.ds(0, R), cols] = du1
            dlg_parts.append(jnp.sum(dlg, axis=0, keepdims=True))
            dlb_parts.append(jnp.sum(dlb, axis=0, keepdims=True))
            db_parts.append(jnp.sum(du1, axis=0, keepdims=True))
            du1n, _, _ = ln_bwd(u1n_ref[:, cols], jnp.where(last, 0.0, d3n_ref[:, cols]), cols)
            d1[pl.ds(R, H), cols] = jnp.where(last, 0.0, du1n)

            _shifted_rows(d1_rolled, d1[:, cols], cols)
            du0 = _window(d1, d1_rolled, KA - 1, R, cols) * w_ref[pl.ds(0, 1), cols]
            for k in range(1, KA):
                du0 = du0 + _window(d1, d1_rolled, KA - 1 - k, R, cols) * w_ref[pl.ds(k, 1), cols]
            av = av_ref[:, cols]
            sg = _sigmoid(ag_ref[:, cols])
            dav_ref[:, cols] = (du0 * sg).astype(BF16)
            dag_ref[:, cols] = (du0 * av * sg * (1.0 - sg)).astype(BF16)

            ext[pl.ds(0, H), cols] = jnp.where(i > 0, avh_ref[:, cols] * _sigmoid(agh_ref[:, cols]), 0.0)
            ext[pl.ds(H, R), cols] = av * sg
            _shifted_rows(ext_rolled, ext[:, cols], cols)
            d1_main = d1[pl.ds(0, R), cols]
            for k in range(KA):
                window = _window(ext, ext_rolled, H - (KA - 1) + k, R, cols)
                dw_parts[k].append(jnp.sum(d1_main * window, axis=0, keepdims=True))

        dlg_row, dlb_row, db_row = lanes(dlg_parts), lanes(dlb_parts), lanes(db_parts)

        @pl.when(i == 0)
        def _():
            dlg_ref[...] = dlg_row
            dlb_ref[...] = dlb_row
            db_ref[...] = db_row

        @pl.when(i > 0)
        def _():
            dlg_ref[...] += dlg_row
            dlb_ref[...] += dlb_row
            db_ref[...] += db_row

        rows = [lanes(p) for p in dw_parts]
        rows.append(jnp.zeros((KA_ROWS - KA, CT), F32))
        _acc_rows_block(dw_ref, i, rows)

    cv = lambda j: j
    cg = lambda j: j + nc
    main_v = pl.BlockSpec((R, CT), lambda j, i: (i, j))
    main_g = pl.BlockSpec((R, CT), lambda j, i: (i, j + nc))
    prev_v = pl.BlockSpec((H, CT), _swap(_prev_halo(R, H, CT, cv)))
    prev_g = pl.BlockSpec((H, CT), _swap(_prev_halo(R, H, CT, cg)))
    nxt = pl.BlockSpec((H, CT), _swap(_next_halo(R, H, T, cv)))
    wspec = pl.BlockSpec((KA_ROWS, CT), lambda j, i: (0, j))
    vec = pl.BlockSpec((1, CT), lambda j, i: (0, j))
    vshape = jax.ShapeDtypeStruct((1, cw), F32)
    return pl.pallas_call(
        body,
        name=name,
        out_shape=(jax.ShapeDtypeStruct((T, cw), BF16), jax.ShapeDtypeStruct((T, cw), BF16),
                   jax.ShapeDtypeStruct((KA_ROWS, cw), F32), vshape, vshape, vshape),
        grid=(nc, n_t),
        in_specs=[main_v, main_g, prev_v, prev_g, main_v, nxt, main_v, nxt, wspec, vec, vec],
        out_specs=(main_v, main_v, wspec, vec, vec, vec),
        scratch_shapes=[pltpu.VMEM((R + H, CT), F32), pltpu.VMEM((R + H, CT), F32),
                        pltpu.VMEM((7, R + H, CT), F32), pltpu.VMEM((7, R + H, CT), F32)],
        compiler_params=_params("parallel", "arbitrary"),
    )(proj, proj, proj, proj, u1, u1, dmix, dmix, wa, lg, lb)


def _mix_b_fwd(proj, wb, *, cw, sw, name):
    T = proj.shape[0]
    R = _pick(T, 256, 8)
    CT = _pick(sw, 512)
    nb, nc, nh = (2 * cw) // CT, (2 * cw + sw) // CT, (2 * cw + 2 * sw) // CT
    H = HALO_S

    def body(b_ref, c_ref, h_ref, ch_ref, hh_ref, w_ref, v_ref, ext):
        i = pl.program_id(0)
        ext[pl.ds(0, H), :] = jnp.where(i > 0, ch_ref[...] * hh_ref[...], 0.0)
        ext[pl.ds(H, R), :] = c_ref[...] * h_ref[...]
        w = w_ref[...]
        zc = ext[pl.ds(H - 2, R), :] * w[0:1, :]
        zc = zc + ext[pl.ds(H - 1, R), :] * w[1:2, :]
        zc = zc + ext[pl.ds(H, R), :] * w[2:3, :]
        v_ref[...] = (b_ref[...] * zc).astype(BF16)

    def main(off):
        return pl.BlockSpec((R, CT), lambda i, j: (i, j + off))

    def prev(off):
        return pl.BlockSpec((H, CT), _prev_halo(R, H, CT, lambda j: j + off))

    return pl.pallas_call(
        body,
        name=name,
        out_shape=jax.ShapeDtypeStruct((T, sw), BF16),
        grid=(T // R, sw // CT),
        in_specs=[main(nb), main(nc), main(nh), prev(nc), prev(nh), pl.BlockSpec((KS_ROWS, CT), lambda i, j: (0, j))],
        out_specs=pl.BlockSpec((R, CT), lambda i, j: (i, j)),
        scratch_shapes=[pltpu.VMEM((R + H, CT), F32)],
        compiler_params=_params("parallel", "parallel"),
    )(proj, proj, proj, proj, proj, wb)


def _mix_b_bwd(proj, dmix, wb, *, cw, sw, name):
    T = proj.shape[0]
    R = _pick(T, 256, 8)
    CT = _pick(sw, 512)
    nb, nc, nh = (2 * cw) // CT, (2 * cw + sw) // CT, (2 * cw + 2 * sw) // CT
    nd = cw // CT
    H = HALO_S
    n_t = T // R

    def body(b_ref, bn_ref, c_ref, cp_ref, cn_ref, h_ref, hp_ref, hn_ref, d_ref, dn_ref, w_ref,
             db_ref, dc_ref, dhh_ref, dw_ref, ext, dze):
        i = pl.program_id(1)
        last = i == n_t - 1
        cc, hh = c_ref[...], h_ref[...]
        ext[pl.ds(0, H), :] = jnp.where(i > 0, cp_ref[...] * hp_ref[...], 0.0)
        ext[pl.ds(H, R), :] = cc * hh
        ext[pl.ds(H + R, H), :] = cn_ref[...] * hn_ref[...]
        w = w_ref[...]
        w0, w1, w2 = w[0:1, :], w[1:2, :], w[2:3, :]

        def conv(start, n):
            z = ext[pl.ds(start + H - 2, n), :] * w0
            z = z + ext[pl.ds(start + H - 1, n), :] * w1
            return z + ext[pl.ds(start + H, n), :] * w2

        d_main = d_ref[...]
        db_ref[...] = (d_main * conv(0, R)).astype(BF16)
        dzc_main = d_main * b_ref[...]
        dze[pl.ds(0, R), :] = dzc_main
        dze[pl.ds(R, H), :] = jnp.where(last, 0.0, dn_ref[...] * bn_ref[...])
        dz = dze[pl.ds(0, R), :] * w2 + dze[pl.ds(1, R), :] * w1 + dze[pl.ds(2, R), :] * w0
        dc_ref[...] = (dz * hh).astype(BF16)
        dhh_ref[...] = (dz * cc).astype(BF16)
        rows = [jnp.sum(dzc_main * ext[pl.ds(H - 2 + k, R), :], axis=0, keepdims=True) for k in range(KS)]
        rows.append(jnp.zeros((KS_ROWS - KS, CT), F32))
        _acc_rows_block(dw_ref, i, rows)

    def main(off):
        return pl.BlockSpec((R, CT), lambda j, i: (i, j + off))

    def prev(off):
        return pl.BlockSpec((H, CT), _swap(_prev_halo(R, H, CT, lambda j: j + off)))

    def nxt(off):
        return pl.BlockSpec((H, CT), _swap(_next_halo(R, H, T, lambda j: j + off)))

    out = pl.BlockSpec((R, CT), lambda j, i: (i, j))
    wspec = pl.BlockSpec((KS_ROWS, CT), lambda j, i: (0, j))
    act = jax.ShapeDtypeStruct((T, sw), BF16)
    return pl.pallas_call(
        body,
        name=name,
        out_shape=(act, act, act, jax.ShapeDtypeStruct((KS_ROWS, sw), F32)),
        grid=(sw // CT, n_t),
        in_specs=[main(nb), nxt(nb), main(nc), prev(nc), nxt(nc), main(nh), prev(nh), nxt(nh), main(nd), nxt(nd), wspec],
        out_specs=(out, out, out, wspec),
        scratch_shapes=[pltpu.VMEM((R + 2 * H, CT), F32), pltpu.VMEM((R + H, CT), F32)],
        compiler_params=_params("parallel", "arbitrary"),
    )(proj, proj, proj, proj, proj, proj, proj, proj, dmix, dmix, wb)


def _adamw(w, g, m, v, *, name, emit_grad=False):
    Rr, Cc = w.shape
    R = _pick(Rr, max(8, ADAMW_BLOCK_BYTES // (4 * Cc)), 8)

    def body(w_ref, g_ref, m_ref, v_ref, d_ref, mo_ref, vo_ref, *g_out):
        g = g_ref[...]
        m2 = ADAM_B1 * m_ref[...] + (1.0 - ADAM_B1) * g
        v2 = ADAM_B2 * v_ref[...] + (1.0 - ADAM_B2) * (g * g)
        m_hat = m2 / (1.0 - ADAM_B1 ** ADAM_STEP)
        v_hat = v2 / (1.0 - ADAM_B2 ** ADAM_STEP)
        d_ref[...] = -ADAM_LR * (m_hat / (jnp.sqrt(v_hat) + ADAM_EPS) + ADAM_WD * w_ref[...])
        mo_ref[...] = m2
        vo_ref[...] = v2
        if emit_grad:
            g_out[0][...] = g

    spec = pl.BlockSpec((R, Cc), lambda i: (i, 0))
    shp = jax.ShapeDtypeStruct((Rr, Cc), F32)
    n_out = 4 if emit_grad else 3
    return pl.pallas_call(
        body, name=name, out_shape=(shp,) * n_out, grid=(Rr // R,), in_specs=[spec] * 4, out_specs=(spec,) * n_out,
        compiler_params=_params("parallel"),
    )(w, g, m, v)


def _half_sum(g, ra, c, *, rows_split, name):
    Rr, Cc = ra.shape
    tr, tc = _pick(Rr, 512, 16), _pick(Cc, 2048)
    nr, ncol = Rr // tr, Cc // tc
    if rows_split:
        g_map = lambda i, j, c_ref: (i + c_ref[0] * nr, j)
    else:
        g_map = lambda i, j, c_ref: (i, j + c_ref[0] * ncol)

    def body(c_ref, g_ref, ra_ref, o_ref):
        o_ref[...] = (g_ref[...] + ra_ref[...]).astype(BF16)

    spec = pl.BlockSpec((tr, tc), lambda i, j, c_ref: (i, j))
    return pl.pallas_call(
        body,
        name=name,
        out_shape=jax.ShapeDtypeStruct((Rr, Cc), BF16),
        grid_spec=pltpu.PrefetchScalarGridSpec(
            num_scalar_prefetch=1, grid=(nr, ncol), in_specs=[pl.BlockSpec((tr, tc), g_map), spec], out_specs=spec),
        compiler_params=_params("parallel", "parallel"),
    )(c, g, ra)


def _sum_chips(cs, rb, pc, lay, *, name):
    _, Rr, Cc = rb.shape
    tr, tc = _pick(Rr, 512, 16), _pick(Cc, 2048)
    nr, ncol = Rr // tr, Cc // tc
    if lay.axis == 0:
        own_map = lambda i, j, s: (i + s[0] * nr, j)
        out_map = lambda i, j, s: (i, j + s[1] * ncol)
    else:
        own_map = lambda i, j, s: (i, j + s[0] * ncol)
        out_map = lambda i, j, s: (i + s[1] * nr, j)

    def body(s_ref, own_ref, rb_ref, o_ref):
        acc = own_ref[...].astype(F32)
        for j in range(N_CHIPS - 1):
            acc = acc + rb_ref[j].astype(F32)
        o_ref[...] = acc

    return pl.pallas_call(
        body,
        name=name,
        out_shape=jax.ShapeDtypeStruct(lay.shard_shape(), F32),
        grid_spec=pltpu.PrefetchScalarGridSpec(
            num_scalar_prefetch=1,
            grid=(nr, ncol),
            in_specs=[pl.BlockSpec((tr, tc), own_map), pl.BlockSpec((N_CHIPS - 1, tr, tc), lambda i, j, s: (0, i, j))],
            out_specs=pl.BlockSpec((tr, tc), out_map)),
        compiler_params=_params("parallel", "parallel"),
    )(pc, cs, rb)


def _place():
    x, y, c = lax.axis_index("x"), lax.axis_index("y"), lax.axis_index("c")
    return x, y, c, 2 * x + y


def _other_chips(x, y):
    return [(1 - x, y, 2 * (1 - x) + y), (x, 1 - y, 2 * x + (1 - y)), (1 - x, 1 - y, 2 * (1 - x) + (1 - y))]


def _allgather_small(buf, *, name, reduce):
    S = buf.shape[0]

    def body(x_ref, o_ref, gat, send_sems, recv_sems):
        x, y, c, _ = _place()
        me = 4 * x + 2 * y + c
        gat[me] = x_ref[...]
        copies = []
        for k in range(1, N_DEV):
            fx, fy, fc = (k >> 2) & 1, (k >> 1) & 1, k & 1
            px = 1 - x if fx else x
            py = 1 - y if fy else y
            pc = 1 - c if fc else c
            peer = 4 * px + 2 * py + pc
            send = pltpu.make_async_remote_copy(
                src_ref=x_ref, dst_ref=gat.at[me], send_sem=send_sems.at[k - 1], recv_sem=recv_sems.at[k - 1],
                device_id=(px, py, pc), device_id_type=MESH)
            send.start()
            arrival = pltpu.make_async_remote_copy(
                src_ref=x_ref, dst_ref=gat.at[peer], send_sem=send_sems.at[k - 1], recv_sem=recv_sems.at[k - 1],
                device_id=(px, py, pc), device_id_type=MESH)
            copies.append((send, arrival))
        for send, arrival in copies:
            arrival.wait_recv()
        for send, arrival in copies:
            send.wait_send()
        if reduce:
            acc = gat[0]
            for d in range(1, N_DEV):
                acc = acc + gat[d]
            o_ref[...] = acc
        else:
            o_ref[...] = gat[...]

    out_shape = jax.ShapeDtypeStruct((S, LANES) if reduce else (N_DEV, S, LANES), F32)
    return pl.pallas_call(
        body,
        name=name,
        out_shape=out_shape,
        in_specs=[pl.BlockSpec(memory_space=pltpu.VMEM)],
        out_specs=pl.BlockSpec(memory_space=pltpu.VMEM),
        scratch_shapes=[pltpu.VMEM((N_DEV, S, LANES), F32), pltpu.SemaphoreType.DMA((N_DEV - 1,)),
                        pltpu.SemaphoreType.DMA((N_DEV - 1,))],
        compiler_params=pltpu.CompilerParams(vmem_limit_bytes=VMEM_LIMIT),
    )(buf)


class _Sharded:
    def __init__(self, shape, axis):
        self.shape = shape
        self.axis = axis
        self.block = shape[axis] // N_CHIPS
        self.half = shape[1 - axis] // 2

    def _sl(self, along, across):
        return (along, across) if self.axis == 0 else (across, along)

    def block_slice(self, q):
        return self._sl(pl.ds(q * self.block, self.block), pl.ds(0, self.shape[1 - self.axis]))

    def block_half_slice(self, q, c):
        return self._sl(pl.ds(q * self.block, self.block), pl.ds(c * self.half, self.half))

    def half_slice(self, c):
        return self._sl(pl.ds(0, self.shape[self.axis]), pl.ds(c * self.half, self.half))

    def shard_half_slice(self, c):
        return self._sl(pl.ds(0, self.block), pl.ds(c * self.half, self.half))

    def shard_shape(self):
        return self._sl(self.block, self.shape[1 - self.axis])

    def half_shape(self):
        return self._sl(self.shape[self.axis], self.half)

    def block_half_shape(self):
        return self._sl(self.block, self.half)

    def block_in_half_slice(self, q):
        return self._sl(pl.ds(q * self.block, self.block), pl.ds(0, self.half))


def _at(ref, sl):
    return ref.at[sl[0], sl[1]]


class _Copy:
    def __init__(self, src, dst, arrive, dev):
        self.src, self.dst, self.arrive, self.dev = src, dst, arrive, dev


HBM = pl.BlockSpec(memory_space=pltpu.HBM)
SEM = pl.BlockSpec(memory_space=pltpu.SEMAPHORE)
EFFECT = pltpu.SideEffectType.DATAFLOW_SIDE_EFFECTING


def _exchange_start(srcs, land_shapes, plan, after, *, name):
    ns, nl = len(srcs), len(land_shapes)
    n_copies = len(plan([None] * ns, [None] * nl, dry=True))

    def body(*refs):
        src_refs, land_refs = refs[:ns], refs[ns:ns + nl]
        send_sems, recv_sems = refs[ns + nl + 1], refs[ns + nl + 2]
        token = refs[-1]
        for k, cp in enumerate(plan(src_refs, land_refs)):
            pltpu.make_async_remote_copy(
                src_ref=cp.src, dst_ref=cp.dst, send_sem=send_sems.at[k], recv_sem=recv_sems.at[k],
                device_id=cp.dev, device_id_type=MESH).start()
        token[...] = jnp.zeros_like(token)

    sem = pltpu.SemaphoreType.DMA((n_copies,))
    lands = [pltpu.with_memory_space_constraint(lax.empty(shp, dt), pltpu.HBM) for shp, dt in land_shapes]
    srcs = [pltpu.with_memory_space_constraint(a, pltpu.HBM) for a in srcs]
    thru = [pltpu.HBM(a.shape, a.dtype) for a in srcs + lands]
    outs = pl.pallas_call(
        body,
        name=name,
        out_shape=(sem, sem, *thru, jax.ShapeDtypeStruct((8, LANES), F32)),
        in_specs=[HBM] * (ns + nl) + [ANY],
        out_specs=(SEM, SEM, *[HBM] * (ns + nl), pl.BlockSpec(memory_space=pltpu.VMEM)),
        input_output_aliases={i: 2 + i for i in range(ns + nl)},
        compiler_params=pltpu.CompilerParams(has_side_effects=EFFECT),
    )(*srcs, *lands, after)
    return outs[0], outs[1], list(outs[2:2 + ns]), list(outs[2 + ns:2 + ns + nl]), outs[-1]


def _exchange_wait(send_sems, recv_sems, srcs, lands, plan, after, *, name):
    ns, nl = len(srcs), len(lands)
    after = tuple(after) if isinstance(after, (tuple, list)) else (after,)

    def body(*refs):
        src_refs, land_refs = refs[:ns], refs[ns:ns + nl]
        send_sems, recv_sems = refs[ns + nl], refs[ns + nl + 1]
        copies = [
            pltpu.make_async_remote_copy(
                src_ref=cp.src, dst_ref=cp.arrive, send_sem=send_sems.at[k], recv_sem=recv_sems.at[k],
                device_id=cp.dev, device_id_type=MESH)
            for k, cp in enumerate(plan(src_refs, land_refs))
        ]
        for cp in copies:
            cp.wait_recv()
        for cp in copies:
            cp.wait_send()

    thru = [pltpu.HBM(a.shape, a.dtype) for a in list(srcs) + list(lands)]
    outs = pl.pallas_call(
        body,
        name=name,
        out_shape=tuple(thru),
        in_specs=[HBM] * (ns + nl) + [SEM, SEM] + [ANY] * len(after),
        out_specs=tuple([HBM] * (ns + nl)),
        input_output_aliases={i: i for i in range(ns + nl)},
        compiler_params=pltpu.CompilerParams(has_side_effects=EFFECT),
    )(*srcs, *lands, send_sems, recv_sems, *after)
    return list(outs[:ns]), list(outs[ns:])


def _gather_plan(layouts):
    def plan(srcs, lands, dry=False):
        if dry:
            return [None] * (4 * len(layouts))
        x, y, c, p = _place()
        copies = []
        for s, g, lay in zip(srcs, lands, layouts):
            own = _at(g, lay.block_slice(p))
            copies.append(_Copy(s, own, own, (x, y, 1 - c)))
            for qx, qy, q in _other_chips(x, y):
                copies.append(_Copy(_at(s, lay.shard_half_slice(c)), _at(g, lay.block_half_slice(p, c)),
                                    _at(g, lay.block_half_slice(q, c)), (qx, qy, c)))
        return copies

    return plan


def _forward_plan(layouts):
    def plan(srcs, lands, dry=False):
        if dry:
            return [None] * (3 * len(layouts))
        x, y, c, _ = _place()
        copies = []
        for g, lay in zip(srcs, layouts):
            for qx, qy, q in _other_chips(x, y):
                landed = _at(g, lay.block_half_slice(q, c))
                copies.append(_Copy(landed, landed, _at(g, lay.block_half_slice(q, 1 - c)), (x, y, 1 - c)))
        return copies

    return plan


def _sibling_halves_plan(layouts):
    def plan(srcs, lands, dry=False):
        if dry:
            return [None] * len(layouts)
        x, y, c, _ = _place()
        return [_Copy(_at(s, lay.half_slice(1 - c)), ra, ra, (x, y, 1 - c)) for s, ra, lay in zip(srcs, lands, layouts)]

    return plan


def _chips_plan(layouts):
    def plan(srcs, lands, dry=False):
        if dry:
            return [None] * (3 * len(layouts))
        x, y, c, _ = _place()
        copies = []
        for s, rb, lay in zip(srcs, lands, layouts):
            for j, (qx, qy, q) in enumerate(_other_chips(x, y)):
                copies.append(_Copy(_at(s, lay.block_in_half_slice(q)), rb.at[j], rb.at[j], (qx, qy, c)))
        return copies

    return plan


def _join_plan(layouts):
    def plan(srcs, lands, dry=False):
        if dry:
            return [None] * len(layouts)
        x, y, c, _ = _place()
        copies = []
        for g, lay in zip(srcs, layouts):
            mine = _at(g, lay.shard_half_slice(c))
            copies.append(_Copy(mine, mine, _at(g, lay.shard_half_slice(1 - c)), (x, y, 1 - c)))
        return copies

    return plan


def _pack(arrays):
    flat = [a.reshape(-1) for a in arrays]
    sizes = [f.shape[0] for f in flat]
    total = sum(sizes)
    rows = _round_up(-(-total // LANES), 8)
    flat.append(jnp.zeros((rows * LANES - total,), F32))
    return jnp.concatenate(flat).reshape(rows, LANES)


def _unpack(buf, shapes):
    flat = buf.reshape(-1)
    out, pos = [], 0
    for shp in shapes:
        n = 1
        for d in shp:
            n *= d
        out.append(flat[pos:pos + n].reshape(shp))
        pos += n
    return out


def _pad_to(a, rows, cols):
    return jnp.pad(a, ((0, rows - a.shape[0]), (0, cols - a.shape[1])))


def kernel(x, mem, g_mix, w_in, conv_a_w, conv_a_b, ln_a_g, ln_a_b, conv_b_w, w_out, g_xattn, g_mem, w_q, w_k, w_v, w_o, g_ffn, w_gate, w_up, conv_f_w, w_down, g_final, loss_target, m_g_mix, m_w_in, m_conv_a_w, m_conv_a_b, m_ln_a_g, m_ln_a_b, m_conv_b_w, m_w_out, m_g_xattn, m_g_mem, m_w_q, m_w_k, m_w_v, m_w_o, m_g_ffn, m_w_gate, m_w_up, m_conv_f_w, m_w_down, m_g_final, v_g_mix, v_w_in, v_conv_a_w, v_conv_a_b, v_ln_a_g, v_ln_a_b, v_conv_b_w, v_w_out, v_g_xattn, v_g_mem, v_w_q, v_w_k, v_w_v, v_w_o, v_g_ffn, v_w_gate, v_w_up, v_conv_f_w, v_w_down, v_g_final):
    T, D = x.shape[1], x.shape[2]
    in_sh = w_in.shape[2]
    cw_sh = conv_a_w.shape[2]
    cw = N_CHIPS * cw_sh
    f_sh = w_gate.shape[2]
    fp = _round_up(f_sh, 256)
    F = N_CHIPS * fp
    rs = w_out.shape[1]
    c_idx = lax.axis_index("c")
    p_idx = 2 * lax.axis_index("x") + lax.axis_index("y")

    def t_(a):
        return jnp.swapaxes(a[0], 0, 1)

    big = {
        "w_in": (lambda: w_in[0].astype(BF16), _Sharded((D, N_CHIPS * in_sh), 1)),
        "w_out": (lambda: w_out[0].astype(BF16), _Sharded((N_CHIPS * rs, D), 0)),
        "w_q": (lambda: w_q[0].astype(BF16), _Sharded((D, D), 0)),
        "w_k": (lambda: w_k[0].astype(BF16), _Sharded((D, D), 0)),
        "w_v": (lambda: w_v[0].astype(BF16), _Sharded((D, D), 0)),
        "w_o": (lambda: w_o[0].astype(BF16), _Sharded((D, D), 0)),
        "w_gate": (lambda: _pad_to(t_(w_gate).astype(BF16), fp, D), _Sharded((F, D), 0)),
        "w_up": (lambda: _pad_to(t_(w_up).astype(BF16), fp, D), _Sharded((F, D), 0)),
        "w_down": (lambda: _pad_to(w_down[0].astype(BF16), fp, D), _Sharded((F, D), 0)),
    }
    names = list(big)
    lay = {k: big[k][1] for k in names}
    c_arr = c_idx.astype(jnp.int32).reshape(1)
    pc_arr = jnp.stack([p_idx, c_idx]).astype(jnp.int32)

    conv_shapes = [(KA_ROWS, cw_sh), (KS_ROWS, cw_sh), (KS_ROWS, fp)]
    conv_pack = _pack([_pad_to(conv_a_w[0], KA_ROWS, cw_sh), _pad_to(conv_b_w[0], KS_ROWS, cw_sh),
                       _pad_to(conv_f_w[0], KS_ROWS, fp)])
    conv_all = _allgather_small(conv_pack, name="allgather_conv", reduce=False)
    per_chip = [_unpack(conv_all[2 * q], conv_shapes) for q in range(N_CHIPS)]
    wa = jnp.concatenate([pc[0] for pc in per_chip], axis=1)
    wb = jnp.concatenate([pc[1] for pc in per_chip], axis=1)
    wf = jnp.concatenate([pc[2] for pc in per_chip], axis=1)

    tok = conv_all

    gather_groups = [["w_in"], ["w_out"], ["w_q", "w_k", "w_v"], ["w_o"], ["w_gate"], ["w_up"], ["w_down"]]
    gathers = []
    for gi, grp in enumerate(gather_groups):
        lays = [lay[k] for k in grp]
        plan = _gather_plan(lays)
        ssem, rsem, srcs, lands, tok = _exchange_start(
            [big[k][0]() for k in grp], [(l.shape, BF16) for l in lays], plan, tok, name=f"gather_start_{gi}")
        gathers.append((ssem, rsem, srcs, lands, plan, lays, grp))
    W = {}
    relays = {}

    def relay(gi, after):
        ssem, rsem, srcs, lands, plan, lays, grp = gathers[gi]
        _, lands = _exchange_wait(ssem, rsem, srcs, lands, plan, after, name=f"gather_wait_{gi}")
        plan = _forward_plan(lays)
        ssem, rsem, lands, _, token = _exchange_start(lands, [], plan, after, name=f"gather_forward_start_{gi}")
        relays[gi] = (ssem, rsem, lands, plan, grp)
        return token

    def gathered(gi, after):
        if gi not in relays:
            relay(gi, after)
        ssem, rsem, lands, plan, grp = relays[gi]
        lands, _ = _exchange_wait(ssem, rsem, lands, [], plan, after, name=f"gather_forward_wait_{gi}")
        W.update(zip(grp, lands))

    relay(0, tok)

    xs, tgt, mems = x[0], loss_target[0], mem[0]
    g_mem2, g_final2 = g_mem[None, :], g_final[None, :]

    memn, rm = _norm_fwd(mems, g_mem2, name="norm_mem")
    xn1, r1 = _norm_fwd(xs, g_mix, name="norm_mix")
    gathered(0, xn1)
    proj = _mm(xn1, W["w_in"], name="mm_in")
    ftok = relay(1, proj)
    u3, u1 = _mix_a_fwd(proj, wa, conv_a_b, ln_a_g, ln_a_b, cw=cw, deps=[ftok], name="mix_a_fwd")
    vb = _mix_b_fwd(proj, wb, cw=cw, sw=cw, name="mix_b_fwd")
    mix = jnp.concatenate([u3, vb], axis=1)
    gathered(1, mix)
    ftok = relay(2, mix)
    h1 = _mm(mix, W["w_out"], add=xs, deps=[ftok], name="mm_out")
    xn2, r2 = _norm_fwd(h1, g_xattn, name="norm_xattn")
    gathered(2, xn2)
    ftok = relay(3, xn2)
    q = _mm(xn2, W["w_q"], out_dtype=BF16, deps=[ftok], name="mm_q")
    k = _mm(memn, W["w_k"], out_dtype=BF16, name="mm_k")
    vm = _mm(memn, W["w_v"], out_dtype=BF16, name="mm_v")
    o = _attn_fwd(q, k, vm, name="attn_fwd")
    gathered(3, o)
    h2 = _mm(o, W["w_o"], add=h1, name="mm_o")
    relay(4, h2)
    xn3, r3 = _norm_fwd(h2, g_ffn, name="norm_ffn")
    gathered(4, xn3)
    gp = _mm(xn3, W["w_gate"], tb=True, name="mm_gate")
    gathered(5, gp)
    up = _mm(xn3, W["w_up"], tb=True, name="mm_up")
    f = _ffn_act_fwd(gp, up, wf, name="ffn_act_fwd")
    gathered(6, f)
    h3 = _mm(f, W["w_down"], add=h2, name="mm_down")

    G = {}

    def siblings_start(tag, grp, after):
        lays = [lay[k] for k in grp]
        plan = _sibling_halves_plan(lays)
        ssem, rsem, srcs, lands, token = _exchange_start(
            [G[k] for k in grp], [(l.half_shape(), F32) for l in lays], plan, after, name=f"rs_siblings_start_{tag}")
        return (tag, grp, lays, plan, ssem, rsem, srcs, lands), token

    def chips_start(state, after):
        tag, grp, lays, plan, ssem, rsem, srcs, lands = state
        srcs, lands = _exchange_wait(ssem, rsem, srcs, lands, plan, after, name=f"rs_siblings_wait_{tag}")
        sums = [_half_sum(g, ra, c_arr, rows_split=(l.axis == 1), name=f"half_sum_{k}")
                for k, g, ra, l in zip(grp, srcs, lands, lays)]
        plan = _chips_plan(lays)
        ssem, rsem, sums, lands, token = _exchange_start(
            sums, [((N_CHIPS - 1,) + l.block_half_shape(), BF16) for l in lays], plan, after, name=f"rs_chips_start_{tag}")
        return (tag, grp, lays, plan, ssem, rsem, sums, lands), token

    def join_start(state, after):
        tag, grp, lays, plan, ssem, rsem, sums, lands = state
        sums, lands = _exchange_wait(ssem, rsem, sums, lands, plan, after, name=f"rs_chips_wait_{tag}")
        halves = [_sum_chips(cs, rb, pc_arr, l, name=f"sum_chips_{k}") for k, cs, rb, l in zip(grp, sums, lands, lays)]
        plan = _join_plan(lays)
        ssem, rsem, halves, _, token = _exchange_start(halves, [], plan, after[0], name=f"rs_join_start_{tag}")
        return (tag, grp, plan, ssem, rsem, halves), token

    def join_finish(state, after):
        tag, grp, plan, ssem, rsem, halves = state
        halves, _ = _exchange_wait(ssem, rsem, halves, [], plan, after, name=f"rs_join_wait_{tag}")
        return dict(zip(grp, halves))

    loss_rows, dh3, dh3b, dg_final = _loss_head(h3, g_final2, tgt, name="loss_head")
    df = _mm(dh3b, W["w_down"], tb=True, out_dtype=BF16, name="mm_d_f")
    G["w_down"] = _mm(f, dh3b, ta=True, name="mm_dw_down")
    rs_down, tok = siblings_start("down", ["w_down"], tok)
    dgp, dup, dwf = _ffn_act_bwd(gp, up, df, wf, name="ffn_act_bwd")
    G["w_gate"] = _mm(dgp, xn3, ta=True, deps=[tok], name="mm_dw_gate")
    G["w_up"] = _mm(dup, xn3, ta=True, name="mm_dw_up")
    rs_ffn, tok = siblings_start("ffn", ["w_gate", "w_up"], tok)
    rs_down, tok = chips_start(rs_down, tok)
    dxn3 = _mm(dgp, W["w_gate"], deps=[tok], name="mm_dxn3_gate")
    dxn3 = _mm(dup, W["w_up"], add=dxn3, name="mm_dxn3_up")
    dh2, dh2b, dg_ffn = _norm_bwd(h2, r3, g_ffn, dxn3, dh3, name="norm_ffn_bwd")
    rs_ffn, tok = chips_start(rs_ffn, dh2b)
    do = _mm(dh2b, W["w_o"], tb=True, out_dtype=BF16, deps=[tok], name="mm_d_o")
    G["w_o"] = _mm(o, dh2b, ta=True, name="mm_dw_o")
    dq, dk, dvm = _attn_bwd(q, k, vm, do, name="attn_bwd")
    dkb, dvb = dk.astype(BF16), dvm.astype(BF16)
    G["w_q"] = _mm(xn2, dq, ta=True, name="mm_dw_q")
    G["w_k"] = _mm(memn, dkb, ta=True, name="mm_dw_k")
    G["w_v"] = _mm(memn, dvb, ta=True, name="mm_dw_v")
    rs_att, tok = siblings_start("att", ["w_o", "w_q", "w_k", "w_v"], tok)
    dxn2 = _mm(dq, W["w_q"], tb=True, deps=[tok], name="mm_dxn2")
    dmemn = _mm(dkb, W["w_k"], tb=True, name="mm_dmem_k")
    dmemn = _mm(dvb, W["w_v"], tb=True, add=dmemn, name="mm_dmem_v")
    dg_mem = _norm_bwd(mems, rm, g_mem2, dmemn, None, name="norm_mem_bwd", want_dh=False)
    dh1, dh1b, dg_xattn = _norm_bwd(h1, r2, g_xattn, dxn2, dh2, name="norm_xattn_bwd")
    rs_att, tok = chips_start(rs_att, dh1b)
    dmix = _mm(dh1b, W["w_out"], tb=True, deps=[tok], name="mm_d_mix")
    G["w_out"] = _mm(mix, dh1b, ta=True, name="mm_dw_out")
    rs_out, tok = siblings_start("out", ["w_out"], tok)
    dav, dag, dwa, dba, dlg, dlb = _mix_a_bwd(proj, u1, dmix, wa, ln_a_g, ln_a_b, cw=cw, name="mix_a_bwd")
    dbg, dcg, dbh, dwb = _mix_b_bwd(proj, dmix, wb, cw=cw, sw=cw, name="mix_b_bwd")
    dproj = jnp.concatenate([dav, dag, dbg, dcg, dbh], axis=1)
    rs_out, tok = chips_start(rs_out, dproj)
    G["w_in"] = _mm(xn1, dproj, ta=True, deps=[tok], name="mm_dw_in")
    rs_in, tok = siblings_start("in", ["w_in"], tok)
    dxn1 = _mm(dproj, W["w_in"], tb=True, deps=[tok], name="mm_dxn1")
    dx, _, dg_mix = _norm_bwd(xs, r1, g_mix, dxn1, dh1, name="norm_mix_bwd")
    rs_in, tok = chips_start(rs_in, dx)

    loss_part = jnp.sum(loss_rows).reshape(1, 1)
    small_parts = [dg_mix, dba, dlg, dlb, dg_xattn, dg_mem, dg_ffn, dg_final, dwa, dwb, dwf, loss_part]
    small_shapes = [a.shape for a in small_parts]
    reduced = _allgather_small(_pack(small_parts), name="allreduce_small", reduce=True)
    (sg_mix, sba, slg, slb, sg_xattn, sg_mem, sg_ffn, sg_final, swa, swb, swf, loss_sum) = _unpack(reduced, small_shapes)
    loss = loss_sum.reshape(())
    ga_w = lax.dynamic_slice(swa, (0, p_idx * cw_sh), (KA, cw_sh))
    gb_w = lax.dynamic_slice(swb, (0, p_idx * cw_sh), (KS, cw_sh))
    gf_w = lax.dynamic_slice(swf, (0, p_idx * fp), (KS, f_sh))

    weights = dict(g_mix=g_mix, w_in=w_in, conv_a_w=conv_a_w, conv_a_b=conv_a_b, ln_a_g=ln_a_g, ln_a_b=ln_a_b,
                   conv_b_w=conv_b_w, w_out=w_out, g_xattn=g_xattn, g_mem=g_mem, w_q=w_q, w_k=w_k, w_v=w_v, w_o=w_o,
                   g_ffn=g_ffn, w_gate=w_gate, w_up=w_up, conv_f_w=conv_f_w, w_down=w_down, g_final=g_final)
    m_in = dict(g_mix=m_g_mix, w_in=m_w_in, conv_a_w=m_conv_a_w, conv_a_b=m_conv_a_b, ln_a_g=m_ln_a_g, ln_a_b=m_ln_a_b,
                conv_b_w=m_conv_b_w, w_out=m_w_out, g_xattn=m_g_xattn, g_mem=m_g_mem, w_q=m_w_q, w_k=m_w_k, w_v=m_w_v,
                w_o=m_w_o, g_ffn=m_g_ffn, w_gate=m_w_gate, w_up=m_w_up, conv_f_w=m_conv_f_w, w_down=m_w_down,
                g_final=m_g_final)
    v_in = dict(g_mix=v_g_mix, w_in=v_w_in, conv_a_w=v_conv_a_w, conv_a_b=v_conv_a_b, ln_a_g=v_ln_a_g, ln_a_b=v_ln_a_b,
                conv_b_w=v_conv_b_w, w_out=v_w_out, g_xattn=v_g_xattn, g_mem=v_g_mem, w_q=v_w_q, w_k=v_w_k, w_v=v_w_v,
                w_o=v_w_o, g_ffn=v_g_ffn, w_gate=v_w_gate, w_up=v_w_up, conv_f_w=v_conv_f_w, w_down=v_w_down,
                g_final=v_g_final)
    order = list(weights)
    grads = dict(g_mix=sg_mix, conv_a_w=ga_w, conv_a_b=sba, ln_a_g=slg, ln_a_b=slb, conv_b_w=gb_w, g_xattn=sg_xattn,
                 g_mem=sg_mem, g_ffn=sg_ffn, conv_f_w=gf_w, g_final=sg_final)
    grads = {k: g.reshape(weights[k].shape) for k, g in grads.items()}

    delta, new_m, new_v = {}, {}, {}
    small = [k for k in order if k not in big]
    small_shapes = [weights[k].shape for k in small]
    packed = [_pack([src[k] for k in small]) for src in (weights, grads, m_in, v_in)]
    d_, m_, v_ = _adamw(*packed, name="adamw_small")
    for k, dd, mm_, vv in zip(small, _unpack(d_, small_shapes), _unpack(m_, small_shapes), _unpack(v_, small_shapes)):
        delta[k], new_m[k], new_v[k] = dd, mm_, vv

    transposed = ("w_gate", "w_up")

    def update(shard_grads):
        last = None
        for k, g in shard_grads.items():
            view = t_ if k in transposed else (lambda a: a[0])
            back = (lambda a: jnp.swapaxes(a, 0, 1)[None]) if k in transposed else (lambda a: a[None])
            padded = g.shape != view(weights[k]).shape
            outs = _adamw(view(weights[k]), g, view(m_in[k]), view(v_in[k]), emit_grad=padded, name=f"adamw_{k}")
            delta[k], new_m[k], new_v[k] = back(outs[0]), back(outs[1]), back(outs[2])
            grads[k] = back(outs[3] if padded else g)
            last = outs[0]
        return last

    after = (d_, tok)
    joining = None
    for state in (rs_down, rs_ffn, rs_att, rs_out, rs_in):
        started, token = join_start(state, after)
        after = (token,)
        if joining is not None:
            after = (update(join_finish(joining, token)), token)
        joining = started
    update(join_finish(joining, after[0]))

    return (loss, dx[None], *[grads[k] for k in order], *[delta[k] for k in order],
            *[new_m[k] for k in order], *[new_v[k] for k in order])
```

```python
import functools

import jax
import jax.numpy as jnp
from jax import lax
from jax.experimental import pallas as pl
from jax.experimental.pallas import tpu as pltpu

F32 = jnp.float32
BF16 = jnp.bfloat16
EPS = 1e-6
N_HEADS = 4
GROUP = 128
KA = 31
KS = 3
KA_ROWS = 32
KS_ROWS = 8
HALO_A = 32
HALO_S = 8
CHUNK_ROWS = 16
CHUNK_COLS = 256
N_CHIPS = 4
N_DEV = 8
LANES = 128
VMEM_LIMIT = 56 * 1024 * 1024
MM_WIDE_TILE_ELEMS = 4096 * 512
ADAMW_BLOCK_BYTES = 1 << 20
MESH = pl.DeviceIdType.MESH
ANY = pl.BlockSpec(memory_space=pl.ANY)

ADAM_LR = 0.001
ADAM_B1 = 0.9
ADAM_B2 = 0.999
ADAM_EPS = 1e-08
ADAM_WD = 0.01
ADAM_STEP = 10


def _pick(dim, pref, mult=LANES):
    if dim <= pref:
        return dim
    t = (pref // mult) * mult
    while t >= mult:
        if dim % t == 0:
            return t
        t -= mult
    return dim


def _round_up(n, m):
    return ((n + m - 1) // m) * m


def _params(*sem):
    return pltpu.CompilerParams(dimension_semantics=sem, vmem_limit_bytes=VMEM_LIMIT)


def _sigmoid(x):
    return jax.nn.sigmoid(x)


def _mm(a, b, *, name, ta=False, tb=False, out_dtype=F32, add=None, deps=(), half=None, tm=1024, tn=512, tk=5632):
    if ta:
        K, M = a.shape
    else:
        M, K = a.shape
    if tb:
        N, K2 = b.shape
    else:
        K2, N = b.shape
    assert K == K2, (a.shape, b.shape)
    half_dim, half_sel = half if half is not None else (None, None)
    if half_dim == "m":
        M //= 2
    elif half_dim == "n":
        N //= 2
    tm, tk = _pick(M, tm), _pick(K, tk)
    nk = K // tk
    tn = _pick(N, 2 * tn if nk == 1 and K * tn <= MM_WIDE_TILE_ELEMS else tn)
    mi, nj = M // tm, N // tn

    def a_map(i, j, k, *s):
        i = i + s[0][0] * mi if half_dim == "m" else i
        return (k, i) if ta else (i, k)

    def b_map(i, j, k, *s):
        j = j + s[0][0] * nj if half_dim == "n" else j
        return (j, k) if tb else (k, j)

    a_spec = pl.BlockSpec((tk, tm) if ta else (tm, tk), a_map)
    b_spec = pl.BlockSpec((tn, tk) if tb else (tk, tn), b_map)
    o_spec = pl.BlockSpec((tm, tn), lambda i, j, k, *s: (i, j))
    dims = (((0,) if ta else (1,), (1,) if tb else (0,)), ((), ()))
    has_add = add is not None
    n_pre = 0 if half is None else 1

    def body(*refs):
        refs = refs[n_pre:]
        a_ref, b_ref = refs[:2]
        add_ref = refs[2] if has_add else None
        o_ref = refs[(3 if has_add else 2) + len(deps)]
        acc_ref = refs[-1] if nk > 1 else None
        k = pl.program_id(2)
        part = lax.dot_general(a_ref[...], b_ref[...], dims, preferred_element_type=F32)

        def finish(r):
            if add_ref is not None:
                r = add_ref[...] + r
            o_ref[...] = r.astype(out_dtype)

        if nk == 1:
            finish(part)
        else:
            @pl.when(k == 0)
            def _():
                acc_ref[...] = part

            @pl.when(jnp.logical_and(k > 0, k < nk - 1))
            def _():
                acc_ref[...] += part

            @pl.when(k == nk - 1)
            def _():
                finish(acc_ref[...] + part)

    in_specs = [a_spec, b_spec] + ([o_spec] if has_add else []) + [ANY] * len(deps)
    args = (a, b) + ((add,) if has_add else ()) + tuple(deps)
    scratch = [pltpu.VMEM((tm, tn), F32)] if nk > 1 else []
    grid = (mi, nj, nk)
    if half is None:
        grid_spec = dict(grid=grid, in_specs=in_specs, out_specs=o_spec, scratch_shapes=scratch)
    else:
        args = (half_sel,) + args
        grid_spec = dict(grid_spec=pltpu.PrefetchScalarGridSpec(
            num_scalar_prefetch=1, grid=grid, in_specs=in_specs, out_specs=o_spec, scratch_shapes=scratch))
    return pl.pallas_call(
        body,
        name=name,
        out_shape=jax.ShapeDtypeStruct((M, N), out_dtype),
        compiler_params=_params("parallel", "parallel", "arbitrary"),
        **grid_spec,
    )(*args)


def _norm_fwd(h, g, *, name):
    T, D = h.shape
    R = _pick(T, 256, 8)

    def body(h_ref, g_ref, xn_ref, r_ref):
        x = h_ref[...]
        r = lax.rsqrt(jnp.mean(x * x, axis=-1, keepdims=True) + EPS)
        xn_ref[...] = ((x * r) * g_ref[...]).astype(BF16)
        r_ref[...] = r

    return pl.pallas_call(
        body,
        name=name,
        out_shape=(jax.ShapeDtypeStruct((T, D), BF16), jax.ShapeDtypeStruct((T, 1), F32)),
        grid=(T // R,),
        in_specs=[pl.BlockSpec((R, D), lambda i: (i, 0)), pl.BlockSpec((1, D), lambda i: (0, 0))],
        out_specs=(pl.BlockSpec((R, D), lambda i: (i, 0)), pl.BlockSpec((R, 1), lambda i: (i, 0))),
        compiler_params=_params("parallel"),
    )(h, g)


def _norm_bwd(h, r, g, dxn, dres, *, name, want_dh=True):
    T, D = h.shape
    R = _pick(T, 128, 8)
    has_res = dres is not None

    def body(*refs):
        h_ref, r_ref, g_ref, dxn_ref = refs[:4]
        pos = 4
        dres_ref = None
        if has_res:
            dres_ref = refs[pos]
            pos += 1
        if want_dh:
            dh_ref, dhb_ref, dg_ref = refs[pos:pos + 3]
        else:
            dg_ref = refs[pos]
        i = pl.program_id(0)
        rr = r_ref[...]
        hn = h_ref[...] * rr
        d = dxn_ref[...].astype(F32)
        gd = d * g_ref[...]
        part = jnp.sum(d * hn, axis=0, keepdims=True)

        @pl.when(i == 0)
        def _():
            dg_ref[...] = part

        @pl.when(i > 0)
        def _():
            dg_ref[...] += part

        if want_dh:
            dh = rr * (gd - hn * jnp.mean(gd * hn, axis=-1, keepdims=True))
            if dres_ref is not None:
                dh = dres_ref[...] + dh
            dh_ref[...] = dh
            dhb_ref[...] = dh.astype(BF16)

    row = pl.BlockSpec((R, D), lambda i: (i, 0))
    vec = pl.BlockSpec((1, D), lambda i: (0, 0))
    in_specs = [row, pl.BlockSpec((R, 1), lambda i: (i, 0)), vec, row] + ([row] if has_res else [])
    args = (h, r, g, dxn) + ((dres,) if has_res else ())
    if want_dh:
        out_shape = (jax.ShapeDtypeStruct((T, D), F32), jax.ShapeDtypeStruct((T, D), BF16), jax.ShapeDtypeStruct((1, D), F32))
        out_specs = (row, row, vec)
    else:
        out_shape = jax.ShapeDtypeStruct((1, D), F32)
        out_specs = vec
    return pl.pallas_call(
        body, name=name, out_shape=out_shape, grid=(T // R,), in_specs=in_specs, out_specs=out_specs,
        compiler_params=_params("arbitrary"),
    )(*args)


def _loss_head(h, g, tgt, *, name):
    T, D = h.shape
    R = _pick(T, 128, 8)

    def body(h_ref, g_ref, t_ref, loss_ref, dh_ref, dhb_ref, dg_ref):
        i = pl.program_id(0)
        x = h_ref[...]
        gg = g_ref[...]
        r = lax.rsqrt(jnp.mean(x * x, axis=-1, keepdims=True) + EPS)
        hn = x * r
        e = hn * gg - t_ref[...]
        loss_ref[...] = 0.5 * jnp.mean(e * e, axis=-1, keepdims=True)
        dy = e * (1.0 / D)
        gd = dy * gg
        dh = r * (gd - hn * jnp.mean(gd * hn, axis=-1, keepdims=True))
        dh_ref[...] = dh
        dhb_ref[...] = dh.astype(BF16)
        part = jnp.sum(dy * hn, axis=0, keepdims=True)

        @pl.when(i == 0)
        def _():
            dg_ref[...] = part

        @pl.when(i > 0)
        def _():
            dg_ref[...] += part

    row = pl.BlockSpec((R, D), lambda i: (i, 0))
    vec = pl.BlockSpec((1, D), lambda i: (0, 0))
    return pl.pallas_call(
        body,
        name=name,
        out_shape=(jax.ShapeDtypeStruct((T, 1), F32), jax.ShapeDtypeStruct((T, D), F32),
                   jax.ShapeDtypeStruct((T, D), BF16), jax.ShapeDtypeStruct((1, D), F32)),
        grid=(T // R,),
        in_specs=[row, vec, row],
        out_specs=(pl.BlockSpec((R, 1), lambda i: (i, 0)), row, row, vec),
        compiler_params=_params("arbitrary"),
    )(h, g, tgt)


_NT = (((1,), (1,)), ((), ()))
_TN = (((0,), (0,)), ((), ()))
_NN = (((1,), (0,)), ((), ()))


def _softmax_rows(s):
    m = jnp.max(s, axis=-1, keepdims=True)
    e = jnp.exp(s - m)
    return e / jnp.sum(e, axis=-1, keepdims=True)


def _attn_fwd(q, k, v, *, name):
    T, D = q.shape
    ML = k.shape[0]
    dh = D // N_HEADS
    scale = dh ** -0.5
    R = _pick(T, 512, 16)

    def body(q_ref, k_ref, v_ref, o_ref):
        s = lax.dot_general(q_ref[...], k_ref[...], _NT, preferred_element_type=F32) * scale
        p = _softmax_rows(s)
        o_ref[...] = lax.dot_general(p.astype(BF16), v_ref[...], _NN, preferred_element_type=F32).astype(BF16)

    qs = pl.BlockSpec((R, dh), lambda i, h: (i, h))
    ks = pl.BlockSpec((ML, dh), lambda i, h: (0, h))
    return pl.pallas_call(
        body, name=name, out_shape=jax.ShapeDtypeStruct((T, D), BF16), grid=(T // R, N_HEADS),
        in_specs=[qs, ks, ks], out_specs=qs, compiler_params=_params("parallel", "parallel"),
    )(q, k, v)


def _attn_bwd(q, k, v, do, *, name):
    T, D = q.shape
    ML = k.shape[0]
    dh = D // N_HEADS
    scale = dh ** -0.5
    R = _pick(T, 512, 16)

    def body(q_ref, k_ref, v_ref, do_ref, dq_ref, dk_ref, dv_ref):
        i = pl.program_id(1)
        qq, kk, vv, dd = q_ref[...], k_ref[...], v_ref[...], do_ref[...]
        s = lax.dot_general(qq, kk, _NT, preferred_element_type=F32) * scale
        p = _softmax_rows(s)
        dp = lax.dot_general(dd, vv, _NT, preferred_element_type=F32)
        dv_part = lax.dot_general(p.astype(BF16), dd, _TN, preferred_element_type=F32)
        ds = (p * (dp - jnp.sum(p * dp, axis=-1, keepdims=True)) * scale).astype(BF16)
        dq_ref[...] = lax.dot_general(ds, kk, _NN, preferred_element_type=F32).astype(BF16)
        dk_part = lax.dot_general(ds, qq, _TN, preferred_element_type=F32)

        @pl.when(i == 0)
        def _():
            dk_ref[...] = dk_part
            dv_ref[...] = dv_part

        @pl.when(i > 0)
        def _():
            dk_ref[...] += dk_part
            dv_ref[...] += dv_part

    qs = pl.BlockSpec((R, dh), lambda h, i: (i, h))
    ks = pl.BlockSpec((ML, dh), lambda h, i: (0, h))
    return pl.pallas_call(
        body,
        name=name,
        out_shape=(jax.ShapeDtypeStruct((T, D), BF16), jax.ShapeDtypeStruct((ML, D), F32), jax.ShapeDtypeStruct((ML, D), F32)),
        grid=(N_HEADS, T // R),
        in_specs=[qs, ks, ks, qs],
        out_specs=(qs, ks, ks),
        compiler_params=_params("parallel", "arbitrary"),
    )(q, k, v, do)


def _prev_halo(R, halo, CT, col):
    per = R // halo
    return lambda i, j: (jnp.maximum(i * per - 1, 0), col(j))


def _next_halo(R, halo, T, col):
    per = R // halo
    last = T // halo - 1
    return lambda i, j: (jnp.minimum((i + 1) * per, last), col(j))


def _swap(f):
    return lambda j, i: f(i, j)


def _ffn_act_fwd(gp, up, wf, *, name):
    T, F = gp.shape
    R = _pick(T, 256, CHUNK_ROWS)
    CT = _pick(F, 1024, CHUNK_COLS)
    H = HALO_S
    RC, CC = CHUNK_ROWS, min(CHUNK_COLS, CT)

    def body(g_ref, gh_ref, u_ref, w_ref, f_ref, ext):
        i = pl.program_id(0)
        ext[pl.ds(0, H), :] = jnp.where(i > 0, gh_ref[...], 0.0)
        ext[pl.ds(H, R), :] = g_ref[...]
        for c0 in range(0, CT, CC):
            cols = pl.ds(c0, CC)
            w0, w1, w2 = [jnp.broadcast_to(w_ref[pl.ds(k, 1), cols], (RC, CC)) for k in range(KS)]
            for r0 in range(0, R, RC):
                g = ext[pl.ds(r0 + H - 2, RC), cols] * w0
                g = g + ext[pl.ds(r0 + H - 1, RC), cols] * w1
                g = g + ext[pl.ds(r0 + H, RC), cols] * w2
                f_ref[pl.ds(r0, RC), cols] = (g * _sigmoid(g) * u_ref[pl.ds(r0, RC), cols]).astype(BF16)

    main = pl.BlockSpec((R, CT), lambda i, j: (i, j))
    return pl.pallas_call(
        body,
        name=name,
        out_shape=jax.ShapeDtypeStruct((T, F), BF16),
        grid=(T // R, F // CT),
        in_specs=[main, pl.BlockSpec((H, CT), _prev_halo(R, H, CT, lambda j: j)), main,
                  pl.BlockSpec((KS_ROWS, CT), lambda i, j: (0, j))],
        out_specs=main,
        scratch_shapes=[pltpu.VMEM((R + H, CT), F32)],
        compiler_params=_params("parallel", "parallel"),
    )(gp, gp, up, wf)


def _ffn_act_bwd(gp, up, df, wf, *, name):
    T, F = gp.shape
    R = _pick(T, 256, CHUNK_ROWS)
    CT = _pick(F, 512, CHUNK_COLS)
    H = HALO_S
    HB = 16
    n_t = T // R
    RC, CC = CHUNK_ROWS, min(CHUNK_COLS, CT)

    def body(g_ref, gp_ref, gn_ref, u_ref, un_ref, d_ref, dn_ref, w_ref, dg_out, du_out, dw_ref, ext, dge):
        i = pl.program_id(1)
        last = i == n_t - 1
        ext[pl.ds(0, H), :] = jnp.where(i > 0, gp_ref[...], 0.0)
        ext[pl.ds(H, R), :] = g_ref[...]
        ext[pl.ds(H + R, H), :] = gn_ref[...]

        def dact(g, u, d):
            sg = _sigmoid(g)
            return d * u * (sg * (1.0 + g * (1.0 - sg))), d * (g * sg)

        dw_rows = [[] for _ in range(KS)]
        for c0 in range(0, CT, CC):
            cols = pl.ds(c0, CC)
            w = [jnp.broadcast_to(w_ref[pl.ds(k, 1), cols], (RC, CC)) for k in range(KS)]
            acc = [jnp.zeros((RC, CC), F32) for _ in range(KS)]
            for r0 in range(0, R, RC):
                rows = pl.ds(r0, RC)
                taps = [ext[pl.ds(r0 + H - 2 + k, RC), cols] for k in range(KS)]
                g = taps[0] * w[0] + taps[1] * w[1] + taps[2] * w[2]
                dg, du = dact(g, u_ref[rows, cols], d_ref[rows, cols].astype(F32))
                du_out[rows, cols] = du.astype(BF16)
                dge[rows, cols] = dg
                acc = [a + dg * t for a, t in zip(acc, taps)]
            taps = [ext[pl.ds(R + H - 2 + k, H), cols] for k in range(KS)]
            g = taps[0] * w[0][:H] + taps[1] * w[1][:H] + taps[2] * w[2][:H]
            d_next = jnp.where(last, 0.0, dn_ref[pl.ds(0, H), cols].astype(F32))
            dg_next, _ = dact(g, un_ref[:, cols], d_next)
            dge[pl.ds(R, H), cols] = jnp.where(last, 0.0, dg_next)
            for r0 in range(0, R, RC):
                dgp = (dge[pl.ds(r0, RC), cols] * w[2] + dge[pl.ds(r0 + 1, RC), cols] * w[1]
                       + dge[pl.ds(r0 + 2, RC), cols] * w[0])
                dg_out[pl.ds(r0, RC), cols] = dgp.astype(BF16)
            for k in range(KS):
                dw_rows[k].append(jnp.sum(acc[k], axis=0, keepdims=True))
        rows = [jnp.concatenate(r, axis=1) if len(r) > 1 else r[0] for r in dw_rows]
        rows.append(jnp.zeros((KS_ROWS - KS, CT), F32))
        _acc_rows_block(dw_ref, i, rows)

    col = lambda j: j
    main = pl.BlockSpec((R, CT), lambda j, i: (i, j))
    prev8 = pl.BlockSpec((H, CT), _swap(_prev_halo(R, H, CT, col)))
    next8 = pl.BlockSpec((H, CT), _swap(_next_halo(R, H, T, col)))
    next16 = pl.BlockSpec((HB, CT), _swap(_next_halo(R, HB, T, col)))
    wspec = pl.BlockSpec((KS_ROWS, CT), lambda j, i: (0, j))
    return pl.pallas_call(
        body,
        name=name,
        out_shape=(jax.ShapeDtypeStruct((T, F), BF16), jax.ShapeDtypeStruct((T, F), BF16), jax.ShapeDtypeStruct((KS_ROWS, F), F32)),
        grid=(F // CT, n_t),
        in_specs=[main, prev8, next8, main, next8, main, next16, wspec],
        out_specs=(main, main, wspec),
        scratch_shapes=[pltpu.VMEM((R + 2 * H, CT), F32), pltpu.VMEM((R + H, CT), F32)],
        compiler_params=_params("parallel", "arbitrary"),
    )(gp, gp, gp, up, up, df, df, wf)


def _acc_rows_block(ref, i, rows):
    *singles, pad = rows

    @pl.when(i == 0)
    def _():
        for k, row in enumerate(singles):
            ref[pl.ds(k, 1), :] = row
        ref[pl.ds(len(singles), pad.shape[0]), :] = pad

    @pl.when(i > 0)
    def _():
        for k, row in enumerate(singles):
            ref[pl.ds(k, 1), :] += row


def _group_stats(x):
    mu = jnp.mean(x, axis=-1, keepdims=True)
    xc = x - mu
    var = jnp.mean(xc * xc, axis=-1, keepdims=True)
    return xc, lax.rsqrt(var + EPS)


def _shifted_rows(rolled, x, cols):
    n = x.shape[0]
    for b in range(1, 8):
        rolled[b - 1, :, cols] = pltpu.roll(x, n - b, axis=0)


def _window(src, rolled, off, rows, cols):
    b = off % 8
    if b == 0:
        return src[pl.ds(off, rows), cols]
    return rolled[b - 1, pl.ds(off - b, rows), cols]


def _mix_a_fwd(proj, wa, ba, lg, lb, *, cw, name, deps=()):
    T = proj.shape[0]
    R = _pick(T, 128, HALO_A)
    CT = _pick(cw, 256)
    nc = cw // CT
    H = HALO_A

    def body(av_ref, ag_ref, avh_ref, agh_ref, w_ref, b_ref, lg_ref, lb_ref, *rest):
        u3_ref, u1_ref, ext, rolled = rest[len(deps):]
        i = pl.program_id(0)
        for gi in range(CT // GROUP):
            cols = pl.ds(gi * GROUP, GROUP)
            ext[pl.ds(0, H), cols] = jnp.where(i > 0, avh_ref[:, cols] * _sigmoid(agh_ref[:, cols]), 0.0)
            ext[pl.ds(H, R), cols] = av_ref[:, cols] * _sigmoid(ag_ref[:, cols])
            _shifted_rows(rolled, ext[:, cols], cols)
            acc = _window(ext, rolled, H - (KA - 1), R, cols) * w_ref[pl.ds(0, 1), cols]
            for k in range(1, KA):
                acc = acc + _window(ext, rolled, H - (KA - 1) + k, R, cols) * w_ref[pl.ds(k, 1), cols]
            u1 = acc + b_ref[:, cols]
            u1_ref[:, cols] = u1
            xc, rstd = _group_stats(u1)
            u2 = (xc * rstd) * lg_ref[:, cols] + lb_ref[:, cols]
            u3_ref[:, cols] = (u2 * _sigmoid(u2)).astype(BF16)

    main_v = pl.BlockSpec((R, CT), lambda i, j: (i, j))
    main_g = pl.BlockSpec((R, CT), lambda i, j: (i, j + nc))
    halo_v = pl.BlockSpec((H, CT), _prev_halo(R, H, CT, lambda j: j))
    halo_g = pl.BlockSpec((H, CT), _prev_halo(R, H, CT, lambda j: j + nc))
    vec = pl.BlockSpec((1, CT), lambda i, j: (0, j))
    out = pl.BlockSpec((R, CT), lambda i, j: (i, j))
    return pl.pallas_call(
        body,
        name=name,
        out_shape=(jax.ShapeDtypeStruct((T, cw), BF16), jax.ShapeDtypeStruct((T, cw), F32)),
        grid=(T // R, nc),
        in_specs=[main_v, main_g, halo_v, halo_g, pl.BlockSpec((KA_ROWS, CT), lambda i, j: (0, j)), vec, vec, vec]
        + [ANY] * len(deps),
        out_specs=(out, out),
        scratch_shapes=[pltpu.VMEM((R + H, CT), F32), pltpu.VMEM((7, R + H, CT), F32)],
        compiler_params=_params("parallel", "parallel"),
    )(proj, proj, proj, proj, wa, ba, lg, lb, *deps)


def _mix_a_bwd(proj, u1, dmix, wa, lg, lb, *, cw, name):
    T = proj.shape[0]
    R = _pick(T, 128, HALO_A)
    CT = _pick(cw, 256)
    nc = cw // CT
    H = HALO_A
    n_t = T // R
    NG = CT // GROUP

    def body(av_ref, ag_ref, avh_ref, agh_ref, u1_ref, u1n_ref, d3_ref, d3n_ref, w_ref, lg_ref, lb_ref,
             dav_ref, dag_ref, dw_ref, db_ref, dlg_ref, dlb_ref, ext, d1, ext_rolled, d1_rolled):
        i = pl.program_id(1)
        last = i == n_t - 1

        def ln_bwd(u1, d3, cols):
            xc, rstd = _group_stats(u1)
            xh = xc * rstd
            g = lg_ref[:, cols]
            u2 = xh * g + lb_ref[:, cols]
            sg = _sigmoid(u2)
            du2 = d3 * (sg * (1.0 + u2 * (1.0 - sg)))
            dxh = du2 * g
            du1 = rstd * (dxh - jnp.mean(dxh, axis=-1, keepdims=True) - xh * jnp.mean(dxh * xh, axis=-1, keepdims=True))
            return du1, du2 * xh, du2

        def lanes(parts):
            return jnp.concatenate(parts, axis=1) if len(parts) > 1 else parts[0]

        dlg_parts, dlb_parts, db_parts = [], [], []
        dw_parts = [[] for _ in range(KA)]
        for gi in range(NG):
            cols = pl.ds(gi * GROUP, GROUP)
            du1, dlg, dlb = ln_bwd(u1_ref[:, cols], d3_ref[:, cols], cols)
            d1[pl.ds(0, R), cols] = du1
            dlg_parts.append(jnp.sum(dlg, axis=0, keepdims=True))
            dlb_parts.append(jnp.sum(dlb, axis=0, keepdims=True))
            db_parts.append(jnp.sum(du1, axis=0, keepdims=True))
            du1n, _, _ = ln_bwd(u1n_ref[:, cols], jnp.where(last, 0.0, d3n_ref[:, cols]), cols)
            d1[pl.ds(R, H), cols] = jnp.where(last, 0.0, du1n)

            _shifted_rows(d1_rolled, d1[:, cols], cols)
            du0 = _window(d1, d1_rolled, KA - 1, R, cols) * w_ref[pl.ds(0, 1), cols]
            for k in range(1, KA):
                du0 = du0 + _window(d1, d1_rolled, KA - 1 - k, R, cols) * w_ref[pl.ds(k, 1), cols]
            av = av_ref[:, cols]
            sg = _sigmoid(ag_ref[:, cols])
            dav_ref[:, cols] = (du0 * sg).astype(BF16)
            dag_ref[:, cols] = (du0 * av * sg * (1.0 - sg)).astype(BF16)

            ext[pl.ds(0, H), cols] = jnp.where(i > 0, avh_ref[:, cols] * _sigmoid(agh_ref[:, cols]), 0.0)
            ext[pl.ds(H, R), cols] = av * sg
            _shifted_rows(ext_rolled, ext[:, cols], cols)
            d1_main = d1[pl.ds(0, R), cols]
            for k in range(KA):
                window = _window(ext, ext_rolled, H - (KA - 1) + k, R, cols)
                dw_parts[k].append(jnp.sum(d1_main * window, axis=0, keepdims=True))

        dlg_row, dlb_row, db_row = lanes(dlg_parts), lanes(dlb_parts), lanes(db_parts)

        @pl.when(i == 0)
        def _():
            dlg_ref[...] = dlg_row
            dlb_ref[...] = dlb_row
            db_ref[...] = db_row

        @pl.when(i > 0)
        def _():
            dlg_ref[...] += dlg_row
            dlb_ref[...] += dlb_row
            db_ref[...] += db_row

        rows = [lanes(p) for p in dw_parts]
        rows.append(jnp.zeros((KA_ROWS - KA, CT), F32))
        _acc_rows_block(dw_ref, i, rows)

    cv = lambda j: j
    cg = lambda j: j + nc
    main_v = pl.BlockSpec((R, CT), lambda j, i: (i, j))
    main_g = pl.BlockSpec((R, CT), lambda j, i: (i, j + nc))
    prev_v = pl.BlockSpec((H, CT), _swap(_prev_halo(R, H, CT, cv)))
    prev_g = pl.BlockSpec((H, CT), _swap(_prev_halo(R, H, CT, cg)))
    nxt = pl.BlockSpec((H, CT), _swap(_next_halo(R, H, T, cv)))
    wspec = pl.BlockSpec((KA_ROWS, CT), lambda j, i: (0, j))
    vec = pl.BlockSpec((1, CT), lambda j, i: (0, j))
    vshape = jax.ShapeDtypeStruct((1, cw), F32)
    return pl.pallas_call(
        body,
        name=name,
        out_shape=(jax.ShapeDtypeStruct((T, cw), BF16), jax.ShapeDtypeStruct((T, cw), BF16),
                   jax.ShapeDtypeStruct((KA_ROWS, cw), F32), vshape, vshape, vshape),
        grid=(nc, n_t),
        in_specs=[main_v, main_g, prev_v, prev_g, main_v, nxt, main_v, nxt, wspec, vec, vec],
        out_specs=(main_v, main_v, wspec, vec, vec, vec),
        scratch_shapes=[pltpu.VMEM((R + H, CT), F32), pltpu.VMEM((R + H, CT), F32),
                        pltpu.VMEM((7, R + H, CT), F32), pltpu.VMEM((7, R + H, CT), F32)],
        compiler_params=_params("parallel", "arbitrary"),
    )(proj, proj, proj, proj, u1, u1, dmix, dmix, wa, lg, lb)


def _mix_b_fwd(proj, wb, *, cw, sw, name):
    T = proj.shape[0]
    R = _pick(T, 256, 8)
    CT = _pick(sw, 512)
    nb, nc, nh = (2 * cw) // CT, (2 * cw + sw) // CT, (2 * cw + 2 * sw) // CT
    H = HALO_S

    def body(b_ref, c_ref, h_ref, ch_ref, hh_ref, w_ref, v_ref, ext):
        i = pl.program_id(0)
        ext[pl.ds(0, H), :] = jnp.where(i > 0, ch_ref[...] * hh_ref[...], 0.0)
        ext[pl.ds(H, R), :] = c_ref[...] * h_ref[...]
        w = w_ref[...]
        zc = ext[pl.ds(H - 2, R), :] * w[0:1, :]
        zc = zc + ext[pl.ds(H - 1, R), :] * w[1:2, :]
        zc = zc + ext[pl.ds(H, R), :] * w[2:3, :]
        v_ref[...] = (b_ref[...] * zc).astype(BF16)

    def main(off):
        return pl.BlockSpec((R, CT), lambda i, j: (i, j + off))

    def prev(off):
        return pl.BlockSpec((H, CT), _prev_halo(R, H, CT, lambda j: j + off))

    return pl.pallas_call(
        body,
        name=name,
        out_shape=jax.ShapeDtypeStruct((T, sw), BF16),
        grid=(T // R, sw // CT),
        in_specs=[main(nb), main(nc), main(nh), prev(nc), prev(nh), pl.BlockSpec((KS_ROWS, CT), lambda i, j: (0, j))],
        out_specs=pl.BlockSpec((R, CT), lambda i, j: (i, j)),
        scratch_shapes=[pltpu.VMEM((R + H, CT), F32)],
        compiler_params=_params("parallel", "parallel"),
    )(proj, proj, proj, proj, proj, wb)


def _mix_b_bwd(proj, dmix, wb, *, cw, sw, name):
    T = proj.shape[0]
    R = _pick(T, 256, 8)
    CT = _pick(sw, 512)
    nb, nc, nh = (2 * cw) // CT, (2 * cw + sw) // CT, (2 * cw + 2 * sw) // CT
    nd = cw // CT
    H = HALO_S
    n_t = T // R

    def body(b_ref, bn_ref, c_ref, cp_ref, cn_ref, h_ref, hp_ref, hn_ref, d_ref, dn_ref, w_ref,
             db_ref, dc_ref, dhh_ref, dw_ref, ext, dze):
        i = pl.program_id(1)
        last = i == n_t - 1
        cc, hh = c_ref[...], h_ref[...]
        ext[pl.ds(0, H), :] = jnp.where(i > 0, cp_ref[...] * hp_ref[...], 0.0)
        ext[pl.ds(H, R), :] = cc * hh
        ext[pl.ds(H + R, H), :] = cn_ref[...] * hn_ref[...]
        w = w_ref[...]
        w0, w1, w2 = w[0:1, :], w[1:2, :], w[2:3, :]

        def conv(start, n):
            z = ext[pl.ds(start + H - 2, n), :] * w0
            z = z + ext[pl.ds(start + H - 1, n), :] * w1
            return z + ext[pl.ds(start + H, n), :] * w2

        d_main = d_ref[...]
        db_ref[...] = (d_main * conv(0, R)).astype(BF16)
        dzc_main = d_main * b_ref[...]
        dze[pl.ds(0, R), :] = dzc_main
        dze[pl.ds(R, H), :] = jnp.where(last, 0.0, dn_ref[...] * bn_ref[...])
        dz = dze[pl.ds(0, R), :] * w2 + dze[pl.ds(1, R), :] * w1 + dze[pl.ds(2, R), :] * w0
        dc_ref[...] = (dz * hh).astype(BF16)
        dhh_ref[...] = (dz * cc).astype(BF16)
        rows = [jnp.sum(dzc_main * ext[pl.ds(H - 2 + k, R), :], axis=0, keepdims=True) for k in range(KS)]
        rows.append(jnp.zeros((KS_ROWS - KS, CT), F32))
        _acc_rows_block(dw_ref, i, rows)

    def main(off):
        return pl.BlockSpec((R, CT), lambda j, i: (i, j + off))

    def prev(off):
        return pl.BlockSpec((H, CT), _swap(_prev_halo(R, H, CT, lambda j: j + off)))

    def nxt(off):
        return pl.BlockSpec((H, CT), _swap(_next_halo(R, H, T, lambda j: j + off)))

    out = pl.BlockSpec((R, CT), lambda j, i: (i, j))
    wspec = pl.BlockSpec((KS_ROWS, CT), lambda j, i: (0, j))
    act = jax.ShapeDtypeStruct((T, sw), BF16)
    return pl.pallas_call(
        body,
        name=name,
        out_shape=(act, act, act, jax.ShapeDtypeStruct((KS_ROWS, sw), F32)),
        grid=(sw // CT, n_t),
        in_specs=[main(nb), nxt(nb), main(nc), prev(nc), nxt(nc), main(nh), prev(nh), nxt(nh), main(nd), nxt(nd), wspec],
        out_specs=(out, out, out, wspec),
        scratch_shapes=[pltpu.VMEM((R + 2 * H, CT), F32), pltpu.VMEM((R + H, CT), F32)],
        compiler_params=_params("parallel", "arbitrary"),
    )(proj, proj, proj, proj, proj, proj, proj, proj, dmix, dmix, wb)


def _adamw(w, g, m, v, *, name, emit_grad=False):
    Rr, Cc = w.shape
    R = _pick(Rr, max(8, ADAMW_BLOCK_BYTES // (4 * Cc)), 8)

    def body(w_ref, g_ref, m_ref, v_ref, d_ref, mo_ref, vo_ref, *g_out):
        g = g_ref[...]
        m2 = ADAM_B1 * m_ref[...] + (1.0 - ADAM_B1) * g
        v2 = ADAM_B2 * v_ref[...] + (1.0 - ADAM_B2) * (g * g)
        m_hat = m2 / (1.0 - ADAM_B1 ** ADAM_STEP)
        v_hat = v2 / (1.0 - ADAM_B2 ** ADAM_STEP)
        d_ref[...] = -ADAM_LR * (m_hat / (jnp.sqrt(v_hat) + ADAM_EPS) + ADAM_WD * w_ref[...])
        mo_ref[...] = m2
        vo_ref[...] = v2
        if emit_grad:
            g_out[0][...] = g

    spec = pl.BlockSpec((R, Cc), lambda i: (i, 0))
    shp = jax.ShapeDtypeStruct((Rr, Cc), F32)
    n_out = 4 if emit_grad else 3
    return pl.pallas_call(
        body, name=name, out_shape=(shp,) * n_out, grid=(Rr // R,), in_specs=[spec] * 4, out_specs=(spec,) * n_out,
        compiler_params=_params("parallel"),
    )(w, g, m, v)


def _sum_chips(cs, rb, pc, lay, *, name):
    _, Rr, Cc = rb.shape
    tr, tc = _pick(Rr, 512, 16), _pick(Cc, 2048)
    nr, ncol = Rr // tr, Cc // tc
    if lay.axis == 0:
        own_map = lambda i, j, s: (i + s[0] * nr, j)
        out_map = lambda i, j, s: (i, j + s[1] * ncol)
    else:
        own_map = lambda i, j, s: (i, j + s[0] * ncol)
        out_map = lambda i, j, s: (i + s[1] * nr, j)

    def body(s_ref, own_ref, rb_ref, o_ref):
        acc = own_ref[...].astype(F32)
        for j in range(N_CHIPS - 1):
            acc = acc + rb_ref[j].astype(F32)
        o_ref[...] = acc

    return pl.pallas_call(
        body,
        name=name,
        out_shape=jax.ShapeDtypeStruct(lay.shard_shape(), F32),
        grid_spec=pltpu.PrefetchScalarGridSpec(
            num_scalar_prefetch=1,
            grid=(nr, ncol),
            in_specs=[pl.BlockSpec((tr, tc), own_map), pl.BlockSpec((N_CHIPS - 1, tr, tc), lambda i, j, s: (0, i, j))],
            out_specs=pl.BlockSpec((tr, tc), out_map)),
        compiler_params=_params("parallel", "parallel"),
    )(pc, cs, rb)


def _place():
    x, y, c = lax.axis_index("x"), lax.axis_index("y"), lax.axis_index("c")
    return x, y, c, 2 * x + y


def _other_chips(x, y):
    return [(1 - x, y, 2 * (1 - x) + y), (x, 1 - y, 2 * x + (1 - y)), (1 - x, 1 - y, 2 * (1 - x) + (1 - y))]


def _allgather_small(buf, *, name, reduce):
    S = buf.shape[0]

    def body(x_ref, o_ref, gat, send_sems, recv_sems):
        x, y, c, _ = _place()
        me = 4 * x + 2 * y + c
        gat[me] = x_ref[...]
        copies = []
        for k in range(1, N_DEV):
            fx, fy, fc = (k >> 2) & 1, (k >> 1) & 1, k & 1
            px = 1 - x if fx else x
            py = 1 - y if fy else y
            pc = 1 - c if fc else c
            peer = 4 * px + 2 * py + pc
            send = pltpu.make_async_remote_copy(
                src_ref=x_ref, dst_ref=gat.at[me], send_sem=send_sems.at[k - 1], recv_sem=recv_sems.at[k - 1],
                device_id=(px, py, pc), device_id_type=MESH)
            send.start()
            arrival = pltpu.make_async_remote_copy(
                src_ref=x_ref, dst_ref=gat.at[peer], send_sem=send_sems.at[k - 1], recv_sem=recv_sems.at[k - 1],
                device_id=(px, py, pc), device_id_type=MESH)
            copies.append((send, arrival))
        for send, arrival in copies:
            arrival.wait_recv()
        for send, arrival in copies:
            send.wait_send()
        if reduce:
            acc = gat[0]
            for d in range(1, N_DEV):
                acc = acc + gat[d]
            o_ref[...] = acc
        else:
            o_ref[...] = gat[...]

    out_shape = jax.ShapeDtypeStruct((S, LANES) if reduce else (N_DEV, S, LANES), F32)
    return pl.pallas_call(
        body,
        name=name,
        out_shape=out_shape,
        in_specs=[pl.BlockSpec(memory_space=pltpu.VMEM)],
        out_specs=pl.BlockSpec(memory_space=pltpu.VMEM),
        scratch_shapes=[pltpu.VMEM((N_DEV, S, LANES), F32), pltpu.SemaphoreType.DMA((N_DEV - 1,)),
                        pltpu.SemaphoreType.DMA((N_DEV - 1,))],
        compiler_params=pltpu.CompilerParams(vmem_limit_bytes=VMEM_LIMIT),
    )(buf)


class _Sharded:
    def __init__(self, shape, axis):
        self.shape = shape
        self.axis = axis
        self.block = shape[axis] // N_CHIPS
        self.half = shape[1 - axis] // 2

    def _sl(self, along, across):
        return (along, across) if self.axis == 0 else (across, along)

    def block_slice(self, q):
        return self._sl(pl.ds(q * self.block, self.block), pl.ds(0, self.shape[1 - self.axis]))

    def block_half_slice(self, q, c):
        return self._sl(pl.ds(q * self.block, self.block), pl.ds(c * self.half, self.half))

    def half_slice(self, c):
        return self._sl(pl.ds(0, self.shape[self.axis]), pl.ds(c * self.half, self.half))

    def shard_half_slice(self, c):
        return self._sl(pl.ds(0, self.block), pl.ds(c * self.half, self.half))

    def shard_shape(self):
        return self._sl(self.block, self.shape[1 - self.axis])

    def half_shape(self):
        return self._sl(self.shape[self.axis], self.half)

    def block_half_shape(self):
        return self._sl(self.block, self.half)

    def block_in_half_slice(self, q):
        return self._sl(pl.ds(q * self.block, self.block), pl.ds(0, self.half))


def _at(ref, sl):
    return ref.at[sl[0], sl[1]]


class _Copy:
    def __init__(self, src, dst, arrive, dev):
        self.src, self.dst, self.arrive, self.dev = src, dst, arrive, dev


HBM = pl.BlockSpec(memory_space=pltpu.HBM)
SEM = pl.BlockSpec(memory_space=pltpu.SEMAPHORE)
EFFECT = pltpu.SideEffectType.DATAFLOW_SIDE_EFFECTING


def _exchange_start(srcs, land_shapes, plan, after, *, name):
    ns, nl = len(srcs), len(land_shapes)
    n_copies = len(plan([None] * ns, [None] * nl, dry=True))

    def body(*refs):
        src_refs, land_refs = refs[:ns], refs[ns:ns + nl]
        send_sems, recv_sems = refs[ns + nl + 1], refs[ns + nl + 2]
        token = refs[-1]
        for k, cp in enumerate(plan(src_refs, land_refs)):
            pltpu.make_async_remote_copy(
                src_ref=cp.src, dst_ref=cp.dst, send_sem=send_sems.at[k], recv_sem=recv_sems.at[k],
                device_id=cp.dev, device_id_type=MESH).start()
        token[...] = jnp.zeros_like(token)

    sem = pltpu.SemaphoreType.DMA((n_copies,))
    lands = [pltpu.with_memory_space_constraint(lax.empty(shp, dt), pltpu.HBM) for shp, dt in land_shapes]
    srcs = [pltpu.with_memory_space_constraint(a, pltpu.HBM) for a in srcs]
    thru = [pltpu.HBM(a.shape, a.dtype) for a in srcs + lands]
    outs = pl.pallas_call(
        body,
        name=name,
        out_shape=(sem, sem, *thru, jax.ShapeDtypeStruct((8, LANES), F32)),
        in_specs=[HBM] * (ns + nl) + [ANY],
        out_specs=(SEM, SEM, *[HBM] * (ns + nl), pl.BlockSpec(memory_space=pltpu.VMEM)),
        input_output_aliases={i: 2 + i for i in range(ns + nl)},
        compiler_params=pltpu.CompilerParams(has_side_effects=EFFECT),
    )(*srcs, *lands, after)
    return outs[0], outs[1], list(outs[2:2 + ns]), list(outs[2 + ns:2 + ns + nl]), outs[-1]


def _exchange_wait(send_sems, recv_sems, srcs, lands, plan, after, *, name):
    ns, nl = len(srcs), len(lands)
    after = tuple(after) if isinstance(after, (tuple, list)) else (after,)

    def body(*refs):
        src_refs, land_refs = refs[:ns], refs[ns:ns + nl]
        send_sems, recv_sems = refs[ns + nl], refs[ns + nl + 1]
        copies = [
            pltpu.make_async_remote_copy(
                src_ref=cp.src, dst_ref=cp.arrive, send_sem=send_sems.at[k], recv_sem=recv_sems.at[k],
                device_id=cp.dev, device_id_type=MESH)
            for k, cp in enumerate(plan(src_refs, land_refs))
        ]
        for cp in copies:
            cp.wait_recv()
        for cp in copies:
            cp.wait_send()

    thru = [pltpu.HBM(a.shape, a.dtype) for a in list(srcs) + list(lands)]
    outs = pl.pallas_call(
        body,
        name=name,
        out_shape=tuple(thru),
        in_specs=[HBM] * (ns + nl) + [SEM, SEM] + [ANY] * len(after),
        out_specs=tuple([HBM] * (ns + nl)),
        input_output_aliases={i: i for i in range(ns + nl)},
        compiler_params=pltpu.CompilerParams(has_side_effects=EFFECT),
    )(*srcs, *lands, send_sems, recv_sems, *after)
    return list(outs[:ns]), list(outs[ns:])


def _gather_plan(layouts):
    def plan(srcs, lands, dry=False):
        if dry:
            return [None] * (4 * len(layouts))
        x, y, c, p = _place()
        copies = []
        for s, g, lay in zip(srcs, lands, layouts):
            own = _at(g, lay.block_slice(p))
            copies.append(_Copy(s, own, own, (x, y, 1 - c)))
            for qx, qy, q in _other_chips(x, y):
                copies.append(_Copy(_at(s, lay.shard_half_slice(c)), _at(g, lay.block_half_slice(p, c)),
                                    _at(g, lay.block_half_slice(q, c)), (qx, qy, c)))
        return copies

    return plan


def _forward_plan(layouts):
    def plan(srcs, lands, dry=False):
        if dry:
            return [None] * (3 * len(layouts))
        x, y, c, _ = _place()
        copies = []
        for g, lay in zip(srcs, layouts):
            for qx, qy, q in _other_chips(x, y):
                landed = _at(g, lay.block_half_slice(q, c))
                copies.append(_Copy(landed, landed, _at(g, lay.block_half_slice(q, 1 - c)), (x, y, 1 - c)))
        return copies

    return plan


def _sibling_plan(n):
    def plan(srcs, lands, dry=False):
        if dry:
            return [None] * n
        x, y, c, _ = _place()
        return [_Copy(s, ra, ra, (x, y, 1 - c)) for s, ra in zip(srcs, lands)]

    return plan


def _chips_plan(layouts):
    def plan(srcs, lands, dry=False):
        if dry:
            return [None] * (3 * len(layouts))
        x, y, c, _ = _place()
        copies = []
        for s, rb, lay in zip(srcs, lands, layouts):
            for j, (qx, qy, q) in enumerate(_other_chips(x, y)):
                copies.append(_Copy(_at(s, lay.block_in_half_slice(q)), rb.at[j], rb.at[j], (qx, qy, c)))
        return copies

    return plan


def _join_plan(layouts):
    def plan(srcs, lands, dry=False):
        if dry:
            return [None] * len(layouts)
        x, y, c, _ = _place()
        copies = []
        for g, lay in zip(srcs, layouts):
            mine = _at(g, lay.shard_half_slice(c))
            copies.append(_Copy(mine, mine, _at(g, lay.shard_half_slice(1 - c)), (x, y, 1 - c)))
        return copies

    return plan


def _pack(arrays):
    flat = [a.reshape(-1) for a in arrays]
    sizes = [f.shape[0] for f in flat]
    total = sum(sizes)
    rows = _round_up(-(-total // LANES), 8)
    flat.append(jnp.zeros((rows * LANES - total,), F32))
    return jnp.concatenate(flat).reshape(rows, LANES)


def _unpack(buf, shapes):
    flat = buf.reshape(-1)
    out, pos = [], 0
    for shp in shapes:
        n = 1
        for d in shp:
            n *= d
        out.append(flat[pos:pos + n].reshape(shp))
        pos += n
    return out


def _pad_to(a, rows, cols):
    return jnp.pad(a, ((0, rows - a.shape[0]), (0, cols - a.shape[1])))


def kernel(x, mem, g_mix, w_in, conv_a_w, conv_a_b, ln_a_g, ln_a_b, conv_b_w, w_out, g_xattn, g_mem, w_q, w_k, w_v, w_o, g_ffn, w_gate, w_up, conv_f_w, w_down, g_final, loss_target, m_g_mix, m_w_in, m_conv_a_w, m_conv_a_b, m_ln_a_g, m_ln_a_b, m_conv_b_w, m_w_out, m_g_xattn, m_g_mem, m_w_q, m_w_k, m_w_v, m_w_o, m_g_ffn, m_w_gate, m_w_up, m_conv_f_w, m_w_down, m_g_final, v_g_mix, v_w_in, v_conv_a_w, v_conv_a_b, v_ln_a_g, v_ln_a_b, v_conv_b_w, v_w_out, v_g_xattn, v_g_mem, v_w_q, v_w_k, v_w_v, v_w_o, v_g_ffn, v_w_gate, v_w_up, v_conv_f_w, v_w_down, v_g_final):
    T, D = x.shape[1], x.shape[2]
    in_sh = w_in.shape[2]
    cw_sh = conv_a_w.shape[2]
    cw = N_CHIPS * cw_sh
    f_sh = w_gate.shape[2]
    fp = _round_up(f_sh, 256)
    F = N_CHIPS * fp
    rs = w_out.shape[1]
    c_idx = lax.axis_index("c")
    p_idx = 2 * lax.axis_index("x") + lax.axis_index("y")

    def t_(a):
        return jnp.swapaxes(a[0], 0, 1)

    big = {
        "w_in": (lambda: w_in[0].astype(BF16), _Sharded((D, N_CHIPS * in_sh), 1)),
        "w_out": (lambda: w_out[0].astype(BF16), _Sharded((N_CHIPS * rs, D), 0)),
        "w_q": (lambda: w_q[0].astype(BF16), _Sharded((D, D), 0)),
        "w_k": (lambda: w_k[0].astype(BF16), _Sharded((D, D), 0)),
        "w_v": (lambda: w_v[0].astype(BF16), _Sharded((D, D), 0)),
        "w_o": (lambda: w_o[0].astype(BF16), _Sharded((D, D), 0)),
        "w_gate": (lambda: _pad_to(t_(w_gate).astype(BF16), fp, D), _Sharded((F, D), 0)),
        "w_up": (lambda: _pad_to(t_(w_up).astype(BF16), fp, D), _Sharded((F, D), 0)),
        "w_down": (lambda: _pad_to(w_down[0].astype(BF16), fp, D), _Sharded((F, D), 0)),
    }
    names = list(big)
    lay = {k: big[k][1] for k in names}
    c_arr = c_idx.astype(jnp.int32).reshape(1)
    pc_arr = jnp.stack([p_idx, c_idx]).astype(jnp.int32)

    conv_shapes = [(KA_ROWS, cw_sh), (KS_ROWS, cw_sh), (KS_ROWS, fp)]
    conv_pack = _pack([_pad_to(conv_a_w[0], KA_ROWS, cw_sh), _pad_to(conv_b_w[0], KS_ROWS, cw_sh),
                       _pad_to(conv_f_w[0], KS_ROWS, fp)])
    conv_all = _allgather_small(conv_pack, name="allgather_conv", reduce=False)
    per_chip = [_unpack(conv_all[2 * q], conv_shapes) for q in range(N_CHIPS)]
    wa = jnp.concatenate([pc[0] for pc in per_chip], axis=1)
    wb = jnp.concatenate([pc[1] for pc in per_chip], axis=1)
    wf = jnp.concatenate([pc[2] for pc in per_chip], axis=1)

    tok = conv_all

    gather_groups = [["w_in"], ["w_out"], ["w_q", "w_k", "w_v"], ["w_o"], ["w_gate"], ["w_up"], ["w_down"]]
    gathers = []
    for gi, grp in enumerate(gather_groups):
        lays = [lay[k] for k in grp]
        plan = _gather_plan(lays)
        ssem, rsem, srcs, lands, tok = _exchange_start(
            [big[k][0]() for k in grp], [(l.shape, BF16) for l in lays], plan, tok, name=f"gather_start_{gi}")
        gathers.append((ssem, rsem, srcs, lands, plan, lays, grp))
    W = {}
    relays = {}

    def relay(gi, after):
        ssem, rsem, srcs, lands, plan, lays, grp = gathers[gi]
        _, lands = _exchange_wait(ssem, rsem, srcs, lands, plan, after, name=f"gather_wait_{gi}")
        plan = _forward_plan(lays)
        ssem, rsem, lands, _, token = _exchange_start(lands, [], plan, after, name=f"gather_forward_start_{gi}")
        relays[gi] = (ssem, rsem, lands, plan, grp)
        return token

    def gathered(gi, after):
        if gi not in relays:
            relay(gi, after)
        ssem, rsem, lands, plan, grp = relays[gi]
        lands, _ = _exchange_wait(ssem, rsem, lands, [], plan, after, name=f"gather_forward_wait_{gi}")
        W.update(zip(grp, lands))

    relay(0, tok)

    xs, tgt, mems = x[0], loss_target[0], mem[0]
    g_mem2, g_final2 = g_mem[None, :], g_final[None, :]

    memn, rm = _norm_fwd(mems, g_mem2, name="norm_mem")
    xn1, r1 = _norm_fwd(xs, g_mix, name="norm_mix")
    gathered(0, xn1)
    proj = _mm(xn1, W["w_in"], name="mm_in")
    ftok = relay(1, proj)
    u3, u1 = _mix_a_fwd(proj, wa, conv_a_b, ln_a_g, ln_a_b, cw=cw, deps=[ftok], name="mix_a_fwd")
    vb = _mix_b_fwd(proj, wb, cw=cw, sw=cw, name="mix_b_fwd")
    mix = jnp.concatenate([u3, vb], axis=1)
    gathered(1, mix)
    ftok = relay(2, mix)
    h1 = _mm(mix, W["w_out"], add=xs, deps=[ftok], name="mm_out")
    xn2, r2 = _norm_fwd(h1, g_xattn, name="norm_xattn")
    gathered(2, xn2)
    ftok = relay(3, xn2)
    q = _mm(xn2, W["w_q"], out_dtype=BF16, deps=[ftok], name="mm_q")
    k = _mm(memn, W["w_k"], out_dtype=BF16, name="mm_k")
    vm = _mm(memn, W["w_v"], out_dtype=BF16, name="mm_v")
    o = _attn_fwd(q, k, vm, name="attn_fwd")
    gathered(3, o)
    h2 = _mm(o, W["w_o"], add=h1, name="mm_o")
    relay(4, h2)
    xn3, r3 = _norm_fwd(h2, g_ffn, name="norm_ffn")
    gathered(4, xn3)
    gp = _mm(xn3, W["w_gate"], tb=True, name="mm_gate")
    gathered(5, gp)
    up = _mm(xn3, W["w_up"], tb=True, name="mm_up")
    f = _ffn_act_fwd(gp, up, wf, name="ffn_act_fwd")
    gathered(6, f)
    h3 = _mm(f, W["w_down"], add=h2, name="mm_down")

    G = {}
    other_arr = 1 - c_arr

    def grad_half(k, which, **kw):
        xk, dyk = G[k]
        return _mm(xk, dyk, ta=True, half=("n" if lay[k].axis == 0 else "m", which), **kw)

    def siblings_start(tag, grp, after):
        lays = [lay[k] for k in grp]
        firsts = [grad_half(k, other_arr, deps=[after], name=f"mm_dw_{k}_sibling") for k in grp]
        plan = _sibling_plan(len(grp))
        ssem, rsem, srcs, lands, token = _exchange_start(
            firsts, [(l.half_shape(), F32) for l in lays], plan, after, name=f"rs_siblings_start_{tag}")
        return (tag, grp, lays, plan, ssem, rsem, srcs, lands), token

    def chips_start(state, after):
        tag, grp, lays, plan, ssem, rsem, srcs, lands = state
        _, lands = _exchange_wait(ssem, rsem, srcs, lands, plan, after, name=f"rs_siblings_wait_{tag}")
        sums = [grad_half(k, c_arr, add=ra, out_dtype=BF16, name=f"mm_dw_{k}_own") for k, ra in zip(grp, lands)]
        plan = _chips_plan(lays)
        ssem, rsem, sums, lands, token = _exchange_start(
            sums, [((N_CHIPS - 1,) + l.block_half_shape(), BF16) for l in lays], plan, after, name=f"rs_chips_start_{tag}")
        return (tag, grp, lays, plan, ssem, rsem, sums, lands), token

    def join_start(state, after):
        tag, grp, lays, plan, ssem, rsem, sums, lands = state
        sums, lands = _exchange_wait(ssem, rsem, sums, lands, plan, after, name=f"rs_chips_wait_{tag}")
        halves = [_sum_chips(cs, rb, pc_arr, l, name=f"sum_chips_{k}") for k, cs, rb, l in zip(grp, sums, lands, lays)]
        plan = _join_plan(lays)
        ssem, rsem, halves, _, token = _exchange_start(halves, [], plan, after[0], name=f"rs_join_start_{tag}")
        return (tag, grp, plan, ssem, rsem, halves), token

    def join_finish(state, after):
        tag, grp, plan, ssem, rsem, halves = state
        halves, _ = _exchange_wait(ssem, rsem, halves, [], plan, after, name=f"rs_join_wait_{tag}")
        return dict(zip(grp, halves))

    loss_rows, dh3, dh3b, dg_final = _loss_head(h3, g_final2, tgt, name="loss_head")
    df = _mm(dh3b, W["w_down"], tb=True, out_dtype=BF16, name="mm_d_f")
    G["w_down"] = (f, dh3b)
    rs_down, tok = siblings_start("down", ["w_down"], tok)
    dgp, dup, dwf = _ffn_act_bwd(gp, up, df, wf, name="ffn_act_bwd")
    G["w_gate"], G["w_up"] = (dgp, xn3), (dup, xn3)
    rs_ffn, tok = siblings_start("ffn", ["w_gate", "w_up"], tok)
    rs_down, tok = chips_start(rs_down, tok)
    dxn3 = _mm(dgp, W["w_gate"], deps=[tok], name="mm_dxn3_gate")
    dxn3 = _mm(dup, W["w_up"], add=dxn3, name="mm_dxn3_up")
    dh2, dh2b, dg_ffn = _norm_bwd(h2, r3, g_ffn, dxn3, dh3, name="norm_ffn_bwd")
    rs_ffn, tok = chips_start(rs_ffn, dh2b)
    do = _mm(dh2b, W["w_o"], tb=True, out_dtype=BF16, deps=[tok], name="mm_d_o")
    G["w_o"] = (o, dh2b)
    dq, dk, dvm = _attn_bwd(q, k, vm, do, name="attn_bwd")
    dkb, dvb = dk.astype(BF16), dvm.astype(BF16)
    G["w_q"], G["w_k"], G["w_v"] = (xn2, dq), (memn, dkb), (memn, dvb)
    rs_att, tok = siblings_start("att", ["w_o", "w_q", "w_k", "w_v"], tok)
    dxn2 = _mm(dq, W["w_q"], tb=True, deps=[tok], name="mm_dxn2")
    dmemn = _mm(dkb, W["w_k"], tb=True, name="mm_dmem_k")
    dmemn = _mm(dvb, W["w_v"], tb=True, add=dmemn, name="mm_dmem_v")
    dg_mem = _norm_bwd(mems, rm, g_mem2, dmemn, None, name="norm_mem_bwd", want_dh=False)
    dh1, dh1b, dg_xattn = _norm_bwd(h1, r2, g_xattn, dxn2, dh2, name="norm_xattn_bwd")
    rs_att, tok = chips_start(rs_att, dh1b)
    dmix = _mm(dh1b, W["w_out"], tb=True, deps=[tok], name="mm_d_mix")
    G["w_out"] = (mix, dh1b)
    rs_out, tok = siblings_start("out", ["w_out"], tok)
    dav, dag, dwa, dba, dlg, dlb = _mix_a_bwd(proj, u1, dmix, wa, ln_a_g, ln_a_b, cw=cw, name="mix_a_bwd")
    dbg, dcg, dbh, dwb = _mix_b_bwd(proj, dmix, wb, cw=cw, sw=cw, name="mix_b_bwd")
    dproj = jnp.concatenate([dav, dag, dbg, dcg, dbh], axis=1)
    rs_out, tok = chips_start(rs_out, dproj)
    G["w_in"] = (xn1, dproj)
    rs_in, tok = siblings_start("in", ["w_in"], tok)
    dxn1 = _mm(dproj, W["w_in"], tb=True, deps=[tok], name="mm_dxn1")
    dx, _, dg_mix = _norm_bwd(xs, r1, g_mix, dxn1, dh1, name="norm_mix_bwd")
    rs_in, tok = chips_start(rs_in, dx)

    loss_part = jnp.sum(loss_rows).reshape(1, 1)
    small_parts = [dg_mix, dba, dlg, dlb, dg_xattn, dg_mem, dg_ffn, dg_final, dwa, dwb, dwf, loss_part]
    small_shapes = [a.shape for a in small_parts]
    reduced = _allgather_small(_pack(small_parts), name="allreduce_small", reduce=True)
    (sg_mix, sba, slg, slb, sg_xattn, sg_mem, sg_ffn, sg_final, swa, swb, swf, loss_sum) = _unpack(reduced, small_shapes)
    loss = loss_sum.reshape(())
    ga_w = lax.dynamic_slice(swa, (0, p_idx * cw_sh), (KA, cw_sh))
    gb_w = lax.dynamic_slice(swb, (0, p_idx * cw_sh), (KS, cw_sh))
    gf_w = lax.dynamic_slice(swf, (0, p_idx * fp), (KS, f_sh))

    weights = dict(g_mix=g_mix, w_in=w_in, conv_a_w=conv_a_w, conv_a_b=conv_a_b, ln_a_g=ln_a_g, ln_a_b=ln_a_b,
                   conv_b_w=conv_b_w, w_out=w_out, g_xattn=g_xattn, g_mem=g_mem, w_q=w_q, w_k=w_k, w_v=w_v, w_o=w_o,
                   g_ffn=g_ffn, w_gate=w_gate, w_up=w_up, conv_f_w=conv_f_w, w_down=w_down, g_final=g_final)
    m_in = dict(g_mix=m_g_mix, w_in=m_w_in, conv_a_w=m_conv_a_w, conv_a_b=m_conv_a_b, ln_a_g=m_ln_a_g, ln_a_b=m_ln_a_b,
                conv_b_w=m_conv_b_w, w_out=m_w_out, g_xattn=m_g_xattn, g_mem=m_g_mem, w_q=m_w_q, w_k=m_w_k, w_v=m_w_v,
                w_o=m_w_o, g_ffn=m_g_ffn, w_gate=m_w_gate, w_up=m_w_up, conv_f_w=m_conv_f_w, w_down=m_w_down,
                g_final=m_g_final)
    v_in = dict(g_mix=v_g_mix, w_in=v_w_in, conv_a_w=v_conv_a_w, conv_a_b=v_conv_a_b, ln_a_g=v_ln_a_g, ln_a_b=v_ln_a_b,
                conv_b_w=v_conv_b_w, w_out=v_w_out, g_xattn=v_g_xattn, g_mem=v_g_mem, w_q=v_w_q, w_k=v_w_k, w_v=v_w_v,
                w_o=v_w_o, g_ffn=v_g_ffn, w_gate=v_w_gate, w_up=v_w_up, conv_f_w=v_conv_f_w, w_down=v_w_down,
                g_final=v_g_final)
    order = list(weights)
    grads = dict(g_mix=sg_mix, conv_a_w=ga_w, conv_a_b=sba, ln_a_g=slg, ln_a_b=slb, conv_b_w=gb_w, g_xattn=sg_xattn,
                 g_mem=sg_mem, g_ffn=sg_ffn, conv_f_w=gf_w, g_final=sg_final)
    grads = {k: g.reshape(weights[k].shape) for k, g in grads.items()}

    delta, new_m, new_v = {}, {}, {}
    small = [k for k in order if k not in big]
    small_shapes = [weights[k].shape for k in small]
    packed = [_pack([src[k] for k in small]) for src in (weights, grads, m_in, v_in)]
    d_, m_, v_ = _adamw(*packed, name="adamw_small")
    for k, dd, mm_, vv in zip(small, _unpack(d_, small_shapes), _unpack(m_, small_shapes), _unpack(v_, small_shapes)):
        delta[k], new_m[k], new_v[k] = dd, mm_, vv

    transposed = ("w_gate", "w_up")

    def update(shard_grads):
        last = None
        for k, g in shard_grads.items():
            view = t_ if k in transposed else (lambda a: a[0])
            back = (lambda a: jnp.swapaxes(a, 0, 1)[None]) if k in transposed else (lambda a: a[None])
            padded = g.shape != view(weights[k]).shape
            outs = _adamw(view(weights[k]), g, view(m_in[k]), view(v_in[k]), emit_grad=padded, name=f"adamw_{k}")
            delta[k], new_m[k], new_v[k] = back(outs[0]), back(outs[1]), back(outs[2])
            grads[k] = back(outs[3] if padded else g)
            last = outs[0]
        return last

    after = (d_, tok)
    joining = None
    for state in (rs_down, rs_ffn, rs_att, rs_out, rs_in):
        started, token = join_start(state, after)
        after = (token,)
        if joining is not None:
            after = (update(join_finish(joining, token)), token)
        joining = started
    update(join_finish(joining, after[0]))

    return (loss, dx[None], *[grads[k] for k in order], *[delta[k] for k in order],
            *[new_m[k] for k in order], *[new_v[k] for k in order])
```

```python
import functools

import jax
import jax.numpy as jnp
from jax import lax
from jax.experimental import pallas as pl
from jax.experimental.pallas import tpu as pltpu

F32 = jnp.float32
BF16 = jnp.bfloat16
EPS = 1e-6
N_HEADS = 4
GROUP = 128
KA = 31
KS = 3
KA_ROWS = 32
KS_ROWS = 8
HALO_A = 32
HALO_S = 8
CHUNK_ROWS = 16
CHUNK_COLS = 256
N_CHIPS = 4
N_DEV = 8
LANES = 128
VMEM_LIMIT = 56 * 1024 * 1024
MM_WIDE_TILE_ELEMS = 4096 * 512
ADAMW_BLOCK_BYTES = 1 << 20
MESH = pl.DeviceIdType.MESH
ANY = pl.BlockSpec(memory_space=pl.ANY)

ADAM_LR = 0.001
ADAM_B1 = 0.9
ADAM_B2 = 0.999
ADAM_EPS = 1e-08
ADAM_WD = 0.01
ADAM_STEP = 10


def _pick(dim, pref, mult=LANES):
    if dim <= pref:
        return dim
    t = (pref // mult) * mult
    while t >= mult:
        if dim % t == 0:
            return t
        t -= mult
    return dim


def _round_up(n, m):
    return ((n + m - 1) // m) * m


def _params(*sem):
    return pltpu.CompilerParams(dimension_semantics=sem, vmem_limit_bytes=VMEM_LIMIT)


def _sigmoid(x):
    return jax.nn.sigmoid(x)


def _mm(a, b, *, name, ta=False, tb=False, out_dtype=F32, add=None, deps=(), half=None, a_k=None, out_cols=None,
        into=None, tm=1024, tn=512, tk=5632):
    if ta:
        K, M = a.shape
    else:
        M, K = a.shape
    if tb:
        N, K2 = b.shape
    else:
        K2, N = b.shape
    k0 = 0
    if a_k is not None:
        k0, K = a_k
    assert K == K2, (a.shape, b.shape)
    half_dim, half_sel = half if half is not None else (None, None)
    if half_dim == "m":
        M //= 2
    elif half_dim == "n":
        N //= 2
    tm, tk = _pick(M, tm), _pick(K, tk)
    nk = K // tk
    tn = _pick(N, 2 * tn if nk == 1 and K * tn <= MM_WIDE_TILE_ELEMS else tn)
    mi, nj = M // tm, N // tn
    n0, n_total = out_cols if out_cols is not None else (0, N)
    assert k0 % tk == 0 and n0 % tn == 0
    koff, noff = k0 // tk, n0 // tn

    def a_map(i, j, k, *s):
        i = i + s[0][0] * mi if half_dim == "m" else i
        return (k + koff, i) if ta else (i, k + koff)

    def b_map(i, j, k, *s):
        j = j + s[0][0] * nj if half_dim == "n" else j
        return (j, k) if tb else (k, j)

    a_spec = pl.BlockSpec((tk, tm) if ta else (tm, tk), a_map)
    b_spec = pl.BlockSpec((tn, tk) if tb else (tk, tn), b_map)
    o_spec = pl.BlockSpec((tm, tn), lambda i, j, k, *s: (i, j + noff))
    add_spec = pl.BlockSpec((tm, tn), lambda i, j, k, *s: (i, j))
    dims = (((0,) if ta else (1,), (1,) if tb else (0,)), ((), ()))
    has_add = add is not None
    n_pre = 0 if half is None else 1
    n_after = len(deps) + (0 if into is None else 1)

    def body(*refs):
        refs = refs[n_pre:]
        a_ref, b_ref = refs[:2]
        add_ref = refs[2] if has_add else None
        o_ref = refs[(3 if has_add else 2) + n_after]
        acc_ref = refs[-1] if nk > 1 else None
        k = pl.program_id(2)
        part = lax.dot_general(a_ref[...], b_ref[...], dims, preferred_element_type=F32)

        def finish(r):
            if add_ref is not None:
                r = add_ref[...] + r
            o_ref[...] = r.astype(out_dtype)

        if nk == 1:
            finish(part)
        else:
            @pl.when(k == 0)
            def _():
                acc_ref[...] = part

            @pl.when(jnp.logical_and(k > 0, k < nk - 1))
            def _():
                acc_ref[...] += part

            @pl.when(k == nk - 1)
            def _():
                finish(acc_ref[...] + part)

    in_specs = [a_spec, b_spec] + ([add_spec] if has_add else []) + [ANY] * n_after
    args = (a, b) + ((add,) if has_add else ()) + tuple(deps) + ((into,) if into is not None else ())
    aliases = {n_pre + len(args) - 1: 0} if into is not None else {}
    scratch = [pltpu.VMEM((tm, tn), F32)] if nk > 1 else []
    grid = (mi, nj, nk)
    if half is None:
        grid_spec = dict(grid=grid, in_specs=in_specs, out_specs=o_spec, scratch_shapes=scratch)
    else:
        args = (half_sel,) + args
        grid_spec = dict(grid_spec=pltpu.PrefetchScalarGridSpec(
            num_scalar_prefetch=1, grid=grid, in_specs=in_specs, out_specs=o_spec, scratch_shapes=scratch))
    return pl.pallas_call(
        body,
        name=name,
        out_shape=jax.ShapeDtypeStruct((M, n_total), out_dtype),
        input_output_aliases=aliases,
        compiler_params=_params("parallel", "parallel", "arbitrary"),
        **grid_spec,
    )(*args)


def _norm_fwd(h, g, *, name):
    T, D = h.shape
    R = _pick(T, 256, 8)

    def body(h_ref, g_ref, xn_ref, r_ref):
        x = h_ref[...]
        r = lax.rsqrt(jnp.mean(x * x, axis=-1, keepdims=True) + EPS)
        xn_ref[...] = ((x * r) * g_ref[...]).astype(BF16)
        r_ref[...] = r

    return pl.pallas_call(
        body,
        name=name,
        out_shape=(jax.ShapeDtypeStruct((T, D), BF16), jax.ShapeDtypeStruct((T, 1), F32)),
        grid=(T // R,),
        in_specs=[pl.BlockSpec((R, D), lambda i: (i, 0)), pl.BlockSpec((1, D), lambda i: (0, 0))],
        out_specs=(pl.BlockSpec((R, D), lambda i: (i, 0)), pl.BlockSpec((R, 1), lambda i: (i, 0))),
        compiler_params=_params("parallel"),
    )(h, g)


def _norm_bwd(h, r, g, dxn, dres, *, name, want_dh=True):
    T, D = h.shape
    R = _pick(T, 128, 8)
    has_res = dres is not None

    def body(*refs):
        h_ref, r_ref, g_ref, dxn_ref = refs[:4]
        pos = 4
        dres_ref = None
        if has_res:
            dres_ref = refs[pos]
            pos += 1
        if want_dh:
            dh_ref, dhb_ref, dg_ref = refs[pos:pos + 3]
        else:
            dg_ref = refs[pos]
        i = pl.program_id(0)
        rr = r_ref[...]
        hn = h_ref[...] * rr
        d = dxn_ref[...].astype(F32)
        gd = d * g_ref[...]
        part = jnp.sum(d * hn, axis=0, keepdims=True)

        @pl.when(i == 0)
        def _():
            dg_ref[...] = part

        @pl.when(i > 0)
        def _():
            dg_ref[...] += part

        if want_dh:
            dh = rr * (gd - hn * jnp.mean(gd * hn, axis=-1, keepdims=True))
            if dres_ref is not None:
                dh = dres_ref[...] + dh
            dh_ref[...] = dh
            dhb_ref[...] = dh.astype(BF16)

    row = pl.BlockSpec((R, D), lambda i: (i, 0))
    vec = pl.BlockSpec((1, D), lambda i: (0, 0))
    in_specs = [row, pl.BlockSpec((R, 1), lambda i: (i, 0)), vec, row] + ([row] if has_res else [])
    args = (h, r, g, dxn) + ((dres,) if has_res else ())
    if want_dh:
        out_shape = (jax.ShapeDtypeStruct((T, D), F32), jax.ShapeDtypeStruct((T, D), BF16), jax.ShapeDtypeStruct((1, D), F32))
        out_specs = (row, row, vec)
    else:
        out_shape = jax.ShapeDtypeStruct((1, D), F32)
        out_specs = vec
    return pl.pallas_call(
        body, name=name, out_shape=out_shape, grid=(T // R,), in_specs=in_specs, out_specs=out_specs,
        compiler_params=_params("arbitrary"),
    )(*args)


def _loss_head(h, g, tgt, *, name):
    T, D = h.shape
    R = _pick(T, 128, 8)

    def body(h_ref, g_ref, t_ref, loss_ref, dh_ref, dhb_ref, dg_ref):
        i = pl.program_id(0)
        x = h_ref[...]
        gg = g_ref[...]
        r = lax.rsqrt(jnp.mean(x * x, axis=-1, keepdims=True) + EPS)
        hn = x * r
        e = hn * gg - t_ref[...]
        loss_ref[...] = 0.5 * jnp.mean(e * e, axis=-1, keepdims=True)
        dy = e * (1.0 / D)
        gd = dy * gg
        dh = r * (gd - hn * jnp.mean(gd * hn, axis=-1, keepdims=True))
        dh_ref[...] = dh
        dhb_ref[...] = dh.astype(BF16)
        part = jnp.sum(dy * hn, axis=0, keepdims=True)

        @pl.when(i == 0)
        def _():
            dg_ref[...] = part

        @pl.when(i > 0)
        def _():
            dg_ref[...] += part

    row = pl.BlockSpec((R, D), lambda i: (i, 0))
    vec = pl.BlockSpec((1, D), lambda i: (0, 0))
    return pl.pallas_call(
        body,
        name=name,
        out_shape=(jax.ShapeDtypeStruct((T, 1), F32), jax.ShapeDtypeStruct((T, D), F32),
                   jax.ShapeDtypeStruct((T, D), BF16), jax.ShapeDtypeStruct((1, D), F32)),
        grid=(T // R,),
        in_specs=[row, vec, row],
        out_specs=(pl.BlockSpec((R, 1), lambda i: (i, 0)), row, row, vec),
        compiler_params=_params("arbitrary"),
    )(h, g, tgt)


_NT = (((1,), (1,)), ((), ()))
_TN = (((0,), (0,)), ((), ()))
_NN = (((1,), (0,)), ((), ()))


def _softmax_rows(s):
    m = jnp.max(s, axis=-1, keepdims=True)
    e = jnp.exp(s - m)
    return e / jnp.sum(e, axis=-1, keepdims=True)


def _attn_fwd(q, k, v, *, name):
    T, D = q.shape
    ML = k.shape[0]
    dh = D // N_HEADS
    scale = dh ** -0.5
    R = _pick(T, 512, 16)

    def body(q_ref, k_ref, v_ref, o_ref):
        s = lax.dot_general(q_ref[...], k_ref[...], _NT, preferred_element_type=F32) * scale
        p = _softmax_rows(s)
        o_ref[...] = lax.dot_general(p.astype(BF16), v_ref[...], _NN, preferred_element_type=F32).astype(BF16)

    qs = pl.BlockSpec((R, dh), lambda i, h: (i, h))
    ks = pl.BlockSpec((ML, dh), lambda i, h: (0, h))
    return pl.pallas_call(
        body, name=name, out_shape=jax.ShapeDtypeStruct((T, D), BF16), grid=(T // R, N_HEADS),
        in_specs=[qs, ks, ks], out_specs=qs, compiler_params=_params("parallel", "parallel"),
    )(q, k, v)


def _attn_bwd(q, k, v, do, *, name):
    T, D = q.shape
    ML = k.shape[0]
    dh = D // N_HEADS
    scale = dh ** -0.5
    R = _pick(T, 512, 16)

    def body(q_ref, k_ref, v_ref, do_ref, dq_ref, dk_ref, dv_ref):
        i = pl.program_id(1)
        qq, kk, vv, dd = q_ref[...], k_ref[...], v_ref[...], do_ref[...]
        s = lax.dot_general(qq, kk, _NT, preferred_element_type=F32) * scale
        p = _softmax_rows(s)
        dp = lax.dot_general(dd, vv, _NT, preferred_element_type=F32)
        dv_part = lax.dot_general(p.astype(BF16), dd, _TN, preferred_element_type=F32)
        ds = (p * (dp - jnp.sum(p * dp, axis=-1, keepdims=True)) * scale).astype(BF16)
        dq_ref[...] = lax.dot_general(ds, kk, _NN, preferred_element_type=F32).astype(BF16)
        dk_part = lax.dot_general(ds, qq, _TN, preferred_element_type=F32)

        @pl.when(i == 0)
        def _():
            dk_ref[...] = dk_part
            dv_ref[...] = dv_part

        @pl.when(i > 0)
        def _():
            dk_ref[...] += dk_part
            dv_ref[...] += dv_part

    qs = pl.BlockSpec((R, dh), lambda h, i: (i, h))
    ks = pl.BlockSpec((ML, dh), lambda h, i: (0, h))
    return pl.pallas_call(
        body,
        name=name,
        out_shape=(jax.ShapeDtypeStruct((T, D), BF16), jax.ShapeDtypeStruct((ML, D), F32), jax.ShapeDtypeStruct((ML, D), F32)),
        grid=(N_HEADS, T // R),
        in_specs=[qs, ks, ks, qs],
        out_specs=(qs, ks, ks),
        compiler_params=_params("parallel", "arbitrary"),
    )(q, k, v, do)


def _prev_halo(R, halo, CT, col):
    per = R // halo
    return lambda i, j: (jnp.maximum(i * per - 1, 0), col(j))


def _next_halo(R, halo, T, col):
    per = R // halo
    last = T // halo - 1
    return lambda i, j: (jnp.minimum((i + 1) * per, last), col(j))


def _swap(f):
    return lambda j, i: f(i, j)


def _ffn_act_fwd(gp, up, wf, *, name):
    T, F = gp.shape
    R = _pick(T, 256, CHUNK_ROWS)
    CT = _pick(F, 1024, CHUNK_COLS)
    H = HALO_S
    RC, CC = CHUNK_ROWS, min(CHUNK_COLS, CT)

    def body(g_ref, gh_ref, u_ref, w_ref, f_ref, ext):
        i = pl.program_id(0)
        ext[pl.ds(0, H), :] = jnp.where(i > 0, gh_ref[...], 0.0)
        ext[pl.ds(H, R), :] = g_ref[...]
        for c0 in range(0, CT, CC):
            cols = pl.ds(c0, CC)
            w0, w1, w2 = [jnp.broadcast_to(w_ref[pl.ds(k, 1), cols], (RC, CC)) for k in range(KS)]
            for r0 in range(0, R, RC):
                g = ext[pl.ds(r0 + H - 2, RC), cols] * w0
                g = g + ext[pl.ds(r0 + H - 1, RC), cols] * w1
                g = g + ext[pl.ds(r0 + H, RC), cols] * w2
                f_ref[pl.ds(r0, RC), cols] = (g * _sigmoid(g) * u_ref[pl.ds(r0, RC), cols]).astype(BF16)

    main = pl.BlockSpec((R, CT), lambda i, j: (i, j))
    return pl.pallas_call(
        body,
        name=name,
        out_shape=jax.ShapeDtypeStruct((T, F), BF16),
        grid=(T // R, F // CT),
        in_specs=[main, pl.BlockSpec((H, CT), _prev_halo(R, H, CT, lambda j: j)), main,
                  pl.BlockSpec((KS_ROWS, CT), lambda i, j: (0, j))],
        out_specs=main,
        scratch_shapes=[pltpu.VMEM((R + H, CT), F32)],
        compiler_params=_params("parallel", "parallel"),
    )(gp, gp, up, wf)


def _ffn_act_bwd(gp, up, df, wf, *, name):
    T, F = gp.shape
    R = _pick(T, 256, CHUNK_ROWS)
    CT = _pick(F, 512, CHUNK_COLS)
    H = HALO_S
    HB = 16
    n_t = T // R
    RC, CC = CHUNK_ROWS, min(CHUNK_COLS, CT)

    def body(g_ref, gp_ref, gn_ref, u_ref, un_ref, d_ref, dn_ref, w_ref, dg_out, du_out, dw_ref, ext, dge):
        i = pl.program_id(1)
        last = i == n_t - 1
        ext[pl.ds(0, H), :] = jnp.where(i > 0, gp_ref[...], 0.0)
        ext[pl.ds(H, R), :] = g_ref[...]
        ext[pl.ds(H + R, H), :] = gn_ref[...]

        def dact(g, u, d):
            sg = _sigmoid(g)
            return d * u * (sg * (1.0 + g * (1.0 - sg))), d * (g * sg)

        dw_rows = [[] for _ in range(KS)]
        for c0 in range(0, CT, CC):
            cols = pl.ds(c0, CC)
            w = [jnp.broadcast_to(w_ref[pl.ds(k, 1), cols], (RC, CC)) for k in range(KS)]
            acc = [jnp.zeros((RC, CC), F32) for _ in range(KS)]
            for r0 in range(0, R, RC):
                rows = pl.ds(r0, RC)
                taps = [ext[pl.ds(r0 + H - 2 + k, RC), cols] for k in range(KS)]
                g = taps[0] * w[0] + taps[1] * w[1] + taps[2] * w[2]
                dg, du = dact(g, u_ref[rows, cols], d_ref[rows, cols].astype(F32))
                du_out[rows, cols] = du.astype(BF16)
                dge[rows, cols] = dg
                acc = [a + dg * t for a, t in zip(acc, taps)]
            taps = [ext[pl.ds(R + H - 2 + k, H), cols] for k in range(KS)]
            g = taps[0] * w[0][:H] + taps[1] * w[1][:H] + taps[2] * w[2][:H]
            d_next = jnp.where(last, 0.0, dn_ref[pl.ds(0, H), cols].astype(F32))
            dg_next, _ = dact(g, un_ref[:, cols], d_next)
            dge[pl.ds(R, H), cols] = jnp.where(last, 0.0, dg_next)
            for r0 in range(0, R, RC):
                dgp = (dge[pl.ds(r0, RC), cols] * w[2] + dge[pl.ds(r0 + 1, RC), cols] * w[1]
                       + dge[pl.ds(r0 + 2, RC), cols] * w[0])
                dg_out[pl.ds(r0, RC), cols] = dgp.astype(BF16)
            for k in range(KS):
                dw_rows[k].append(jnp.sum(acc[k], axis=0, keepdims=True))
        rows = [jnp.concatenate(r, axis=1) if len(r) > 1 else r[0] for r in dw_rows]
        rows.append(jnp.zeros((KS_ROWS - KS, CT), F32))
        _acc_rows_block(dw_ref, i, rows)

    col = lambda j: j
    main = pl.BlockSpec((R, CT), lambda j, i: (i, j))
    prev8 = pl.BlockSpec((H, CT), _swap(_prev_halo(R, H, CT, col)))
    next8 = pl.BlockSpec((H, CT), _swap(_next_halo(R, H, T, col)))
    next16 = pl.BlockSpec((HB, CT), _swap(_next_halo(R, HB, T, col)))
    wspec = pl.BlockSpec((KS_ROWS, CT), lambda j, i: (0, j))
    return pl.pallas_call(
        body,
        name=name,
        out_shape=(jax.ShapeDtypeStruct((T, F), BF16), jax.ShapeDtypeStruct((T, F), BF16), jax.ShapeDtypeStruct((KS_ROWS, F), F32)),
        grid=(F // CT, n_t),
        in_specs=[main, prev8, next8, main, next8, main, next16, wspec],
        out_specs=(main, main, wspec),
        scratch_shapes=[pltpu.VMEM((R + 2 * H, CT), F32), pltpu.VMEM((R + H, CT), F32)],
        compiler_params=_params("parallel", "arbitrary"),
    )(gp, gp, gp, up, up, df, df, wf)


def _acc_rows_block(ref, i, rows):
    *singles, pad = rows

    @pl.when(i == 0)
    def _():
        for k, row in enumerate(singles):
            ref[pl.ds(k, 1), :] = row
        ref[pl.ds(len(singles), pad.shape[0]), :] = pad

    @pl.when(i > 0)
    def _():
        for k, row in enumerate(singles):
            ref[pl.ds(k, 1), :] += row


def _group_stats(x):
    mu = jnp.mean(x, axis=-1, keepdims=True)
    xc = x - mu
    var = jnp.mean(xc * xc, axis=-1, keepdims=True)
    return xc, lax.rsqrt(var + EPS)


def _shifted_rows(rolled, x, cols):
    n = x.shape[0]
    for b in range(1, 8):
        rolled[b - 1, :, cols] = pltpu.roll(x, n - b, axis=0)


def _window(src, rolled, off, rows, cols):
    b = off % 8
    if b == 0:
        return src[pl.ds(off, rows), cols]
    return rolled[b - 1, pl.ds(off - b, rows), cols]


def _mix_a_fwd(proj, wa, ba, lg, lb, *, cw, name, deps=()):
    T = proj.shape[0]
    R = _pick(T, 128, HALO_A)
    CT = _pick(cw, 256)
    nc = cw // CT
    H = HALO_A

    def body(av_ref, ag_ref, avh_ref, agh_ref, w_ref, b_ref, lg_ref, lb_ref, *rest):
        u3_ref, u1_ref, ext, rolled = rest[len(deps):]
        i = pl.program_id(0)
        for gi in range(CT // GROUP):
            cols = pl.ds(gi * GROUP, GROUP)
            ext[pl.ds(0, H), cols] = jnp.where(i > 0, avh_ref[:, cols] * _sigmoid(agh_ref[:, cols]), 0.0)
            ext[pl.ds(H, R), cols] = av_ref[:, cols] * _sigmoid(ag_ref[:, cols])
            _shifted_rows(rolled, ext[:, cols], cols)
            acc = _window(ext, rolled, H - (KA - 1), R, cols) * w_ref[pl.ds(0, 1), cols]
            for k in range(1, KA):
                acc = acc + _window(ext, rolled, H - (KA - 1) + k, R, cols) * w_ref[pl.ds(k, 1), cols]
            u1 = acc + b_ref[:, cols]
            u1_ref[:, cols] = u1
            xc, rstd = _group_stats(u1)
            u2 = (xc * rstd) * lg_ref[:, cols] + lb_ref[:, cols]
            u3_ref[:, cols] = (u2 * _sigmoid(u2)).astype(BF16)

    main_v = pl.BlockSpec((R, CT), lambda i, j: (i, j))
    main_g = pl.BlockSpec((R, CT), lambda i, j: (i, j + nc))
    halo_v = pl.BlockSpec((H, CT), _prev_halo(R, H, CT, lambda j: j))
    halo_g = pl.BlockSpec((H, CT), _prev_halo(R, H, CT, lambda j: j + nc))
    vec = pl.BlockSpec((1, CT), lambda i, j: (0, j))
    out = pl.BlockSpec((R, CT), lambda i, j: (i, j))
    return pl.pallas_call(
        body,
        name=name,
        out_shape=(jax.ShapeDtypeStruct((T, cw), BF16), jax.ShapeDtypeStruct((T, cw), F32)),
        grid=(T // R, nc),
        in_specs=[main_v, main_g, halo_v, halo_g, pl.BlockSpec((KA_ROWS, CT), lambda i, j: (0, j)), vec, vec, vec]
        + [ANY] * len(deps),
        out_specs=(out, out),
        scratch_shapes=[pltpu.VMEM((R + H, CT), F32), pltpu.VMEM((7, R + H, CT), F32)],
        compiler_params=_params("parallel", "parallel"),
    )(proj, proj, proj, proj, wa, ba, lg, lb, *deps)


def _mix_a_bwd(proj, u1, dmix, wa, lg, lb, *, cw, name):
    T = proj.shape[0]
    R = _pick(T, 128, HALO_A)
    CT = _pick(cw, 256)
    nc = cw // CT
    H = HALO_A
    n_t = T // R
    NG = CT // GROUP

    def body(av_ref, ag_ref, avh_ref, agh_ref, u1_ref, u1n_ref, d3_ref, d3n_ref, w_ref, lg_ref, lb_ref,
             dav_ref, dag_ref, dw_ref, db_ref, dlg_ref, dlb_ref, ext, d1, ext_rolled, d1_rolled):
        i = pl.program_id(1)
        last = i == n_t - 1

        def ln_bwd(u1, d3, cols):
            xc, rstd = _group_stats(u1)
            xh = xc * rstd
            g = lg_ref[:, cols]
            u2 = xh * g + lb_ref[:, cols]
            sg = _sigmoid(u2)
            du2 = d3 * (sg * (1.0 + u2 * (1.0 - sg)))
            dxh = du2 * g
            du1 = rstd * (dxh - jnp.mean(dxh, axis=-1, keepdims=True) - xh * jnp.mean(dxh * xh, axis=-1, keepdims=True))
            return du1, du2 * xh, du2

        def lanes(parts):
            return jnp.concatenate(parts, axis=1) if len(parts) > 1 else parts[0]

        dlg_parts, dlb_parts, db_parts = [], [], []
        dw_parts = [[] for _ in range(KA)]
        for gi in range(NG):
            cols = pl.ds(gi * GROUP, GROUP)
            du1, dlg, dlb = ln_bwd(u1_ref[:, cols], d3_ref[:, cols], cols)
            d1[pl.ds(0, R), cols] = du1
            dlg_parts.append(jnp.sum(dlg, axis=0, keepdims=True))
            dlb_parts.append(jnp.sum(dlb, axis=0, keepdims=True))
            db_parts.append(jnp.sum(du1, axis=0, keepdims=True))
            du1n, _, _ = ln_bwd(u1n_ref[:, cols], jnp.where(last, 0.0, d3n_ref[:, cols]), cols)
            d1[pl.ds(R, H), cols] = jnp.where(last, 0.0, du1n)

            _shifted_rows(d1_rolled, d1[:, cols], cols)
            du0 = _window(d1, d1_rolled, KA - 1, R, cols) * w_ref[pl.ds(0, 1), cols]
            for k in range(1, KA):
                du0 = du0 + _window(d1, d1_rolled, KA - 1 - k, R, cols) * w_ref[pl.ds(k, 1), cols]
            av = av_ref[:, cols]
            sg = _sigmoid(ag_ref[:, cols])
            dav_ref[:, cols] = (du0 * sg).astype(BF16)
            dag_ref[:, cols] = (du0 * av * sg * (1.0 - sg)).astype(BF16)

            ext[pl.ds(0, H), cols] = jnp.where(i > 0, avh_ref[:, cols] * _sigmoid(agh_ref[:, cols]), 0.0)
            ext[pl.ds(H, R), cols] = av * sg
            _shifted_rows(ext_rolled, ext[:, cols], cols)
            d1_main = d1[pl.ds(0, R), cols]
            for k in range(KA):
                window = _window(ext, ext_rolled, H - (KA - 1) + k, R, cols)
                dw_parts[k].append(jnp.sum(d1_main * window, axis=0, keepdims=True))

        dlg_row, dlb_row, db_row = lanes(dlg_parts), lanes(dlb_parts), lanes(db_parts)

        @pl.when(i == 0)
        def _():
            dlg_ref[...] = dlg_row
            dlb_ref[...] = dlb_row
            db_ref[...] = db_row

        @pl.when(i > 0)
        def _():
            dlg_ref[...] += dlg_row
            dlb_ref[...] += dlb_row
            db_ref[...] += db_row

        rows = [lanes(p) for p in dw_parts]
        rows.append(jnp.zeros((KA_ROWS - KA, CT), F32))
        _acc_rows_block(dw_ref, i, rows)

    cv = lambda j: j
    cg = lambda j: j + nc
    main_v = pl.BlockSpec((R, CT), lambda j, i: (i, j))
    main_g = pl.BlockSpec((R, CT), lambda j, i: (i, j + nc))
    prev_v = pl.BlockSpec((H, CT), _swap(_prev_halo(R, H, CT, cv)))
    prev_g = pl.BlockSpec((H, CT), _swap(_prev_halo(R, H, CT, cg)))
    nxt = pl.BlockSpec((H, CT), _swap(_next_halo(R, H, T, cv)))
    wspec = pl.BlockSpec((KA_ROWS, CT), lambda j, i: (0, j))
    vec = pl.BlockSpec((1, CT), lambda j, i: (0, j))
    vshape = jax.ShapeDtypeStruct((1, cw), F32)
    return pl.pallas_call(
        body,
        name=name,
        out_shape=(jax.ShapeDtypeStruct((T, cw), BF16), jax.ShapeDtypeStruct((T, cw), BF16),
                   jax.ShapeDtypeStruct((KA_ROWS, cw), F32), vshape, vshape, vshape),
        grid=(nc, n_t),
        in_specs=[main_v, main_g, prev_v, prev_g, main_v, nxt, main_v, nxt, wspec, vec, vec],
        out_specs=(main_v, main_v, wspec, vec, vec, vec),
        scratch_shapes=[pltpu.VMEM((R + H, CT), F32), pltpu.VMEM((R + H, CT), F32),
                        pltpu.VMEM((7, R + H, CT), F32), pltpu.VMEM((7, R + H, CT), F32)],
        compiler_params=_params("parallel", "arbitrary"),
    )(proj, proj, proj, proj, u1, u1, dmix, dmix, wa, lg, lb)


def _mix_b_fwd(proj, wb, *, cw, sw, name):
    T = proj.shape[0]
    R = _pick(T, 256, 8)
    CT = _pick(sw, 512)
    nb, nc, nh = (2 * cw) // CT, (2 * cw + sw) // CT, (2 * cw + 2 * sw) // CT
    H = HALO_S

    def body(b_ref, c_ref, h_ref, ch_ref, hh_ref, w_ref, v_ref, ext):
        i = pl.program_id(0)
        ext[pl.ds(0, H), :] = jnp.where(i > 0, ch_ref[...] * hh_ref[...], 0.0)
        ext[pl.ds(H, R), :] = c_ref[...] * h_ref[...]
        w = w_ref[...]
        zc = ext[pl.ds(H - 2, R), :] * w[0:1, :]
        zc = zc + ext[pl.ds(H - 1, R), :] * w[1:2, :]
        zc = zc + ext[pl.ds(H, R), :] * w[2:3, :]
        v_ref[...] = (b_ref[...] * zc).astype(BF16)

    def main(off):
        return pl.BlockSpec((R, CT), lambda i, j: (i, j + off))

    def prev(off):
        return pl.BlockSpec((H, CT), _prev_halo(R, H, CT, lambda j: j + off))

    return pl.pallas_call(
        body,
        name=name,
        out_shape=jax.ShapeDtypeStruct((T, sw), BF16),
        grid=(T // R, sw // CT),
        in_specs=[main(nb), main(nc), main(nh), prev(nc), prev(nh), pl.BlockSpec((KS_ROWS, CT), lambda i, j: (0, j))],
        out_specs=pl.BlockSpec((R, CT), lambda i, j: (i, j)),
        scratch_shapes=[pltpu.VMEM((R + H, CT), F32)],
        compiler_params=_params("parallel", "parallel"),
    )(proj, proj, proj, proj, proj, wb)


def _mix_b_bwd(proj, dmix, wb, *, cw, sw, name):
    T = proj.shape[0]
    R = _pick(T, 256, 8)
    CT = _pick(sw, 512)
    nb, nc, nh = (2 * cw) // CT, (2 * cw + sw) // CT, (2 * cw + 2 * sw) // CT
    nd = cw // CT
    H = HALO_S
    n_t = T // R

    def body(b_ref, bn_ref, c_ref, cp_ref, cn_ref, h_ref, hp_ref, hn_ref, d_ref, dn_ref, w_ref,
             db_ref, dc_ref, dhh_ref, dw_ref, ext, dze):
        i = pl.program_id(1)
        last = i == n_t - 1
        cc, hh = c_ref[...], h_ref[...]
        ext[pl.ds(0, H), :] = jnp.where(i > 0, cp_ref[...] * hp_ref[...], 0.0)
        ext[pl.ds(H, R), :] = cc * hh
        ext[pl.ds(H + R, H), :] = cn_ref[...] * hn_ref[...]
        w = w_ref[...]
        w0, w1, w2 = w[0:1, :], w[1:2, :], w[2:3, :]

        def conv(start, n):
            z = ext[pl.ds(start + H - 2, n), :] * w0
            z = z + ext[pl.ds(start + H - 1, n), :] * w1
            return z + ext[pl.ds(start + H, n), :] * w2

        d_main = d_ref[...]
        db_ref[...] = (d_main * conv(0, R)).astype(BF16)
        dzc_main = d_main * b_ref[...]
        dze[pl.ds(0, R), :] = dzc_main
        dze[pl.ds(R, H), :] = jnp.where(last, 0.0, dn_ref[...] * bn_ref[...])
        dz = dze[pl.ds(0, R), :] * w2 + dze[pl.ds(1, R), :] * w1 + dze[pl.ds(2, R), :] * w0
        dc_ref[...] = (dz * hh).astype(BF16)
        dhh_ref[...] = (dz * cc).astype(BF16)
        rows = [jnp.sum(dzc_main * ext[pl.ds(H - 2 + k, R), :], axis=0, keepdims=True) for k in range(KS)]
        rows.append(jnp.zeros((KS_ROWS - KS, CT), F32))
        _acc_rows_block(dw_ref, i, rows)

    def main(off):
        return pl.BlockSpec((R, CT), lambda j, i: (i, j + off))

    def prev(off):
        return pl.BlockSpec((H, CT), _swap(_prev_halo(R, H, CT, lambda j: j + off)))

    def nxt(off):
        return pl.BlockSpec((H, CT), _swap(_next_halo(R, H, T, lambda j: j + off)))

    out = pl.BlockSpec((R, CT), lambda j, i: (i, j))
    wspec = pl.BlockSpec((KS_ROWS, CT), lambda j, i: (0, j))
    act = jax.ShapeDtypeStruct((T, sw), BF16)
    return pl.pallas_call(
        body,
        name=name,
        out_shape=(act, act, act, jax.ShapeDtypeStruct((KS_ROWS, sw), F32)),
        grid=(sw // CT, n_t),
        in_specs=[main(nb), nxt(nb), main(nc), prev(nc), nxt(nc), main(nh), prev(nh), nxt(nh), main(nd), nxt(nd), wspec],
        out_specs=(out, out, out, wspec),
        scratch_shapes=[pltpu.VMEM((R + 2 * H, CT), F32), pltpu.VMEM((R + H, CT), F32)],
        compiler_params=_params("parallel", "arbitrary"),
    )(proj, proj, proj, proj, proj, proj, proj, proj, dmix, dmix, wb)


def _adamw(w, g, m, v, *, name, emit_grad=False):
    Rr, Cc = w.shape
    R = _pick(Rr, max(8, ADAMW_BLOCK_BYTES // (4 * Cc)), 8)

    def body(w_ref, g_ref, m_ref, v_ref, d_ref, mo_ref, vo_ref, *g_out):
        g = g_ref[...]
        m2 = ADAM_B1 * m_ref[...] + (1.0 - ADAM_B1) * g
        v2 = ADAM_B2 * v_ref[...] + (1.0 - ADAM_B2) * (g * g)
        m_hat = m2 / (1.0 - ADAM_B1 ** ADAM_STEP)
        v_hat = v2 / (1.0 - ADAM_B2 ** ADAM_STEP)
        d_ref[...] = -ADAM_LR * (m_hat / (jnp.sqrt(v_hat) + ADAM_EPS) + ADAM_WD * w_ref[...])
        mo_ref[...] = m2
        vo_ref[...] = v2
        if emit_grad:
            g_out[0][...] = g

    spec = pl.BlockSpec((R, Cc), lambda i: (i, 0))
    shp = jax.ShapeDtypeStruct((Rr, Cc), F32)
    n_out = 4 if emit_grad else 3
    return pl.pallas_call(
        body, name=name, out_shape=(shp,) * n_out, grid=(Rr // R,), in_specs=[spec] * 4, out_specs=(spec,) * n_out,
        compiler_params=_params("parallel"),
    )(w, g, m, v)


def _sum_chips(cs, rb, pc, lay, *, name):
    _, Rr, Cc = rb.shape
    tr, tc = _pick(Rr, 512, 16), _pick(Cc, 2048)
    nr, ncol = Rr // tr, Cc // tc
    if lay.axis == 0:
        own_map = lambda i, j, s: (i + s[0] * nr, j)
        out_map = lambda i, j, s: (i, j + s[1] * ncol)
    else:
        own_map = lambda i, j, s: (i, j + s[0] * ncol)
        out_map = lambda i, j, s: (i + s[1] * nr, j)

    def body(s_ref, own_ref, rb_ref, o_ref):
        acc = own_ref[...].astype(F32)
        for j in range(N_CHIPS - 1):
            acc = acc + rb_ref[j].astype(F32)
        o_ref[...] = acc

    return pl.pallas_call(
        body,
        name=name,
        out_shape=jax.ShapeDtypeStruct(lay.shard_shape(), F32),
        grid_spec=pltpu.PrefetchScalarGridSpec(
            num_scalar_prefetch=1,
            grid=(nr, ncol),
            in_specs=[pl.BlockSpec((tr, tc), own_map), pl.BlockSpec((N_CHIPS - 1, tr, tc), lambda i, j, s: (0, i, j))],
            out_specs=pl.BlockSpec((tr, tc), out_map)),
        compiler_params=_params("parallel", "parallel"),
    )(pc, cs, rb)


def _place():
    x, y, c = lax.axis_index("x"), lax.axis_index("y"), lax.axis_index("c")
    return x, y, c, 2 * x + y


def _other_chips(x, y):
    return [(1 - x, y, 2 * (1 - x) + y), (x, 1 - y, 2 * x + (1 - y)), (1 - x, 1 - y, 2 * (1 - x) + (1 - y))]


def _allgather_small(buf, *, name, reduce):
    S = buf.shape[0]

    def body(x_ref, o_ref, gat, send_sems, recv_sems):
        x, y, c, _ = _place()
        me = 4 * x + 2 * y + c
        gat[me] = x_ref[...]
        copies = []
        for k in range(1, N_DEV):
            fx, fy, fc = (k >> 2) & 1, (k >> 1) & 1, k & 1
            px = 1 - x if fx else x
            py = 1 - y if fy else y
            pc = 1 - c if fc else c
            peer = 4 * px + 2 * py + pc
            send = pltpu.make_async_remote_copy(
                src_ref=x_ref, dst_ref=gat.at[me], send_sem=send_sems.at[k - 1], recv_sem=recv_sems.at[k - 1],
                device_id=(px, py, pc), device_id_type=MESH)
            send.start()
            arrival = pltpu.make_async_remote_copy(
                src_ref=x_ref, dst_ref=gat.at[peer], send_sem=send_sems.at[k - 1], recv_sem=recv_sems.at[k - 1],
                device_id=(px, py, pc), device_id_type=MESH)
            copies.append((send, arrival))
        for send, arrival in copies:
            arrival.wait_recv()
        for send, arrival in copies:
            send.wait_send()
        if reduce:
            acc = gat[0]
            for d in range(1, N_DEV):
                acc = acc + gat[d]
            o_ref[...] = acc
        else:
            o_ref[...] = gat[...]

    out_shape = jax.ShapeDtypeStruct((S, LANES) if reduce else (N_DEV, S, LANES), F32)
    return pl.pallas_call(
        body,
        name=name,
        out_shape=out_shape,
        in_specs=[pl.BlockSpec(memory_space=pltpu.VMEM)],
        out_specs=pl.BlockSpec(memory_space=pltpu.VMEM),
        scratch_shapes=[pltpu.VMEM((N_DEV, S, LANES), F32), pltpu.SemaphoreType.DMA((N_DEV - 1,)),
                        pltpu.SemaphoreType.DMA((N_DEV - 1,))],
        compiler_params=pltpu.CompilerParams(vmem_limit_bytes=VMEM_LIMIT),
    )(buf)


class _Sharded:
    def __init__(self, shape, axis):
        self.shape = shape
        self.axis = axis
        self.block = shape[axis] // N_CHIPS
        self.half = shape[1 - axis] // 2

    def _sl(self, along, across):
        return (along, across) if self.axis == 0 else (across, along)

    def block_slice(self, q):
        return self._sl(pl.ds(q * self.block, self.block), pl.ds(0, self.shape[1 - self.axis]))

    def block_half_slice(self, q, c):
        return self._sl(pl.ds(q * self.block, self.block), pl.ds(c * self.half, self.half))

    def half_slice(self, c):
        return self._sl(pl.ds(0, self.shape[self.axis]), pl.ds(c * self.half, self.half))

    def shard_half_slice(self, c):
        return self._sl(pl.ds(0, self.block), pl.ds(c * self.half, self.half))

    def shard_shape(self):
        return self._sl(self.block, self.shape[1 - self.axis])

    def half_shape(self):
        return self._sl(self.shape[self.axis], self.half)

    def block_half_shape(self):
        return self._sl(self.block, self.half)

    def block_in_half_slice(self, q):
        return self._sl(pl.ds(q * self.block, self.block), pl.ds(0, self.half))


def _at(ref, sl):
    return ref.at[sl[0], sl[1]]


class _Copy:
    def __init__(self, src, dst, arrive, dev):
        self.src, self.dst, self.arrive, self.dev = src, dst, arrive, dev


HBM = pl.BlockSpec(memory_space=pltpu.HBM)
SEM = pl.BlockSpec(memory_space=pltpu.SEMAPHORE)
EFFECT = pltpu.SideEffectType.DATAFLOW_SIDE_EFFECTING


def _exchange_start(srcs, land_shapes, plan, after, *, name):
    ns, nl = len(srcs), len(land_shapes)
    n_copies = len(plan([None] * ns, [None] * nl, dry=True))

    def body(*refs):
        src_refs, land_refs = refs[:ns], refs[ns:ns + nl]
        send_sems, recv_sems = refs[ns + nl + 1], refs[ns + nl + 2]
        token = refs[-1]
        for k, cp in enumerate(plan(src_refs, land_refs)):
            pltpu.make_async_remote_copy(
                src_ref=cp.src, dst_ref=cp.dst, send_sem=send_sems.at[k], recv_sem=recv_sems.at[k],
                device_id=cp.dev, device_id_type=MESH).start()
        token[...] = jnp.zeros_like(token)

    sem = pltpu.SemaphoreType.DMA((n_copies,))
    lands = [pltpu.with_memory_space_constraint(lax.empty(shp, dt), pltpu.HBM) for shp, dt in land_shapes]
    srcs = [pltpu.with_memory_space_constraint(a, pltpu.HBM) for a in srcs]
    thru = [pltpu.HBM(a.shape, a.dtype) for a in srcs + lands]
    outs = pl.pallas_call(
        body,
        name=name,
        out_shape=(sem, sem, *thru, jax.ShapeDtypeStruct((8, LANES), F32)),
        in_specs=[HBM] * (ns + nl) + [ANY],
        out_specs=(SEM, SEM, *[HBM] * (ns + nl), pl.BlockSpec(memory_space=pltpu.VMEM)),
        input_output_aliases={i: 2 + i for i in range(ns + nl)},
        compiler_params=pltpu.CompilerParams(has_side_effects=EFFECT),
    )(*srcs, *lands, after)
    return outs[0], outs[1], list(outs[2:2 + ns]), list(outs[2 + ns:2 + ns + nl]), outs[-1]


def _exchange_wait(send_sems, recv_sems, srcs, lands, plan, after, *, name):
    ns, nl = len(srcs), len(lands)
    after = tuple(after) if isinstance(after, (tuple, list)) else (after,)

    def body(*refs):
        src_refs, land_refs = refs[:ns], refs[ns:ns + nl]
        send_sems, recv_sems = refs[ns + nl], refs[ns + nl + 1]
        copies = [
            pltpu.make_async_remote_copy(
                src_ref=cp.src, dst_ref=cp.arrive, send_sem=send_sems.at[k], recv_sem=recv_sems.at[k],
                device_id=cp.dev, device_id_type=MESH)
            for k, cp in enumerate(plan(src_refs, land_refs))
        ]
        for cp in copies:
            cp.wait_recv()
        for cp in copies:
            cp.wait_send()

    thru = [pltpu.HBM(a.shape, a.dtype) for a in list(srcs) + list(lands)]
    outs = pl.pallas_call(
        body,
        name=name,
        out_shape=tuple(thru),
        in_specs=[HBM] * (ns + nl) + [SEM, SEM] + [ANY] * len(after),
        out_specs=tuple([HBM] * (ns + nl)),
        input_output_aliases={i: i for i in range(ns + nl)},
        compiler_params=pltpu.CompilerParams(has_side_effects=EFFECT),
    )(*srcs, *lands, send_sems, recv_sems, *after)
    return list(outs[:ns]), list(outs[ns:])


def _gather_plan(layouts):
    def plan(srcs, lands, dry=False):
        if dry:
            return [None] * (4 * len(layouts))
        x, y, c, p = _place()
        copies = []
        for s, g, lay in zip(srcs, lands, layouts):
            own = _at(g, lay.block_slice(p))
            copies.append(_Copy(s, own, own, (x, y, 1 - c)))
            for qx, qy, q in _other_chips(x, y):
                copies.append(_Copy(_at(s, lay.shard_half_slice(c)), _at(g, lay.block_half_slice(p, c)),
                                    _at(g, lay.block_half_slice(q, c)), (qx, qy, c)))
        return copies

    return plan


def _forward_plan(layouts):
    def plan(srcs, lands, dry=False):
        if dry:
            return [None] * (3 * len(layouts))
        x, y, c, _ = _place()
        copies = []
        for g, lay in zip(srcs, layouts):
            for qx, qy, q in _other_chips(x, y):
                landed = _at(g, lay.block_half_slice(q, c))
                copies.append(_Copy(landed, landed, _at(g, lay.block_half_slice(q, 1 - c)), (x, y, 1 - c)))
        return copies

    return plan


def _sibling_plan(n):
    def plan(srcs, lands, dry=False):
        if dry:
            return [None] * n
        x, y, c, _ = _place()
        return [_Copy(s, ra, ra, (x, y, 1 - c)) for s, ra in zip(srcs, lands)]

    return plan


def _chips_plan(layouts):
    def plan(srcs, lands, dry=False):
        if dry:
            return [None] * (3 * len(layouts))
        x, y, c, _ = _place()
        copies = []
        for s, rb, lay in zip(srcs, lands, layouts):
            for j, (qx, qy, q) in enumerate(_other_chips(x, y)):
                copies.append(_Copy(_at(s, lay.block_in_half_slice(q)), rb.at[j], rb.at[j], (qx, qy, c)))
        return copies

    return plan


def _join_plan(layouts):
    def plan(srcs, lands, dry=False):
        if dry:
            return [None] * len(layouts)
        x, y, c, _ = _place()
        copies = []
        for g, lay in zip(srcs, layouts):
            mine = _at(g, lay.shard_half_slice(c))
            copies.append(_Copy(mine, mine, _at(g, lay.shard_half_slice(1 - c)), (x, y, 1 - c)))
        return copies

    return plan


def _pack(arrays):
    flat = [a.reshape(-1) for a in arrays]
    sizes = [f.shape[0] for f in flat]
    total = sum(sizes)
    rows = _round_up(-(-total // LANES), 8)
    flat.append(jnp.zeros((rows * LANES - total,), F32))
    return jnp.concatenate(flat).reshape(rows, LANES)


def _unpack(buf, shapes):
    flat = buf.reshape(-1)
    out, pos = [], 0
    for shp in shapes:
        n = 1
        for d in shp:
            n *= d
        out.append(flat[pos:pos + n].reshape(shp))
        pos += n
    return out


def _pad_to(a, rows, cols):
    return jnp.pad(a, ((0, rows - a.shape[0]), (0, cols - a.shape[1])))


def kernel(x, mem, g_mix, w_in, conv_a_w, conv_a_b, ln_a_g, ln_a_b, conv_b_w, w_out, g_xattn, g_mem, w_q, w_k, w_v, w_o, g_ffn, w_gate, w_up, conv_f_w, w_down, g_final, loss_target, m_g_mix, m_w_in, m_conv_a_w, m_conv_a_b, m_ln_a_g, m_ln_a_b, m_conv_b_w, m_w_out, m_g_xattn, m_g_mem, m_w_q, m_w_k, m_w_v, m_w_o, m_g_ffn, m_w_gate, m_w_up, m_conv_f_w, m_w_down, m_g_final, v_g_mix, v_w_in, v_conv_a_w, v_conv_a_b, v_ln_a_g, v_ln_a_b, v_conv_b_w, v_w_out, v_g_xattn, v_g_mem, v_w_q, v_w_k, v_w_v, v_w_o, v_g_ffn, v_w_gate, v_w_up, v_conv_f_w, v_w_down, v_g_final):
    T, D = x.shape[1], x.shape[2]
    in_sh = w_in.shape[2]
    cw_sh = conv_a_w.shape[2]
    cw = N_CHIPS * cw_sh
    f_sh = w_gate.shape[2]
    fp = _round_up(f_sh, 256)
    F = N_CHIPS * fp
    rs = w_out.shape[1]
    c_idx = lax.axis_index("c")
    p_idx = 2 * lax.axis_index("x") + lax.axis_index("y")

    def t_(a):
        return jnp.swapaxes(a[0], 0, 1)

    big = {
        "w_in": (lambda: w_in[0].astype(BF16), _Sharded((D, N_CHIPS * in_sh), 1)),
        "w_out": (lambda: w_out[0].astype(BF16), _Sharded((N_CHIPS * rs, D), 0)),
        "w_q": (lambda: w_q[0].astype(BF16), _Sharded((D, D), 0)),
        "w_k": (lambda: w_k[0].astype(BF16), _Sharded((D, D), 0)),
        "w_v": (lambda: w_v[0].astype(BF16), _Sharded((D, D), 0)),
        "w_o": (lambda: w_o[0].astype(BF16), _Sharded((D, D), 0)),
        "w_gate": (lambda: _pad_to(t_(w_gate).astype(BF16), fp, D), _Sharded((F, D), 0)),
        "w_up": (lambda: _pad_to(t_(w_up).astype(BF16), fp, D), _Sharded((F, D), 0)),
        "w_down": (lambda: _pad_to(w_down[0].astype(BF16), fp, D), _Sharded((F, D), 0)),
    }
    names = list(big)
    lay = {k: big[k][1] for k in names}
    c_arr = c_idx.astype(jnp.int32).reshape(1)
    pc_arr = jnp.stack([p_idx, c_idx]).astype(jnp.int32)

    conv_shapes = [(KA_ROWS, cw_sh), (KS_ROWS, cw_sh), (KS_ROWS, fp)]
    conv_pack = _pack([_pad_to(conv_a_w[0], KA_ROWS, cw_sh), _pad_to(conv_b_w[0], KS_ROWS, cw_sh),
                       _pad_to(conv_f_w[0], KS_ROWS, fp)])
    conv_all = _allgather_small(conv_pack, name="allgather_conv", reduce=False)
    per_chip = [_unpack(conv_all[2 * q], conv_shapes) for q in range(N_CHIPS)]
    wa = jnp.concatenate([pc[0] for pc in per_chip], axis=1)
    wb = jnp.concatenate([pc[1] for pc in per_chip], axis=1)
    wf = jnp.concatenate([pc[2] for pc in per_chip], axis=1)

    tok = conv_all

    in_half = _Sharded((D // 2, N_CHIPS * in_sh), 1)
    sources = {k: big[k] for k in names if k != "w_in"}
    sources["w_in_lo"] = (lambda: w_in[0, :D // 2].astype(BF16), in_half)
    sources["w_in_hi"] = (lambda: w_in[0, D // 2:].astype(BF16), in_half)
    gather_groups = {"in_lo": ["w_in_lo"], "in_hi": ["w_in_hi"], "out": ["w_out"], "qkv": ["w_q", "w_k", "w_v"],
                     "o": ["w_o"], "gate": ["w_gate"], "up": ["w_up"], "down": ["w_down"]}
    gathers = {}
    for tag, grp in gather_groups.items():
        lays = [sources[k][1] for k in grp]
        plan = _gather_plan(lays)
        ssem, rsem, srcs, lands, tok = _exchange_start(
            [sources[k][0]() for k in grp], [(l.shape, BF16) for l in lays], plan, tok, name=f"gather_start_{tag}")
        gathers[tag] = (ssem, rsem, srcs, lands, plan, lays, grp)
    W = {}
    relays = {}

    def relay(tag, after):
        ssem, rsem, srcs, lands, plan, lays, grp = gathers[tag]
        _, lands = _exchange_wait(ssem, rsem, srcs, lands, plan, after, name=f"gather_wait_{tag}")
        plan = _forward_plan(lays)
        ssem, rsem, lands, _, token = _exchange_start(lands, [], plan, after, name=f"gather_forward_start_{tag}")
        relays[tag] = (ssem, rsem, lands, plan, grp)
        return token

    def gathered(tag, after):
        if tag not in relays:
            relay(tag, after)
        ssem, rsem, lands, plan, grp = relays[tag]
        lands, _ = _exchange_wait(ssem, rsem, lands, [], plan, after, name=f"gather_forward_wait_{tag}")
        W.update(zip(grp, lands))

    relay("in_lo", tok)

    xs, tgt, mems = x[0], loss_target[0], mem[0]
    g_mem2, g_final2 = g_mem[None, :], g_final[None, :]

    memn, rm = _norm_fwd(mems, g_mem2, name="norm_mem")
    xn1, r1 = _norm_fwd(xs, g_mix, name="norm_mix")
    gathered("in_lo", xn1)
    proj = _mm(xn1, W["w_in_lo"], a_k=(0, D // 2), name="mm_in_lo")
    gathered("in_hi", proj)
    proj = _mm(xn1, W["w_in_hi"], a_k=(D // 2, D // 2), add=proj, name="mm_in_hi")
    ftok = relay("out", proj)
    u3, u1 = _mix_a_fwd(proj, wa, conv_a_b, ln_a_g, ln_a_b, cw=cw, deps=[ftok], name="mix_a_fwd")
    vb = _mix_b_fwd(proj, wb, cw=cw, sw=cw, name="mix_b_fwd")
    mix = jnp.concatenate([u3, vb], axis=1)
    gathered("out", mix)
    ftok = relay("qkv", mix)
    h1 = _mm(mix, W["w_out"], add=xs, deps=[ftok], name="mm_out")
    xn2, r2 = _norm_fwd(h1, g_xattn, name="norm_xattn")
    gathered("qkv", xn2)
    ftok = relay("o", xn2)
    q = _mm(xn2, W["w_q"], out_dtype=BF16, deps=[ftok], name="mm_q")
    k = _mm(memn, W["w_k"], out_dtype=BF16, name="mm_k")
    vm = _mm(memn, W["w_v"], out_dtype=BF16, name="mm_v")
    o = _attn_fwd(q, k, vm, name="attn_fwd")
    gathered("o", o)
    h2 = _mm(o, W["w_o"], add=h1, name="mm_o")
    relay("gate", h2)
    xn3, r3 = _norm_fwd(h2, g_ffn, name="norm_ffn")
    gathered("gate", xn3)
    gp = _mm(xn3, W["w_gate"], tb=True, name="mm_gate")
    gathered("up", gp)
    up = _mm(xn3, W["w_up"], tb=True, name="mm_up")
    f = _ffn_act_fwd(gp, up, wf, name="ffn_act_fwd")
    gathered("down", f)
    h3 = _mm(f, W["w_down"], add=h2, name="mm_down")

    G = {}
    other_arr = 1 - c_arr

    def grad_half(k, which, **kw):
        xk, dyk = G[k]
        return _mm(xk, dyk, ta=True, half=("n" if lay[k].axis == 0 else "m", which), **kw)

    def siblings_start(tag, grp, after):
        lays = [lay[k] for k in grp]
        firsts = [grad_half(k, other_arr, deps=[after], name=f"mm_dw_{k}_sibling") for k in grp]
        plan = _sibling_plan(len(grp))
        ssem, rsem, srcs, lands, token = _exchange_start(
            firsts, [(l.half_shape(), F32) for l in lays], plan, after, name=f"rs_siblings_start_{tag}")
        return (tag, grp, lays, plan, ssem, rsem, srcs, lands), token

    def chips_start(state, after):
        tag, grp, lays, plan, ssem, rsem, srcs, lands = state
        _, lands = _exchange_wait(ssem, rsem, srcs, lands, plan, after, name=f"rs_siblings_wait_{tag}")
        sums = [grad_half(k, c_arr, add=ra, out_dtype=BF16, name=f"mm_dw_{k}_own") for k, ra in zip(grp, lands)]
        plan = _chips_plan(lays)
        ssem, rsem, sums, lands, token = _exchange_start(
            sums, [((N_CHIPS - 1,) + l.block_half_shape(), BF16) for l in lays], plan, after, name=f"rs_chips_start_{tag}")
        return (tag, grp, lays, plan, ssem, rsem, sums, lands), token

    def join_start(state, after):
        tag, grp, lays, plan, ssem, rsem, sums, lands = state
        sums, lands = _exchange_wait(ssem, rsem, sums, lands, plan, after, name=f"rs_chips_wait_{tag}")
        halves = [_sum_chips(cs, rb, pc_arr, l, name=f"sum_chips_{k}") for k, cs, rb, l in zip(grp, sums, lands, lays)]
        plan = _join_plan(lays)
        ssem, rsem, halves, _, token = _exchange_start(halves, [], plan, after[0], name=f"rs_join_start_{tag}")
        return (tag, grp, plan, ssem, rsem, halves), token

    def join_finish(state, after):
        tag, grp, plan, ssem, rsem, halves = state
        halves, _ = _exchange_wait(ssem, rsem, halves, [], plan, after, name=f"rs_join_wait_{tag}")
        return dict(zip(grp, halves))

    loss_rows, dh3, dh3b, dg_final = _loss_head(h3, g_final2, tgt, name="loss_head")
    df = _mm(dh3b, W["w_down"], tb=True, out_dtype=BF16, name="mm_d_f")
    G["w_down"] = (f, dh3b)
    rs_down, tok = siblings_start("down", ["w_down"], tok)
    dgp, dup, dwf = _ffn_act_bwd(gp, up, df, wf, name="ffn_act_bwd")
    G["w_gate"], G["w_up"] = (dgp, xn3), (dup, xn3)
    rs_ffn, tok = siblings_start("ffn", ["w_gate", "w_up"], tok)
    rs_down, tok = chips_start(rs_down, tok)
    dxn3 = _mm(dgp, W["w_gate"], deps=[tok], name="mm_dxn3_gate")
    dxn3 = _mm(dup, W["w_up"], add=dxn3, name="mm_dxn3_up")
    dh2, dh2b, dg_ffn = _norm_bwd(h2, r3, g_ffn, dxn3, dh3, name="norm_ffn_bwd")
    rs_ffn, tok = chips_start(rs_ffn, dh2b)
    do = _mm(dh2b, W["w_o"], tb=True, out_dtype=BF16, deps=[tok], name="mm_d_o")
    G["w_o"] = (o, dh2b)
    dq, dk, dvm = _attn_bwd(q, k, vm, do, name="attn_bwd")
    dkb, dvb = dk.astype(BF16), dvm.astype(BF16)
    G["w_q"], G["w_k"], G["w_v"] = (xn2, dq), (memn, dkb), (memn, dvb)
    rs_att, tok = siblings_start("att", ["w_o", "w_q", "w_k", "w_v"], tok)
    dxn2 = _mm(dq, W["w_q"], tb=True, deps=[tok], name="mm_dxn2")
    dmemn = _mm(dkb, W["w_k"], tb=True, name="mm_dmem_k")
    dmemn = _mm(dvb, W["w_v"], tb=True, add=dmemn, name="mm_dmem_v")
    dg_mem = _norm_bwd(mems, rm, g_mem2, dmemn, None, name="norm_mem_bwd", want_dh=False)
    dh1, dh1b, dg_xattn = _norm_bwd(h1, r2, g_xattn, dxn2, dh2, name="norm_xattn_bwd")
    rs_att, tok = chips_start(rs_att, dh1b)
    dmix = _mm(dh1b, W["w_out"], tb=True, deps=[tok], name="mm_d_mix")
    G["w_out"] = (mix, dh1b)
    rs_out, tok = siblings_start("out", ["w_out"], tok)
    dav, dag, dwa, dba, dlg, dlb = _mix_a_bwd(proj, u1, dmix, wa, ln_a_g, ln_a_b, cw=cw, name="mix_a_bwd")
    dbg, dcg, dbh, dwb = _mix_b_bwd(proj, dmix, wb, cw=cw, sw=cw, name="mix_b_bwd")
    dproj = jnp.concatenate([dav, dag, dbg, dcg, dbh], axis=1)
    G["w_in"] = (xn1, dproj)
    rs_in, tok = siblings_start("in", ["w_in"], tok)
    rs_out, tok = chips_start(rs_out, tok)
    dxn1 = _mm(dproj, W["w_in_lo"], tb=True, out_cols=(0, D), deps=[tok], name="mm_dxn1_lo")
    dxn1 = _mm(dproj, W["w_in_hi"], tb=True, out_cols=(D // 2, D), into=dxn1, name="mm_dxn1_hi")
    dx, _, dg_mix = _norm_bwd(xs, r1, g_mix, dxn1, dh1, name="norm_mix_bwd")
    rs_in, tok = chips_start(rs_in, dx)

    loss_part = jnp.sum(loss_rows).reshape(1, 1)
    small_parts = [dg_mix, dba, dlg, dlb, dg_xattn, dg_mem, dg_ffn, dg_final, dwa, dwb, dwf, loss_part]
    small_shapes = [a.shape for a in small_parts]
    reduced = _allgather_small(_pack(small_parts), name="allreduce_small", reduce=True)
    (sg_mix, sba, slg, slb, sg_xattn, sg_mem, sg_ffn, sg_final, swa, swb, swf, loss_sum) = _unpack(reduced, small_shapes)
    loss = loss_sum.reshape(())
    ga_w = lax.dynamic_slice(swa, (0, p_idx * cw_sh), (KA, cw_sh))
    gb_w = lax.dynamic_slice(swb, (0, p_idx * cw_sh), (KS, cw_sh))
    gf_w = lax.dynamic_slice(swf, (0, p_idx * fp), (KS, f_sh))

    weights = dict(g_mix=g_mix, w_in=w_in, conv_a_w=conv_a_w, conv_a_b=conv_a_b, ln_a_g=ln_a_g, ln_a_b=ln_a_b,
                   conv_b_w=conv_b_w, w_out=w_out, g_xattn=g_xattn, g_mem=g_mem, w_q=w_q, w_k=w_k, w_v=w_v, w_o=w_o,
                   g_ffn=g_ffn, w_gate=w_gate, w_up=w_up, conv_f_w=conv_f_w, w_down=w_down, g_final=g_final)
    m_in = dict(g_mix=m_g_mix, w_in=m_w_in, conv_a_w=m_conv_a_w, conv_a_b=m_conv_a_b, ln_a_g=m_ln_a_g, ln_a_b=m_ln_a_b,
                conv_b_w=m_conv_b_w, w_out=m_w_out, g_xattn=m_g_xattn, g_mem=m_g_mem, w_q=m_w_q, w_k=m_w_k, w_v=m_w_v,
                w_o=m_w_o, g_ffn=m_g_ffn, w_gate=m_w_gate, w_up=m_w_up, conv_f_w=m_conv_f_w, w_down=m_w_down,
                g_final=m_g_final)
    v_in = dict(g_mix=v_g_mix, w_in=v_w_in, conv_a_w=v_conv_a_w, conv_a_b=v_conv_a_b, ln_a_g=v_ln_a_g, ln_a_b=v_ln_a_b,
                conv_b_w=v_conv_b_w, w_out=v_w_out, g_xattn=v_g_xattn, g_mem=v_g_mem, w_q=v_w_q, w_k=v_w_k, w_v=v_w_v,
                w_o=v_w_o, g_ffn=v_g_ffn, w_gate=v_w_gate, w_up=v_w_up, conv_f_w=v_conv_f_w, w_down=v_w_down,
                g_final=v_g_final)
    order = list(weights)
    grads = dict(g_mix=sg_mix, conv_a_w=ga_w, conv_a_b=sba, ln_a_g=slg, ln_a_b=slb, conv_b_w=gb_w, g_xattn=sg_xattn,
                 g_mem=sg_mem, g_ffn=sg_ffn, conv_f_w=gf_w, g_final=sg_final)
    grads = {k: g.reshape(weights[k].shape) for k, g in grads.items()}

    delta, new_m, new_v = {}, {}, {}
    small = [k for k in order if k not in big]
    small_shapes = [weights[k].shape for k in small]
    packed = [_pack([src[k] for k in small]) for src in (weights, grads, m_in, v_in)]
    d_, m_, v_ = _adamw(*packed, name="adamw_small")
    for k, dd, mm_, vv in zip(small, _unpack(d_, small_shapes), _unpack(m_, small_shapes), _unpack(v_, small_shapes)):
        delta[k], new_m[k], new_v[k] = dd, mm_, vv

    transposed = ("w_gate", "w_up")

    def update(shard_grads):
        last = None
        for k, g in shard_grads.items():
            view = t_ if k in transposed else (lambda a: a[0])
            back = (lambda a: jnp.swapaxes(a, 0, 1)[None]) if k in transposed else (lambda a: a[None])
            padded = g.shape != view(weights[k]).shape
            outs = _adamw(view(weights[k]), g, view(m_in[k]), view(v_in[k]), emit_grad=padded, name=f"adamw_{k}")
            delta[k], new_m[k], new_v[k] = back(outs[0]), back(outs[1]), back(outs[2])
            grads[k] = back(outs[3] if padded else g)
            last = outs[0]
        return last

    after = (d_, tok)
    joining = None
    for state in (rs_down, rs_ffn, rs_att, rs_out, rs_in):
        started, token = join_start(state, after)
        after = (token,)
        if joining is not None:
            after = (update(join_finish(joining, token)), token)
        joining = started
    update(join_finish(joining, after[0]))

    return (loss, dx[None], *[grads[k] for k in order], *[delta[k] for k in order],
            *[new_m[k] for k in order], *[new_v[k] for k in order])
```

```python
import functools

import jax
import jax.numpy as jnp
from jax import lax
from jax.experimental import pallas as pl
from jax.experimental.pallas import tpu as pltpu

F32 = jnp.float32
BF16 = jnp.bfloat16
EPS = 1e-6
N_HEADS = 4
GROUP = 128
KA = 31
KS = 3
KA_ROWS = 32
KS_ROWS = 8
HALO_A = 32
HALO_S = 8
CHUNK_ROWS = 16
CHUNK_COLS = 256
N_CHIPS = 4
N_DEV = 8
LANES = 128
VMEM_LIMIT = 56 * 1024 * 1024
MM_WIDE_TILE_ELEMS = 4096 * 512
ADAMW_BLOCK_BYTES = 1 << 20
MESH = pl.DeviceIdType.MESH
ANY = pl.BlockSpec(memory_space=pl.ANY)

ADAM_LR = 0.001
ADAM_B1 = 0.9
ADAM_B2 = 0.999
ADAM_EPS = 1e-08
ADAM_WD = 0.01
ADAM_STEP = 10


def _pick(dim, pref, mult=LANES):
    if dim <= pref:
        return dim
    t = (pref // mult) * mult
    while t >= mult:
        if dim % t == 0:
            return t
        t -= mult
    return dim


def _round_up(n, m):
    return ((n + m - 1) // m) * m


def _params(*sem):
    return pltpu.CompilerParams(dimension_semantics=sem, vmem_limit_bytes=VMEM_LIMIT)


def _sigmoid(x):
    return jax.nn.sigmoid(x)


def _mm(a, b, *, name, ta=False, tb=False, out_dtype=F32, add=None, deps=(), half=None, a_k=None, out_cols=None,
        into=None, tm=1024, tn=512, tk=5632):
    if ta:
        K, M = a.shape
    else:
        M, K = a.shape
    if tb:
        N, K2 = b.shape
    else:
        K2, N = b.shape
    k0 = 0
    if a_k is not None:
        k0, K = a_k
    assert K == K2, (a.shape, b.shape)
    half_dim, half_sel = half if half is not None else (None, None)
    if half_dim == "m":
        M //= 2
    elif half_dim == "n":
        N //= 2
    tm, tk = _pick(M, tm), _pick(K, tk)
    nk = K // tk
    tn = _pick(N, 2 * tn if nk == 1 and K * tn <= MM_WIDE_TILE_ELEMS else tn)
    mi, nj = M // tm, N // tn
    n0, n_total = out_cols if out_cols is not None else (0, N)
    assert k0 % tk == 0 and n0 % tn == 0
    koff, noff = k0 // tk, n0 // tn

    def a_map(i, j, k, *s):
        i = i + s[0][0] * mi if half_dim == "m" else i
        return (k + koff, i) if ta else (i, k + koff)

    def b_map(i, j, k, *s):
        j = j + s[0][0] * nj if half_dim == "n" else j
        return (j, k) if tb else (k, j)

    a_spec = pl.BlockSpec((tk, tm) if ta else (tm, tk), a_map)
    b_spec = pl.BlockSpec((tn, tk) if tb else (tk, tn), b_map)
    o_spec = pl.BlockSpec((tm, tn), lambda i, j, k, *s: (i, j + noff))
    add_spec = pl.BlockSpec((tm, tn), lambda i, j, k, *s: (i, j))
    dims = (((0,) if ta else (1,), (1,) if tb else (0,)), ((), ()))
    has_add = add is not None
    n_pre = 0 if half is None else 1
    n_after = len(deps) + (0 if into is None else 1)

    def body(*refs):
        refs = refs[n_pre:]
        a_ref, b_ref = refs[:2]
        add_ref = refs[2] if has_add else None
        o_ref = refs[(3 if has_add else 2) + n_after]
        acc_ref = refs[-1] if nk > 1 else None
        k = pl.program_id(2)
        part = lax.dot_general(a_ref[...], b_ref[...], dims, preferred_element_type=F32)

        def finish(r):
            if add_ref is not None:
                r = add_ref[...] + r
            o_ref[...] = r.astype(out_dtype)

        if nk == 1:
            finish(part)
        else:
            @pl.when(k == 0)
            def _():
                acc_ref[...] = part

            @pl.when(jnp.logical_and(k > 0, k < nk - 1))
            def _():
                acc_ref[...] += part

            @pl.when(k == nk - 1)
            def _():
                finish(acc_ref[...] + part)

    in_specs = [a_spec, b_spec] + ([add_spec] if has_add else []) + [ANY] * n_after
    args = (a, b) + ((add,) if has_add else ()) + tuple(deps) + ((into,) if into is not None else ())
    aliases = {n_pre + len(args) - 1: 0} if into is not None else {}
    scratch = [pltpu.VMEM((tm, tn), F32)] if nk > 1 else []
    grid = (mi, nj, nk)
    if half is None:
        grid_spec = dict(grid=grid, in_specs=in_specs, out_specs=o_spec, scratch_shapes=scratch)
    else:
        args = (half_sel,) + args
        grid_spec = dict(grid_spec=pltpu.PrefetchScalarGridSpec(
            num_scalar_prefetch=1, grid=grid, in_specs=in_specs, out_specs=o_spec, scratch_shapes=scratch))
    return pl.pallas_call(
        body,
        name=name,
        out_shape=jax.ShapeDtypeStruct((M, n_total), out_dtype),
        input_output_aliases=aliases,
        compiler_params=_params("parallel", "parallel", "arbitrary"),
        **grid_spec,
    )(*args)


def _norm_fwd(h, g, *, name):
    T, D = h.shape
    R = _pick(T, 256, 8)

    def body(h_ref, g_ref, xn_ref, r_ref):
        x = h_ref[...]
        r = lax.rsqrt(jnp.mean(x * x, axis=-1, keepdims=True) + EPS)
        xn_ref[...] = ((x * r) * g_ref[...]).astype(BF16)
        r_ref[...] = r

    return pl.pallas_call(
        body,
        name=name,
        out_shape=(jax.ShapeDtypeStruct((T, D), BF16), jax.ShapeDtypeStruct((T, 1), F32)),
        grid=(T // R,),
        in_specs=[pl.BlockSpec((R, D), lambda i: (i, 0)), pl.BlockSpec((1, D), lambda i: (0, 0))],
        out_specs=(pl.BlockSpec((R, D), lambda i: (i, 0)), pl.BlockSpec((R, 1), lambda i: (i, 0))),
        compiler_params=_params("parallel"),
    )(h, g)


def _norm_bwd(h, r, g, dxn, dres, *, name, want_dh=True):
    T, D = h.shape
    R = _pick(T, 128, 8)
    has_res = dres is not None

    def body(*refs):
        h_ref, r_ref, g_ref, dxn_ref = refs[:4]
        pos = 4
        dres_ref = None
        if has_res:
            dres_ref = refs[pos]
            pos += 1
        if want_dh:
            dh_ref, dhb_ref, dg_ref = refs[pos:pos + 3]
        else:
            dg_ref = refs[pos]
        i = pl.program_id(0)
        rr = r_ref[...]
        hn = h_ref[...] * rr
        d = dxn_ref[...].astype(F32)
        gd = d * g_ref[...]
        part = jnp.sum(d * hn, axis=0, keepdims=True)

        @pl.when(i == 0)
        def _():
            dg_ref[...] = part

        @pl.when(i > 0)
        def _():
            dg_ref[...] += part

        if want_dh:
            dh = rr * (gd - hn * jnp.mean(gd * hn, axis=-1, keepdims=True))
            if dres_ref is not None:
                dh = dres_ref[...] + dh
            dh_ref[...] = dh
            dhb_ref[...] = dh.astype(BF16)

    row = pl.BlockSpec((R, D), lambda i: (i, 0))
    vec = pl.BlockSpec((1, D), lambda i: (0, 0))
    in_specs = [row, pl.BlockSpec((R, 1), lambda i: (i, 0)), vec, row] + ([row] if has_res else [])
    args = (h, r, g, dxn) + ((dres,) if has_res else ())
    if want_dh:
        out_shape = (jax.ShapeDtypeStruct((T, D), F32), jax.ShapeDtypeStruct((T, D), BF16), jax.ShapeDtypeStruct((1, D), F32))
        out_specs = (row, row, vec)
    else:
        out_shape = jax.ShapeDtypeStruct((1, D), F32)
        out_specs = vec
    return pl.pallas_call(
        body, name=name, out_shape=out_shape, grid=(T // R,), in_specs=in_specs, out_specs=out_specs,
        compiler_params=_params("arbitrary"),
    )(*args)


def _loss_head(h, g, tgt, *, name):
    T, D = h.shape
    R = _pick(T, 128, 8)

    def body(h_ref, g_ref, t_ref, loss_ref, dh_ref, dhb_ref, dg_ref):
        i = pl.program_id(0)
        x = h_ref[...]
        gg = g_ref[...]
        r = lax.rsqrt(jnp.mean(x * x, axis=-1, keepdims=True) + EPS)
        hn = x * r
        e = hn * gg - t_ref[...]
        loss_ref[...] = 0.5 * jnp.mean(e * e, axis=-1, keepdims=True)
        dy = e * (1.0 / D)
        gd = dy * gg
        dh = r * (gd - hn * jnp.mean(gd * hn, axis=-1, keepdims=True))
        dh_ref[...] = dh
        dhb_ref[...] = dh.astype(BF16)
        part = jnp.sum(dy * hn, axis=0, keepdims=True)

        @pl.when(i == 0)
        def _():
            dg_ref[...] = part

        @pl.when(i > 0)
        def _():
            dg_ref[...] += part

    row = pl.BlockSpec((R, D), lambda i: (i, 0))
    vec = pl.BlockSpec((1, D), lambda i: (0, 0))
    return pl.pallas_call(
        body,
        name=name,
        out_shape=(jax.ShapeDtypeStruct((T, 1), F32), jax.ShapeDtypeStruct((T, D), F32),
                   jax.ShapeDtypeStruct((T, D), BF16), jax.ShapeDtypeStruct((1, D), F32)),
        grid=(T // R,),
        in_specs=[row, vec, row],
        out_specs=(pl.BlockSpec((R, 1), lambda i: (i, 0)), row, row, vec),
        compiler_params=_params("arbitrary"),
    )(h, g, tgt)


_NT = (((1,), (1,)), ((), ()))
_TN = (((0,), (0,)), ((), ()))
_NN = (((1,), (0,)), ((), ()))


def _softmax_rows(s):
    m = jnp.max(s, axis=-1, keepdims=True)
    e = jnp.exp(s - m)
    return e / jnp.sum(e, axis=-1, keepdims=True)


def _attn_fwd(q, k, v, *, name):
    T, D = q.shape
    ML = k.shape[0]
    dh = D // N_HEADS
    scale = dh ** -0.5
    R = _pick(T, 512, 16)

    def body(q_ref, k_ref, v_ref, o_ref):
        s = lax.dot_general(q_ref[...], k_ref[...], _NT, preferred_element_type=F32) * scale
        p = _softmax_rows(s)
        o_ref[...] = lax.dot_general(p.astype(BF16), v_ref[...], _NN, preferred_element_type=F32).astype(BF16)

    qs = pl.BlockSpec((R, dh), lambda i, h: (i, h))
    ks = pl.BlockSpec((ML, dh), lambda i, h: (0, h))
    return pl.pallas_call(
        body, name=name, out_shape=jax.ShapeDtypeStruct((T, D), BF16), grid=(T // R, N_HEADS),
        in_specs=[qs, ks, ks], out_specs=qs, compiler_params=_params("parallel", "parallel"),
    )(q, k, v)


def _attn_bwd(q, k, v, do, *, name):
    T, D = q.shape
    ML = k.shape[0]
    dh = D // N_HEADS
    scale = dh ** -0.5
    R = _pick(T, 512, 16)

    def body(q_ref, k_ref, v_ref, do_ref, dq_ref, dk_ref, dv_ref):
        i = pl.program_id(1)
        qq, kk, vv, dd = q_ref[...], k_ref[...], v_ref[...], do_ref[...]
        s = lax.dot_general(qq, kk, _NT, preferred_element_type=F32) * scale
        p = _softmax_rows(s)
        dp = lax.dot_general(dd, vv, _NT, preferred_element_type=F32)
        dv_part = lax.dot_general(p.astype(BF16), dd, _TN, preferred_element_type=F32)
        ds = (p * (dp - jnp.sum(p * dp, axis=-1, keepdims=True)) * scale).astype(BF16)
        dq_ref[...] = lax.dot_general(ds, kk, _NN, preferred_element_type=F32).astype(BF16)
        dk_part = lax.dot_general(ds, qq, _TN, preferred_element_type=F32)

        @pl.when(i == 0)
        def _():
            dk_ref[...] = dk_part
            dv_ref[...] = dv_part

        @pl.when(i > 0)
        def _():
            dk_ref[...] += dk_part
            dv_ref[...] += dv_part

    qs = pl.BlockSpec((R, dh), lambda h, i: (i, h))
    ks = pl.BlockSpec((ML, dh), lambda h, i: (0, h))
    return pl.pallas_call(
        body,
        name=name,
        out_shape=(jax.ShapeDtypeStruct((T, D), BF16), jax.ShapeDtypeStruct((ML, D), F32), jax.ShapeDtypeStruct((ML, D), F32)),
        grid=(N_HEADS, T // R),
        in_specs=[qs, ks, ks, qs],
        out_specs=(qs, ks, ks),
        compiler_params=_params("parallel", "arbitrary"),
    )(q, k, v, do)


def _prev_halo(R, halo, CT, col):
    per = R // halo
    return lambda i, j: (jnp.maximum(i * per - 1, 0), col(j))


def _next_halo(R, halo, T, col):
    per = R // halo
    last = T // halo - 1
    return lambda i, j: (jnp.minimum((i + 1) * per, last), col(j))


def _swap(f):
    return lambda j, i: f(i, j)


def _ffn_act_fwd(gp, up, wf, *, name):
    T, F = gp.shape
    R = _pick(T, 256, CHUNK_ROWS)
    CT = _pick(F, 1024, CHUNK_COLS)
    H = HALO_S
    RC, CC = CHUNK_ROWS, min(CHUNK_COLS, CT)

    def body(g_ref, gh_ref, u_ref, w_ref, f_ref, ext):
        i = pl.program_id(0)
        ext[pl.ds(0, H), :] = jnp.where(i > 0, gh_ref[...], 0.0)
        ext[pl.ds(H, R), :] = g_ref[...]
        for c0 in range(0, CT, CC):
            cols = pl.ds(c0, CC)
            w0, w1, w2 = [jnp.broadcast_to(w_ref[pl.ds(k, 1), cols], (RC, CC)) for k in range(KS)]
            for r0 in range(0, R, RC):
                g = ext[pl.ds(r0 + H - 2, RC), cols] * w0
                g = g + ext[pl.ds(r0 + H - 1, RC), cols] * w1
                g = g + ext[pl.ds(r0 + H, RC), cols] * w2
                f_ref[pl.ds(r0, RC), cols] = (g * _sigmoid(g) * u_ref[pl.ds(r0, RC), cols]).astype(BF16)

    main = pl.BlockSpec((R, CT), lambda i, j: (i, j))
    return pl.pallas_call(
        body,
        name=name,
        out_shape=jax.ShapeDtypeStruct((T, F), BF16),
        grid=(T // R, F // CT),
        in_specs=[main, pl.BlockSpec((H, CT), _prev_halo(R, H, CT, lambda j: j)), main,
                  pl.BlockSpec((KS_ROWS, CT), lambda i, j: (0, j))],
        out_specs=main,
        scratch_shapes=[pltpu.VMEM((R + H, CT), F32)],
        compiler_params=_params("parallel", "parallel"),
    )(gp, gp, up, wf)


def _ffn_act_bwd(gp, up, df, wf, *, name):
    T, F = gp.shape
    R = _pick(T, 256, CHUNK_ROWS)
    CT = _pick(F, 512, CHUNK_COLS)
    H = HALO_S
    HB = 16
    n_t = T // R
    RC, CC = CHUNK_ROWS, min(CHUNK_COLS, CT)

    def body(g_ref, gp_ref, gn_ref, u_ref, un_ref, d_ref, dn_ref, w_ref, dg_out, du_out, dw_ref, ext, dge):
        i = pl.program_id(1)
        last = i == n_t - 1
        ext[pl.ds(0, H), :] = jnp.where(i > 0, gp_ref[...], 0.0)
        ext[pl.ds(H, R), :] = g_ref[...]
        ext[pl.ds(H + R, H), :] = gn_ref[...]

        def dact(g, u, d):
            sg = _sigmoid(g)
            return d * u * (sg * (1.0 + g * (1.0 - sg))), d * (g * sg)

        dw_rows = [[] for _ in range(KS)]
        for c0 in range(0, CT, CC):
            cols = pl.ds(c0, CC)
            w = [jnp.broadcast_to(w_ref[pl.ds(k, 1), cols], (RC, CC)) for k in range(KS)]
            acc = [jnp.zeros((RC, CC), F32) for _ in range(KS)]
            for r0 in range(0, R, RC):
                rows = pl.ds(r0, RC)
                taps = [ext[pl.ds(r0 + H - 2 + k, RC), cols] for k in range(KS)]
                g = taps[0] * w[0] + taps[1] * w[1] + taps[2] * w[2]
                dg, du = dact(g, u_ref[rows, cols], d_ref[rows, cols].astype(F32))
                du_out[rows, cols] = du.astype(BF16)
                dge[rows, cols] = dg
                acc = [a + dg * t for a, t in zip(acc, taps)]
            taps = [ext[pl.ds(R + H - 2 + k, H), cols] for k in range(KS)]
            g = taps[0] * w[0][:H] + taps[1] * w[1][:H] + taps[2] * w[2][:H]
            d_next = jnp.where(last, 0.0, dn_ref[pl.ds(0, H), cols].astype(F32))
            dg_next, _ = dact(g, un_ref[:, cols], d_next)
            dge[pl.ds(R, H), cols] = jnp.where(last, 0.0, dg_next)
            for r0 in range(0, R, RC):
                dgp = (dge[pl.ds(r0, RC), cols] * w[2] + dge[pl.ds(r0 + 1, RC), cols] * w[1]
                       + dge[pl.ds(r0 + 2, RC), cols] * w[0])
                dg_out[pl.ds(r0, RC), cols] = dgp.astype(BF16)
            for k in range(KS):
                dw_rows[k].append(jnp.sum(acc[k], axis=0, keepdims=True))
        rows = [jnp.concatenate(r, axis=1) if len(r) > 1 else r[0] for r in dw_rows]
        rows.append(jnp.zeros((KS_ROWS - KS, CT), F32))
        _acc_rows_block(dw_ref, i, rows)

    col = lambda j: j
    main = pl.BlockSpec((R, CT), lambda j, i: (i, j))
    prev8 = pl.BlockSpec((H, CT), _swap(_prev_halo(R, H, CT, col)))
    next8 = pl.BlockSpec((H, CT), _swap(_next_halo(R, H, T, col)))
    next16 = pl.BlockSpec((HB, CT), _swap(_next_halo(R, HB, T, col)))
    wspec = pl.BlockSpec((KS_ROWS, CT), lambda j, i: (0, j))
    return pl.pallas_call(
        body,
        name=name,
        out_shape=(jax.ShapeDtypeStruct((T, F), BF16), jax.ShapeDtypeStruct((T, F), BF16), jax.ShapeDtypeStruct((KS_ROWS, F), F32)),
        grid=(F // CT, n_t),
        in_specs=[main, prev8, next8, main, next8, main, next16, wspec],
        out_specs=(main, main, wspec),
        scratch_shapes=[pltpu.VMEM((R + 2 * H, CT), F32), pltpu.VMEM((R + H, CT), F32)],
        compiler_params=_params("parallel", "arbitrary"),
    )(gp, gp, gp, up, up, df, df, wf)


def _acc_rows_block(ref, i, rows):
    *singles, pad = rows

    @pl.when(i == 0)
    def _():
        for k, row in enumerate(singles):
            ref[pl.ds(k, 1), :] = row
        ref[pl.ds(len(singles), pad.shape[0]), :] = pad

    @pl.when(i > 0)
    def _():
        for k, row in enumerate(singles):
            ref[pl.ds(k, 1), :] += row


def _group_stats(x):
    mu = jnp.mean(x, axis=-1, keepdims=True)
    xc = x - mu
    var = jnp.mean(xc * xc, axis=-1, keepdims=True)
    return xc, lax.rsqrt(var + EPS)


def _shifted_rows(rolled, x, cols):
    n = x.shape[0]
    for b in range(1, 8):
        rolled[b - 1, :, cols] = pltpu.roll(x, n - b, axis=0)


def _window(src, rolled, off, rows, cols):
    b = off % 8
    if b == 0:
        return src[pl.ds(off, rows), cols]
    return rolled[b - 1, pl.ds(off - b, rows), cols]


def _mix_a_fwd(proj, wa, ba, lg, lb, *, cw, name, deps=()):
    T = proj.shape[0]
    R = _pick(T, 128, HALO_A)
    CT = _pick(cw, 256)
    nc = cw // CT
    H = HALO_A

    def body(av_ref, ag_ref, avh_ref, agh_ref, w_ref, b_ref, lg_ref, lb_ref, *rest):
        u3_ref, u1_ref, ext, rolled = rest[len(deps):]
        i = pl.program_id(0)
        for gi in range(CT // GROUP):
            cols = pl.ds(gi * GROUP, GROUP)
            ext[pl.ds(0, H), cols] = jnp.where(i > 0, avh_ref[:, cols] * _sigmoid(agh_ref[:, cols]), 0.0)
            ext[pl.ds(H, R), cols] = av_ref[:, cols] * _sigmoid(ag_ref[:, cols])
            _shifted_rows(rolled, ext[:, cols], cols)
            acc = _window(ext, rolled, H - (KA - 1), R, cols) * w_ref[pl.ds(0, 1), cols]
            for k in range(1, KA):
                acc = acc + _window(ext, rolled, H - (KA - 1) + k, R, cols) * w_ref[pl.ds(k, 1), cols]
            u1 = acc + b_ref[:, cols]
            u1_ref[:, cols] = u1
            xc, rstd = _group_stats(u1)
            u2 = (xc * rstd) * lg_ref[:, cols] + lb_ref[:, cols]
            u3_ref[:, cols] = (u2 * _sigmoid(u2)).astype(BF16)

    main_v = pl.BlockSpec((R, CT), lambda i, j: (i, j))
    main_g = pl.BlockSpec((R, CT), lambda i, j: (i, j + nc))
    halo_v = pl.BlockSpec((H, CT), _prev_halo(R, H, CT, lambda j: j))
    halo_g = pl.BlockSpec((H, CT), _prev_halo(R, H, CT, lambda j: j + nc))
    vec = pl.BlockSpec((1, CT), lambda i, j: (0, j))
    out = pl.BlockSpec((R, CT), lambda i, j: (i, j))
    return pl.pallas_call(
        body,
        name=name,
        out_shape=(jax.ShapeDtypeStruct((T, cw), BF16), jax.ShapeDtypeStruct((T, cw), F32)),
        grid=(T // R, nc),
        in_specs=[main_v, main_g, halo_v, halo_g, pl.BlockSpec((KA_ROWS, CT), lambda i, j: (0, j)), vec, vec, vec]
        + [ANY] * len(deps),
        out_specs=(out, out),
        scratch_shapes=[pltpu.VMEM((R + H, CT), F32), pltpu.VMEM((7, R + H, CT), F32)],
        compiler_params=_params("parallel", "parallel"),
    )(proj, proj, proj, proj, wa, ba, lg, lb, *deps)


def _mix_a_bwd(proj, u1, dmix, wa, lg, lb, *, cw, name):
    T = proj.shape[0]
    R = _pick(T, 128, HALO_A)
    CT = _pick(cw, 256)
    nc = cw // CT
    H = HALO_A
    n_t = T // R
    NG = CT // GROUP

    def body(av_ref, ag_ref, avh_ref, agh_ref, u1_ref, u1n_ref, d3_ref, d3n_ref, w_ref, lg_ref, lb_ref,
             dav_ref, dag_ref, dw_ref, db_ref, dlg_ref, dlb_ref, ext, d1, ext_rolled, d1_rolled):
        i = pl.program_id(1)
        last = i == n_t - 1

        def ln_bwd(u1, d3, cols):
            xc, rstd = _group_stats(u1)
            xh = xc * rstd
            g = lg_ref[:, cols]
            u2 = xh * g + lb_ref[:, cols]
            sg = _sigmoid(u2)
            du2 = d3 * (sg * (1.0 + u2 * (1.0 - sg)))
            dxh = du2 * g
            du1 = rstd * (dxh - jnp.mean(dxh, axis=-1, keepdims=True) - xh * jnp.mean(dxh * xh, axis=-1, keepdims=True))
            return du1, du2 * xh, du2

        def lanes(parts):
            return jnp.concatenate(parts, axis=1) if len(parts) > 1 else parts[0]

        dlg_parts, dlb_parts, db_parts = [], [], []
        dw_parts = [[] for _ in range(KA)]
        for gi in range(NG):
            cols = pl.ds(gi * GROUP, GROUP)
            du1, dlg, dlb = ln_bwd(u1_ref[:, cols], d3_ref[:, cols], cols)
            d1[pl.ds(0, R), cols] = du1
            dlg_parts.append(jnp.sum(dlg, axis=0, keepdims=True))
            dlb_parts.append(jnp.sum(dlb, axis=0, keepdims=True))
            db_parts.append(jnp.sum(du1, axis=0, keepdims=True))
            du1n, _, _ = ln_bwd(u1n_ref[:, cols], jnp.where(last, 0.0, d3n_ref[:, cols]), cols)
            d1[pl.ds(R, H), cols] = jnp.where(last, 0.0, du1n)

            _shifted_rows(d1_rolled, d1[:, cols], cols)
            du0 = _window(d1, d1_rolled, KA - 1, R, cols) * w_ref[pl.ds(0, 1), cols]
            for k in range(1, KA):
                du0 = du0 + _window(d1, d1_rolled, KA - 1 - k, R, cols) * w_ref[pl.ds(k, 1), cols]
            av = av_ref[:, cols]
            sg = _sigmoid(ag_ref[:, cols])
            dav_ref[:, cols] = (du0 * sg).astype(BF16)
            dag_ref[:, cols] = (du0 * av * sg * (1.0 - sg)).astype(BF16)

            ext[pl.ds(0, H), cols] = jnp.where(i > 0, avh_ref[:, cols] * _sigmoid(agh_ref[:, cols]), 0.0)
            ext[pl.ds(H, R), cols] = av * sg
            _shifted_rows(ext_rolled, ext[:, cols], cols)
            d1_main = d1[pl.ds(0, R), cols]
            for k in range(KA):
                window = _window(ext, ext_rolled, H - (KA - 1) + k, R, cols)
                dw_parts[k].append(jnp.sum(d1_main * window, axis=0, keepdims=True))

        dlg_row, dlb_row, db_row = lanes(dlg_parts), lanes(dlb_parts), lanes(db_parts)

        @pl.when(i == 0)
        def _():
            dlg_ref[...] = dlg_row
            dlb_ref[...] = dlb_row
            db_ref[...] = db_row

        @pl.when(i > 0)
        def _():
            dlg_ref[...] += dlg_row
            dlb_ref[...] += dlb_row
            db_ref[...] += db_row

        rows = [lanes(p) for p in dw_parts]
        rows.append(jnp.zeros((KA_ROWS - KA, CT), F32))
        _acc_rows_block(dw_ref, i, rows)

    cv = lambda j: j
    cg = lambda j: j + nc
    main_v = pl.BlockSpec((R, CT), lambda j, i: (i, j))
    main_g = pl.BlockSpec((R, CT), lambda j, i: (i, j + nc))
    prev_v = pl.BlockSpec((H, CT), _swap(_prev_halo(R, H, CT, cv)))
    prev_g = pl.BlockSpec((H, CT), _swap(_prev_halo(R, H, CT, cg)))
    nxt = pl.BlockSpec((H, CT), _swap(_next_halo(R, H, T, cv)))
    wspec = pl.BlockSpec((KA_ROWS, CT), lambda j, i: (0, j))
    vec = pl.BlockSpec((1, CT), lambda j, i: (0, j))
    vshape = jax.ShapeDtypeStruct((1, cw), F32)
    return pl.pallas_call(
        body,
        name=name,
        out_shape=(jax.ShapeDtypeStruct((T, cw), BF16), jax.ShapeDtypeStruct((T, cw), BF16),
                   jax.ShapeDtypeStruct((KA_ROWS, cw), F32), vshape, vshape, vshape),
        grid=(nc, n_t),
        in_specs=[main_v, main_g, prev_v, prev_g, main_v, nxt, main_v, nxt, wspec, vec, vec],
        out_specs=(main_v, main_v, wspec, vec, vec, vec),
        scratch_shapes=[pltpu.VMEM((R + H, CT), F32), pltpu.VMEM((R + H, CT), F32),
                        pltpu.VMEM((7, R + H, CT), F32), pltpu.VMEM((7, R + H, CT), F32)],
        compiler_params=_params("parallel", "arbitrary"),
    )(proj, proj, proj, proj, u1, u1, dmix, dmix, wa, lg, lb)


def _mix_b_fwd(proj, wb, *, cw, sw, name):
    T = proj.shape[0]
    R = _pick(T, 256, 8)
    CT = _pick(sw, 512)
    nb, nc, nh = (2 * cw) // CT, (2 * cw + sw) // CT, (2 * cw + 2 * sw) // CT
    H = HALO_S

    def body(b_ref, c_ref, h_ref, ch_ref, hh_ref, w_ref, v_ref, ext):
        i = pl.program_id(0)
        ext[pl.ds(0, H), :] = jnp.where(i > 0, ch_ref[...] * hh_ref[...], 0.0)
        ext[pl.ds(H, R), :] = c_ref[...] * h_ref[...]
        w = w_ref[...]
        zc = ext[pl.ds(H - 2, R), :] * w[0:1, :]
        zc = zc + ext[pl.ds(H - 1, R), :] * w[1:2, :]
        zc = zc + ext[pl.ds(H, R), :] * w[2:3, :]
        v_ref[...] = (b_ref[...] * zc).astype(BF16)

    def main(off):
        return pl.BlockSpec((R, CT), lambda i, j: (i, j + off))

    def prev(off):
        return pl.BlockSpec((H, CT), _prev_halo(R, H, CT, lambda j: j + off))

    return pl.pallas_call(
        body,
        name=name,
        out_shape=jax.ShapeDtypeStruct((T, sw), BF16),
        grid=(T // R, sw // CT),
        in_specs=[main(nb), main(nc), main(nh), prev(nc), prev(nh), pl.BlockSpec((KS_ROWS, CT), lambda i, j: (0, j))],
        out_specs=pl.BlockSpec((R, CT), lambda i, j: (i, j)),
        scratch_shapes=[pltpu.VMEM((R + H, CT), F32)],
        compiler_params=_params("parallel", "parallel"),
    )(proj, proj, proj, proj, proj, wb)


def _mix_b_bwd(proj, dmix, wb, *, cw, sw, name):
    T = proj.shape[0]
    R = _pick(T, 256, 8)
    CT = _pick(sw, 512)
    nb, nc, nh = (2 * cw) // CT, (2 * cw + sw) // CT, (2 * cw + 2 * sw) // CT
    nd = cw // CT
    H = HALO_S
    n_t = T // R

    def body(b_ref, bn_ref, c_ref, cp_ref, cn_ref, h_ref, hp_ref, hn_ref, d_ref, dn_ref, w_ref,
             db_ref, dc_ref, dhh_ref, dw_ref, ext, dze):
        i = pl.program_id(1)
        last = i == n_t - 1
        cc, hh = c_ref[...], h_ref[...]
        ext[pl.ds(0, H), :] = jnp.where(i > 0, cp_ref[...] * hp_ref[...], 0.0)
        ext[pl.ds(H, R), :] = cc * hh
        ext[pl.ds(H + R, H), :] = cn_ref[...] * hn_ref[...]
        w = w_ref[...]
        w0, w1, w2 = w[0:1, :], w[1:2, :], w[2:3, :]

        def conv(start, n):
            z = ext[pl.ds(start + H - 2, n), :] * w0
            z = z + ext[pl.ds(start + H - 1, n), :] * w1
            return z + ext[pl.ds(start + H, n), :] * w2

        d_main = d_ref[...]
        db_ref[...] = (d_main * conv(0, R)).astype(BF16)
        dzc_main = d_main * b_ref[...]
        dze[pl.ds(0, R), :] = dzc_main
        dze[pl.ds(R, H), :] = jnp.where(last, 0.0, dn_ref[...] * bn_ref[...])
        dz = dze[pl.ds(0, R), :] * w2 + dze[pl.ds(1, R), :] * w1 + dze[pl.ds(2, R), :] * w0
        dc_ref[...] = (dz * hh).astype(BF16)
        dhh_ref[...] = (dz * cc).astype(BF16)
        rows = [jnp.sum(dzc_main * ext[pl.ds(H - 2 + k, R), :], axis=0, keepdims=True) for k in range(KS)]
        rows.append(jnp.zeros((KS_ROWS - KS, CT), F32))
        _acc_rows_block(dw_ref, i, rows)

    def main(off):
        return pl.BlockSpec((R, CT), lambda j, i: (i, j + off))

    def prev(off):
        return pl.BlockSpec((H, CT), _swap(_prev_halo(R, H, CT, lambda j: j + off)))

    def nxt(off):
        return pl.BlockSpec((H, CT), _swap(_next_halo(R, H, T, lambda j: j + off)))

    out = pl.BlockSpec((R, CT), lambda j, i: (i, j))
    wspec = pl.BlockSpec((KS_ROWS, CT), lambda j, i: (0, j))
    act = jax.ShapeDtypeStruct((T, sw), BF16)
    return pl.pallas_call(
        body,
        name=name,
        out_shape=(act, act, act, jax.ShapeDtypeStruct((KS_ROWS, sw), F32)),
        grid=(sw // CT, n_t),
        in_specs=[main(nb), nxt(nb), main(nc), prev(nc), nxt(nc), main(nh), prev(nh), nxt(nh), main(nd), nxt(nd), wspec],
        out_specs=(out, out, out, wspec),
        scratch_shapes=[pltpu.VMEM((R + 2 * H, CT), F32), pltpu.VMEM((R + H, CT), F32)],
        compiler_params=_params("parallel", "arbitrary"),
    )(proj, proj, proj, proj, proj, proj, proj, proj, dmix, dmix, wb)


def _adamw(w, g, m, v, *, name, emit_grad=False):
    Rr, Cc = w.shape
    R = _pick(Rr, max(8, ADAMW_BLOCK_BYTES // (4 * Cc)), 8)

    def body(w_ref, g_ref, m_ref, v_ref, d_ref, mo_ref, vo_ref, *g_out):
        g = g_ref[...]
        m2 = ADAM_B1 * m_ref[...] + (1.0 - ADAM_B1) * g
        v2 = ADAM_B2 * v_ref[...] + (1.0 - ADAM_B2) * (g * g)
        m_hat = m2 / (1.0 - ADAM_B1 ** ADAM_STEP)
        v_hat = v2 / (1.0 - ADAM_B2 ** ADAM_STEP)
        d_ref[...] = -ADAM_LR * (m_hat / (jnp.sqrt(v_hat) + ADAM_EPS) + ADAM_WD * w_ref[...])
        mo_ref[...] = m2
        vo_ref[...] = v2
        if emit_grad:
            g_out[0][...] = g

    spec = pl.BlockSpec((R, Cc), lambda i: (i, 0))
    shp = jax.ShapeDtypeStruct((Rr, Cc), F32)
    n_out = 4 if emit_grad else 3
    return pl.pallas_call(
        body, name=name, out_shape=(shp,) * n_out, grid=(Rr // R,), in_specs=[spec] * 4, out_specs=(spec,) * n_out,
        compiler_params=_params("parallel"),
    )(w, g, m, v)


def _sum_chips(cs, rb, pc, lay, *, name):
    _, Rr, Cc = rb.shape
    tr, tc = _pick(Rr, 512, 16), _pick(Cc, 2048)
    nr, ncol = Rr // tr, Cc // tc
    if lay.axis == 0:
        own_map = lambda i, j, s: (i + s[0] * nr, j)
        out_map = lambda i, j, s: (i, j + s[1] * ncol)
    else:
        own_map = lambda i, j, s: (i, j + s[0] * ncol)
        out_map = lambda i, j, s: (i + s[1] * nr, j)

    def body(s_ref, own_ref, rb_ref, o_ref):
        acc = own_ref[...].astype(F32)
        for j in range(N_CHIPS - 1):
            acc = acc + rb_ref[j].astype(F32)
        o_ref[...] = acc

    return pl.pallas_call(
        body,
        name=name,
        out_shape=jax.ShapeDtypeStruct(lay.shard_shape(), F32),
        grid_spec=pltpu.PrefetchScalarGridSpec(
            num_scalar_prefetch=1,
            grid=(nr, ncol),
            in_specs=[pl.BlockSpec((tr, tc), own_map), pl.BlockSpec((N_CHIPS - 1, tr, tc), lambda i, j, s: (0, i, j))],
            out_specs=pl.BlockSpec((tr, tc), out_map)),
        compiler_params=_params("parallel", "parallel"),
    )(pc, cs, rb)


def _place():
    x, y, c = lax.axis_index("x"), lax.axis_index("y"), lax.axis_index("c")
    return x, y, c, 2 * x + y


def _other_chips(x, y):
    return [(1 - x, y, 2 * (1 - x) + y), (x, 1 - y, 2 * x + (1 - y)), (1 - x, 1 - y, 2 * (1 - x) + (1 - y))]


def _allgather_small(buf, *, name, reduce):
    S = buf.shape[0]

    def body(x_ref, o_ref, gat, send_sems, recv_sems):
        x, y, c, _ = _place()
        me = 4 * x + 2 * y + c
        gat[me] = x_ref[...]
        copies = []
        for k in range(1, N_DEV):
            fx, fy, fc = (k >> 2) & 1, (k >> 1) & 1, k & 1
            px = 1 - x if fx else x
            py = 1 - y if fy else y
            pc = 1 - c if fc else c
            peer = 4 * px + 2 * py + pc
            send = pltpu.make_async_remote_copy(
                src_ref=x_ref, dst_ref=gat.at[me], send_sem=send_sems.at[k - 1], recv_sem=recv_sems.at[k - 1],
                device_id=(px, py, pc), device_id_type=MESH)
            send.start()
            arrival = pltpu.make_async_remote_copy(
                src_ref=x_ref, dst_ref=gat.at[peer], send_sem=send_sems.at[k - 1], recv_sem=recv_sems.at[k - 1],
                device_id=(px, py, pc), device_id_type=MESH)
            copies.append((send, arrival))
        for send, arrival in copies:
            arrival.wait_recv()
        for send, arrival in copies:
            send.wait_send()
        if reduce:
            acc = gat[0]
            for d in range(1, N_DEV):
                acc = acc + gat[d]
            o_ref[...] = acc
        else:
            o_ref[...] = gat[...]

    out_shape = jax.ShapeDtypeStruct((S, LANES) if reduce else (N_DEV, S, LANES), F32)
    return pl.pallas_call(
        body,
        name=name,
        out_shape=out_shape,
        in_specs=[pl.BlockSpec(memory_space=pltpu.VMEM)],
        out_specs=pl.BlockSpec(memory_space=pltpu.VMEM),
        scratch_shapes=[pltpu.VMEM((N_DEV, S, LANES), F32), pltpu.SemaphoreType.DMA((N_DEV - 1,)),
                        pltpu.SemaphoreType.DMA((N_DEV - 1,))],
        compiler_params=pltpu.CompilerParams(vmem_limit_bytes=VMEM_LIMIT),
    )(buf)


class _Sharded:
    def __init__(self, shape, axis):
        self.shape = shape
        self.axis = axis
        self.block = shape[axis] // N_CHIPS
        self.half = shape[1 - axis] // 2

    def _sl(self, along, across):
        return (along, across) if self.axis == 0 else (across, along)

    def block_slice(self, q):
        return self._sl(pl.ds(q * self.block, self.block), pl.ds(0, self.shape[1 - self.axis]))

    def block_half_slice(self, q, c):
        return self._sl(pl.ds(q * self.block, self.block), pl.ds(c * self.half, self.half))

    def half_slice(self, c):
        return self._sl(pl.ds(0, self.shape[self.axis]), pl.ds(c * self.half, self.half))

    def shard_half_slice(self, c):
        return self._sl(pl.ds(0, self.block), pl.ds(c * self.half, self.half))

    def shard_shape(self):
        return self._sl(self.block, self.shape[1 - self.axis])

    def half_shape(self):
        return self._sl(self.shape[self.axis], self.half)

    def block_half_shape(self):
        return self._sl(self.block, self.half)

    def block_in_half_slice(self, q):
        return self._sl(pl.ds(q * self.block, self.block), pl.ds(0, self.half))


def _at(ref, sl):
    return ref.at[sl[0], sl[1]]


class _Copy:
    def __init__(self, src, dst, arrive, dev):
        self.src, self.dst, self.arrive, self.dev = src, dst, arrive, dev


HBM = pl.BlockSpec(memory_space=pltpu.HBM)
SEM = pl.BlockSpec(memory_space=pltpu.SEMAPHORE)
EFFECT = pltpu.SideEffectType.DATAFLOW_SIDE_EFFECTING


def _exchange_start(srcs, land_shapes, plan, after, *, name):
    ns, nl = len(srcs), len(land_shapes)
    n_copies = len(plan([None] * ns, [None] * nl, dry=True))

    def body(*refs):
        src_refs, land_refs = refs[:ns], refs[ns:ns + nl]
        send_sems, recv_sems = refs[ns + nl + 1], refs[ns + nl + 2]
        token = refs[-1]
        for k, cp in enumerate(plan(src_refs, land_refs)):
            pltpu.make_async_remote_copy(
                src_ref=cp.src, dst_ref=cp.dst, send_sem=send_sems.at[k], recv_sem=recv_sems.at[k],
                device_id=cp.dev, device_id_type=MESH).start()
        token[...] = jnp.zeros_like(token)

    sem = pltpu.SemaphoreType.DMA((n_copies,))
    lands = [pltpu.with_memory_space_constraint(lax.empty(shp, dt), pltpu.HBM) for shp, dt in land_shapes]
    srcs = [pltpu.with_memory_space_constraint(a, pltpu.HBM) for a in srcs]
    thru = [pltpu.HBM(a.shape, a.dtype) for a in srcs + lands]
    outs = pl.pallas_call(
        body,
        name=name,
        out_shape=(sem, sem, *thru, jax.ShapeDtypeStruct((8, LANES), F32)),
        in_specs=[HBM] * (ns + nl) + [ANY],
        out_specs=(SEM, SEM, *[HBM] * (ns + nl), pl.BlockSpec(memory_space=pltpu.VMEM)),
        input_output_aliases={i: 2 + i for i in range(ns + nl)},
        compiler_params=pltpu.CompilerParams(has_side_effects=EFFECT),
    )(*srcs, *lands, after)
    return outs[0], outs[1], list(outs[2:2 + ns]), list(outs[2 + ns:2 + ns + nl]), outs[-1]


def _exchange_wait(send_sems, recv_sems, srcs, lands, plan, after, *, name):
    ns, nl = len(srcs), len(lands)
    after = tuple(after) if isinstance(after, (tuple, list)) else (after,)

    def body(*refs):
        src_refs, land_refs = refs[:ns], refs[ns:ns + nl]
        send_sems, recv_sems = refs[ns + nl], refs[ns + nl + 1]
        copies = [
            pltpu.make_async_remote_copy(
                src_ref=cp.src, dst_ref=cp.arrive, send_sem=send_sems.at[k], recv_sem=recv_sems.at[k],
                device_id=cp.dev, device_id_type=MESH)
            for k, cp in enumerate(plan(src_refs, land_refs))
        ]
        for cp in copies:
            cp.wait_recv()
        for cp in copies:
            cp.wait_send()

    thru = [pltpu.HBM(a.shape, a.dtype) for a in list(srcs) + list(lands)]
    outs = pl.pallas_call(
        body,
        name=name,
        out_shape=tuple(thru),
        in_specs=[HBM] * (ns + nl) + [SEM, SEM] + [ANY] * len(after),
        out_specs=tuple([HBM] * (ns + nl)),
        input_output_aliases={i: i for i in range(ns + nl)},
        compiler_params=pltpu.CompilerParams(has_side_effects=EFFECT),
    )(*srcs, *lands, send_sems, recv_sems, *after)
    return list(outs[:ns]), list(outs[ns:])


def _gather_plan(layouts):
    def plan(srcs, lands, dry=False):
        if dry:
            return [None] * (4 * len(layouts))
        x, y, c, p = _place()
        copies = []
        for s, g, lay in zip(srcs, lands, layouts):
            own = _at(g, lay.block_slice(p))
            copies.append(_Copy(s, own, own, (x, y, 1 - c)))
            for qx, qy, q in _other_chips(x, y):
                copies.append(_Copy(_at(s, lay.shard_half_slice(c)), _at(g, lay.block_half_slice(p, c)),
                                    _at(g, lay.block_half_slice(q, c)), (qx, qy, c)))
        return copies

    return plan


def _forward_plan(layouts):
    def plan(srcs, lands, dry=False):
        if dry:
            return [None] * (3 * len(layouts))
        x, y, c, _ = _place()
        copies = []
        for g, lay in zip(srcs, layouts):
            for qx, qy, q in _other_chips(x, y):
                landed = _at(g, lay.block_half_slice(q, c))
                copies.append(_Copy(landed, landed, _at(g, lay.block_half_slice(q, 1 - c)), (x, y, 1 - c)))
        return copies

    return plan


def _sibling_plan(n):
    def plan(srcs, lands, dry=False):
        if dry:
            return [None] * n
        x, y, c, _ = _place()
        return [_Copy(s, ra, ra, (x, y, 1 - c)) for s, ra in zip(srcs, lands)]

    return plan


def _chips_plan(layouts):
    def plan(srcs, lands, dry=False):
        if dry:
            return [None] * (3 * len(layouts))
        x, y, c, _ = _place()
        copies = []
        for s, rb, lay in zip(srcs, lands, layouts):
            for j, (qx, qy, q) in enumerate(_other_chips(x, y)):
                copies.append(_Copy(_at(s, lay.block_in_half_slice(q)), rb.at[j], rb.at[j], (qx, qy, c)))
        return copies

    return plan


def _join_plan(layouts):
    def plan(srcs, lands, dry=False):
        if dry:
            return [None] * len(layouts)
        x, y, c, _ = _place()
        copies = []
        for g, lay in zip(srcs, layouts):
            mine = _at(g, lay.shard_half_slice(c))
            copies.append(_Copy(mine, mine, _at(g, lay.shard_half_slice(1 - c)), (x, y, 1 - c)))
        return copies

    return plan


def _pack(arrays):
    flat = [a.reshape(-1) for a in arrays]
    sizes = [f.shape[0] for f in flat]
    total = sum(sizes)
    rows = _round_up(-(-total // LANES), 8)
    flat.append(jnp.zeros((rows * LANES - total,), F32))
    return jnp.concatenate(flat).reshape(rows, LANES)


def _unpack(buf, shapes):
    flat = buf.reshape(-1)
    out, pos = [], 0
    for shp in shapes:
        n = 1
        for d in shp:
            n *= d
        out.append(flat[pos:pos + n].reshape(shp))
        pos += n
    return out


def _pad_to(a, rows, cols):
    return jnp.pad(a, ((0, rows - a.shape[0]), (0, cols - a.shape[1])))


def kernel(x, mem, g_mix, w_in, conv_a_w, conv_a_b, ln_a_g, ln_a_b, conv_b_w, w_out, g_xattn, g_mem, w_q, w_k, w_v, w_o, g_ffn, w_gate, w_up, conv_f_w, w_down, g_final, loss_target, m_g_mix, m_w_in, m_conv_a_w, m_conv_a_b, m_ln_a_g, m_ln_a_b, m_conv_b_w, m_w_out, m_g_xattn, m_g_mem, m_w_q, m_w_k, m_w_v, m_w_o, m_g_ffn, m_w_gate, m_w_up, m_conv_f_w, m_w_down, m_g_final, v_g_mix, v_w_in, v_conv_a_w, v_conv_a_b, v_ln_a_g, v_ln_a_b, v_conv_b_w, v_w_out, v_g_xattn, v_g_mem, v_w_q, v_w_k, v_w_v, v_w_o, v_g_ffn, v_w_gate, v_w_up, v_conv_f_w, v_w_down, v_g_final):
    T, D = x.shape[1], x.shape[2]
    in_sh = w_in.shape[2]
    cw_sh = conv_a_w.shape[2]
    cw = N_CHIPS * cw_sh
    f_sh = w_gate.shape[2]
    fp = _round_up(f_sh, 256)
    F = N_CHIPS * fp
    rs = w_out.shape[1]
    c_idx = lax.axis_index("c")
    p_idx = 2 * lax.axis_index("x") + lax.axis_index("y")

    def t_(a):
        return jnp.swapaxes(a[0], 0, 1)

    big = {
        "w_in": (lambda: w_in[0].astype(BF16), _Sharded((D, N_CHIPS * in_sh), 1)),
        "w_out": (lambda: w_out[0].astype(BF16), _Sharded((N_CHIPS * rs, D), 0)),
        "w_q": (lambda: w_q[0].astype(BF16), _Sharded((D, D), 0)),
        "w_k": (lambda: w_k[0].astype(BF16), _Sharded((D, D), 0)),
        "w_v": (lambda: w_v[0].astype(BF16), _Sharded((D, D), 0)),
        "w_o": (lambda: w_o[0].astype(BF16), _Sharded((D, D), 0)),
        "w_gate": (lambda: _pad_to(t_(w_gate).astype(BF16), fp, D), _Sharded((F, D), 0)),
        "w_up": (lambda: _pad_to(t_(w_up).astype(BF16), fp, D), _Sharded((F, D), 0)),
        "w_down": (lambda: _pad_to(w_down[0].astype(BF16), fp, D), _Sharded((F, D), 0)),
    }
    names = list(big)
    lay = {k: big[k][1] for k in names}
    c_arr = c_idx.astype(jnp.int32).reshape(1)
    pc_arr = jnp.stack([p_idx, c_idx]).astype(jnp.int32)

    conv_shapes = [(KA_ROWS, cw_sh), (KS_ROWS, cw_sh), (KS_ROWS, fp)]
    conv_pack = _pack([_pad_to(conv_a_w[0], KA_ROWS, cw_sh), _pad_to(conv_b_w[0], KS_ROWS, cw_sh),
                       _pad_to(conv_f_w[0], KS_ROWS, fp)])
    conv_all = _allgather_small(conv_pack, name="allgather_conv", reduce=False)
    per_chip = [_unpack(conv_all[2 * q], conv_shapes) for q in range(N_CHIPS)]
    wa = jnp.concatenate([pc[0] for pc in per_chip], axis=1)
    wb = jnp.concatenate([pc[1] for pc in per_chip], axis=1)
    wf = jnp.concatenate([pc[2] for pc in per_chip], axis=1)

    tok = conv_all

    in_half = _Sharded((D // 2, N_CHIPS * in_sh), 1)
    sources = {k: big[k] for k in names if k != "w_in"}
    sources["w_in_lo"] = (lambda: w_in[0, :D // 2].astype(BF16), in_half)
    sources["w_in_hi"] = (lambda: w_in[0, D // 2:].astype(BF16), in_half)
    gather_groups = {"in_lo": ["w_in_lo"], "in_hi": ["w_in_hi"], "out": ["w_out"], "qkv": ["w_q", "w_k", "w_v"],
                     "o": ["w_o"], "gate": ["w_gate"], "up": ["w_up"], "down": ["w_down"]}
    gathers = {}
    W = {}
    relays = {}

    def shards(tag):
        return [sources[k][0]() for k in gather_groups[tag]]

    def gather_start(tag, srcs, after):
        grp = gather_groups[tag]
        lays = [sources[k][1] for k in grp]
        plan = _gather_plan(lays)
        ssem, rsem, srcs, lands, token = _exchange_start(
            srcs, [(l.shape, BF16) for l in lays], plan, after, name=f"gather_start_{tag}")
        gathers[tag] = (ssem, rsem, srcs, lands, plan, lays, grp)
        return token

    def relay(tag, after, also=()):
        ssem, rsem, srcs, lands, plan, lays, grp = gathers[tag]
        _, lands = _exchange_wait(ssem, rsem, srcs, lands, plan, (after, *also), name=f"gather_wait_{tag}")
        plan = _forward_plan(lays)
        ssem, rsem, lands, _, token = _exchange_start(lands, [], plan, after, name=f"gather_forward_start_{tag}")
        relays[tag] = (ssem, rsem, lands, plan, grp)
        return token

    def gathered(tag, after):
        if tag not in relays:
            relay(tag, after)
        ssem, rsem, lands, plan, grp = relays[tag]
        lands, _ = _exchange_wait(ssem, rsem, lands, [], plan, after, name=f"gather_forward_wait_{tag}")
        W.update(zip(grp, lands))

    tok = gather_start("in_lo", shards("in_lo"), tok)
    tok = gather_start("in_hi", shards("in_hi"), tok)
    later = {tag: shards(tag) for tag in gather_groups if tag not in gathers}
    tok = relay("in_lo", tok, also=[a for srcs in later.values() for a in srcs])
    for tag, srcs in later.items():
        tok = gather_start(tag, srcs, tok)

    xs, tgt, mems = x[0], loss_target[0], mem[0]
    g_mem2, g_final2 = g_mem[None, :], g_final[None, :]

    memn, rm = _norm_fwd(mems, g_mem2, name="norm_mem")
    xn1, r1 = _norm_fwd(xs, g_mix, name="norm_mix")
    gathered("in_lo", xn1)
    proj = _mm(xn1, W["w_in_lo"], a_k=(0, D // 2), name="mm_in_lo")
    gathered("in_hi", proj)
    proj = _mm(xn1, W["w_in_hi"], a_k=(D // 2, D // 2), add=proj, name="mm_in_hi")
    ftok = relay("out", proj)
    u3, u1 = _mix_a_fwd(proj, wa, conv_a_b, ln_a_g, ln_a_b, cw=cw, deps=[ftok], name="mix_a_fwd")
    vb = _mix_b_fwd(proj, wb, cw=cw, sw=cw, name="mix_b_fwd")
    mix = jnp.concatenate([u3, vb], axis=1)
    gathered("out", mix)
    ftok = relay("qkv", mix)
    h1 = _mm(mix, W["w_out"], add=xs, deps=[ftok], name="mm_out")
    xn2, r2 = _norm_fwd(h1, g_xattn, name="norm_xattn")
    gathered("qkv", xn2)
    ftok = relay("o", xn2)
    q = _mm(xn2, W["w_q"], out_dtype=BF16, deps=[ftok], name="mm_q")
    k = _mm(memn, W["w_k"], out_dtype=BF16, name="mm_k")
    vm = _mm(memn, W["w_v"], out_dtype=BF16, name="mm_v")
    o = _attn_fwd(q, k, vm, name="attn_fwd")
    gathered("o", o)
    h2 = _mm(o, W["w_o"], add=h1, name="mm_o")
    relay("gate", h2)
    xn3, r3 = _norm_fwd(h2, g_ffn, name="norm_ffn")
    gathered("gate", xn3)
    gp = _mm(xn3, W["w_gate"], tb=True, name="mm_gate")
    gathered("up", gp)
    up = _mm(xn3, W["w_up"], tb=True, name="mm_up")
    f = _ffn_act_fwd(gp, up, wf, name="ffn_act_fwd")
    gathered("down", f)
    h3 = _mm(f, W["w_down"], add=h2, name="mm_down")

    G = {}
    other_arr = 1 - c_arr

    def grad_half(k, which, **kw):
        xk, dyk = G[k]
        return _mm(xk, dyk, ta=True, half=("n" if lay[k].axis == 0 else "m", which), **kw)

    def siblings_start(tag, grp, after):
        lays = [lay[k] for k in grp]
        firsts = [grad_half(k, other_arr, deps=[after], name=f"mm_dw_{k}_sibling") for k in grp]
        plan = _sibling_plan(len(grp))
        ssem, rsem, srcs, lands, token = _exchange_start(
            firsts, [(l.half_shape(), F32) for l in lays], plan, after, name=f"rs_siblings_start_{tag}")
        return (tag, grp, lays, plan, ssem, rsem, srcs, lands), token

    def chips_start(state, after):
        tag, grp, lays, plan, ssem, rsem, srcs, lands = state
        _, lands = _exchange_wait(ssem, rsem, srcs, lands, plan, after, name=f"rs_siblings_wait_{tag}")
        sums = [grad_half(k, c_arr, add=ra, out_dtype=BF16, name=f"mm_dw_{k}_own") for k, ra in zip(grp, lands)]
        plan = _chips_plan(lays)
        ssem, rsem, sums, lands, token = _exchange_start(
            sums, [((N_CHIPS - 1,) + l.block_half_shape(), BF16) for l in lays], plan, after, name=f"rs_chips_start_{tag}")
        return (tag, grp, lays, plan, ssem, rsem, sums, lands), token

    def join_start(state, after):
        tag, grp, lays, plan, ssem, rsem, sums, lands = state
        sums, lands = _exchange_wait(ssem, rsem, sums, lands, plan, after, name=f"rs_chips_wait_{tag}")
        halves = [_sum_chips(cs, rb, pc_arr, l, name=f"sum_chips_{k}") for k, cs, rb, l in zip(grp, sums, lands, lays)]
        plan = _join_plan(lays)
        ssem, rsem, halves, _, token = _exchange_start(halves, [], plan, after[0], name=f"rs_join_start_{tag}")
        return (tag, grp, plan, ssem, rsem, halves), token

    def join_finish(state, after):
        tag, grp, plan, ssem, rsem, halves = state
        halves, _ = _exchange_wait(ssem, rsem, halves, [], plan, after, name=f"rs_join_wait_{tag}")
        return dict(zip(grp, halves))

    loss_rows, dh3, dh3b, dg_final = _loss_head(h3, g_final2, tgt, name="loss_head")
    df = _mm(dh3b, W["w_down"], tb=True, out_dtype=BF16, name="mm_d_f")
    G["w_down"] = (f, dh3b)
    rs_down, tok = siblings_start("down", ["w_down"], tok)
    dgp, dup, dwf = _ffn_act_bwd(gp, up, df, wf, name="ffn_act_bwd")
    G["w_gate"], G["w_up"] = (dgp, xn3), (dup, xn3)
    rs_ffn, tok = siblings_start("ffn", ["w_gate", "w_up"], tok)
    rs_down, tok = chips_start(rs_down, tok)
    dxn3 = _mm(dgp, W["w_gate"], deps=[tok], name="mm_dxn3_gate")
    dxn3 = _mm(dup, W["w_up"], add=dxn3, name="mm_dxn3_up")
    dh2, dh2b, dg_ffn = _norm_bwd(h2, r3, g_ffn, dxn3, dh3, name="norm_ffn_bwd")
    rs_ffn, tok = chips_start(rs_ffn, dh2b)
    do = _mm(dh2b, W["w_o"], tb=True, out_dtype=BF16, deps=[tok], name="mm_d_o")
    G["w_o"] = (o, dh2b)
    dq, dk, dvm = _attn_bwd(q, k, vm, do, name="attn_bwd")
    dkb, dvb = dk.astype(BF16), dvm.astype(BF16)
    G["w_q"], G["w_k"], G["w_v"] = (xn2, dq), (memn, dkb), (memn, dvb)
    rs_att, tok = siblings_start("att", ["w_o", "w_q", "w_k", "w_v"], tok)
    dxn2 = _mm(dq, W["w_q"], tb=True, deps=[tok], name="mm_dxn2")
    dmemn = _mm(dkb, W["w_k"], tb=True, name="mm_dmem_k")
    dmemn = _mm(dvb, W["w_v"], tb=True, add=dmemn, name="mm_dmem_v")
    dg_mem = _norm_bwd(mems, rm, g_mem2, dmemn, None, name="norm_mem_bwd", want_dh=False)
    dh1, dh1b, dg_xattn = _norm_bwd(h1, r2, g_xattn, dxn2, dh2, name="norm_xattn_bwd")
    rs_att, tok = chips_start(rs_att, dh1b)
    dmix = _mm(dh1b, W["w_out"], tb=True, deps=[tok], name="mm_d_mix")
    G["w_out"] = (mix, dh1b)
    rs_out, tok = siblings_start("out", ["w_out"], tok)
    dav, dag, dwa, dba, dlg, dlb = _mix_a_bwd(proj, u1, dmix, wa, ln_a_g, ln_a_b, cw=cw, name="mix_a_bwd")
    dbg, dcg, dbh, dwb = _mix_b_bwd(proj, dmix, wb, cw=cw, sw=cw, name="mix_b_bwd")
    dproj = jnp.concatenate([dav, dag, dbg, dcg, dbh], axis=1)
    G["w_in"] = (xn1, dproj)
    rs_in, tok = siblings_start("in", ["w_in"], tok)
    rs_out, tok = chips_start(rs_out, tok)
    dxn1 = _mm(dproj, W["w_in_lo"], tb=True, out_cols=(0, D), deps=[tok], name="mm_dxn1_lo")
    dxn1 = _mm(dproj, W["w_in_hi"], tb=True, out_cols=(D // 2, D), into=dxn1, name="mm_dxn1_hi")
    dx, _, dg_mix = _norm_bwd(xs, r1, g_mix, dxn1, dh1, name="norm_mix_bwd")
    rs_in, tok = chips_start(rs_in, dx)

    loss_part = jnp.sum(loss_rows).reshape(1, 1)
    small_parts = [dg_mix, dba, dlg, dlb, dg_xattn, dg_mem, dg_ffn, dg_final, dwa, dwb, dwf, loss_part]
    small_shapes = [a.shape for a in small_parts]
    reduced = _allgather_small(_pack(small_parts), name="allreduce_small", reduce=True)
    (sg_mix, sba, slg, slb, sg_xattn, sg_mem, sg_ffn, sg_final, swa, swb, swf, loss_sum) = _unpack(reduced, small_shapes)
    loss = loss_sum.reshape(())
    ga_w = lax.dynamic_slice(swa, (0, p_idx * cw_sh), (KA, cw_sh))
    gb_w = lax.dynamic_slice(swb, (0, p_idx * cw_sh), (KS, cw_sh))
    gf_w = lax.dynamic_slice(swf, (0, p_idx * fp), (KS, f_sh))

    weights = dict(g_mix=g_mix, w_in=w_in, conv_a_w=conv_a_w, conv_a_b=conv_a_b, ln_a_g=ln_a_g, ln_a_b=ln_a_b,
                   conv_b_w=conv_b_w, w_out=w_out, g_xattn=g_xattn, g_mem=g_mem, w_q=w_q, w_k=w_k, w_v=w_v, w_o=w_o,
                   g_ffn=g_ffn, w_gate=w_gate, w_up=w_up, conv_f_w=conv_f_w, w_down=w_down, g_final=g_final)
    m_in = dict(g_mix=m_g_mix, w_in=m_w_in, conv_a_w=m_conv_a_w, conv_a_b=m_conv_a_b, ln_a_g=m_ln_a_g, ln_a_b=m_ln_a_b,
                conv_b_w=m_conv_b_w, w_out=m_w_out, g_xattn=m_g_xattn, g_mem=m_g_mem, w_q=m_w_q, w_k=m_w_k, w_v=m_w_v,
                w_o=m_w_o, g_ffn=m_g_ffn, w_gate=m_w_gate, w_up=m_w_up, conv_f_w=m_conv_f_w, w_down=m_w_down,
                g_final=m_g_final)
    v_in = dict(g_mix=v_g_mix, w_in=v_w_in, conv_a_w=v_conv_a_w, conv_a_b=v_conv_a_b, ln_a_g=v_ln_a_g, ln_a_b=v_ln_a_b,
                conv_b_w=v_conv_b_w, w_out=v_w_out, g_xattn=v_g_xattn, g_mem=v_g_mem, w_q=v_w_q, w_k=v_w_k, w_v=v_w_v,
                w_o=v_w_o, g_ffn=v_g_ffn, w_gate=v_w_gate, w_up=v_w_up, conv_f_w=v_conv_f_w, w_down=v_w_down,
                g_final=v_g_final)
    order = list(weights)
    grads = dict(g_mix=sg_mix, conv_a_w=ga_w, conv_a_b=sba, ln_a_g=slg, ln_a_b=slb, conv_b_w=gb_w, g_xattn=sg_xattn,
                 g_mem=sg_mem, g_ffn=sg_ffn, conv_f_w=gf_w, g_final=sg_final)
    grads = {k: g.reshape(weights[k].shape) for k, g in grads.items()}

    delta, new_m, new_v = {}, {}, {}
    small = [k for k in order if k not in big]
    small_shapes = [weights[k].shape for k in small]
    packed = [_pack([src[k] for k in small]) for src in (weights, grads, m_in, v_in)]
    d_, m_, v_ = _adamw(*packed, name="adamw_small")
    for k, dd, mm_, vv in zip(small, _unpack(d_, small_shapes), _unpack(m_, small_shapes), _unpack(v_, small_shapes)):
        delta[k], new_m[k], new_v[k] = dd, mm_, vv

    transposed = ("w_gate", "w_up")

    def update(shard_grads):
        last = None
        for k, g in shard_grads.items():
            view = t_ if k in transposed else (lambda a: a[0])
            back = (lambda a: jnp.swapaxes(a, 0, 1)[None]) if k in transposed else (lambda a: a[None])
            padded = g.shape != view(weights[k]).shape
            outs = _adamw(view(weights[k]), g, view(m_in[k]), view(v_in[k]), emit_grad=padded, name=f"adamw_{k}")
            delta[k], new_m[k], new_v[k] = back(outs[0]), back(outs[1]), back(outs[2])
            grads[k] = back(outs[3] if padded else g)
            last = outs[0]
        return last

    after = (d_, tok)
    joining = None
    for state in (rs_down, rs_ffn, rs_att, rs_out, rs_in):
        started, token = join_start(state, after)
        after = (token,)
        if joining is not None:
            after = (update(join_finish(joining, token)), token)
        joining = started
    update(join_finish(joining, after[0]))

    return (loss, dx[None], *[grads[k] for k in order], *[delta[k] for k in order],
            *[new_m[k] for k in order], *[new_v[k] for k in order])
```

```python
import functools

import jax
import jax.numpy as jnp
from jax import lax
from jax.experimental import pallas as pl
from jax.experimental.pallas import tpu as pltpu

F32 = jnp.float32
BF16 = jnp.bfloat16
EPS = 1e-6
N_HEADS = 4
GROUP = 128
KA = 31
KS = 3
KA_ROWS = 32
KS_ROWS = 8
HALO_A = 32
HALO_S = 8
CHUNK_ROWS = 16
CHUNK_COLS = 256
N_CHIPS = 4
N_DEV = 8
LANES = 128
VMEM_LIMIT = 56 * 1024 * 1024
MM_WIDE_TILE_ELEMS = 4096 * 512
ADAMW_BLOCK_BYTES = 1 << 20
MESH = pl.DeviceIdType.MESH
ANY = pl.BlockSpec(memory_space=pl.ANY)

ADAM_LR = 0.001
ADAM_B1 = 0.9
ADAM_B2 = 0.999
ADAM_EPS = 1e-08
ADAM_WD = 0.01
ADAM_STEP = 10


def _pick(dim, pref, mult=LANES):
    if dim <= pref:
        return dim
    t = (pref // mult) * mult
    while t >= mult:
        if dim % t == 0:
            return t
        t -= mult
    return dim


def _round_up(n, m):
    return ((n + m - 1) // m) * m


def _params(*sem):
    return pltpu.CompilerParams(dimension_semantics=sem, vmem_limit_bytes=VMEM_LIMIT)


def _sigmoid(x):
    return jax.nn.sigmoid(x)


def _mm(a, b, *, name, ta=False, tb=False, out_dtype=F32, add=None, deps=(), half=None, a_k=None, out_cols=None,
        into=None, tm=1024, tn=512, tk=5632):
    if ta:
        K, M = a.shape
    else:
        M, K = a.shape
    if tb:
        N, K2 = b.shape
    else:
        K2, N = b.shape
    k0 = 0
    if a_k is not None:
        k0, K = a_k
    assert K == K2, (a.shape, b.shape)
    half_dim, half_sel = half if half is not None else (None, None)
    if half_dim == "m":
        M //= 2
    elif half_dim == "n":
        N //= 2
    tm, tk = _pick(M, tm), _pick(K, tk)
    nk = K // tk
    tn = _pick(N, 2 * tn if nk == 1 and K * tn <= MM_WIDE_TILE_ELEMS else tn)
    mi, nj = M // tm, N // tn
    n0, n_total = out_cols if out_cols is not None else (0, N)
    assert k0 % tk == 0 and n0 % tn == 0
    koff, noff = k0 // tk, n0 // tn

    def a_map(i, j, k, *s):
        i = i + s[0][0] * mi if half_dim == "m" else i
        return (k + koff, i) if ta else (i, k + koff)

    def b_map(i, j, k, *s):
        j = j + s[0][0] * nj if half_dim == "n" else j
        return (j, k) if tb else (k, j)

    a_spec = pl.BlockSpec((tk, tm) if ta else (tm, tk), a_map)
    b_spec = pl.BlockSpec((tn, tk) if tb else (tk, tn), b_map)
    o_spec = pl.BlockSpec((tm, tn), lambda i, j, k, *s: (i, j + noff))
    add_spec = pl.BlockSpec((tm, tn), lambda i, j, k, *s: (i, j))
    dims = (((0,) if ta else (1,), (1,) if tb else (0,)), ((), ()))
    has_add = add is not None
    n_pre = 0 if half is None else 1
    n_after = len(deps) + (0 if into is None else 1)

    def body(*refs):
        refs = refs[n_pre:]
        a_ref, b_ref = refs[:2]
        add_ref = refs[2] if has_add else None
        o_ref = refs[(3 if has_add else 2) + n_after]
        acc_ref = refs[-1] if nk > 1 else None
        k = pl.program_id(2)
        part = lax.dot_general(a_ref[...], b_ref[...], dims, preferred_element_type=F32)

        def finish(r):
            if add_ref is not None:
                r = add_ref[...] + r
            o_ref[...] = r.astype(out_dtype)

        if nk == 1:
            finish(part)
        else:
            @pl.when(k == 0)
            def _():
                acc_ref[...] = part

            @pl.when(jnp.logical_and(k > 0, k < nk - 1))
            def _():
                acc_ref[...] += part

            @pl.when(k == nk - 1)
            def _():
                finish(acc_ref[...] + part)

    in_specs = [a_spec, b_spec] + ([add_spec] if has_add else []) + [ANY] * n_after
    args = (a, b) + ((add,) if has_add else ()) + tuple(deps) + ((into,) if into is not None else ())
    aliases = {n_pre + len(args) - 1: 0} if into is not None else {}
    scratch = [pltpu.VMEM((tm, tn), F32)] if nk > 1 else []
    grid = (mi, nj, nk)
    if half is None:
        grid_spec = dict(grid=grid, in_specs=in_specs, out_specs=o_spec, scratch_shapes=scratch)
    else:
        args = (half_sel,) + args
        grid_spec = dict(grid_spec=pltpu.PrefetchScalarGridSpec(
            num_scalar_prefetch=1, grid=grid, in_specs=in_specs, out_specs=o_spec, scratch_shapes=scratch))
    return pl.pallas_call(
        body,
        name=name,
        out_shape=jax.ShapeDtypeStruct((M, n_total), out_dtype),
        input_output_aliases=aliases,
        compiler_params=_params("parallel", "parallel", "arbitrary"),
        **grid_spec,
    )(*args)


def _norm_fwd(h, g, *, name):
    T, D = h.shape
    R = _pick(T, 256, 8)

    def body(h_ref, g_ref, xn_ref, r_ref):
        x = h_ref[...]
        r = lax.rsqrt(jnp.mean(x * x, axis=-1, keepdims=True) + EPS)
        xn_ref[...] = ((x * r) * g_ref[...]).astype(BF16)
        r_ref[...] = r

    return pl.pallas_call(
        body,
        name=name,
        out_shape=(jax.ShapeDtypeStruct((T, D), BF16), jax.ShapeDtypeStruct((T, 1), F32)),
        grid=(T // R,),
        in_specs=[pl.BlockSpec((R, D), lambda i: (i, 0)), pl.BlockSpec((1, D), lambda i: (0, 0))],
        out_specs=(pl.BlockSpec((R, D), lambda i: (i, 0)), pl.BlockSpec((R, 1), lambda i: (i, 0))),
        compiler_params=_params("parallel"),
    )(h, g)


def _norm_bwd(h, r, g, dxn, dres, *, name, want_dh=True):
    T, D = h.shape
    R = _pick(T, 128, 8)
    has_res = dres is not None

    def body(*refs):
        h_ref, r_ref, g_ref, dxn_ref = refs[:4]
        pos = 4
        dres_ref = None
        if has_res:
            dres_ref = refs[pos]
            pos += 1
        if want_dh:
            dh_ref, dhb_ref, dg_ref = refs[pos:pos + 3]
        else:
            dg_ref = refs[pos]
        i = pl.program_id(0)
        rr = r_ref[...]
        hn = h_ref[...] * rr
        d = dxn_ref[...].astype(F32)
        gd = d * g_ref[...]
        part = jnp.sum(d * hn, axis=0, keepdims=True)

        @pl.when(i == 0)
        def _():
            dg_ref[...] = part

        @pl.when(i > 0)
        def _():
            dg_ref[...] += part

        if want_dh:
            dh = rr * (gd - hn * jnp.mean(gd * hn, axis=-1, keepdims=True))
            if dres_ref is not None:
                dh = dres_ref[...] + dh
            dh_ref[...] = dh
            dhb_ref[...] = dh.astype(BF16)

    row = pl.BlockSpec((R, D), lambda i: (i, 0))
    vec = pl.BlockSpec((1, D), lambda i: (0, 0))
    in_specs = [row, pl.BlockSpec((R, 1), lambda i: (i, 0)), vec, row] + ([row] if has_res else [])
    args = (h, r, g, dxn) + ((dres,) if has_res else ())
    if want_dh:
        out_shape = (jax.ShapeDtypeStruct((T, D), F32), jax.ShapeDtypeStruct((T, D), BF16), jax.ShapeDtypeStruct((1, D), F32))
        out_specs = (row, row, vec)
    else:
        out_shape = jax.ShapeDtypeStruct((1, D), F32)
        out_specs = vec
    return pl.pallas_call(
        body, name=name, out_shape=out_shape, grid=(T // R,), in_specs=in_specs, out_specs=out_specs,
        compiler_params=_params("arbitrary"),
    )(*args)


def _loss_head(h, g, tgt, *, name):
    T, D = h.shape
    R = _pick(T, 128, 8)

    def body(h_ref, g_ref, t_ref, loss_ref, dh_ref, dhb_ref, dg_ref):
        i = pl.program_id(0)
        x = h_ref[...]
        gg = g_ref[...]
        r = lax.rsqrt(jnp.mean(x * x, axis=-1, keepdims=True) + EPS)
        hn = x * r
        e = hn * gg - t_ref[...]
        loss_ref[...] = 0.5 * jnp.mean(e * e, axis=-1, keepdims=True)
        dy = e * (1.0 / D)
        gd = dy * gg
        dh = r * (gd - hn * jnp.mean(gd * hn, axis=-1, keepdims=True))
        dh_ref[...] = dh
        dhb_ref[...] = dh.astype(BF16)
        part = jnp.sum(dy * hn, axis=0, keepdims=True)

        @pl.when(i == 0)
        def _():
            dg_ref[...] = part

        @pl.when(i > 0)
        def _():
            dg_ref[...] += part

    row = pl.BlockSpec((R, D), lambda i: (i, 0))
    vec = pl.BlockSpec((1, D), lambda i: (0, 0))
    return pl.pallas_call(
        body,
        name=name,
        out_shape=(jax.ShapeDtypeStruct((T, 1), F32), jax.ShapeDtypeStruct((T, D), F32),
                   jax.ShapeDtypeStruct((T, D), BF16), jax.ShapeDtypeStruct((1, D), F32)),
        grid=(T // R,),
        in_specs=[row, vec, row],
        out_specs=(pl.BlockSpec((R, 1), lambda i: (i, 0)), row, row, vec),
        compiler_params=_params("arbitrary"),
    )(h, g, tgt)


_NT = (((1,), (1,)), ((), ()))
_TN = (((0,), (0,)), ((), ()))
_NN = (((1,), (0,)), ((), ()))


def _softmax_rows(s):
    m = jnp.max(s, axis=-1, keepdims=True)
    e = jnp.exp(s - m)
    return e / jnp.sum(e, axis=-1, keepdims=True)


def _attn_fwd(q, k, v, *, name):
    T, D = q.shape
    ML = k.shape[0]
    dh = D // N_HEADS
    scale = dh ** -0.5
    R = _pick(T, 512, 16)

    def body(q_ref, k_ref, v_ref, o_ref):
        s = lax.dot_general(q_ref[...], k_ref[...], _NT, preferred_element_type=F32) * scale
        p = _softmax_rows(s)
        o_ref[...] = lax.dot_general(p.astype(BF16), v_ref[...], _NN, preferred_element_type=F32).astype(BF16)

    qs = pl.BlockSpec((R, dh), lambda i, h: (i, h))
    ks = pl.BlockSpec((ML, dh), lambda i, h: (0, h))
    return pl.pallas_call(
        body, name=name, out_shape=jax.ShapeDtypeStruct((T, D), BF16), grid=(T // R, N_HEADS),
        in_specs=[qs, ks, ks], out_specs=qs, compiler_params=_params("parallel", "parallel"),
    )(q, k, v)


def _attn_bwd(q, k, v, do, *, name):
    T, D = q.shape
    ML = k.shape[0]
    dh = D // N_HEADS
    scale = dh ** -0.5
    R = _pick(T, 512, 16)

    def body(q_ref, k_ref, v_ref, do_ref, dq_ref, dk_ref, dv_ref):
        i = pl.program_id(1)
        qq, kk, vv, dd = q_ref[...], k_ref[...], v_ref[...], do_ref[...]
        s = lax.dot_general(qq, kk, _NT, preferred_element_type=F32) * scale
        p = _softmax_rows(s)
        dp = lax.dot_general(dd, vv, _NT, preferred_element_type=F32)
        dv_part = lax.dot_general(p.astype(BF16), dd, _TN, preferred_element_type=F32)
        ds = (p * (dp - jnp.sum(p * dp, axis=-1, keepdims=True)) * scale).astype(BF16)
        dq_ref[...] = lax.dot_general(ds, kk, _NN, preferred_element_type=F32).astype(BF16)
        dk_part = lax.dot_general(ds, qq, _TN, preferred_element_type=F32)

        @pl.when(i == 0)
        def _():
            dk_ref[...] = dk_part
            dv_ref[...] = dv_part

        @pl.when(i > 0)
        def _():
            dk_ref[...] += dk_part
            dv_ref[...] += dv_part

    qs = pl.BlockSpec((R, dh), lambda h, i: (i, h))
    ks = pl.BlockSpec((ML, dh), lambda h, i: (0, h))
    return pl.pallas_call(
        body,
        name=name,
        out_shape=(jax.ShapeDtypeStruct((T, D), BF16), jax.ShapeDtypeStruct((ML, D), F32), jax.ShapeDtypeStruct((ML, D), F32)),
        grid=(N_HEADS, T // R),
        in_specs=[qs, ks, ks, qs],
        out_specs=(qs, ks, ks),
        compiler_params=_params("parallel", "arbitrary"),
    )(q, k, v, do)


def _prev_halo(R, halo, CT, col):
    per = R // halo
    return lambda i, j: (jnp.maximum(i * per - 1, 0), col(j))


def _next_halo(R, halo, T, col):
    per = R // halo
    last = T // halo - 1
    return lambda i, j: (jnp.minimum((i + 1) * per, last), col(j))


def _swap(f):
    return lambda j, i: f(i, j)


def _ffn_act_fwd(gp, up, wf, *, name):
    T, F = gp.shape
    R = _pick(T, 256, CHUNK_ROWS)
    CT = _pick(F, 1024, CHUNK_COLS)
    H = HALO_S
    RC, CC = CHUNK_ROWS, min(CHUNK_COLS, CT)

    def body(g_ref, gh_ref, u_ref, w_ref, f_ref, ext):
        i = pl.program_id(0)
        ext[pl.ds(0, H), :] = jnp.where(i > 0, gh_ref[...], 0.0)
        ext[pl.ds(H, R), :] = g_ref[...]
        for c0 in range(0, CT, CC):
            cols = pl.ds(c0, CC)
            w0, w1, w2 = [jnp.broadcast_to(w_ref[pl.ds(k, 1), cols], (RC, CC)) for k in range(KS)]
            for r0 in range(0, R, RC):
                g = ext[pl.ds(r0 + H - 2, RC), cols] * w0
                g = g + ext[pl.ds(r0 + H - 1, RC), cols] * w1
                g = g + ext[pl.ds(r0 + H, RC), cols] * w2
                f_ref[pl.ds(r0, RC), cols] = (g * _sigmoid(g) * u_ref[pl.ds(r0, RC), cols]).astype(BF16)

    main = pl.BlockSpec((R, CT), lambda i, j: (i, j))
    return pl.pallas_call(
        body,
        name=name,
        out_shape=jax.ShapeDtypeStruct((T, F), BF16),
        grid=(T // R, F // CT),
        in_specs=[main, pl.BlockSpec((H, CT), _prev_halo(R, H, CT, lambda j: j)), main,
                  pl.BlockSpec((KS_ROWS, CT), lambda i, j: (0, j))],
        out_specs=main,
        scratch_shapes=[pltpu.VMEM((R + H, CT), F32)],
        compiler_params=_params("parallel", "parallel"),
    )(gp, gp, up, wf)


def _ffn_act_bwd(gp, up, df, wf, *, name):
    T, F = gp.shape
    R = _pick(T, 256, CHUNK_ROWS)
    CT = _pick(F, 512, CHUNK_COLS)
    H = HALO_S
    HB = 16
    n_t = T // R
    RC, CC = CHUNK_ROWS, min(CHUNK_COLS, CT)

    def body(g_ref, gp_ref, gn_ref, u_ref, un_ref, d_ref, dn_ref, w_ref, dg_out, du_out, dw_ref, ext, dge):
        i = pl.program_id(1)
        last = i == n_t - 1
        ext[pl.ds(0, H), :] = jnp.where(i > 0, gp_ref[...], 0.0)
        ext[pl.ds(H, R), :] = g_ref[...]
        ext[pl.ds(H + R, H), :] = gn_ref[...]

        def dact(g, u, d):
            sg = _sigmoid(g)
            return d * u * (sg * (1.0 + g * (1.0 - sg))), d * (g * sg)

        dw_rows = [[] for _ in range(KS)]
        for c0 in range(0, CT, CC):
            cols = pl.ds(c0, CC)
            w = [jnp.broadcast_to(w_ref[pl.ds(k, 1), cols], (RC, CC)) for k in range(KS)]
            acc = [jnp.zeros((RC, CC), F32) for _ in range(KS)]
            for r0 in range(0, R, RC):
                rows = pl.ds(r0, RC)
                taps = [ext[pl.ds(r0 + H - 2 + k, RC), cols] for k in range(KS)]
                g = taps[0] * w[0] + taps[1] * w[1] + taps[2] * w[2]
                dg, du = dact(g, u_ref[rows, cols], d_ref[rows, cols].astype(F32))
                du_out[rows, cols] = du.astype(BF16)
                dge[rows, cols] = dg
                acc = [a + dg * t for a, t in zip(acc, taps)]
            taps = [ext[pl.ds(R + H - 2 + k, H), cols] for k in range(KS)]
            g = taps[0] * w[0][:H] + taps[1] * w[1][:H] + taps[2] * w[2][:H]
            d_next = jnp.where(last, 0.0, dn_ref[pl.ds(0, H), cols].astype(F32))
            dg_next, _ = dact(g, un_ref[:, cols], d_next)
            dge[pl.ds(R, H), cols] = jnp.where(last, 0.0, dg_next)
            for r0 in range(0, R, RC):
                dgp = (dge[pl.ds(r0, RC), cols] * w[2] + dge[pl.ds(r0 + 1, RC), cols] * w[1]
                       + dge[pl.ds(r0 + 2, RC), cols] * w[0])
                dg_out[pl.ds(r0, RC), cols] = dgp.astype(BF16)
            for k in range(KS):
                dw_rows[k].append(jnp.sum(acc[k], axis=0, keepdims=True))
        rows = [jnp.concatenate(r, axis=1) if len(r) > 1 else r[0] for r in dw_rows]
        rows.append(jnp.zeros((KS_ROWS - KS, CT), F32))
        _acc_rows_block(dw_ref, i, rows)

    col = lambda j: j
    main = pl.BlockSpec((R, CT), lambda j, i: (i, j))
    prev8 = pl.BlockSpec((H, CT), _swap(_prev_halo(R, H, CT, col)))
    next8 = pl.BlockSpec((H, CT), _swap(_next_halo(R, H, T, col)))
    next16 = pl.BlockSpec((HB, CT), _swap(_next_halo(R, HB, T, col)))
    wspec = pl.BlockSpec((KS_ROWS, CT), lambda j, i: (0, j))
    return pl.pallas_call(
        body,
        name=name,
        out_shape=(jax.ShapeDtypeStruct((T, F), BF16), jax.ShapeDtypeStruct((T, F), BF16), jax.ShapeDtypeStruct((KS_ROWS, F), F32)),
        grid=(F // CT, n_t),
        in_specs=[main, prev8, next8, main, next8, main, next16, wspec],
        out_specs=(main, main, wspec),
        scratch_shapes=[pltpu.VMEM((R + 2 * H, CT), F32), pltpu.VMEM((R + H, CT), F32)],
        compiler_params=_params("parallel", "arbitrary"),
    )(gp, gp, gp, up, up, df, df, wf)


def _acc_rows_block(ref, i, rows):
    *singles, pad = rows

    @pl.when(i == 0)
    def _():
        for k, row in enumerate(singles):
            ref[pl.ds(k, 1), :] = row
        ref[pl.ds(len(singles), pad.shape[0]), :] = pad

    @pl.when(i > 0)
    def _():
        for k, row in enumerate(singles):
            ref[pl.ds(k, 1), :] += row


def _group_stats(x):
    mu = jnp.mean(x, axis=-1, keepdims=True)
    xc = x - mu
    var = jnp.mean(xc * xc, axis=-1, keepdims=True)
    return xc, lax.rsqrt(var + EPS)


def _shifted_rows(rolled, x, cols):
    n = x.shape[0]
    for b in range(1, 8):
        rolled[b - 1, :, cols] = pltpu.roll(x, n - b, axis=0)


def _window(src, rolled, off, rows, cols):
    b = off % 8
    if b == 0:
        return src[pl.ds(off, rows), cols]
    return rolled[b - 1, pl.ds(off - b, rows), cols]


def _mix_a_fwd(proj, wa, ba, lg, lb, *, cw, name, deps=()):
    T = proj.shape[0]
    R = _pick(T, 128, HALO_A)
    CT = _pick(cw, 256)
    nc = cw // CT
    H = HALO_A

    def body(av_ref, ag_ref, avh_ref, agh_ref, w_ref, b_ref, lg_ref, lb_ref, *rest):
        u3_ref, u1_ref, ext, rolled = rest[len(deps):]
        i = pl.program_id(0)
        for gi in range(CT // GROUP):
            cols = pl.ds(gi * GROUP, GROUP)
            ext[pl.ds(0, H), cols] = jnp.where(i > 0, avh_ref[:, cols] * _sigmoid(agh_ref[:, cols]), 0.0)
            ext[pl.ds(H, R), cols] = av_ref[:, cols] * _sigmoid(ag_ref[:, cols])
            _shifted_rows(rolled, ext[:, cols], cols)
            acc = _window(ext, rolled, H - (KA - 1), R, cols) * w_ref[pl.ds(0, 1), cols]
            for k in range(1, KA):
                acc = acc + _window(ext, rolled, H - (KA - 1) + k, R, cols) * w_ref[pl.ds(k, 1), cols]
            u1 = acc + b_ref[:, cols]
            u1_ref[:, cols] = u1
            xc, rstd = _group_stats(u1)
            u2 = (xc * rstd) * lg_ref[:, cols] + lb_ref[:, cols]
            u3_ref[:, cols] = (u2 * _sigmoid(u2)).astype(BF16)

    main_v = pl.BlockSpec((R, CT), lambda i, j: (i, j))
    main_g = pl.BlockSpec((R, CT), lambda i, j: (i, j + nc))
    halo_v = pl.BlockSpec((H, CT), _prev_halo(R, H, CT, lambda j: j))
    halo_g = pl.BlockSpec((H, CT), _prev_halo(R, H, CT, lambda j: j + nc))
    vec = pl.BlockSpec((1, CT), lambda i, j: (0, j))
    out = pl.BlockSpec((R, CT), lambda i, j: (i, j))
    return pl.pallas_call(
        body,
        name=name,
        out_shape=(jax.ShapeDtypeStruct((T, cw), BF16), jax.ShapeDtypeStruct((T, cw), F32)),
        grid=(T // R, nc),
        in_specs=[main_v, main_g, halo_v, halo_g, pl.BlockSpec((KA_ROWS, CT), lambda i, j: (0, j)), vec, vec, vec]
        + [ANY] * len(deps),
        out_specs=(out, out),
        scratch_shapes=[pltpu.VMEM((R + H, CT), F32), pltpu.VMEM((7, R + H, CT), F32)],
        compiler_params=_params("parallel", "parallel"),
    )(proj, proj, proj, proj, wa, ba, lg, lb, *deps)


def _mix_a_bwd(proj, u1, dmix, wa, lg, lb, *, cw, name):
    T = proj.shape[0]
    R = _pick(T, 128, HALO_A)
    CT = _pick(cw, 256)
    nc = cw // CT
    H = HALO_A
    n_t = T // R
    NG = CT // GROUP

    def body(av_ref, ag_ref, avh_ref, agh_ref, u1_ref, u1n_ref, d3_ref, d3n_ref, w_ref, lg_ref, lb_ref,
             dav_ref, dag_ref, dw_ref, db_ref, dlg_ref, dlb_ref, ext, d1, ext_rolled, d1_rolled):
        i = pl.program_id(1)
        last = i == n_t - 1

        def ln_bwd(u1, d3, cols):
            xc, rstd = _group_stats(u1)
            xh = xc * rstd
            g = lg_ref[:, cols]
            u2 = xh * g + lb_ref[:, cols]
            sg = _sigmoid(u2)
            du2 = d3 * (sg * (1.0 + u2 * (1.0 - sg)))
            dxh = du2 * g
            du1 = rstd * (dxh - jnp.mean(dxh, axis=-1, keepdims=True) - xh * jnp.mean(dxh * xh, axis=-1, keepdims=True))
            return du1, du2 * xh, du2

        def lanes(parts):
            return jnp.concatenate(parts, axis=1) if len(parts) > 1 else parts[0]

        dlg_parts, dlb_parts, db_parts = [], [], []
        dw_parts = [[] for _ in range(KA)]
        for gi in range(NG):
            cols = pl.ds(gi * GROUP, GROUP)
            du1, dlg, dlb = ln_bwd(u1_ref[:, cols], d3_ref[:, cols], cols)
            d1[pl.ds(0, R), cols] = du1
            dlg_parts.append(jnp.sum(dlg, axis=0, keepdims=True))
            dlb_parts.append(jnp.sum(dlb, axis=0, keepdims=True))
            db_parts.append(jnp.sum(du1, axis=0, keepdims=True))
            du1n, _, _ = ln_bwd(u1n_ref[:, cols], jnp.where(last, 0.0, d3n_ref[:, cols]), cols)
            d1[pl.ds(R, H), cols] = jnp.where(last, 0.0, du1n)

            _shifted_rows(d1_rolled, d1[:, cols], cols)
            du0 = _window(d1, d1_rolled, KA - 1, R, cols) * w_ref[pl.ds(0, 1), cols]
            for k in range(1, KA):
                du0 = du0 + _window(d1, d1_rolled, KA - 1 - k, R, cols) * w_ref[pl.ds(k, 1), cols]
            av = av_ref[:, cols]
            sg = _sigmoid(ag_ref[:, cols])
            dav_ref[:, cols] = (du0 * sg).astype(BF16)
            dag_ref[:, cols] = (du0 * av * sg * (1.0 - sg)).astype(BF16)

            ext[pl.ds(0, H), cols] = jnp.where(i > 0, avh_ref[:, cols] * _sigmoid(agh_ref[:, cols]), 0.0)
            ext[pl.ds(H, R), cols] = av * sg
            _shifted_rows(ext_rolled, ext[:, cols], cols)
            d1_main = d1[pl.ds(0, R), cols]
            for k in range(KA):
                window = _window(ext, ext_rolled, H - (KA - 1) + k, R, cols)
                dw_parts[k].append(jnp.sum(d1_main * window, axis=0, keepdims=True))

        dlg_row, dlb_row, db_row = lanes(dlg_parts), lanes(dlb_parts), lanes(db_parts)

        @pl.when(i == 0)
        def _():
            dlg_ref[...] = dlg_row
            dlb_ref[...] = dlb_row
            db_ref[...] = db_row

        @pl.when(i > 0)
        def _():
            dlg_ref[...] += dlg_row
            dlb_ref[...] += dlb_row
            db_ref[...] += db_row

        rows = [lanes(p) for p in dw_parts]
        rows.append(jnp.zeros((KA_ROWS - KA, CT), F32))
        _acc_rows_block(dw_ref, i, rows)

    cv = lambda j: j
    cg = lambda j: j + nc
    main_v = pl.BlockSpec((R, CT), lambda j, i: (i, j))
    main_g = pl.BlockSpec((R, CT), lambda j, i: (i, j + nc))
    prev_v = pl.BlockSpec((H, CT), _swap(_prev_halo(R, H, CT, cv)))
    prev_g = pl.BlockSpec((H, CT), _swap(_prev_halo(R, H, CT, cg)))
    nxt = pl.BlockSpec((H, CT), _swap(_next_halo(R, H, T, cv)))
    wspec = pl.BlockSpec((KA_ROWS, CT), lambda j, i: (0, j))
    vec = pl.BlockSpec((1, CT), lambda j, i: (0, j))
    vshape = jax.ShapeDtypeStruct((1, cw), F32)
    return pl.pallas_call(
        body,
        name=name,
        out_shape=(jax.ShapeDtypeStruct((T, cw), BF16), jax.ShapeDtypeStruct((T, cw), BF16),
                   jax.ShapeDtypeStruct((KA_ROWS, cw), F32), vshape, vshape, vshape),
        grid=(nc, n_t),
        in_specs=[main_v, main_g, prev_v, prev_g, main_v, nxt, main_v, nxt, wspec, vec, vec],
        out_specs=(main_v, main_v, wspec, vec, vec, vec),
        scratch_shapes=[pltpu.VMEM((R + H, CT), F32), pltpu.VMEM((R + H, CT), F32),
                        pltpu.VMEM((7, R + H, CT), F32), pltpu.VMEM((7, R + H, CT), F32)],
        compiler_params=_params("parallel", "arbitrary"),
    )(proj, proj, proj, proj, u1, u1, dmix, dmix, wa, lg, lb)


def _mix_b_fwd(proj, wb, *, cw, sw, name):
    T = proj.shape[0]
    R = _pick(T, 256, 8)
    CT = _pick(sw, 512)
    nb, nc, nh = (2 * cw) // CT, (2 * cw + sw) // CT, (2 * cw + 2 * sw) // CT
    H = HALO_S

    def body(b_ref, c_ref, h_ref, ch_ref, hh_ref, w_ref, v_ref, ext):
        i = pl.program_id(0)
        ext[pl.ds(0, H), :] = jnp.where(i > 0, ch_ref[...] * hh_ref[...], 0.0)
        ext[pl.ds(H, R), :] = c_ref[...] * h_ref[...]
        w = w_ref[...]
        zc = ext[pl.ds(H - 2, R), :] * w[0:1, :]
        zc = zc + ext[pl.ds(H - 1, R), :] * w[1:2, :]
        zc = zc + ext[pl.ds(H, R), :] * w[2:3, :]
        v_ref[...] = (b_ref[...] * zc).astype(BF16)

    def main(off):
        return pl.BlockSpec((R, CT), lambda i, j: (i, j + off))

    def prev(off):
        return pl.BlockSpec((H, CT), _prev_halo(R, H, CT, lambda j: j + off))

    return pl.pallas_call(
        body,
        name=name,
        out_shape=jax.ShapeDtypeStruct((T, sw), BF16),
        grid=(T // R, sw // CT),
        in_specs=[main(nb), main(nc), main(nh), prev(nc), prev(nh), pl.BlockSpec((KS_ROWS, CT), lambda i, j: (0, j))],
        out_specs=pl.BlockSpec((R, CT), lambda i, j: (i, j)),
        scratch_shapes=[pltpu.VMEM((R + H, CT), F32)],
        compiler_params=_params("parallel", "parallel"),
    )(proj, proj, proj, proj, proj, wb)


def _mix_b_bwd(proj, dmix, wb, *, cw, sw, name):
    T = proj.shape[0]
    R = _pick(T, 256, 8)
    CT = _pick(sw, 512)
    nb, nc, nh = (2 * cw) // CT, (2 * cw + sw) // CT, (2 * cw + 2 * sw) // CT
    nd = cw // CT
    H = HALO_S
    n_t = T // R

    def body(b_ref, bn_ref, c_ref, cp_ref, cn_ref, h_ref, hp_ref, hn_ref, d_ref, dn_ref, w_ref,
             db_ref, dc_ref, dhh_ref, dw_ref, ext, dze):
        i = pl.program_id(1)
        last = i == n_t - 1
        cc, hh = c_ref[...], h_ref[...]
        ext[pl.ds(0, H), :] = jnp.where(i > 0, cp_ref[...] * hp_ref[...], 0.0)
        ext[pl.ds(H, R), :] = cc * hh
        ext[pl.ds(H + R, H), :] = cn_ref[...] * hn_ref[...]
        w = w_ref[...]
        w0, w1, w2 = w[0:1, :], w[1:2, :], w[2:3, :]

        def conv(start, n):
            z = ext[pl.ds(start + H - 2, n), :] * w0
            z = z + ext[pl.ds(start + H - 1, n), :] * w1
            return z + ext[pl.ds(start + H, n), :] * w2

        d_main = d_ref[...]
        db_ref[...] = (d_main * conv(0, R)).astype(BF16)
        dzc_main = d_main * b_ref[...]
        dze[pl.ds(0, R), :] = dzc_main
        dze[pl.ds(R, H), :] = jnp.where(last, 0.0, dn_ref[...] * bn_ref[...])
        dz = dze[pl.ds(0, R), :] * w2 + dze[pl.ds(1, R), :] * w1 + dze[pl.ds(2, R), :] * w0
        dc_ref[...] = (dz * hh).astype(BF16)
        dhh_ref[...] = (dz * cc).astype(BF16)
        rows = [jnp.sum(dzc_main * ext[pl.ds(H - 2 + k, R), :], axis=0, keepdims=True) for k in range(KS)]
        rows.append(jnp.zeros((KS_ROWS - KS, CT), F32))
        _acc_rows_block(dw_ref, i, rows)

    def main(off):
        return pl.BlockSpec((R, CT), lambda j, i: (i, j + off))

    def prev(off):
        return pl.BlockSpec((H, CT), _swap(_prev_halo(R, H, CT, lambda j: j + off)))

    def nxt(off):
        return pl.BlockSpec((H, CT), _swap(_next_halo(R, H, T, lambda j: j + off)))

    out = pl.BlockSpec((R, CT), lambda j, i: (i, j))
    wspec = pl.BlockSpec((KS_ROWS, CT), lambda j, i: (0, j))
    act = jax.ShapeDtypeStruct((T, sw), BF16)
    return pl.pallas_call(
        body,
        name=name,
        out_shape=(act, act, act, jax.ShapeDtypeStruct((KS_ROWS, sw), F32)),
        grid=(sw // CT, n_t),
        in_specs=[main(nb), nxt(nb), main(nc), prev(nc), nxt(nc), main(nh), prev(nh), nxt(nh), main(nd), nxt(nd), wspec],
        out_specs=(out, out, out, wspec),
        scratch_shapes=[pltpu.VMEM((R + 2 * H, CT), F32), pltpu.VMEM((R + H, CT), F32)],
        compiler_params=_params("parallel", "arbitrary"),
    )(proj, proj, proj, proj, proj, proj, proj, proj, dmix, dmix, wb)


def _adamw(w, g, m, v, *, name, emit_grad=False):
    Rr, Cc = w.shape
    R = _pick(Rr, max(8, ADAMW_BLOCK_BYTES // (4 * Cc)), 8)

    def body(w_ref, g_ref, m_ref, v_ref, d_ref, mo_ref, vo_ref, *g_out):
        g = g_ref[...]
        m2 = ADAM_B1 * m_ref[...] + (1.0 - ADAM_B1) * g
        v2 = ADAM_B2 * v_ref[...] + (1.0 - ADAM_B2) * (g * g)
        m_hat = m2 / (1.0 - ADAM_B1 ** ADAM_STEP)
        v_hat = v2 / (1.0 - ADAM_B2 ** ADAM_STEP)
        d_ref[...] = -ADAM_LR * (m_hat / (jnp.sqrt(v_hat) + ADAM_EPS) + ADAM_WD * w_ref[...])
        mo_ref[...] = m2
        vo_ref[...] = v2
        if emit_grad:
            g_out[0][...] = g

    spec = pl.BlockSpec((R, Cc), lambda i: (i, 0))
    shp = jax.ShapeDtypeStruct((Rr, Cc), F32)
    n_out = 4 if emit_grad else 3
    return pl.pallas_call(
        body, name=name, out_shape=(shp,) * n_out, grid=(Rr // R,), in_specs=[spec] * 4, out_specs=(spec,) * n_out,
        compiler_params=_params("parallel"),
    )(w, g, m, v)


def _sum_chips(cs, rb, pc, lay, *, name):
    _, Rr, Cc = rb.shape
    tr, tc = _pick(Rr, 512, 16), _pick(Cc, 2048)
    nr, ncol = Rr // tr, Cc // tc
    if lay.axis == 0:
        own_map = lambda i, j, s: (i + s[0] * nr, j)
        out_map = lambda i, j, s: (i, j + s[1] * ncol)
    else:
        own_map = lambda i, j, s: (i, j + s[0] * ncol)
        out_map = lambda i, j, s: (i + s[1] * nr, j)

    def body(s_ref, own_ref, rb_ref, o_ref):
        acc = own_ref[...].astype(F32)
        for j in range(N_CHIPS - 1):
            acc = acc + rb_ref[j].astype(F32)
        o_ref[...] = acc

    return pl.pallas_call(
        body,
        name=name,
        out_shape=jax.ShapeDtypeStruct(lay.shard_shape(), F32),
        grid_spec=pltpu.PrefetchScalarGridSpec(
            num_scalar_prefetch=1,
            grid=(nr, ncol),
            in_specs=[pl.BlockSpec((tr, tc), own_map), pl.BlockSpec((N_CHIPS - 1, tr, tc), lambda i, j, s: (0, i, j))],
            out_specs=pl.BlockSpec((tr, tc), out_map)),
        compiler_params=_params("parallel", "parallel"),
    )(pc, cs, rb)


def _place():
    x, y, c = lax.axis_index("x"), lax.axis_index("y"), lax.axis_index("c")
    return x, y, c, 2 * x + y


def _other_chips(x, y):
    return [(1 - x, y, 2 * (1 - x) + y), (x, 1 - y, 2 * x + (1 - y)), (1 - x, 1 - y, 2 * (1 - x) + (1 - y))]


def _allgather_small(buf, *, name, reduce):
    S = buf.shape[0]

    def body(x_ref, o_ref, gat, send_sems, recv_sems):
        x, y, c, _ = _place()
        me = 4 * x + 2 * y + c
        gat[me] = x_ref[...]
        copies = []
        for k in range(1, N_DEV):
            fx, fy, fc = (k >> 2) & 1, (k >> 1) & 1, k & 1
            px = 1 - x if fx else x
            py = 1 - y if fy else y
            pc = 1 - c if fc else c
            peer = 4 * px + 2 * py + pc
            send = pltpu.make_async_remote_copy(
                src_ref=x_ref, dst_ref=gat.at[me], send_sem=send_sems.at[k - 1], recv_sem=recv_sems.at[k - 1],
                device_id=(px, py, pc), device_id_type=MESH)
            send.start()
            arrival = pltpu.make_async_remote_copy(
                src_ref=x_ref, dst_ref=gat.at[peer], send_sem=send_sems.at[k - 1], recv_sem=recv_sems.at[k - 1],
                device_id=(px, py, pc), device_id_type=MESH)
            copies.append((send, arrival))
        for send, arrival in copies:
            arrival.wait_recv()
        for send, arrival in copies:
            send.wait_send()
        if reduce:
            acc = gat[0]
            for d in range(1, N_DEV):
                acc = acc + gat[d]
            o_ref[...] = acc
        else:
            o_ref[...] = gat[...]

    out_shape = jax.ShapeDtypeStruct((S, LANES) if reduce else (N_DEV, S, LANES), F32)
    return pl.pallas_call(
        body,
        name=name,
        out_shape=out_shape,
        in_specs=[pl.BlockSpec(memory_space=pltpu.VMEM)],
        out_specs=pl.BlockSpec(memory_space=pltpu.VMEM),
        scratch_shapes=[pltpu.VMEM((N_DEV, S, LANES), F32), pltpu.SemaphoreType.DMA((N_DEV - 1,)),
                        pltpu.SemaphoreType.DMA((N_DEV - 1,))],
        compiler_params=pltpu.CompilerParams(vmem_limit_bytes=VMEM_LIMIT),
    )(buf)


class _Sharded:
    def __init__(self, shape, axis):
        self.shape = shape
        self.axis = axis
        self.block = shape[axis] // N_CHIPS
        self.half = shape[1 - axis] // 2

    def _sl(self, along, across):
        return (along, across) if self.axis == 0 else (across, along)

    def block_slice(self, q):
        return self._sl(pl.ds(q * self.block, self.block), pl.ds(0, self.shape[1 - self.axis]))

    def block_half_slice(self, q, c):
        return self._sl(pl.ds(q * self.block, self.block), pl.ds(c * self.half, self.half))

    def half_slice(self, c):
        return self._sl(pl.ds(0, self.shape[self.axis]), pl.ds(c * self.half, self.half))

    def shard_half_slice(self, c):
        return self._sl(pl.ds(0, self.block), pl.ds(c * self.half, self.half))

    def shard_shape(self):
        return self._sl(self.block, self.shape[1 - self.axis])

    def half_shape(self):
        return self._sl(self.shape[self.axis], self.half)

    def block_half_shape(self):
        return self._sl(self.block, self.half)

    def block_in_half_slice(self, q):
        return self._sl(pl.ds(q * self.block, self.block), pl.ds(0, self.half))


def _at(ref, sl):
    return ref.at[sl[0], sl[1]]


class _Copy:
    def __init__(self, src, dst, arrive, dev):
        self.src, self.dst, self.arrive, self.dev = src, dst, arrive, dev


HBM = pl.BlockSpec(memory_space=pltpu.HBM)
SEM = pl.BlockSpec(memory_space=pltpu.SEMAPHORE)
EFFECT = pltpu.SideEffectType.DATAFLOW_SIDE_EFFECTING


def _exchange_start(srcs, land_shapes, plan, after, *, name):
    ns, nl = len(srcs), len(land_shapes)
    n_copies = len(plan([None] * ns, [None] * nl, dry=True))

    def body(*refs):
        src_refs, land_refs = refs[:ns], refs[ns:ns + nl]
        send_sems, recv_sems = refs[ns + nl + 1], refs[ns + nl + 2]
        token = refs[-1]
        for k, cp in enumerate(plan(src_refs, land_refs)):
            pltpu.make_async_remote_copy(
                src_ref=cp.src, dst_ref=cp.dst, send_sem=send_sems.at[k], recv_sem=recv_sems.at[k],
                device_id=cp.dev, device_id_type=MESH).start()
        token[...] = jnp.zeros_like(token)

    sem = pltpu.SemaphoreType.DMA((n_copies,))
    lands = [pltpu.with_memory_space_constraint(lax.empty(shp, dt), pltpu.HBM) for shp, dt in land_shapes]
    srcs = [pltpu.with_memory_space_constraint(a, pltpu.HBM) for a in srcs]
    thru = [pltpu.HBM(a.shape, a.dtype) for a in srcs + lands]
    outs = pl.pallas_call(
        body,
        name=name,
        out_shape=(sem, sem, *thru, jax.ShapeDtypeStruct((8, LANES), F32)),
        in_specs=[HBM] * (ns + nl) + [ANY],
        out_specs=(SEM, SEM, *[HBM] * (ns + nl), pl.BlockSpec(memory_space=pltpu.VMEM)),
        input_output_aliases={i: 2 + i for i in range(ns + nl)},
        compiler_params=pltpu.CompilerParams(has_side_effects=EFFECT),
    )(*srcs, *lands, after)
    return outs[0], outs[1], list(outs[2:2 + ns]), list(outs[2 + ns:2 + ns + nl]), outs[-1]


def _exchange_wait(send_sems, recv_sems, srcs, lands, plan, after, *, name):
    ns, nl = len(srcs), len(lands)
    after = tuple(after) if isinstance(after, (tuple, list)) else (after,)

    def body(*refs):
        src_refs, land_refs = refs[:ns], refs[ns:ns + nl]
        send_sems, recv_sems = refs[ns + nl], refs[ns + nl + 1]
        copies = [
            pltpu.make_async_remote_copy(
                src_ref=cp.src, dst_ref=cp.arrive, send_sem=send_sems.at[k], recv_sem=recv_sems.at[k],
                device_id=cp.dev, device_id_type=MESH)
            for k, cp in enumerate(plan(src_refs, land_refs))
        ]
        for cp in copies:
            cp.wait_recv()
        for cp in copies:
            cp.wait_send()

    thru = [pltpu.HBM(a.shape, a.dtype) for a in list(srcs) + list(lands)]
    outs = pl.pallas_call(
        body,
        name=name,
        out_shape=tuple(thru),
        in_specs=[HBM] * (ns + nl) + [SEM, SEM] + [ANY] * len(after),
        out_specs=tuple([HBM] * (ns + nl)),
        input_output_aliases={i: i for i in range(ns + nl)},
        compiler_params=pltpu.CompilerParams(has_side_effects=EFFECT),
    )(*srcs, *lands, send_sems, recv_sems, *after)
    return list(outs[:ns]), list(outs[ns:])


def _gather_plan(layouts):
    def plan(srcs, lands, dry=False):
        if dry:
            return [None] * (4 * len(layouts))
        x, y, c, p = _place()
        copies = []
        for s, g, lay in zip(srcs, lands, layouts):
            own = _at(g, lay.block_slice(p))
            copies.append(_Copy(s, own, own, (x, y, 1 - c)))
            for qx, qy, q in _other_chips(x, y):
                copies.append(_Copy(_at(s, lay.shard_half_slice(c)), _at(g, lay.block_half_slice(p, c)),
                                    _at(g, lay.block_half_slice(q, c)), (qx, qy, c)))
        return copies

    return plan


def _forward_plan(layouts):
    def plan(srcs, lands, dry=False):
        if dry:
            return [None] * (3 * len(layouts))
        x, y, c, _ = _place()
        copies = []
        for g, lay in zip(srcs, layouts):
            for qx, qy, q in _other_chips(x, y):
                landed = _at(g, lay.block_half_slice(q, c))
                copies.append(_Copy(landed, landed, _at(g, lay.block_half_slice(q, 1 - c)), (x, y, 1 - c)))
        return copies

    return plan


def _sibling_plan(n):
    def plan(srcs, lands, dry=False):
        if dry:
            return [None] * n
        x, y, c, _ = _place()
        return [_Copy(s, ra, ra, (x, y, 1 - c)) for s, ra in zip(srcs, lands)]

    return plan


def _chips_plan(layouts):
    def plan(srcs, lands, dry=False):
        if dry:
            return [None] * (3 * len(layouts))
        x, y, c, _ = _place()
        copies = []
        for s, rb, lay in zip(srcs, lands, layouts):
            for j, (qx, qy, q) in enumerate(_other_chips(x, y)):
                copies.append(_Copy(_at(s, lay.block_in_half_slice(q)), rb.at[j], rb.at[j], (qx, qy, c)))
        return copies

    return plan


def _join_plan(layouts):
    def plan(srcs, lands, dry=False):
        if dry:
            return [None] * len(layouts)
        x, y, c, _ = _place()
        copies = []
        for g, lay in zip(srcs, layouts):
            mine = _at(g, lay.shard_half_slice(c))
            copies.append(_Copy(mine, mine, _at(g, lay.shard_half_slice(1 - c)), (x, y, 1 - c)))
        return copies

    return plan


def _pack(arrays):
    flat = [a.reshape(-1) for a in arrays]
    sizes = [f.shape[0] for f in flat]
    total = sum(sizes)
    rows = _round_up(-(-total // LANES), 8)
    flat.append(jnp.zeros((rows * LANES - total,), F32))
    return jnp.concatenate(flat).reshape(rows, LANES)


def _unpack(buf, shapes):
    flat = buf.reshape(-1)
    out, pos = [], 0
    for shp in shapes:
        n = 1
        for d in shp:
            n *= d
        out.append(flat[pos:pos + n].reshape(shp))
        pos += n
    return out


def _pad_to(a, rows, cols):
    return jnp.pad(a, ((0, rows - a.shape[0]), (0, cols - a.shape[1])))


def kernel(x, mem, g_mix, w_in, conv_a_w, conv_a_b, ln_a_g, ln_a_b, conv_b_w, w_out, g_xattn, g_mem, w_q, w_k, w_v, w_o, g_ffn, w_gate, w_up, conv_f_w, w_down, g_final, loss_target, m_g_mix, m_w_in, m_conv_a_w, m_conv_a_b, m_ln_a_g, m_ln_a_b, m_conv_b_w, m_w_out, m_g_xattn, m_g_mem, m_w_q, m_w_k, m_w_v, m_w_o, m_g_ffn, m_w_gate, m_w_up, m_conv_f_w, m_w_down, m_g_final, v_g_mix, v_w_in, v_conv_a_w, v_conv_a_b, v_ln_a_g, v_ln_a_b, v_conv_b_w, v_w_out, v_g_xattn, v_g_mem, v_w_q, v_w_k, v_w_v, v_w_o, v_g_ffn, v_w_gate, v_w_up, v_conv_f_w, v_w_down, v_g_final):
    T, D = x.shape[1], x.shape[2]
    in_sh = w_in.shape[2]
    cw_sh = conv_a_w.shape[2]
    cw = N_CHIPS * cw_sh
    f_sh = w_gate.shape[2]
    fp = _round_up(f_sh, 256)
    F = N_CHIPS * fp
    rs = w_out.shape[1]
    c_idx = lax.axis_index("c")
    p_idx = 2 * lax.axis_index("x") + lax.axis_index("y")

    def t_(a):
        return jnp.swapaxes(a[0], 0, 1)

    big = {
        "w_in": (lambda: w_in[0].astype(BF16), _Sharded((D, N_CHIPS * in_sh), 1)),
        "w_out": (lambda: w_out[0].astype(BF16), _Sharded((N_CHIPS * rs, D), 0)),
        "w_q": (lambda: w_q[0].astype(BF16), _Sharded((D, D), 0)),
        "w_k": (lambda: w_k[0].astype(BF16), _Sharded((D, D), 0)),
        "w_v": (lambda: w_v[0].astype(BF16), _Sharded((D, D), 0)),
        "w_o": (lambda: w_o[0].astype(BF16), _Sharded((D, D), 0)),
        "w_gate": (lambda: _pad_to(t_(w_gate).astype(BF16), fp, D), _Sharded((F, D), 0)),
        "w_up": (lambda: _pad_to(t_(w_up).astype(BF16), fp, D), _Sharded((F, D), 0)),
        "w_down": (lambda: _pad_to(w_down[0].astype(BF16), fp, D), _Sharded((F, D), 0)),
    }
    names = list(big)
    lay = {k: big[k][1] for k in names}
    c_arr = c_idx.astype(jnp.int32).reshape(1)
    pc_arr = jnp.stack([p_idx, c_idx]).astype(jnp.int32)

    conv_shapes = [(KA_ROWS, cw_sh), (KS_ROWS, cw_sh), (KS_ROWS, fp)]
    conv_pack = _pack([_pad_to(conv_a_w[0], KA_ROWS, cw_sh), _pad_to(conv_b_w[0], KS_ROWS, cw_sh),
                       _pad_to(conv_f_w[0], KS_ROWS, fp)])
    conv_all = _allgather_small(conv_pack, name="allgather_conv", reduce=False)
    per_chip = [_unpack(conv_all[2 * q], conv_shapes) for q in range(N_CHIPS)]
    wa = jnp.concatenate([pc[0] for pc in per_chip], axis=1)
    wb = jnp.concatenate([pc[1] for pc in per_chip], axis=1)
    wf = jnp.concatenate([pc[2] for pc in per_chip], axis=1)

    tok = conv_all

    in_half = _Sharded((D // 2, N_CHIPS * in_sh), 1)
    sources = {k: big[k] for k in names if k != "w_in"}
    sources["w_in_lo"] = (lambda: w_in[0, :D // 2].astype(BF16), in_half)
    sources["w_in_hi"] = (lambda: w_in[0, D // 2:].astype(BF16), in_half)
    gather_groups = {"in_lo": ["w_in_lo"], "in_hi": ["w_in_hi"], "out": ["w_out"], "qkv": ["w_q", "w_k", "w_v"],
                     "o": ["w_o"], "gate": ["w_gate"], "up": ["w_up"], "down": ["w_down"]}
    gathers = {}
    W = {}
    relays = {}

    def shards(tag):
        return [sources[k][0]() for k in gather_groups[tag]]

    def gather_start(tag, srcs, after):
        grp = gather_groups[tag]
        lays = [sources[k][1] for k in grp]
        plan = _gather_plan(lays)
        ssem, rsem, srcs, lands, token = _exchange_start(
            srcs, [(l.shape, BF16) for l in lays], plan, after, name=f"gather_start_{tag}")
        gathers[tag] = (ssem, rsem, srcs, lands, plan, lays, grp)
        return token

    def relay(tag, after, also=()):
        ssem, rsem, srcs, lands, plan, lays, grp = gathers[tag]
        _, lands = _exchange_wait(ssem, rsem, srcs, lands, plan, (after, *also), name=f"gather_wait_{tag}")
        plan = _forward_plan(lays)
        ssem, rsem, lands, _, token = _exchange_start(lands, [], plan, after, name=f"gather_forward_start_{tag}")
        relays[tag] = (ssem, rsem, lands, plan, grp)
        return token

    def gathered(tag, after):
        if tag not in relays:
            relay(tag, after)
        ssem, rsem, lands, plan, grp = relays[tag]
        lands, _ = _exchange_wait(ssem, rsem, lands, [], plan, after, name=f"gather_forward_wait_{tag}")
        W.update(zip(grp, lands))

    tok = gather_start("in_lo", shards("in_lo"), tok)
    tok = gather_start("in_hi", shards("in_hi"), tok)
    later = {tag: shards(tag) for tag in gather_groups if tag not in gathers}
    tok = relay("in_lo", tok, also=[a for srcs in later.values() for a in srcs])
    for tag, srcs in later.items():
        tok = gather_start(tag, srcs, tok)

    xs, tgt, mems = x[0], loss_target[0], mem[0]
    g_mem2, g_final2 = g_mem[None, :], g_final[None, :]

    memn, rm = _norm_fwd(mems, g_mem2, name="norm_mem")
    xn1, r1 = _norm_fwd(xs, g_mix, name="norm_mix")
    gathered("in_lo", (xn1, tok))
    proj = _mm(xn1, W["w_in_lo"], a_k=(0, D // 2), name="mm_in_lo")
    gathered("in_hi", proj)
    proj = _mm(xn1, W["w_in_hi"], a_k=(D // 2, D // 2), add=proj, name="mm_in_hi")
    ftok = relay("out", proj)
    u3, u1 = _mix_a_fwd(proj, wa, conv_a_b, ln_a_g, ln_a_b, cw=cw, deps=[ftok], name="mix_a_fwd")
    vb = _mix_b_fwd(proj, wb, cw=cw, sw=cw, name="mix_b_fwd")
    mix = jnp.concatenate([u3, vb], axis=1)
    gathered("out", mix)
    ftok = relay("qkv", mix)
    h1 = _mm(mix, W["w_out"], add=xs, deps=[ftok], name="mm_out")
    xn2, r2 = _norm_fwd(h1, g_xattn, name="norm_xattn")
    gathered("qkv", xn2)
    ftok = relay("o", xn2)
    q = _mm(xn2, W["w_q"], out_dtype=BF16, deps=[ftok], name="mm_q")
    k = _mm(memn, W["w_k"], out_dtype=BF16, name="mm_k")
    vm = _mm(memn, W["w_v"], out_dtype=BF16, name="mm_v")
    o = _attn_fwd(q, k, vm, name="attn_fwd")
    gathered("o", o)
    h2 = _mm(o, W["w_o"], add=h1, name="mm_o")
    relay("gate", h2)
    xn3, r3 = _norm_fwd(h2, g_ffn, name="norm_ffn")
    gathered("gate", xn3)
    gp = _mm(xn3, W["w_gate"], tb=True, name="mm_gate")
    gathered("up", gp)
    up = _mm(xn3, W["w_up"], tb=True, name="mm_up")
    f = _ffn_act_fwd(gp, up, wf, name="ffn_act_fwd")
    gathered("down", f)
    h3 = _mm(f, W["w_down"], add=h2, name="mm_down")

    G = {}
    other_arr = 1 - c_arr

    def grad_half(k, which, **kw):
        xk, dyk = G[k]
        return _mm(xk, dyk, ta=True, half=("n" if lay[k].axis == 0 else "m", which), **kw)

    def siblings_start(tag, grp, after):
        lays = [lay[k] for k in grp]
        firsts = [grad_half(k, other_arr, deps=[after], name=f"mm_dw_{k}_sibling") for k in grp]
        plan = _sibling_plan(len(grp))
        ssem, rsem, srcs, lands, token = _exchange_start(
            firsts, [(l.half_shape(), F32) for l in lays], plan, after, name=f"rs_siblings_start_{tag}")
        return (tag, grp, lays, plan, ssem, rsem, srcs, lands), token

    def chips_start(state, after):
        tag, grp, lays, plan, ssem, rsem, srcs, lands = state
        _, lands = _exchange_wait(ssem, rsem, srcs, lands, plan, after, name=f"rs_siblings_wait_{tag}")
        sums = [grad_half(k, c_arr, add=ra, out_dtype=BF16, name=f"mm_dw_{k}_own") for k, ra in zip(grp, lands)]
        plan = _chips_plan(lays)
        ssem, rsem, sums, lands, token = _exchange_start(
            sums, [((N_CHIPS - 1,) + l.block_half_shape(), BF16) for l in lays], plan, after, name=f"rs_chips_start_{tag}")
        return (tag, grp, lays, plan, ssem, rsem, sums, lands), token

    def join_start(state, after):
        tag, grp, lays, plan, ssem, rsem, sums, lands = state
        sums, lands = _exchange_wait(ssem, rsem, sums, lands, plan, after, name=f"rs_chips_wait_{tag}")
        halves = [_sum_chips(cs, rb, pc_arr, l, name=f"sum_chips_{k}") for k, cs, rb, l in zip(grp, sums, lands, lays)]
        plan = _join_plan(lays)
        ssem, rsem, halves, _, token = _exchange_start(halves, [], plan, after[0], name=f"rs_join_start_{tag}")
        return (tag, grp, plan, ssem, rsem, halves), token

    def join_finish(state, after):
        tag, grp, plan, ssem, rsem, halves = state
        halves, _ = _exchange_wait(ssem, rsem, halves, [], plan, after, name=f"rs_join_wait_{tag}")
        return dict(zip(grp, halves))

    loss_rows, dh3, dh3b, dg_final = _loss_head(h3, g_final2, tgt, name="loss_head")
    df = _mm(dh3b, W["w_down"], tb=True, out_dtype=BF16, name="mm_d_f")
    G["w_down"] = (f, dh3b)
    rs_down, tok = siblings_start("down", ["w_down"], tok)
    dgp, dup, dwf = _ffn_act_bwd(gp, up, df, wf, name="ffn_act_bwd")
    G["w_gate"], G["w_up"] = (dgp, xn3), (dup, xn3)
    rs_ffn, tok = siblings_start("ffn", ["w_gate", "w_up"], tok)
    rs_down, tok = chips_start(rs_down, tok)
    dxn3 = _mm(dgp, W["w_gate"], deps=[tok], name="mm_dxn3_gate")
    dxn3 = _mm(dup, W["w_up"], add=dxn3, name="mm_dxn3_up")
    dh2, dh2b, dg_ffn = _norm_bwd(h2, r3, g_ffn, dxn3, dh3, name="norm_ffn_bwd")
    rs_ffn, tok = chips_start(rs_ffn, dh2b)
    do = _mm(dh2b, W["w_o"], tb=True, out_dtype=BF16, deps=[tok], name="mm_d_o")
    G["w_o"] = (o, dh2b)
    dq, dk, dvm = _attn_bwd(q, k, vm, do, name="attn_bwd")
    dkb, dvb = dk.astype(BF16), dvm.astype(BF16)
    G["w_q"], G["w_k"], G["w_v"] = (xn2, dq), (memn, dkb), (memn, dvb)
    rs_att, tok = siblings_start("att", ["w_o", "w_q", "w_k", "w_v"], tok)
    dxn2 = _mm(dq, W["w_q"], tb=True, deps=[tok], name="mm_dxn2")
    dmemn = _mm(dkb, W["w_k"], tb=True, name="mm_dmem_k")
    dmemn = _mm(dvb, W["w_v"], tb=True, add=dmemn, name="mm_dmem_v")
    dg_mem = _norm_bwd(mems, rm, g_mem2, dmemn, None, name="norm_mem_bwd", want_dh=False)
    dh1, dh1b, dg_xattn = _norm_bwd(h1, r2, g_xattn, dxn2, dh2, name="norm_xattn_bwd")
    rs_att, tok = chips_start(rs_att, dh1b)
    dmix = _mm(dh1b, W["w_out"], tb=True, deps=[tok], name="mm_d_mix")
    G["w_out"] = (mix, dh1b)
    rs_out, tok = siblings_start("out", ["w_out"], tok)
    dav, dag, dwa, dba, dlg, dlb = _mix_a_bwd(proj, u1, dmix, wa, ln_a_g, ln_a_b, cw=cw, name="mix_a_bwd")
    dbg, dcg, dbh, dwb = _mix_b_bwd(proj, dmix, wb, cw=cw, sw=cw, name="mix_b_bwd")
    dproj = jnp.concatenate([dav, dag, dbg, dcg, dbh], axis=1)
    G["w_in"] = (xn1, dproj)
    rs_in, tok = siblings_start("in", ["w_in"], tok)
    rs_out, tok = chips_start(rs_out, tok)
    dxn1 = _mm(dproj, W["w_in_lo"], tb=True, out_cols=(0, D), deps=[tok], name="mm_dxn1_lo")
    dxn1 = _mm(dproj, W["w_in_hi"], tb=True, out_cols=(D // 2, D), into=dxn1, name="mm_dxn1_hi")
    dx, _, dg_mix = _norm_bwd(xs, r1, g_mix, dxn1, dh1, name="norm_mix_bwd")
    rs_in, tok = chips_start(rs_in, dx)

    loss_part = jnp.sum(loss_rows).reshape(1, 1)
    small_parts = [dg_mix, dba, dlg, dlb, dg_xattn, dg_mem, dg_ffn, dg_final, dwa, dwb, dwf, loss_part]
    small_shapes = [a.shape for a in small_parts]
    reduced = _allgather_small(_pack(small_parts), name="allreduce_small", reduce=True)
    (sg_mix, sba, slg, slb, sg_xattn, sg_mem, sg_ffn, sg_final, swa, swb, swf, loss_sum) = _unpack(reduced, small_shapes)
    loss = loss_sum.reshape(())
    ga_w = lax.dynamic_slice(swa, (0, p_idx * cw_sh), (KA, cw_sh))
    gb_w = lax.dynamic_slice(swb, (0, p_idx * cw_sh), (KS, cw_sh))
    gf_w = lax.dynamic_slice(swf, (0, p_idx * fp), (KS, f_sh))

    weights = dict(g_mix=g_mix, w_in=w_in, conv_a_w=conv_a_w, conv_a_b=conv_a_b, ln_a_g=ln_a_g, ln_a_b=ln_a_b,
                   conv_b_w=conv_b_w, w_out=w_out, g_xattn=g_xattn, g_mem=g_mem, w_q=w_q, w_k=w_k, w_v=w_v, w_o=w_o,
                   g_ffn=g_ffn, w_gate=w_gate, w_up=w_up, conv_f_w=conv_f_w, w_down=w_down, g_final=g_final)
    m_in = dict(g_mix=m_g_mix, w_in=m_w_in, conv_a_w=m_conv_a_w, conv_a_b=m_conv_a_b, ln_a_g=m_ln_a_g, ln_a_b=m_ln_a_b,
                conv_b_w=m_conv_b_w, w_out=m_w_out, g_xattn=m_g_xattn, g_mem=m_g_mem, w_q=m_w_q, w_k=m_w_k, w_v=m_w_v,
                w_o=m_w_o, g_ffn=m_g_ffn, w_gate=m_w_gate, w_up=m_w_up, conv_f_w=m_conv_f_w, w_down=m_w_down,
                g_final=m_g_final)
    v_in = dict(g_mix=v_g_mix, w_in=v_w_in, conv_a_w=v_conv_a_w, conv_a_b=v_conv_a_b, ln_a_g=v_ln_a_g, ln_a_b=v_ln_a_b,
                conv_b_w=v_conv_b_w, w_out=v_w_out, g_xattn=v_g_xattn, g_mem=v_g_mem, w_q=v_w_q, w_k=v_w_k, w_v=v_w_v,
                w_o=v_w_o, g_ffn=v_g_ffn, w_gate=v_w_gate, w_up=v_w_up, conv_f_w=v_conv_f_w, w_down=v_w_down,
                g_final=v_g_final)
    order = list(weights)
    grads = dict(g_mix=sg_mix, conv_a_w=ga_w, conv_a_b=sba, ln_a_g=slg, ln_a_b=slb, conv_b_w=gb_w, g_xattn=sg_xattn,
                 g_mem=sg_mem, g_ffn=sg_ffn, conv_f_w=gf_w, g_final=sg_final)
    grads = {k: g.reshape(weights[k].shape) for k, g in grads.items()}

    delta, new_m, new_v = {}, {}, {}
    small = [k for k in order if k not in big]
    small_shapes = [weights[k].shape for k in small]
    packed = [_pack([src[k] for k in small]) for src in (weights, grads, m_in, v_in)]
    d_, m_, v_ = _adamw(*packed, name="adamw_small")
    for k, dd, mm_, vv in zip(small, _unpack(d_, small_shapes), _unpack(m_, small_shapes), _unpack(v_, small_shapes)):
        delta[k], new_m[k], new_v[k] = dd, mm_, vv

    transposed = ("w_gate", "w_up")

    def update(shard_grads):
        last = None
        for k, g in shard_grads.items():
            view = t_ if k in transposed else (lambda a: a[0])
            back = (lambda a: jnp.swapaxes(a, 0, 1)[None]) if k in transposed else (lambda a: a[None])
            padded = g.shape != view(weights[k]).shape
            outs = _adamw(view(weights[k]), g, view(m_in[k]), view(v_in[k]), emit_grad=padded, name=f"adamw_{k}")
            delta[k], new_m[k], new_v[k] = back(outs[0]), back(outs[1]), back(outs[2])
            grads[k] = back(outs[3] if padded else g)
            last = outs[0]
        return last

    after = (d_, tok)
    joining = None
    for state in (rs_down, rs_ffn, rs_att, rs_out, rs_in):
        started, token = join_start(state, after)
        after = (token,)
        if joining is not None:
            after = (update(join_finish(joining, token)), token)
        joining = started
    update(join_finish(joining, after[0]))

    return (loss, dx[None], *[grads[k] for k in order], *[delta[k] for k in order],
            *[new_m[k] for k in order], *[new_v[k] for k in order])
```

```python
import jax
import jax.numpy as jnp
from jax import lax
from jax.experimental import pallas as pl
from jax.experimental.pallas import tpu as pltpu

F32 = jnp.float32
BF16 = jnp.bfloat16
EPS = 1e-6
N_HEADS = 4
GROUP = 128
KA = 31
KS = 3
KA_ROWS = 32
KS_ROWS = 8
HALO_A = 32
HALO_S = 8
CHUNK_ROWS = 16
CHUNK_COLS = 256
N_CHIPS = 4
N_DEV = 8
LANES = 128
VMEM_LIMIT = 56 * 1024 * 1024
MM_WIDE_TILE_ELEMS = 4096 * 512
ADAMW_BLOCK_BYTES = 1 << 20
MESH = pl.DeviceIdType.MESH
ANY = pl.BlockSpec(memory_space=pl.ANY)

ADAM_LR = 0.001
ADAM_B1 = 0.9
ADAM_B2 = 0.999
ADAM_EPS = 1e-08
ADAM_WD = 0.01
ADAM_STEP = 10


def _pick(dim, pref, mult=LANES):
    if dim <= pref:
        return dim
    t = (pref // mult) * mult
    while t >= mult:
        if dim % t == 0:
            return t
        t -= mult
    return dim


def _round_up(n, m):
    return ((n + m - 1) // m) * m


def _params(*sem):
    return pltpu.CompilerParams(dimension_semantics=sem, vmem_limit_bytes=VMEM_LIMIT)


def _sigmoid(x):
    return jax.nn.sigmoid(x)


def _mm(a, b, *, name, ta=False, tb=False, out_dtype=F32, add=None, deps=(), half=None, a_k=None, out_cols=None,
        into=None, tm=1024, tn=512, tk=5632):
    if ta:
        K, M = a.shape
    else:
        M, K = a.shape
    if tb:
        N, K2 = b.shape
    else:
        K2, N = b.shape
    k0 = 0
    if a_k is not None:
        k0, K = a_k
    assert K == K2, (a.shape, b.shape)
    half_dim, half_sel = half if half is not None else (None, None)
    if half_dim == "m":
        M //= 2
    elif half_dim == "n":
        N //= 2
    tm, tk = _pick(M, tm), _pick(K, tk)
    nk = K // tk
    tn = _pick(N, 2 * tn if nk == 1 and K * tn <= MM_WIDE_TILE_ELEMS else tn)
    mi, nj = M // tm, N // tn
    n0, n_total = out_cols if out_cols is not None else (0, N)
    assert k0 % tk == 0 and n0 % tn == 0
    koff, noff = k0 // tk, n0 // tn

    def a_map(i, j, k, *s):
        i = i + s[0][0] * mi if half_dim == "m" else i
        return (k + koff, i) if ta else (i, k + koff)

    def b_map(i, j, k, *s):
        j = j + s[0][0] * nj if half_dim == "n" else j
        return (j, k) if tb else (k, j)

    a_spec = pl.BlockSpec((tk, tm) if ta else (tm, tk), a_map)
    b_spec = pl.BlockSpec((tn, tk) if tb else (tk, tn), b_map)
    o_spec = pl.BlockSpec((tm, tn), lambda i, j, k, *s: (i, j + noff))
    add_spec = pl.BlockSpec((tm, tn), lambda i, j, k, *s: (i, j))
    dims = (((0,) if ta else (1,), (1,) if tb else (0,)), ((), ()))
    has_add = add is not None
    n_pre = 0 if half is None else 1
    n_after = len(deps) + (0 if into is None else 1)

    def body(*refs):
        refs = refs[n_pre:]
        a_ref, b_ref = refs[:2]
        add_ref = refs[2] if has_add else None
        o_ref = refs[(3 if has_add else 2) + n_after]
        acc_ref = refs[-1] if nk > 1 else None
        k = pl.program_id(2)
        part = lax.dot_general(a_ref[...], b_ref[...], dims, preferred_element_type=F32)

        def finish(r):
            if add_ref is not None:
                r = add_ref[...] + r
            o_ref[...] = r.astype(out_dtype)

        if nk == 1:
            finish(part)
        else:
            @pl.when(k == 0)
            def _():
                acc_ref[...] = part

            @pl.when(jnp.logical_and(k > 0, k < nk - 1))
            def _():
                acc_ref[...] += part

            @pl.when(k == nk - 1)
            def _():
                finish(acc_ref[...] + part)

    in_specs = [a_spec, b_spec] + ([add_spec] if has_add else []) + [ANY] * n_after
    args = (a, b) + ((add,) if has_add else ()) + tuple(deps) + ((into,) if into is not None else ())
    aliases = {n_pre + len(args) - 1: 0} if into is not None else {}
    scratch = [pltpu.VMEM((tm, tn), F32)] if nk > 1 else []
    grid = (mi, nj, nk)
    if half is None:
        grid_spec = dict(grid=grid, in_specs=in_specs, out_specs=o_spec, scratch_shapes=scratch)
    else:
        args = (half_sel,) + args
        grid_spec = dict(grid_spec=pltpu.PrefetchScalarGridSpec(
            num_scalar_prefetch=1, grid=grid, in_specs=in_specs, out_specs=o_spec, scratch_shapes=scratch))
    return pl.pallas_call(
        body,
        name=name,
        out_shape=jax.ShapeDtypeStruct((M, n_total), out_dtype),
        input_output_aliases=aliases,
        compiler_params=_params("parallel", "parallel", "arbitrary"),
        **grid_spec,
    )(*args)


def _norm_fwd(h, g, *, name):
    T, D = h.shape
    R = _pick(T, 256, 8)

    def body(h_ref, g_ref, xn_ref, r_ref):
        x = h_ref[...]
        r = lax.rsqrt(jnp.mean(x * x, axis=-1, keepdims=True) + EPS)
        xn_ref[...] = ((x * r) * g_ref[...]).astype(BF16)
        r_ref[...] = r

    return pl.pallas_call(
        body,
        name=name,
        out_shape=(jax.ShapeDtypeStruct((T, D), BF16), jax.ShapeDtypeStruct((T, 1), F32)),
        grid=(T // R,),
        in_specs=[pl.BlockSpec((R, D), lambda i: (i, 0)), pl.BlockSpec((1, D), lambda i: (0, 0))],
        out_specs=(pl.BlockSpec((R, D), lambda i: (i, 0)), pl.BlockSpec((R, 1), lambda i: (i, 0))),
        compiler_params=_params("parallel"),
    )(h, g)


def _norm_bwd(h, r, g, dxn, dres, *, name, want_dh=True):
    T, D = h.shape
    R = _pick(T, 128, 8)
    has_res = dres is not None

    def body(*refs):
        h_ref, r_ref, g_ref, dxn_ref = refs[:4]
        pos = 4
        dres_ref = None
        if has_res:
            dres_ref = refs[pos]
            pos += 1
        if want_dh:
            dh_ref, dhb_ref, dg_ref = refs[pos:pos + 3]
        else:
            dg_ref = refs[pos]
        i = pl.program_id(0)
        rr = r_ref[...]
        hn = h_ref[...] * rr
        d = dxn_ref[...].astype(F32)
        gd = d * g_ref[...]
        part = jnp.sum(d * hn, axis=0, keepdims=True)

        @pl.when(i == 0)
        def _():
            dg_ref[...] = part

        @pl.when(i > 0)
        def _():
            dg_ref[...] += part

        if want_dh:
            dh = rr * (gd - hn * jnp.mean(gd * hn, axis=-1, keepdims=True))
            if dres_ref is not None:
                dh = dres_ref[...] + dh
            dh_ref[...] = dh
            dhb_ref[...] = dh.astype(BF16)

    row = pl.BlockSpec((R, D), lambda i: (i, 0))
    vec = pl.BlockSpec((1, D), lambda i: (0, 0))
    in_specs = [row, pl.BlockSpec((R, 1), lambda i: (i, 0)), vec, row] + ([row] if has_res else [])
    args = (h, r, g, dxn) + ((dres,) if has_res else ())
    if want_dh:
        out_shape = (jax.ShapeDtypeStruct((T, D), F32), jax.ShapeDtypeStruct((T, D), BF16), jax.ShapeDtypeStruct((1, D), F32))
        out_specs = (row, row, vec)
    else:
        out_shape = jax.ShapeDtypeStruct((1, D), F32)
        out_specs = vec
    return pl.pallas_call(
        body, name=name, out_shape=out_shape, grid=(T // R,), in_specs=in_specs, out_specs=out_specs,
        compiler_params=_params("arbitrary"),
    )(*args)


def _loss_head(h, g, tgt, *, name):
    T, D = h.shape
    R = _pick(T, 128, 8)

    def body(h_ref, g_ref, t_ref, loss_ref, dh_ref, dhb_ref, dg_ref):
        i = pl.program_id(0)
        x = h_ref[...]
        gg = g_ref[...]
        r = lax.rsqrt(jnp.mean(x * x, axis=-1, keepdims=True) + EPS)
        hn = x * r
        e = hn * gg - t_ref[...]
        loss_ref[...] = 0.5 * jnp.mean(e * e, axis=-1, keepdims=True)
        dy = e * (1.0 / D)
        gd = dy * gg
        dh = r * (gd - hn * jnp.mean(gd * hn, axis=-1, keepdims=True))
        dh_ref[...] = dh
        dhb_ref[...] = dh.astype(BF16)
        part = jnp.sum(dy * hn, axis=0, keepdims=True)

        @pl.when(i == 0)
        def _():
            dg_ref[...] = part

        @pl.when(i > 0)
        def _():
            dg_ref[...] += part

    row = pl.BlockSpec((R, D), lambda i: (i, 0))
    vec = pl.BlockSpec((1, D), lambda i: (0, 0))
    return pl.pallas_call(
        body,
        name=name,
        out_shape=(jax.ShapeDtypeStruct((T, 1), F32), jax.ShapeDtypeStruct((T, D), F32),
                   jax.ShapeDtypeStruct((T, D), BF16), jax.ShapeDtypeStruct((1, D), F32)),
        grid=(T // R,),
        in_specs=[row, vec, row],
        out_specs=(pl.BlockSpec((R, 1), lambda i: (i, 0)), row, row, vec),
        compiler_params=_params("arbitrary"),
    )(h, g, tgt)


_NT = (((1,), (1,)), ((), ()))
_TN = (((0,), (0,)), ((), ()))
_NN = (((1,), (0,)), ((), ()))


def _softmax_rows(s):
    m = jnp.max(s, axis=-1, keepdims=True)
    e = jnp.exp(s - m)
    return e / jnp.sum(e, axis=-1, keepdims=True)


def _attn_fwd(q, k, v, *, name):
    T, D = q.shape
    ML = k.shape[0]
    dh = D // N_HEADS
    scale = dh ** -0.5
    R = _pick(T, 512, 16)

    def body(q_ref, k_ref, v_ref, o_ref):
        s = lax.dot_general(q_ref[...], k_ref[...], _NT, preferred_element_type=F32) * scale
        p = _softmax_rows(s)
        o_ref[...] = lax.dot_general(p.astype(BF16), v_ref[...], _NN, preferred_element_type=F32).astype(BF16)

    qs = pl.BlockSpec((R, dh), lambda i, h: (i, h))
    ks = pl.BlockSpec((ML, dh), lambda i, h: (0, h))
    return pl.pallas_call(
        body, name=name, out_shape=jax.ShapeDtypeStruct((T, D), BF16), grid=(T // R, N_HEADS),
        in_specs=[qs, ks, ks], out_specs=qs, compiler_params=_params("parallel", "parallel"),
    )(q, k, v)


def _attn_bwd(q, k, v, do, *, name):
    T, D = q.shape
    ML = k.shape[0]
    dh = D // N_HEADS
    scale = dh ** -0.5
    R = _pick(T, 512, 16)

    def body(q_ref, k_ref, v_ref, do_ref, dq_ref, dk_ref, dv_ref):
        i = pl.program_id(1)
        qq, kk, vv, dd = q_ref[...], k_ref[...], v_ref[...], do_ref[...]
        s = lax.dot_general(qq, kk, _NT, preferred_element_type=F32) * scale
        p = _softmax_rows(s)
        dp = lax.dot_general(dd, vv, _NT, preferred_element_type=F32)
        dv_part = lax.dot_general(p.astype(BF16), dd, _TN, preferred_element_type=F32)
        ds = (p * (dp - jnp.sum(p * dp, axis=-1, keepdims=True)) * scale).astype(BF16)
        dq_ref[...] = lax.dot_general(ds, kk, _NN, preferred_element_type=F32).astype(BF16)
        dk_part = lax.dot_general(ds, qq, _TN, preferred_element_type=F32)

        @pl.when(i == 0)
        def _():
            dk_ref[...] = dk_part
            dv_ref[...] = dv_part

        @pl.when(i > 0)
        def _():
            dk_ref[...] += dk_part
            dv_ref[...] += dv_part

    qs = pl.BlockSpec((R, dh), lambda h, i: (i, h))
    ks = pl.BlockSpec((ML, dh), lambda h, i: (0, h))
    return pl.pallas_call(
        body,
        name=name,
        out_shape=(jax.ShapeDtypeStruct((T, D), BF16), jax.ShapeDtypeStruct((ML, D), F32), jax.ShapeDtypeStruct((ML, D), F32)),
        grid=(N_HEADS, T // R),
        in_specs=[qs, ks, ks, qs],
        out_specs=(qs, ks, ks),
        compiler_params=_params("parallel", "arbitrary"),
    )(q, k, v, do)


def _prev_halo(R, halo, CT, col):
    per = R // halo
    return lambda i, j: (jnp.maximum(i * per - 1, 0), col(j))


def _next_halo(R, halo, T, col):
    per = R // halo
    last = T // halo - 1
    return lambda i, j: (jnp.minimum((i + 1) * per, last), col(j))


def _swap(f):
    return lambda j, i: f(i, j)


def _ffn_act_fwd(gp, up, wf, *, name, deps=()):
    T, F = gp.shape
    R = _pick(T, 256, CHUNK_ROWS)
    CT = _pick(F, 1024, CHUNK_COLS)
    H = HALO_S
    RC, CC = CHUNK_ROWS, min(CHUNK_COLS, CT)

    def body(g_ref, gh_ref, u_ref, w_ref, *rest):
        f_ref, ext = rest[len(deps):]
        i = pl.program_id(0)
        ext[pl.ds(0, H), :] = jnp.where(i > 0, gh_ref[...], 0.0)
        ext[pl.ds(H, R), :] = g_ref[...]
        for c0 in range(0, CT, CC):
            cols = pl.ds(c0, CC)
            w0, w1, w2 = [jnp.broadcast_to(w_ref[pl.ds(k, 1), cols], (RC, CC)) for k in range(KS)]
            for r0 in range(0, R, RC):
                g = ext[pl.ds(r0 + H - 2, RC), cols] * w0
                g = g + ext[pl.ds(r0 + H - 1, RC), cols] * w1
                g = g + ext[pl.ds(r0 + H, RC), cols] * w2
                f_ref[pl.ds(r0, RC), cols] = (g * _sigmoid(g) * u_ref[pl.ds(r0, RC), cols]).astype(BF16)

    main = pl.BlockSpec((R, CT), lambda i, j: (i, j))
    return pl.pallas_call(
        body,
        name=name,
        out_shape=jax.ShapeDtypeStruct((T, F), BF16),
        grid=(T // R, F // CT),
        in_specs=[main, pl.BlockSpec((H, CT), _prev_halo(R, H, CT, lambda j: j)), main,
                  pl.BlockSpec((KS_ROWS, CT), lambda i, j: (0, j))] + [ANY] * len(deps),
        out_specs=main,
        scratch_shapes=[pltpu.VMEM((R + H, CT), F32)],
        compiler_params=_params("parallel", "parallel"),
    )(gp, gp, up, wf, *deps)


def _ffn_act_bwd(gp, up, df, wf, *, name):
    T, F = gp.shape
    R = _pick(T, 256, CHUNK_ROWS)
    CT = _pick(F, 512, CHUNK_COLS)
    H = HALO_S
    HB = 16
    n_t = T // R
    RC, CC = CHUNK_ROWS, min(CHUNK_COLS, CT)

    def body(g_ref, gp_ref, gn_ref, u_ref, un_ref, d_ref, dn_ref, w_ref, dg_out, du_out, dw_ref, ext, dge):
        i = pl.program_id(1)
        last = i == n_t - 1
        ext[pl.ds(0, H), :] = jnp.where(i > 0, gp_ref[...], 0.0)
        ext[pl.ds(H, R), :] = g_ref[...]
        ext[pl.ds(H + R, H), :] = gn_ref[...]

        def dact(g, u, d):
            sg = _sigmoid(g)
            return d * u * (sg * (1.0 + g * (1.0 - sg))), d * (g * sg)

        dw_rows = [[] for _ in range(KS)]
        for c0 in range(0, CT, CC):
            cols = pl.ds(c0, CC)
            w = [jnp.broadcast_to(w_ref[pl.ds(k, 1), cols], (RC, CC)) for k in range(KS)]
            acc = [jnp.zeros((RC, CC), F32) for _ in range(KS)]
            for r0 in range(0, R, RC):
                rows = pl.ds(r0, RC)
                taps = [ext[pl.ds(r0 + H - 2 + k, RC), cols] for k in range(KS)]
                g = taps[0] * w[0] + taps[1] * w[1] + taps[2] * w[2]
                dg, du = dact(g, u_ref[rows, cols], d_ref[rows, cols].astype(F32))
                du_out[rows, cols] = du.astype(BF16)
                dge[rows, cols] = dg
                acc = [a + dg * t for a, t in zip(acc, taps)]
            taps = [ext[pl.ds(R + H - 2 + k, H), cols] for k in range(KS)]
            g = taps[0] * w[0][:H] + taps[1] * w[1][:H] + taps[2] * w[2][:H]
            d_next = jnp.where(last, 0.0, dn_ref[pl.ds(0, H), cols].astype(F32))
            dg_next, _ = dact(g, un_ref[:, cols], d_next)
            dge[pl.ds(R, H), cols] = jnp.where(last, 0.0, dg_next)
            for r0 in range(0, R, RC):
                dgp = (dge[pl.ds(r0, RC), cols] * w[2] + dge[pl.ds(r0 + 1, RC), cols] * w[1]
                       + dge[pl.ds(r0 + 2, RC), cols] * w[0])
                dg_out[pl.ds(r0, RC), cols] = dgp.astype(BF16)
            for k in range(KS):
                dw_rows[k].append(jnp.sum(acc[k], axis=0, keepdims=True))
        rows = [jnp.concatenate(r, axis=1) if len(r) > 1 else r[0] for r in dw_rows]
        rows.append(jnp.zeros((KS_ROWS - KS, CT), F32))
        _acc_rows_block(dw_ref, i, rows)

    col = lambda j: j
    main = pl.BlockSpec((R, CT), lambda j, i: (i, j))
    prev8 = pl.BlockSpec((H, CT), _swap(_prev_halo(R, H, CT, col)))
    next8 = pl.BlockSpec((H, CT), _swap(_next_halo(R, H, T, col)))
    next16 = pl.BlockSpec((HB, CT), _swap(_next_halo(R, HB, T, col)))
    wspec = pl.BlockSpec((KS_ROWS, CT), lambda j, i: (0, j))
    return pl.pallas_call(
        body,
        name=name,
        out_shape=(jax.ShapeDtypeStruct((T, F), BF16), jax.ShapeDtypeStruct((T, F), BF16), jax.ShapeDtypeStruct((KS_ROWS, F), F32)),
        grid=(F // CT, n_t),
        in_specs=[main, prev8, next8, main, next8, main, next16, wspec],
        out_specs=(main, main, wspec),
        scratch_shapes=[pltpu.VMEM((R + 2 * H, CT), F32), pltpu.VMEM((R + H, CT), F32)],
        compiler_params=_params("parallel", "arbitrary"),
    )(gp, gp, gp, up, up, df, df, wf)


def _acc_rows_block(ref, i, rows):
    *singles, pad = rows

    @pl.when(i == 0)
    def _():
        for k, row in enumerate(singles):
            ref[pl.ds(k, 1), :] = row
        ref[pl.ds(len(singles), pad.shape[0]), :] = pad

    @pl.when(i > 0)
    def _():
        for k, row in enumerate(singles):
            ref[pl.ds(k, 1), :] += row


def _group_stats(x):
    mu = jnp.mean(x, axis=-1, keepdims=True)
    xc = x - mu
    var = jnp.mean(xc * xc, axis=-1, keepdims=True)
    return xc, lax.rsqrt(var + EPS)


def _shifted_rows(rolled, x, cols):
    n = x.shape[0]
    for b in range(1, 8):
        rolled[b - 1, :, cols] = pltpu.roll(x, n - b, axis=0)


def _window(src, rolled, off, rows, cols):
    b = off % 8
    if b == 0:
        return src[pl.ds(off, rows), cols]
    return rolled[b - 1, pl.ds(off - b, rows), cols]


def _mix_a_fwd(proj, wa, ba, lg, lb, *, cw, name, deps=()):
    T = proj.shape[0]
    R = _pick(T, 128, HALO_A)
    CT = _pick(cw, 256)
    nc = cw // CT
    H = HALO_A

    def body(av_ref, ag_ref, avh_ref, agh_ref, w_ref, b_ref, lg_ref, lb_ref, *rest):
        u3_ref, u1_ref, ext, rolled = rest[len(deps):]
        i = pl.program_id(0)
        for gi in range(CT // GROUP):
            cols = pl.ds(gi * GROUP, GROUP)
            ext[pl.ds(0, H), cols] = jnp.where(i > 0, avh_ref[:, cols] * _sigmoid(agh_ref[:, cols]), 0.0)
            ext[pl.ds(H, R), cols] = av_ref[:, cols] * _sigmoid(ag_ref[:, cols])
            _shifted_rows(rolled, ext[:, cols], cols)
            acc = _window(ext, rolled, H - (KA - 1), R, cols) * w_ref[pl.ds(0, 1), cols]
            for k in range(1, KA):
                acc = acc + _window(ext, rolled, H - (KA - 1) + k, R, cols) * w_ref[pl.ds(k, 1), cols]
            u1 = acc + b_ref[:, cols]
            u1_ref[:, cols] = u1
            xc, rstd = _group_stats(u1)
            u2 = (xc * rstd) * lg_ref[:, cols] + lb_ref[:, cols]
            u3_ref[:, cols] = (u2 * _sigmoid(u2)).astype(BF16)

    main_v = pl.BlockSpec((R, CT), lambda i, j: (i, j))
    main_g = pl.BlockSpec((R, CT), lambda i, j: (i, j + nc))
    halo_v = pl.BlockSpec((H, CT), _prev_halo(R, H, CT, lambda j: j))
    halo_g = pl.BlockSpec((H, CT), _prev_halo(R, H, CT, lambda j: j + nc))
    vec = pl.BlockSpec((1, CT), lambda i, j: (0, j))
    out = pl.BlockSpec((R, CT), lambda i, j: (i, j))
    return pl.pallas_call(
        body,
        name=name,
        out_shape=(jax.ShapeDtypeStruct((T, cw), BF16), jax.ShapeDtypeStruct((T, cw), F32)),
        grid=(T // R, nc),
        in_specs=[main_v, main_g, halo_v, halo_g, pl.BlockSpec((KA_ROWS, CT), lambda i, j: (0, j)), vec, vec, vec]
        + [ANY] * len(deps),
        out_specs=(out, out),
        scratch_shapes=[pltpu.VMEM((R + H, CT), F32), pltpu.VMEM((7, R + H, CT), F32)],
        compiler_params=_params("parallel", "parallel"),
    )(proj, proj, proj, proj, wa, ba, lg, lb, *deps)


def _mix_a_bwd(proj, u1, dmix, wa, lg, lb, *, cw, name):
    T = proj.shape[0]
    R = _pick(T, 128, HALO_A)
    CT = _pick(cw, 256)
    nc = cw // CT
    H = HALO_A
    n_t = T // R
    NG = CT // GROUP

    def body(av_ref, ag_ref, avh_ref, agh_ref, u1_ref, u1n_ref, d3_ref, d3n_ref, w_ref, lg_ref, lb_ref,
             dav_ref, dag_ref, dw_ref, db_ref, dlg_ref, dlb_ref, ext, d1, ext_rolled, d1_rolled):
        i = pl.program_id(1)
        last = i == n_t - 1

        def ln_bwd(u1, d3, cols):
            xc, rstd = _group_stats(u1)
            xh = xc * rstd
            g = lg_ref[:, cols]
            u2 = xh * g + lb_ref[:, cols]
            sg = _sigmoid(u2)
            du2 = d3 * (sg * (1.0 + u2 * (1.0 - sg)))
            dxh = du2 * g
            du1 = rstd * (dxh - jnp.mean(dxh, axis=-1, keepdims=True) - xh * jnp.mean(dxh * xh, axis=-1, keepdims=True))
            return du1, du2 * xh, du2

        def lanes(parts):
            return jnp.concatenate(parts, axis=1) if len(parts) > 1 else parts[0]

        dlg_parts, dlb_parts, db_parts = [], [], []
        dw_parts = [[] for _ in range(KA)]
        for gi in range(NG):
            cols = pl.ds(gi * GROUP, GROUP)
            du1, dlg, dlb = ln_bwd(u1_ref[:, cols], d3_ref[:, cols], cols)
            d1[pl.ds(0, R), cols] = du1
            dlg_parts.append(jnp.sum(dlg, axis=0, keepdims=True))
            dlb_parts.append(jnp.sum(dlb, axis=0, keepdims=True))
            db_parts.append(jnp.sum(du1, axis=0, keepdims=True))
            du1n, _, _ = ln_bwd(u1n_ref[:, cols], jnp.where(last, 0.0, d3n_ref[:, cols]), cols)
            d1[pl.ds(R, H), cols] = jnp.where(last, 0.0, du1n)

            _shifted_rows(d1_rolled, d1[:, cols], cols)
            du0 = _window(d1, d1_rolled, KA - 1, R, cols) * w_ref[pl.ds(0, 1), cols]
            for k in range(1, KA):
                du0 = du0 + _window(d1, d1_rolled, KA - 1 - k, R, cols) * w_ref[pl.ds(k, 1), cols]
            av = av_ref[:, cols]
            sg = _sigmoid(ag_ref[:, cols])
            dav_ref[:, cols] = (du0 * sg).astype(BF16)
            dag_ref[:, cols] = (du0 * av * sg * (1.0 - sg)).astype(BF16)

            ext[pl.ds(0, H), cols] = jnp.where(i > 0, avh_ref[:, cols] * _sigmoid(agh_ref[:, cols]), 0.0)
            ext[pl.ds(H, R), cols] = av * sg
            _shifted_rows(ext_rolled, ext[:, cols], cols)
            d1_main = d1[pl.ds(0, R), cols]
            for k in range(KA):
                window = _window(ext, ext_rolled, H - (KA - 1) + k, R, cols)
                dw_parts[k].append(jnp.sum(d1_main * window, axis=0, keepdims=True))

        dlg_row, dlb_row, db_row = lanes(dlg_parts), lanes(dlb_parts), lanes(db_parts)

        @pl.when(i == 0)
        def _():
            dlg_ref[...] = dlg_row
            dlb_ref[...] = dlb_row
            db_ref[...] = db_row

        @pl.when(i > 0)
        def _():
            dlg_ref[...] += dlg_row
            dlb_ref[...] += dlb_row
            db_ref[...] += db_row

        rows = [lanes(p) for p in dw_parts]
        rows.append(jnp.zeros((KA_ROWS - KA, CT), F32))
        _acc_rows_block(dw_ref, i, rows)

    cv = lambda j: j
    cg = lambda j: j + nc
    main_v = pl.BlockSpec((R, CT), lambda j, i: (i, j))
    main_g = pl.BlockSpec((R, CT), lambda j, i: (i, j + nc))
    prev_v = pl.BlockSpec((H, CT), _swap(_prev_halo(R, H, CT, cv)))
    prev_g = pl.BlockSpec((H, CT), _swap(_prev_halo(R, H, CT, cg)))
    nxt = pl.BlockSpec((H, CT), _swap(_next_halo(R, H, T, cv)))
    wspec = pl.BlockSpec((KA_ROWS, CT), lambda j, i: (0, j))
    vec = pl.BlockSpec((1, CT), lambda j, i: (0, j))
    vshape = jax.ShapeDtypeStruct((1, cw), F32)
    return pl.pallas_call(
        body,
        name=name,
        out_shape=(jax.ShapeDtypeStruct((T, cw), BF16), jax.ShapeDtypeStruct((T, cw), BF16),
                   jax.ShapeDtypeStruct((KA_ROWS, cw), F32), vshape, vshape, vshape),
        grid=(nc, n_t),
        in_specs=[main_v, main_g, prev_v, prev_g, main_v, nxt, main_v, nxt, wspec, vec, vec],
        out_specs=(main_v, main_v, wspec, vec, vec, vec),
        scratch_shapes=[pltpu.VMEM((R + H, CT), F32), pltpu.VMEM((R + H, CT), F32),
                        pltpu.VMEM((7, R + H, CT), F32), pltpu.VMEM((7, R + H, CT), F32)],
        compiler_params=_params("parallel", "arbitrary"),
    )(proj, proj, proj, proj, u1, u1, dmix, dmix, wa, lg, lb)


def _mix_b_fwd(proj, wb, *, cw, sw, name):
    T = proj.shape[0]
    R = _pick(T, 256, 8)
    CT = _pick(sw, 512)
    nb, nc, nh = (2 * cw) // CT, (2 * cw + sw) // CT, (2 * cw + 2 * sw) // CT
    H = HALO_S

    def body(b_ref, c_ref, h_ref, ch_ref, hh_ref, w_ref, v_ref, ext):
        i = pl.program_id(0)
        ext[pl.ds(0, H), :] = jnp.where(i > 0, ch_ref[...] * hh_ref[...], 0.0)
        ext[pl.ds(H, R), :] = c_ref[...] * h_ref[...]
        w = w_ref[...]
        zc = ext[pl.ds(H - 2, R), :] * w[0:1, :]
        zc = zc + ext[pl.ds(H - 1, R), :] * w[1:2, :]
        zc = zc + ext[pl.ds(H, R), :] * w[2:3, :]
        v_ref[...] = (b_ref[...] * zc).astype(BF16)

    def main(off):
        return pl.BlockSpec((R, CT), lambda i, j: (i, j + off))

    def prev(off):
        return pl.BlockSpec((H, CT), _prev_halo(R, H, CT, lambda j: j + off))

    return pl.pallas_call(
        body,
        name=name,
        out_shape=jax.ShapeDtypeStruct((T, sw), BF16),
        grid=(T // R, sw // CT),
        in_specs=[main(nb), main(nc), main(nh), prev(nc), prev(nh), pl.BlockSpec((KS_ROWS, CT), lambda i, j: (0, j))],
        out_specs=pl.BlockSpec((R, CT), lambda i, j: (i, j)),
        scratch_shapes=[pltpu.VMEM((R + H, CT), F32)],
        compiler_params=_params("parallel", "parallel"),
    )(proj, proj, proj, proj, proj, wb)


def _mix_b_bwd(proj, dmix, wb, *, cw, sw, name):
    T = proj.shape[0]
    R = _pick(T, 256, 8)
    CT = _pick(sw, 512)
    nb, nc, nh = (2 * cw) // CT, (2 * cw + sw) // CT, (2 * cw + 2 * sw) // CT
    nd = cw // CT
    H = HALO_S
    n_t = T // R

    def body(b_ref, bn_ref, c_ref, cp_ref, cn_ref, h_ref, hp_ref, hn_ref, d_ref, dn_ref, w_ref,
             db_ref, dc_ref, dhh_ref, dw_ref, ext, dze):
        i = pl.program_id(1)
        last = i == n_t - 1
        cc, hh = c_ref[...], h_ref[...]
        ext[pl.ds(0, H), :] = jnp.where(i > 0, cp_ref[...] * hp_ref[...], 0.0)
        ext[pl.ds(H, R), :] = cc * hh
        ext[pl.ds(H + R, H), :] = cn_ref[...] * hn_ref[...]
        w = w_ref[...]
        w0, w1, w2 = w[0:1, :], w[1:2, :], w[2:3, :]

        def conv(start, n):
            z = ext[pl.ds(start + H - 2, n), :] * w0
            z = z + ext[pl.ds(start + H - 1, n), :] * w1
            return z + ext[pl.ds(start + H, n), :] * w2

        d_main = d_ref[...]
        db_ref[...] = (d_main * conv(0, R)).astype(BF16)
        dzc_main = d_main * b_ref[...]
        dze[pl.ds(0, R), :] = dzc_main
        dze[pl.ds(R, H), :] = jnp.where(last, 0.0, dn_ref[...] * bn_ref[...])
        dz = dze[pl.ds(0, R), :] * w2 + dze[pl.ds(1, R), :] * w1 + dze[pl.ds(2, R), :] * w0
        dc_ref[...] = (dz * hh).astype(BF16)
        dhh_ref[...] = (dz * cc).astype(BF16)
        rows = [jnp.sum(dzc_main * ext[pl.ds(H - 2 + k, R), :], axis=0, keepdims=True) for k in range(KS)]
        rows.append(jnp.zeros((KS_ROWS - KS, CT), F32))
        _acc_rows_block(dw_ref, i, rows)

    def main(off):
        return pl.BlockSpec((R, CT), lambda j, i: (i, j + off))

    def prev(off):
        return pl.BlockSpec((H, CT), _swap(_prev_halo(R, H, CT, lambda j: j + off)))

    def nxt(off):
        return pl.BlockSpec((H, CT), _swap(_next_halo(R, H, T, lambda j: j + off)))

    out = pl.BlockSpec((R, CT), lambda j, i: (i, j))
    wspec = pl.BlockSpec((KS_ROWS, CT), lambda j, i: (0, j))
    act = jax.ShapeDtypeStruct((T, sw), BF16)
    return pl.pallas_call(
        body,
        name=name,
        out_shape=(act, act, act, jax.ShapeDtypeStruct((KS_ROWS, sw), F32)),
        grid=(sw // CT, n_t),
        in_specs=[main(nb), nxt(nb), main(nc), prev(nc), nxt(nc), main(nh), prev(nh), nxt(nh), main(nd), nxt(nd), wspec],
        out_specs=(out, out, out, wspec),
        scratch_shapes=[pltpu.VMEM((R + 2 * H, CT), F32), pltpu.VMEM((R + H, CT), F32)],
        compiler_params=_params("parallel", "arbitrary"),
    )(proj, proj, proj, proj, proj, proj, proj, proj, dmix, dmix, wb)


def _adamw(w, g, m, v, *, name, emit_grad=False):
    Rr, Cc = w.shape
    R = _pick(Rr, max(8, ADAMW_BLOCK_BYTES // (4 * Cc)), 8)

    def body(w_ref, g_ref, m_ref, v_ref, d_ref, mo_ref, vo_ref, *g_out):
        g = g_ref[...]
        m2 = ADAM_B1 * m_ref[...] + (1.0 - ADAM_B1) * g
        v2 = ADAM_B2 * v_ref[...] + (1.0 - ADAM_B2) * (g * g)
        m_hat = m2 / (1.0 - ADAM_B1 ** ADAM_STEP)
        v_hat = v2 / (1.0 - ADAM_B2 ** ADAM_STEP)
        d_ref[...] = -ADAM_LR * (m_hat / (jnp.sqrt(v_hat) + ADAM_EPS) + ADAM_WD * w_ref[...])
        mo_ref[...] = m2
        vo_ref[...] = v2
        if emit_grad:
            g_out[0][...] = g

    spec = pl.BlockSpec((R, Cc), lambda i: (i, 0))
    shp = jax.ShapeDtypeStruct((Rr, Cc), F32)
    n_out = 4 if emit_grad else 3
    return pl.pallas_call(
        body, name=name, out_shape=(shp,) * n_out, grid=(Rr // R,), in_specs=[spec] * 4, out_specs=(spec,) * n_out,
        compiler_params=_params("parallel"),
    )(w, g, m, v)


def _sum_chips(cs, rb, pc, lay, *, name):
    _, Rr, Cc = rb.shape
    tr, tc = _pick(Rr, 512, 16), _pick(Cc, 2048)
    nr, ncol = Rr // tr, Cc // tc
    if lay.axis == 0:
        own_map = lambda i, j, s: (i + s[0] * nr, j)
        out_map = lambda i, j, s: (i, j + s[1] * ncol)
    else:
        own_map = lambda i, j, s: (i, j + s[0] * ncol)
        out_map = lambda i, j, s: (i + s[1] * nr, j)

    def body(s_ref, own_ref, rb_ref, o_ref):
        acc = own_ref[...].astype(F32)
        for j in range(N_CHIPS - 1):
            acc = acc + rb_ref[j].astype(F32)
        o_ref[...] = acc

    return pl.pallas_call(
        body,
        name=name,
        out_shape=jax.ShapeDtypeStruct(lay.shard_shape(), F32),
        grid_spec=pltpu.PrefetchScalarGridSpec(
            num_scalar_prefetch=1,
            grid=(nr, ncol),
            in_specs=[pl.BlockSpec((tr, tc), own_map), pl.BlockSpec((N_CHIPS - 1, tr, tc), lambda i, j, s: (0, i, j))],
            out_specs=pl.BlockSpec((tr, tc), out_map)),
        compiler_params=_params("parallel", "parallel"),
    )(pc, cs, rb)


def _place():
    x, y, c = lax.axis_index("x"), lax.axis_index("y"), lax.axis_index("c")
    return x, y, c, 2 * x + y


def _other_chips(x, y):
    return [(1 - x, y, 2 * (1 - x) + y), (x, 1 - y, 2 * x + (1 - y)), (1 - x, 1 - y, 2 * (1 - x) + (1 - y))]


def _allgather_small(buf, *, name, reduce):
    S = buf.shape[0]

    def body(x_ref, o_ref, gat, send_sems, recv_sems):
        x, y, c, _ = _place()
        me = 4 * x + 2 * y + c
        gat[me] = x_ref[...]
        copies = []
        for k in range(1, N_DEV):
            fx, fy, fc = (k >> 2) & 1, (k >> 1) & 1, k & 1
            px = 1 - x if fx else x
            py = 1 - y if fy else y
            pc = 1 - c if fc else c
            peer = 4 * px + 2 * py + pc
            send = pltpu.make_async_remote_copy(
                src_ref=x_ref, dst_ref=gat.at[me], send_sem=send_sems.at[k - 1], recv_sem=recv_sems.at[k - 1],
                device_id=(px, py, pc), device_id_type=MESH)
            send.start()
            arrival = pltpu.make_async_remote_copy(
                src_ref=x_ref, dst_ref=gat.at[peer], send_sem=send_sems.at[k - 1], recv_sem=recv_sems.at[k - 1],
                device_id=(px, py, pc), device_id_type=MESH)
            copies.append((send, arrival))
        for send, arrival in copies:
            arrival.wait_recv()
        for send, arrival in copies:
            send.wait_send()
        if reduce:
            acc = gat[0]
            for d in range(1, N_DEV):
                acc = acc + gat[d]
            o_ref[...] = acc
        else:
            o_ref[...] = gat[...]

    out_shape = jax.ShapeDtypeStruct((S, LANES) if reduce else (N_DEV, S, LANES), F32)
    return pl.pallas_call(
        body,
        name=name,
        out_shape=out_shape,
        in_specs=[pl.BlockSpec(memory_space=pltpu.VMEM)],
        out_specs=pl.BlockSpec(memory_space=pltpu.VMEM),
        scratch_shapes=[pltpu.VMEM((N_DEV, S, LANES), F32), pltpu.SemaphoreType.DMA((N_DEV - 1,)),
                        pltpu.SemaphoreType.DMA((N_DEV - 1,))],
        compiler_params=pltpu.CompilerParams(vmem_limit_bytes=VMEM_LIMIT),
    )(buf)


class _Sharded:
    def __init__(self, shape, axis):
        self.shape = shape
        self.axis = axis
        self.block = shape[axis] // N_CHIPS
        self.half = shape[1 - axis] // 2

    def _sl(self, along, across):
        return (along, across) if self.axis == 0 else (across, along)

    def block_slice(self, q):
        return self._sl(pl.ds(q * self.block, self.block), pl.ds(0, self.shape[1 - self.axis]))

    def block_half_slice(self, q, c):
        return self._sl(pl.ds(q * self.block, self.block), pl.ds(c * self.half, self.half))

    def shard_half_slice(self, c):
        return self._sl(pl.ds(0, self.block), pl.ds(c * self.half, self.half))

    def shard_shape(self):
        return self._sl(self.block, self.shape[1 - self.axis])

    def half_shape(self):
        return self._sl(self.shape[self.axis], self.half)

    def block_half_shape(self):
        return self._sl(self.block, self.half)

    def block_in_half_slice(self, q):
        return self._sl(pl.ds(q * self.block, self.block), pl.ds(0, self.half))


def _at(ref, sl):
    return ref.at[sl[0], sl[1]]


class _Copy:
    def __init__(self, src, dst, arrive, dev):
        self.src, self.dst, self.arrive, self.dev = src, dst, arrive, dev


HBM = pl.BlockSpec(memory_space=pltpu.HBM)
SEM = pl.BlockSpec(memory_space=pltpu.SEMAPHORE)
EFFECT = pltpu.SideEffectType.DATAFLOW_SIDE_EFFECTING


def _exchange_start(srcs, land_shapes, plan, after, *, name):
    ns, nl = len(srcs), len(land_shapes)
    n_copies = len(plan([None] * ns, [None] * nl, dry=True))

    def body(*refs):
        src_refs, land_refs = refs[:ns], refs[ns:ns + nl]
        send_sems, recv_sems = refs[ns + nl + 1], refs[ns + nl + 2]
        token = refs[-1]
        for k, cp in enumerate(plan(src_refs, land_refs)):
            pltpu.make_async_remote_copy(
                src_ref=cp.src, dst_ref=cp.dst, send_sem=send_sems.at[k], recv_sem=recv_sems.at[k],
                device_id=cp.dev, device_id_type=MESH).start()
        token[...] = jnp.zeros_like(token)

    sem = pltpu.SemaphoreType.DMA((n_copies,))
    lands = [pltpu.with_memory_space_constraint(lax.empty(shp, dt), pltpu.HBM) for shp, dt in land_shapes]
    srcs = [pltpu.with_memory_space_constraint(a, pltpu.HBM) for a in srcs]
    thru = [pltpu.HBM(a.shape, a.dtype) for a in srcs + lands]
    outs = pl.pallas_call(
        body,
        name=name,
        out_shape=(sem, sem, *thru, jax.ShapeDtypeStruct((8, LANES), F32)),
        in_specs=[HBM] * (ns + nl) + [ANY],
        out_specs=(SEM, SEM, *[HBM] * (ns + nl), pl.BlockSpec(memory_space=pltpu.VMEM)),
        input_output_aliases={i: 2 + i for i in range(ns + nl)},
        compiler_params=pltpu.CompilerParams(has_side_effects=EFFECT),
    )(*srcs, *lands, after)
    return outs[0], outs[1], list(outs[2:2 + ns]), list(outs[2 + ns:2 + ns + nl]), outs[-1]


def _exchange_wait(send_sems, recv_sems, srcs, lands, plan, after, *, name):
    ns, nl = len(srcs), len(lands)
    after = tuple(after) if isinstance(after, (tuple, list)) else (after,)

    def body(*refs):
        src_refs, land_refs = refs[:ns], refs[ns:ns + nl]
        send_sems, recv_sems = refs[ns + nl], refs[ns + nl + 1]
        copies = [
            pltpu.make_async_remote_copy(
                src_ref=cp.src, dst_ref=cp.arrive, send_sem=send_sems.at[k], recv_sem=recv_sems.at[k],
                device_id=cp.dev, device_id_type=MESH)
            for k, cp in enumerate(plan(src_refs, land_refs))
        ]
        for cp in copies:
            cp.wait_recv()
        for cp in copies:
            cp.wait_send()

    thru = [pltpu.HBM(a.shape, a.dtype) for a in list(srcs) + list(lands)]
    outs = pl.pallas_call(
        body,
        name=name,
        out_shape=tuple(thru),
        in_specs=[HBM] * (ns + nl) + [SEM, SEM] + [ANY] * len(after),
        out_specs=tuple([HBM] * (ns + nl)),
        input_output_aliases={i: i for i in range(ns + nl)},
        compiler_params=pltpu.CompilerParams(has_side_effects=EFFECT),
    )(*srcs, *lands, send_sems, recv_sems, *after)
    return list(outs[:ns]), list(outs[ns:])


def _gather_plan(layouts):
    def plan(srcs, lands, dry=False):
        if dry:
            return [None] * (4 * len(layouts))
        x, y, c, p = _place()
        copies = []
        for s, g, lay in zip(srcs, lands, layouts):
            own = _at(g, lay.block_slice(p))
            copies.append(_Copy(s, own, own, (x, y, 1 - c)))
            for qx, qy, q in _other_chips(x, y):
                copies.append(_Copy(_at(s, lay.shard_half_slice(c)), _at(g, lay.block_half_slice(p, c)),
                                    _at(g, lay.block_half_slice(q, c)), (qx, qy, c)))
        return copies

    return plan


def _forward_plan(layouts):
    def plan(srcs, lands, dry=False):
        if dry:
            return [None] * (3 * len(layouts))
        x, y, c, _ = _place()
        copies = []
        for g, lay in zip(srcs, layouts):
            for qx, qy, q in _other_chips(x, y):
                landed = _at(g, lay.block_half_slice(q, c))
                copies.append(_Copy(landed, landed, _at(g, lay.block_half_slice(q, 1 - c)), (x, y, 1 - c)))
        return copies

    return plan


def _sibling_plan(n):
    def plan(srcs, lands, dry=False):
        if dry:
            return [None] * n
        x, y, c, _ = _place()
        return [_Copy(s, ra, ra, (x, y, 1 - c)) for s, ra in zip(srcs, lands)]

    return plan


def _chips_plan(layouts):
    def plan(srcs, lands, dry=False):
        if dry:
            return [None] * (3 * len(layouts))
        x, y, c, _ = _place()
        copies = []
        for s, rb, lay in zip(srcs, lands, layouts):
            for j, (qx, qy, q) in enumerate(_other_chips(x, y)):
                copies.append(_Copy(_at(s, lay.block_in_half_slice(q)), rb.at[j], rb.at[j], (qx, qy, c)))
        return copies

    return plan


def _join_plan(layouts):
    def plan(srcs, lands, dry=False):
        if dry:
            return [None] * len(layouts)
        x, y, c, _ = _place()
        copies = []
        for g, lay in zip(srcs, layouts):
            mine = _at(g, lay.shard_half_slice(c))
            copies.append(_Copy(mine, mine, _at(g, lay.shard_half_slice(1 - c)), (x, y, 1 - c)))
        return copies

    return plan


def _pack(arrays):
    flat = [a.reshape(-1) for a in arrays]
    sizes = [f.shape[0] for f in flat]
    total = sum(sizes)
    rows = _round_up(-(-total // LANES), 8)
    flat.append(jnp.zeros((rows * LANES - total,), F32))
    return jnp.concatenate(flat).reshape(rows, LANES)


def _unpack(buf, shapes):
    flat = buf.reshape(-1)
    out, pos = [], 0
    for shp in shapes:
        n = 1
        for d in shp:
            n *= d
        out.append(flat[pos:pos + n].reshape(shp))
        pos += n
    return out


def _pad_to(a, rows, cols):
    return jnp.pad(a, ((0, rows - a.shape[0]), (0, cols - a.shape[1])))


def kernel(x, mem, g_mix, w_in, conv_a_w, conv_a_b, ln_a_g, ln_a_b, conv_b_w, w_out, g_xattn, g_mem, w_q, w_k, w_v, w_o, g_ffn, w_gate, w_up, conv_f_w, w_down, g_final, loss_target, m_g_mix, m_w_in, m_conv_a_w, m_conv_a_b, m_ln_a_g, m_ln_a_b, m_conv_b_w, m_w_out, m_g_xattn, m_g_mem, m_w_q, m_w_k, m_w_v, m_w_o, m_g_ffn, m_w_gate, m_w_up, m_conv_f_w, m_w_down, m_g_final, v_g_mix, v_w_in, v_conv_a_w, v_conv_a_b, v_ln_a_g, v_ln_a_b, v_conv_b_w, v_w_out, v_g_xattn, v_g_mem, v_w_q, v_w_k, v_w_v, v_w_o, v_g_ffn, v_w_gate, v_w_up, v_conv_f_w, v_w_down, v_g_final):
    T, D = x.shape[1], x.shape[2]
    in_sh = w_in.shape[2]
    cw_sh = conv_a_w.shape[2]
    cw = N_CHIPS * cw_sh
    f_sh = w_gate.shape[2]
    fp = _round_up(f_sh, 256)
    F = N_CHIPS * fp
    rs = w_out.shape[1]
    c_idx = lax.axis_index("c")
    p_idx = 2 * lax.axis_index("x") + lax.axis_index("y")

    def t_(a):
        return jnp.swapaxes(a[0], 0, 1)

    big = {
        "w_in": (lambda: w_in[0].astype(BF16), _Sharded((D, N_CHIPS * in_sh), 1)),
        "w_out": (lambda: w_out[0].astype(BF16), _Sharded((N_CHIPS * rs, D), 0)),
        "w_q": (lambda: w_q[0].astype(BF16), _Sharded((D, D), 0)),
        "w_k": (lambda: w_k[0].astype(BF16), _Sharded((D, D), 0)),
        "w_v": (lambda: w_v[0].astype(BF16), _Sharded((D, D), 0)),
        "w_o": (lambda: w_o[0].astype(BF16), _Sharded((D, D), 0)),
        "w_gate": (lambda: _pad_to(t_(w_gate).astype(BF16), fp, D), _Sharded((F, D), 0)),
        "w_up": (lambda: _pad_to(t_(w_up).astype(BF16), fp, D), _Sharded((F, D), 0)),
        "w_down": (lambda: _pad_to(w_down[0].astype(BF16), fp, D), _Sharded((F, D), 0)),
    }
    names = list(big)
    lay = {k: big[k][1] for k in names}
    c_arr = c_idx.astype(jnp.int32).reshape(1)
    pc_arr = jnp.stack([p_idx, c_idx]).astype(jnp.int32)

    conv_shapes = [(KA_ROWS, cw_sh), (KS_ROWS, cw_sh), (KS_ROWS, fp)]
    conv_pack = _pack([_pad_to(conv_a_w[0], KA_ROWS, cw_sh), _pad_to(conv_b_w[0], KS_ROWS, cw_sh),
                       _pad_to(conv_f_w[0], KS_ROWS, fp)])
    conv_all = _allgather_small(conv_pack, name="allgather_conv", reduce=False)
    per_chip = [_unpack(conv_all[2 * q], conv_shapes) for q in range(N_CHIPS)]
    wa = jnp.concatenate([pc[0] for pc in per_chip], axis=1)
    wb = jnp.concatenate([pc[1] for pc in per_chip], axis=1)
    wf = jnp.concatenate([pc[2] for pc in per_chip], axis=1)

    tok = conv_all

    in_half = _Sharded((D // 2, N_CHIPS * in_sh), 1)
    sources = {k: big[k] for k in names if k != "w_in"}
    sources["w_in_lo"] = (lambda: w_in[0, :D // 2].astype(BF16), in_half)
    sources["w_in_hi"] = (lambda: w_in[0, D // 2:].astype(BF16), in_half)
    gather_groups = {"in_lo": ["w_in_lo"], "in_hi": ["w_in_hi"], "out": ["w_out"], "qkv": ["w_q", "w_k", "w_v"],
                     "o": ["w_o"], "gate": ["w_gate"], "up": ["w_up"], "down": ["w_down"]}
    gathers = {}
    W = {}
    relays = {}

    def shards(tag):
        return [sources[k][0]() for k in gather_groups[tag]]

    def gather_start(tag, srcs, after):
        grp = gather_groups[tag]
        lays = [sources[k][1] for k in grp]
        plan = _gather_plan(lays)
        ssem, rsem, srcs, lands, token = _exchange_start(
            srcs, [(l.shape, BF16) for l in lays], plan, after, name=f"gather_start_{tag}")
        gathers[tag] = (ssem, rsem, srcs, lands, plan, lays, grp)
        return token

    def relay(tag, after, also=()):
        ssem, rsem, srcs, lands, plan, lays, grp = gathers[tag]
        _, lands = _exchange_wait(ssem, rsem, srcs, lands, plan, (after, *also), name=f"gather_wait_{tag}")
        plan = _forward_plan(lays)
        ssem, rsem, lands, _, token = _exchange_start(lands, [], plan, after, name=f"gather_forward_start_{tag}")
        relays[tag] = (ssem, rsem, lands, plan, grp)
        return token

    def gathered(tag, after):
        if tag not in relays:
            relay(tag, after)
        ssem, rsem, lands, plan, grp = relays[tag]
        lands, _ = _exchange_wait(ssem, rsem, lands, [], plan, after, name=f"gather_forward_wait_{tag}")
        W.update(zip(grp, lands))

    tok = gather_start("in_lo", shards("in_lo"), tok)
    tok = gather_start("in_hi", shards("in_hi"), tok)
    later = {tag: shards(tag) for tag in gather_groups if tag not in gathers}
    tok = relay("in_lo", tok, also=[a for srcs in later.values() for a in srcs])
    for tag, srcs in later.items():
        tok = gather_start(tag, srcs, tok)

    xs, tgt, mems = x[0], loss_target[0], mem[0]
    g_mem2, g_final2 = g_mem[None, :], g_final[None, :]

    memn, rm = _norm_fwd(mems, g_mem2, name="norm_mem")
    xn1, r1 = _norm_fwd(xs, g_mix, name="norm_mix")
    gathered("in_lo", (xn1, tok))
    proj = _mm(xn1, W["w_in_lo"], a_k=(0, D // 2), name="mm_in_lo")
    gathered("in_hi", proj)
    proj = _mm(xn1, W["w_in_hi"], a_k=(D // 2, D // 2), add=proj, name="mm_in_hi")
    ftok = relay("out", proj)
    u3, u1 = _mix_a_fwd(proj, wa, conv_a_b, ln_a_g, ln_a_b, cw=cw, deps=[ftok], name="mix_a_fwd")
    vb = _mix_b_fwd(proj, wb, cw=cw, sw=cw, name="mix_b_fwd")
    mix = jnp.concatenate([u3, vb], axis=1)
    gathered("out", mix)
    ftok = relay("qkv", mix)
    h1 = _mm(mix, W["w_out"], add=xs, deps=[ftok], name="mm_out")
    xn2, r2 = _norm_fwd(h1, g_xattn, name="norm_xattn")
    gathered("qkv", xn2)
    ftok = relay("o", xn2)
    q = _mm(xn2, W["w_q"], out_dtype=BF16, deps=[ftok], name="mm_q")
    k = _mm(memn, W["w_k"], out_dtype=BF16, name="mm_k")
    vm = _mm(memn, W["w_v"], out_dtype=BF16, name="mm_v")
    o = _attn_fwd(q, k, vm, name="attn_fwd")
    gathered("o", o)
    h2 = _mm(o, W["w_o"], add=h1, name="mm_o")
    relay("gate", h2)
    xn3, r3 = _norm_fwd(h2, g_ffn, name="norm_ffn")
    gathered("gate", xn3)
    gp = _mm(xn3, W["w_gate"], tb=True, name="mm_gate")
    gathered("up", gp)
    up = _mm(xn3, W["w_up"], tb=True, name="mm_up")
    ftok = relay("down", up)
    f = _ffn_act_fwd(gp, up, wf, deps=[ftok], name="ffn_act_fwd")
    gathered("down", f)
    h3 = _mm(f, W["w_down"], add=h2, name="mm_down")

    G = {}
    other_arr = 1 - c_arr

    def grad_half(k, which, **kw):
        xk, dyk = G[k]
        return _mm(xk, dyk, ta=True, half=("n" if lay[k].axis == 0 else "m", which), **kw)

    def siblings_start(tag, grp, after):
        lays = [lay[k] for k in grp]
        firsts = [grad_half(k, other_arr, deps=[after], name=f"mm_dw_{k}_sibling") for k in grp]
        plan = _sibling_plan(len(grp))
        ssem, rsem, srcs, lands, token = _exchange_start(
            firsts, [(l.half_shape(), F32) for l in lays], plan, after, name=f"rs_siblings_start_{tag}")
        return (tag, grp, lays, plan, ssem, rsem, srcs, lands), token

    def chips_start(state, after):
        tag, grp, lays, plan, ssem, rsem, srcs, lands = state
        _, lands = _exchange_wait(ssem, rsem, srcs, lands, plan, after, name=f"rs_siblings_wait_{tag}")
        sums = [grad_half(k, c_arr, add=ra, out_dtype=BF16, name=f"mm_dw_{k}_own") for k, ra in zip(grp, lands)]
        plan = _chips_plan(lays)
        ssem, rsem, sums, lands, token = _exchange_start(
            sums, [((N_CHIPS - 1,) + l.block_half_shape(), BF16) for l in lays], plan, after, name=f"rs_chips_start_{tag}")
        return (tag, grp, lays, plan, ssem, rsem, sums, lands), token

    def join_start(state, after):
        tag, grp, lays, plan, ssem, rsem, sums, lands = state
        sums, lands = _exchange_wait(ssem, rsem, sums, lands, plan, after, name=f"rs_chips_wait_{tag}")
        halves = [_sum_chips(cs, rb, pc_arr, l, name=f"sum_chips_{k}") for k, cs, rb, l in zip(grp, sums, lands, lays)]
        plan = _join_plan(lays)
        ssem, rsem, halves, _, token = _exchange_start(halves, [], plan, after[0], name=f"rs_join_start_{tag}")
        return (tag, grp, plan, ssem, rsem, halves), token

    def join_finish(state, after):
        tag, grp, plan, ssem, rsem, halves = state
        halves, _ = _exchange_wait(ssem, rsem, halves, [], plan, after, name=f"rs_join_wait_{tag}")
        return dict(zip(grp, halves))

    loss_rows, dh3, dh3b, dg_final = _loss_head(h3, g_final2, tgt, name="loss_head")
    df = _mm(dh3b, W["w_down"], tb=True, out_dtype=BF16, name="mm_d_f")
    G["w_down"] = (f, dh3b)
    rs_down, tok = siblings_start("down", ["w_down"], tok)
    dgp, dup, dwf = _ffn_act_bwd(gp, up, df, wf, name="ffn_act_bwd")
    G["w_gate"], G["w_up"] = (dgp, xn3), (dup, xn3)
    rs_ffn, tok = siblings_start("ffn", ["w_gate", "w_up"], tok)
    rs_down, tok = chips_start(rs_down, tok)
    dxn3 = _mm(dgp, W["w_gate"], deps=[tok], name="mm_dxn3_gate")
    dxn3 = _mm(dup, W["w_up"], add=dxn3, name="mm_dxn3_up")
    dh2, dh2b, dg_ffn = _norm_bwd(h2, r3, g_ffn, dxn3, dh3, name="norm_ffn_bwd")
    rs_ffn, tok = chips_start(rs_ffn, dh2b)
    do = _mm(dh2b, W["w_o"], tb=True, out_dtype=BF16, deps=[tok], name="mm_d_o")
    G["w_o"] = (o, dh2b)
    dq, dk, dvm = _attn_bwd(q, k, vm, do, name="attn_bwd")
    dkb, dvb = dk.astype(BF16), dvm.astype(BF16)
    G["w_q"], G["w_k"], G["w_v"] = (xn2, dq), (memn, dkb), (memn, dvb)
    rs_att, tok = siblings_start("att", ["w_o", "w_q", "w_k", "w_v"], tok)
    dxn2 = _mm(dq, W["w_q"], tb=True, deps=[tok], name="mm_dxn2")
    dmemn = _mm(dkb, W["w_k"], tb=True, name="mm_dmem_k")
    dmemn = _mm(dvb, W["w_v"], tb=True, add=dmemn, name="mm_dmem_v")
    dg_mem = _norm_bwd(mems, rm, g_mem2, dmemn, None, name="norm_mem_bwd", want_dh=False)
    dh1, dh1b, dg_xattn = _norm_bwd(h1, r2, g_xattn, dxn2, dh2, name="norm_xattn_bwd")
    rs_att, tok = chips_start(rs_att, dh1b)
    dmix = _mm(dh1b, W["w_out"], tb=True, deps=[tok], name="mm_d_mix")
    G["w_out"] = (mix, dh1b)
    rs_out, tok = siblings_start("out", ["w_out"], tok)
    dav, dag, dwa, dba, dlg, dlb = _mix_a_bwd(proj, u1, dmix, wa, ln_a_g, ln_a_b, cw=cw, name="mix_a_bwd")
    dbg, dcg, dbh, dwb = _mix_b_bwd(proj, dmix, wb, cw=cw, sw=cw, name="mix_b_bwd")
    dproj = jnp.concatenate([dav, dag, dbg, dcg, dbh], axis=1)
    G["w_in"] = (xn1, dproj)
    rs_in, tok = siblings_start("in", ["w_in"], tok)
    rs_out, tok = chips_start(rs_out, tok)
    dxn1 = _mm(dproj, W["w_in_lo"], tb=True, out_cols=(0, D), deps=[tok], name="mm_dxn1_lo")
    dxn1 = _mm(dproj, W["w_in_hi"], tb=True, out_cols=(D // 2, D), into=dxn1, name="mm_dxn1_hi")
    dx, _, dg_mix = _norm_bwd(xs, r1, g_mix, dxn1, dh1, name="norm_mix_bwd")
    rs_in, tok = chips_start(rs_in, dx)

    loss_part = jnp.sum(loss_rows).reshape(1, 1)
    small_parts = [dg_mix, dba, dlg, dlb, dg_xattn, dg_mem, dg_ffn, dg_final, dwa, dwb, dwf, loss_part]
    small_shapes = [a.shape for a in small_parts]
    reduced = _allgather_small(_pack(small_parts), name="allreduce_small", reduce=True)
    (sg_mix, sba, slg, slb, sg_xattn, sg_mem, sg_ffn, sg_final, swa, swb, swf, loss_sum) = _unpack(reduced, small_shapes)
    loss = loss_sum.reshape(())
    ga_w = lax.dynamic_slice(swa, (0, p_idx * cw_sh), (KA, cw_sh))
    gb_w = lax.dynamic_slice(swb, (0, p_idx * cw_sh), (KS, cw_sh))
    gf_w = lax.dynamic_slice(swf, (0, p_idx * fp), (KS, f_sh))

    weights = dict(g_mix=g_mix, w_in=w_in, conv_a_w=conv_a_w, conv_a_b=conv_a_b, ln_a_g=ln_a_g, ln_a_b=ln_a_b,
                   conv_b_w=conv_b_w, w_out=w_out, g_xattn=g_xattn, g_mem=g_mem, w_q=w_q, w_k=w_k, w_v=w_v, w_o=w_o,
                   g_ffn=g_ffn, w_gate=w_gate, w_up=w_up, conv_f_w=conv_f_w, w_down=w_down, g_final=g_final)
    m_in = dict(g_mix=m_g_mix, w_in=m_w_in, conv_a_w=m_conv_a_w, conv_a_b=m_conv_a_b, ln_a_g=m_ln_a_g, ln_a_b=m_ln_a_b,
                conv_b_w=m_conv_b_w, w_out=m_w_out, g_xattn=m_g_xattn, g_mem=m_g_mem, w_q=m_w_q, w_k=m_w_k, w_v=m_w_v,
                w_o=m_w_o, g_ffn=m_g_ffn, w_gate=m_w_gate, w_up=m_w_up, conv_f_w=m_conv_f_w, w_down=m_w_down,
                g_final=m_g_final)
    v_in = dict(g_mix=v_g_mix, w_in=v_w_in, conv_a_w=v_conv_a_w, conv_a_b=v_conv_a_b, ln_a_g=v_ln_a_g, ln_a_b=v_ln_a_b,
                conv_b_w=v_conv_b_w, w_out=v_w_out, g_xattn=v_g_xattn, g_mem=v_g_mem, w_q=v_w_q, w_k=v_w_k, w_v=v_w_v,
                w_o=v_w_o, g_ffn=v_g_ffn, w_gate=v_w_gate, w_up=v_w_up, conv_f_w=v_conv_f_w, w_down=v_w_down,
                g_final=v_g_final)
    order = list(weights)
    grads = dict(g_mix=sg_mix, conv_a_w=ga_w, conv_a_b=sba, ln_a_g=slg, ln_a_b=slb, conv_b_w=gb_w, g_xattn=sg_xattn,
                 g_mem=sg_mem, g_ffn=sg_ffn, conv_f_w=gf_w, g_final=sg_final)
    grads = {k: g.reshape(weights[k].shape) for k, g in grads.items()}

    delta, new_m, new_v = {}, {}, {}
    small = [k for k in order if k not in big]
    small_shapes = [weights[k].shape for k in small]
    packed = [_pack([src[k] for k in small]) for src in (weights, grads, m_in, v_in)]
    d_, m_, v_ = _adamw(*packed, name="adamw_small")
    for k, dd, mm_, vv in zip(small, _unpack(d_, small_shapes), _unpack(m_, small_shapes), _unpack(v_, small_shapes)):
        delta[k], new_m[k], new_v[k] = dd, mm_, vv

    transposed = ("w_gate", "w_up")

    def update(shard_grads):
        last = None
        for k, g in shard_grads.items():
            view = t_ if k in transposed else (lambda a: a[0])
            back = (lambda a: jnp.swapaxes(a, 0, 1)[None]) if k in transposed else (lambda a: a[None])
            padded = g.shape != view(weights[k]).shape
            outs = _adamw(view(weights[k]), g, view(m_in[k]), view(v_in[k]), emit_grad=padded, name=f"adamw_{k}")
            delta[k], new_m[k], new_v[k] = back(outs[0]), back(outs[1]), back(outs[2])
            grads[k] = back(outs[3] if padded else g)
            last = outs[0]
        return last

    after = (d_, tok)
    joining = None
    for state in (rs_down, rs_ffn, rs_att, rs_out, rs_in):
        started, token = join_start(state, after)
        after = (token,)
        if joining is not None:
            after = (update(join_finish(joining, token)), token)
        joining = started
    update(join_finish(joining, after[0]))

    return (loss, dx[None], *[grads[k] for k in order], *[delta[k] for k in order],
            *[new_m[k] for k in order], *[new_v[k] for k in order])
```

```python
import jax
import jax.numpy as jnp
from jax import lax
from jax.experimental import pallas as pl
from jax.experimental.pallas import tpu as pltpu

F32 = jnp.float32
BF16 = jnp.bfloat16
EPS = 1e-6
N_HEADS = 4
GROUP = 128
KA = 31
KS = 3
KA_ROWS = 32
KS_ROWS = 8
HALO_A = 32
HALO_S = 8
CHUNK_ROWS = 16
CHUNK_COLS = 256
N_CHIPS = 4
N_DEV = 8
LANES = 128
VMEM_LIMIT = 56 * 1024 * 1024
MM_WIDE_TILE_ELEMS = 4096 * 512
ADAMW_BLOCK_BYTES = 1 << 20
MESH = pl.DeviceIdType.MESH
ANY = pl.BlockSpec(memory_space=pl.ANY)

ADAM_LR = 0.001
ADAM_B1 = 0.9
ADAM_B2 = 0.999
ADAM_EPS = 1e-08
ADAM_WD = 0.01
ADAM_STEP = 10


def _pick(dim, pref, mult=LANES):
    if dim <= pref:
        return dim
    t = (pref // mult) * mult
    while t >= mult:
        if dim % t == 0:
            return t
        t -= mult
    return dim


def _round_up(n, m):
    return ((n + m - 1) // m) * m


def _params(*sem):
    return pltpu.CompilerParams(dimension_semantics=sem, vmem_limit_bytes=VMEM_LIMIT)


def _sigmoid(x):
    return jax.nn.sigmoid(x)


def _mm(a, b, *, name, ta=False, tb=False, out_dtype=F32, add=None, deps=(), half=None, a_k=None, b_k=None,
        out_cols=None, into=None, tm=1024, tn=512, tk=5632):
    if ta:
        K, M = a.shape
    else:
        M, K = a.shape
    if tb:
        N, K2 = b.shape
    else:
        K2, N = b.shape
    k0 = kb0 = 0
    if a_k is not None:
        k0, K = a_k
    if b_k is not None:
        kb0, K2 = b_k
    assert K == K2, (a.shape, b.shape)
    half_dim, half_sel = half if half is not None else (None, None)
    if half_dim == "m":
        M //= 2
    elif half_dim == "n":
        N //= 2
    tm, tk = _pick(M, tm), _pick(K, tk)
    nk = K // tk
    tn = _pick(N, 2 * tn if nk == 1 and K * tn <= MM_WIDE_TILE_ELEMS else tn)
    mi, nj = M // tm, N // tn
    n0, n_total = out_cols if out_cols is not None else (0, N)
    assert k0 % tk == 0 and kb0 % tk == 0 and n0 % tn == 0
    koff, kboff, noff = k0 // tk, kb0 // tk, n0 // tn

    def a_map(i, j, k, *s):
        i = i + s[0][0] * mi if half_dim == "m" else i
        return (k + koff, i) if ta else (i, k + koff)

    def b_map(i, j, k, *s):
        j = j + s[0][0] * nj if half_dim == "n" else j
        return (j, k + kboff) if tb else (k + kboff, j)

    a_spec = pl.BlockSpec((tk, tm) if ta else (tm, tk), a_map)
    b_spec = pl.BlockSpec((tn, tk) if tb else (tk, tn), b_map)
    o_spec = pl.BlockSpec((tm, tn), lambda i, j, k, *s: (i, j + noff))
    add_spec = pl.BlockSpec((tm, tn), lambda i, j, k, *s: (i, j))
    dims = (((0,) if ta else (1,), (1,) if tb else (0,)), ((), ()))
    has_add = add is not None
    n_pre = 0 if half is None else 1
    n_after = len(deps) + (0 if into is None else 1)

    def body(*refs):
        refs = refs[n_pre:]
        a_ref, b_ref = refs[:2]
        add_ref = refs[2] if has_add else None
        o_ref = refs[(3 if has_add else 2) + n_after]
        acc_ref = refs[-1] if nk > 1 else None
        k = pl.program_id(2)
        part = lax.dot_general(a_ref[...], b_ref[...], dims, preferred_element_type=F32)

        def finish(r):
            if add_ref is not None:
                r = add_ref[...] + r
            o_ref[...] = r.astype(out_dtype)

        if nk == 1:
            finish(part)
        else:
            @pl.when(k == 0)
            def _():
                acc_ref[...] = part

            @pl.when(jnp.logical_and(k > 0, k < nk - 1))
            def _():
                acc_ref[...] += part

            @pl.when(k == nk - 1)
            def _():
                finish(acc_ref[...] + part)

    in_specs = [a_spec, b_spec] + ([add_spec] if has_add else []) + [ANY] * n_after
    args = (a, b) + ((add,) if has_add else ()) + tuple(deps) + ((into,) if into is not None else ())
    aliases = {n_pre + len(args) - 1: 0} if into is not None else {}
    scratch = [pltpu.VMEM((tm, tn), F32)] if nk > 1 else []
    grid = (mi, nj, nk)
    if half is None:
        grid_spec = dict(grid=grid, in_specs=in_specs, out_specs=o_spec, scratch_shapes=scratch)
    else:
        args = (half_sel,) + args
        grid_spec = dict(grid_spec=pltpu.PrefetchScalarGridSpec(
            num_scalar_prefetch=1, grid=grid, in_specs=in_specs, out_specs=o_spec, scratch_shapes=scratch))
    return pl.pallas_call(
        body,
        name=name,
        out_shape=jax.ShapeDtypeStruct((M, n_total), out_dtype),
        input_output_aliases=aliases,
        compiler_params=_params("parallel", "parallel", "arbitrary"),
        **grid_spec,
    )(*args)


def _norm_fwd(h, g, *, name):
    T, D = h.shape
    R = _pick(T, 256, 8)

    def body(h_ref, g_ref, xn_ref, r_ref):
        x = h_ref[...]
        r = lax.rsqrt(jnp.mean(x * x, axis=-1, keepdims=True) + EPS)
        xn_ref[...] = ((x * r) * g_ref[...]).astype(BF16)
        r_ref[...] = r

    return pl.pallas_call(
        body,
        name=name,
        out_shape=(jax.ShapeDtypeStruct((T, D), BF16), jax.ShapeDtypeStruct((T, 1), F32)),
        grid=(T // R,),
        in_specs=[pl.BlockSpec((R, D), lambda i: (i, 0)), pl.BlockSpec((1, D), lambda i: (0, 0))],
        out_specs=(pl.BlockSpec((R, D), lambda i: (i, 0)), pl.BlockSpec((R, 1), lambda i: (i, 0))),
        compiler_params=_params("parallel"),
    )(h, g)


def _norm_bwd(h, r, g, dxn, dres, *, name, want_dh=True):
    T, D = h.shape
    R = _pick(T, 128, 8)
    has_res = dres is not None

    def body(*refs):
        h_ref, r_ref, g_ref, dxn_ref = refs[:4]
        pos = 4
        dres_ref = None
        if has_res:
            dres_ref = refs[pos]
            pos += 1
        if want_dh:
            dh_ref, dhb_ref, dg_ref = refs[pos:pos + 3]
        else:
            dg_ref = refs[pos]
        i = pl.program_id(0)
        rr = r_ref[...]
        hn = h_ref[...] * rr
        d = dxn_ref[...].astype(F32)
        gd = d * g_ref[...]
        part = jnp.sum(d * hn, axis=0, keepdims=True)

        @pl.when(i == 0)
        def _():
            dg_ref[...] = part

        @pl.when(i > 0)
        def _():
            dg_ref[...] += part

        if want_dh:
            dh = rr * (gd - hn * jnp.mean(gd * hn, axis=-1, keepdims=True))
            if dres_ref is not None:
                dh = dres_ref[...] + dh
            dh_ref[...] = dh
            dhb_ref[...] = dh.astype(BF16)

    row = pl.BlockSpec((R, D), lambda i: (i, 0))
    vec = pl.BlockSpec((1, D), lambda i: (0, 0))
    in_specs = [row, pl.BlockSpec((R, 1), lambda i: (i, 0)), vec, row] + ([row] if has_res else [])
    args = (h, r, g, dxn) + ((dres,) if has_res else ())
    if want_dh:
        out_shape = (jax.ShapeDtypeStruct((T, D), F32), jax.ShapeDtypeStruct((T, D), BF16), jax.ShapeDtypeStruct((1, D), F32))
        out_specs = (row, row, vec)
    else:
        out_shape = jax.ShapeDtypeStruct((1, D), F32)
        out_specs = vec
    return pl.pallas_call(
        body, name=name, out_shape=out_shape, grid=(T // R,), in_specs=in_specs, out_specs=out_specs,
        compiler_params=_params("arbitrary"),
    )(*args)


def _loss_head(h, g, tgt, *, name):
    T, D = h.shape
    R = _pick(T, 128, 8)

    def body(h_ref, g_ref, t_ref, loss_ref, dh_ref, dhb_ref, dg_ref):
        i = pl.program_id(0)
        x = h_ref[...]
        gg = g_ref[...]
        r = lax.rsqrt(jnp.mean(x * x, axis=-1, keepdims=True) + EPS)
        hn = x * r
        e = hn * gg - t_ref[...]
        loss_ref[...] = 0.5 * jnp.mean(e * e, axis=-1, keepdims=True)
        dy = e * (1.0 / D)
        gd = dy * gg
        dh = r * (gd - hn * jnp.mean(gd * hn, axis=-1, keepdims=True))
        dh_ref[...] = dh
        dhb_ref[...] = dh.astype(BF16)
        part = jnp.sum(dy * hn, axis=0, keepdims=True)

        @pl.when(i == 0)
        def _():
            dg_ref[...] = part

        @pl.when(i > 0)
        def _():
            dg_ref[...] += part

    row = pl.BlockSpec((R, D), lambda i: (i, 0))
    vec = pl.BlockSpec((1, D), lambda i: (0, 0))
    return pl.pallas_call(
        body,
        name=name,
        out_shape=(jax.ShapeDtypeStruct((T, 1), F32), jax.ShapeDtypeStruct((T, D), F32),
                   jax.ShapeDtypeStruct((T, D), BF16), jax.ShapeDtypeStruct((1, D), F32)),
        grid=(T // R,),
        in_specs=[row, vec, row],
        out_specs=(pl.BlockSpec((R, 1), lambda i: (i, 0)), row, row, vec),
        compiler_params=_params("arbitrary"),
    )(h, g, tgt)


_NT = (((1,), (1,)), ((), ()))
_TN = (((0,), (0,)), ((), ()))
_NN = (((1,), (0,)), ((), ()))


def _softmax_rows(s):
    m = jnp.max(s, axis=-1, keepdims=True)
    e = jnp.exp(s - m)
    return e / jnp.sum(e, axis=-1, keepdims=True)


def _attn_fwd(q, k, v, *, name):
    T, D = q.shape
    ML = k.shape[0]
    dh = D // N_HEADS
    scale = dh ** -0.5
    R = _pick(T, 512, 16)

    def body(q_ref, k_ref, v_ref, o_ref):
        s = lax.dot_general(q_ref[...], k_ref[...], _NT, preferred_element_type=F32) * scale
        p = _softmax_rows(s)
        o_ref[...] = lax.dot_general(p.astype(BF16), v_ref[...], _NN, preferred_element_type=F32).astype(BF16)

    qs = pl.BlockSpec((R, dh), lambda i, h: (i, h))
    ks = pl.BlockSpec((ML, dh), lambda i, h: (0, h))
    return pl.pallas_call(
        body, name=name, out_shape=jax.ShapeDtypeStruct((T, D), BF16), grid=(T // R, N_HEADS),
        in_specs=[qs, ks, ks], out_specs=qs, compiler_params=_params("parallel", "parallel"),
    )(q, k, v)


def _attn_bwd(q, k, v, do, *, name):
    T, D = q.shape
    ML = k.shape[0]
    dh = D // N_HEADS
    scale = dh ** -0.5
    R = _pick(T, 512, 16)

    def body(q_ref, k_ref, v_ref, do_ref, dq_ref, dk_ref, dv_ref):
        i = pl.program_id(1)
        qq, kk, vv, dd = q_ref[...], k_ref[...], v_ref[...], do_ref[...]
        s = lax.dot_general(qq, kk, _NT, preferred_element_type=F32) * scale
        p = _softmax_rows(s)
        dp = lax.dot_general(dd, vv, _NT, preferred_element_type=F32)
        dv_part = lax.dot_general(p.astype(BF16), dd, _TN, preferred_element_type=F32)
        ds = (p * (dp - jnp.sum(p * dp, axis=-1, keepdims=True)) * scale).astype(BF16)
        dq_ref[...] = lax.dot_general(ds, kk, _NN, preferred_element_type=F32).astype(BF16)
        dk_part = lax.dot_general(ds, qq, _TN, preferred_element_type=F32)

        @pl.when(i == 0)
        def _():
            dk_ref[...] = dk_part
            dv_ref[...] = dv_part

        @pl.when(i > 0)
        def _():
            dk_ref[...] += dk_part
            dv_ref[...] += dv_part

    qs = pl.BlockSpec((R, dh), lambda h, i: (i, h))
    ks = pl.BlockSpec((ML, dh), lambda h, i: (0, h))
    return pl.pallas_call(
        body,
        name=name,
        out_shape=(jax.ShapeDtypeStruct((T, D), BF16), jax.ShapeDtypeStruct((ML, D), F32), jax.ShapeDtypeStruct((ML, D), F32)),
        grid=(N_HEADS, T // R),
        in_specs=[qs, ks, ks, qs],
        out_specs=(qs, ks, ks),
        compiler_params=_params("parallel", "arbitrary"),
    )(q, k, v, do)


def _prev_halo(R, halo, CT, col):
    per = R // halo
    return lambda i, j: (jnp.maximum(i * per - 1, 0), col(j))


def _next_halo(R, halo, T, col):
    per = R // halo
    last = T // halo - 1
    return lambda i, j: (jnp.minimum((i + 1) * per, last), col(j))


def _swap(f):
    return lambda j, i: f(i, j)


def _ffn_act_fwd(gp, up, wf, *, name, deps=()):
    T, F = gp.shape
    R = _pick(T, 256, CHUNK_ROWS)
    CT = _pick(F, 1024, CHUNK_COLS)
    H = HALO_S
    RC, CC = CHUNK_ROWS, min(CHUNK_COLS, CT)

    def body(g_ref, gh_ref, u_ref, w_ref, *rest):
        f_ref, ext = rest[len(deps):]
        i = pl.program_id(0)
        ext[pl.ds(0, H), :] = jnp.where(i > 0, gh_ref[...], 0.0)
        ext[pl.ds(H, R), :] = g_ref[...]
        for c0 in range(0, CT, CC):
            cols = pl.ds(c0, CC)
            w0, w1, w2 = [jnp.broadcast_to(w_ref[pl.ds(k, 1), cols], (RC, CC)) for k in range(KS)]
            for r0 in range(0, R, RC):
                g = ext[pl.ds(r0 + H - 2, RC), cols] * w0
                g = g + ext[pl.ds(r0 + H - 1, RC), cols] * w1
                g = g + ext[pl.ds(r0 + H, RC), cols] * w2
                f_ref[pl.ds(r0, RC), cols] = (g * _sigmoid(g) * u_ref[pl.ds(r0, RC), cols]).astype(BF16)

    main = pl.BlockSpec((R, CT), lambda i, j: (i, j))
    return pl.pallas_call(
        body,
        name=name,
        out_shape=jax.ShapeDtypeStruct((T, F), BF16),
        grid=(T // R, F // CT),
        in_specs=[main, pl.BlockSpec((H, CT), _prev_halo(R, H, CT, lambda j: j)), main,
                  pl.BlockSpec((KS_ROWS, CT), lambda i, j: (0, j))] + [ANY] * len(deps),
        out_specs=main,
        scratch_shapes=[pltpu.VMEM((R + H, CT), F32)],
        compiler_params=_params("parallel", "parallel"),
    )(gp, gp, up, wf, *deps)


def _ffn_act_bwd(gp, up, df, wf, *, name):
    T, F = gp.shape
    R = _pick(T, 256, CHUNK_ROWS)
    CT = _pick(F, 512, CHUNK_COLS)
    H = HALO_S
    HB = 16
    n_t = T // R
    RC, CC = CHUNK_ROWS, min(CHUNK_COLS, CT)

    def body(g_ref, gp_ref, gn_ref, u_ref, un_ref, d_ref, dn_ref, w_ref, dg_out, du_out, dw_ref, ext, dge):
        i = pl.program_id(1)
        last = i == n_t - 1
        ext[pl.ds(0, H), :] = jnp.where(i > 0, gp_ref[...], 0.0)
        ext[pl.ds(H, R), :] = g_ref[...]
        ext[pl.ds(H + R, H), :] = gn_ref[...]

        def dact(g, u, d):
            sg = _sigmoid(g)
            return d * u * (sg * (1.0 + g * (1.0 - sg))), d * (g * sg)

        dw_rows = [[] for _ in range(KS)]
        for c0 in range(0, CT, CC):
            cols = pl.ds(c0, CC)
            w = [jnp.broadcast_to(w_ref[pl.ds(k, 1), cols], (RC, CC)) for k in range(KS)]
            acc = [jnp.zeros((RC, CC), F32) for _ in range(KS)]
            for r0 in range(0, R, RC):
                rows = pl.ds(r0, RC)
                taps = [ext[pl.ds(r0 + H - 2 + k, RC), cols] for k in range(KS)]
                g = taps[0] * w[0] + taps[1] * w[1] + taps[2] * w[2]
                dg, du = dact(g, u_ref[rows, cols], d_ref[rows, cols].astype(F32))
                du_out[rows, cols] = du.astype(BF16)
                dge[rows, cols] = dg
                acc = [a + dg * t for a, t in zip(acc, taps)]
            taps = [ext[pl.ds(R + H - 2 + k, H), cols] for k in range(KS)]
            g = taps[0] * w[0][:H] + taps[1] * w[1][:H] + taps[2] * w[2][:H]
            d_next = jnp.where(last, 0.0, dn_ref[pl.ds(0, H), cols].astype(F32))
            dg_next, _ = dact(g, un_ref[:, cols], d_next)
            dge[pl.ds(R, H), cols] = jnp.where(last, 0.0, dg_next)
            for r0 in range(0, R, RC):
                dgp = (dge[pl.ds(r0, RC), cols] * w[2] + dge[pl.ds(r0 + 1, RC), cols] * w[1]
                       + dge[pl.ds(r0 + 2, RC), cols] * w[0])
                dg_out[pl.ds(r0, RC), cols] = dgp.astype(BF16)
            for k in range(KS):
                dw_rows[k].append(jnp.sum(acc[k], axis=0, keepdims=True))
        rows = [jnp.concatenate(r, axis=1) if len(r) > 1 else r[0] for r in dw_rows]
        rows.append(jnp.zeros((KS_ROWS - KS, CT), F32))
        _acc_rows_block(dw_ref, i, rows)

    col = lambda j: j
    main = pl.BlockSpec((R, CT), lambda j, i: (i, j))
    prev8 = pl.BlockSpec((H, CT), _swap(_prev_halo(R, H, CT, col)))
    next8 = pl.BlockSpec((H, CT), _swap(_next_halo(R, H, T, col)))
    next16 = pl.BlockSpec((HB, CT), _swap(_next_halo(R, HB, T, col)))
    wspec = pl.BlockSpec((KS_ROWS, CT), lambda j, i: (0, j))
    return pl.pallas_call(
        body,
        name=name,
        out_shape=(jax.ShapeDtypeStruct((T, F), BF16), jax.ShapeDtypeStruct((T, F), BF16), jax.ShapeDtypeStruct((KS_ROWS, F), F32)),
        grid=(F // CT, n_t),
        in_specs=[main, prev8, next8, main, next8, main, next16, wspec],
        out_specs=(main, main, wspec),
        scratch_shapes=[pltpu.VMEM((R + 2 * H, CT), F32), pltpu.VMEM((R + H, CT), F32)],
        compiler_params=_params("parallel", "arbitrary"),
    )(gp, gp, gp, up, up, df, df, wf)


def _acc_rows_block(ref, i, rows):
    *singles, pad = rows

    @pl.when(i == 0)
    def _():
        for k, row in enumerate(singles):
            ref[pl.ds(k, 1), :] = row
        ref[pl.ds(len(singles), pad.shape[0]), :] = pad

    @pl.when(i > 0)
    def _():
        for k, row in enumerate(singles):
            ref[pl.ds(k, 1), :] += row


def _group_stats(x):
    mu = jnp.mean(x, axis=-1, keepdims=True)
    xc = x - mu
    var = jnp.mean(xc * xc, axis=-1, keepdims=True)
    return xc, lax.rsqrt(var + EPS)


def _shifted_rows(rolled, x, cols):
    n = x.shape[0]
    for b in range(1, 8):
        rolled[b - 1, :, cols] = pltpu.roll(x, n - b, axis=0)


def _window(src, rolled, off, rows, cols):
    b = off % 8
    if b == 0:
        return src[pl.ds(off, rows), cols]
    return rolled[b - 1, pl.ds(off - b, rows), cols]


def _mix_a_fwd(proj, wa, ba, lg, lb, *, cw, name, deps=()):
    T = proj.shape[0]
    R = _pick(T, 128, HALO_A)
    CT = _pick(cw, 256)
    nc = cw // CT
    H = HALO_A

    def body(av_ref, ag_ref, avh_ref, agh_ref, w_ref, b_ref, lg_ref, lb_ref, *rest):
        u3_ref, u1_ref, ext, rolled = rest[len(deps):]
        i = pl.program_id(0)
        for gi in range(CT // GROUP):
            cols = pl.ds(gi * GROUP, GROUP)
            ext[pl.ds(0, H), cols] = jnp.where(i > 0, avh_ref[:, cols] * _sigmoid(agh_ref[:, cols]), 0.0)
            ext[pl.ds(H, R), cols] = av_ref[:, cols] * _sigmoid(ag_ref[:, cols])
            _shifted_rows(rolled, ext[:, cols], cols)
            acc = _window(ext, rolled, H - (KA - 1), R, cols) * w_ref[pl.ds(0, 1), cols]
            for k in range(1, KA):
                acc = acc + _window(ext, rolled, H - (KA - 1) + k, R, cols) * w_ref[pl.ds(k, 1), cols]
            u1 = acc + b_ref[:, cols]
            u1_ref[:, cols] = u1
            xc, rstd = _group_stats(u1)
            u2 = (xc * rstd) * lg_ref[:, cols] + lb_ref[:, cols]
            u3_ref[:, cols] = (u2 * _sigmoid(u2)).astype(BF16)

    main_v = pl.BlockSpec((R, CT), lambda i, j: (i, j))
    main_g = pl.BlockSpec((R, CT), lambda i, j: (i, j + nc))
    halo_v = pl.BlockSpec((H, CT), _prev_halo(R, H, CT, lambda j: j))
    halo_g = pl.BlockSpec((H, CT), _prev_halo(R, H, CT, lambda j: j + nc))
    vec = pl.BlockSpec((1, CT), lambda i, j: (0, j))
    out = pl.BlockSpec((R, CT), lambda i, j: (i, j))
    return pl.pallas_call(
        body,
        name=name,
        out_shape=(jax.ShapeDtypeStruct((T, cw), BF16), jax.ShapeDtypeStruct((T, cw), F32)),
        grid=(T // R, nc),
        in_specs=[main_v, main_g, halo_v, halo_g, pl.BlockSpec((KA_ROWS, CT), lambda i, j: (0, j)), vec, vec, vec]
        + [ANY] * len(deps),
        out_specs=(out, out),
        scratch_shapes=[pltpu.VMEM((R + H, CT), F32), pltpu.VMEM((7, R + H, CT), F32)],
        compiler_params=_params("parallel", "parallel"),
    )(proj, proj, proj, proj, wa, ba, lg, lb, *deps)


def _mix_a_bwd(proj, u1, dmix, wa, lg, lb, *, cw, name):
    T = proj.shape[0]
    R = _pick(T, 128, HALO_A)
    CT = _pick(cw, 256)
    nc = cw // CT
    H = HALO_A
    n_t = T // R
    NG = CT // GROUP

    def body(av_ref, ag_ref, avh_ref, agh_ref, u1_ref, u1n_ref, d3_ref, d3n_ref, w_ref, lg_ref, lb_ref,
             dav_ref, dag_ref, dw_ref, db_ref, dlg_ref, dlb_ref, ext, d1, ext_rolled, d1_rolled):
        i = pl.program_id(1)
        last = i == n_t - 1

        def ln_bwd(u1, d3, cols):
            xc, rstd = _group_stats(u1)
            xh = xc * rstd
            g = lg_ref[:, cols]
            u2 = xh * g + lb_ref[:, cols]
            sg = _sigmoid(u2)
            du2 = d3 * (sg * (1.0 + u2 * (1.0 - sg)))
            dxh = du2 * g
            du1 = rstd * (dxh - jnp.mean(dxh, axis=-1, keepdims=True) - xh * jnp.mean(dxh * xh, axis=-1, keepdims=True))
            return du1, du2 * xh, du2

        def lanes(parts):
            return jnp.concatenate(parts, axis=1) if len(parts) > 1 else parts[0]

        dlg_parts, dlb_parts, db_parts = [], [], []
        dw_parts = [[] for _ in range(KA)]
        for gi in range(NG):
            cols = pl.ds(gi * GROUP, GROUP)
            du1, dlg, dlb = ln_bwd(u1_ref[:, cols], d3_ref[:, cols], cols)
            d1[pl.ds(0, R), cols] = du1
            dlg_parts.append(jnp.sum(dlg, axis=0, keepdims=True))
            dlb_parts.append(jnp.sum(dlb, axis=0, keepdims=True))
            db_parts.append(jnp.sum(du1, axis=0, keepdims=True))
            du1n, _, _ = ln_bwd(u1n_ref[:, cols], jnp.where(last, 0.0, d3n_ref[:, cols]), cols)
            d1[pl.ds(R, H), cols] = jnp.where(last, 0.0, du1n)

            _shifted_rows(d1_rolled, d1[:, cols], cols)
            du0 = _window(d1, d1_rolled, KA - 1, R, cols) * w_ref[pl.ds(0, 1), cols]
            for k in range(1, KA):
                du0 = du0 + _window(d1, d1_rolled, KA - 1 - k, R, cols) * w_ref[pl.ds(k, 1), cols]
            av = av_ref[:, cols]
            sg = _sigmoid(ag_ref[:, cols])
            dav_ref[:, cols] = (du0 * sg).astype(BF16)
            dag_ref[:, cols] = (du0 * av * sg * (1.0 - sg)).astype(BF16)

            ext[pl.ds(0, H), cols] = jnp.where(i > 0, avh_ref[:, cols] * _sigmoid(agh_ref[:, cols]), 0.0)
            ext[pl.ds(H, R), cols] = av * sg
            _shifted_rows(ext_rolled, ext[:, cols], cols)
            d1_main = d1[pl.ds(0, R), cols]
            for k in range(KA):
                window = _window(ext, ext_rolled, H - (KA - 1) + k, R, cols)
                dw_parts[k].append(jnp.sum(d1_main * window, axis=0, keepdims=True))

        dlg_row, dlb_row, db_row = lanes(dlg_parts), lanes(dlb_parts), lanes(db_parts)

        @pl.when(i == 0)
        def _():
            dlg_ref[...] = dlg_row
            dlb_ref[...] = dlb_row
            db_ref[...] = db_row

        @pl.when(i > 0)
        def _():
            dlg_ref[...] += dlg_row
            dlb_ref[...] += dlb_row
            db_ref[...] += db_row

        rows = [lanes(p) for p in dw_parts]
        rows.append(jnp.zeros((KA_ROWS - KA, CT), F32))
        _acc_rows_block(dw_ref, i, rows)

    cv = lambda j: j
    cg = lambda j: j + nc
    main_v = pl.BlockSpec((R, CT), lambda j, i: (i, j))
    main_g = pl.BlockSpec((R, CT), lambda j, i: (i, j + nc))
    prev_v = pl.BlockSpec((H, CT), _swap(_prev_halo(R, H, CT, cv)))
    prev_g = pl.BlockSpec((H, CT), _swap(_prev_halo(R, H, CT, cg)))
    nxt = pl.BlockSpec((H, CT), _swap(_next_halo(R, H, T, cv)))
    wspec = pl.BlockSpec((KA_ROWS, CT), lambda j, i: (0, j))
    vec = pl.BlockSpec((1, CT), lambda j, i: (0, j))
    vshape = jax.ShapeDtypeStruct((1, cw), F32)
    return pl.pallas_call(
        body,
        name=name,
        out_shape=(jax.ShapeDtypeStruct((T, cw), BF16), jax.ShapeDtypeStruct((T, cw), BF16),
                   jax.ShapeDtypeStruct((KA_ROWS, cw), F32), vshape, vshape, vshape),
        grid=(nc, n_t),
        in_specs=[main_v, main_g, prev_v, prev_g, main_v, nxt, main_v, nxt, wspec, vec, vec],
        out_specs=(main_v, main_v, wspec, vec, vec, vec),
        scratch_shapes=[pltpu.VMEM((R + H, CT), F32), pltpu.VMEM((R + H, CT), F32),
                        pltpu.VMEM((7, R + H, CT), F32), pltpu.VMEM((7, R + H, CT), F32)],
        compiler_params=_params("parallel", "arbitrary"),
    )(proj, proj, proj, proj, u1, u1, dmix, dmix, wa, lg, lb)


def _mix_b_fwd(proj, wb, *, cw, sw, name):
    T = proj.shape[0]
    R = _pick(T, 256, 8)
    CT = _pick(sw, 512)
    nb, nc, nh = (2 * cw) // CT, (2 * cw + sw) // CT, (2 * cw + 2 * sw) // CT
    H = HALO_S

    def body(b_ref, c_ref, h_ref, ch_ref, hh_ref, w_ref, v_ref, ext):
        i = pl.program_id(0)
        ext[pl.ds(0, H), :] = jnp.where(i > 0, ch_ref[...] * hh_ref[...], 0.0)
        ext[pl.ds(H, R), :] = c_ref[...] * h_ref[...]
        w = w_ref[...]
        zc = ext[pl.ds(H - 2, R), :] * w[0:1, :]
        zc = zc + ext[pl.ds(H - 1, R), :] * w[1:2, :]
        zc = zc + ext[pl.ds(H, R), :] * w[2:3, :]
        v_ref[...] = (b_ref[...] * zc).astype(BF16)

    def main(off):
        return pl.BlockSpec((R, CT), lambda i, j: (i, j + off))

    def prev(off):
        return pl.BlockSpec((H, CT), _prev_halo(R, H, CT, lambda j: j + off))

    return pl.pallas_call(
        body,
        name=name,
        out_shape=jax.ShapeDtypeStruct((T, sw), BF16),
        grid=(T // R, sw // CT),
        in_specs=[main(nb), main(nc), main(nh), prev(nc), prev(nh), pl.BlockSpec((KS_ROWS, CT), lambda i, j: (0, j))],
        out_specs=pl.BlockSpec((R, CT), lambda i, j: (i, j)),
        scratch_shapes=[pltpu.VMEM((R + H, CT), F32)],
        compiler_params=_params("parallel", "parallel"),
    )(proj, proj, proj, proj, proj, wb)


def _mix_b_bwd(proj, dmix, wb, *, cw, sw, name):
    T = proj.shape[0]
    R = _pick(T, 256, 8)
    CT = _pick(sw, 512)
    nb, nc, nh = (2 * cw) // CT, (2 * cw + sw) // CT, (2 * cw + 2 * sw) // CT
    nd = cw // CT
    H = HALO_S
    n_t = T // R

    def body(b_ref, bn_ref, c_ref, cp_ref, cn_ref, h_ref, hp_ref, hn_ref, d_ref, dn_ref, w_ref,
             db_ref, dc_ref, dhh_ref, dw_ref, ext, dze):
        i = pl.program_id(1)
        last = i == n_t - 1
        cc, hh = c_ref[...], h_ref[...]
        ext[pl.ds(0, H), :] = jnp.where(i > 0, cp_ref[...] * hp_ref[...], 0.0)
        ext[pl.ds(H, R), :] = cc * hh
        ext[pl.ds(H + R, H), :] = cn_ref[...] * hn_ref[...]
        w = w_ref[...]
        w0, w1, w2 = w[0:1, :], w[1:2, :], w[2:3, :]

        def conv(start, n):
            z = ext[pl.ds(start + H - 2, n), :] * w0
            z = z + ext[pl.ds(start + H - 1, n), :] * w1
            return z + ext[pl.ds(start + H, n), :] * w2

        d_main = d_ref[...]
        db_ref[...] = (d_main * conv(0, R)).astype(BF16)
        dzc_main = d_main * b_ref[...]
        dze[pl.ds(0, R), :] = dzc_main
        dze[pl.ds(R, H), :] = jnp.where(last, 0.0, dn_ref[...] * bn_ref[...])
        dz = dze[pl.ds(0, R), :] * w2 + dze[pl.ds(1, R), :] * w1 + dze[pl.ds(2, R), :] * w0
        dc_ref[...] = (dz * hh).astype(BF16)
        dhh_ref[...] = (dz * cc).astype(BF16)
        rows = [jnp.sum(dzc_main * ext[pl.ds(H - 2 + k, R), :], axis=0, keepdims=True) for k in range(KS)]
        rows.append(jnp.zeros((KS_ROWS - KS, CT), F32))
        _acc_rows_block(dw_ref, i, rows)

    def main(off):
        return pl.BlockSpec((R, CT), lambda j, i: (i, j + off))

    def prev(off):
        return pl.BlockSpec((H, CT), _swap(_prev_halo(R, H, CT, lambda j: j + off)))

    def nxt(off):
        return pl.BlockSpec((H, CT), _swap(_next_halo(R, H, T, lambda j: j + off)))

    out = pl.BlockSpec((R, CT), lambda j, i: (i, j))
    wspec = pl.BlockSpec((KS_ROWS, CT), lambda j, i: (0, j))
    act = jax.ShapeDtypeStruct((T, sw), BF16)
    return pl.pallas_call(
        body,
        name=name,
        out_shape=(act, act, act, jax.ShapeDtypeStruct((KS_ROWS, sw), F32)),
        grid=(sw // CT, n_t),
        in_specs=[main(nb), nxt(nb), main(nc), prev(nc), nxt(nc), main(nh), prev(nh), nxt(nh), main(nd), nxt(nd), wspec],
        out_specs=(out, out, out, wspec),
        scratch_shapes=[pltpu.VMEM((R + 2 * H, CT), F32), pltpu.VMEM((R + H, CT), F32)],
        compiler_params=_params("parallel", "arbitrary"),
    )(proj, proj, proj, proj, proj, proj, proj, proj, dmix, dmix, wb)


def _adamw(w, g, m, v, *, name, emit_grad=False):
    Rr, Cc = w.shape
    R = _pick(Rr, max(8, ADAMW_BLOCK_BYTES // (4 * Cc)), 8)

    def body(w_ref, g_ref, m_ref, v_ref, d_ref, mo_ref, vo_ref, *g_out):
        g = g_ref[...]
        m2 = ADAM_B1 * m_ref[...] + (1.0 - ADAM_B1) * g
        v2 = ADAM_B2 * v_ref[...] + (1.0 - ADAM_B2) * (g * g)
        m_hat = m2 / (1.0 - ADAM_B1 ** ADAM_STEP)
        v_hat = v2 / (1.0 - ADAM_B2 ** ADAM_STEP)
        d_ref[...] = -ADAM_LR * (m_hat / (jnp.sqrt(v_hat) + ADAM_EPS) + ADAM_WD * w_ref[...])
        mo_ref[...] = m2
        vo_ref[...] = v2
        if emit_grad:
            g_out[0][...] = g

    spec = pl.BlockSpec((R, Cc), lambda i: (i, 0))
    shp = jax.ShapeDtypeStruct((Rr, Cc), F32)
    n_out = 4 if emit_grad else 3
    return pl.pallas_call(
        body, name=name, out_shape=(shp,) * n_out, grid=(Rr // R,), in_specs=[spec] * 4, out_specs=(spec,) * n_out,
        compiler_params=_params("parallel"),
    )(w, g, m, v)


def _sum_chips(cs, rb, pc, lay, *, name):
    _, Rr, Cc = rb.shape
    tr, tc = _pick(Rr, 512, 16), _pick(Cc, 2048)
    nr, ncol = Rr // tr, Cc // tc
    if lay.axis == 0:
        own_map = lambda i, j, s: (i + s[0] * nr, j)
        out_map = lambda i, j, s: (i, j + s[1] * ncol)
    else:
        own_map = lambda i, j, s: (i, j + s[0] * ncol)
        out_map = lambda i, j, s: (i + s[1] * nr, j)

    def body(s_ref, own_ref, rb_ref, o_ref):
        acc = own_ref[...].astype(F32)
        for j in range(N_CHIPS - 1):
            acc = acc + rb_ref[j].astype(F32)
        o_ref[...] = acc

    return pl.pallas_call(
        body,
        name=name,
        out_shape=jax.ShapeDtypeStruct(lay.shard_shape(), F32),
        grid_spec=pltpu.PrefetchScalarGridSpec(
            num_scalar_prefetch=1,
            grid=(nr, ncol),
            in_specs=[pl.BlockSpec((tr, tc), own_map), pl.BlockSpec((N_CHIPS - 1, tr, tc), lambda i, j, s: (0, i, j))],
            out_specs=pl.BlockSpec((tr, tc), out_map)),
        compiler_params=_params("parallel", "parallel"),
    )(pc, cs, rb)


def _place():
    x, y, c = lax.axis_index("x"), lax.axis_index("y"), lax.axis_index("c")
    return x, y, c, 2 * x + y


def _other_chips(x, y):
    return [(1 - x, y, 2 * (1 - x) + y), (x, 1 - y, 2 * x + (1 - y)), (1 - x, 1 - y, 2 * (1 - x) + (1 - y))]


def _allgather_small(buf, *, name, reduce):
    S = buf.shape[0]

    def body(x_ref, o_ref, gat, send_sems, recv_sems):
        x, y, c, _ = _place()
        me = 4 * x + 2 * y + c
        gat[me] = x_ref[...]
        copies = []
        for k in range(1, N_DEV):
            fx, fy, fc = (k >> 2) & 1, (k >> 1) & 1, k & 1
            px = 1 - x if fx else x
            py = 1 - y if fy else y
            pc = 1 - c if fc else c
            peer = 4 * px + 2 * py + pc
            send = pltpu.make_async_remote_copy(
                src_ref=x_ref, dst_ref=gat.at[me], send_sem=send_sems.at[k - 1], recv_sem=recv_sems.at[k - 1],
                device_id=(px, py, pc), device_id_type=MESH)
            send.start()
            arrival = pltpu.make_async_remote_copy(
                src_ref=x_ref, dst_ref=gat.at[peer], send_sem=send_sems.at[k - 1], recv_sem=recv_sems.at[k - 1],
                device_id=(px, py, pc), device_id_type=MESH)
            copies.append((send, arrival))
        for send, arrival in copies:
            arrival.wait_recv()
        for send, arrival in copies:
            send.wait_send()
        if reduce:
            acc = gat[0]
            for d in range(1, N_DEV):
                acc = acc + gat[d]
            o_ref[...] = acc
        else:
            o_ref[...] = gat[...]

    out_shape = jax.ShapeDtypeStruct((S, LANES) if reduce else (N_DEV, S, LANES), F32)
    return pl.pallas_call(
        body,
        name=name,
        out_shape=out_shape,
        in_specs=[pl.BlockSpec(memory_space=pltpu.VMEM)],
        out_specs=pl.BlockSpec(memory_space=pltpu.VMEM),
        scratch_shapes=[pltpu.VMEM((N_DEV, S, LANES), F32), pltpu.SemaphoreType.DMA((N_DEV - 1,)),
                        pltpu.SemaphoreType.DMA((N_DEV - 1,))],
        compiler_params=pltpu.CompilerParams(vmem_limit_bytes=VMEM_LIMIT),
    )(buf)


class _Sharded:
    def __init__(self, shape, axis):
        self.shape = shape
        self.axis = axis
        self.block = shape[axis] // N_CHIPS
        self.half = shape[1 - axis] // 2

    def _sl(self, along, across):
        return (along, across) if self.axis == 0 else (across, along)

    def block_slice(self, q):
        return self._sl(pl.ds(q * self.block, self.block), pl.ds(0, self.shape[1 - self.axis]))

    def block_half_slice(self, q, c):
        return self._sl(pl.ds(q * self.block, self.block), pl.ds(c * self.half, self.half))

    def shard_half_slice(self, c):
        return self._sl(pl.ds(0, self.block), pl.ds(c * self.half, self.half))

    def shard_shape(self):
        return self._sl(self.block, self.shape[1 - self.axis])

    def half_shape(self):
        return self._sl(self.shape[self.axis], self.half)

    def block_half_shape(self):
        return self._sl(self.block, self.half)

    def block_in_half_slice(self, q):
        return self._sl(pl.ds(q * self.block, self.block), pl.ds(0, self.half))


def _at(ref, sl):
    return ref.at[sl[0], sl[1]]


class _Copy:
    def __init__(self, src, dst, arrive, dev):
        self.src, self.dst, self.arrive, self.dev = src, dst, arrive, dev


HBM = pl.BlockSpec(memory_space=pltpu.HBM)
SEM = pl.BlockSpec(memory_space=pltpu.SEMAPHORE)
EFFECT = pltpu.SideEffectType.DATAFLOW_SIDE_EFFECTING


def _exchange_start(srcs, land_shapes, plan, after, *, name):
    ns, nl = len(srcs), len(land_shapes)
    n_copies = len(plan([None] * ns, [None] * nl, dry=True))

    def body(*refs):
        src_refs, land_refs = refs[:ns], refs[ns:ns + nl]
        send_sems, recv_sems = refs[ns + nl + 1], refs[ns + nl + 2]
        token = refs[-1]
        for k, cp in enumerate(plan(src_refs, land_refs)):
            pltpu.make_async_remote_copy(
                src_ref=cp.src, dst_ref=cp.dst, send_sem=send_sems.at[k], recv_sem=recv_sems.at[k],
                device_id=cp.dev, device_id_type=MESH).start()
        token[...] = jnp.zeros_like(token)

    sem = pltpu.SemaphoreType.DMA((n_copies,))
    lands = [pltpu.with_memory_space_constraint(lax.empty(shp, dt), pltpu.HBM) for shp, dt in land_shapes]
    srcs = [pltpu.with_memory_space_constraint(a, pltpu.HBM) for a in srcs]
    thru = [pltpu.HBM(a.shape, a.dtype) for a in srcs + lands]
    outs = pl.pallas_call(
        body,
        name=name,
        out_shape=(sem, sem, *thru, jax.ShapeDtypeStruct((8, LANES), F32)),
        in_specs=[HBM] * (ns + nl) + [ANY],
        out_specs=(SEM, SEM, *[HBM] * (ns + nl), pl.BlockSpec(memory_space=pltpu.VMEM)),
        input_output_aliases={i: 2 + i for i in range(ns + nl)},
        compiler_params=pltpu.CompilerParams(has_side_effects=EFFECT),
    )(*srcs, *lands, after)
    return outs[0], outs[1], list(outs[2:2 + ns]), list(outs[2 + ns:2 + ns + nl]), outs[-1]


def _exchange_wait(send_sems, recv_sems, srcs, lands, plan, after, *, name):
    ns, nl = len(srcs), len(lands)
    after = tuple(after) if isinstance(after, (tuple, list)) else (after,)

    def body(*refs):
        src_refs, land_refs = refs[:ns], refs[ns:ns + nl]
        send_sems, recv_sems = refs[ns + nl], refs[ns + nl + 1]
        copies = [
            pltpu.make_async_remote_copy(
                src_ref=cp.src, dst_ref=cp.arrive, send_sem=send_sems.at[k], recv_sem=recv_sems.at[k],
                device_id=cp.dev, device_id_type=MESH)
            for k, cp in enumerate(plan(src_refs, land_refs))
        ]
        for cp in copies:
            cp.wait_recv()
        for cp in copies:
            cp.wait_send()

    thru = [pltpu.HBM(a.shape, a.dtype) for a in list(srcs) + list(lands)]
    outs = pl.pallas_call(
        body,
        name=name,
        out_shape=tuple(thru),
        in_specs=[HBM] * (ns + nl) + [SEM, SEM] + [ANY] * len(after),
        out_specs=tuple([HBM] * (ns + nl)),
        input_output_aliases={i: i for i in range(ns + nl)},
        compiler_params=pltpu.CompilerParams(has_side_effects=EFFECT),
    )(*srcs, *lands, send_sems, recv_sems, *after)
    return list(outs[:ns]), list(outs[ns:])


def _gather_plan(layouts):
    def plan(srcs, lands, dry=False):
        if dry:
            return [None] * (4 * len(layouts))
        x, y, c, p = _place()
        copies = []
        for s, g, lay in zip(srcs, lands, layouts):
            own = _at(g, lay.block_slice(p))
            copies.append(_Copy(s, own, own, (x, y, 1 - c)))
            for qx, qy, q in _other_chips(x, y):
                copies.append(_Copy(_at(s, lay.shard_half_slice(c)), _at(g, lay.block_half_slice(p, c)),
                                    _at(g, lay.block_half_slice(q, c)), (qx, qy, c)))
        return copies

    return plan


def _forward_plan(layouts):
    def plan(srcs, lands, dry=False):
        if dry:
            return [None] * (3 * len(layouts))
        x, y, c, _ = _place()
        copies = []
        for g, lay in zip(srcs, layouts):
            for qx, qy, q in _other_chips(x, y):
                landed = _at(g, lay.block_half_slice(q, c))
                copies.append(_Copy(landed, landed, _at(g, lay.block_half_slice(q, 1 - c)), (x, y, 1 - c)))
        return copies

    return plan


def _sibling_plan(n):
    def plan(srcs, lands, dry=False):
        if dry:
            return [None] * n
        x, y, c, _ = _place()
        return [_Copy(s, ra, ra, (x, y, 1 - c)) for s, ra in zip(srcs, lands)]

    return plan


def _chips_plan(layouts):
    def plan(srcs, lands, dry=False):
        if dry:
            return [None] * (3 * len(layouts))
        x, y, c, _ = _place()
        copies = []
        for s, rb, lay in zip(srcs, lands, layouts):
            for j, (qx, qy, q) in enumerate(_other_chips(x, y)):
                copies.append(_Copy(_at(s, lay.block_in_half_slice(q)), rb.at[j], rb.at[j], (qx, qy, c)))
        return copies

    return plan


def _join_plan(layouts):
    def plan(srcs, lands, dry=False):
        if dry:
            return [None] * len(layouts)
        x, y, c, _ = _place()
        copies = []
        for g, lay in zip(srcs, layouts):
            mine = _at(g, lay.shard_half_slice(c))
            copies.append(_Copy(mine, mine, _at(g, lay.shard_half_slice(1 - c)), (x, y, 1 - c)))
        return copies

    return plan


def _pack(arrays):
    flat = [a.reshape(-1) for a in arrays]
    sizes = [f.shape[0] for f in flat]
    total = sum(sizes)
    rows = _round_up(-(-total // LANES), 8)
    flat.append(jnp.zeros((rows * LANES - total,), F32))
    return jnp.concatenate(flat).reshape(rows, LANES)


def _unpack(buf, shapes):
    flat = buf.reshape(-1)
    out, pos = [], 0
    for shp in shapes:
        n = 1
        for d in shp:
            n *= d
        out.append(flat[pos:pos + n].reshape(shp))
        pos += n
    return out


def _pad_to(a, rows, cols):
    return jnp.pad(a, ((0, rows - a.shape[0]), (0, cols - a.shape[1])))


def kernel(x, mem, g_mix, w_in, conv_a_w, conv_a_b, ln_a_g, ln_a_b, conv_b_w, w_out, g_xattn, g_mem, w_q, w_k, w_v, w_o, g_ffn, w_gate, w_up, conv_f_w, w_down, g_final, loss_target, m_g_mix, m_w_in, m_conv_a_w, m_conv_a_b, m_ln_a_g, m_ln_a_b, m_conv_b_w, m_w_out, m_g_xattn, m_g_mem, m_w_q, m_w_k, m_w_v, m_w_o, m_g_ffn, m_w_gate, m_w_up, m_conv_f_w, m_w_down, m_g_final, v_g_mix, v_w_in, v_conv_a_w, v_conv_a_b, v_ln_a_g, v_ln_a_b, v_conv_b_w, v_w_out, v_g_xattn, v_g_mem, v_w_q, v_w_k, v_w_v, v_w_o, v_g_ffn, v_w_gate, v_w_up, v_conv_f_w, v_w_down, v_g_final):
    T, D = x.shape[1], x.shape[2]
    in_sh = w_in.shape[2]
    cw_sh = conv_a_w.shape[2]
    cw = N_CHIPS * cw_sh
    f_sh = w_gate.shape[2]
    fp = _round_up(f_sh, 256)
    F = N_CHIPS * fp
    rs = w_out.shape[1]
    c_idx = lax.axis_index("c")
    p_idx = 2 * lax.axis_index("x") + lax.axis_index("y")

    def t_(a):
        return jnp.swapaxes(a[0], 0, 1)

    big = {
        "w_in": (lambda: w_in[0].astype(BF16), _Sharded((D, N_CHIPS * in_sh), 1)),
        "w_out": (lambda: w_out[0].astype(BF16), _Sharded((N_CHIPS * rs, D), 0)),
        "w_q": (lambda: w_q[0].astype(BF16), _Sharded((D, D), 0)),
        "w_k": (lambda: w_k[0].astype(BF16), _Sharded((D, D), 0)),
        "w_v": (lambda: w_v[0].astype(BF16), _Sharded((D, D), 0)),
        "w_o": (lambda: w_o[0].astype(BF16), _Sharded((D, D), 0)),
        "w_gate": (lambda: _pad_to(t_(w_gate).astype(BF16), fp, D), _Sharded((F, D), 0)),
        "w_up": (lambda: _pad_to(t_(w_up).astype(BF16), fp, D), _Sharded((F, D), 0)),
        "w_down": (lambda: _pad_to(w_down[0].astype(BF16), fp, D), _Sharded((F, D), 0)),
    }
    names = list(big)
    lay = {k: big[k][1] for k in names}
    c_arr = c_idx.astype(jnp.int32).reshape(1)
    pc_arr = jnp.stack([p_idx, c_idx]).astype(jnp.int32)

    conv_shapes = [(KA_ROWS, cw_sh), (KS_ROWS, cw_sh), (KS_ROWS, fp)]
    conv_pack = _pack([_pad_to(conv_a_w[0], KA_ROWS, cw_sh), _pad_to(conv_b_w[0], KS_ROWS, cw_sh),
                       _pad_to(conv_f_w[0], KS_ROWS, fp)])
    conv_all = _allgather_small(conv_pack, name="allgather_conv", reduce=False)
    per_chip = [_unpack(conv_all[2 * q], conv_shapes) for q in range(N_CHIPS)]
    wa = jnp.concatenate([pc[0] for pc in per_chip], axis=1)
    wb = jnp.concatenate([pc[1] for pc in per_chip], axis=1)
    wf = jnp.concatenate([pc[2] for pc in per_chip], axis=1)

    tok = conv_all

    in_half = _Sharded((D // 2, N_CHIPS * in_sh), 1)
    sources = {k: big[k] for k in names if k != "w_in"}
    sources["w_in_lo"] = (lambda: w_in[0, :D // 2].astype(BF16), in_half)
    sources["w_in_hi"] = (lambda: w_in[0, D // 2:].astype(BF16), in_half)
    gather_groups = {"in_lo": ["w_in_lo"], "in_hi": ["w_in_hi"], "out": ["w_out"], "qkv": ["w_q", "w_k", "w_v"],
                     "o": ["w_o"], "gate": ["w_gate"], "up": ["w_up"], "down": ["w_down"]}
    gathers = {}
    W = {}
    relays = {}

    def shards(tag):
        return [sources[k][0]() for k in gather_groups[tag]]

    def gather_start(tag, srcs, after):
        grp = gather_groups[tag]
        lays = [sources[k][1] for k in grp]
        plan = _gather_plan(lays)
        ssem, rsem, srcs, lands, token = _exchange_start(
            srcs, [(l.shape, BF16) for l in lays], plan, after, name=f"gather_start_{tag}")
        gathers[tag] = (ssem, rsem, srcs, lands, plan, lays, grp)
        return token

    def relay(tag, after, also=()):
        ssem, rsem, srcs, lands, plan, lays, grp = gathers[tag]
        _, lands = _exchange_wait(ssem, rsem, srcs, lands, plan, (after, *also), name=f"gather_wait_{tag}")
        plan = _forward_plan(lays)
        ssem, rsem, lands, _, token = _exchange_start(lands, [], plan, after, name=f"gather_forward_start_{tag}")
        relays[tag] = (ssem, rsem, lands, plan, grp)
        return token

    def gathered(tag, after):
        if tag not in relays:
            relay(tag, after)
        ssem, rsem, lands, plan, grp = relays[tag]
        lands, _ = _exchange_wait(ssem, rsem, lands, [], plan, after, name=f"gather_forward_wait_{tag}")
        W.update(zip(grp, lands))

    tok = gather_start("in_lo", shards("in_lo"), tok)
    tok = gather_start("in_hi", shards("in_hi"), tok)
    later = {tag: shards(tag) for tag in gather_groups if tag not in gathers}
    tok = relay("in_lo", tok, also=[a for srcs in later.values() for a in srcs])
    for tag, srcs in later.items():
        tok = gather_start(tag, srcs, tok)

    xs, tgt, mems = x[0], loss_target[0], mem[0]
    g_mem2, g_final2 = g_mem[None, :], g_final[None, :]

    memn, rm = _norm_fwd(mems, g_mem2, name="norm_mem")
    xn1, r1 = _norm_fwd(xs, g_mix, name="norm_mix")
    gathered("in_lo", (xn1, tok))
    proj = _mm(xn1, W["w_in_lo"], a_k=(0, D // 2), name="mm_in_lo")
    gathered("in_hi", proj)
    proj = _mm(xn1, W["w_in_hi"], a_k=(D // 2, D // 2), add=proj, name="mm_in_hi")
    ftok = relay("out", proj)
    u3, u1 = _mix_a_fwd(proj, wa, conv_a_b, ln_a_g, ln_a_b, cw=cw, deps=[ftok], name="mix_a_fwd")
    vb = _mix_b_fwd(proj, wb, cw=cw, sw=cw, name="mix_b_fwd")
    mix = jnp.concatenate([u3, vb], axis=1)
    gathered("out", mix)
    ftok = relay("qkv", mix)
    h1 = _mm(mix, W["w_out"], add=xs, deps=[ftok], name="mm_out")
    xn2, r2 = _norm_fwd(h1, g_xattn, name="norm_xattn")
    gathered("qkv", xn2)
    ftok = relay("o", xn2)
    q = _mm(xn2, W["w_q"], out_dtype=BF16, deps=[ftok], name="mm_q")
    k = _mm(memn, W["w_k"], out_dtype=BF16, name="mm_k")
    vm = _mm(memn, W["w_v"], out_dtype=BF16, name="mm_v")
    o = _attn_fwd(q, k, vm, name="attn_fwd")
    gathered("o", o)
    h2 = _mm(o, W["w_o"], add=h1, name="mm_o")
    relay("gate", h2)
    xn3, r3 = _norm_fwd(h2, g_ffn, name="norm_ffn")
    gathered("gate", xn3)
    gp = _mm(xn3, W["w_gate"], tb=True, name="mm_gate")
    gathered("up", gp)
    up = _mm(xn3, W["w_up"], tb=True, name="mm_up")
    ftok = relay("down", up)
    f = _ffn_act_fwd(gp, up, wf, deps=[ftok], name="ffn_act_fwd")
    gathered("down", f)
    def mm_over_ffn(a, b, *, name, add=None, deps=()):
        lo = _mm(a, b, a_k=(0, F // 2), b_k=(0, F // 2), add=add, deps=deps, name=name + "_lo")
        return _mm(a, b, a_k=(F // 2, F // 2), b_k=(F // 2, F // 2), add=lo, name=name + "_hi")

    h3 = mm_over_ffn(f, W["w_down"], add=h2, name="mm_down")

    G = {}
    other_arr = 1 - c_arr

    def grad_half(k, which, **kw):
        xk, dyk = G[k]
        return _mm(xk, dyk, ta=True, half=("n" if lay[k].axis == 0 else "m", which), **kw)

    def siblings_start(tag, grp, after):
        lays = [lay[k] for k in grp]
        firsts = [grad_half(k, other_arr, deps=[after], name=f"mm_dw_{k}_sibling") for k in grp]
        plan = _sibling_plan(len(grp))
        ssem, rsem, srcs, lands, token = _exchange_start(
            firsts, [(l.half_shape(), F32) for l in lays], plan, after, name=f"rs_siblings_start_{tag}")
        return (tag, grp, lays, plan, ssem, rsem, srcs, lands), token

    def chips_start(state, after):
        tag, grp, lays, plan, ssem, rsem, srcs, lands = state
        _, lands = _exchange_wait(ssem, rsem, srcs, lands, plan, after, name=f"rs_siblings_wait_{tag}")
        sums = [grad_half(k, c_arr, add=ra, out_dtype=BF16, name=f"mm_dw_{k}_own") for k, ra in zip(grp, lands)]
        plan = _chips_plan(lays)
        ssem, rsem, sums, lands, token = _exchange_start(
            sums, [((N_CHIPS - 1,) + l.block_half_shape(), BF16) for l in lays], plan, after, name=f"rs_chips_start_{tag}")
        return (tag, grp, lays, plan, ssem, rsem, sums, lands), token

    def join_start(state, after):
        tag, grp, lays, plan, ssem, rsem, sums, lands = state
        sums, lands = _exchange_wait(ssem, rsem, sums, lands, plan, after, name=f"rs_chips_wait_{tag}")
        halves = [_sum_chips(cs, rb, pc_arr, l, name=f"sum_chips_{k}") for k, cs, rb, l in zip(grp, sums, lands, lays)]
        plan = _join_plan(lays)
        ssem, rsem, halves, _, token = _exchange_start(halves, [], plan, after[0], name=f"rs_join_start_{tag}")
        return (tag, grp, plan, ssem, rsem, halves), token

    def join_finish(state, after):
        tag, grp, plan, ssem, rsem, halves = state
        halves, _ = _exchange_wait(ssem, rsem, halves, [], plan, after, name=f"rs_join_wait_{tag}")
        return dict(zip(grp, halves))

    loss_rows, dh3, dh3b, dg_final = _loss_head(h3, g_final2, tgt, name="loss_head")
    df = _mm(dh3b, W["w_down"], tb=True, out_dtype=BF16, name="mm_d_f")
    G["w_down"] = (f, dh3b)
    rs_down, tok = siblings_start("down", ["w_down"], tok)
    dgp, dup, dwf = _ffn_act_bwd(gp, up, df, wf, name="ffn_act_bwd")
    G["w_gate"], G["w_up"] = (dgp, xn3), (dup, xn3)
    rs_ffn, tok = siblings_start("ffn", ["w_gate", "w_up"], tok)
    rs_down, tok = chips_start(rs_down, tok)
    dxn3 = mm_over_ffn(dgp, W["w_gate"], deps=[tok], name="mm_dxn3_gate")
    dxn3 = mm_over_ffn(dup, W["w_up"], add=dxn3, name="mm_dxn3_up")
    dh2, dh2b, dg_ffn = _norm_bwd(h2, r3, g_ffn, dxn3, dh3, name="norm_ffn_bwd")
    rs_ffn, tok = chips_start(rs_ffn, dh2b)
    do = _mm(dh2b, W["w_o"], tb=True, out_dtype=BF16, deps=[tok], name="mm_d_o")
    G["w_o"] = (o, dh2b)
    dq, dk, dvm = _attn_bwd(q, k, vm, do, name="attn_bwd")
    dkb, dvb = dk.astype(BF16), dvm.astype(BF16)
    G["w_q"], G["w_k"], G["w_v"] = (xn2, dq), (memn, dkb), (memn, dvb)
    rs_att, tok = siblings_start("att", ["w_o", "w_q", "w_k", "w_v"], tok)
    dxn2 = _mm(dq, W["w_q"], tb=True, deps=[tok], name="mm_dxn2")
    dmemn = _mm(dkb, W["w_k"], tb=True, name="mm_dmem_k")
    dmemn = _mm(dvb, W["w_v"], tb=True, add=dmemn, name="mm_dmem_v")
    dg_mem = _norm_bwd(mems, rm, g_mem2, dmemn, None, name="norm_mem_bwd", want_dh=False)
    dh1, dh1b, dg_xattn = _norm_bwd(h1, r2, g_xattn, dxn2, dh2, name="norm_xattn_bwd")
    rs_att, tok = chips_start(rs_att, dh1b)
    dmix = _mm(dh1b, W["w_out"], tb=True, deps=[tok], name="mm_d_mix")
    G["w_out"] = (mix, dh1b)
    rs_out, tok = siblings_start("out", ["w_out"], tok)
    dav, dag, dwa, dba, dlg, dlb = _mix_a_bwd(proj, u1, dmix, wa, ln_a_g, ln_a_b, cw=cw, name="mix_a_bwd")
    dbg, dcg, dbh, dwb = _mix_b_bwd(proj, dmix, wb, cw=cw, sw=cw, name="mix_b_bwd")
    dproj = jnp.concatenate([dav, dag, dbg, dcg, dbh], axis=1)
    G["w_in"] = (xn1, dproj)
    rs_in, tok = siblings_start("in", ["w_in"], tok)
    rs_out, tok = chips_start(rs_out, tok)
    dxn1 = _mm(dproj, W["w_in_lo"], tb=True, out_cols=(0, D), deps=[tok], name="mm_dxn1_lo")
    dxn1 = _mm(dproj, W["w_in_hi"], tb=True, out_cols=(D // 2, D), into=dxn1, name="mm_dxn1_hi")
    dx, _, dg_mix = _norm_bwd(xs, r1, g_mix, dxn1, dh1, name="norm_mix_bwd")
    rs_in, tok = chips_start(rs_in, dx)

    loss_part = jnp.sum(loss_rows).reshape(1, 1)
    small_parts = [dg_mix, dba, dlg, dlb, dg_xattn, dg_mem, dg_ffn, dg_final, dwa, dwb, dwf, loss_part]
    small_shapes = [a.shape for a in small_parts]
    reduced = _allgather_small(_pack(small_parts), name="allreduce_small", reduce=True)
    (sg_mix, sba, slg, slb, sg_xattn, sg_mem, sg_ffn, sg_final, swa, swb, swf, loss_sum) = _unpack(reduced, small_shapes)
    loss = loss_sum.reshape(())
    ga_w = lax.dynamic_slice(swa, (0, p_idx * cw_sh), (KA, cw_sh))
    gb_w = lax.dynamic_slice(swb, (0, p_idx * cw_sh), (KS, cw_sh))
    gf_w = lax.dynamic_slice(swf, (0, p_idx * fp), (KS, f_sh))

    weights = dict(g_mix=g_mix, w_in=w_in, conv_a_w=conv_a_w, conv_a_b=conv_a_b, ln_a_g=ln_a_g, ln_a_b=ln_a_b,
                   conv_b_w=conv_b_w, w_out=w_out, g_xattn=g_xattn, g_mem=g_mem, w_q=w_q, w_k=w_k, w_v=w_v, w_o=w_o,
                   g_ffn=g_ffn, w_gate=w_gate, w_up=w_up, conv_f_w=conv_f_w, w_down=w_down, g_final=g_final)
    m_in = dict(g_mix=m_g_mix, w_in=m_w_in, conv_a_w=m_conv_a_w, conv_a_b=m_conv_a_b, ln_a_g=m_ln_a_g, ln_a_b=m_ln_a_b,
                conv_b_w=m_conv_b_w, w_out=m_w_out, g_xattn=m_g_xattn, g_mem=m_g_mem, w_q=m_w_q, w_k=m_w_k, w_v=m_w_v,
                w_o=m_w_o, g_ffn=m_g_ffn, w_gate=m_w_gate, w_up=m_w_up, conv_f_w=m_conv_f_w, w_down=m_w_down,
                g_final=m_g_final)
    v_in = dict(g_mix=v_g_mix, w_in=v_w_in, conv_a_w=v_conv_a_w, conv_a_b=v_conv_a_b, ln_a_g=v_ln_a_g, ln_a_b=v_ln_a_b,
                conv_b_w=v_conv_b_w, w_out=v_w_out, g_xattn=v_g_xattn, g_mem=v_g_mem, w_q=v_w_q, w_k=v_w_k, w_v=v_w_v,
                w_o=v_w_o, g_ffn=v_g_ffn, w_gate=v_w_gate, w_up=v_w_up, conv_f_w=v_conv_f_w, w_down=v_w_down,
                g_final=v_g_final)
    order = list(weights)
    grads = dict(g_mix=sg_mix, conv_a_w=ga_w, conv_a_b=sba, ln_a_g=slg, ln_a_b=slb, conv_b_w=gb_w, g_xattn=sg_xattn,
                 g_mem=sg_mem, g_ffn=sg_ffn, conv_f_w=gf_w, g_final=sg_final)
    grads = {k: g.reshape(weights[k].shape) for k, g in grads.items()}

    delta, new_m, new_v = {}, {}, {}
    small = [k for k in order if k not in big]
    small_shapes = [weights[k].shape for k in small]
    packed = [_pack([src[k] for k in small]) for src in (weights, grads, m_in, v_in)]
    d_, m_, v_ = _adamw(*packed, name="adamw_small")
    for k, dd, mm_, vv in zip(small, _unpack(d_, small_shapes), _unpack(m_, small_shapes), _unpack(v_, small_shapes)):
        delta[k], new_m[k], new_v[k] = dd, mm_, vv

    transposed = ("w_gate", "w_up")

    def update(shard_grads):
        last = None
        for k, g in shard_grads.items():
            view = t_ if k in transposed else (lambda a: a[0])
            back = (lambda a: jnp.swapaxes(a, 0, 1)[None]) if k in transposed else (lambda a: a[None])
            padded = g.shape != view(weights[k]).shape
            outs = _adamw(view(weights[k]), g, view(m_in[k]), view(v_in[k]), emit_grad=padded, name=f"adamw_{k}")
            delta[k], new_m[k], new_v[k] = back(outs[0]), back(outs[1]), back(outs[2])
            grads[k] = back(outs[3] if padded else g)
            last = outs[0]
        return last

    after = (d_, tok)
    joining = None
    for state in (rs_down, rs_ffn, rs_att, rs_out, rs_in):
        started, token = join_start(state, after)
        after = (token,)
        if joining is not None:
            after = (update(join_finish(joining, token)), token)
        joining = started
    update(join_finish(joining, after[0]))

    return (loss, dx[None], *[grads[k] for k in order], *[delta[k] for k in order],
            *[new_m[k] for k in order], *[new_v[k] for k in order])
```

```python
import jax
import jax.numpy as jnp
from jax import lax
from jax.experimental import pallas as pl
from jax.experimental.pallas import tpu as pltpu

F32 = jnp.float32
BF16 = jnp.bfloat16
EPS = 1e-6
N_HEADS = 4
GROUP = 128
KA = 31
KS = 3
KA_ROWS = 32
KS_ROWS = 8
HALO_A = 32
HALO_S = 8
CHUNK_ROWS = 16
CHUNK_COLS = 256
N_CHIPS = 4
N_DEV = 8
LANES = 128
VMEM_LIMIT = 56 * 1024 * 1024
MM_WIDE_TILE_ELEMS = 4096 * 512
ADAMW_BLOCK_BYTES = 2 << 20
MESH = pl.DeviceIdType.MESH
ANY = pl.BlockSpec(memory_space=pl.ANY)

ADAM_LR = 0.001
ADAM_B1 = 0.9
ADAM_B2 = 0.999
ADAM_EPS = 1e-08
ADAM_WD = 0.01
ADAM_STEP = 10


def _pick(dim, pref, mult=LANES):
    if dim <= pref:
        return dim
    t = (pref // mult) * mult
    while t >= mult:
        if dim % t == 0:
            return t
        t -= mult
    return dim


def _round_up(n, m):
    return ((n + m - 1) // m) * m


def _params(*sem):
    return pltpu.CompilerParams(dimension_semantics=sem, vmem_limit_bytes=VMEM_LIMIT)


def _sigmoid(x):
    return jax.nn.sigmoid(x)


def _mm(a, b, *, name, ta=False, tb=False, out_dtype=F32, add=None, deps=(), half=None, a_k=None, b_k=None,
        out_cols=None, into=None, tm=1024, tn=512, tk=5632):
    if ta:
        K, M = a.shape
    else:
        M, K = a.shape
    if tb:
        N, K2 = b.shape
    else:
        K2, N = b.shape
    k0 = kb0 = 0
    if a_k is not None:
        k0, K = a_k
    if b_k is not None:
        kb0, K2 = b_k
    assert K == K2, (a.shape, b.shape)
    half_dim, half_sel = half if half is not None else (None, None)
    if half_dim == "m":
        M //= 2
    elif half_dim == "n":
        N //= 2
    tm, tk = _pick(M, tm), _pick(K, tk)
    nk = K // tk
    tn = _pick(N, 2 * tn if nk == 1 and K * tn <= MM_WIDE_TILE_ELEMS else tn)
    mi, nj = M // tm, N // tn
    n0, n_total = out_cols if out_cols is not None else (0, N)
    assert k0 % tk == 0 and kb0 % tk == 0 and n0 % tn == 0
    koff, kboff, noff = k0 // tk, kb0 // tk, n0 // tn

    def a_map(i, j, k, *s):
        i = i + s[0][0] * mi if half_dim == "m" else i
        return (k + koff, i) if ta else (i, k + koff)

    def b_map(i, j, k, *s):
        j = j + s[0][0] * nj if half_dim == "n" else j
        return (j, k + kboff) if tb else (k + kboff, j)

    a_spec = pl.BlockSpec((tk, tm) if ta else (tm, tk), a_map)
    b_spec = pl.BlockSpec((tn, tk) if tb else (tk, tn), b_map)
    o_spec = pl.BlockSpec((tm, tn), lambda i, j, k, *s: (i, j + noff))
    add_spec = pl.BlockSpec((tm, tn), lambda i, j, k, *s: (i, j))
    dims = (((0,) if ta else (1,), (1,) if tb else (0,)), ((), ()))
    has_add = add is not None
    n_pre = 0 if half is None else 1
    n_after = len(deps) + (0 if into is None else 1)

    def body(*refs):
        refs = refs[n_pre:]
        a_ref, b_ref = refs[:2]
        add_ref = refs[2] if has_add else None
        o_ref = refs[(3 if has_add else 2) + n_after]
        acc_ref = refs[-1] if nk > 1 else None
        k = pl.program_id(2)
        part = lax.dot_general(a_ref[...], b_ref[...], dims, preferred_element_type=F32)

        def finish(r):
            if add_ref is not None:
                r = add_ref[...] + r
            o_ref[...] = r.astype(out_dtype)

        if nk == 1:
            finish(part)
        else:
            @pl.when(k == 0)
            def _():
                acc_ref[...] = part

            @pl.when(jnp.logical_and(k > 0, k < nk - 1))
            def _():
                acc_ref[...] += part

            @pl.when(k == nk - 1)
            def _():
                finish(acc_ref[...] + part)

    in_specs = [a_spec, b_spec] + ([add_spec] if has_add else []) + [ANY] * n_after
    args = (a, b) + ((add,) if has_add else ()) + tuple(deps) + ((into,) if into is not None else ())
    aliases = {n_pre + len(args) - 1: 0} if into is not None else {}
    scratch = [pltpu.VMEM((tm, tn), F32)] if nk > 1 else []
    grid = (mi, nj, nk)
    if half is None:
        grid_spec = dict(grid=grid, in_specs=in_specs, out_specs=o_spec, scratch_shapes=scratch)
    else:
        args = (half_sel,) + args
        grid_spec = dict(grid_spec=pltpu.PrefetchScalarGridSpec(
            num_scalar_prefetch=1, grid=grid, in_specs=in_specs, out_specs=o_spec, scratch_shapes=scratch))
    return pl.pallas_call(
        body,
        name=name,
        out_shape=jax.ShapeDtypeStruct((M, n_total), out_dtype),
        input_output_aliases=aliases,
        compiler_params=_params("parallel", "parallel", "arbitrary"),
        **grid_spec,
    )(*args)


def _norm_fwd(h, g, *, name):
    T, D = h.shape
    R = _pick(T, 256, 8)

    def body(h_ref, g_ref, xn_ref, r_ref):
        x = h_ref[...]
        r = lax.rsqrt(jnp.mean(x * x, axis=-1, keepdims=True) + EPS)
        xn_ref[...] = ((x * r) * g_ref[...]).astype(BF16)
        r_ref[...] = r

    return pl.pallas_call(
        body,
        name=name,
        out_shape=(jax.ShapeDtypeStruct((T, D), BF16), jax.ShapeDtypeStruct((T, 1), F32)),
        grid=(T // R,),
        in_specs=[pl.BlockSpec((R, D), lambda i: (i, 0)), pl.BlockSpec((1, D), lambda i: (0, 0))],
        out_specs=(pl.BlockSpec((R, D), lambda i: (i, 0)), pl.BlockSpec((R, 1), lambda i: (i, 0))),
        compiler_params=_params("parallel"),
    )(h, g)


def _norm_bwd(h, r, g, dxn, dres, *, name, want_dh=True):
    T, D = h.shape
    R = _pick(T, 256, 8)
    has_res = dres is not None

    def body(*refs):
        h_ref, r_ref, g_ref, dxn_ref = refs[:4]
        pos = 4
        dres_ref = None
        if has_res:
            dres_ref = refs[pos]
            pos += 1
        if want_dh:
            dh_ref, dhb_ref, dg_ref = refs[pos:pos + 3]
        else:
            dg_ref = refs[pos]
        i = pl.program_id(0)
        rr = r_ref[...]
        hn = h_ref[...] * rr
        d = dxn_ref[...].astype(F32)
        gd = d * g_ref[...]
        part = jnp.sum(d * hn, axis=0, keepdims=True)

        @pl.when(i == 0)
        def _():
            dg_ref[...] = part

        @pl.when(i > 0)
        def _():
            dg_ref[...] += part

        if want_dh:
            dh = rr * (gd - hn * jnp.mean(gd * hn, axis=-1, keepdims=True))
            if dres_ref is not None:
                dh = dres_ref[...] + dh
            dh_ref[...] = dh
            dhb_ref[...] = dh.astype(BF16)

    row = pl.BlockSpec((R, D), lambda i: (i, 0))
    vec = pl.BlockSpec((1, D), lambda i: (0, 0))
    in_specs = [row, pl.BlockSpec((R, 1), lambda i: (i, 0)), vec, row] + ([row] if has_res else [])
    args = (h, r, g, dxn) + ((dres,) if has_res else ())
    if want_dh:
        out_shape = (jax.ShapeDtypeStruct((T, D), F32), jax.ShapeDtypeStruct((T, D), BF16), jax.ShapeDtypeStruct((1, D), F32))
        out_specs = (row, row, vec)
    else:
        out_shape = jax.ShapeDtypeStruct((1, D), F32)
        out_specs = vec
    return pl.pallas_call(
        body, name=name, out_shape=out_shape, grid=(T // R,), in_specs=in_specs, out_specs=out_specs,
        compiler_params=_params("arbitrary"),
    )(*args)


def _loss_head(h, g, tgt, *, name):
    T, D = h.shape
    R = _pick(T, 256, 8)

    def body(h_ref, g_ref, t_ref, loss_ref, dh_ref, dhb_ref, dg_ref):
        i = pl.program_id(0)
        x = h_ref[...]
        gg = g_ref[...]
        r = lax.rsqrt(jnp.mean(x * x, axis=-1, keepdims=True) + EPS)
        hn = x * r
        e = hn * gg - t_ref[...]
        loss_ref[...] = 0.5 * jnp.mean(e * e, axis=-1, keepdims=True)
        dy = e * (1.0 / D)
        gd = dy * gg
        dh = r * (gd - hn * jnp.mean(gd * hn, axis=-1, keepdims=True))
        dh_ref[...] = dh
        dhb_ref[...] = dh.astype(BF16)
        part = jnp.sum(dy * hn, axis=0, keepdims=True)

        @pl.when(i == 0)
        def _():
            dg_ref[...] = part

        @pl.when(i > 0)
        def _():
            dg_ref[...] += part

    row = pl.BlockSpec((R, D), lambda i: (i, 0))
    vec = pl.BlockSpec((1, D), lambda i: (0, 0))
    return pl.pallas_call(
        body,
        name=name,
        out_shape=(jax.ShapeDtypeStruct((T, 1), F32), jax.ShapeDtypeStruct((T, D), F32),
                   jax.ShapeDtypeStruct((T, D), BF16), jax.ShapeDtypeStruct((1, D), F32)),
        grid=(T // R,),
        in_specs=[row, vec, row],
        out_specs=(pl.BlockSpec((R, 1), lambda i: (i, 0)), row, row, vec),
        compiler_params=_params("arbitrary"),
    )(h, g, tgt)


_NT = (((1,), (1,)), ((), ()))
_TN = (((0,), (0,)), ((), ()))
_NN = (((1,), (0,)), ((), ()))


def _softmax_rows(s):
    m = jnp.max(s, axis=-1, keepdims=True)
    e = jnp.exp(s - m)
    return e / jnp.sum(e, axis=-1, keepdims=True)


def _attn_fwd(q, k, v, *, name):
    T, D = q.shape
    ML = k.shape[0]
    dh = D // N_HEADS
    scale = dh ** -0.5
    R = _pick(T, 512, 16)

    def body(q_ref, k_ref, v_ref, o_ref):
        s = lax.dot_general(q_ref[...], k_ref[...], _NT, preferred_element_type=F32) * scale
        p = _softmax_rows(s)
        o_ref[...] = lax.dot_general(p.astype(BF16), v_ref[...], _NN, preferred_element_type=F32).astype(BF16)

    qs = pl.BlockSpec((R, dh), lambda i, h: (i, h))
    ks = pl.BlockSpec((ML, dh), lambda i, h: (0, h))
    return pl.pallas_call(
        body, name=name, out_shape=jax.ShapeDtypeStruct((T, D), BF16), grid=(T // R, N_HEADS),
        in_specs=[qs, ks, ks], out_specs=qs, compiler_params=_params("parallel", "parallel"),
    )(q, k, v)


def _attn_bwd(q, k, v, do, *, name):
    T, D = q.shape
    ML = k.shape[0]
    dh = D // N_HEADS
    scale = dh ** -0.5
    R = _pick(T, 512, 16)

    def body(q_ref, k_ref, v_ref, do_ref, dq_ref, dk_ref, dv_ref):
        i = pl.program_id(1)
        qq, kk, vv, dd = q_ref[...], k_ref[...], v_ref[...], do_ref[...]
        s = lax.dot_general(qq, kk, _NT, preferred_element_type=F32) * scale
        p = _softmax_rows(s)
        dp = lax.dot_general(dd, vv, _NT, preferred_element_type=F32)
        dv_part = lax.dot_general(p.astype(BF16), dd, _TN, preferred_element_type=F32)
        ds = (p * (dp - jnp.sum(p * dp, axis=-1, keepdims=True)) * scale).astype(BF16)
        dq_ref[...] = lax.dot_general(ds, kk, _NN, preferred_element_type=F32).astype(BF16)
        dk_part = lax.dot_general(ds, qq, _TN, preferred_element_type=F32)

        @pl.when(i == 0)
        def _():
            dk_ref[...] = dk_part
            dv_ref[...] = dv_part

        @pl.when(i > 0)
        def _():
            dk_ref[...] += dk_part
            dv_ref[...] += dv_part

    qs = pl.BlockSpec((R, dh), lambda h, i: (i, h))
    ks = pl.BlockSpec((ML, dh), lambda h, i: (0, h))
    return pl.pallas_call(
        body,
        name=name,
        out_shape=(jax.ShapeDtypeStruct((T, D), BF16), jax.ShapeDtypeStruct((ML, D), F32), jax.ShapeDtypeStruct((ML, D), F32)),
        grid=(N_HEADS, T // R),
        in_specs=[qs, ks, ks, qs],
        out_specs=(qs, ks, ks),
        compiler_params=_params("parallel", "arbitrary"),
    )(q, k, v, do)


def _prev_halo(R, halo, CT, col):
    per = R // halo
    return lambda i, j: (jnp.maximum(i * per - 1, 0), col(j))


def _next_halo(R, halo, T, col):
    per = R // halo
    last = T // halo - 1
    return lambda i, j: (jnp.minimum((i + 1) * per, last), col(j))


def _swap(f):
    return lambda j, i: f(i, j)


def _ffn_act_fwd(gp, up, wf, *, name, deps=()):
    T, F = gp.shape
    R = _pick(T, 256, CHUNK_ROWS)
    CT = _pick(F, 1024, CHUNK_COLS)
    H = HALO_S
    RC, CC = CHUNK_ROWS, min(CHUNK_COLS, CT)

    def body(g_ref, gh_ref, u_ref, w_ref, *rest):
        f_ref, ext = rest[len(deps):]
        i = pl.program_id(0)
        ext[pl.ds(0, H), :] = jnp.where(i > 0, gh_ref[...], 0.0)
        ext[pl.ds(H, R), :] = g_ref[...]
        for c0 in range(0, CT, CC):
            cols = pl.ds(c0, CC)
            w0, w1, w2 = [jnp.broadcast_to(w_ref[pl.ds(k, 1), cols], (RC, CC)) for k in range(KS)]
            for r0 in range(0, R, RC):
                g = ext[pl.ds(r0 + H - 2, RC), cols] * w0
                g = g + ext[pl.ds(r0 + H - 1, RC), cols] * w1
                g = g + ext[pl.ds(r0 + H, RC), cols] * w2
                f_ref[pl.ds(r0, RC), cols] = (g * _sigmoid(g) * u_ref[pl.ds(r0, RC), cols]).astype(BF16)

    main = pl.BlockSpec((R, CT), lambda i, j: (i, j))
    return pl.pallas_call(
        body,
        name=name,
        out_shape=jax.ShapeDtypeStruct((T, F), BF16),
        grid=(T // R, F // CT),
        in_specs=[main, pl.BlockSpec((H, CT), _prev_halo(R, H, CT, lambda j: j)), main,
                  pl.BlockSpec((KS_ROWS, CT), lambda i, j: (0, j))] + [ANY] * len(deps),
        out_specs=main,
        scratch_shapes=[pltpu.VMEM((R + H, CT), F32)],
        compiler_params=_params("parallel", "parallel"),
    )(gp, gp, up, wf, *deps)


def _ffn_act_bwd(gp, up, df, wf, *, name):
    T, F = gp.shape
    R = _pick(T, 256, CHUNK_ROWS)
    CT = _pick(F, 512, CHUNK_COLS)
    H = HALO_S
    HB = 16
    n_t = T // R
    RC, CC = CHUNK_ROWS, min(CHUNK_COLS, CT)

    def body(g_ref, gp_ref, gn_ref, u_ref, un_ref, d_ref, dn_ref, w_ref, dg_out, du_out, dw_ref, ext, dge):
        i = pl.program_id(1)
        last = i == n_t - 1
        ext[pl.ds(0, H), :] = jnp.where(i > 0, gp_ref[...], 0.0)
        ext[pl.ds(H, R), :] = g_ref[...]
        ext[pl.ds(H + R, H), :] = gn_ref[...]

        def dact(g, u, d):
            sg = _sigmoid(g)
            return d * u * (sg * (1.0 + g * (1.0 - sg))), d * (g * sg)

        dw_rows = [[] for _ in range(KS)]
        for c0 in range(0, CT, CC):
            cols = pl.ds(c0, CC)
            w = [jnp.broadcast_to(w_ref[pl.ds(k, 1), cols], (RC, CC)) for k in range(KS)]
            acc = [jnp.zeros((RC, CC), F32) for _ in range(KS)]
            for r0 in range(0, R, RC):
                rows = pl.ds(r0, RC)
                taps = [ext[pl.ds(r0 + H - 2 + k, RC), cols] for k in range(KS)]
                g = taps[0] * w[0] + taps[1] * w[1] + taps[2] * w[2]
                dg, du = dact(g, u_ref[rows, cols], d_ref[rows, cols].astype(F32))
                du_out[rows, cols] = du.astype(BF16)
                dge[rows, cols] = dg
                acc = [a + dg * t for a, t in zip(acc, taps)]
            taps = [ext[pl.ds(R + H - 2 + k, H), cols] for k in range(KS)]
            g = taps[0] * w[0][:H] + taps[1] * w[1][:H] + taps[2] * w[2][:H]
            d_next = jnp.where(last, 0.0, dn_ref[pl.ds(0, H), cols].astype(F32))
            dg_next, _ = dact(g, un_ref[:, cols], d_next)
            dge[pl.ds(R, H), cols] = jnp.where(last, 0.0, dg_next)
            for r0 in range(0, R, RC):
                dgp = (dge[pl.ds(r0, RC), cols] * w[2] + dge[pl.ds(r0 + 1, RC), cols] * w[1]
                       + dge[pl.ds(r0 + 2, RC), cols] * w[0])
                dg_out[pl.ds(r0, RC), cols] = dgp.astype(BF16)
            for k in range(KS):
                dw_rows[k].append(jnp.sum(acc[k], axis=0, keepdims=True))
        rows = [jnp.concatenate(r, axis=1) if len(r) > 1 else r[0] for r in dw_rows]
        rows.append(jnp.zeros((KS_ROWS - KS, CT), F32))
        _acc_rows_block(dw_ref, i, rows)

    col = lambda j: j
    main = pl.BlockSpec((R, CT), lambda j, i: (i, j))
    prev8 = pl.BlockSpec((H, CT), _swap(_prev_halo(R, H, CT, col)))
    next8 = pl.BlockSpec((H, CT), _swap(_next_halo(R, H, T, col)))
    next16 = pl.BlockSpec((HB, CT), _swap(_next_halo(R, HB, T, col)))
    wspec = pl.BlockSpec((KS_ROWS, CT), lambda j, i: (0, j))
    return pl.pallas_call(
        body,
        name=name,
        out_shape=(jax.ShapeDtypeStruct((T, F), BF16), jax.ShapeDtypeStruct((T, F), BF16), jax.ShapeDtypeStruct((KS_ROWS, F), F32)),
        grid=(F // CT, n_t),
        in_specs=[main, prev8, next8, main, next8, main, next16, wspec],
        out_specs=(main, main, wspec),
        scratch_shapes=[pltpu.VMEM((R + 2 * H, CT), F32), pltpu.VMEM((R + H, CT), F32)],
        compiler_params=_params("parallel", "arbitrary"),
    )(gp, gp, gp, up, up, df, df, wf)


def _acc_rows_block(ref, i, rows):
    *singles, pad = rows

    @pl.when(i == 0)
    def _():
        for k, row in enumerate(singles):
            ref[pl.ds(k, 1), :] = row
        ref[pl.ds(len(singles), pad.shape[0]), :] = pad

    @pl.when(i > 0)
    def _():
        for k, row in enumerate(singles):
            ref[pl.ds(k, 1), :] += row


def _group_stats(x):
    mu = jnp.mean(x, axis=-1, keepdims=True)
    xc = x - mu
    var = jnp.mean(xc * xc, axis=-1, keepdims=True)
    return xc, lax.rsqrt(var + EPS)


def _shifted_rows(rolled, x, cols):
    n = x.shape[0]
    for b in range(1, 8):
        rolled[b - 1, :, cols] = pltpu.roll(x, n - b, axis=0)


def _window(src, rolled, off, rows, cols):
    b = off % 8
    if b == 0:
        return src[pl.ds(off, rows), cols]
    return rolled[b - 1, pl.ds(off - b, rows), cols]


def _mix_a_fwd(proj, wa, ba, lg, lb, *, cw, name, deps=()):
    T = proj.shape[0]
    R = _pick(T, 128, HALO_A)
    CT = _pick(cw, 256)
    nc = cw // CT
    H = HALO_A

    def body(av_ref, ag_ref, avh_ref, agh_ref, w_ref, b_ref, lg_ref, lb_ref, *rest):
        u3_ref, u1_ref, ext, rolled = rest[len(deps):]
        i = pl.program_id(0)
        for gi in range(CT // GROUP):
            cols = pl.ds(gi * GROUP, GROUP)
            ext[pl.ds(0, H), cols] = jnp.where(i > 0, avh_ref[:, cols] * _sigmoid(agh_ref[:, cols]), 0.0)
            ext[pl.ds(H, R), cols] = av_ref[:, cols] * _sigmoid(ag_ref[:, cols])
            _shifted_rows(rolled, ext[:, cols], cols)
            acc = _window(ext, rolled, H - (KA - 1), R, cols) * w_ref[pl.ds(0, 1), cols]
            for k in range(1, KA):
                acc = acc + _window(ext, rolled, H - (KA - 1) + k, R, cols) * w_ref[pl.ds(k, 1), cols]
            u1 = acc + b_ref[:, cols]
            u1_ref[:, cols] = u1
            xc, rstd = _group_stats(u1)
            u2 = (xc * rstd) * lg_ref[:, cols] + lb_ref[:, cols]
            u3_ref[:, cols] = (u2 * _sigmoid(u2)).astype(BF16)

    main_v = pl.BlockSpec((R, CT), lambda i, j: (i, j))
    main_g = pl.BlockSpec((R, CT), lambda i, j: (i, j + nc))
    halo_v = pl.BlockSpec((H, CT), _prev_halo(R, H, CT, lambda j: j))
    halo_g = pl.BlockSpec((H, CT), _prev_halo(R, H, CT, lambda j: j + nc))
    vec = pl.BlockSpec((1, CT), lambda i, j: (0, j))
    out = pl.BlockSpec((R, CT), lambda i, j: (i, j))
    return pl.pallas_call(
        body,
        name=name,
        out_shape=(jax.ShapeDtypeStruct((T, cw), BF16), jax.ShapeDtypeStruct((T, cw), F32)),
        grid=(T // R, nc),
        in_specs=[main_v, main_g, halo_v, halo_g, pl.BlockSpec((KA_ROWS, CT), lambda i, j: (0, j)), vec, vec, vec]
        + [ANY] * len(deps),
        out_specs=(out, out),
        scratch_shapes=[pltpu.VMEM((R + H, CT), F32), pltpu.VMEM((7, R + H, CT), F32)],
        compiler_params=_params("parallel", "parallel"),
    )(proj, proj, proj, proj, wa, ba, lg, lb, *deps)


def _mix_a_bwd(proj, u1, dmix, wa, lg, lb, *, cw, name):
    T = proj.shape[0]
    R = _pick(T, 128, HALO_A)
    CT = _pick(cw, 256)
    nc = cw // CT
    H = HALO_A
    n_t = T // R
    NG = CT // GROUP

    def body(av_ref, ag_ref, avh_ref, agh_ref, u1_ref, u1n_ref, d3_ref, d3n_ref, w_ref, lg_ref, lb_ref,
             dav_ref, dag_ref, dw_ref, db_ref, dlg_ref, dlb_ref, ext, d1, ext_rolled, d1_rolled):
        i = pl.program_id(1)
        last = i == n_t - 1

        def ln_bwd(u1, d3, cols):
            xc, rstd = _group_stats(u1)
            xh = xc * rstd
            g = lg_ref[:, cols]
            u2 = xh * g + lb_ref[:, cols]
            sg = _sigmoid(u2)
            du2 = d3 * (sg * (1.0 + u2 * (1.0 - sg)))
            dxh = du2 * g
            du1 = rstd * (dxh - jnp.mean(dxh, axis=-1, keepdims=True) - xh * jnp.mean(dxh * xh, axis=-1, keepdims=True))
            return du1, du2 * xh, du2

        def lanes(parts):
            return jnp.concatenate(parts, axis=1) if len(parts) > 1 else parts[0]

        dlg_parts, dlb_parts, db_parts = [], [], []
        dw_parts = [[] for _ in range(KA)]
        for gi in range(NG):
            cols = pl.ds(gi * GROUP, GROUP)
            du1, dlg, dlb = ln_bwd(u1_ref[:, cols], d3_ref[:, cols], cols)
            d1[pl.ds(0, R), cols] = du1
            dlg_parts.append(jnp.sum(dlg, axis=0, keepdims=True))
            dlb_parts.append(jnp.sum(dlb, axis=0, keepdims=True))
            db_parts.append(jnp.sum(du1, axis=0, keepdims=True))
            du1n, _, _ = ln_bwd(u1n_ref[:, cols], jnp.where(last, 0.0, d3n_ref[:, cols]), cols)
            d1[pl.ds(R, H), cols] = jnp.where(last, 0.0, du1n)

            _shifted_rows(d1_rolled, d1[:, cols], cols)
            du0 = _window(d1, d1_rolled, KA - 1, R, cols) * w_ref[pl.ds(0, 1), cols]
            for k in range(1, KA):
                du0 = du0 + _window(d1, d1_rolled, KA - 1 - k, R, cols) * w_ref[pl.ds(k, 1), cols]
            av = av_ref[:, cols]
            sg = _sigmoid(ag_ref[:, cols])
            dav_ref[:, cols] = (du0 * sg).astype(BF16)
            dag_ref[:, cols] = (du0 * av * sg * (1.0 - sg)).astype(BF16)

            ext[pl.ds(0, H), cols] = jnp.where(i > 0, avh_ref[:, cols] * _sigmoid(agh_ref[:, cols]), 0.0)
            ext[pl.ds(H, R), cols] = av * sg
            _shifted_rows(ext_rolled, ext[:, cols], cols)
            d1_main = d1[pl.ds(0, R), cols]
            for k in range(KA):
                window = _window(ext, ext_rolled, H - (KA - 1) + k, R, cols)
                dw_parts[k].append(jnp.sum(d1_main * window, axis=0, keepdims=True))

        dlg_row, dlb_row, db_row = lanes(dlg_parts), lanes(dlb_parts), lanes(db_parts)

        @pl.when(i == 0)
        def _():
            dlg_ref[...] = dlg_row
            dlb_ref[...] = dlb_row
            db_ref[...] = db_row

        @pl.when(i > 0)
        def _():
            dlg_ref[...] += dlg_row
            dlb_ref[...] += dlb_row
            db_ref[...] += db_row

        rows = [lanes(p) for p in dw_parts]
        rows.append(jnp.zeros((KA_ROWS - KA, CT), F32))
        _acc_rows_block(dw_ref, i, rows)

    cv = lambda j: j
    cg = lambda j: j + nc
    main_v = pl.BlockSpec((R, CT), lambda j, i: (i, j))
    main_g = pl.BlockSpec((R, CT), lambda j, i: (i, j + nc))
    prev_v = pl.BlockSpec((H, CT), _swap(_prev_halo(R, H, CT, cv)))
    prev_g = pl.BlockSpec((H, CT), _swap(_prev_halo(R, H, CT, cg)))
    nxt = pl.BlockSpec((H, CT), _swap(_next_halo(R, H, T, cv)))
    wspec = pl.BlockSpec((KA_ROWS, CT), lambda j, i: (0, j))
    vec = pl.BlockSpec((1, CT), lambda j, i: (0, j))
    vshape = jax.ShapeDtypeStruct((1, cw), F32)
    return pl.pallas_call(
        body,
        name=name,
        out_shape=(jax.ShapeDtypeStruct((T, cw), BF16), jax.ShapeDtypeStruct((T, cw), BF16),
                   jax.ShapeDtypeStruct((KA_ROWS, cw), F32), vshape, vshape, vshape),
        grid=(nc, n_t),
        in_specs=[main_v, main_g, prev_v, prev_g, main_v, nxt, main_v, nxt, wspec, vec, vec],
        out_specs=(main_v, main_v, wspec, vec, vec, vec),
        scratch_shapes=[pltpu.VMEM((R + H, CT), F32), pltpu.VMEM((R + H, CT), F32),
                        pltpu.VMEM((7, R + H, CT), F32), pltpu.VMEM((7, R + H, CT), F32)],
        compiler_params=_params("parallel", "arbitrary"),
    )(proj, proj, proj, proj, u1, u1, dmix, dmix, wa, lg, lb)


def _mix_b_fwd(proj, wb, *, cw, sw, name):
    T = proj.shape[0]
    R = _pick(T, 256, 8)
    CT = _pick(sw, 512)
    nb, nc, nh = (2 * cw) // CT, (2 * cw + sw) // CT, (2 * cw + 2 * sw) // CT
    H = HALO_S

    def body(b_ref, c_ref, h_ref, ch_ref, hh_ref, w_ref, v_ref, ext):
        i = pl.program_id(0)
        ext[pl.ds(0, H), :] = jnp.where(i > 0, ch_ref[...] * hh_ref[...], 0.0)
        ext[pl.ds(H, R), :] = c_ref[...] * h_ref[...]
        w = w_ref[...]
        zc = ext[pl.ds(H - 2, R), :] * w[0:1, :]
        zc = zc + ext[pl.ds(H - 1, R), :] * w[1:2, :]
        zc = zc + ext[pl.ds(H, R), :] * w[2:3, :]
        v_ref[...] = (b_ref[...] * zc).astype(BF16)

    def main(off):
        return pl.BlockSpec((R, CT), lambda i, j: (i, j + off))

    def prev(off):
        return pl.BlockSpec((H, CT), _prev_halo(R, H, CT, lambda j: j + off))

    return pl.pallas_call(
        body,
        name=name,
        out_shape=jax.ShapeDtypeStruct((T, sw), BF16),
        grid=(T // R, sw // CT),
        in_specs=[main(nb), main(nc), main(nh), prev(nc), prev(nh), pl.BlockSpec((KS_ROWS, CT), lambda i, j: (0, j))],
        out_specs=pl.BlockSpec((R, CT), lambda i, j: (i, j)),
        scratch_shapes=[pltpu.VMEM((R + H, CT), F32)],
        compiler_params=_params("parallel", "parallel"),
    )(proj, proj, proj, proj, proj, wb)


def _mix_b_bwd(proj, dmix, wb, *, cw, sw, name):
    T = proj.shape[0]
    R = _pick(T, 256, 8)
    CT = _pick(sw, 512)
    nb, nc, nh = (2 * cw) // CT, (2 * cw + sw) // CT, (2 * cw + 2 * sw) // CT
    nd = cw // CT
    H = HALO_S
    n_t = T // R

    def body(b_ref, bn_ref, c_ref, cp_ref, cn_ref, h_ref, hp_ref, hn_ref, d_ref, dn_ref, w_ref,
             db_ref, dc_ref, dhh_ref, dw_ref, ext, dze):
        i = pl.program_id(1)
        last = i == n_t - 1
        cc, hh = c_ref[...], h_ref[...]
        ext[pl.ds(0, H), :] = jnp.where(i > 0, cp_ref[...] * hp_ref[...], 0.0)
        ext[pl.ds(H, R), :] = cc * hh
        ext[pl.ds(H + R, H), :] = cn_ref[...] * hn_ref[...]
        w = w_ref[...]
        w0, w1, w2 = w[0:1, :], w[1:2, :], w[2:3, :]

        def conv(start, n):
            z = ext[pl.ds(start + H - 2, n), :] * w0
            z = z + ext[pl.ds(start + H - 1, n), :] * w1
            return z + ext[pl.ds(start + H, n), :] * w2

        d_main = d_ref[...]
        db_ref[...] = (d_main * conv(0, R)).astype(BF16)
        dzc_main = d_main * b_ref[...]
        dze[pl.ds(0, R), :] = dzc_main
        dze[pl.ds(R, H), :] = jnp.where(last, 0.0, dn_ref[...] * bn_ref[...])
        dz = dze[pl.ds(0, R), :] * w2 + dze[pl.ds(1, R), :] * w1 + dze[pl.ds(2, R), :] * w0
        dc_ref[...] = (dz * hh).astype(BF16)
        dhh_ref[...] = (dz * cc).astype(BF16)
        rows = [jnp.sum(dzc_main * ext[pl.ds(H - 2 + k, R), :], axis=0, keepdims=True) for k in range(KS)]
        rows.append(jnp.zeros((KS_ROWS - KS, CT), F32))
        _acc_rows_block(dw_ref, i, rows)

    def main(off):
        return pl.BlockSpec((R, CT), lambda j, i: (i, j + off))

    def prev(off):
        return pl.BlockSpec((H, CT), _swap(_prev_halo(R, H, CT, lambda j: j + off)))

    def nxt(off):
        return pl.BlockSpec((H, CT), _swap(_next_halo(R, H, T, lambda j: j + off)))

    out = pl.BlockSpec((R, CT), lambda j, i: (i, j))
    wspec = pl.BlockSpec((KS_ROWS, CT), lambda j, i: (0, j))
    act = jax.ShapeDtypeStruct((T, sw), BF16)
    return pl.pallas_call(
        body,
        name=name,
        out_shape=(act, act, act, jax.ShapeDtypeStruct((KS_ROWS, sw), F32)),
        grid=(sw // CT, n_t),
        in_specs=[main(nb), nxt(nb), main(nc), prev(nc), nxt(nc), main(nh), prev(nh), nxt(nh), main(nd), nxt(nd), wspec],
        out_specs=(out, out, out, wspec),
        scratch_shapes=[pltpu.VMEM((R + 2 * H, CT), F32), pltpu.VMEM((R + H, CT), F32)],
        compiler_params=_params("parallel", "arbitrary"),
    )(proj, proj, proj, proj, proj, proj, proj, proj, dmix, dmix, wb)


def _adamw(w, g, m, v, *, name, emit_grad=False):
    Rr, Cc = w.shape
    R = _pick(Rr, max(8, ADAMW_BLOCK_BYTES // (4 * Cc)), 8)

    def body(w_ref, g_ref, m_ref, v_ref, d_ref, mo_ref, vo_ref, *g_out):
        g = g_ref[...]
        m2 = ADAM_B1 * m_ref[...] + (1.0 - ADAM_B1) * g
        v2 = ADAM_B2 * v_ref[...] + (1.0 - ADAM_B2) * (g * g)
        m_hat = m2 / (1.0 - ADAM_B1 ** ADAM_STEP)
        v_hat = v2 / (1.0 - ADAM_B2 ** ADAM_STEP)
        d_ref[...] = -ADAM_LR * (m_hat / (jnp.sqrt(v_hat) + ADAM_EPS) + ADAM_WD * w_ref[...])
        mo_ref[...] = m2
        vo_ref[...] = v2
        if emit_grad:
            g_out[0][...] = g

    spec = pl.BlockSpec((R, Cc), lambda i: (i, 0))
    shp = jax.ShapeDtypeStruct((Rr, Cc), F32)
    n_out = 4 if emit_grad else 3
    return pl.pallas_call(
        body, name=name, out_shape=(shp,) * n_out, grid=(Rr // R,), in_specs=[spec] * 4, out_specs=(spec,) * n_out,
        compiler_params=_params("parallel"),
    )(w, g, m, v)


def _sum_chips(cs, rb, pc, lay, *, name):
    _, Rr, Cc = rb.shape
    tr, tc = _pick(Rr, 512, 16), _pick(Cc, 2048)
    nr, ncol = Rr // tr, Cc // tc
    if lay.axis == 0:
        own_map = lambda i, j, s: (i + s[0] * nr, j)
        out_map = lambda i, j, s: (i, j + s[1] * ncol)
    else:
        own_map = lambda i, j, s: (i, j + s[0] * ncol)
        out_map = lambda i, j, s: (i + s[1] * nr, j)

    def body(s_ref, own_ref, rb_ref, o_ref):
        acc = own_ref[...].astype(F32)
        for j in range(N_CHIPS - 1):
            acc = acc + rb_ref[j].astype(F32)
        o_ref[...] = acc

    return pl.pallas_call(
        body,
        name=name,
        out_shape=jax.ShapeDtypeStruct(lay.shard_shape(), F32),
        grid_spec=pltpu.PrefetchScalarGridSpec(
            num_scalar_prefetch=1,
            grid=(nr, ncol),
            in_specs=[pl.BlockSpec((tr, tc), own_map), pl.BlockSpec((N_CHIPS - 1, tr, tc), lambda i, j, s: (0, i, j))],
            out_specs=pl.BlockSpec((tr, tc), out_map)),
        compiler_params=_params("parallel", "parallel"),
    )(pc, cs, rb)


def _place():
    x, y, c = lax.axis_index("x"), lax.axis_index("y"), lax.axis_index("c")
    return x, y, c, 2 * x + y


def _other_chips(x, y):
    return [(1 - x, y, 2 * (1 - x) + y), (x, 1 - y, 2 * x + (1 - y)), (1 - x, 1 - y, 2 * (1 - x) + (1 - y))]


def _allgather_small(buf, *, name, reduce):
    S = buf.shape[0]

    def body(x_ref, o_ref, gat, send_sems, recv_sems):
        x, y, c, _ = _place()
        me = 4 * x + 2 * y + c
        gat[me] = x_ref[...]
        copies = []
        for k in range(1, N_DEV):
            fx, fy, fc = (k >> 2) & 1, (k >> 1) & 1, k & 1
            px = 1 - x if fx else x
            py = 1 - y if fy else y
            pc = 1 - c if fc else c
            peer = 4 * px + 2 * py + pc
            send = pltpu.make_async_remote_copy(
                src_ref=x_ref, dst_ref=gat.at[me], send_sem=send_sems.at[k - 1], recv_sem=recv_sems.at[k - 1],
                device_id=(px, py, pc), device_id_type=MESH)
            send.start()
            arrival = pltpu.make_async_remote_copy(
                src_ref=x_ref, dst_ref=gat.at[peer], send_sem=send_sems.at[k - 1], recv_sem=recv_sems.at[k - 1],
                device_id=(px, py, pc), device_id_type=MESH)
            copies.append((send, arrival))
        for send, arrival in copies:
            arrival.wait_recv()
        for send, arrival in copies:
            send.wait_send()
        if reduce:
            acc = gat[0]
            for d in range(1, N_DEV):
                acc = acc + gat[d]
            o_ref[...] = acc
        else:
            o_ref[...] = gat[...]

    out_shape = jax.ShapeDtypeStruct((S, LANES) if reduce else (N_DEV, S, LANES), F32)
    return pl.pallas_call(
        body,
        name=name,
        out_shape=out_shape,
        in_specs=[pl.BlockSpec(memory_space=pltpu.VMEM)],
        out_specs=pl.BlockSpec(memory_space=pltpu.VMEM),
        scratch_shapes=[pltpu.VMEM((N_DEV, S, LANES), F32), pltpu.SemaphoreType.DMA((N_DEV - 1,)),
                        pltpu.SemaphoreType.DMA((N_DEV - 1,))],
        compiler_params=pltpu.CompilerParams(vmem_limit_bytes=VMEM_LIMIT),
    )(buf)


class _Sharded:
    def __init__(self, shape, axis):
        self.shape = shape
        self.axis = axis
        self.block = shape[axis] // N_CHIPS
        self.half = shape[1 - axis] // 2

    def _sl(self, along, across):
        return (along, across) if self.axis == 0 else (across, along)

    def block_slice(self, q):
        return self._sl(pl.ds(q * self.block, self.block), pl.ds(0, self.shape[1 - self.axis]))

    def block_half_slice(self, q, c):
        return self._sl(pl.ds(q * self.block, self.block), pl.ds(c * self.half, self.half))

    def shard_half_slice(self, c):
        return self._sl(pl.ds(0, self.block), pl.ds(c * self.half, self.half))

    def shard_shape(self):
        return self._sl(self.block, self.shape[1 - self.axis])

    def half_shape(self):
        return self._sl(self.shape[self.axis], self.half)

    def block_half_shape(self):
        return self._sl(self.block, self.half)

    def block_in_half_slice(self, q):
        return self._sl(pl.ds(q * self.block, self.block), pl.ds(0, self.half))


def _at(ref, sl):
    return ref.at[sl[0], sl[1]]


class _Copy:
    def __init__(self, src, dst, arrive, dev):
        self.src, self.dst, self.arrive, self.dev = src, dst, arrive, dev


HBM = pl.BlockSpec(memory_space=pltpu.HBM)
SEM = pl.BlockSpec(memory_space=pltpu.SEMAPHORE)
EFFECT = pltpu.SideEffectType.DATAFLOW_SIDE_EFFECTING


def _exchange_start(srcs, land_shapes, plan, after, *, name):
    ns, nl = len(srcs), len(land_shapes)
    n_copies = len(plan([None] * ns, [None] * nl, dry=True))

    def body(*refs):
        src_refs, land_refs = refs[:ns], refs[ns:ns + nl]
        send_sems, recv_sems = refs[ns + nl + 1], refs[ns + nl + 2]
        token = refs[-1]
        for k, cp in enumerate(plan(src_refs, land_refs)):
            pltpu.make_async_remote_copy(
                src_ref=cp.src, dst_ref=cp.dst, send_sem=send_sems.at[k], recv_sem=recv_sems.at[k],
                device_id=cp.dev, device_id_type=MESH).start()
        token[...] = jnp.zeros_like(token)

    sem = pltpu.SemaphoreType.DMA((n_copies,))
    lands = [pltpu.with_memory_space_constraint(lax.empty(shp, dt), pltpu.HBM) for shp, dt in land_shapes]
    srcs = [pltpu.with_memory_space_constraint(a, pltpu.HBM) for a in srcs]
    thru = [pltpu.HBM(a.shape, a.dtype) for a in srcs + lands]
    outs = pl.pallas_call(
        body,
        name=name,
        out_shape=(sem, sem, *thru, jax.ShapeDtypeStruct((8, LANES), F32)),
        in_specs=[HBM] * (ns + nl) + [ANY],
        out_specs=(SEM, SEM, *[HBM] * (ns + nl), pl.BlockSpec(memory_space=pltpu.VMEM)),
        input_output_aliases={i: 2 + i for i in range(ns + nl)},
        compiler_params=pltpu.CompilerParams(has_side_effects=EFFECT),
    )(*srcs, *lands, after)
    return outs[0], outs[1], list(outs[2:2 + ns]), list(outs[2 + ns:2 + ns + nl]), outs[-1]


def _exchange_wait(send_sems, recv_sems, srcs, lands, plan, after, *, name):
    ns, nl = len(srcs), len(lands)
    after = tuple(after) if isinstance(after, (tuple, list)) else (after,)

    def body(*refs):
        src_refs, land_refs = refs[:ns], refs[ns:ns + nl]
        send_sems, recv_sems = refs[ns + nl], refs[ns + nl + 1]
        copies = [
            pltpu.make_async_remote_copy(
                src_ref=cp.src, dst_ref=cp.arrive, send_sem=send_sems.at[k], recv_sem=recv_sems.at[k],
                device_id=cp.dev, device_id_type=MESH)
            for k, cp in enumerate(plan(src_refs, land_refs))
        ]
        for cp in copies:
            cp.wait_recv()
        for cp in copies:
            cp.wait_send()

    thru = [pltpu.HBM(a.shape, a.dtype) for a in list(srcs) + list(lands)]
    outs = pl.pallas_call(
        body,
        name=name,
        out_shape=tuple(thru),
        in_specs=[HBM] * (ns + nl) + [SEM, SEM] + [ANY] * len(after),
        out_specs=tuple([HBM] * (ns + nl)),
        input_output_aliases={i: i for i in range(ns + nl)},
        compiler_params=pltpu.CompilerParams(has_side_effects=EFFECT),
    )(*srcs, *lands, send_sems, recv_sems, *after)
    return list(outs[:ns]), list(outs[ns:])


def _gather_plan(layouts):
    def plan(srcs, lands, dry=False):
        if dry:
            return [None] * (4 * len(layouts))
        x, y, c, p = _place()
        copies = []
        for s, g, lay in zip(srcs, lands, layouts):
            own = _at(g, lay.block_slice(p))
            copies.append(_Copy(s, own, own, (x, y, 1 - c)))
            for qx, qy, q in _other_chips(x, y):
                copies.append(_Copy(_at(s, lay.shard_half_slice(c)), _at(g, lay.block_half_slice(p, c)),
                                    _at(g, lay.block_half_slice(q, c)), (qx, qy, c)))
        return copies

    return plan


def _forward_plan(layouts):
    def plan(srcs, lands, dry=False):
        if dry:
            return [None] * (3 * len(layouts))
        x, y, c, _ = _place()
        copies = []
        for g, lay in zip(srcs, layouts):
            for qx, qy, q in _other_chips(x, y):
                landed = _at(g, lay.block_half_slice(q, c))
                copies.append(_Copy(landed, landed, _at(g, lay.block_half_slice(q, 1 - c)), (x, y, 1 - c)))
        return copies

    return plan


def _sibling_plan(n):
    def plan(srcs, lands, dry=False):
        if dry:
            return [None] * n
        x, y, c, _ = _place()
        return [_Copy(s, ra, ra, (x, y, 1 - c)) for s, ra in zip(srcs, lands)]

    return plan


def _chips_plan(layouts):
    def plan(srcs, lands, dry=False):
        if dry:
            return [None] * (3 * len(layouts))
        x, y, c, _ = _place()
        copies = []
        for s, rb, lay in zip(srcs, lands, layouts):
            for j, (qx, qy, q) in enumerate(_other_chips(x, y)):
                copies.append(_Copy(_at(s, lay.block_in_half_slice(q)), rb.at[j], rb.at[j], (qx, qy, c)))
        return copies

    return plan


def _join_plan(layouts):
    def plan(srcs, lands, dry=False):
        if dry:
            return [None] * len(layouts)
        x, y, c, _ = _place()
        copies = []
        for g, lay in zip(srcs, layouts):
            mine = _at(g, lay.shard_half_slice(c))
            copies.append(_Copy(mine, mine, _at(g, lay.shard_half_slice(1 - c)), (x, y, 1 - c)))
        return copies

    return plan


def _pack(arrays):
    flat = [a.reshape(-1) for a in arrays]
    sizes = [f.shape[0] for f in flat]
    total = sum(sizes)
    rows = _round_up(-(-total // LANES), 8)
    flat.append(jnp.zeros((rows * LANES - total,), F32))
    return jnp.concatenate(flat).reshape(rows, LANES)


def _unpack(buf, shapes):
    flat = buf.reshape(-1)
    out, pos = [], 0
    for shp in shapes:
        n = 1
        for d in shp:
            n *= d
        out.append(flat[pos:pos + n].reshape(shp))
        pos += n
    return out


def _pad_to(a, rows, cols):
    return jnp.pad(a, ((0, rows - a.shape[0]), (0, cols - a.shape[1])))


def kernel(x, mem, g_mix, w_in, conv_a_w, conv_a_b, ln_a_g, ln_a_b, conv_b_w, w_out, g_xattn, g_mem, w_q, w_k, w_v, w_o, g_ffn, w_gate, w_up, conv_f_w, w_down, g_final, loss_target, m_g_mix, m_w_in, m_conv_a_w, m_conv_a_b, m_ln_a_g, m_ln_a_b, m_conv_b_w, m_w_out, m_g_xattn, m_g_mem, m_w_q, m_w_k, m_w_v, m_w_o, m_g_ffn, m_w_gate, m_w_up, m_conv_f_w, m_w_down, m_g_final, v_g_mix, v_w_in, v_conv_a_w, v_conv_a_b, v_ln_a_g, v_ln_a_b, v_conv_b_w, v_w_out, v_g_xattn, v_g_mem, v_w_q, v_w_k, v_w_v, v_w_o, v_g_ffn, v_w_gate, v_w_up, v_conv_f_w, v_w_down, v_g_final):
    T, D = x.shape[1], x.shape[2]
    in_sh = w_in.shape[2]
    cw_sh = conv_a_w.shape[2]
    cw = N_CHIPS * cw_sh
    f_sh = w_gate.shape[2]
    fp = _round_up(f_sh, 256)
    F = N_CHIPS * fp
    rs = w_out.shape[1]
    c_idx = lax.axis_index("c")
    p_idx = 2 * lax.axis_index("x") + lax.axis_index("y")

    def t_(a):
        return jnp.swapaxes(a[0], 0, 1)

    big = {
        "w_in": (lambda: w_in[0].astype(BF16), _Sharded((D, N_CHIPS * in_sh), 1)),
        "w_out": (lambda: w_out[0].astype(BF16), _Sharded((N_CHIPS * rs, D), 0)),
        "w_q": (lambda: w_q[0].astype(BF16), _Sharded((D, D), 0)),
        "w_k": (lambda: w_k[0].astype(BF16), _Sharded((D, D), 0)),
        "w_v": (lambda: w_v[0].astype(BF16), _Sharded((D, D), 0)),
        "w_o": (lambda: w_o[0].astype(BF16), _Sharded((D, D), 0)),
        "w_gate": (lambda: _pad_to(t_(w_gate).astype(BF16), fp, D), _Sharded((F, D), 0)),
        "w_up": (lambda: _pad_to(t_(w_up).astype(BF16), fp, D), _Sharded((F, D), 0)),
        "w_down": (lambda: _pad_to(w_down[0].astype(BF16), fp, D), _Sharded((F, D), 0)),
    }
    names = list(big)
    lay = {k: big[k][1] for k in names}
    c_arr = c_idx.astype(jnp.int32).reshape(1)
    pc_arr = jnp.stack([p_idx, c_idx]).astype(jnp.int32)

    conv_shapes = [(KA_ROWS, cw_sh), (KS_ROWS, cw_sh), (KS_ROWS, fp)]
    conv_pack = _pack([_pad_to(conv_a_w[0], KA_ROWS, cw_sh), _pad_to(conv_b_w[0], KS_ROWS, cw_sh),
                       _pad_to(conv_f_w[0], KS_ROWS, fp)])
    conv_all = _allgather_small(conv_pack, name="allgather_conv", reduce=False)
    per_chip = [_unpack(conv_all[2 * q], conv_shapes) for q in range(N_CHIPS)]
    wa = jnp.concatenate([pc[0] for pc in per_chip], axis=1)
    wb = jnp.concatenate([pc[1] for pc in per_chip], axis=1)
    wf = jnp.concatenate([pc[2] for pc in per_chip], axis=1)

    tok = conv_all

    in_half = _Sharded((D // 2, N_CHIPS * in_sh), 1)
    sources = {k: big[k] for k in names if k != "w_in"}
    sources["w_in_lo"] = (lambda: w_in[0, :D // 2].astype(BF16), in_half)
    sources["w_in_hi"] = (lambda: w_in[0, D // 2:].astype(BF16), in_half)
    gather_groups = {"in_lo": ["w_in_lo"], "in_hi": ["w_in_hi"], "out": ["w_out"], "qkv": ["w_q", "w_k", "w_v"],
                     "o": ["w_o"], "gate": ["w_gate"], "up": ["w_up"], "down": ["w_down"]}
    gathers = {}
    W = {}
    relays = {}

    def shards(tag):
        return [sources[k][0]() for k in gather_groups[tag]]

    def gather_start(tag, srcs, after):
        grp = gather_groups[tag]
        lays = [sources[k][1] for k in grp]
        plan = _gather_plan(lays)
        ssem, rsem, srcs, lands, token = _exchange_start(
            srcs, [(l.shape, BF16) for l in lays], plan, after, name=f"gather_start_{tag}")
        gathers[tag] = (ssem, rsem, srcs, lands, plan, lays, grp)
        return token

    def relay(tag, after, also=()):
        ssem, rsem, srcs, lands, plan, lays, grp = gathers[tag]
        _, lands = _exchange_wait(ssem, rsem, srcs, lands, plan, (after, *also), name=f"gather_wait_{tag}")
        plan = _forward_plan(lays)
        ssem, rsem, lands, _, token = _exchange_start(lands, [], plan, after, name=f"gather_forward_start_{tag}")
        relays[tag] = (ssem, rsem, lands, plan, grp)
        return token

    def gathered(tag, after):
        if tag not in relays:
            relay(tag, after)
        ssem, rsem, lands, plan, grp = relays[tag]
        lands, _ = _exchange_wait(ssem, rsem, lands, [], plan, after, name=f"gather_forward_wait_{tag}")
        W.update(zip(grp, lands))

    tok = gather_start("in_lo", shards("in_lo"), tok)
    tok = gather_start("in_hi", shards("in_hi"), tok)
    later = {tag: shards(tag) for tag in gather_groups if tag not in gathers}
    tok = relay("in_lo", tok, also=[a for srcs in later.values() for a in srcs])
    for tag, srcs in later.items():
        tok = gather_start(tag, srcs, tok)

    xs, tgt, mems = x[0], loss_target[0], mem[0]
    g_mem2, g_final2 = g_mem[None, :], g_final[None, :]

    memn, rm = _norm_fwd(mems, g_mem2, name="norm_mem")
    xn1, r1 = _norm_fwd(xs, g_mix, name="norm_mix")
    gathered("in_lo", (xn1, tok))
    proj = _mm(xn1, W["w_in_lo"], a_k=(0, D // 2), name="mm_in_lo")
    gathered("in_hi", proj)
    proj = _mm(xn1, W["w_in_hi"], a_k=(D // 2, D // 2), add=proj, name="mm_in_hi")
    ftok = relay("out", proj)
    u3, u1 = _mix_a_fwd(proj, wa, conv_a_b, ln_a_g, ln_a_b, cw=cw, deps=[ftok], name="mix_a_fwd")
    vb = _mix_b_fwd(proj, wb, cw=cw, sw=cw, name="mix_b_fwd")
    mix = jnp.concatenate([u3, vb], axis=1)
    gathered("out", mix)
    ftok = relay("qkv", mix)
    h1 = _mm(mix, W["w_out"], add=xs, deps=[ftok], name="mm_out")
    xn2, r2 = _norm_fwd(h1, g_xattn, name="norm_xattn")
    gathered("qkv", xn2)
    ftok = relay("o", xn2)
    q = _mm(xn2, W["w_q"], out_dtype=BF16, deps=[ftok], name="mm_q")
    k = _mm(memn, W["w_k"], out_dtype=BF16, name="mm_k")
    vm = _mm(memn, W["w_v"], out_dtype=BF16, name="mm_v")
    o = _attn_fwd(q, k, vm, name="attn_fwd")
    gathered("o", o)
    h2 = _mm(o, W["w_o"], add=h1, name="mm_o")
    relay("gate", h2)
    xn3, r3 = _norm_fwd(h2, g_ffn, name="norm_ffn")
    gathered("gate", xn3)
    gp = _mm(xn3, W["w_gate"], tb=True, name="mm_gate")
    gathered("up", gp)
    up = _mm(xn3, W["w_up"], tb=True, name="mm_up")
    ftok = relay("down", up)
    f = _ffn_act_fwd(gp, up, wf, deps=[ftok], name="ffn_act_fwd")
    gathered("down", f)
    def mm_over_ffn(a, b, *, name, add=None, deps=()):
        lo = _mm(a, b, a_k=(0, F // 2), b_k=(0, F // 2), add=add, deps=deps, name=name + "_lo")
        return _mm(a, b, a_k=(F // 2, F // 2), b_k=(F // 2, F // 2), add=lo, name=name + "_hi")

    h3 = mm_over_ffn(f, W["w_down"], add=h2, name="mm_down")

    G = {}
    other_arr = 1 - c_arr

    def grad_half(k, which, **kw):
        xk, dyk = G[k]
        return _mm(xk, dyk, ta=True, half=("n" if lay[k].axis == 0 else "m", which), **kw)

    def siblings_start(tag, grp, after):
        lays = [lay[k] for k in grp]
        firsts = [grad_half(k, other_arr, deps=[after], name=f"mm_dw_{k}_sibling") for k in grp]
        plan = _sibling_plan(len(grp))
        ssem, rsem, srcs, lands, token = _exchange_start(
            firsts, [(l.half_shape(), F32) for l in lays], plan, after, name=f"rs_siblings_start_{tag}")
        return (tag, grp, lays, plan, ssem, rsem, srcs, lands), token

    def chips_start(state, after):
        tag, grp, lays, plan, ssem, rsem, srcs, lands = state
        _, lands = _exchange_wait(ssem, rsem, srcs, lands, plan, after, name=f"rs_siblings_wait_{tag}")
        sums = [grad_half(k, c_arr, add=ra, out_dtype=BF16, name=f"mm_dw_{k}_own") for k, ra in zip(grp, lands)]
        plan = _chips_plan(lays)
        ssem, rsem, sums, lands, token = _exchange_start(
            sums, [((N_CHIPS - 1,) + l.block_half_shape(), BF16) for l in lays], plan, after, name=f"rs_chips_start_{tag}")
        return (tag, grp, lays, plan, ssem, rsem, sums, lands), token

    def join_start(state, after):
        tag, grp, lays, plan, ssem, rsem, sums, lands = state
        sums, lands = _exchange_wait(ssem, rsem, sums, lands, plan, after, name=f"rs_chips_wait_{tag}")
        halves = [_sum_chips(cs, rb, pc_arr, l, name=f"sum_chips_{k}") for k, cs, rb, l in zip(grp, sums, lands, lays)]
        plan = _join_plan(lays)
        ssem, rsem, halves, _, token = _exchange_start(halves, [], plan, after[0], name=f"rs_join_start_{tag}")
        return (tag, grp, plan, ssem, rsem, halves), token

    def join_finish(state, after):
        tag, grp, plan, ssem, rsem, halves = state
        halves, _ = _exchange_wait(ssem, rsem, halves, [], plan, after, name=f"rs_join_wait_{tag}")
        return dict(zip(grp, halves))

    loss_rows, dh3, dh3b, dg_final = _loss_head(h3, g_final2, tgt, name="loss_head")
    df = _mm(dh3b, W["w_down"], tb=True, out_dtype=BF16, name="mm_d_f")
    G["w_down"] = (f, dh3b)
    rs_down, tok = siblings_start("down", ["w_down"], tok)
    dgp, dup, dwf = _ffn_act_bwd(gp, up, df, wf, name="ffn_act_bwd")
    G["w_gate"], G["w_up"] = (dgp, xn3), (dup, xn3)
    rs_ffn, tok = siblings_start("ffn", ["w_gate", "w_up"], tok)
    rs_down, tok = chips_start(rs_down, tok)
    dxn3 = mm_over_ffn(dgp, W["w_gate"], deps=[tok], name="mm_dxn3_gate")
    dxn3 = mm_over_ffn(dup, W["w_up"], add=dxn3, name="mm_dxn3_up")
    dh2, dh2b, dg_ffn = _norm_bwd(h2, r3, g_ffn, dxn3, dh3, name="norm_ffn_bwd")
    rs_ffn, tok = chips_start(rs_ffn, dh2b)
    do = _mm(dh2b, W["w_o"], tb=True, out_dtype=BF16, deps=[tok], name="mm_d_o")
    G["w_o"] = (o, dh2b)
    dq, dk, dvm = _attn_bwd(q, k, vm, do, name="attn_bwd")
    dkb, dvb = dk.astype(BF16), dvm.astype(BF16)
    G["w_q"], G["w_k"], G["w_v"] = (xn2, dq), (memn, dkb), (memn, dvb)
    rs_att, tok = siblings_start("att", ["w_o", "w_q", "w_k", "w_v"], tok)
    dxn2 = _mm(dq, W["w_q"], tb=True, deps=[tok], name="mm_dxn2")
    dmemn = _mm(dkb, W["w_k"], tb=True, name="mm_dmem_k")
    dmemn = _mm(dvb, W["w_v"], tb=True, add=dmemn, name="mm_dmem_v")
    dg_mem = _norm_bwd(mems, rm, g_mem2, dmemn, None, name="norm_mem_bwd", want_dh=False)
    dh1, dh1b, dg_xattn = _norm_bwd(h1, r2, g_xattn, dxn2, dh2, name="norm_xattn_bwd")
    rs_att, tok = chips_start(rs_att, dh1b)
    dmix = _mm(dh1b, W["w_out"], tb=True, deps=[tok], name="mm_d_mix")
    G["w_out"] = (mix, dh1b)
    rs_out, tok = siblings_start("out", ["w_out"], tok)
    dav, dag, dwa, dba, dlg, dlb = _mix_a_bwd(proj, u1, dmix, wa, ln_a_g, ln_a_b, cw=cw, name="mix_a_bwd")
    dbg, dcg, dbh, dwb = _mix_b_bwd(proj, dmix, wb, cw=cw, sw=cw, name="mix_b_bwd")
    dproj = jnp.concatenate([dav, dag, dbg, dcg, dbh], axis=1)
    G["w_in"] = (xn1, dproj)
    rs_in, tok = siblings_start("in", ["w_in"], tok)
    rs_out, tok = chips_start(rs_out, tok)
    dxn1 = _mm(dproj, W["w_in_lo"], tb=True, out_cols=(0, D), deps=[tok], name="mm_dxn1_lo")
    dxn1 = _mm(dproj, W["w_in_hi"], tb=True, out_cols=(D // 2, D), into=dxn1, name="mm_dxn1_hi")
    dx, _, dg_mix = _norm_bwd(xs, r1, g_mix, dxn1, dh1, name="norm_mix_bwd")
    rs_in, tok = chips_start(rs_in, dx)

    loss_part = jnp.sum(loss_rows).reshape(1, 1)
    small_parts = [dg_mix, dba, dlg, dlb, dg_xattn, dg_mem, dg_ffn, dg_final, dwa, dwb, dwf, loss_part]
    small_shapes = [a.shape for a in small_parts]
    reduced = _allgather_small(_pack(small_parts), name="allreduce_small", reduce=True)
    (sg_mix, sba, slg, slb, sg_xattn, sg_mem, sg_ffn, sg_final, swa, swb, swf, loss_sum) = _unpack(reduced, small_shapes)
    loss = loss_sum.reshape(())
    ga_w = lax.dynamic_slice(swa, (0, p_idx * cw_sh), (KA, cw_sh))
    gb_w = lax.dynamic_slice(swb, (0, p_idx * cw_sh), (KS, cw_sh))
    gf_w = lax.dynamic_slice(swf, (0, p_idx * fp), (KS, f_sh))

    weights = dict(g_mix=g_mix, w_in=w_in, conv_a_w=conv_a_w, conv_a_b=conv_a_b, ln_a_g=ln_a_g, ln_a_b=ln_a_b,
                   conv_b_w=conv_b_w, w_out=w_out, g_xattn=g_xattn, g_mem=g_mem, w_q=w_q, w_k=w_k, w_v=w_v, w_o=w_o,
                   g_ffn=g_ffn, w_gate=w_gate, w_up=w_up, conv_f_w=conv_f_w, w_down=w_down, g_final=g_final)
    m_in = dict(g_mix=m_g_mix, w_in=m_w_in, conv_a_w=m_conv_a_w, conv_a_b=m_conv_a_b, ln_a_g=m_ln_a_g, ln_a_b=m_ln_a_b,
                conv_b_w=m_conv_b_w, w_out=m_w_out, g_xattn=m_g_xattn, g_mem=m_g_mem, w_q=m_w_q, w_k=m_w_k, w_v=m_w_v,
                w_o=m_w_o, g_ffn=m_g_ffn, w_gate=m_w_gate, w_up=m_w_up, conv_f_w=m_conv_f_w, w_down=m_w_down,
                g_final=m_g_final)
    v_in = dict(g_mix=v_g_mix, w_in=v_w_in, conv_a_w=v_conv_a_w, conv_a_b=v_conv_a_b, ln_a_g=v_ln_a_g, ln_a_b=v_ln_a_b,
                conv_b_w=v_conv_b_w, w_out=v_w_out, g_xattn=v_g_xattn, g_mem=v_g_mem, w_q=v_w_q, w_k=v_w_k, w_v=v_w_v,
                w_o=v_w_o, g_ffn=v_g_ffn, w_gate=v_w_gate, w_up=v_w_up, conv_f_w=v_conv_f_w, w_down=v_w_down,
                g_final=v_g_final)
    order = list(weights)
    grads = dict(g_mix=sg_mix, conv_a_w=ga_w, conv_a_b=sba, ln_a_g=slg, ln_a_b=slb, conv_b_w=gb_w, g_xattn=sg_xattn,
                 g_mem=sg_mem, g_ffn=sg_ffn, conv_f_w=gf_w, g_final=sg_final)
    grads = {k: g.reshape(weights[k].shape) for k, g in grads.items()}

    delta, new_m, new_v = {}, {}, {}
    small = [k for k in order if k not in big]
    small_shapes = [weights[k].shape for k in small]
    packed = [_pack([src[k] for k in small]) for src in (weights, grads, m_in, v_in)]
    d_, m_, v_ = _adamw(*packed, name="adamw_small")
    for k, dd, mm_, vv in zip(small, _unpack(d_, small_shapes), _unpack(m_, small_shapes), _unpack(v_, small_shapes)):
        delta[k], new_m[k], new_v[k] = dd, mm_, vv

    transposed = ("w_gate", "w_up")

    def update(shard_grads):
        last = None
        for k, g in shard_grads.items():
            view = t_ if k in transposed else (lambda a: a[0])
            back = (lambda a: jnp.swapaxes(a, 0, 1)[None]) if k in transposed else (lambda a: a[None])
            padded = g.shape != view(weights[k]).shape
            outs = _adamw(view(weights[k]), g, view(m_in[k]), view(v_in[k]), emit_grad=padded, name=f"adamw_{k}")
            delta[k], new_m[k], new_v[k] = back(outs[0]), back(outs[1]), back(outs[2])
            grads[k] = back(outs[3] if padded else g)
            last = outs[0]
        return last

    after = (d_, tok)
    joining = None
    for state in (rs_down, rs_ffn, rs_att, rs_out, rs_in):
        started, token = join_start(state, after)
        after = (token,)
        if joining is not None:
            after = (update(join_finish(joining, token)), token)
        joining = started
    update(join_finish(joining, after[0]))

    return (loss, dx[None], *[grads[k] for k in order], *[delta[k] for k in order],
            *[new_m[k] for k in order], *[new_v[k] for k in order])
```

```python
import jax
import jax.numpy as jnp
from jax import lax
from jax.experimental import pallas as pl
from jax.experimental.pallas import tpu as pltpu

F32 = jnp.float32
BF16 = jnp.bfloat16
EPS = 1e-6
N_HEADS = 4
GROUP = 128
KA = 31
KS = 3
KA_ROWS = 32
KS_ROWS = 8
HALO_A = 32
HALO_S = 8
CHUNK_ROWS = 16
CHUNK_COLS = 256
N_CHIPS = 4
N_DEV = 8
LANES = 128
VMEM_LIMIT = 56 * 1024 * 1024
MM_WIDE_TILE_ELEMS = 4096 * 512
ADAMW_BLOCK_BYTES = 2 << 20
MESH = pl.DeviceIdType.MESH
ANY = pl.BlockSpec(memory_space=pl.ANY)

ADAM_LR = 0.001
ADAM_B1 = 0.9
ADAM_B2 = 0.999
ADAM_EPS = 1e-08
ADAM_WD = 0.01
ADAM_STEP = 10


def _pick(dim, pref, mult=LANES):
    if dim <= pref:
        return dim
    t = (pref // mult) * mult
    while t >= mult:
        if dim % t == 0:
            return t
        t -= mult
    return dim


def _round_up(n, m):
    return ((n + m - 1) // m) * m


def _params(*sem):
    return pltpu.CompilerParams(dimension_semantics=sem, vmem_limit_bytes=VMEM_LIMIT)


def _sigmoid(x):
    return jax.nn.sigmoid(x)


def _mm(a, b, *, name, ta=False, tb=False, out_dtype=F32, add=None, deps=(), half=None, a_k=None, b_k=None,
        out_cols=None, into=None, tm=1024, tn=512, tk=5632):
    if ta:
        K, M = a.shape
    else:
        M, K = a.shape
    if tb:
        N, K2 = b.shape
    else:
        K2, N = b.shape
    k0 = kb0 = 0
    if a_k is not None:
        k0, K = a_k
    if b_k is not None:
        kb0, K2 = b_k
    assert K == K2, (a.shape, b.shape)
    half_dim, half_sel = half if half is not None else (None, None)
    if half_dim == "m":
        M //= 2
    elif half_dim == "n":
        N //= 2
    tm, tk = _pick(M, tm), _pick(K, tk)
    nk = K // tk
    tn = _pick(N, 2 * tn if nk == 1 and K * tn <= MM_WIDE_TILE_ELEMS else tn)
    mi, nj = M // tm, N // tn
    n0, n_total = out_cols if out_cols is not None else (0, N)
    assert k0 % tk == 0 and kb0 % tk == 0 and n0 % tn == 0
    koff, kboff, noff = k0 // tk, kb0 // tk, n0 // tn

    def a_map(i, j, k, *s):
        i = i + s[0][0] * mi if half_dim == "m" else i
        return (k + koff, i) if ta else (i, k + koff)

    def b_map(i, j, k, *s):
        j = j + s[0][0] * nj if half_dim == "n" else j
        return (j, k + kboff) if tb else (k + kboff, j)

    a_spec = pl.BlockSpec((tk, tm) if ta else (tm, tk), a_map)
    b_spec = pl.BlockSpec((tn, tk) if tb else (tk, tn), b_map)
    o_spec = pl.BlockSpec((tm, tn), lambda i, j, k, *s: (i, j + noff))
    add_spec = pl.BlockSpec((tm, tn), lambda i, j, k, *s: (i, j))
    dims = (((0,) if ta else (1,), (1,) if tb else (0,)), ((), ()))
    has_add = add is not None
    n_pre = 0 if half is None else 1
    n_after = len(deps) + (0 if into is None else 1)

    def body(*refs):
        refs = refs[n_pre:]
        a_ref, b_ref = refs[:2]
        add_ref = refs[2] if has_add else None
        o_ref = refs[(3 if has_add else 2) + n_after]
        acc_ref = refs[-1] if nk > 1 else None
        k = pl.program_id(2)
        part = lax.dot_general(a_ref[...], b_ref[...], dims, preferred_element_type=F32)

        def finish(r):
            if add_ref is not None:
                r = add_ref[...] + r
            o_ref[...] = r.astype(out_dtype)

        if nk == 1:
            finish(part)
        else:
            @pl.when(k == 0)
            def _():
                acc_ref[...] = part

            @pl.when(jnp.logical_and(k > 0, k < nk - 1))
            def _():
                acc_ref[...] += part

            @pl.when(k == nk - 1)
            def _():
                finish(acc_ref[...] + part)

    in_specs = [a_spec, b_spec] + ([add_spec] if has_add else []) + [ANY] * n_after
    args = (a, b) + ((add,) if has_add else ()) + tuple(deps) + ((into,) if into is not None else ())
    aliases = {n_pre + len(args) - 1: 0} if into is not None else {}
    scratch = [pltpu.VMEM((tm, tn), F32)] if nk > 1 else []
    grid = (mi, nj, nk)
    if half is None:
        grid_spec = dict(grid=grid, in_specs=in_specs, out_specs=o_spec, scratch_shapes=scratch)
    else:
        args = (half_sel,) + args
        grid_spec = dict(grid_spec=pltpu.PrefetchScalarGridSpec(
            num_scalar_prefetch=1, grid=grid, in_specs=in_specs, out_specs=o_spec, scratch_shapes=scratch))
    return pl.pallas_call(
        body,
        name=name,
        out_shape=jax.ShapeDtypeStruct((M, n_total), out_dtype),
        input_output_aliases=aliases,
        compiler_params=_params("parallel", "parallel", "arbitrary"),
        **grid_spec,
    )(*args)


def _norm_fwd(h, g, *, name):
    T, D = h.shape
    R = _pick(T, 256, 8)

    def body(h_ref, g_ref, xn_ref, r_ref):
        x = h_ref[...]
        r = lax.rsqrt(jnp.mean(x * x, axis=-1, keepdims=True) + EPS)
        xn_ref[...] = ((x * r) * g_ref[...]).astype(BF16)
        r_ref[...] = r

    return pl.pallas_call(
        body,
        name=name,
        out_shape=(jax.ShapeDtypeStruct((T, D), BF16), jax.ShapeDtypeStruct((T, 1), F32)),
        grid=(T // R,),
        in_specs=[pl.BlockSpec((R, D), lambda i: (i, 0)), pl.BlockSpec((1, D), lambda i: (0, 0))],
        out_specs=(pl.BlockSpec((R, D), lambda i: (i, 0)), pl.BlockSpec((R, 1), lambda i: (i, 0))),
        compiler_params=_params("parallel"),
    )(h, g)


def _norm_bwd(h, r, g, dxn, dres, *, name, want_dh=True):
    T, D = h.shape
    R = _pick(T, 256, 8)
    has_res = dres is not None

    def body(*refs):
        h_ref, r_ref, g_ref, dxn_ref = refs[:4]
        pos = 4
        dres_ref = None
        if has_res:
            dres_ref = refs[pos]
            pos += 1
        if want_dh:
            dh_ref, dhb_ref, dg_ref = refs[pos:pos + 3]
        else:
            dg_ref = refs[pos]
        i = pl.program_id(0)
        rr = r_ref[...]
        hn = h_ref[...] * rr
        d = dxn_ref[...].astype(F32)
        gd = d * g_ref[...]
        part = jnp.sum(d * hn, axis=0, keepdims=True)

        @pl.when(i == 0)
        def _():
            dg_ref[...] = part

        @pl.when(i > 0)
        def _():
            dg_ref[...] += part

        if want_dh:
            dh = rr * (gd - hn * jnp.mean(gd * hn, axis=-1, keepdims=True))
            if dres_ref is not None:
                dh = dres_ref[...] + dh
            dh_ref[...] = dh
            dhb_ref[...] = dh.astype(BF16)

    row = pl.BlockSpec((R, D), lambda i: (i, 0))
    vec = pl.BlockSpec((1, D), lambda i: (0, 0))
    in_specs = [row, pl.BlockSpec((R, 1), lambda i: (i, 0)), vec, row] + ([row] if has_res else [])
    args = (h, r, g, dxn) + ((dres,) if has_res else ())
    if want_dh:
        out_shape = (jax.ShapeDtypeStruct((T, D), F32), jax.ShapeDtypeStruct((T, D), BF16), jax.ShapeDtypeStruct((1, D), F32))
        out_specs = (row, row, vec)
    else:
        out_shape = jax.ShapeDtypeStruct((1, D), F32)
        out_specs = vec
    return pl.pallas_call(
        body, name=name, out_shape=out_shape, grid=(T // R,), in_specs=in_specs, out_specs=out_specs,
        compiler_params=_params("arbitrary"),
    )(*args)


def _loss_head(h, g, tgt, *, name):
    T, D = h.shape
    R = _pick(T, 256, 8)

    def body(h_ref, g_ref, t_ref, loss_ref, dh_ref, dhb_ref, dg_ref):
        i = pl.program_id(0)
        x = h_ref[...]
        gg = g_ref[...]
        r = lax.rsqrt(jnp.mean(x * x, axis=-1, keepdims=True) + EPS)
        hn = x * r
        e = hn * gg - t_ref[...]
        loss_ref[...] = 0.5 * jnp.mean(e * e, axis=-1, keepdims=True)
        dy = e * (1.0 / D)
        gd = dy * gg
        dh = r * (gd - hn * jnp.mean(gd * hn, axis=-1, keepdims=True))
        dh_ref[...] = dh
        dhb_ref[...] = dh.astype(BF16)
        part = jnp.sum(dy * hn, axis=0, keepdims=True)

        @pl.when(i == 0)
        def _():
            dg_ref[...] = part

        @pl.when(i > 0)
        def _():
            dg_ref[...] += part

    row = pl.BlockSpec((R, D), lambda i: (i, 0))
    vec = pl.BlockSpec((1, D), lambda i: (0, 0))
    return pl.pallas_call(
        body,
        name=name,
        out_shape=(jax.ShapeDtypeStruct((T, 1), F32), jax.ShapeDtypeStruct((T, D), F32),
                   jax.ShapeDtypeStruct((T, D), BF16), jax.ShapeDtypeStruct((1, D), F32)),
        grid=(T // R,),
        in_specs=[row, vec, row],
        out_specs=(pl.BlockSpec((R, 1), lambda i: (i, 0)), row, row, vec),
        compiler_params=_params("arbitrary"),
    )(h, g, tgt)


_NT = (((1,), (1,)), ((), ()))
_TN = (((0,), (0,)), ((), ()))
_NN = (((1,), (0,)), ((), ()))


def _softmax_rows(s):
    m = jnp.max(s, axis=-1, keepdims=True)
    e = jnp.exp(s - m)
    return e / jnp.sum(e, axis=-1, keepdims=True)


def _attn_fwd(q, k, v, *, name):
    T, D = q.shape
    ML = k.shape[0]
    dh = D // N_HEADS
    scale = dh ** -0.5
    R = _pick(T, 512, 16)

    def body(q_ref, k_ref, v_ref, o_ref):
        s = lax.dot_general(q_ref[...], k_ref[...], _NT, preferred_element_type=F32) * scale
        p = _softmax_rows(s)
        o_ref[...] = lax.dot_general(p.astype(BF16), v_ref[...], _NN, preferred_element_type=F32).astype(BF16)

    qs = pl.BlockSpec((R, dh), lambda i, h: (i, h))
    ks = pl.BlockSpec((ML, dh), lambda i, h: (0, h))
    return pl.pallas_call(
        body, name=name, out_shape=jax.ShapeDtypeStruct((T, D), BF16), grid=(T // R, N_HEADS),
        in_specs=[qs, ks, ks], out_specs=qs, compiler_params=_params("parallel", "parallel"),
    )(q, k, v)


def _attn_bwd(q, k, v, do, *, name):
    T, D = q.shape
    ML = k.shape[0]
    dh = D // N_HEADS
    scale = dh ** -0.5
    R = _pick(T, 512, 16)

    def body(q_ref, k_ref, v_ref, do_ref, dq_ref, dk_ref, dv_ref):
        i = pl.program_id(1)
        qq, kk, vv, dd = q_ref[...], k_ref[...], v_ref[...], do_ref[...]
        s = lax.dot_general(qq, kk, _NT, preferred_element_type=F32) * scale
        p = _softmax_rows(s)
        dp = lax.dot_general(dd, vv, _NT, preferred_element_type=F32)
        dv_part = lax.dot_general(p.astype(BF16), dd, _TN, preferred_element_type=F32)
        ds = (p * (dp - jnp.sum(p * dp, axis=-1, keepdims=True)) * scale).astype(BF16)
        dq_ref[...] = lax.dot_general(ds, kk, _NN, preferred_element_type=F32).astype(BF16)
        dk_part = lax.dot_general(ds, qq, _TN, preferred_element_type=F32)

        @pl.when(i == 0)
        def _():
            dk_ref[...] = dk_part
            dv_ref[...] = dv_part

        @pl.when(i > 0)
        def _():
            dk_ref[...] += dk_part
            dv_ref[...] += dv_part

    qs = pl.BlockSpec((R, dh), lambda h, i: (i, h))
    ks = pl.BlockSpec((ML, dh), lambda h, i: (0, h))
    return pl.pallas_call(
        body,
        name=name,
        out_shape=(jax.ShapeDtypeStruct((T, D), BF16), jax.ShapeDtypeStruct((ML, D), F32), jax.ShapeDtypeStruct((ML, D), F32)),
        grid=(N_HEADS, T // R),
        in_specs=[qs, ks, ks, qs],
        out_specs=(qs, ks, ks),
        compiler_params=_params("parallel", "arbitrary"),
    )(q, k, v, do)


def _prev_halo(R, halo, CT, col):
    per = R // halo
    return lambda i, j: (jnp.maximum(i * per - 1, 0), col(j))


def _next_halo(R, halo, T, col):
    per = R // halo
    last = T // halo - 1
    return lambda i, j: (jnp.minimum((i + 1) * per, last), col(j))


def _swap(f):
    return lambda j, i: f(i, j)


def _ffn_act_fwd(gp, up, wf, *, name, deps=()):
    T, F = gp.shape
    R = _pick(T, 256, CHUNK_ROWS)
    CT = _pick(F, 1024, CHUNK_COLS)
    H = HALO_S
    RC, CC = CHUNK_ROWS, min(CHUNK_COLS, CT)

    def body(g_ref, gh_ref, u_ref, w_ref, *rest):
        f_ref, ext = rest[len(deps):]
        i = pl.program_id(0)
        ext[pl.ds(0, H), :] = jnp.where(i > 0, gh_ref[...], 0.0)
        ext[pl.ds(H, R), :] = g_ref[...]
        for c0 in range(0, CT, CC):
            cols = pl.ds(c0, CC)
            w0, w1, w2 = [jnp.broadcast_to(w_ref[pl.ds(k, 1), cols], (RC, CC)) for k in range(KS)]
            for r0 in range(0, R, RC):
                g = ext[pl.ds(r0 + H - 2, RC), cols] * w0
                g = g + ext[pl.ds(r0 + H - 1, RC), cols] * w1
                g = g + ext[pl.ds(r0 + H, RC), cols] * w2
                f_ref[pl.ds(r0, RC), cols] = (g * _sigmoid(g) * u_ref[pl.ds(r0, RC), cols]).astype(BF16)

    main = pl.BlockSpec((R, CT), lambda i, j: (i, j))
    return pl.pallas_call(
        body,
        name=name,
        out_shape=jax.ShapeDtypeStruct((T, F), BF16),
        grid=(T // R, F // CT),
        in_specs=[main, pl.BlockSpec((H, CT), _prev_halo(R, H, CT, lambda j: j)), main,
                  pl.BlockSpec((KS_ROWS, CT), lambda i, j: (0, j))] + [ANY] * len(deps),
        out_specs=main,
        scratch_shapes=[pltpu.VMEM((R + H, CT), F32)],
        compiler_params=_params("parallel", "parallel"),
    )(gp, gp, up, wf, *deps)


def _ffn_act_bwd(gp, up, df, wf, *, name):
    T, F = gp.shape
    R = _pick(T, 256, CHUNK_ROWS)
    CT = _pick(F, 512, CHUNK_COLS)
    H = HALO_S
    HB = 16
    n_t = T // R
    RC, CC = CHUNK_ROWS, min(CHUNK_COLS, CT)

    def body(g_ref, gp_ref, gn_ref, u_ref, un_ref, d_ref, dn_ref, w_ref, dg_out, du_out, dw_ref, ext, dge):
        i = pl.program_id(1)
        last = i == n_t - 1
        ext[pl.ds(0, H), :] = jnp.where(i > 0, gp_ref[...], 0.0)
        ext[pl.ds(H, R), :] = g_ref[...]
        ext[pl.ds(H + R, H), :] = gn_ref[...]

        def dact(g, u, d):
            sg = _sigmoid(g)
            return d * u * (sg * (1.0 + g * (1.0 - sg))), d * (g * sg)

        dw_rows = [[] for _ in range(KS)]
        for c0 in range(0, CT, CC):
            cols = pl.ds(c0, CC)
            w = [jnp.broadcast_to(w_ref[pl.ds(k, 1), cols], (RC, CC)) for k in range(KS)]
            acc = [jnp.zeros((RC, CC), F32) for _ in range(KS)]
            for r0 in range(0, R, RC):
                rows = pl.ds(r0, RC)
                taps = [ext[pl.ds(r0 + H - 2 + k, RC), cols] for k in range(KS)]
                g = taps[0] * w[0] + taps[1] * w[1] + taps[2] * w[2]
                dg, du = dact(g, u_ref[rows, cols], d_ref[rows, cols].astype(F32))
                du_out[rows, cols] = du.astype(BF16)
                dge[rows, cols] = dg
                acc = [a + dg * t for a, t in zip(acc, taps)]
            taps = [ext[pl.ds(R + H - 2 + k, H), cols] for k in range(KS)]
            g = taps[0] * w[0][:H] + taps[1] * w[1][:H] + taps[2] * w[2][:H]
            d_next = jnp.where(last, 0.0, dn_ref[pl.ds(0, H), cols].astype(F32))
            dg_next, _ = dact(g, un_ref[:, cols], d_next)
            dge[pl.ds(R, H), cols] = jnp.where(last, 0.0, dg_next)
            for r0 in range(0, R, RC):
                dgp = (dge[pl.ds(r0, RC), cols] * w[2] + dge[pl.ds(r0 + 1, RC), cols] * w[1]
                       + dge[pl.ds(r0 + 2, RC), cols] * w[0])
                dg_out[pl.ds(r0, RC), cols] = dgp.astype(BF16)
            for k in range(KS):
                dw_rows[k].append(jnp.sum(acc[k], axis=0, keepdims=True))
        rows = [jnp.concatenate(r, axis=1) if len(r) > 1 else r[0] for r in dw_rows]
        rows.append(jnp.zeros((KS_ROWS - KS, CT), F32))
        _acc_rows_block(dw_ref, i, rows)

    col = lambda j: j
    main = pl.BlockSpec((R, CT), lambda j, i: (i, j))
    prev8 = pl.BlockSpec((H, CT), _swap(_prev_halo(R, H, CT, col)))
    next8 = pl.BlockSpec((H, CT), _swap(_next_halo(R, H, T, col)))
    next16 = pl.BlockSpec((HB, CT), _swap(_next_halo(R, HB, T, col)))
    wspec = pl.BlockSpec((KS_ROWS, CT), lambda j, i: (0, j))
    return pl.pallas_call(
        body,
        name=name,
        out_shape=(jax.ShapeDtypeStruct((T, F), BF16), jax.ShapeDtypeStruct((T, F), BF16), jax.ShapeDtypeStruct((KS_ROWS, F), F32)),
        grid=(F // CT, n_t),
        in_specs=[main, prev8, next8, main, next8, main, next16, wspec],
        out_specs=(main, main, wspec),
        scratch_shapes=[pltpu.VMEM((R + 2 * H, CT), F32), pltpu.VMEM((R + H, CT), F32)],
        compiler_params=_params("parallel", "arbitrary"),
    )(gp, gp, gp, up, up, df, df, wf)


def _acc_rows_block(ref, i, rows):
    *singles, pad = rows

    @pl.when(i == 0)
    def _():
        for k, row in enumerate(singles):
            ref[pl.ds(k, 1), :] = row
        ref[pl.ds(len(singles), pad.shape[0]), :] = pad

    @pl.when(i > 0)
    def _():
        for k, row in enumerate(singles):
            ref[pl.ds(k, 1), :] += row


def _group_stats(x):
    mu = jnp.mean(x, axis=-1, keepdims=True)
    xc = x - mu
    var = jnp.mean(xc * xc, axis=-1, keepdims=True)
    return xc, lax.rsqrt(var + EPS)


def _shifted_rows(rolled, x, cols):
    n = x.shape[0]
    for b in range(1, 8):
        rolled[b - 1, :, cols] = pltpu.roll(x, n - b, axis=0)


def _window(src, rolled, off, rows, cols):
    b = off % 8
    if b == 0:
        return src[pl.ds(off, rows), cols]
    return rolled[b - 1, pl.ds(off - b, rows), cols]


def _mix_a_fwd(proj, wa, ba, lg, lb, *, cw, name, deps=()):
    T = proj.shape[0]
    R = _pick(T, 128, HALO_A)
    CT = _pick(cw, 256)
    nc = cw // CT
    H = HALO_A

    def body(av_ref, ag_ref, avh_ref, agh_ref, w_ref, b_ref, lg_ref, lb_ref, *rest):
        u3_ref, u1_ref, ext, rolled = rest[len(deps):]
        i = pl.program_id(0)
        for gi in range(CT // GROUP):
            cols = pl.ds(gi * GROUP, GROUP)
            ext[pl.ds(0, H), cols] = jnp.where(i > 0, avh_ref[:, cols] * _sigmoid(agh_ref[:, cols]), 0.0)
            ext[pl.ds(H, R), cols] = av_ref[:, cols] * _sigmoid(ag_ref[:, cols])
            _shifted_rows(rolled, ext[:, cols], cols)
            acc = _window(ext, rolled, H - (KA - 1), R, cols) * w_ref[pl.ds(0, 1), cols]
            for k in range(1, KA):
                acc = acc + _window(ext, rolled, H - (KA - 1) + k, R, cols) * w_ref[pl.ds(k, 1), cols]
            u1 = acc + b_ref[:, cols]
            u1_ref[:, cols] = u1
            xc, rstd = _group_stats(u1)
            u2 = (xc * rstd) * lg_ref[:, cols] + lb_ref[:, cols]
            u3_ref[:, cols] = (u2 * _sigmoid(u2)).astype(BF16)

    main_v = pl.BlockSpec((R, CT), lambda i, j: (i, j))
    main_g = pl.BlockSpec((R, CT), lambda i, j: (i, j + nc))
    halo_v = pl.BlockSpec((H, CT), _prev_halo(R, H, CT, lambda j: j))
    halo_g = pl.BlockSpec((H, CT), _prev_halo(R, H, CT, lambda j: j + nc))
    vec = pl.BlockSpec((1, CT), lambda i, j: (0, j))
    out = pl.BlockSpec((R, CT), lambda i, j: (i, j))
    return pl.pallas_call(
        body,
        name=name,
        out_shape=(jax.ShapeDtypeStruct((T, cw), BF16), jax.ShapeDtypeStruct((T, cw), F32)),
        grid=(T // R, nc),
        in_specs=[main_v, main_g, halo_v, halo_g, pl.BlockSpec((KA_ROWS, CT), lambda i, j: (0, j)), vec, vec, vec]
        + [ANY] * len(deps),
        out_specs=(out, out),
        scratch_shapes=[pltpu.VMEM((R + H, CT), F32), pltpu.VMEM((7, R + H, CT), F32)],
        compiler_params=_params("parallel", "parallel"),
    )(proj, proj, proj, proj, wa, ba, lg, lb, *deps)


def _mix_a_bwd(proj, u1, dmix, wa, lg, lb, *, cw, name):
    T = proj.shape[0]
    R = _pick(T, 128, HALO_A)
    CT = _pick(cw, 256)
    nc = cw // CT
    H = HALO_A
    n_t = T // R
    NG = CT // GROUP

    def body(av_ref, ag_ref, avh_ref, agh_ref, u1_ref, u1n_ref, d3_ref, d3n_ref, w_ref, lg_ref, lb_ref,
             dav_ref, dag_ref, dw_ref, db_ref, dlg_ref, dlb_ref, ext, d1, ext_rolled, d1_rolled):
        i = pl.program_id(1)
        last = i == n_t - 1

        def ln_bwd(u1, d3, cols):
            xc, rstd = _group_stats(u1)
            xh = xc * rstd
            g = lg_ref[:, cols]
            u2 = xh * g + lb_ref[:, cols]
            sg = _sigmoid(u2)
            du2 = d3 * (sg * (1.0 + u2 * (1.0 - sg)))
            dxh = du2 * g
            du1 = rstd * (dxh - jnp.mean(dxh, axis=-1, keepdims=True) - xh * jnp.mean(dxh * xh, axis=-1, keepdims=True))
            return du1, du2 * xh, du2

        def lanes(parts):
            return jnp.concatenate(parts, axis=1) if len(parts) > 1 else parts[0]

        dlg_parts, dlb_parts, db_parts = [], [], []
        dw_parts = [[] for _ in range(KA)]
        for gi in range(NG):
            cols = pl.ds(gi * GROUP, GROUP)
            du1, dlg, dlb = ln_bwd(u1_ref[:, cols], d3_ref[:, cols], cols)
            d1[pl.ds(0, R), cols] = du1
            dlg_parts.append(jnp.sum(dlg, axis=0, keepdims=True))
            dlb_parts.append(jnp.sum(dlb, axis=0, keepdims=True))
            db_parts.append(jnp.sum(du1, axis=0, keepdims=True))
            du1n, _, _ = ln_bwd(u1n_ref[:, cols], jnp.where(last, 0.0, d3n_ref[:, cols]), cols)
            d1[pl.ds(R, H), cols] = jnp.where(last, 0.0, du1n)

            _shifted_rows(d1_rolled, d1[:, cols], cols)
            du0 = _window(d1, d1_rolled, KA - 1, R, cols) * w_ref[pl.ds(0, 1), cols]
            for k in range(1, KA):
                du0 = du0 + _window(d1, d1_rolled, KA - 1 - k, R, cols) * w_ref[pl.ds(k, 1), cols]
            av = av_ref[:, cols]
            sg = _sigmoid(ag_ref[:, cols])
            dav_ref[:, cols] = (du0 * sg).astype(BF16)
            dag_ref[:, cols] = (du0 * av * sg * (1.0 - sg)).astype(BF16)

            ext[pl.ds(0, H), cols] = jnp.where(i > 0, avh_ref[:, cols] * _sigmoid(agh_ref[:, cols]), 0.0)
            ext[pl.ds(H, R), cols] = av * sg
            _shifted_rows(ext_rolled, ext[:, cols], cols)
            d1_main = d1[pl.ds(0, R), cols]
            for k in range(KA):
                window = _window(ext, ext_rolled, H - (KA - 1) + k, R, cols)
                dw_parts[k].append(jnp.sum(d1_main * window, axis=0, keepdims=True))

        dlg_row, dlb_row, db_row = lanes(dlg_parts), lanes(dlb_parts), lanes(db_parts)

        @pl.when(i == 0)
        def _():
            dlg_ref[...] = dlg_row
            dlb_ref[...] = dlb_row
            db_ref[...] = db_row

        @pl.when(i > 0)
        def _():
            dlg_ref[...] += dlg_row
            dlb_ref[...] += dlb_row
            db_ref[...] += db_row

        rows = [lanes(p) for p in dw_parts]
        rows.append(jnp.zeros((KA_ROWS - KA, CT), F32))
        _acc_rows_block(dw_ref, i, rows)

    cv = lambda j: j
    cg = lambda j: j + nc
    main_v = pl.BlockSpec((R, CT), lambda j, i: (i, j))
    main_g = pl.BlockSpec((R, CT), lambda j, i: (i, j + nc))
    prev_v = pl.BlockSpec((H, CT), _swap(_prev_halo(R, H, CT, cv)))
    prev_g = pl.BlockSpec((H, CT), _swap(_prev_halo(R, H, CT, cg)))
    nxt = pl.BlockSpec((H, CT), _swap(_next_halo(R, H, T, cv)))
    wspec = pl.BlockSpec((KA_ROWS, CT), lambda j, i: (0, j))
    vec = pl.BlockSpec((1, CT), lambda j, i: (0, j))
    vshape = jax.ShapeDtypeStruct((1, cw), F32)
    return pl.pallas_call(
        body,
        name=name,
        out_shape=(jax.ShapeDtypeStruct((T, cw), BF16), jax.ShapeDtypeStruct((T, cw), BF16),
                   jax.ShapeDtypeStruct((KA_ROWS, cw), F32), vshape, vshape, vshape),
        grid=(nc, n_t),
        in_specs=[main_v, main_g, prev_v, prev_g, main_v, nxt, main_v, nxt, wspec, vec, vec],
        out_specs=(main_v, main_v, wspec, vec, vec, vec),
        scratch_shapes=[pltpu.VMEM((R + H, CT), F32), pltpu.VMEM((R + H, CT), F32),
                        pltpu.VMEM((7, R + H, CT), F32), pltpu.VMEM((7, R + H, CT), F32)],
        compiler_params=_params("parallel", "arbitrary"),
    )(proj, proj, proj, proj, u1, u1, dmix, dmix, wa, lg, lb)


def _mix_b_fwd(proj, wb, *, cw, sw, name):
    T = proj.shape[0]
    R = _pick(T, 256, 8)
    CT = _pick(sw, 512)
    nb, nc, nh = (2 * cw) // CT, (2 * cw + sw) // CT, (2 * cw + 2 * sw) // CT
    H = HALO_S

    def body(b_ref, c_ref, h_ref, ch_ref, hh_ref, w_ref, v_ref, ext):
        i = pl.program_id(0)
        ext[pl.ds(0, H), :] = jnp.where(i > 0, ch_ref[...] * hh_ref[...], 0.0)
        ext[pl.ds(H, R), :] = c_ref[...] * h_ref[...]
        w = w_ref[...]
        zc = ext[pl.ds(H - 2, R), :] * w[0:1, :]
        zc = zc + ext[pl.ds(H - 1, R), :] * w[1:2, :]
        zc = zc + ext[pl.ds(H, R), :] * w[2:3, :]
        v_ref[...] = (b_ref[...] * zc).astype(BF16)

    def main(off):
        return pl.BlockSpec((R, CT), lambda i, j: (i, j + off))

    def prev(off):
        return pl.BlockSpec((H, CT), _prev_halo(R, H, CT, lambda j: j + off))

    return pl.pallas_call(
        body,
        name=name,
        out_shape=jax.ShapeDtypeStruct((T, sw), BF16),
        grid=(T // R, sw // CT),
        in_specs=[main(nb), main(nc), main(nh), prev(nc), prev(nh), pl.BlockSpec((KS_ROWS, CT), lambda i, j: (0, j))],
        out_specs=pl.BlockSpec((R, CT), lambda i, j: (i, j)),
        scratch_shapes=[pltpu.VMEM((R + H, CT), F32)],
        compiler_params=_params("parallel", "parallel"),
    )(proj, proj, proj, proj, proj, wb)


def _mix_b_bwd(proj, dmix, wb, *, cw, sw, name):
    T = proj.shape[0]
    R = _pick(T, 256, 8)
    CT = _pick(sw, 512)
    nb, nc, nh = (2 * cw) // CT, (2 * cw + sw) // CT, (2 * cw + 2 * sw) // CT
    nd = cw // CT
    H = HALO_S
    n_t = T // R

    def body(b_ref, bn_ref, c_ref, cp_ref, cn_ref, h_ref, hp_ref, hn_ref, d_ref, dn_ref, w_ref,
             db_ref, dc_ref, dhh_ref, dw_ref, ext, dze):
        i = pl.program_id(1)
        last = i == n_t - 1
        cc, hh = c_ref[...], h_ref[...]
        ext[pl.ds(0, H), :] = jnp.where(i > 0, cp_ref[...] * hp_ref[...], 0.0)
        ext[pl.ds(H, R), :] = cc * hh
        ext[pl.ds(H + R, H), :] = cn_ref[...] * hn_ref[...]
        w = w_ref[...]
        w0, w1, w2 = w[0:1, :], w[1:2, :], w[2:3, :]

        def conv(start, n):
            z = ext[pl.ds(start + H - 2, n), :] * w0
            z = z + ext[pl.ds(start + H - 1, n), :] * w1
            return z + ext[pl.ds(start + H, n), :] * w2

        d_main = d_ref[...]
        db_ref[...] = (d_main * conv(0, R)).astype(BF16)
        dzc_main = d_main * b_ref[...]
        dze[pl.ds(0, R), :] = dzc_main
        dze[pl.ds(R, H), :] = jnp.where(last, 0.0, dn_ref[...] * bn_ref[...])
        dz = dze[pl.ds(0, R), :] * w2 + dze[pl.ds(1, R), :] * w1 + dze[pl.ds(2, R), :] * w0
        dc_ref[...] = (dz * hh).astype(BF16)
        dhh_ref[...] = (dz * cc).astype(BF16)
        rows = [jnp.sum(dzc_main * ext[pl.ds(H - 2 + k, R), :], axis=0, keepdims=True) for k in range(KS)]
        rows.append(jnp.zeros((KS_ROWS - KS, CT), F32))
        _acc_rows_block(dw_ref, i, rows)

    def main(off):
        return pl.BlockSpec((R, CT), lambda j, i: (i, j + off))

    def prev(off):
        return pl.BlockSpec((H, CT), _swap(_prev_halo(R, H, CT, lambda j: j + off)))

    def nxt(off):
        return pl.BlockSpec((H, CT), _swap(_next_halo(R, H, T, lambda j: j + off)))

    out = pl.BlockSpec((R, CT), lambda j, i: (i, j))
    wspec = pl.BlockSpec((KS_ROWS, CT), lambda j, i: (0, j))
    act = jax.ShapeDtypeStruct((T, sw), BF16)
    return pl.pallas_call(
        body,
        name=name,
        out_shape=(act, act, act, jax.ShapeDtypeStruct((KS_ROWS, sw), F32)),
        grid=(sw // CT, n_t),
        in_specs=[main(nb), nxt(nb), main(nc), prev(nc), nxt(nc), main(nh), prev(nh), nxt(nh), main(nd), nxt(nd), wspec],
        out_specs=(out, out, out, wspec),
        scratch_shapes=[pltpu.VMEM((R + 2 * H, CT), F32), pltpu.VMEM((R + H, CT), F32)],
        compiler_params=_params("parallel", "arbitrary"),
    )(proj, proj, proj, proj, proj, proj, proj, proj, dmix, dmix, wb)


def _adamw(w, g, m, v, *, name, emit_grad=False):
    Rr, Cc = w.shape
    R = _pick(Rr, max(8, ADAMW_BLOCK_BYTES // (4 * Cc)), 8)

    def body(w_ref, g_ref, m_ref, v_ref, d_ref, mo_ref, vo_ref, *g_out):
        g = g_ref[...]
        m2 = ADAM_B1 * m_ref[...] + (1.0 - ADAM_B1) * g
        v2 = ADAM_B2 * v_ref[...] + (1.0 - ADAM_B2) * (g * g)
        m_hat = m2 / (1.0 - ADAM_B1 ** ADAM_STEP)
        v_hat = v2 / (1.0 - ADAM_B2 ** ADAM_STEP)
        d_ref[...] = -ADAM_LR * (m_hat / (jnp.sqrt(v_hat) + ADAM_EPS) + ADAM_WD * w_ref[...])
        mo_ref[...] = m2
        vo_ref[...] = v2
        if emit_grad:
            g_out[0][...] = g

    spec = pl.BlockSpec((R, Cc), lambda i: (i, 0))
    shp = jax.ShapeDtypeStruct((Rr, Cc), F32)
    n_out = 4 if emit_grad else 3
    return pl.pallas_call(
        body, name=name, out_shape=(shp,) * n_out, grid=(Rr // R,), in_specs=[spec] * 4, out_specs=(spec,) * n_out,
        compiler_params=_params("parallel"),
    )(w, g, m, v)


def _sum_chips(cs, rb, pc, lay, *, name):
    _, Rr, Cc = rb.shape
    tr, tc = _pick(Rr, 512, 16), _pick(Cc, 2048)
    nr, ncol = Rr // tr, Cc // tc
    if lay.axis == 0:
        own_map = lambda i, j, s: (i + s[0] * nr, j)
        out_map = lambda i, j, s: (i, j + s[1] * ncol)
    else:
        own_map = lambda i, j, s: (i, j + s[0] * ncol)
        out_map = lambda i, j, s: (i + s[1] * nr, j)

    def body(s_ref, own_ref, rb_ref, o_ref):
        acc = own_ref[...].astype(F32)
        for j in range(N_CHIPS - 1):
            acc = acc + rb_ref[j].astype(F32)
        o_ref[...] = acc

    return pl.pallas_call(
        body,
        name=name,
        out_shape=jax.ShapeDtypeStruct(lay.shard_shape(), F32),
        grid_spec=pltpu.PrefetchScalarGridSpec(
            num_scalar_prefetch=1,
            grid=(nr, ncol),
            in_specs=[pl.BlockSpec((tr, tc), own_map), pl.BlockSpec((N_CHIPS - 1, tr, tc), lambda i, j, s: (0, i, j))],
            out_specs=pl.BlockSpec((tr, tc), out_map)),
        compiler_params=_params("parallel", "parallel"),
    )(pc, cs, rb)


def _place():
    x, y, c = lax.axis_index("x"), lax.axis_index("y"), lax.axis_index("c")
    return x, y, c, 2 * x + y


def _other_chips(x, y):
    return [(1 - x, y, 2 * (1 - x) + y), (x, 1 - y, 2 * x + (1 - y)), (1 - x, 1 - y, 2 * (1 - x) + (1 - y))]


def _allgather_small(buf, *, name, reduce):
    S = buf.shape[0]

    def body(x_ref, o_ref, gat, send_sems, recv_sems):
        x, y, c, _ = _place()
        me = 4 * x + 2 * y + c
        gat[me] = x_ref[...]
        copies = []
        for k in range(1, N_DEV):
            fx, fy, fc = (k >> 2) & 1, (k >> 1) & 1, k & 1
            px = 1 - x if fx else x
            py = 1 - y if fy else y
            pc = 1 - c if fc else c
            peer = 4 * px + 2 * py + pc
            send = pltpu.make_async_remote_copy(
                src_ref=x_ref, dst_ref=gat.at[me], send_sem=send_sems.at[k - 1], recv_sem=recv_sems.at[k - 1],
                device_id=(px, py, pc), device_id_type=MESH)
            send.start()
            arrival = pltpu.make_async_remote_copy(
                src_ref=x_ref, dst_ref=gat.at[peer], send_sem=send_sems.at[k - 1], recv_sem=recv_sems.at[k - 1],
                device_id=(px, py, pc), device_id_type=MESH)
            copies.append((send, arrival))
        for send, arrival in copies:
            arrival.wait_recv()
        for send, arrival in copies:
            send.wait_send()
        if reduce:
            acc = gat[0]
            for d in range(1, N_DEV):
                acc = acc + gat[d]
            o_ref[...] = acc
        else:
            o_ref[...] = gat[...]

    out_shape = jax.ShapeDtypeStruct((S, LANES) if reduce else (N_DEV, S, LANES), F32)
    return pl.pallas_call(
        body,
        name=name,
        out_shape=out_shape,
        in_specs=[pl.BlockSpec(memory_space=pltpu.VMEM)],
        out_specs=pl.BlockSpec(memory_space=pltpu.VMEM),
        scratch_shapes=[pltpu.VMEM((N_DEV, S, LANES), F32), pltpu.SemaphoreType.DMA((N_DEV - 1,)),
                        pltpu.SemaphoreType.DMA((N_DEV - 1,))],
        compiler_params=pltpu.CompilerParams(vmem_limit_bytes=VMEM_LIMIT),
    )(buf)


class _Sharded:
    def __init__(self, shape, axis):
        self.shape = shape
        self.axis = axis
        self.block = shape[axis] // N_CHIPS
        self.half = shape[1 - axis] // 2

    def _sl(self, along, across):
        return (along, across) if self.axis == 0 else (across, along)

    def block_slice(self, q):
        return self._sl(pl.ds(q * self.block, self.block), pl.ds(0, self.shape[1 - self.axis]))

    def block_half_slice(self, q, c):
        return self._sl(pl.ds(q * self.block, self.block), pl.ds(c * self.half, self.half))

    def shard_half_slice(self, c):
        return self._sl(pl.ds(0, self.block), pl.ds(c * self.half, self.half))

    def shard_shape(self):
        return self._sl(self.block, self.shape[1 - self.axis])

    def half_shape(self):
        return self._sl(self.shape[self.axis], self.half)

    def block_half_shape(self):
        return self._sl(self.block, self.half)

    def block_in_half_slice(self, q):
        return self._sl(pl.ds(q * self.block, self.block), pl.ds(0, self.half))


def _at(ref, sl):
    return ref.at[sl[0], sl[1]]


class _Copy:
    def __init__(self, src, dst, arrive, dev):
        self.src, self.dst, self.arrive, self.dev = src, dst, arrive, dev


HBM = pl.BlockSpec(memory_space=pltpu.HBM)
SEM = pl.BlockSpec(memory_space=pltpu.SEMAPHORE)
EFFECT = pltpu.SideEffectType.DATAFLOW_SIDE_EFFECTING


def _exchange_start(srcs, land_shapes, plan, after, *, name):
    ns, nl = len(srcs), len(land_shapes)
    n_copies = len(plan([None] * ns, [None] * nl, dry=True))

    def body(*refs):
        src_refs, land_refs = refs[:ns], refs[ns:ns + nl]
        send_sems, recv_sems = refs[ns + nl + 1], refs[ns + nl + 2]
        token = refs[-1]
        for k, cp in enumerate(plan(src_refs, land_refs)):
            pltpu.make_async_remote_copy(
                src_ref=cp.src, dst_ref=cp.dst, send_sem=send_sems.at[k], recv_sem=recv_sems.at[k],
                device_id=cp.dev, device_id_type=MESH).start()
        token[...] = jnp.zeros_like(token)

    sem = pltpu.SemaphoreType.DMA((n_copies,))
    lands = [pltpu.with_memory_space_constraint(lax.empty(shp, dt), pltpu.HBM) for shp, dt in land_shapes]
    srcs = [pltpu.with_memory_space_constraint(a, pltpu.HBM) for a in srcs]
    thru = [pltpu.HBM(a.shape, a.dtype) for a in srcs + lands]
    outs = pl.pallas_call(
        body,
        name=name,
        out_shape=(sem, sem, *thru, jax.ShapeDtypeStruct((8, LANES), F32)),
        in_specs=[HBM] * (ns + nl) + [ANY],
        out_specs=(SEM, SEM, *[HBM] * (ns + nl), pl.BlockSpec(memory_space=pltpu.VMEM)),
        input_output_aliases={i: 2 + i for i in range(ns + nl)},
        compiler_params=pltpu.CompilerParams(has_side_effects=EFFECT),
    )(*srcs, *lands, after)
    return outs[0], outs[1], list(outs[2:2 + ns]), list(outs[2 + ns:2 + ns + nl]), outs[-1]


def _exchange_wait(send_sems, recv_sems, srcs, lands, plan, after, *, name):
    ns, nl = len(srcs), len(lands)
    after = tuple(after) if isinstance(after, (tuple, list)) else (after,)

    def body(*refs):
        src_refs, land_refs = refs[:ns], refs[ns:ns + nl]
        send_sems, recv_sems = refs[ns + nl], refs[ns + nl + 1]
        copies = [
            pltpu.make_async_remote_copy(
                src_ref=cp.src, dst_ref=cp.arrive, send_sem=send_sems.at[k], recv_sem=recv_sems.at[k],
                device_id=cp.dev, device_id_type=MESH)
            for k, cp in enumerate(plan(src_refs, land_refs))
        ]
        for cp in copies:
            cp.wait_recv()
        for cp in copies:
            cp.wait_send()

    thru = [pltpu.HBM(a.shape, a.dtype) for a in list(srcs) + list(lands)]
    outs = pl.pallas_call(
        body,
        name=name,
        out_shape=tuple(thru),
        in_specs=[HBM] * (ns + nl) + [SEM, SEM] + [ANY] * len(after),
        out_specs=tuple([HBM] * (ns + nl)),
        input_output_aliases={i: i for i in range(ns + nl)},
        compiler_params=pltpu.CompilerParams(has_side_effects=EFFECT),
    )(*srcs, *lands, send_sems, recv_sems, *after)
    return list(outs[:ns]), list(outs[ns:])


def _gather_plan(layouts):
    def plan(srcs, lands, dry=False):
        if dry:
            return [None] * (4 * len(layouts))
        x, y, c, p = _place()
        copies = []
        for s, g, lay in zip(srcs, lands, layouts):
            own = _at(g, lay.block_slice(p))
            copies.append(_Copy(s, own, own, (x, y, 1 - c)))
            for qx, qy, q in _other_chips(x, y):
                copies.append(_Copy(_at(s, lay.shard_half_slice(c)), _at(g, lay.block_half_slice(p, c)),
                                    _at(g, lay.block_half_slice(q, c)), (qx, qy, c)))
        return copies

    return plan


def _forward_plan(layouts):
    def plan(srcs, lands, dry=False):
        if dry:
            return [None] * (3 * len(layouts))
        x, y, c, _ = _place()
        copies = []
        for g, lay in zip(srcs, layouts):
            for qx, qy, q in _other_chips(x, y):
                landed = _at(g, lay.block_half_slice(q, c))
                copies.append(_Copy(landed, landed, _at(g, lay.block_half_slice(q, 1 - c)), (x, y, 1 - c)))
        return copies

    return plan


def _sibling_plan(n):
    def plan(srcs, lands, dry=False):
        if dry:
            return [None] * n
        x, y, c, _ = _place()
        return [_Copy(s, ra, ra, (x, y, 1 - c)) for s, ra in zip(srcs, lands)]

    return plan


def _chips_plan(layouts):
    def plan(srcs, lands, dry=False):
        if dry:
            return [None] * (3 * len(layouts))
        x, y, c, _ = _place()
        copies = []
        for s, rb, lay in zip(srcs, lands, layouts):
            for j, (qx, qy, q) in enumerate(_other_chips(x, y)):
                copies.append(_Copy(_at(s, lay.block_in_half_slice(q)), rb.at[j], rb.at[j], (qx, qy, c)))
        return copies

    return plan


def _join_plan(layouts):
    def plan(srcs, lands, dry=False):
        if dry:
            return [None] * len(layouts)
        x, y, c, _ = _place()
        copies = []
        for g, lay in zip(srcs, layouts):
            mine = _at(g, lay.shard_half_slice(c))
            copies.append(_Copy(mine, mine, _at(g, lay.shard_half_slice(1 - c)), (x, y, 1 - c)))
        return copies

    return plan


def _pack(arrays):
    flat = [a.reshape(-1) for a in arrays]
    sizes = [f.shape[0] for f in flat]
    total = sum(sizes)
    rows = _round_up(-(-total // LANES), 8)
    flat.append(jnp.zeros((rows * LANES - total,), F32))
    return jnp.concatenate(flat).reshape(rows, LANES)


def _unpack(buf, shapes):
    flat = buf.reshape(-1)
    out, pos = [], 0
    for shp in shapes:
        n = 1
        for d in shp:
            n *= d
        out.append(flat[pos:pos + n].reshape(shp))
        pos += n
    return out


def _pad_to(a, rows, cols):
    return jnp.pad(a, ((0, rows - a.shape[0]), (0, cols - a.shape[1])))


def kernel(x, mem, g_mix, w_in, conv_a_w, conv_a_b, ln_a_g, ln_a_b, conv_b_w, w_out, g_xattn, g_mem, w_q, w_k, w_v, w_o, g_ffn, w_gate, w_up, conv_f_w, w_down, g_final, loss_target, m_g_mix, m_w_in, m_conv_a_w, m_conv_a_b, m_ln_a_g, m_ln_a_b, m_conv_b_w, m_w_out, m_g_xattn, m_g_mem, m_w_q, m_w_k, m_w_v, m_w_o, m_g_ffn, m_w_gate, m_w_up, m_conv_f_w, m_w_down, m_g_final, v_g_mix, v_w_in, v_conv_a_w, v_conv_a_b, v_ln_a_g, v_ln_a_b, v_conv_b_w, v_w_out, v_g_xattn, v_g_mem, v_w_q, v_w_k, v_w_v, v_w_o, v_g_ffn, v_w_gate, v_w_up, v_conv_f_w, v_w_down, v_g_final):
    T, D = x.shape[1], x.shape[2]
    in_sh = w_in.shape[2]
    cw_sh = conv_a_w.shape[2]
    cw = N_CHIPS * cw_sh
    f_sh = w_gate.shape[2]
    fp = _round_up(f_sh, 256)
    F = N_CHIPS * fp
    rs = w_out.shape[1]
    c_idx = lax.axis_index("c")
    p_idx = 2 * lax.axis_index("x") + lax.axis_index("y")

    def t_(a):
        return jnp.swapaxes(a[0], 0, 1)

    big = {
        "w_in": (lambda: w_in[0].astype(BF16), _Sharded((D, N_CHIPS * in_sh), 1)),
        "w_out": (lambda: w_out[0].astype(BF16), _Sharded((N_CHIPS * rs, D), 0)),
        "w_q": (lambda: w_q[0].astype(BF16), _Sharded((D, D), 0)),
        "w_k": (lambda: w_k[0].astype(BF16), _Sharded((D, D), 0)),
        "w_v": (lambda: w_v[0].astype(BF16), _Sharded((D, D), 0)),
        "w_o": (lambda: w_o[0].astype(BF16), _Sharded((D, D), 0)),
        "w_gate": (lambda: _pad_to(t_(w_gate).astype(BF16), fp, D), _Sharded((F, D), 0)),
        "w_up": (lambda: _pad_to(t_(w_up).astype(BF16), fp, D), _Sharded((F, D), 0)),
        "w_down": (lambda: _pad_to(w_down[0].astype(BF16), fp, D), _Sharded((F, D), 0)),
    }
    names = list(big)
    lay = {k: big[k][1] for k in names}
    c_arr = c_idx.astype(jnp.int32).reshape(1)
    pc_arr = jnp.stack([p_idx, c_idx]).astype(jnp.int32)

    conv_shapes = [(KA_ROWS, cw_sh), (KS_ROWS, cw_sh), (KS_ROWS, fp)]
    conv_pack = _pack([_pad_to(conv_a_w[0], KA_ROWS, cw_sh), _pad_to(conv_b_w[0], KS_ROWS, cw_sh),
                       _pad_to(conv_f_w[0], KS_ROWS, fp)])
    conv_all = _allgather_small(conv_pack, name="allgather_conv", reduce=False)
    per_chip = [_unpack(conv_all[2 * q], conv_shapes) for q in range(N_CHIPS)]
    wa = jnp.concatenate([pc[0] for pc in per_chip], axis=1)
    wb = jnp.concatenate([pc[1] for pc in per_chip], axis=1)
    wf = jnp.concatenate([pc[2] for pc in per_chip], axis=1)

    tok = conv_all

    in_half = _Sharded((D // 2, N_CHIPS * in_sh), 1)
    sources = {k: big[k] for k in names if k != "w_in"}
    sources["w_in_lo"] = (lambda: w_in[0, :D // 2].astype(BF16), in_half)
    sources["w_in_hi"] = (lambda: w_in[0, D // 2:].astype(BF16), in_half)
    gather_groups = {"in_lo": ["w_in_lo"], "in_hi": ["w_in_hi"], "out": ["w_out"], "q": ["w_q"], "kv": ["w_k", "w_v"],
                     "o": ["w_o"], "gate": ["w_gate"], "up": ["w_up"], "down": ["w_down"]}
    gathers = {}
    W = {}
    relays = {}

    def shards(tag):
        return [sources[k][0]() for k in gather_groups[tag]]

    def gather_start(tag, srcs, after):
        grp = gather_groups[tag]
        lays = [sources[k][1] for k in grp]
        plan = _gather_plan(lays)
        ssem, rsem, srcs, lands, token = _exchange_start(
            srcs, [(l.shape, BF16) for l in lays], plan, after, name=f"gather_start_{tag}")
        gathers[tag] = (ssem, rsem, srcs, lands, plan, lays, grp)
        return token

    def relay(tag, after, also=()):
        ssem, rsem, srcs, lands, plan, lays, grp = gathers[tag]
        _, lands = _exchange_wait(ssem, rsem, srcs, lands, plan, (after, *also), name=f"gather_wait_{tag}")
        plan = _forward_plan(lays)
        ssem, rsem, lands, _, token = _exchange_start(lands, [], plan, after, name=f"gather_forward_start_{tag}")
        relays[tag] = (ssem, rsem, lands, plan, grp)
        return token

    def gathered(tag, after):
        if tag not in relays:
            relay(tag, after)
        ssem, rsem, lands, plan, grp = relays[tag]
        lands, _ = _exchange_wait(ssem, rsem, lands, [], plan, after, name=f"gather_forward_wait_{tag}")
        W.update(zip(grp, lands))

    tok = gather_start("in_lo", shards("in_lo"), tok)
    tok = gather_start("in_hi", shards("in_hi"), tok)
    later = {tag: shards(tag) for tag in gather_groups if tag not in gathers}
    tok = relay("in_lo", tok, also=[a for srcs in later.values() for a in srcs])
    for tag, srcs in later.items():
        tok = gather_start(tag, srcs, tok)

    xs, tgt, mems = x[0], loss_target[0], mem[0]
    g_mem2, g_final2 = g_mem[None, :], g_final[None, :]

    memn, rm = _norm_fwd(mems, g_mem2, name="norm_mem")
    xn1, r1 = _norm_fwd(xs, g_mix, name="norm_mix")
    gathered("in_lo", (xn1, tok))
    proj = _mm(xn1, W["w_in_lo"], a_k=(0, D // 2), name="mm_in_lo")
    gathered("in_hi", proj)
    proj = _mm(xn1, W["w_in_hi"], a_k=(D // 2, D // 2), add=proj, name="mm_in_hi")
    ftok = relay("out", proj)
    u3, u1 = _mix_a_fwd(proj, wa, conv_a_b, ln_a_g, ln_a_b, cw=cw, deps=[ftok], name="mix_a_fwd")
    vb = _mix_b_fwd(proj, wb, cw=cw, sw=cw, name="mix_b_fwd")
    mix = jnp.concatenate([u3, vb], axis=1)
    gathered("out", mix)
    ftok = relay("q", mix)
    h1 = _mm(mix, W["w_out"], add=xs, deps=[ftok], name="mm_out")
    xn2, r2 = _norm_fwd(h1, g_xattn, name="norm_xattn")
    gathered("q", xn2)
    ftok = relay("kv", xn2)
    q = _mm(xn2, W["w_q"], out_dtype=BF16, deps=[ftok], name="mm_q")
    gathered("kv", q)
    ftok = relay("o", q)
    k = _mm(memn, W["w_k"], out_dtype=BF16, deps=[ftok], name="mm_k")
    vm = _mm(memn, W["w_v"], out_dtype=BF16, name="mm_v")
    o = _attn_fwd(q, k, vm, name="attn_fwd")
    gathered("o", o)
    h2 = _mm(o, W["w_o"], add=h1, name="mm_o")
    relay("gate", h2)
    xn3, r3 = _norm_fwd(h2, g_ffn, name="norm_ffn")
    gathered("gate", xn3)
    gp = _mm(xn3, W["w_gate"], tb=True, name="mm_gate")
    gathered("up", gp)
    up = _mm(xn3, W["w_up"], tb=True, name="mm_up")
    ftok = relay("down", up)
    f = _ffn_act_fwd(gp, up, wf, deps=[ftok], name="ffn_act_fwd")
    gathered("down", f)
    def mm_over_ffn(a, b, *, name, add=None, deps=()):
        lo = _mm(a, b, a_k=(0, F // 2), b_k=(0, F // 2), add=add, deps=deps, name=name + "_lo")
        return _mm(a, b, a_k=(F // 2, F // 2), b_k=(F // 2, F // 2), add=lo, name=name + "_hi")

    h3 = mm_over_ffn(f, W["w_down"], add=h2, name="mm_down")

    G = {}
    other_arr = 1 - c_arr

    def grad_half(k, which, **kw):
        xk, dyk = G[k]
        return _mm(xk, dyk, ta=True, half=("n" if lay[k].axis == 0 else "m", which), **kw)

    def siblings_start(tag, grp, after):
        lays = [lay[k] for k in grp]
        firsts = [grad_half(k, other_arr, deps=[after], name=f"mm_dw_{k}_sibling") for k in grp]
        plan = _sibling_plan(len(grp))
        ssem, rsem, srcs, lands, token = _exchange_start(
            firsts, [(l.half_shape(), F32) for l in lays], plan, after, name=f"rs_siblings_start_{tag}")
        return (tag, grp, lays, plan, ssem, rsem, srcs, lands), token

    def chips_start(state, after):
        tag, grp, lays, plan, ssem, rsem, srcs, lands = state
        _, lands = _exchange_wait(ssem, rsem, srcs, lands, plan, after, name=f"rs_siblings_wait_{tag}")
        sums = [grad_half(k, c_arr, add=ra, out_dtype=BF16, name=f"mm_dw_{k}_own") for k, ra in zip(grp, lands)]
        plan = _chips_plan(lays)
        ssem, rsem, sums, lands, token = _exchange_start(
            sums, [((N_CHIPS - 1,) + l.block_half_shape(), BF16) for l in lays], plan, after, name=f"rs_chips_start_{tag}")
        return (tag, grp, lays, plan, ssem, rsem, sums, lands), token

    def join_start(state, after):
        tag, grp, lays, plan, ssem, rsem, sums, lands = state
        sums, lands = _exchange_wait(ssem, rsem, sums, lands, plan, after, name=f"rs_chips_wait_{tag}")
        halves = [_sum_chips(cs, rb, pc_arr, l, name=f"sum_chips_{k}") for k, cs, rb, l in zip(grp, sums, lands, lays)]
        plan = _join_plan(lays)
        ssem, rsem, halves, _, token = _exchange_start(halves, [], plan, after[0], name=f"rs_join_start_{tag}")
        return (tag, grp, plan, ssem, rsem, halves), token

    def join_finish(state, after):
        tag, grp, plan, ssem, rsem, halves = state
        halves, _ = _exchange_wait(ssem, rsem, halves, [], plan, after, name=f"rs_join_wait_{tag}")
        return dict(zip(grp, halves))

    loss_rows, dh3, dh3b, dg_final = _loss_head(h3, g_final2, tgt, name="loss_head")
    df = _mm(dh3b, W["w_down"], tb=True, out_dtype=BF16, name="mm_d_f")
    G["w_down"] = (f, dh3b)
    rs_down, tok = siblings_start("down", ["w_down"], tok)
    dgp, dup, dwf = _ffn_act_bwd(gp, up, df, wf, name="ffn_act_bwd")
    G["w_gate"], G["w_up"] = (dgp, xn3), (dup, xn3)
    rs_ffn, tok = siblings_start("ffn", ["w_gate", "w_up"], tok)
    rs_down, tok = chips_start(rs_down, tok)
    dxn3 = mm_over_ffn(dgp, W["w_gate"], deps=[tok], name="mm_dxn3_gate")
    dxn3 = mm_over_ffn(dup, W["w_up"], add=dxn3, name="mm_dxn3_up")
    dh2, dh2b, dg_ffn = _norm_bwd(h2, r3, g_ffn, dxn3, dh3, name="norm_ffn_bwd")
    rs_ffn, tok = chips_start(rs_ffn, dh2b)
    do = _mm(dh2b, W["w_o"], tb=True, out_dtype=BF16, deps=[tok], name="mm_d_o")
    G["w_o"] = (o, dh2b)
    dq, dk, dvm = _attn_bwd(q, k, vm, do, name="attn_bwd")
    dkb, dvb = dk.astype(BF16), dvm.astype(BF16)
    G["w_q"], G["w_k"], G["w_v"] = (xn2, dq), (memn, dkb), (memn, dvb)
    rs_att, tok = siblings_start("att", ["w_o", "w_q", "w_k", "w_v"], tok)
    dxn2 = _mm(dq, W["w_q"], tb=True, deps=[tok], name="mm_dxn2")
    dmemn = _mm(dkb, W["w_k"], tb=True, name="mm_dmem_k")
    dmemn = _mm(dvb, W["w_v"], tb=True, add=dmemn, name="mm_dmem_v")
    dg_mem = _norm_bwd(mems, rm, g_mem2, dmemn, None, name="norm_mem_bwd", want_dh=False)
    dh1, dh1b, dg_xattn = _norm_bwd(h1, r2, g_xattn, dxn2, dh2, name="norm_xattn_bwd")
    rs_att, tok = chips_start(rs_att, dh1b)
    dmix = _mm(dh1b, W["w_out"], tb=True, deps=[tok], name="mm_d_mix")
    G["w_out"] = (mix, dh1b)
    rs_out, tok = siblings_start("out", ["w_out"], tok)
    dav, dag, dwa, dba, dlg, dlb = _mix_a_bwd(proj, u1, dmix, wa, ln_a_g, ln_a_b, cw=cw, name="mix_a_bwd")
    dbg, dcg, dbh, dwb = _mix_b_bwd(proj, dmix, wb, cw=cw, sw=cw, name="mix_b_bwd")
    dproj = jnp.concatenate([dav, dag, dbg, dcg, dbh], axis=1)
    G["w_in"] = (xn1, dproj)
    rs_in, tok = siblings_start("in", ["w_in"], tok)
    rs_out, tok = chips_start(rs_out, tok)
    dxn1 = _mm(dproj, W["w_in_lo"], tb=True, out_cols=(0, D), deps=[tok], name="mm_dxn1_lo")
    dxn1 = _mm(dproj, W["w_in_hi"], tb=True, out_cols=(D // 2, D), into=dxn1, name="mm_dxn1_hi")
    dx, _, dg_mix = _norm_bwd(xs, r1, g_mix, dxn1, dh1, name="norm_mix_bwd")
    rs_in, tok = chips_start(rs_in, dx)

    loss_part = jnp.sum(loss_rows).reshape(1, 1)
    small_parts = [dg_mix, dba, dlg, dlb, dg_xattn, dg_mem, dg_ffn, dg_final, dwa, dwb, dwf, loss_part]
    small_shapes = [a.shape for a in small_parts]
    reduced = _allgather_small(_pack(small_parts), name="allreduce_small", reduce=True)
    (sg_mix, sba, slg, slb, sg_xattn, sg_mem, sg_ffn, sg_final, swa, swb, swf, loss_sum) = _unpack(reduced, small_shapes)
    loss = loss_sum.reshape(())
    ga_w = lax.dynamic_slice(swa, (0, p_idx * cw_sh), (KA, cw_sh))
    gb_w = lax.dynamic_slice(swb, (0, p_idx * cw_sh), (KS, cw_sh))
    gf_w = lax.dynamic_slice(swf, (0, p_idx * fp), (KS, f_sh))

    weights = dict(g_mix=g_mix, w_in=w_in, conv_a_w=conv_a_w, conv_a_b=conv_a_b, ln_a_g=ln_a_g, ln_a_b=ln_a_b,
                   conv_b_w=conv_b_w, w_out=w_out, g_xattn=g_xattn, g_mem=g_mem, w_q=w_q, w_k=w_k, w_v=w_v, w_o=w_o,
                   g_ffn=g_ffn, w_gate=w_gate, w_up=w_up, conv_f_w=conv_f_w, w_down=w_down, g_final=g_final)
    m_in = dict(g_mix=m_g_mix, w_in=m_w_in, conv_a_w=m_conv_a_w, conv_a_b=m_conv_a_b, ln_a_g=m_ln_a_g, ln_a_b=m_ln_a_b,
                conv_b_w=m_conv_b_w, w_out=m_w_out, g_xattn=m_g_xattn, g_mem=m_g_mem, w_q=m_w_q, w_k=m_w_k, w_v=m_w_v,
                w_o=m_w_o, g_ffn=m_g_ffn, w_gate=m_w_gate, w_up=m_w_up, conv_f_w=m_conv_f_w, w_down=m_w_down,
                g_final=m_g_final)
    v_in = dict(g_mix=v_g_mix, w_in=v_w_in, conv_a_w=v_conv_a_w, conv_a_b=v_conv_a_b, ln_a_g=v_ln_a_g, ln_a_b=v_ln_a_b,
                conv_b_w=v_conv_b_w, w_out=v_w_out, g_xattn=v_g_xattn, g_mem=v_g_mem, w_q=v_w_q, w_k=v_w_k, w_v=v_w_v,
                w_o=v_w_o, g_ffn=v_g_ffn, w_gate=v_w_gate, w_up=v_w_up, conv_f_w=v_conv_f_w, w_down=v_w_down,
                g_final=v_g_final)
    order = list(weights)
    grads = dict(g_mix=sg_mix, conv_a_w=ga_w, conv_a_b=sba, ln_a_g=slg, ln_a_b=slb, conv_b_w=gb_w, g_xattn=sg_xattn,
                 g_mem=sg_mem, g_ffn=sg_ffn, conv_f_w=gf_w, g_final=sg_final)
    grads = {k: g.reshape(weights[k].shape) for k, g in grads.items()}

    delta, new_m, new_v = {}, {}, {}
    small = [k for k in order if k not in big]
    small_shapes = [weights[k].shape for k in small]
    packed = [_pack([src[k] for k in small]) for src in (weights, grads, m_in, v_in)]
    d_, m_, v_ = _adamw(*packed, name="adamw_small")
    for k, dd, mm_, vv in zip(small, _unpack(d_, small_shapes), _unpack(m_, small_shapes), _unpack(v_, small_shapes)):
        delta[k], new_m[k], new_v[k] = dd, mm_, vv

    transposed = ("w_gate", "w_up")

    def update(shard_grads):
        last = None
        for k, g in shard_grads.items():
            view = t_ if k in transposed else (lambda a: a[0])
            back = (lambda a: jnp.swapaxes(a, 0, 1)[None]) if k in transposed else (lambda a: a[None])
            padded = g.shape != view(weights[k]).shape
            outs = _adamw(view(weights[k]), g, view(m_in[k]), view(v_in[k]), emit_grad=padded, name=f"adamw_{k}")
            delta[k], new_m[k], new_v[k] = back(outs[0]), back(outs[1]), back(outs[2])
            grads[k] = back(outs[3] if padded else g)
            last = outs[0]
        return last

    after = (d_, tok)
    joining = None
    for state in (rs_down, rs_ffn, rs_att, rs_out, rs_in):
        started, token = join_start(state, after)
        after = (token,)
        if joining is not None:
            after = (update(join_finish(joining, token)), token)
        joining = started
    update(join_finish(joining, after[0]))

    return (loss, dx[None], *[grads[k] for k in order], *[delta[k] for k in order],
            *[new_m[k] for k in order], *[new_v[k] for k in order])
```
